```python
import jax, jax.numpy as jnp
from jax import lax
import numpy as np

D_MODEL = 1024
BATCH = 8
SEQ = 16384
DEPTH = 1

GRID_W = 64
CTX_LEN = 256
MLSTM_HEADS = 4
MLSTM_DIM = 1024
MLSTM_HEAD_DIM = MLSTM_DIM // MLSTM_HEADS
MLSTM_CHUNK = 64
QK_CONV = 3
SGU_GROUPS = 4
SGU_DIM = 1024
SGU_GROUP_DIM = SGU_DIM // SGU_GROUPS
SGU_CHUNK = 128
D_FF = 2816
FFN_CONV = 3
N_BRANCH = 2
N_MOD = 6
N_GATES = 4 * MLSTM_HEADS
N_SCAN_IN = 3 * MLSTM_DIM + N_GATES
N_IN = N_SCAN_IN + MLSTM_DIM + 2 * SGU_DIM + N_BRANCH * D_MODEL
SCAN_SPLITS = (MLSTM_DIM, 2 * MLSTM_DIM, 3 * MLSTM_DIM)
REST_SPLITS = (MLSTM_DIM, MLSTM_DIM + SGU_DIM, MLSTM_DIM + 2 * SGU_DIM, MLSTM_DIM + 2 * SGU_DIM + D_MODEL)
EPS = 1e-6
M_INIT = -1e30

kernel_name = "hybrid_mlstm_sgu_convffn_dit_block"


def rmsnorm(x, g):
    xf = x.astype(jnp.float32)
    y = xf * lax.rsqrt(jnp.mean(xf * xf, axis=-1, keepdims=True) + EPS)
    return (y * g.astype(jnp.float32)).astype(x.dtype)


def modulate(h, shift, scale):
    return h * (1.0 + scale) + shift


def dwconv1d(x, w):
    return lax.conv_general_dilated(x, w[:, None, :].astype(x.dtype), window_strides=(1,), padding='SAME',
                                    dimension_numbers=('NWC', 'WIO', 'NWC'), feature_group_count=x.shape[-1])


def dwconv2d(x, w):
    return lax.conv_general_dilated(x, w[:, :, None, :].astype(x.dtype), window_strides=(1, 1), padding='SAME',
                                    dimension_numbers=('NHWC', 'HWIO', 'NHWC'), feature_group_count=x.shape[-1])


def mlstm_init_state(b):
    return (jnp.zeros((b, MLSTM_HEADS, MLSTM_HEAD_DIM, MLSTM_HEAD_DIM), jnp.float32),
            jnp.zeros((b, MLSTM_HEADS, MLSTM_HEAD_DIM), jnp.float32),
            jnp.full((b, MLSTM_HEADS), M_INIT, jnp.float32))


def mlstm_chunk_stats(k, v, log_i, log_f):
    b = jnp.cumsum(log_f, axis=-1)
    g = b[..., -1]
    a = g[..., None] - b + log_i
    m_loc = jnp.max(a, axis=-1)
    kw = k * jnp.exp(a - m_loc[..., None])[..., None]
    kv = jnp.einsum('bnhld,bnhle->bnhde', kw, v)
    kn = jnp.sum(kw, axis=3)
    return b, g, m_loc, kv, kn


def mlstm_state_scan(g, m_loc, kv, kn, state0):
    def step(carry, inp):
        c_st, n_st, m_st = carry
        g_c, m_c, kv_c, kn_c = inp
        m_new = jnp.maximum(g_c + m_st, m_c)
        a_old = jnp.exp(g_c + m_st - m_new)
        a_new = jnp.exp(m_c - m_new)
        c_new = a_old[..., None, None] * c_st + a_new[..., None, None] * kv_c
        n_new = a_old[..., None] * n_st + a_new[..., None] * kn_c
        return (c_new, n_new, m_new), (c_st, n_st, m_st)
    xs = (jnp.moveaxis(g, 1, 0), jnp.moveaxis(m_loc, 1, 0), jnp.moveaxis(kv, 1, 0), jnp.moveaxis(kn, 1, 0))
    final, prev = lax.scan(step, state0, xs)
    prev = (jnp.moveaxis(prev[0], 0, 1), jnp.moveaxis(prev[1], 0, 1), jnp.moveaxis(prev[2], 0, 1))
    return prev, final


def mlstm_chunk_out(q, k, v, log_i, b, prev):
    c_prev, n_prev, m_prev = prev
    L = q.shape[3]
    inter = b + m_prev[..., None]
    d = b[..., :, None] - b[..., None, :] + log_i[..., None, :]
    scan_order = jnp.tril(jnp.ones((L, L), dtype=bool))
    d = jnp.where(scan_order, d, -jnp.inf)
    m_row = jnp.maximum(inter, jnp.max(d, axis=-1))
    w_inter = jnp.exp(inter - m_row)
    s = jnp.einsum('bnhld,bnhsd->bnhls', q, k) * jnp.exp(d - m_row[..., None])
    num = jnp.einsum('bnhls,bnhse->bnhle', s, v) + w_inter[..., None] * jnp.einsum('bnhld,bnhde->bnhle', q, c_prev)
    den = jnp.sum(s, axis=-1) + w_inter * jnp.einsum('bnhld,bnhd->bnhl', q, n_prev)
    return num / jnp.maximum(jnp.abs(den), jnp.exp(-m_row))[..., None]


def mlstm_direction(q, k, v, i_pre, f_pre, state0, with_out):
    bsz, t_len = q.shape[0], q.shape[1]
    n_chunks = t_len // MLSTM_CHUNK

    def chunks(t):
        return jnp.swapaxes(t.reshape((bsz, n_chunks, MLSTM_CHUNK) + t.shape[2:]), 2, 3)

    qc, kc, vc = chunks(q), chunks(k), chunks(v)
    log_i = chunks(i_pre)
    log_f = chunks(jax.nn.log_sigmoid(f_pre))
    b, g, m_loc, kv, kn = mlstm_chunk_stats(kc, vc, log_i, log_f)
    prev, final = mlstm_state_scan(g, m_loc, kv, kn, state0)
    if not with_out:
        return None, final
    h = mlstm_chunk_out(qc, kc, vc, log_i, b, prev)
    return jnp.swapaxes(h, 2, 3).reshape(bsz, t_len, MLSTM_HEADS, MLSTM_HEAD_DIM), final


def mlstm_bidir(q, k, v, gates, state_f, state_b, with_out):
    flip = lambda t: jnp.flip(t, axis=1)
    h_f, fin_f = mlstm_direction(q, k, v, gates[:, :, 0], gates[:, :, 1], state_f, with_out)
    h_b, fin_b = mlstm_direction(flip(q), flip(k), flip(v), flip(gates[:, :, 2]), flip(gates[:, :, 3]), state_b, with_out)
    h = h_f + flip(h_b) if with_out else None
    return h, fin_f, fin_b


def spatial_gating(u, v, ln_g, ln_b, w_s, b_s):
    bsz, t_len, _ = u.shape
    vf = v.astype(jnp.float32)
    vf = vf - jnp.mean(vf, axis=-1, keepdims=True)
    vn = vf * lax.rsqrt(jnp.mean(vf * vf, axis=-1, keepdims=True) + EPS) * ln_g + ln_b
    vn = vn.reshape(bsz, t_len // SGU_CHUNK, SGU_CHUNK, SGU_GROUPS, SGU_GROUP_DIM)
    mixed = jnp.einsum('gpq,bnqgd->bnpgd', w_s.astype(jnp.float32), vn) + b_s.T[:, :, None]
    return u * mixed.reshape(bsz, t_len, SGU_DIM).astype(u.dtype)


def token_mixer(hn, w_in_l, b_gate_l, conv_qk_l, head_norm_g_l, sgu_ln_g_l, sgu_ln_b_l, w_s_l, b_s_l,
                w_branch_mlstm_l, w_branch_sgu_l, w_out_l, state_f, state_b, with_out):
    bsz, t_len, _ = hn.shape
    n_cols = N_IN if with_out else N_SCAN_IN
    z = hn @ w_in_l[:, :n_cols]
    z_q, z_k, z_v, z_g = jnp.split(z[..., :N_SCAN_IN], SCAN_SPLITS, axis=-1)
    qk = jax.nn.silu(dwconv1d(jnp.concatenate([z_q, z_k], axis=-1), conv_qk_l)).astype(jnp.float32)
    q, k = jnp.split(qk, 2, axis=-1)
    heads = lambda t: t.reshape(bsz, t_len, MLSTM_HEADS, MLSTM_HEAD_DIM)
    q = heads(q) * (MLSTM_HEAD_DIM ** -0.5)
    gates = (z_g + b_gate_l).astype(jnp.float32).reshape(bsz, t_len, 4, MLSTM_HEADS)
    h_m, fin_f, fin_b = mlstm_bidir(q, heads(k), heads(z_v.astype(jnp.float32)), gates, state_f, state_b, with_out)
    if not with_out:
        return None, fin_f, fin_b
    z_o, z_u, z_vg, z_gm, z_gg = jnp.split(z[..., N_SCAN_IN:], REST_SPLITS, axis=-1)
    h_m = h_m * lax.rsqrt(jnp.mean(h_m * h_m, axis=-1, keepdims=True) + EPS)
    h_m = (h_m * head_norm_g_l.astype(jnp.float32).reshape(MLSTM_HEADS, MLSTM_HEAD_DIM)).reshape(bsz, t_len, MLSTM_DIM)
    y_m = (jax.nn.sigmoid(z_o.astype(jnp.float32)) * h_m).astype(hn.dtype)
    y_s = spatial_gating(jax.nn.gelu(z_u), jax.nn.gelu(z_vg), sgu_ln_g_l, sgu_ln_b_l, w_s_l, b_s_l)
    y = jax.nn.sigmoid(z_gm) * (y_m @ w_branch_mlstm_l) + jax.nn.sigmoid(z_gg) * (y_s @ w_branch_sgu_l)
    return y @ w_out_l, fin_f, fin_b


def conv_ffn(hn, w_up_l, w_conv_l, w_down_l, grid_rows):
    a, b = jnp.split(hn @ w_up_l, 2, axis=-1)
    bsz, t_len, _ = a.shape
    if grid_rows is None:
        a = dwconv1d(a, w_conv_l[FFN_CONV // 2])
    else:
        a = dwconv2d(a.reshape(bsz, grid_rows, GRID_W, D_FF), w_conv_l).reshape(bsz, t_len, D_FF)
    return (jax.nn.silu(a) * b) @ w_down_l


def _fwd_setup_inputs(seed: int = 0) -> dict:
    key = jax.random.key(seed)
    ks = iter(jax.random.split(key, 40))
    nrm = lambda shape, std: std * jax.random.normal(next(ks), shape, jnp.float32)
    L, D = DEPTH, D_MODEL
    b_i = nrm((L, 2, MLSTM_HEADS), 0.1)
    b_f = jax.random.uniform(next(ks), (L, 2, MLSTM_HEADS), jnp.float32, 3.0, 6.0)
    b_gate = jnp.stack([b_i, b_f], axis=2).reshape(L, N_GATES)
    return {
        "x": nrm((BATCH, SEQ, D), 1.0),
        "c": nrm((BATCH, D), 1.0),
        "ctx": nrm((BATCH, CTX_LEN, D), 1.0),
        "c_ctx": nrm((D,), 1.0),
        "w_mod": nrm((L, D, N_MOD * D), 0.02),
        "b_mod": nrm((L, N_MOD * D), 0.02),
        "norm1_g": 1.0 + nrm((L, D), 0.02),
        "w_in": nrm((L, D, N_IN), D ** -0.5),
        "b_gate": b_gate,
        "conv_qk": nrm((L, QK_CONV, 2 * MLSTM_DIM), QK_CONV ** -0.5),
        "head_norm_g": 1.0 + nrm((L, MLSTM_DIM), 0.02),
        "sgu_ln_g": 1.0 + nrm((L, SGU_DIM), 0.02),
        "sgu_ln_b": nrm((L, SGU_DIM), 0.02),
        "w_s": nrm((L, SGU_GROUPS, SGU_CHUNK, SGU_CHUNK), SGU_CHUNK ** -0.5),
        "b_s": 1.0 + nrm((L, SGU_GROUPS, SGU_CHUNK), 0.1),
        "w_branch_mlstm": nrm((L, MLSTM_DIM, D), MLSTM_DIM ** -0.5),
        "w_branch_sgu": nrm((L, SGU_DIM, D), SGU_DIM ** -0.5),
        "w_out": nrm((L, D, D), D ** -0.5),
        "norm2_g": 1.0 + nrm((L, D), 0.02),
        "w_up": nrm((L, D, 2 * D_FF), D ** -0.5),
        "w_ffn_conv": nrm((L, FFN_CONV, FFN_CONV, D_FF), 1.0 / FFN_CONV),
        "w_down": nrm((L, D_FF, D), D_FF ** -0.5),
        "final_g": 1.0 + nrm((D,), 0.02),
    }


def _fwd_reference(x, c, ctx, c_ctx, w_mod, b_mod, norm1_g, w_in, b_gate, conv_qk, head_norm_g, sgu_ln_g, sgu_ln_b,
              w_s, b_s, w_branch_mlstm, w_branch_sgu, w_out, norm2_g, w_up, w_ffn_conv, w_down, final_g):
    bsz, t_len, _ = x.shape
    rows = t_len // GRID_W
    h_x, h_c = x, ctx
    for l in range(DEPTH):
        last = l == DEPTH - 1
        mx = jnp.split((jax.nn.silu(c) @ w_mod[l] + b_mod[l])[:, None, :], N_MOD, axis=-1)
        mc = jnp.split(jax.nn.silu(c_ctx) @ w_mod[l] + b_mod[l], N_MOD, axis=-1)
        mixer_w = (w_in[l], b_gate[l], conv_qk[l], head_norm_g[l], sgu_ln_g[l], sgu_ln_b[l], w_s[l], b_s[l],
                   w_branch_mlstm[l], w_branch_sgu[l], w_out[l])
        hn_c = modulate(rmsnorm(h_c, norm1_g[l]), mc[0], mc[1])
        init = mlstm_init_state(bsz)
        out_c, st_f, st_b = token_mixer(hn_c, *mixer_w, init, init, not last)
        hn_x = modulate(rmsnorm(h_x, norm1_g[l]), mx[0], mx[1])
        out_x, _, _ = token_mixer(hn_x, *mixer_w, st_f, st_b, True)
        h_x = h_x + mx[2] * out_x
        h_x = h_x + mx[5] * conv_ffn(modulate(rmsnorm(h_x, norm2_g[l]), mx[3], mx[4]), w_up[l], w_ffn_conv[l], w_down[l], rows)
        if not last:
            h_c = h_c + mc[2] * out_c
            h_c = h_c + mc[5] * conv_ffn(modulate(rmsnorm(h_c, norm2_g[l]), mc[3], mc[4]), w_up[l], w_ffn_conv[l], w_down[l], None)
    return rmsnorm(h_x, final_g)


import jax as _jax
import jax.numpy as _jnp

TWIN_FORMAT = 'train_step'
FWD_PARAMS = ['x', 'c', 'ctx', 'c_ctx', 'w_mod', 'b_mod', 'norm1_g', 'w_in', 'b_gate', 'conv_qk', 'head_norm_g', 'sgu_ln_g', 'sgu_ln_b', 'w_s', 'b_s', 'w_branch_mlstm', 'w_branch_sgu', 'w_out', 'norm2_g', 'w_up', 'w_ffn_conv', 'w_down', 'final_g']
TWIN_WEIGHTS = ['c_ctx', 'w_mod', 'b_mod', 'norm1_g', 'w_in', 'b_gate', 'conv_qk', 'head_norm_g', 'sgu_ln_g', 'sgu_ln_b', 'w_s', 'b_s', 'w_branch_mlstm', 'w_branch_sgu', 'w_out', 'norm2_g', 'w_up', 'w_ffn_conv', 'w_down', 'final_g']
TWIN_DIFF_INPUT = 'x'
TWIN_INPUTS = ['x', 'c', 'ctx', 'c_ctx', 'w_mod', 'b_mod', 'norm1_g', 'w_in', 'b_gate', 'conv_qk', 'head_norm_g', 'sgu_ln_g', 'sgu_ln_b', 'w_s', 'b_s', 'w_branch_mlstm', 'w_branch_sgu', 'w_out', 'norm2_g', 'w_up', 'w_ffn_conv', 'w_down', 'final_g', 'loss_target', 'm_c_ctx', 'm_w_mod', 'm_b_mod', 'm_norm1_g', 'm_w_in', 'm_b_gate', 'm_conv_qk', 'm_head_norm_g', 'm_sgu_ln_g', 'm_sgu_ln_b', 'm_w_s', 'm_b_s', 'm_w_branch_mlstm', 'm_w_branch_sgu', 'm_w_out', 'm_norm2_g', 'm_w_up', 'm_w_ffn_conv', 'm_w_down', 'm_final_g', 'v_c_ctx', 'v_w_mod', 'v_b_mod', 'v_norm1_g', 'v_w_in', 'v_b_gate', 'v_conv_qk', 'v_head_norm_g', 'v_sgu_ln_g', 'v_sgu_ln_b', 'v_w_s', 'v_b_s', 'v_w_branch_mlstm', 'v_w_branch_sgu', 'v_w_out', 'v_norm2_g', 'v_w_up', 'v_w_ffn_conv', 'v_w_down', 'v_final_g']
TWIN_OUTPUTS = ['loss', 'grad_x', 'grad_c_ctx', 'grad_w_mod', 'grad_b_mod', 'grad_norm1_g', 'grad_w_in', 'grad_b_gate', 'grad_conv_qk', 'grad_head_norm_g', 'grad_sgu_ln_g', 'grad_sgu_ln_b', 'grad_w_s', 'grad_b_s', 'grad_w_branch_mlstm', 'grad_w_branch_sgu', 'grad_w_out', 'grad_norm2_g', 'grad_w_up', 'grad_w_ffn_conv', 'grad_w_down', 'grad_final_g', 'delta_c_ctx', 'delta_w_mod', 'delta_b_mod', 'delta_norm1_g', 'delta_w_in', 'delta_b_gate', 'delta_conv_qk', 'delta_head_norm_g', 'delta_sgu_ln_g', 'delta_sgu_ln_b', 'delta_w_s', 'delta_b_s', 'delta_w_branch_mlstm', 'delta_w_branch_sgu', 'delta_w_out', 'delta_norm2_g', 'delta_w_up', 'delta_w_ffn_conv', 'delta_w_down', 'delta_final_g', 'new_m_c_ctx', 'new_m_w_mod', 'new_m_b_mod', 'new_m_norm1_g', 'new_m_w_in', 'new_m_b_gate', 'new_m_conv_qk', 'new_m_head_norm_g', 'new_m_sgu_ln_g', 'new_m_sgu_ln_b', 'new_m_w_s', 'new_m_b_s', 'new_m_w_branch_mlstm', 'new_m_w_branch_sgu', 'new_m_w_out', 'new_m_norm2_g', 'new_m_w_up', 'new_m_w_ffn_conv', 'new_m_w_down', 'new_m_final_g', 'new_v_c_ctx', 'new_v_w_mod', 'new_v_b_mod', 'new_v_norm1_g', 'new_v_w_in', 'new_v_b_gate', 'new_v_conv_qk', 'new_v_head_norm_g', 'new_v_sgu_ln_g', 'new_v_sgu_ln_b', 'new_v_w_s', 'new_v_b_s', 'new_v_w_branch_mlstm', 'new_v_w_branch_sgu', 'new_v_w_out', 'new_v_norm2_g', 'new_v_w_up', 'new_v_w_ffn_conv', 'new_v_w_down', 'new_v_final_g']
TWIN_LEAF_KINDS = {'loss': 'loss', 'grad_x': 'grad_x', 'grad_c_ctx': 'grad_w', 'grad_w_mod': 'grad_w', 'grad_b_mod': 'grad_w', 'grad_norm1_g': 'grad_w', 'grad_w_in': 'grad_w', 'grad_b_gate': 'grad_w', 'grad_conv_qk': 'grad_w', 'grad_head_norm_g': 'grad_w', 'grad_sgu_ln_g': 'grad_w', 'grad_sgu_ln_b': 'grad_w', 'grad_w_s': 'grad_w', 'grad_b_s': 'grad_w', 'grad_w_branch_mlstm': 'grad_w', 'grad_w_branch_sgu': 'grad_w', 'grad_w_out': 'grad_w', 'grad_norm2_g': 'grad_w', 'grad_w_up': 'grad_w', 'grad_w_ffn_conv': 'grad_w', 'grad_w_down': 'grad_w', 'grad_final_g': 'grad_w', 'delta_c_ctx': 'delta_w', 'delta_w_mod': 'delta_w', 'delta_b_mod': 'delta_w', 'delta_norm1_g': 'delta_w', 'delta_w_in': 'delta_w', 'delta_b_gate': 'delta_w', 'delta_conv_qk': 'delta_w', 'delta_head_norm_g': 'delta_w', 'delta_sgu_ln_g': 'delta_w', 'delta_sgu_ln_b': 'delta_w', 'delta_w_s': 'delta_w', 'delta_b_s': 'delta_w', 'delta_w_branch_mlstm': 'delta_w', 'delta_w_branch_sgu': 'delta_w', 'delta_w_out': 'delta_w', 'delta_norm2_g': 'delta_w', 'delta_w_up': 'delta_w', 'delta_w_ffn_conv': 'delta_w', 'delta_w_down': 'delta_w', 'delta_final_g': 'delta_w', 'new_m_c_ctx': 'new_m', 'new_m_w_mod': 'new_m', 'new_m_b_mod': 'new_m', 'new_m_norm1_g': 'new_m', 'new_m_w_in': 'new_m', 'new_m_b_gate': 'new_m', 'new_m_conv_qk': 'new_m', 'new_m_head_norm_g': 'new_m', 'new_m_sgu_ln_g': 'new_m', 'new_m_sgu_ln_b': 'new_m', 'new_m_w_s': 'new_m', 'new_m_b_s': 'new_m', 'new_m_w_branch_mlstm': 'new_m', 'new_m_w_branch_sgu': 'new_m', 'new_m_w_out': 'new_m', 'new_m_norm2_g': 'new_m', 'new_m_w_up': 'new_m', 'new_m_w_ffn_conv': 'new_m', 'new_m_w_down': 'new_m', 'new_m_final_g': 'new_m', 'new_v_c_ctx': 'new_v', 'new_v_w_mod': 'new_v', 'new_v_b_mod': 'new_v', 'new_v_norm1_g': 'new_v', 'new_v_w_in': 'new_v', 'new_v_b_gate': 'new_v', 'new_v_conv_qk': 'new_v', 'new_v_head_norm_g': 'new_v', 'new_v_sgu_ln_g': 'new_v', 'new_v_sgu_ln_b': 'new_v', 'new_v_w_s': 'new_v', 'new_v_b_s': 'new_v', 'new_v_w_branch_mlstm': 'new_v', 'new_v_w_branch_sgu': 'new_v', 'new_v_w_out': 'new_v', 'new_v_norm2_g': 'new_v', 'new_v_w_up': 'new_v', 'new_v_w_ffn_conv': 'new_v', 'new_v_w_down': 'new_v', 'new_v_final_g': 'new_v'}


def _forward(args):
    return _fwd_reference(*[args[k] for k in FWD_PARAMS])


def _output_shape():
    def fwd():
        inp = _fwd_setup_inputs(0)
        return _fwd_reference(*[inp[k] for k in FWD_PARAMS])
    out = _jax.eval_shape(fwd)
    return out.shape, out.dtype

N_MICROBATCH = 1
ADAM_LR = 0.001
ADAM_B1 = 0.9
ADAM_B2 = 0.999
ADAM_EPS = 1e-08
ADAM_WD = 0.01
ADAM_STEP = 10
PER_EXAMPLE_BATCH_AXIS = {'x': 0, 'c': 0, 'ctx': 0, 'loss_target': 0}
SHARED_INPUTS = []
_WEIGHT_DTYPES = {'c_ctx': _jnp.float32, 'w_mod': _jnp.float32, 'b_mod': _jnp.float32, 'norm1_g': _jnp.float32, 'w_in': _jnp.float32, 'b_gate': _jnp.float32, 'conv_qk': _jnp.float32, 'head_norm_g': _jnp.float32, 'sgu_ln_g': _jnp.float32, 'sgu_ln_b': _jnp.float32, 'w_s': _jnp.float32, 'b_s': _jnp.float32, 'w_branch_mlstm': _jnp.float32, 'w_branch_sgu': _jnp.float32, 'w_out': _jnp.float32, 'norm2_g': _jnp.float32, 'w_up': _jnp.float32, 'w_ffn_conv': _jnp.float32, 'w_down': _jnp.float32, 'final_g': _jnp.float32}
MOMENT_SCALE = {'c_ctx': 9.592509e-03, 'w_mod': 1.247780e-01, 'b_mod': 2.286455e-01, 'norm1_g': 1.213898e-01, 'w_in': 4.355048e-02, 'b_gate': 1.109411e-01, 'conv_qk': 9.574001e-03, 'head_norm_g': 4.212437e-02, 'sgu_ln_g': 5.532425e-02, 'sgu_ln_b': 5.981038e-02, 'w_s': 8.215748e-02, 'b_s': 7.789112e-02, 'w_branch_mlstm': 4.209566e-02, 'w_branch_sgu': 8.082580e-02, 'w_out': 9.092142e-02, 'norm2_g': 1.690030e-01, 'w_up': 7.068666e-02, 'w_ffn_conv': 7.206308e-02, 'w_down': 1.162209e-01, 'final_g': 1.279288e+02}


def _to_microbatches(a, axis):
    t = _jnp.moveaxis(a, axis, 0)
    t = t.reshape((N_MICROBATCH, t.shape[0] // N_MICROBATCH) + t.shape[1:])
    return _jnp.moveaxis(t, 1, axis + 1)


def setup_inputs(seed: int = 0) -> dict:
    inp = _fwd_setup_inputs(seed)
    key = _jax.random.fold_in(_jax.random.key(seed), 7919)
    shape, _ = _output_shape()
    out = dict(inp)
    out["loss_target"] = _jax.random.normal(_jax.random.fold_in(key, 0), shape, _jnp.float32)
    for i, name in enumerate(TWIN_WEIGHTS):
        w = inp[name].astype(_jnp.float32)
        if MOMENT_SCALE is None:
            s = _jnp.sqrt(_jnp.mean(_jnp.square(w)) + 1e-30)
        else:
            s = MOMENT_SCALE[name]
        km, kv = _jax.random.split(_jax.random.fold_in(key, i + 1))
        out[name] = w
        out["m_" + name] = s * _jax.random.normal(km, w.shape, _jnp.float32)
        out["v_" + name] = (s * s) * _jax.random.uniform(kv, w.shape, _jnp.float32, 0.5, 1.5)
    if N_MICROBATCH > 1:
        for name, axis in PER_EXAMPLE_BATCH_AXIS.items():
            out[name] = _to_microbatches(out[name], axis)
    return {'x': out['x'], 'c': out['c'], 'ctx': out['ctx'], 'c_ctx': out['c_ctx'], 'w_mod': out['w_mod'], 'b_mod': out['b_mod'], 'norm1_g': out['norm1_g'], 'w_in': out['w_in'], 'b_gate': out['b_gate'], 'conv_qk': out['conv_qk'], 'head_norm_g': out['head_norm_g'], 'sgu_ln_g': out['sgu_ln_g'], 'sgu_ln_b': out['sgu_ln_b'], 'w_s': out['w_s'], 'b_s': out['b_s'], 'w_branch_mlstm': out['w_branch_mlstm'], 'w_branch_sgu': out['w_branch_sgu'], 'w_out': out['w_out'], 'norm2_g': out['norm2_g'], 'w_up': out['w_up'], 'w_ffn_conv': out['w_ffn_conv'], 'w_down': out['w_down'], 'final_g': out['final_g'], 'loss_target': out['loss_target'], 'm_c_ctx': out['m_c_ctx'], 'm_w_mod': out['m_w_mod'], 'm_b_mod': out['m_b_mod'], 'm_norm1_g': out['m_norm1_g'], 'm_w_in': out['m_w_in'], 'm_b_gate': out['m_b_gate'], 'm_conv_qk': out['m_conv_qk'], 'm_head_norm_g': out['m_head_norm_g'], 'm_sgu_ln_g': out['m_sgu_ln_g'], 'm_sgu_ln_b': out['m_sgu_ln_b'], 'm_w_s': out['m_w_s'], 'm_b_s': out['m_b_s'], 'm_w_branch_mlstm': out['m_w_branch_mlstm'], 'm_w_branch_sgu': out['m_w_branch_sgu'], 'm_w_out': out['m_w_out'], 'm_norm2_g': out['m_norm2_g'], 'm_w_up': out['m_w_up'], 'm_w_ffn_conv': out['m_w_ffn_conv'], 'm_w_down': out['m_w_down'], 'm_final_g': out['m_final_g'], 'v_c_ctx': out['v_c_ctx'], 'v_w_mod': out['v_w_mod'], 'v_b_mod': out['v_b_mod'], 'v_norm1_g': out['v_norm1_g'], 'v_w_in': out['v_w_in'], 'v_b_gate': out['v_b_gate'], 'v_conv_qk': out['v_conv_qk'], 'v_head_norm_g': out['v_head_norm_g'], 'v_sgu_ln_g': out['v_sgu_ln_g'], 'v_sgu_ln_b': out['v_sgu_ln_b'], 'v_w_s': out['v_w_s'], 'v_b_s': out['v_b_s'], 'v_w_branch_mlstm': out['v_w_branch_mlstm'], 'v_w_branch_sgu': out['v_w_branch_sgu'], 'v_w_out': out['v_w_out'], 'v_norm2_g': out['v_norm2_g'], 'v_w_up': out['v_w_up'], 'v_w_ffn_conv': out['v_w_ffn_conv'], 'v_w_down': out['v_w_down'], 'v_final_g': out['v_final_g']}


def _loss(weights, diff, rest, loss_target):
    with _jax.named_scope("forward"):
        args = {**rest, TWIN_DIFF_INPUT: diff, **{k: w.astype(_WEIGHT_DTYPES[k]) for k, w in weights.items()}}
        y = _forward(args)
    with _jax.named_scope("loss_head"):
        err = _jnp.square(y.astype(_jnp.float32) - loss_target)
        return 0.5 * _jnp.sum(_jnp.mean(err, axis=-1)) if err.ndim else 0.5 * err


def _adamw(w, g, m, v):
    m = ADAM_B1 * m + (1.0 - ADAM_B1) * g
    v = ADAM_B2 * v + (1.0 - ADAM_B2) * _jnp.square(g)
    m_hat = m / (1.0 - ADAM_B1 ** ADAM_STEP)
    v_hat = v / (1.0 - ADAM_B2 ** ADAM_STEP)
    delta = -ADAM_LR * (m_hat / (_jnp.sqrt(v_hat) + ADAM_EPS) + ADAM_WD * w)
    return delta, m, v


def reference(x, c, ctx, c_ctx, w_mod, b_mod, norm1_g, w_in, b_gate, conv_qk, head_norm_g, sgu_ln_g, sgu_ln_b, w_s, b_s, w_branch_mlstm, w_branch_sgu, w_out, norm2_g, w_up, w_ffn_conv, w_down, final_g, loss_target, m_c_ctx, m_w_mod, m_b_mod, m_norm1_g, m_w_in, m_b_gate, m_conv_qk, m_head_norm_g, m_sgu_ln_g, m_sgu_ln_b, m_w_s, m_b_s, m_w_branch_mlstm, m_w_branch_sgu, m_w_out, m_norm2_g, m_w_up, m_w_ffn_conv, m_w_down, m_final_g, v_c_ctx, v_w_mod, v_b_mod, v_norm1_g, v_w_in, v_b_gate, v_conv_qk, v_head_norm_g, v_sgu_ln_g, v_sgu_ln_b, v_w_s, v_b_s, v_w_branch_mlstm, v_w_branch_sgu, v_w_out, v_norm2_g, v_w_up, v_w_ffn_conv, v_w_down, v_final_g):
    given = dict(x=x, c=c, ctx=ctx, c_ctx=c_ctx, w_mod=w_mod, b_mod=b_mod, norm1_g=norm1_g, w_in=w_in, b_gate=b_gate, conv_qk=conv_qk, head_norm_g=head_norm_g, sgu_ln_g=sgu_ln_g, sgu_ln_b=sgu_ln_b, w_s=w_s, b_s=b_s, w_branch_mlstm=w_branch_mlstm, w_branch_sgu=w_branch_sgu, w_out=w_out, norm2_g=norm2_g, w_up=w_up, w_ffn_conv=w_ffn_conv, w_down=w_down, final_g=final_g, loss_target=loss_target, m_c_ctx=m_c_ctx, m_w_mod=m_w_mod, m_b_mod=m_b_mod, m_norm1_g=m_norm1_g, m_w_in=m_w_in, m_b_gate=m_b_gate, m_conv_qk=m_conv_qk, m_head_norm_g=m_head_norm_g, m_sgu_ln_g=m_sgu_ln_g, m_sgu_ln_b=m_sgu_ln_b, m_w_s=m_w_s, m_b_s=m_b_s, m_w_branch_mlstm=m_w_branch_mlstm, m_w_branch_sgu=m_w_branch_sgu, m_w_out=m_w_out, m_norm2_g=m_norm2_g, m_w_up=m_w_up, m_w_ffn_conv=m_w_ffn_conv, m_w_down=m_w_down, m_final_g=m_final_g, v_c_ctx=v_c_ctx, v_w_mod=v_w_mod, v_b_mod=v_b_mod, v_norm1_g=v_norm1_g, v_w_in=v_w_in, v_b_gate=v_b_gate, v_conv_qk=v_conv_qk, v_head_norm_g=v_head_norm_g, v_sgu_ln_g=v_sgu_ln_g, v_sgu_ln_b=v_sgu_ln_b, v_w_s=v_w_s, v_b_s=v_b_s, v_w_branch_mlstm=v_w_branch_mlstm, v_w_branch_sgu=v_w_branch_sgu, v_w_out=v_w_out, v_norm2_g=v_norm2_g, v_w_up=v_w_up, v_w_ffn_conv=v_w_ffn_conv, v_w_down=v_w_down, v_final_g=v_final_g)
    weights = {n: given[n] for n in TWIN_WEIGHTS}
    shared = {n: given[n] for n in SHARED_INPUTS}
    per_example = {n: given[n] for n in ['x', 'c', 'ctx']}
    grad_fn = _jax.value_and_grad(_loss, argnums=(0, 1))

    def one_microbatch(ex, loss_target):
        ex = dict(ex)
        diff = ex.pop(TWIN_DIFF_INPUT)
        return grad_fn(weights, diff, {**shared, **ex}, loss_target)

    if N_MICROBATCH == 1:
        loss, (grad_w, grad_x) = one_microbatch(per_example, given["loss_target"])
    else:
        def body(carry, xs):
            loss_sum, grad_sum = carry
            l_k, (gw_k, gx_k) = one_microbatch(xs[0], xs[1])
            with _jax.named_scope("update"):
                return (loss_sum + l_k, _jax.tree.map(_jnp.add, grad_sum, gw_k)), gx_k

        init = (_jnp.zeros((), _jnp.float32), _jax.tree.map(_jnp.zeros_like, weights))
        (loss, grad_w), grad_x = _jax.lax.scan(body, init, (per_example, given["loss_target"]))
    with _jax.named_scope("update"):
        delta_w, new_m, new_v = {}, {}, {}
        for n in TWIN_WEIGHTS:
            delta_w[n], new_m[n], new_v[n] = _adamw(weights[n], grad_w[n], given["m_" + n], given["v_" + n])
    return (loss, grad_x, *[grad_w[n] for n in TWIN_WEIGHTS], *[delta_w[n] for n in TWIN_WEIGHTS],
            *[new_m[n] for n in TWIN_WEIGHTS], *[new_v[n] for n in TWIN_WEIGHTS])
```

```python
import functools
import math

import jax
import jax.numpy as jnp
from jax import lax
from jax.experimental import pallas as pl
from jax.experimental.pallas import tpu as pltpu

F32, BF16 = jnp.float32, jnp.bfloat16
EPS = 1e-6
M_INIT = -1e30
NEG = -1e30
GRID_W = 64
LCH = 256
N_MOD = 6
N_DEV = 8
LANES = 128
ADAM_LR, ADAM_B1, ADAM_B2, ADAM_EPS, ADAM_WD, ADAM_STEP = 0.001, 0.9, 0.999, 1e-08, 0.01, 10
GELU_C = math.sqrt(2.0 / math.pi)
GELU_A = 0.044715
VMEM_LIMIT = 56 * 1024 * 1024
HI = lax.Precision.HIGHEST
SDS = jax.ShapeDtypeStruct
MESH_ID = pl.DeviceIdType.MESH


def _pick(n, cands):
    for c in cands:
        if n % c == 0:
            return c
    raise ValueError(f"no block size for {n} in {cands}")


def _cp(*sem):
    return pltpu.CompilerParams(dimension_semantics=sem if sem else None, vmem_limit_bytes=VMEM_LIMIT)


def _sigmoid(x):
    return 1.0 / (1.0 + jnp.exp(-x))


def _gelu(x):
    return 0.5 * x * (1.0 + jnp.tanh(GELU_C * (x + GELU_A * x * x * x)))


def _gelu_grad(x):
    t = jnp.tanh(GELU_C * (x + GELU_A * x * x * x))
    return 0.5 * (1.0 + t) + 0.5 * x * (1.0 - t * t) * GELU_C * (1.0 + 3.0 * GELU_A * x * x)


def _log_sigmoid(x):
    return jnp.minimum(x, 0.0) - jnp.log(1.0 + jnp.exp(-jnp.abs(x)))


def _dot(a, b):
    return jnp.dot(a, b, preferred_element_type=F32)


def _dot_nt(a, b):
    return lax.dot_general(a, b, (((1,), (1,)), ((), ())), preferred_element_type=F32)


def _dot_tn(a, b):
    return lax.dot_general(a, b, (((0,), (0,)), ((), ())), preferred_element_type=F32)


def _bf(x):
    return x.astype(BF16)


def _allgather(p):
    r = p.shape[0]

    def body(x_ref, out_ref, send_sems, recv_sems, local_sem):
        x, y, c = lax.axis_index("x"), lax.axis_index("y"), lax.axis_index("c")
        me, sibling = (x, y, c), (x, y, 1 - c)
        chips = [(1 - x, y), (x, 1 - y), (1 - x, 1 - y)]

        def slot(px, py, pc):
            return out_ref.at[4 * px + 2 * py + pc]

        def copy(k, block, to, src=None):
            return pltpu.make_async_remote_copy(
                src_ref=slot(*block) if src is None else src, dst_ref=slot(*block),
                send_sem=send_sems.at[k], recv_sem=recv_sems.at[k], device_id=to, device_id_type=MESH_ID)

        mine = pltpu.make_async_copy(x_ref, slot(*me), local_sem)
        mine.start()
        first = [copy(0, me, sibling, src=x_ref)]
        first += [copy(1 + j, me, (*chip, c), src=x_ref) for j, chip in enumerate(chips)]
        for cp in first:
            cp.start()
        passed = [copy(4 + j, (*chip, c), sibling) for j, chip in enumerate(chips)]
        for j, chip in enumerate(chips):
            copy(1 + j, (*chip, c), me).wait_recv()
            passed[j].start()
        copy(0, sibling, me).wait_recv()
        for j, chip in enumerate(chips):
            copy(4 + j, (*chip, 1 - c), me).wait_recv()
        for cp in first + passed:
            cp.wait_send()
        mine.wait()

    return pl.pallas_call(
        body, name="weights_allgather",
        out_shape=SDS((N_DEV, r, LANES), p.dtype),
        in_specs=[pl.BlockSpec(memory_space=pl.ANY)],
        out_specs=pl.BlockSpec(memory_space=pl.ANY),
        scratch_shapes=[pltpu.SemaphoreType.DMA((7,)), pltpu.SemaphoreType.DMA((7,)), pltpu.SemaphoreType.DMA(())],
    )(p)


def _grad_exchange(gbig, gsmall):
    r, rs = gbig.shape[1], gsmall.shape[0]

    def body(gb_ref, gs_ref, rb_ref, rs_ref, send_sems, recv_sems, local_sems):
        x, y, c = lax.axis_index("x"), lax.axis_index("y"), lax.axis_index("c")
        me = 4 * x + 2 * y + c
        loc = [pltpu.make_async_copy(gb_ref.at[me], rb_ref.at[me], local_sems.at[0]),
               pltpu.make_async_copy(gs_ref, rs_ref.at[me], local_sems.at[1])]
        for cp in loc:
            cp.start()
        sends, recvs = [], []
        for k in range(1, N_DEV):
            px = 1 - x if k & 4 else x
            py = 1 - y if k & 2 else y
            pc = 1 - c if k & 1 else c
            peer, pidx = (px, py, pc), 4 * px + 2 * py + pc
            sends.append(pltpu.make_async_remote_copy(
                src_ref=gb_ref.at[pidx], dst_ref=rb_ref.at[me], send_sem=send_sems.at[k - 1],
                recv_sem=recv_sems.at[k - 1], device_id=peer, device_id_type=MESH_ID))
            sends.append(pltpu.make_async_remote_copy(
                src_ref=gs_ref, dst_ref=rs_ref.at[me], send_sem=send_sems.at[6 + k],
                recv_sem=recv_sems.at[6 + k], device_id=peer, device_id_type=MESH_ID))
            recvs.append(pltpu.make_async_remote_copy(
                src_ref=gb_ref.at[pidx], dst_ref=rb_ref.at[pidx], send_sem=send_sems.at[k - 1],
                recv_sem=recv_sems.at[k - 1], device_id=peer, device_id_type=MESH_ID))
            recvs.append(pltpu.make_async_remote_copy(
                src_ref=gs_ref, dst_ref=rs_ref.at[pidx], send_sem=send_sems.at[6 + k],
                recv_sem=recv_sems.at[6 + k], device_id=peer, device_id_type=MESH_ID))
        for cp in sends:
            cp.start()
        for cp in recvs:
            cp.wait_recv()
        for cp in sends:
            cp.wait_send()
        for cp in loc:
            cp.wait()

    return pl.pallas_call(
        body, name="grad_exchange",
        out_shape=(SDS((N_DEV, r, LANES), gbig.dtype), SDS((N_DEV, rs, LANES), gsmall.dtype)),
        in_specs=[pl.BlockSpec(memory_space=pl.ANY), pl.BlockSpec(memory_space=pl.ANY)],
        out_specs=(pl.BlockSpec(memory_space=pl.ANY), pl.BlockSpec(memory_space=pl.ANY)),
        scratch_shapes=[pltpu.SemaphoreType.DMA((14,)), pltpu.SemaphoreType.DMA((14,)), pltpu.SemaphoreType.DMA((2,))],
    )(gbig, gsmall)


def _modulation(cvec, w_mod, b_mod):
    d, n = w_mod.shape

    def body(c_ref, w_ref, b_ref, s_ref, o_ref):
        cv = c_ref[...]
        s = cv * _sigmoid(cv)
        s_ref[...] = s
        o_ref[...] = _dot(_bf(s), w_ref[...]) + b_ref[...]

    return pl.pallas_call(body, name="modulation", out_shape=(SDS((8, d), F32), SDS((8, n), F32)),
                          compiler_params=_cp())(cvec, w_mod, b_mod)


def _norm_mod_proj(xs, g, shsc, w_main, w_gate, t_rows, name):
    s_rows, d = xs.shape
    n = w_main.shape[1]
    tb = _pick(s_rows, (1280, 1024, 256))
    cb = _pick(n, (1408, 1024, 768, 512, 384, 256, 128))
    gate = w_gate is not None

    def body(*refs):
        if gate:
            x_ref, g_ref, ss_ref, wm_ref, wg_ref, hn_ref, z_ref, zg_ref, hn_sc = refs
        else:
            x_ref, g_ref, ss_ref, wm_ref, hn_ref, z_ref, hn_sc = refs
        i, j = pl.program_id(0), pl.program_id(1)

        @pl.when(j == 0)
        def _():
            x = x_ref[...]
            r = lax.rsqrt(jnp.mean(x * x, axis=-1, keepdims=True) + EPS)
            row = i * tb + lax.broadcasted_iota(jnp.int32, (tb, 1), 0)
            isx = row < t_rows
            sh = jnp.where(isx, ss_ref[0:1, :], ss_ref[2:3, :])
            sc = jnp.where(isx, ss_ref[1:2, :], ss_ref[3:4, :])
            hb = _bf((x * r * g_ref[...]) * (1.0 + sc) + sh)
            hn_sc[...] = hb
            hn_ref[...] = hb
            if gate:
                zg_ref[...] = _dot(hb, wg_ref[...])

        z_ref[...] = _bf(_dot(hn_sc[...], wm_ref[...]))

    in_specs = [pl.BlockSpec((tb, d), lambda i, j: (i, 0)), pl.BlockSpec((1, d), lambda i, j: (0, 0)),
                pl.BlockSpec((4, d), lambda i, j: (0, 0)), pl.BlockSpec((d, cb), lambda i, j: (0, j))]
    out_specs = [pl.BlockSpec((tb, d), lambda i, j: (i, 0)), pl.BlockSpec((tb, cb), lambda i, j: (i, j))]
    out_shape = [SDS((s_rows, d), BF16), SDS((s_rows, n), BF16)]
    args = [xs, g, shsc, w_main]
    if gate:
        in_specs.append(pl.BlockSpec((d, LANES), lambda i, j: (0, 0)))
        out_specs.append(pl.BlockSpec((tb, LANES), lambda i, j: (i, 0)))
        out_shape.append(SDS((s_rows, LANES), F32))
        args.append(w_gate)
    return pl.pallas_call(
        body, name=name, grid=(s_rows // tb, n // cb), in_specs=in_specs, out_specs=out_specs, out_shape=out_shape,
        scratch_shapes=[pltpu.VMEM((tb, d), BF16)], compiler_params=_cp("arbitrary", "arbitrary"))(*args)


def _seg_masks(row, t_rows, s_rows):
    prev_ok = (row != 0) & (row != t_rows)
    next_ok = (row != t_rows - 1) & (row != s_rows - 1)
    return prev_ok, next_ok


def _shift_rows(z, halo_prev, halo_next, tb):
    loc = lax.broadcasted_iota(jnp.int32, (tb, 1), 0)
    zp = jnp.where(loc == 0, halo_prev, pltpu.roll(z, 1, 0))
    zn = jnp.where(loc == tb - 1, halo_next, pltpu.roll(z, tb - 1, 0))
    return zp, zn


def _qk_conv(z_main, conv_w, t_rows, md, qscale):
    s_rows = z_main.shape[0]
    tb = _pick(s_rows, (1280, 1024, 256))
    cb = _pick(md, (512, 256, 128))
    nb8 = tb // 8

    def body(zm, zp, zn, w_ref, o_ref):
        i, j = pl.program_id(0), pl.program_id(1)
        z = zm[...].astype(F32)
        zprev, znext = _shift_rows(z, zp[7:8, :].astype(F32), zn[0:1, :].astype(F32), tb)
        row = i * tb + lax.broadcasted_iota(jnp.int32, (tb, 1), 0)
        prev_ok, next_ok = _seg_masks(row, t_rows, s_rows)
        pre = (w_ref[0:1, :] * jnp.where(prev_ok, zprev, 0.0) + w_ref[1:2, :] * z
               + w_ref[2:3, :] * jnp.where(next_ok, znext, 0.0))
        scale = jnp.where(j * cb < md, qscale, 1.0)
        o_ref[...] = _bf(pre * _sigmoid(pre) * scale)

    return pl.pallas_call(
        body, name="qk_conv", grid=(s_rows // tb, 2 * md // cb),
        in_specs=[pl.BlockSpec((tb, cb), lambda i, j: (i, j)),
                  pl.BlockSpec((8, cb), lambda i, j: (jnp.maximum(i * nb8 - 1, 0), j)),
                  pl.BlockSpec((8, cb), lambda i, j: (jnp.minimum((i + 1) * nb8, s_rows // 8 - 1), j)),
                  pl.BlockSpec((3, cb), lambda i, j: (0, j))],
        out_specs=pl.BlockSpec((tb, cb), lambda i, j: (i, j)),
        out_shape=SDS((s_rows, 2 * md), BF16), compiler_params=_cp("arbitrary", "arbitrary"))(z_main, z_main, z_main, conv_w)


def _chunk_gates(gates, bias, rev):
    ln = gates.shape[0]
    gz = gates + bias
    logf = _log_sigmoid(gz)
    r_id = lax.broadcasted_iota(jnp.int32, (ln, ln), 0)
    c_id = lax.broadcasted_iota(jnp.int32, (ln, ln), 1)
    mask = (c_id >= r_id) if rev else (c_id <= r_id)
    mf = mask.astype(F32)
    b_all = jnp.dot(mf, logf, preferred_element_type=F32, precision=HI)
    g_all = jnp.sum(logf, axis=0, keepdims=True)
    return gz, b_all, b_all.T, gz.T, g_all, mask, mf


def _head_weights(b_col, b_row, i_row, m_in, mask):
    d = jnp.where(mask, b_col - b_row + i_row, NEG)
    inter = b_col + m_in
    m_row = jnp.maximum(inter, jnp.max(d, axis=1, keepdims=True))
    return jnp.exp(d - m_row), jnp.exp(inter - m_row), m_row


def _head_state_coeffs(g, b_col, i_col, m_in):
    a = g - b_col + i_col
    m_new = jnp.maximum(g + m_in, jnp.max(a, axis=0, keepdims=True))
    return jnp.exp(g + m_in - m_new), jnp.exp(a - m_new), m_new


def _mlstm_fwd(qk, z_main, zg, bias, nh):
    s_rows = qk.shape[0]
    md = qk.shape[1] // 2
    dh = md // nh
    nc = s_rows // LCH
    ln = LCH

    def chunk_f(i):
        return jnp.where(i == 0, nc - 1, i - 1)

    def chunk_b(i):
        return jnp.where(i == 0, nc - 1, nc - 1 - i)

    def body(qf, kf, vf, gf, qb, kb, vb, gb, bias_ref, hf_ref, hb_ref, cf_ref, nf_ref, mf_ref, cb_ref, nb_ref, mb_ref,
             c_sc, n_sc, m_sc):
        i = pl.program_id(0)

        @pl.when(i == 0)
        def _():
            c_sc[...] = jnp.zeros_like(c_sc)
            n_sc[...] = jnp.zeros_like(n_sc)
            m_sc[...] = jnp.full(m_sc.shape, M_INIT, F32)

        for dr, (q_ref, k_ref, v_ref, g_ref, h_ref, c_out, n_out, m_out) in enumerate(
                ((qf, kf, vf, gf, hf_ref, cf_ref, nf_ref, mf_ref), (qb, kb, vb, gb, hb_ref, cb_ref, nb_ref, mb_ref))):
            gz, b_all, b_t, g_t, g_all, mask, _ = _chunk_gates(g_ref[...], bias_ref[...], dr == 1)
            for h in range(nh):
                ci, cf = 2 * dr * nh + h, (2 * dr + 1) * nh + h
                sl = slice(h * dh, (h + 1) * dh)
                q, k, v = q_ref[:, sl], k_ref[:, sl], v_ref[:, sl]
                c_in, n_in, m_in = c_sc[dr, h], n_sc[dr, h, 0:1, :], m_sc[dr, h, 0:1, 0:1]
                c_out[sl, :] = c_in
                n_out[:, sl] = n_sc[dr, h]
                m_out[h] = m_sc[dr, h]
                b_col, b_row, i_col, i_row = b_all[:, cf:cf + 1], b_t[cf:cf + 1, :], gz[:, ci:ci + 1], g_t[ci:ci + 1, :]
                g = g_all[:, cf:cf + 1]
                w, w_int, m_row = _head_weights(b_col, b_row, i_row, m_in, mask)
                s_mat = _dot_nt(q, k) * w
                num = _dot(_bf(s_mat), v) + w_int * _dot(q, _bf(c_in))
                den = jnp.sum(s_mat, axis=1, keepdims=True) + w_int * jnp.sum(q.astype(F32) * n_in, axis=1, keepdims=True)
                h_ref[:, sl] = _bf(num / jnp.maximum(jnp.abs(den), jnp.exp(-m_row)))
                a_old, coef, m_new = _head_state_coeffs(g, b_col, i_col, m_in)
                kw = k.astype(F32) * coef
                c_sc[dr, h] = a_old * c_in + _dot_tn(_bf(kw), v)
                n_sc[dr, h] = jnp.broadcast_to(a_old * n_in + jnp.sum(kw, axis=0, keepdims=True), (8, dh))
                m_sc[dr, h] = jnp.broadcast_to(m_new, (8, LANES))

    def tok(cfn, col):
        return pl.BlockSpec((ln, md), lambda i: (cfn(i), col))

    def gat(cfn):
        return pl.BlockSpec((ln, LANES), lambda i: (cfn(i), 0))

    def st(cfn, shape):
        return pl.BlockSpec((None,) + shape, lambda i: (cfn(i),) + (0,) * len(shape))

    st_shapes = ((nh * dh, dh), (8, md), (nh, 8, LANES))
    return pl.pallas_call(
        body, name="mlstm_fwd", grid=(nc,),
        in_specs=[tok(chunk_f, 0), tok(chunk_f, 1), tok(chunk_f, 2), gat(chunk_f),
                  tok(chunk_b, 0), tok(chunk_b, 1), tok(chunk_b, 2), gat(chunk_b),
                  pl.BlockSpec((1, LANES), lambda i: (0, 0))],
        out_specs=[tok(chunk_f, 0), tok(chunk_b, 0)] + [st(chunk_f, s) for s in st_shapes] + [st(chunk_b, s) for s in st_shapes],
        out_shape=[SDS((s_rows, md), BF16)] * 2 + [SDS((nc,) + s, F32) for s in st_shapes] * 2,
        scratch_shapes=[pltpu.VMEM((2, nh, dh, dh), F32), pltpu.VMEM((2, nh, 8, dh), F32), pltpu.VMEM((2, nh, 8, LANES), F32)],
        compiler_params=_cp("arbitrary"))(qk, qk, z_main, zg, qk, qk, z_main, zg, bias)


def _head_rms(hs, nh, dh):
    parts, scales = [], []
    for h in range(nh):
        hh = hs[:, h * dh:(h + 1) * dh]
        r = lax.rsqrt(jnp.mean(hh * hh, axis=-1, keepdims=True) + EPS)
        parts.append(hh * r)
        scales.append(r)
    return jnp.concatenate(parts, axis=1), scales


def _layer_norm(v):
    vc = v - jnp.mean(v, axis=-1, keepdims=True)
    r = lax.rsqrt(jnp.mean(vc * vc, axis=-1, keepdims=True) + EPS)
    return vc * r, r


def _sgu_mix(vnb, ws_ref, bs_ref, tb, ng, gd, sc):
    rows = []
    for ch in range(tb // sc):
        cols = []
        for g in range(ng):
            blk = vnb[ch * sc:(ch + 1) * sc, g * gd:(g + 1) * gd]
            cols.append(_dot(_bf(ws_ref[g]), blk) + bs_ref[:, g:g + 1])
        rows.append(jnp.concatenate(cols, axis=1))
    return jnp.concatenate(rows, axis=0)


def _mixer_fwd(hf, hb, z_main, xs, hg, lng, lnb, w_s, b_st, wbm, wbs, wout, mx2, t_rows, nh):
    d = xs.shape[1]
    ng, sc = w_s.shape[0], w_s.shape[1]
    dh, gd = d // nh, d // ng
    tb = _pick(t_rows, (256,))

    def body(hf_ref, hb_ref, zo, zu, zv, zgm, zgg, x_ref, hg_ref, lng_ref, lnb_ref, ws_ref, bs_ref, wbm_ref, wbs_ref,
             wo_ref, mx2_ref, h1_ref, ym_ref, ys_ref, pm_ref, ps_ref, y_ref, out_ref):
        hs = hf_ref[...].astype(F32) + hb_ref[...].astype(F32)
        hn, _ = _head_rms(hs, nh, dh)
        ym = _bf(_sigmoid(zo[...].astype(F32)) * (hn * hg_ref[...]))
        ym_ref[...] = ym
        vhat, _ = _layer_norm(_gelu(zv[...].astype(F32)))
        vnb = _bf(vhat * lng_ref[...] + lnb_ref[...])
        ys = _bf(_gelu(zu[...].astype(F32)) * _sgu_mix(vnb, ws_ref, bs_ref, tb, ng, gd, sc))
        ys_ref[...] = ys
        pm = _dot(ym, wbm_ref[...])
        ps = _dot(ys, wbs_ref[...])
        pm_ref[...] = _bf(pm)
        ps_ref[...] = _bf(ps)
        y = _bf(_sigmoid(zgm[...].astype(F32)) * pm + _sigmoid(zgg[...].astype(F32)) * ps)
        y_ref[...] = y
        out = _dot(y, wo_ref[...])
        out_ref[...] = _bf(out)
        h1_ref[...] = x_ref[...] + mx2_ref[...] * out

    def tok(col):
        return pl.BlockSpec((tb, d), lambda i: (i, col))

    def full(shape):
        return pl.BlockSpec(shape, lambda i: (0,) * len(shape))

    return pl.pallas_call(
        body, name="mixer_fwd", grid=(t_rows // tb,),
        in_specs=[tok(0), tok(0), tok(3), tok(4), tok(5), tok(6), tok(7), tok(0), full((1, d)), full((1, d)), full((1, d)),
                  full((ng, sc, sc)), full((sc, LANES)), full((d, d)), full((d, d)), full((d, d)), full((1, d))],
        out_specs=[tok(0)] * 7,
        out_shape=[SDS((t_rows, d), F32)] + [SDS((t_rows, d), BF16)] * 6,
        compiler_params=_cp("arbitrary"))(hf, hb, z_main, z_main, z_main, z_main, z_main, xs, hg, lng, lnb, w_s, b_st,
                                          wbm, wbs, wout, mx2)


def _grid_taps(a_ext, n_ext):
    col = lax.broadcasted_iota(jnp.int32, (n_ext, 1), 0) % GRID_W
    left = jnp.where(col != 0, pltpu.roll(a_ext, 1, 0), 0.0)
    right = jnp.where(col != GRID_W - 1, pltpu.roll(a_ext, n_ext - 1, 0), 0.0)
    return left, right


def _with_halo(prev, main, nxt, i, ni, tb):
    ext = jnp.concatenate([prev, main, nxt], axis=0).astype(F32)
    pos = lax.broadcasted_iota(jnp.int32, (tb + 2 * GRID_W, 1), 0)
    inside = ((pos >= GRID_W) | (i > 0)) & ((pos < tb + GRID_W) | (i < ni - 1))
    return jnp.where(inside, ext, 0.0)


def _halo_specs(tb, cb, t_rows, col0=0):
    nh64 = tb // GRID_W
    return [pl.BlockSpec((tb, cb), lambda i, j: (i, col0 + j)),
            pl.BlockSpec((GRID_W, cb), lambda i, j: (jnp.maximum(i * nh64 - 1, 0), col0 + j)),
            pl.BlockSpec((GRID_W, cb), lambda i, j: (jnp.minimum((i + 1) * nh64, t_rows // GRID_W - 1), col0 + j))]


def _ffn_tail(ab, w_conv9, w_down, h1, mx5, gfin, target, dff):
    t_rows, d = h1.shape
    tb = _pick(t_rows, (512,))
    cb = _pick(dff, (256, 128))
    ni, nj = t_rows // tb, dff // cb
    n_ext = tb + 2 * GRID_W

    def body(am, ap, an, b_ref, wc_ref, wd_ref, h1_ref, mx5_ref, gf_ref, tg_ref, ac_ref, f_ref, dh2_ref, dffn_ref, st_ref, acc):
        i, j = pl.program_id(0), pl.program_id(1)
        a_ext = _with_halo(ap[...], am[...], an[...], i, ni, tb)
        left, right = _grid_taps(a_ext, n_ext)
        conv = jnp.zeros((tb, cb), F32)
        for di in range(3):
            o = di * GRID_W
            conv = conv + (wc_ref[3 * di:3 * di + 1, :] * left[o:o + tb] + wc_ref[3 * di + 1:3 * di + 2, :] * a_ext[o:o + tb]
                           + wc_ref[3 * di + 2:3 * di + 3, :] * right[o:o + tb])
        ac_ref[...] = _bf(conv)
        fb = _bf(conv * _sigmoid(conv) * b_ref[...].astype(F32))
        f_ref[...] = fb

        @pl.when(j == 0)
        def _():
            acc[...] = jnp.zeros_like(acc)

        @pl.when((i == 0) & (j == 0))
        def _():
            st_ref[...] = jnp.zeros_like(st_ref)

        acc[...] += _dot(fb, wd_ref[...])

        @pl.when(j == nj - 1)
        def _():
            ffn = acc[...]
            h2 = h1_ref[...] + mx5_ref[...] * ffn
            r = lax.rsqrt(jnp.mean(h2 * h2, axis=-1, keepdims=True) + EPS)
            xn = h2 * r
            e = xn * gf_ref[...] - tg_ref[...]
            loss = 0.5 * jnp.sum(jnp.sum(e * e, axis=1, keepdims=True), axis=0, keepdims=True) / d
            dy = e * (1.0 / d)
            dxn = dy * gf_ref[...]
            dh2 = r * (dxn - xn * jnp.mean(dxn * xn, axis=-1, keepdims=True))
            dh2_ref[...] = dh2
            dffn_ref[...] = _bf(dh2 * mx5_ref[...])
            st_ref[...] += jnp.concatenate(
                [jnp.sum(dy * xn, axis=0, keepdims=True), jnp.sum(dh2 * ffn, axis=0, keepdims=True),
                 jnp.broadcast_to(loss, (1, d)), jnp.zeros((5, d), F32)], axis=0)

    def tokd():
        return pl.BlockSpec((tb, d), lambda i, j: (i, 0))

    def rowd():
        return pl.BlockSpec((1, d), lambda i, j: (0, 0))

    return pl.pallas_call(
        body, name="ffn_tail", grid=(ni, nj),
        in_specs=_halo_specs(tb, cb, t_rows) + [pl.BlockSpec((tb, cb), lambda i, j: (i, nj + j)),
                                                pl.BlockSpec((9, cb), lambda i, j: (0, j)),
                                                pl.BlockSpec((cb, d), lambda i, j: (j, 0)), tokd(), rowd(), rowd(), tokd()],
        out_specs=[pl.BlockSpec((tb, cb), lambda i, j: (i, j)), pl.BlockSpec((tb, cb), lambda i, j: (i, j)), tokd(), tokd(),
                   pl.BlockSpec((8, d), lambda i, j: (0, 0))],
        out_shape=[SDS((t_rows, dff), BF16), SDS((t_rows, dff), BF16), SDS((t_rows, d), F32), SDS((t_rows, d), BF16),
                   SDS((8, d), F32)],
        scratch_shapes=[pltpu.VMEM((tb, d), F32)],
        compiler_params=_cp("arbitrary", "arbitrary"))(ab, ab, ab, ab, w_conv9, w_down, h1, mx5, gfin, target)


def _ffn_bwd_gate(dffn, w_down, aconv, ab, dff):
    t_rows, d = dffn.shape
    tb = _pick(t_rows, (512,))
    cb = _pick(dff, (256, 128))
    nj = dff // cb

    def body(g_ref, wd_ref, ac_ref, b_ref, db_ref, dac_ref):
        df = _dot_nt(g_ref[...], wd_ref[...])
        ac = ac_ref[...].astype(F32)
        sa = _sigmoid(ac)
        db_ref[...] = _bf(df * ac * sa)
        dac_ref[...] = _bf(df * b_ref[...].astype(F32) * (sa * (1.0 + ac * (1.0 - sa))))

    blk = pl.BlockSpec((tb, cb), lambda i, j: (i, j))
    return pl.pallas_call(
        body, name="ffn_bwd_gate", grid=(t_rows // tb, nj),
        in_specs=[pl.BlockSpec((tb, d), lambda i, j: (i, 0)), pl.BlockSpec((cb, d), lambda i, j: (j, 0)), blk,
                  pl.BlockSpec((tb, cb), lambda i, j: (i, nj + j))],
        out_specs=[blk, blk], out_shape=[SDS((t_rows, dff), BF16)] * 2,
        compiler_params=_cp("arbitrary", "arbitrary"))(dffn, w_down, aconv, ab)


def _ffn_conv_bwd(dac, ab, w_conv9, dff):
    t_rows = dac.shape[0]
    tb = _pick(t_rows, (512,))
    cb = _pick(dff, (256, 128))
    ni, nj = t_rows // tb, dff // cb
    n_ext = tb + 2 * GRID_W
    nh64 = tb // GRID_W

    def body(dm, dp, dn, am, ap, an, wc_ref, da_ref, gw_ref):
        i = pl.program_id(1)
        d_ext = _with_halo(dp[...], dm[...], dn[...], i, ni, tb)
        a_ext = _with_halo(ap[...], am[...], an[...], i, ni, tb)
        d_left, d_right = _grid_taps(d_ext, n_ext)
        a_left, a_right = _grid_taps(a_ext, n_ext)
        dmain = d_ext[GRID_W:GRID_W + tb]
        da = jnp.zeros((tb, cb), F32)
        rows = []
        for di in range(3):
            o = (2 - di) * GRID_W
            da = da + (wc_ref[3 * di:3 * di + 1, :] * d_right[o:o + tb] + wc_ref[3 * di + 1:3 * di + 2, :] * d_ext[o:o + tb]
                       + wc_ref[3 * di + 2:3 * di + 3, :] * d_left[o:o + tb])
            o = di * GRID_W
            for tap in (a_left, a_ext, a_right):
                rows.append(jnp.sum(dmain * tap[o:o + tb], axis=0, keepdims=True))
        da_ref[...] = _bf(da)

        @pl.when(i == 0)
        def _():
            gw_ref[...] = jnp.zeros_like(gw_ref)

        gw_ref[...] += jnp.concatenate(rows, axis=0)

    def halo(col0):
        return [pl.BlockSpec((tb, cb), lambda j, i: (i, col0 + j)),
                pl.BlockSpec((GRID_W, cb), lambda j, i: (jnp.maximum(i * nh64 - 1, 0), col0 + j)),
                pl.BlockSpec((GRID_W, cb), lambda j, i: (jnp.minimum((i + 1) * nh64, t_rows // GRID_W - 1), col0 + j))]

    return pl.pallas_call(
        body, name="ffn_conv_bwd", grid=(nj, ni),
        in_specs=halo(0) + halo(0) + [pl.BlockSpec((9, cb), lambda j, i: (0, j))],
        out_specs=[pl.BlockSpec((tb, cb), lambda j, i: (i, j)), pl.BlockSpec((9, cb), lambda j, i: (0, j))],
        out_shape=[SDS((t_rows, dff), BF16), SDS((9, dff), F32)],
        compiler_params=_cp("arbitrary", "arbitrary"))(dac, dac, dac, ab, ab, ab, w_conv9)


def _proj_norm_bwd(pairs, x_arr, x_row0, g, scale, resid, m_rows, name):
    d = x_arr.shape[1]
    tm = _pick(m_rows, (512, 256))
    ni = m_rows // tm
    starts, total = [], 0
    for (_, _, _, _, k_p, tk_p) in pairs:
        starts.append(total)
        total += k_p // tk_p
    npairs = len(pairs)
    has_dx = resid is not None

    def body(*refs):
        a_refs, b_refs = refs[0:2 * npairs:2], refs[1:2 * npairs:2]
        rest = refs[2 * npairs:]
        if has_dx:
            x_ref, g_ref, sc_ref, r_ref, dx_ref, st_ref, acc = rest
        else:
            x_ref, g_ref, sc_ref, st_ref, acc = rest
        i, k = pl.program_id(0), pl.program_id(1)

        @pl.when(k == 0)
        def _():
            acc[...] = jnp.zeros_like(acc)

        @pl.when((i == 0) & (k == 0))
        def _():
            st_ref[...] = jnp.zeros_like(st_ref)

        for p in range(npairs):
            nk = pairs[p][4] // pairs[p][5]

            @pl.when((k >= starts[p]) & (k < starts[p] + nk))
            def _(p=p):
                acc[...] += _dot_nt(a_refs[p][...], b_refs[p][...])

        @pl.when(k == total - 1)
        def _():
            dhn = acc[...]
            x = x_ref[...]
            r = lax.rsqrt(jnp.mean(x * x, axis=-1, keepdims=True) + EPS)
            xn = x * r
            dmod = dhn * (1.0 + sc_ref[...])
            dxn = dmod * g_ref[...]
            if has_dx:
                dx_ref[...] = r * (dxn - xn * jnp.mean(dxn * xn, axis=-1, keepdims=True)) + r_ref[...]
            st_ref[...] += jnp.concatenate(
                [jnp.sum(dmod * xn, axis=0, keepdims=True), jnp.sum(dhn, axis=0, keepdims=True),
                 jnp.sum(dhn * (xn * g_ref[...]), axis=0, keepdims=True), jnp.zeros((5, d), F32)], axis=0)

    in_specs, args = [], []
    for p, (a, a_row0, b, b_col0, k_p, tk_p) in enumerate(pairs):
        nk, s0, ar, bc = k_p // tk_p, starts[p], a_row0 // tm, b_col0 // tk_p

        def kk(k, s0=s0, nk=nk):
            return jnp.clip(k - s0, 0, nk - 1)

        in_specs.append(pl.BlockSpec((tm, tk_p), lambda i, k, ar=ar, kk=kk: (ar + i, kk(k))))
        in_specs.append(pl.BlockSpec((d, tk_p), lambda i, k, bc=bc, kk=kk: (0, bc + kk(k))))
        args += [a, b]
    xr = x_row0 // tm
    in_specs += [pl.BlockSpec((tm, d), lambda i, k: (xr + i, 0)), pl.BlockSpec((1, d), lambda i, k: (0, 0)),
                 pl.BlockSpec((1, d), lambda i, k: (0, 0))]
    args += [x_arr, g, scale]
    out_specs, out_shape = [], []
    if has_dx:
        in_specs.append(pl.BlockSpec((tm, d), lambda i, k: (i, 0)))
        args.append(resid)
        out_specs.append(pl.BlockSpec((tm, d), lambda i, k: (i, 0)))
        out_shape.append(SDS((m_rows, d), F32))
    out_specs.append(pl.BlockSpec((8, d), lambda i, k: (0, 0)))
    out_shape.append(SDS((8, d), F32))
    return pl.pallas_call(
        body, name=name, grid=(ni, total), in_specs=in_specs, out_specs=out_specs, out_shape=out_shape,
        scratch_shapes=[pltpu.VMEM((tm, d), F32)], compiler_params=_cp("arbitrary", "arbitrary"))(*args)


def _wgrad(a, b, k_rows, name):
    m, n = a.shape[1], b.shape[1]
    tm = _pick(m, (1408, 1024, 512, 384, 256, 128))
    tn = _pick(n, (1408, 1024, 768, 512, 384, 256, 128))
    tk = _pick(k_rows, (1280, 1024, 256))

    def body(a_ref, b_ref, o_ref):
        @pl.when(pl.program_id(2) == 0)
        def _():
            o_ref[...] = jnp.zeros_like(o_ref)

        o_ref[...] += _dot_tn(a_ref[...], b_ref[...])

    return pl.pallas_call(
        body, name=name, grid=(m // tm, n // tn, k_rows // tk),
        in_specs=[pl.BlockSpec((tk, tm), lambda i, j, k: (k, i)), pl.BlockSpec((tk, tn), lambda i, j, k: (k, j))],
        out_specs=pl.BlockSpec((tm, tn), lambda i, j, k: (i, j)), out_shape=SDS((m, n), F32),
        compiler_params=_cp("arbitrary", "arbitrary", "arbitrary"))(a, b)


def _lane_put(col, lane_idx):
    lane = lax.broadcasted_iota(jnp.int32, (1, LANES), 1)
    return jnp.where(lane == lane_idx, col, 0.0)


def _mixer_bwd(dh1, out, hf, hb, z_main, pm, ps, hg, lng, lnb, w_s, b_st, wbm, wbs, wout, mx2, t_rows, nh):
    d = dh1.shape[1]
    ng, sc = w_s.shape[0], w_s.shape[1]
    dh, gd = d // nh, d // ng
    tb = _pick(t_rows, (256,))

    def body(dh1_ref, out_ref, hf_ref, hb_ref, zo, zu, zv, zgm, zgg, pm_ref, ps_ref, hg_ref, lng_ref, lnb_ref, ws_ref, bs_ref,
             wbm_ref, wbs_ref, wo_ref, mx2_ref, dz_ref, dhs_ref, dout_ref, dpm_ref, dps_ref, st_ref, dws_ref, dbs_ref):
        i = pl.program_id(0)

        @pl.when(i == 0)
        def _():
            st_ref[...] = jnp.zeros_like(st_ref)
            dws_ref[...] = jnp.zeros_like(dws_ref)
            dbs_ref[...] = jnp.zeros_like(dbs_ref)

        dh1v = dh1_ref[...]
        doutb = _bf(dh1v * mx2_ref[...])
        dout_ref[...] = doutb
        d_mx2 = jnp.sum(dh1v * out_ref[...].astype(F32), axis=0, keepdims=True)
        dy = _dot_nt(doutb, wo_ref[...])
        sgm, sgg = _sigmoid(zgm[...].astype(F32)), _sigmoid(zgg[...].astype(F32))
        dpmb, dpsb = _bf(dy * sgm), _bf(dy * sgg)
        dpm_ref[...] = dpmb
        dps_ref[...] = dpsb
        dz_ref[:, 3 * d:4 * d] = _bf(dy * pm_ref[...].astype(F32) * sgm * (1.0 - sgm))
        dz_ref[:, 4 * d:5 * d] = _bf(dy * ps_ref[...].astype(F32) * sgg * (1.0 - sgg))
        dym = _dot_nt(dpmb, wbm_ref[...])
        dys = _dot_nt(dpsb, wbs_ref[...])
        hs = hf_ref[...].astype(F32) + hb_ref[...].astype(F32)
        hn, scales = _head_rms(hs, nh, dh)
        so = _sigmoid(zo[...].astype(F32))
        dz_ref[:, 0:d] = _bf(dym * (hn * hg_ref[...]) * so * (1.0 - so))
        dhmn = dym * so
        d_hg = jnp.sum(dhmn * hn, axis=0, keepdims=True)
        dhn = dhmn * hg_ref[...]
        for h in range(nh):
            sl = slice(h * dh, (h + 1) * dh)
            dhs_ref[:, sl] = _bf(scales[h] * (dhn[:, sl] - hn[:, sl] * jnp.mean(dhn[:, sl] * hn[:, sl], axis=-1, keepdims=True)))
        zuv, zvv = zu[...].astype(F32), zv[...].astype(F32)
        u = _gelu(zuv)
        vhat, rstd = _layer_norm(_gelu(zvv))
        vnb = _bf(vhat * lng_ref[...] + lnb_ref[...])
        mixed = _sgu_mix(vnb, ws_ref, bs_ref, tb, ng, gd, sc)
        dz_ref[:, d:2 * d] = _bf(dys * mixed * _gelu_grad(zuv))
        dmix = dys * u
        rows = []
        dbs = jnp.zeros((sc, LANES), F32)
        for ch in range(tb // sc):
            cols = []
            for g in range(ng):
                dm = dmix[ch * sc:(ch + 1) * sc, g * gd:(g + 1) * gd]
                dmb = _bf(dm)
                dws_ref[g] += _dot_nt(dmb, vnb[ch * sc:(ch + 1) * sc, g * gd:(g + 1) * gd])
                dbs = dbs + _lane_put(jnp.sum(dm, axis=1, keepdims=True), g)
                cols.append(_dot_tn(_bf(ws_ref[g]), dmb))
            rows.append(jnp.concatenate(cols, axis=1))
        dbs_ref[...] += dbs
        dvn = jnp.concatenate(rows, axis=0)
        d_lng = jnp.sum(dvn * vhat, axis=0, keepdims=True)
        d_lnb = jnp.sum(dvn, axis=0, keepdims=True)
        dvh = dvn * lng_ref[...]
        dvg = rstd * (dvh - jnp.mean(dvh, axis=-1, keepdims=True) - vhat * jnp.mean(dvh * vhat, axis=-1, keepdims=True))
        dz_ref[:, 2 * d:3 * d] = _bf(dvg * _gelu_grad(zvv))
        st_ref[...] += jnp.concatenate([d_mx2, d_hg, d_lng, d_lnb, jnp.zeros((4, d), F32)], axis=0)

    def tok(col):
        return pl.BlockSpec((tb, d), lambda i: (i, col))

    def full(shape):
        return pl.BlockSpec(shape, lambda i: (0,) * len(shape))

    return pl.pallas_call(
        body, name="mixer_bwd", grid=(t_rows // tb,),
        in_specs=[tok(0), tok(0), tok(0), tok(0), tok(3), tok(4), tok(5), tok(6), tok(7), tok(0), tok(0), full((1, d)),
                  full((1, d)), full((1, d)), full((ng, sc, sc)), full((sc, LANES)), full((d, d)), full((d, d)), full((d, d)),
                  full((1, d))],
        out_specs=[pl.BlockSpec((tb, 5 * d), lambda i: (i, 0)), tok(0), tok(0), tok(0), tok(0), full((8, d)), full((ng, sc, sc)),
                   full((sc, LANES))],
        out_shape=[SDS((t_rows, 5 * d), BF16)] + [SDS((t_rows, d), BF16)] * 4 + [SDS((8, d), F32), SDS((ng, sc, sc), F32),
                                                                                SDS((sc, LANES), F32)],
        compiler_params=_cp("arbitrary"))(dh1, out, hf, hb, z_main, z_main, z_main, z_main, z_main, pm, ps, hg, lng, lnb, w_s,
                                          b_st, wbm, wbs, wout, mx2)


def _mlstm_bwd(qk, z_main, zg, bias, dhs, states_f, states_b, nh, t_rows):
    s_rows = qk.shape[0]
    md = qk.shape[1] // 2
    dh = md // nh
    nc = s_rows // LCH
    nx = t_rows // LCH
    ln = LCH

    def chunk_f(i):
        return jnp.where(i == nc - 1, nc - 1, nc - 2 - i)

    def chunk_b(i):
        return jnp.where(i == nc - 1, nc - 1, i)

    def body(qf, kf, vf, gf, dhf, cf, nf, mf_, qb, kb, vb, gb, dhb, cb, nb, mb_, bias_ref, dqkvf_ref, dgf_ref, dqkvb_ref, dgb_ref,
             dc_sc, dn_sc):
        i = pl.program_id(0)
        is_ctx = i == nc - 1

        @pl.when(i == 0)
        def _():
            dc_sc[...] = jnp.zeros_like(dc_sc)
            dn_sc[...] = jnp.zeros_like(dn_sc)

        for dr, (q_ref, k_ref, v_ref, g_ref, dh_ref, c_ref, n_ref, m_ref, dqkv_ref, dg_ref) in enumerate(
                ((qf, kf, vf, gf, dhf, cf, nf, mf_, dqkvf_ref, dgf_ref), (qb, kb, vb, gb, dhb, cb, nb, mb_, dqkvb_ref, dgb_ref))):
            gz, b_all, b_t, g_t, g_all, mask, mfl = _chunk_gates(g_ref[...], bias_ref[...], dr == 1)
            x1 = jnp.zeros((ln, LANES), F32)
            x2 = jnp.zeros((ln, LANES), F32)
            dig = jnp.zeros((ln, LANES), F32)
            e_row = jnp.zeros((1, LANES), F32)
            for h in range(nh):
                ci, cfl = 2 * dr * nh + h, (2 * dr + 1) * nh + h
                sl = slice(h * dh, (h + 1) * dh)
                q, k, v = q_ref[:, sl], k_ref[:, sl], v_ref[:, sl]
                qf32, kf32 = q.astype(F32), k.astype(F32)
                dhv = jnp.where(is_ctx, 0.0, dh_ref[:, sl].astype(F32))
                c_in, n_in, m_in = c_ref[sl, :], n_ref[0:1, sl], m_ref[h, 0:1, 0:1]
                b_col, b_row, i_col, i_row = b_all[:, cfl:cfl + 1], b_t[cfl:cfl + 1, :], gz[:, ci:ci + 1], g_t[ci:ci + 1, :]
                g = g_all[:, cfl:cfl + 1]
                w, w_int, m_row = _head_weights(b_col, b_row, i_row, m_in, mask)
                s_mat = _dot_nt(q, k) * w
                sb, cb16 = _bf(s_mat), _bf(c_in)
                num = _dot(sb, v) + w_int * _dot(q, cb16)
                den = jnp.sum(s_mat, axis=1, keepdims=True) + w_int * jnp.sum(qf32 * n_in, axis=1, keepdims=True)
                e_m = jnp.exp(-m_row)
                dnm = jnp.maximum(jnp.abs(den), e_m)
                dnum = dhv / dnm
                hdh = jnp.sum((num / dnm) * dhv, axis=1, keepdims=True)
                dden = jnp.where(jnp.abs(den) > e_m, -(hdh / dnm) * jnp.sign(den), 0.0)
                dnum_b = _bf(dnum)
                ds = _dot_nt(dnum_b, v) + dden
                pb = _bf(w * ds)
                gmat = s_mat * ds
                a_old, coef, _ = _head_state_coeffs(g, b_col, i_col, m_in)
                dc_new, dn_new = dc_sc[dr, h], dn_sc[dr, h, 0:1, :]
                dcb = _bf(dc_new)
                dv = _dot_tn(sb, dnum_b) + _dot(_bf(kf32 * coef), dcb)
                dq_inter = w_int * (_dot_nt(dnum_b, cb16) + dden * n_in)
                dq = _dot(pb, k) + dq_inter
                dk_state = coef * (_dot_nt(v, dcb) + dn_new)
                dk = _dot_tn(pb, q) + dk_state
                dqkv_ref[:, sl] = _bf(dq)
                dqkv_ref[:, md + h * dh:md + (h + 1) * dh] = _bf(dk)
                dqkv_ref[:, 2 * md + h * dh:2 * md + (h + 1) * dh] = _bf(dv)
                row_intra = jnp.sum(gmat, axis=1, keepdims=True)
                col_intra = jnp.sum(gmat.T, axis=1, keepdims=True)
                row_inter = jnp.sum(qf32 * dq_inter, axis=1, keepdims=True)
                col_inter = jnp.sum(kf32 * dk_state, axis=1, keepdims=True)
                e_old = a_old * (jnp.sum(jnp.sum(dc_new * c_in, axis=1, keepdims=True), axis=0, keepdims=True)
                                 + jnp.sum(dn_new * n_in, axis=1, keepdims=True))
                x1 = x1 + _lane_put(row_intra - col_intra + row_inter, cfl)
                x2 = x2 + _lane_put(col_inter, cfl)
                e_row = e_row + _lane_put(e_old, cfl)
                dig = dig + _lane_put(col_intra + col_inter, ci)
                dc_sc[dr, h] = a_old * dc_new + _dot_tn(_bf(qf32 * w_int), dnum_b)
                dn_sc[dr, h] = jnp.broadcast_to(a_old * dn_new + jnp.sum(qf32 * (w_int * dden), axis=0, keepdims=True), (8, dh))
            dlogf = (lax.dot_general(mfl, x1, (((0,), (0,)), ((), ())), preferred_element_type=F32, precision=HI)
                     + jnp.dot(mfl, x2, preferred_element_type=F32, precision=HI) - x2 + e_row)
            dg_ref[...] = dig + dlogf * _sigmoid(-gz)

    def tok(cfn, col):
        return pl.BlockSpec((ln, md), lambda i: (cfn(i), col))

    def dht(cfn):
        return pl.BlockSpec((ln, md), lambda i: (jnp.minimum(cfn(i), nx - 1), 0))

    def gat(cfn):
        return pl.BlockSpec((ln, LANES), lambda i: (cfn(i), 0))

    def st(cfn, shape):
        return pl.BlockSpec((None,) + shape, lambda i: (cfn(i),) + (0,) * len(shape))

    st_shapes = ((nh * dh, dh), (8, md), (nh, 8, LANES))

    def side(cfn):
        return [tok(cfn, 0), tok(cfn, 1), tok(cfn, 2), gat(cfn), dht(cfn)] + [st(cfn, s) for s in st_shapes]

    def outs(cfn):
        return [pl.BlockSpec((ln, 3 * md), lambda i: (cfn(i), 0)), gat(cfn)]

    return pl.pallas_call(
        body, name="mlstm_bwd", grid=(nc,),
        in_specs=side(chunk_f) + side(chunk_b) + [pl.BlockSpec((1, LANES), lambda i: (0, 0))],
        out_specs=outs(chunk_f) + outs(chunk_b),
        out_shape=[SDS((s_rows, 3 * md), BF16), SDS((s_rows, LANES), F32)] * 2,
        scratch_shapes=[pltpu.VMEM((2, nh, dh, dh), F32), pltpu.VMEM((2, nh, 8, dh), F32)],
        compiler_params=_cp("arbitrary"))(qk, qk, z_main, zg, dhs, *states_f, qk, qk, z_main, zg, dhs, *states_b, bias)


def _qkv_conv_bwd(dqkv_f, dqkv_b, z_main, conv_w, t_rows, md, qscale):
    s_rows = z_main.shape[0]
    tb = _pick(s_rows, (1280, 1024, 256))
    cb = _pick(md, (512, 256, 128))
    ni, nj, ncq = s_rows // tb, 3 * md // cb, 2 * md // cb
    nb8 = tb // 8
    n_ext = tb + 16

    def body(fm, fp, fn, bm, bp, bn, zm, zp, zn, w_ref, dz_ref, gw_ref):
        j, i = pl.program_id(0), pl.program_id(1)

        @pl.when(j < ncq)
        def _():
            z = jnp.concatenate([zp[...], zm[...], zn[...]], axis=0).astype(F32)
            dqk = (jnp.concatenate([fp[...], fm[...], fn[...]], axis=0).astype(F32)
                   + jnp.concatenate([bp[...], bm[...], bn[...]], axis=0).astype(F32)) * jnp.where(j * cb < md, qscale, 1.0)
            row = i * tb - 8 + lax.broadcasted_iota(jnp.int32, (n_ext, 1), 0)
            prev_ok, next_ok = _seg_masks(row, t_rows, s_rows)
            zprev = jnp.where(prev_ok, pltpu.roll(z, 1, 0), 0.0)
            znext = jnp.where(next_ok, pltpu.roll(z, n_ext - 1, 0), 0.0)
            pre = w_ref[0:1, :] * zprev + w_ref[1:2, :] * z + w_ref[2:3, :] * znext
            sg = _sigmoid(pre)
            dpre = dqk * (sg * (1.0 + pre * (1.0 - sg)))
            dz = (w_ref[1:2, :] * dpre + w_ref[0:1, :] * jnp.where(next_ok, pltpu.roll(dpre, n_ext - 1, 0), 0.0)
                  + w_ref[2:3, :] * jnp.where(prev_ok, pltpu.roll(dpre, 1, 0), 0.0))
            dz_ref[...] = _bf(dz[8:8 + tb])
            dm = dpre[8:8 + tb]

            @pl.when(i == 0)
            def _():
                gw_ref[...] = jnp.zeros_like(gw_ref)

            gw_ref[...] += jnp.concatenate(
                [jnp.sum(dm * zprev[8:8 + tb], axis=0, keepdims=True), jnp.sum(dm * z[8:8 + tb], axis=0, keepdims=True),
                 jnp.sum(dm * znext[8:8 + tb], axis=0, keepdims=True), jnp.zeros((5, cb), F32)], axis=0)

        @pl.when(j >= ncq)
        def _():
            dz_ref[...] = _bf(fm[...].astype(F32) + bm[...].astype(F32))

    def halo(clampj):
        def cj(j):
            return jnp.minimum(j, ncq - 1) if clampj else j
        return [pl.BlockSpec((tb, cb), lambda j, i: (i, cj(j))),
                pl.BlockSpec((8, cb), lambda j, i: (jnp.maximum(i * nb8 - 1, 0), cj(j))),
                pl.BlockSpec((8, cb), lambda j, i: (jnp.minimum((i + 1) * nb8, s_rows // 8 - 1), cj(j)))]

    return pl.pallas_call(
        body, name="qkv_conv_bwd", grid=(nj, ni),
        in_specs=halo(False) + halo(False) + halo(True) + [pl.BlockSpec((3, cb), lambda j, i: (0, jnp.minimum(j, ncq - 1)))],
        out_specs=[pl.BlockSpec((tb, cb), lambda j, i: (i, j)), pl.BlockSpec((8, cb), lambda j, i: (0, jnp.minimum(j, ncq - 1)))],
        out_shape=[SDS((s_rows, 3 * md), BF16), SDS((8, 2 * md), F32)],
        compiler_params=_cp("arbitrary", "arbitrary"))(dqkv_f, dqkv_f, dqkv_f, dqkv_b, dqkv_b, dqkv_b, z_main, z_main, z_main, conv_w)


def _gate_grad_sum(dg_f, dg_b):
    s_rows = dg_f.shape[0]
    tb = _pick(s_rows, (1280, 1024, 256))

    def body(a_ref, b_ref, o_ref, st_ref):
        @pl.when(pl.program_id(0) == 0)
        def _():
            st_ref[...] = jnp.zeros_like(st_ref)

        s = a_ref[...] + b_ref[...]
        o_ref[...] = _bf(s)
        st_ref[...] += jnp.concatenate([jnp.sum(s, axis=0, keepdims=True), jnp.zeros((7, LANES), F32)], axis=0)

    blk = pl.BlockSpec((tb, LANES), lambda i: (i, 0))
    return pl.pallas_call(
        body, name="gate_grad_sum", grid=(s_rows // tb,), in_specs=[blk, blk],
        out_specs=[blk, pl.BlockSpec((8, LANES), lambda i: (0, 0))],
        out_shape=[SDS((s_rows, LANES), BF16), SDS((8, LANES), F32)], compiler_params=_cp("arbitrary"))(dg_f, dg_b)


def _mod_grads(silu_slots, dmx_sh, dmx_slots, dmc_tot, dmc_sh, silu_cctx, c_ctx, w_mod_c):
    d = silu_slots.shape[1]
    ncol, n6 = dmx_sh.shape[1], dmx_slots.shape[1]

    def body(ss_ref, dsh_ref, dsl_ref, dct_ref, dcs_ref, sc_ref, c_ref, w_ref, gw_ref, gb_ref, gc_ref):
        a = jnp.concatenate([ss_ref[...], sc_ref[...], jnp.zeros((7, d), F32)], axis=0)
        b = jnp.concatenate([dsh_ref[...], dcs_ref[...], jnp.zeros((7, ncol), F32)], axis=0)
        gw_ref[...] = lax.dot_general(a, b, (((0,), (0,)), ((), ())), preferred_element_type=F32, precision=HI)
        dct = dct_ref[...]
        gb_ref[...] = jnp.sum(dsl_ref[...], axis=0, keepdims=True) + jnp.concatenate(
            [dct, jnp.zeros((1, n6 - dct.shape[1]), F32)], axis=1)
        t = _dot_nt(_bf(jnp.broadcast_to(dct, (8, dct.shape[1]))), w_ref[...])
        cv = c_ref[...]
        s = _sigmoid(cv)
        gc_ref[...] = t[0:1, :] * (s * (1.0 + cv * (1.0 - s)))

    return pl.pallas_call(body, name="mod_grads", out_shape=[SDS((d, ncol), F32), SDS((1, n6), F32), SDS((1, d), F32)],
                          compiler_params=_cp())(silu_slots, dmx_sh, dmx_slots, dmc_tot, dmc_sh, silu_cctx, c_ctx, w_mod_c)


def _slot_sum(slots):
    ns, r = slots.shape[0], slots.shape[1]
    tb = _pick(r, (1024, 512, 256, 128, 64, 32, 16, 8))

    def body(s_ref, o_ref):
        acc = s_ref[0]
        for k in range(1, ns):
            acc = acc + s_ref[k]
        o_ref[...] = acc

    return pl.pallas_call(
        body, name="slot_sum", grid=(r // tb,), in_specs=[pl.BlockSpec((ns, tb, LANES), lambda i: (0, i, 0))],
        out_specs=pl.BlockSpec((tb, LANES), lambda i: (i, 0)), out_shape=SDS((r, LANES), F32),
        compiler_params=_cp("arbitrary"))(slots)


def _adamw(w, gslots, m, v, name):
    ns, r = gslots.shape[0], gslots.shape[1]
    tb = _pick(r, (1024, 512, 256, 128, 64, 32, 16, 8))
    bc1, bc2 = 1.0 - ADAM_B1 ** ADAM_STEP, 1.0 - ADAM_B2 ** ADAM_STEP

    def body(w_ref, g_ref, m_ref, v_ref, go_ref, d_ref, mo_ref, vo_ref):
        g = g_ref[0]
        for k in range(1, ns):
            g = g + g_ref[k]
        mn = ADAM_B1 * m_ref[...] + (1.0 - ADAM_B1) * g
        vn = ADAM_B2 * v_ref[...] + (1.0 - ADAM_B2) * (g * g)
        go_ref[...] = g
        mo_ref[...] = mn
        vo_ref[...] = vn
        d_ref[...] = -ADAM_LR * ((mn / bc1) / (jnp.sqrt(vn / bc2) + ADAM_EPS) + ADAM_WD * w_ref[...])

    blk = pl.BlockSpec((tb, LANES), lambda i: (i, 0))
    return pl.pallas_call(
        body, name=name, grid=(r // tb,), in_specs=[blk, pl.BlockSpec((ns, tb, LANES), lambda i: (0, i, 0)), blk, blk],
        out_specs=[blk] * 4, out_shape=[SDS((r, LANES), F32)] * 4, compiler_params=_cp("arbitrary"))(w, gslots, m, v)


def _pack(parts, row_mult):
    flat = jnp.concatenate([p.reshape(-1) for p in parts])
    n = flat.shape[0]
    rows = -(-n // LANES)
    rows = -(-rows // row_mult) * row_mult
    return jnp.pad(flat, (0, rows * LANES - n)).reshape(rows, LANES)


def _pack_slots(parts, row_mult):
    flat = jnp.concatenate(parts, axis=1)
    n = flat.shape[1]
    rows = -(-n // LANES)
    rows = -(-rows // row_mult) * row_mult
    return jnp.pad(flat, ((0, 0), (0, rows * LANES - n))).reshape(flat.shape[0], rows, LANES)


def _unpack(buf, shapes):
    flat = buf.reshape(-1)
    out, off = [], 0
    for s in shapes:
        n = math.prod(s)
        out.append(flat[off:off + n].reshape(s))
        off += n
    return out


def _col_shards(full):
    lead, n = full.shape[:-1], full.shape[-1] // N_DEV
    return jnp.moveaxis(full.reshape(lead + (N_DEV, n)), -2, 0).reshape(N_DEV, -1)


def _col_gather(seg, lead):
    n = seg.shape[1] // math.prod(lead)
    return jnp.moveaxis(seg.reshape((N_DEV,) + lead + (n,)), 0, -2).reshape(lead + (N_DEV * n,))


def _as_bf16_pairs(a):
    return lax.bitcast_convert_type(a, BF16).reshape(-1)


def _from_bf16_pairs(seg):
    return lax.bitcast_convert_type(seg.reshape(seg.shape[:-1] + (seg.shape[-1] // 2, 2)), F32)


def _pad_lanes(a):
    return jnp.pad(a, ((0, 0), (0, LANES - a.shape[1])))


def kernel(x, c, ctx, c_ctx, w_mod, b_mod, norm1_g, w_in, b_gate, conv_qk, head_norm_g, sgu_ln_g, sgu_ln_b, w_s, b_s, w_branch_mlstm, w_branch_sgu, w_out, norm2_g, w_up, w_ffn_conv, w_down, final_g, loss_target, m_c_ctx, m_w_mod, m_b_mod, m_norm1_g, m_w_in, m_b_gate, m_conv_qk, m_head_norm_g, m_sgu_ln_g, m_sgu_ln_b, m_w_s, m_b_s, m_w_branch_mlstm, m_w_branch_sgu, m_w_out, m_norm2_g, m_w_up, m_w_ffn_conv, m_w_down, m_final_g, v_c_ctx, v_w_mod, v_b_mod, v_norm1_g, v_w_in, v_b_gate, v_conv_qk, v_head_norm_g, v_sgu_ln_g, v_sgu_ln_b, v_w_s, v_b_s, v_w_branch_mlstm, v_w_branch_sgu, v_w_out, v_norm2_g, v_w_up, v_w_ffn_conv, v_w_down, v_final_g):
    t, d = x.shape[1], x.shape[2]
    n_ctx = ctx.shape[1]
    s_rows = t + n_ctx
    nh = b_gate.shape[1] // 4
    md = head_norm_g.shape[1]
    dh = md // nh
    ng, sc = w_s.shape[1], w_s.shape[2]
    dff = w_down.shape[1] * N_DEV
    n_in = w_in.shape[2] * N_DEV
    assert md == d and sgu_ln_g.shape[1] == d and n_ctx == LCH and t % LCH == 0 and t % (8 * GRID_W) == 0
    assert n_in == 8 * d + 4 * nh and 4 * nh <= LANES
    me = 4 * lax.axis_index("x") + 2 * lax.axis_index("y") + lax.axis_index("c")

    big_w = [w_in[0], w_branch_mlstm[0], w_branch_sgu[0], w_out[0], w_up[0], w_down[0]]
    send = _pack([_bf(w_mod[0])] + [_bf(a) for a in big_w] + [_as_bf16_pairs(conv_qk[0]), _as_bf16_pairs(w_ffn_conv[0])], 16)
    gathered = _allgather(send).reshape(N_DEV, -1)
    sizes = [w_mod[0].size] + [a.size for a in big_w] + [2 * conv_qk[0].size, 2 * w_ffn_conv[0].size]
    segs, off = [], 0
    for n in sizes:
        segs.append(gathered[:, off:off + n])
        off += n
    w_mod_f = _col_gather(segs[0], (d,))
    w_in_f = _col_gather(segs[1], (d,))
    wbm_f, wbs_f, wout_f = (segs[k].reshape(d, d) for k in (2, 3, 4))
    w_up_f = _col_gather(segs[5], (d,))
    w_down_f = segs[6].reshape(dff, d)
    convw = _col_gather(_from_bf16_pairs(segs[7]), (3,))
    wconv9 = _col_gather(_from_bf16_pairs(segs[8]), (3, 3)).reshape(9, dff)
    w_main = jnp.concatenate([w_in_f[:, :3 * md], w_in_f[:, 3 * md + 4 * nh:]], axis=1)
    w_gate = _pad_lanes(w_in_f[:, 3 * md:3 * md + 4 * nh])

    cvec = jnp.concatenate([c, c_ctx[None], jnp.zeros((6, d), F32)], axis=0)
    silu_v, mod = _modulation(cvec, w_mod_f, b_mod)
    mx = [mod[0:1, k * d:(k + 1) * d] for k in range(N_MOD)]
    mc = [mod[1:2, k * d:(k + 1) * d] for k in range(2)]
    xs = jnp.concatenate([x[0], ctx[0]], axis=0)
    hn, z_main, zg = _norm_mod_proj(xs, norm1_g, jnp.concatenate([mx[0], mx[1], mc[0], mc[1]], axis=0), w_main, w_gate, t, "in_proj")
    qscale = dh ** -0.5
    qk = _qk_conv(z_main, convw, t, md, qscale)
    bias = _pad_lanes(b_gate)
    fwd = _mlstm_fwd(qk, z_main, zg, bias, nh)
    hf, hb, states_f, states_b = fwd[0], fwd[1], fwd[2:5], fwd[5:8]
    b_st = _pad_lanes(b_s[0].T)
    h1, ym, ys, pm, ps, y, out = _mixer_fwd(hf, hb, z_main, xs, head_norm_g, sgu_ln_g, sgu_ln_b, w_s[0], b_st, wbm_f, wbs_f,
                                            wout_f, mx[2], t, nh)
    hn2, ab = _norm_mod_proj(h1, norm2_g, jnp.concatenate([mx[3], mx[4], mx[3], mx[4]], axis=0), w_up_f, None, t, "up_proj")
    aconv, f, dh2, dffn, st_tail = _ffn_tail(ab, wconv9, w_down_f, h1, mx[5], final_g[None], loss_target[0], dff)

    db, dac = _ffn_bwd_gate(dffn, w_down_f, aconv, ab, dff)
    da, g_wconv9 = _ffn_conv_bwd(dac, ab, wconv9, dff)
    g_wdown = _wgrad(f, dffn, t, "wgrad_down")
    g_wup = jnp.concatenate([_wgrad(hn2, da, t, "wgrad_up_a"), _wgrad(hn2, db, t, "wgrad_up_b")], axis=1)
    tkf = _pick(dff, (1408, 704, 384, 128))
    dh1, st_n2 = _proj_norm_bwd([(da, 0, w_up_f, 0, dff, tkf), (db, 0, w_up_f, dff, dff, tkf)], h1, 0, norm2_g, mx[4], dh2, t,
                                "up_proj_bwd")
    dz_rest, dhs, dout, dpm, dps, st_mix, g_ws, g_bst = _mixer_bwd(dh1, out, hf, hb, z_main, pm, ps, head_norm_g, sgu_ln_g,
                                                                    sgu_ln_b, w_s[0], b_st, wbm_f, wbs_f, wout_f, mx[2], t, nh)
    g_wout = _wgrad(y, dout, t, "wgrad_out")
    g_wbm = _wgrad(ym, dpm, t, "wgrad_branch_mlstm")
    g_wbs = _wgrad(ys, dps, t, "wgrad_branch_sgu")
    dqkv_f, dg_f, dqkv_b, dg_b = _mlstm_bwd(qk, z_main, zg, bias, dhs, states_f, states_b, nh, t)
    dz_qkv, g_convqk = _qkv_conv_bwd(dqkv_f, dqkv_b, z_main, convw, t, md, qscale)
    dz_g, st_gate = _gate_grad_sum(dg_f, dg_b)
    g_win = jnp.concatenate([_wgrad(hn, dz_qkv, s_rows, "wgrad_in_qkv"), _wgrad(hn, dz_g, s_rows, "wgrad_in_gate")[:, :4 * nh],
                             _wgrad(hn, dz_rest, t, "wgrad_in_rest")], axis=1)
    tk = _pick(md, (1024, 512, 256))
    grad_x, st_n1x = _proj_norm_bwd(
        [(dz_qkv, 0, w_main, 0, 3 * md, tk), (dz_rest, 0, w_main, 3 * md, 5 * d, tk), (dz_g, 0, w_gate, 0, LANES, LANES)],
        xs, 0, norm1_g, mx[1], dh1, t, "in_proj_bwd")
    (st_n1c,) = _proj_norm_bwd([(dz_qkv, t, w_main, 0, 3 * md, tk), (dz_g, t, w_gate, 0, LANES, LANES)],
                               xs, t, norm1_g, mc[1], None, n_ctx, "in_proj_bwd_ctx")

    gbig = _pack_slots([_col_shards(g_win), g_wbm.reshape(N_DEV, -1), g_wbs.reshape(N_DEV, -1), g_wout.reshape(N_DEV, -1),
                        _col_shards(g_wup), g_wdown.reshape(N_DEV, -1), _col_shards(g_convqk[:3]),
                        _col_shards(g_wconv9.reshape(3, 3, dff))], 1024)
    small_parts = [st_n1x[1], st_n1x[2], st_mix[0], st_n2[1], st_n2[2], st_tail[1],
                   st_n1c[1], st_n1c[2],
                   silu_v[0], st_n1x[0] + st_n1c[0], st_gate[0], st_mix[1], st_mix[2], st_mix[3],
                   g_ws.reshape(-1), g_bst[:, :ng].T.reshape(-1), st_n2[0], st_tail[0]]
    gsmall = _pack(small_parts, 8)
    recv_big, recv_small = _grad_exchange(gbig, gsmall)
    small_sum = _slot_sum(recv_small).reshape(-1)
    small_slots = recv_small.reshape(N_DEV, -1)
    o_silu, o_n1 = 8 * d, 9 * d
    ncol = N_MOD * d // N_DEV
    dmc_tot = small_sum[6 * d:8 * d][None]
    dmc_pad = jnp.concatenate([dmc_tot, jnp.zeros((1, 4 * d), F32)], axis=1)
    g_wmod, g_bmod, g_cctx = _mod_grads(
        small_slots[:, o_silu:o_silu + d], lax.dynamic_slice_in_dim(small_slots[:, :6 * d], me * ncol, ncol, axis=1),
        small_slots[:, :6 * d], dmc_tot, lax.dynamic_slice_in_dim(dmc_pad, me * ncol, ncol, axis=1), silu_v[1:2], c_ctx[None],
        w_mod_f[:, :2 * d])

    big_names = (m_w_in, m_w_branch_mlstm, m_w_branch_sgu, m_w_out, m_w_up, m_w_down, m_conv_qk, m_w_ffn_conv)
    big_v = (v_w_in, v_w_branch_mlstm, v_w_branch_sgu, v_w_out, v_w_up, v_w_down, v_conv_qk, v_w_ffn_conv)
    big_all = big_w + [conv_qk[0], w_ffn_conv[0]]
    big_shapes = [a.shape for a in big_all]
    big_out = _adamw(_pack(big_all, 1024), recv_big, _pack([a[0] for a in big_names], 1024), _pack([a[0] for a in big_v], 1024),
                     "adamw_shards")
    g_big, d_big, m_big, v_big = (_unpack(b, big_shapes) for b in big_out)
    rmod = w_mod[0].size // LANES
    mod_out = _adamw(w_mod[0].reshape(rmod, LANES), g_wmod.reshape(1, rmod, LANES), m_w_mod[0].reshape(rmod, LANES),
                     v_w_mod[0].reshape(rmod, LANES), "adamw_w_mod")
    g_mod, d_mod, m_mod, v_mod = (b.reshape(w_mod.shape) for b in mod_out)

    def rep(cc, bm, n1, bg, hg, lg, lb, ws, bs, n2, fg):
        return [cc.reshape(-1), bm.reshape(-1), n1.reshape(-1), _pad_lanes(bg.reshape(1, -1)).reshape(-1), hg.reshape(-1),
                lg.reshape(-1), lb.reshape(-1), ws.reshape(-1), bs.reshape(-1), n2.reshape(-1), fg.reshape(-1)]

    o = o_n1
    g_rep_parts = [g_cctx, g_bmod]
    for n in (d, LANES, d, d, d, ng * sc * sc, ng * sc, d, d):
        g_rep_parts.append(small_sum[o:o + n])
        o += n
    rep_shapes = [(d,), (1, N_MOD * d), (1, d), (1, LANES), (1, d), (1, d), (1, d), (1, ng, sc, sc), (1, ng, sc), (1, d), (d,)]
    rep_out = _adamw(
        _pack(rep(c_ctx, b_mod, norm1_g, b_gate, head_norm_g, sgu_ln_g, sgu_ln_b, w_s, b_s, norm2_g, final_g), 8),
        _pack(g_rep_parts, 8)[None],
        _pack(rep(m_c_ctx, m_b_mod, m_norm1_g, m_b_gate, m_head_norm_g, m_sgu_ln_g, m_sgu_ln_b, m_w_s, m_b_s, m_norm2_g, m_final_g), 8),
        _pack(rep(v_c_ctx, v_b_mod, v_norm1_g, v_b_gate, v_head_norm_g, v_sgu_ln_g, v_sgu_ln_b, v_w_s, v_b_s, v_norm2_g, v_final_g), 8),
        "adamw_replicated")

    def assemble(big, modv, repbuf):
        r = _unpack(repbuf, rep_shapes)
        b = [a[None] for a in big]
        return [r[0], modv, r[1], r[2], b[0], r[3][:, :4 * nh], b[6], r[4], r[5], r[6], r[7], r[8], b[1], b[2], b[3], r[9], b[4],
                b[7], b[5], r[10]]

    loss = lax.psum(st_tail[2, 0], ("x", "y", "c"))
    outs = [loss, grad_x[None]]
    for k in range(4):
        outs += assemble((g_big, d_big, m_big, v_big)[k], (g_mod, d_mod, m_mod, v_mod)[k], rep_out[k])
    return tuple(outs)
```

```python
import functools
import math

import jax
import jax.numpy as jnp
from jax import lax
from jax.experimental import pallas as pl
from jax.experimental.pallas import tpu as pltpu

F32, BF16 = jnp.float32, jnp.bfloat16
EPS = 1e-6
M_INIT = -1e30
NEG = -1e30
GRID_W = 64
LCH = 256
N_MOD = 6
N_DEV = 8
LANES = 128
ADAM_LR, ADAM_B1, ADAM_B2, ADAM_EPS, ADAM_WD, ADAM_STEP = 0.001, 0.9, 0.999, 1e-08, 0.01, 10
GELU_C = math.sqrt(2.0 / math.pi)
GELU_A = 0.044715
VMEM_LIMIT = 56 * 1024 * 1024
HI = lax.Precision.HIGHEST
SDS = jax.ShapeDtypeStruct
MESH_ID = pl.DeviceIdType.MESH


def _pick(n, cands):
    for c in cands:
        if n % c == 0:
            return c
    raise ValueError(f"no block size for {n} in {cands}")


def _cp(*sem):
    return pltpu.CompilerParams(dimension_semantics=sem if sem else None, vmem_limit_bytes=VMEM_LIMIT)


def _sigmoid(x):
    return 1.0 / (1.0 + jnp.exp(-x))


def _gelu(x):
    return 0.5 * x * (1.0 + jnp.tanh(GELU_C * (x + GELU_A * x * x * x)))


def _gelu_grad(x):
    t = jnp.tanh(GELU_C * (x + GELU_A * x * x * x))
    return 0.5 * (1.0 + t) + 0.5 * x * (1.0 - t * t) * GELU_C * (1.0 + 3.0 * GELU_A * x * x)


def _log_sigmoid(x):
    return jnp.minimum(x, 0.0) - jnp.log(1.0 + jnp.exp(-jnp.abs(x)))


def _dot(a, b):
    return jnp.dot(a, b, preferred_element_type=F32)


def _dot_nt(a, b):
    return lax.dot_general(a, b, (((1,), (1,)), ((), ())), preferred_element_type=F32)


def _dot_tn(a, b):
    return lax.dot_general(a, b, (((0,), (0,)), ((), ())), preferred_element_type=F32)


def _bf(x):
    return x.astype(BF16)


def _allgather(arrs):
    na = len(arrs)

    def body(*refs):
        x_refs, o_refs = refs[:na], refs[na:2 * na]
        send_sems, recv_sems, local_sems = refs[2 * na:]
        x, y, c = lax.axis_index("x"), lax.axis_index("y"), lax.axis_index("c")
        me, sibling = (x, y, c), (x, y, 1 - c)
        chips = [(1 - x, y), (x, 1 - y), (1 - x, 1 - y)]

        def copy(a, k, block, to, src=None):
            slot = o_refs[a].at[4 * block[0] + 2 * block[1] + block[2]]
            return pltpu.make_async_remote_copy(
                src_ref=slot if src is None else src, dst_ref=slot, send_sem=send_sems.at[7 * a + k],
                recv_sem=recv_sems.at[7 * a + k], device_id=to, device_id_type=MESH_ID)

        mine = [pltpu.make_async_copy(x_refs[a], o_refs[a].at[4 * x + 2 * y + c], local_sems.at[a]) for a in range(na)]
        for cp in mine:
            cp.start()
        first = []
        for a in range(na):
            first.append(copy(a, 0, me, sibling, src=x_refs[a]))
            first += [copy(a, 1 + j, me, (*chip, c), src=x_refs[a]) for j, chip in enumerate(chips)]
        for cp in first:
            cp.start()
        passed = []
        for j, chip in enumerate(chips):
            for a in range(na):
                copy(a, 1 + j, (*chip, c), me).wait_recv()
                passed.append(copy(a, 4 + j, (*chip, c), sibling))
                passed[-1].start()
        for a in range(na):
            copy(a, 0, sibling, me).wait_recv()
            for j, chip in enumerate(chips):
                copy(a, 4 + j, (*chip, 1 - c), me).wait_recv()
        for cp in first + passed:
            cp.wait_send()
        for cp in mine:
            cp.wait()

    anyspec = pl.BlockSpec(memory_space=pl.ANY)
    return pl.pallas_call(
        body, name="weights_allgather",
        out_shape=[SDS((N_DEV,) + a.shape, a.dtype) for a in arrs],
        in_specs=[anyspec] * na, out_specs=[anyspec] * na,
        scratch_shapes=[pltpu.SemaphoreType.DMA((7 * na,)), pltpu.SemaphoreType.DMA((7 * na,)), pltpu.SemaphoreType.DMA((na,))],
    )(*arrs)


def _grad_exchange(per_dest, shared):
    nd, ns = len(per_dest), len(shared)
    na = nd + ns

    def body(*refs):
        in_refs, out_refs = refs[:na], refs[na:2 * na]
        send_sems, recv_sems, local_sems = refs[2 * na:]
        x, y, c = lax.axis_index("x"), lax.axis_index("y"), lax.axis_index("c")
        me = 4 * x + 2 * y + c

        def src(a, idx):
            return in_refs[a].at[idx] if a < nd else in_refs[a]

        loc = [pltpu.make_async_copy(src(a, me), out_refs[a].at[me], local_sems.at[a]) for a in range(na)]
        for cp in loc:
            cp.start()
        sends, recvs = [], []
        for k in range(1, N_DEV):
            px = 1 - x if k & 4 else x
            py = 1 - y if k & 2 else y
            pc = 1 - c if k & 1 else c
            peer, pidx = (px, py, pc), 4 * px + 2 * py + pc
            for a in range(na):
                sem = 7 * a + k - 1
                sends.append(pltpu.make_async_remote_copy(
                    src_ref=src(a, pidx), dst_ref=out_refs[a].at[me], send_sem=send_sems.at[sem],
                    recv_sem=recv_sems.at[sem], device_id=peer, device_id_type=MESH_ID))
                recvs.append(pltpu.make_async_remote_copy(
                    src_ref=src(a, pidx), dst_ref=out_refs[a].at[pidx], send_sem=send_sems.at[sem],
                    recv_sem=recv_sems.at[sem], device_id=peer, device_id_type=MESH_ID))
        for cp in sends:
            cp.start()
        for cp in recvs:
            cp.wait_recv()
        for cp in sends:
            cp.wait_send()
        for cp in loc:
            cp.wait()

    anyspec = pl.BlockSpec(memory_space=pl.ANY)
    return pl.pallas_call(
        body, name="grad_exchange",
        out_shape=[SDS(a.shape, a.dtype) for a in per_dest] + [SDS((N_DEV,) + a.shape, a.dtype) for a in shared],
        in_specs=[anyspec] * na, out_specs=[anyspec] * na,
        scratch_shapes=[pltpu.SemaphoreType.DMA((7 * na,)), pltpu.SemaphoreType.DMA((7 * na,)), pltpu.SemaphoreType.DMA((na,))],
    )(*per_dest, *shared)


def _col_pieces(n, segments):
    out = []
    for j in range(N_DEV):
        lo, hi = j * n, (j + 1) * n
        for (k, s0, w, c0) in segments:
            a, b = max(lo, s0), min(hi, s0 + w)
            if a < b:
                out.append((j, a - lo, b - lo, k, c0 + a - s0, c0 + b - s0))
    return out


def _assemble_cols(slots, groups, out_widths, name):
    r, p = slots.shape[1], slots.shape[2]
    tb = _pick(r, (128, 64, 32, 16, 8))
    covered = [0] * len(out_widths)
    for (_, n, segs) in groups:
        for (k, _, w, _) in segs:
            covered[k] += w

    def body(s_ref, *o_refs):
        for k, wd in enumerate(out_widths):
            if covered[k] < wd:
                o_refs[k][...] = jnp.zeros_like(o_refs[k])
        for (off, n, segs) in groups:
            for (j, a0, a1, k, d0, d1) in _col_pieces(n, segs):
                o_refs[k][:, d0:d1] = s_ref[j, :, off + a0:off + a1]

    return pl.pallas_call(
        body, name=name, grid=(r // tb,), in_specs=[pl.BlockSpec((N_DEV, tb, p), lambda i: (0, i, 0))],
        out_specs=[pl.BlockSpec((tb, w), lambda i: (i, 0)) for w in out_widths],
        out_shape=[SDS((r, w), slots.dtype) for w in out_widths], compiler_params=_cp("arbitrary"))(slots)


def _scatter_cols(pieces, segments, n, name):
    r = pieces[0].shape[0]
    tb = _pick(r, (128, 64, 32, 16, 8))

    def body(*refs):
        p_refs, o_ref = refs[:-1], refs[-1]
        for (j, a0, a1, k, d0, d1) in _col_pieces(n, segments):
            o_ref[j, :, a0:a1] = p_refs[k][:, d0:d1]

    return pl.pallas_call(
        body, name=name, grid=(r // tb,), in_specs=[pl.BlockSpec((tb, a.shape[1]), lambda i: (i, 0)) for a in pieces],
        out_specs=pl.BlockSpec((N_DEV, tb, n), lambda i: (0, i, 0)), out_shape=SDS((N_DEV, r, n), pieces[0].dtype),
        compiler_params=_cp("arbitrary"))(*pieces)


def _modulation(cvec, w_mod, b_mod):
    d, n = w_mod.shape

    def body(c_ref, w_ref, b_ref, s_ref, o_ref):
        cv = c_ref[...]
        s = cv * _sigmoid(cv)
        s_ref[...] = s
        o_ref[...] = _dot(_bf(s), w_ref[...]) + b_ref[...]

    return pl.pallas_call(body, name="modulation", out_shape=(SDS((8, d), F32), SDS((8, n), F32)),
                          compiler_params=_cp())(cvec, w_mod, b_mod)


def _norm_mod_proj(xs, g, shsc, w_main, w_gate, t_rows, name):
    s_rows, d = xs.shape
    n = w_main.shape[1]
    tb = _pick(s_rows, (1280, 1024, 256))
    cb = _pick(n, (1408, 1024, 768, 512, 384, 256, 128))
    gate = w_gate is not None

    def body(*refs):
        if gate:
            x_ref, g_ref, ss_ref, wm_ref, wg_ref, hn_ref, z_ref, zg_ref, hn_sc = refs
        else:
            x_ref, g_ref, ss_ref, wm_ref, hn_ref, z_ref, hn_sc = refs
        i, j = pl.program_id(0), pl.program_id(1)

        @pl.when(j == 0)
        def _():
            x = x_ref[...]
            r = lax.rsqrt(jnp.mean(x * x, axis=-1, keepdims=True) + EPS)
            row = i * tb + lax.broadcasted_iota(jnp.int32, (tb, 1), 0)
            isx = row < t_rows
            sh = jnp.where(isx, ss_ref[0:1, :], ss_ref[2:3, :])
            sc = jnp.where(isx, ss_ref[1:2, :], ss_ref[3:4, :])
            hb = _bf((x * r * g_ref[...]) * (1.0 + sc) + sh)
            hn_sc[...] = hb
            hn_ref[...] = hb
            if gate:
                zg_ref[...] = _dot(hb, wg_ref[...])

        z_ref[...] = _bf(_dot(hn_sc[...], wm_ref[...]))

    in_specs = [pl.BlockSpec((tb, d), lambda i, j: (i, 0)), pl.BlockSpec((1, d), lambda i, j: (0, 0)),
                pl.BlockSpec((4, d), lambda i, j: (0, 0)), pl.BlockSpec((d, cb), lambda i, j: (0, j))]
    out_specs = [pl.BlockSpec((tb, d), lambda i, j: (i, 0)), pl.BlockSpec((tb, cb), lambda i, j: (i, j))]
    out_shape = [SDS((s_rows, d), BF16), SDS((s_rows, n), BF16)]
    args = [xs, g, shsc, w_main]
    if gate:
        in_specs.append(pl.BlockSpec((d, LANES), lambda i, j: (0, 0)))
        out_specs.append(pl.BlockSpec((tb, LANES), lambda i, j: (i, 0)))
        out_shape.append(SDS((s_rows, LANES), F32))
        args.append(w_gate)
    return pl.pallas_call(
        body, name=name, grid=(s_rows // tb, n // cb), in_specs=in_specs, out_specs=out_specs, out_shape=out_shape,
        scratch_shapes=[pltpu.VMEM((tb, d), BF16)], compiler_params=_cp("arbitrary", "arbitrary"))(*args)


def _seg_masks(row, t_rows, s_rows):
    prev_ok = (row != 0) & (row != t_rows)
    next_ok = (row != t_rows - 1) & (row != s_rows - 1)
    return prev_ok, next_ok


def _shift_rows(z, halo_prev, halo_next, tb):
    loc = lax.broadcasted_iota(jnp.int32, (tb, 1), 0)
    zp = jnp.where(loc == 0, halo_prev, pltpu.roll(z, 1, 0))
    zn = jnp.where(loc == tb - 1, halo_next, pltpu.roll(z, tb - 1, 0))
    return zp, zn


def _qk_conv(z_main, conv_w, t_rows, md, qscale):
    s_rows = z_main.shape[0]
    tb = _pick(s_rows, (1280, 1024, 256))
    cb = _pick(md, (512, 256, 128))
    nb8 = tb // 8

    def body(zm, zp, zn, w_ref, o_ref):
        i, j = pl.program_id(0), pl.program_id(1)
        z = zm[...].astype(F32)
        zprev, znext = _shift_rows(z, zp[7:8, :].astype(F32), zn[0:1, :].astype(F32), tb)
        row = i * tb + lax.broadcasted_iota(jnp.int32, (tb, 1), 0)
        prev_ok, next_ok = _seg_masks(row, t_rows, s_rows)
        pre = (w_ref[0:1, :] * jnp.where(prev_ok, zprev, 0.0) + w_ref[1:2, :] * z
               + w_ref[2:3, :] * jnp.where(next_ok, znext, 0.0))
        scale = jnp.where(j * cb < md, qscale, 1.0)
        o_ref[...] = _bf(pre * _sigmoid(pre) * scale)

    return pl.pallas_call(
        body, name="qk_conv", grid=(s_rows // tb, 2 * md // cb),
        in_specs=[pl.BlockSpec((tb, cb), lambda i, j: (i, j)),
                  pl.BlockSpec((8, cb), lambda i, j: (jnp.maximum(i * nb8 - 1, 0), j)),
                  pl.BlockSpec((8, cb), lambda i, j: (jnp.minimum((i + 1) * nb8, s_rows // 8 - 1), j)),
                  pl.BlockSpec((8, cb), lambda i, j: (0, j))],
        out_specs=pl.BlockSpec((tb, cb), lambda i, j: (i, j)),
        out_shape=SDS((s_rows, 2 * md), BF16), compiler_params=_cp("arbitrary", "arbitrary"))(z_main, z_main, z_main, conv_w)


def _chunk_gates(gates, bias, rev):
    ln = gates.shape[0]
    gz = gates + bias
    logf = _log_sigmoid(gz)
    r_id = lax.broadcasted_iota(jnp.int32, (ln, ln), 0)
    c_id = lax.broadcasted_iota(jnp.int32, (ln, ln), 1)
    mask = (c_id >= r_id) if rev else (c_id <= r_id)
    mf = mask.astype(F32)
    b_all = jnp.dot(mf, logf, preferred_element_type=F32, precision=HI)
    g_all = jnp.sum(logf, axis=0, keepdims=True)
    return gz, b_all, b_all.T, gz.T, g_all, mask, mf


def _head_weights(b_col, b_row, i_row, m_in, mask):
    d = jnp.where(mask, b_col - b_row + i_row, NEG)
    inter = b_col + m_in
    m_row = jnp.maximum(inter, jnp.max(d, axis=1, keepdims=True))
    return jnp.exp(d - m_row), jnp.exp(inter - m_row), m_row


def _head_state_coeffs(g, b_col, i_col, m_in):
    a = g - b_col + i_col
    m_new = jnp.maximum(g + m_in, jnp.max(a, axis=0, keepdims=True))
    return jnp.exp(g + m_in - m_new), jnp.exp(a - m_new), m_new


def _mlstm_fwd(qk, z_main, zg, bias, nh):
    s_rows = qk.shape[0]
    md = qk.shape[1] // 2
    dh = md // nh
    nc = s_rows // LCH
    ln = LCH

    def chunk_f(i):
        return jnp.where(i == 0, nc - 1, i - 1)

    def chunk_b(i):
        return jnp.where(i == 0, nc - 1, nc - 1 - i)

    def body(qf, kf, vf, gf, qb, kb, vb, gb, bias_ref, hf_ref, hb_ref, cf_ref, nf_ref, mf_ref, cb_ref, nb_ref, mb_ref,
             c_sc, n_sc, m_sc):
        i = pl.program_id(0)

        @pl.when(i == 0)
        def _():
            c_sc[...] = jnp.zeros_like(c_sc)
            n_sc[...] = jnp.zeros_like(n_sc)
            m_sc[...] = jnp.full(m_sc.shape, M_INIT, F32)

        for dr, (q_ref, k_ref, v_ref, g_ref, h_ref, c_out, n_out, m_out) in enumerate(
                ((qf, kf, vf, gf, hf_ref, cf_ref, nf_ref, mf_ref), (qb, kb, vb, gb, hb_ref, cb_ref, nb_ref, mb_ref))):
            gz, b_all, b_t, g_t, g_all, mask, _ = _chunk_gates(g_ref[...], bias_ref[...], dr == 1)
            for h in range(nh):
                ci, cf = 2 * dr * nh + h, (2 * dr + 1) * nh + h
                sl = slice(h * dh, (h + 1) * dh)
                q, k, v = q_ref[:, sl], k_ref[:, sl], v_ref[:, sl]
                c_in, n_in, m_in = c_sc[dr, h], n_sc[dr, h, 0:1, :], m_sc[dr, h, 0:1, 0:1]
                c_out[sl, :] = c_in
                n_out[:, sl] = n_sc[dr, h]
                m_out[h] = m_sc[dr, h]
                b_col, b_row, i_col, i_row = b_all[:, cf:cf + 1], b_t[cf:cf + 1, :], gz[:, ci:ci + 1], g_t[ci:ci + 1, :]
                g = g_all[:, cf:cf + 1]
                w, w_int, m_row = _head_weights(b_col, b_row, i_row, m_in, mask)
                s_mat = _dot_nt(q, k) * w
                num = _dot(_bf(s_mat), v) + w_int * _dot(q, _bf(c_in))
                den = jnp.sum(s_mat, axis=1, keepdims=True) + w_int * jnp.sum(q.astype(F32) * n_in, axis=1, keepdims=True)
                h_ref[:, sl] = _bf(num / jnp.maximum(jnp.abs(den), jnp.exp(-m_row)))
                a_old, coef, m_new = _head_state_coeffs(g, b_col, i_col, m_in)
                kw = k.astype(F32) * coef
                c_sc[dr, h] = a_old * c_in + _dot_tn(_bf(kw), v)
                n_sc[dr, h] = jnp.broadcast_to(a_old * n_in + jnp.sum(kw, axis=0, keepdims=True), (8, dh))
                m_sc[dr, h] = jnp.broadcast_to(m_new, (8, LANES))

    def tok(cfn, col):
        return pl.BlockSpec((ln, md), lambda i: (cfn(i), col))

    def gat(cfn):
        return pl.BlockSpec((ln, LANES), lambda i: (cfn(i), 0))

    def st(cfn, shape):
        return pl.BlockSpec((None,) + shape, lambda i: (cfn(i),) + (0,) * len(shape))

    st_shapes = ((nh * dh, dh), (8, md), (nh, 8, LANES))
    return pl.pallas_call(
        body, name="mlstm_fwd", grid=(nc,),
        in_specs=[tok(chunk_f, 0), tok(chunk_f, 1), tok(chunk_f, 2), gat(chunk_f),
                  tok(chunk_b, 0), tok(chunk_b, 1), tok(chunk_b, 2), gat(chunk_b),
                  pl.BlockSpec((1, LANES), lambda i: (0, 0))],
        out_specs=[tok(chunk_f, 0), tok(chunk_b, 0)] + [st(chunk_f, s) for s in st_shapes] + [st(chunk_b, s) for s in st_shapes],
        out_shape=[SDS((s_rows, md), BF16)] * 2 + [SDS((nc,) + s, F32) for s in st_shapes] * 2,
        scratch_shapes=[pltpu.VMEM((2, nh, dh, dh), F32), pltpu.VMEM((2, nh, 8, dh), F32), pltpu.VMEM((2, nh, 8, LANES), F32)],
        compiler_params=_cp("arbitrary"))(qk, qk, z_main, zg, qk, qk, z_main, zg, bias)


def _head_rms(hs, nh, dh):
    parts, scales = [], []
    for h in range(nh):
        hh = hs[:, h * dh:(h + 1) * dh]
        r = lax.rsqrt(jnp.mean(hh * hh, axis=-1, keepdims=True) + EPS)
        parts.append(hh * r)
        scales.append(r)
    return jnp.concatenate(parts, axis=1), scales


def _layer_norm(v):
    vc = v - jnp.mean(v, axis=-1, keepdims=True)
    r = lax.rsqrt(jnp.mean(vc * vc, axis=-1, keepdims=True) + EPS)
    return vc * r, r


def _sgu_mix(vnb, ws_ref, bs_ref, tb, ng, gd, sc):
    rows = []
    for ch in range(tb // sc):
        cols = []
        for g in range(ng):
            blk = vnb[ch * sc:(ch + 1) * sc, g * gd:(g + 1) * gd]
            cols.append(_dot(_bf(ws_ref[g]), blk) + bs_ref[:, g:g + 1])
        rows.append(jnp.concatenate(cols, axis=1))
    return jnp.concatenate(rows, axis=0)


def _mixer_fwd(hf, hb, z_main, xs, hg, lng, lnb, w_s, b_st, wbm, wbs, wout, mx2, t_rows, nh):
    d = xs.shape[1]
    ng, sc = w_s.shape[0], w_s.shape[1]
    dh, gd = d // nh, d // ng
    tb = _pick(t_rows, (256,))

    def body(hf_ref, hb_ref, zo, zu, zv, zgm, zgg, x_ref, hg_ref, lng_ref, lnb_ref, ws_ref, bs_ref, wbm_ref, wbs_ref,
             wo_ref, mx2_ref, h1_ref, ym_ref, ys_ref, pm_ref, ps_ref, y_ref, out_ref):
        hs = hf_ref[...].astype(F32) + hb_ref[...].astype(F32)
        hn, _ = _head_rms(hs, nh, dh)
        ym = _bf(_sigmoid(zo[...].astype(F32)) * (hn * hg_ref[...]))
        ym_ref[...] = ym
        vhat, _ = _layer_norm(_gelu(zv[...].astype(F32)))
        vnb = _bf(vhat * lng_ref[...] + lnb_ref[...])
        ys = _bf(_gelu(zu[...].astype(F32)) * _sgu_mix(vnb, ws_ref, bs_ref, tb, ng, gd, sc))
        ys_ref[...] = ys
        pm = _dot(ym, wbm_ref[...])
        ps = _dot(ys, wbs_ref[...])
        pm_ref[...] = _bf(pm)
        ps_ref[...] = _bf(ps)
        y = _bf(_sigmoid(zgm[...].astype(F32)) * pm + _sigmoid(zgg[...].astype(F32)) * ps)
        y_ref[...] = y
        out = _dot(y, wo_ref[...])
        out_ref[...] = _bf(out)
        h1_ref[...] = x_ref[...] + mx2_ref[...] * out

    def tok(col):
        return pl.BlockSpec((tb, d), lambda i: (i, col))

    def full(shape):
        return pl.BlockSpec(shape, lambda i: (0,) * len(shape))

    return pl.pallas_call(
        body, name="mixer_fwd", grid=(t_rows // tb,),
        in_specs=[tok(0), tok(0), tok(3), tok(4), tok(5), tok(6), tok(7), tok(0), full((1, d)), full((1, d)), full((1, d)),
                  full((ng, sc, sc)), full((sc, LANES)), full((d, d)), full((d, d)), full((d, d)), full((1, d))],
        out_specs=[tok(0)] * 7,
        out_shape=[SDS((t_rows, d), F32)] + [SDS((t_rows, d), BF16)] * 6,
        compiler_params=_cp("arbitrary"))(hf, hb, z_main, z_main, z_main, z_main, z_main, xs, hg, lng, lnb, w_s, b_st,
                                          wbm, wbs, wout, mx2)


def _grid_taps(a_ext, n_ext):
    col = lax.broadcasted_iota(jnp.int32, (n_ext, 1), 0) % GRID_W
    left = jnp.where(col != 0, pltpu.roll(a_ext, 1, 0), 0.0)
    right = jnp.where(col != GRID_W - 1, pltpu.roll(a_ext, n_ext - 1, 0), 0.0)
    return left, right


def _with_halo(prev, main, nxt, i, ni, tb):
    ext = jnp.concatenate([prev, main, nxt], axis=0).astype(F32)
    pos = lax.broadcasted_iota(jnp.int32, (tb + 2 * GRID_W, 1), 0)
    inside = ((pos >= GRID_W) | (i > 0)) & ((pos < tb + GRID_W) | (i < ni - 1))
    return jnp.where(inside, ext, 0.0)


def _halo_specs(tb, cb, t_rows, col0=0):
    nh64 = tb // GRID_W
    return [pl.BlockSpec((tb, cb), lambda i, j: (i, col0 + j)),
            pl.BlockSpec((GRID_W, cb), lambda i, j: (jnp.maximum(i * nh64 - 1, 0), col0 + j)),
            pl.BlockSpec((GRID_W, cb), lambda i, j: (jnp.minimum((i + 1) * nh64, t_rows // GRID_W - 1), col0 + j))]


def _ffn_tail(ab, w_conv9, w_down, h1, mx5, gfin, target, dff):
    t_rows, d = h1.shape
    tb = _pick(t_rows, (512,))
    cb = _pick(dff, (256, 128))
    ni, nj = t_rows // tb, dff // cb
    n_ext = tb + 2 * GRID_W

    def body(am, ap, an, b_ref, wc_ref, wd_ref, h1_ref, mx5_ref, gf_ref, tg_ref, ac_ref, f_ref, dh2_ref, dffn_ref, st_ref, acc):
        i, j = pl.program_id(0), pl.program_id(1)
        a_ext = _with_halo(ap[...], am[...], an[...], i, ni, tb)
        left, right = _grid_taps(a_ext, n_ext)
        conv = jnp.zeros((tb, cb), F32)
        for di in range(3):
            o = di * GRID_W
            conv = conv + (wc_ref[3 * di:3 * di + 1, :] * left[o:o + tb] + wc_ref[3 * di + 1:3 * di + 2, :] * a_ext[o:o + tb]
                           + wc_ref[3 * di + 2:3 * di + 3, :] * right[o:o + tb])
        ac_ref[...] = _bf(conv)
        fb = _bf(conv * _sigmoid(conv) * b_ref[...].astype(F32))
        f_ref[...] = fb

        @pl.when(j == 0)
        def _():
            acc[...] = jnp.zeros_like(acc)

        @pl.when((i == 0) & (j == 0))
        def _():
            st_ref[...] = jnp.zeros_like(st_ref)

        acc[...] += _dot(fb, wd_ref[...])

        @pl.when(j == nj - 1)
        def _():
            ffn = acc[...]
            h2 = h1_ref[...] + mx5_ref[...] * ffn
            r = lax.rsqrt(jnp.mean(h2 * h2, axis=-1, keepdims=True) + EPS)
            xn = h2 * r
            e = xn * gf_ref[...] - tg_ref[...]
            loss = 0.5 * jnp.sum(jnp.sum(e * e, axis=1, keepdims=True), axis=0, keepdims=True) / d
            dy = e * (1.0 / d)
            dxn = dy * gf_ref[...]
            dh2 = r * (dxn - xn * jnp.mean(dxn * xn, axis=-1, keepdims=True))
            dh2_ref[...] = dh2
            dffn_ref[...] = _bf(dh2 * mx5_ref[...])
            st_ref[...] += jnp.concatenate(
                [jnp.sum(dy * xn, axis=0, keepdims=True), jnp.sum(dh2 * ffn, axis=0, keepdims=True),
                 jnp.broadcast_to(loss, (1, d)), jnp.zeros((5, d), F32)], axis=0)

    def tokd():
        return pl.BlockSpec((tb, d), lambda i, j: (i, 0))

    def rowd():
        return pl.BlockSpec((1, d), lambda i, j: (0, 0))

    return pl.pallas_call(
        body, name="ffn_tail", grid=(ni, nj),
        in_specs=_halo_specs(tb, cb, t_rows) + [pl.BlockSpec((tb, cb), lambda i, j: (i, nj + j)),
                                                pl.BlockSpec((16, cb), lambda i, j: (0, j)),
                                                pl.BlockSpec((cb, d), lambda i, j: (j, 0)), tokd(), rowd(), rowd(), tokd()],
        out_specs=[pl.BlockSpec((tb, cb), lambda i, j: (i, j)), pl.BlockSpec((tb, cb), lambda i, j: (i, j)), tokd(), tokd(),
                   pl.BlockSpec((8, d), lambda i, j: (0, 0))],
        out_shape=[SDS((t_rows, dff), BF16), SDS((t_rows, dff), BF16), SDS((t_rows, d), F32), SDS((t_rows, d), BF16),
                   SDS((8, d), F32)],
        scratch_shapes=[pltpu.VMEM((tb, d), F32)],
        compiler_params=_cp("arbitrary", "arbitrary"))(ab, ab, ab, ab, w_conv9, w_down, h1, mx5, gfin, target)


def _ffn_bwd_gate(dffn, w_down, aconv, ab, dff):
    t_rows, d = dffn.shape
    tb = _pick(t_rows, (512,))
    cb = _pick(dff, (256, 128))
    nj = dff // cb

    def body(g_ref, wd_ref, ac_ref, b_ref, db_ref, dac_ref):
        df = _dot_nt(g_ref[...], wd_ref[...])
        ac = ac_ref[...].astype(F32)
        sa = _sigmoid(ac)
        db_ref[...] = _bf(df * ac * sa)
        dac_ref[...] = _bf(df * b_ref[...].astype(F32) * (sa * (1.0 + ac * (1.0 - sa))))

    blk = pl.BlockSpec((tb, cb), lambda i, j: (i, j))
    return pl.pallas_call(
        body, name="ffn_bwd_gate", grid=(t_rows // tb, nj),
        in_specs=[pl.BlockSpec((tb, d), lambda i, j: (i, 0)), pl.BlockSpec((cb, d), lambda i, j: (j, 0)), blk,
                  pl.BlockSpec((tb, cb), lambda i, j: (i, nj + j))],
        out_specs=[blk, blk], out_shape=[SDS((t_rows, dff), BF16)] * 2,
        compiler_params=_cp("arbitrary", "arbitrary"))(dffn, w_down, aconv, ab)


def _ffn_conv_bwd(dac, ab, w_conv9, dff):
    t_rows = dac.shape[0]
    tb = _pick(t_rows, (512,))
    cb = _pick(dff, (256, 128))
    ni, nj = t_rows // tb, dff // cb
    n_ext = tb + 2 * GRID_W
    nh64 = tb // GRID_W

    def body(dm, dp, dn, am, ap, an, wc_ref, da_ref, gw_ref):
        i = pl.program_id(1)
        d_ext = _with_halo(dp[...], dm[...], dn[...], i, ni, tb)
        a_ext = _with_halo(ap[...], am[...], an[...], i, ni, tb)
        d_left, d_right = _grid_taps(d_ext, n_ext)
        a_left, a_right = _grid_taps(a_ext, n_ext)
        dmain = d_ext[GRID_W:GRID_W + tb]
        da = jnp.zeros((tb, cb), F32)
        rows = []
        for di in range(3):
            o = (2 - di) * GRID_W
            da = da + (wc_ref[3 * di:3 * di + 1, :] * d_right[o:o + tb] + wc_ref[3 * di + 1:3 * di + 2, :] * d_ext[o:o + tb]
                       + wc_ref[3 * di + 2:3 * di + 3, :] * d_left[o:o + tb])
            o = di * GRID_W
            for tap in (a_left, a_ext, a_right):
                rows.append(jnp.sum(dmain * tap[o:o + tb], axis=0, keepdims=True))
        da_ref[...] = _bf(da)

        @pl.when(i == 0)
        def _():
            gw_ref[...] = jnp.zeros_like(gw_ref)

        gw_ref[...] += jnp.concatenate(rows + [jnp.zeros((7, cb), F32)], axis=0)

    def halo(col0):
        return [pl.BlockSpec((tb, cb), lambda j, i: (i, col0 + j)),
                pl.BlockSpec((GRID_W, cb), lambda j, i: (jnp.maximum(i * nh64 - 1, 0), col0 + j)),
                pl.BlockSpec((GRID_W, cb), lambda j, i: (jnp.minimum((i + 1) * nh64, t_rows // GRID_W - 1), col0 + j))]

    return pl.pallas_call(
        body, name="ffn_conv_bwd", grid=(nj, ni),
        in_specs=halo(0) + halo(0) + [pl.BlockSpec((16, cb), lambda j, i: (0, j))],
        out_specs=[pl.BlockSpec((tb, cb), lambda j, i: (i, j)), pl.BlockSpec((16, cb), lambda j, i: (0, j))],
        out_shape=[SDS((t_rows, dff), BF16), SDS((16, dff), F32)],
        compiler_params=_cp("arbitrary", "arbitrary"))(dac, dac, dac, ab, ab, ab, w_conv9)


def _proj_norm_bwd(pairs, x_arr, x_row0, g, scale, resid, m_rows, name):
    d = x_arr.shape[1]
    tm = _pick(m_rows, (512, 256))
    ni = m_rows // tm
    starts, total = [], 0
    for (_, _, _, _, k_p, tk_p) in pairs:
        starts.append(total)
        total += k_p // tk_p
    npairs = len(pairs)
    has_dx = resid is not None

    def body(*refs):
        a_refs, b_refs = refs[0:2 * npairs:2], refs[1:2 * npairs:2]
        rest = refs[2 * npairs:]
        if has_dx:
            x_ref, g_ref, sc_ref, r_ref, dx_ref, st_ref, acc = rest
        else:
            x_ref, g_ref, sc_ref, st_ref, acc = rest
        i, k = pl.program_id(0), pl.program_id(1)

        @pl.when(k == 0)
        def _():
            acc[...] = jnp.zeros_like(acc)

        @pl.when((i == 0) & (k == 0))
        def _():
            st_ref[...] = jnp.zeros_like(st_ref)

        for p in range(npairs):
            nk = pairs[p][4] // pairs[p][5]

            @pl.when((k >= starts[p]) & (k < starts[p] + nk))
            def _(p=p):
                acc[...] += _dot_nt(a_refs[p][...], b_refs[p][...])

        @pl.when(k == total - 1)
        def _():
            dhn = acc[...]
            x = x_ref[...]
            r = lax.rsqrt(jnp.mean(x * x, axis=-1, keepdims=True) + EPS)
            xn = x * r
            dmod = dhn * (1.0 + sc_ref[...])
            dxn = dmod * g_ref[...]
            if has_dx:
                dx_ref[...] = r * (dxn - xn * jnp.mean(dxn * xn, axis=-1, keepdims=True)) + r_ref[...]
            st_ref[...] += jnp.concatenate(
                [jnp.sum(dmod * xn, axis=0, keepdims=True), jnp.sum(dhn, axis=0, keepdims=True),
                 jnp.sum(dhn * (xn * g_ref[...]), axis=0, keepdims=True), jnp.zeros((5, d), F32)], axis=0)

    in_specs, args = [], []
    for p, (a, a_row0, b, b_col0, k_p, tk_p) in enumerate(pairs):
        nk, s0, ar, bc = k_p // tk_p, starts[p], a_row0 // tm, b_col0 // tk_p

        def kk(k, s0=s0, nk=nk):
            return jnp.clip(k - s0, 0, nk - 1)

        in_specs.append(pl.BlockSpec((tm, tk_p), lambda i, k, ar=ar, kk=kk: (ar + i, kk(k))))
        in_specs.append(pl.BlockSpec((d, tk_p), lambda i, k, bc=bc, kk=kk: (0, bc + kk(k))))
        args += [a, b]
    xr = x_row0 // tm
    in_specs += [pl.BlockSpec((tm, d), lambda i, k: (xr + i, 0)), pl.BlockSpec((1, d), lambda i, k: (0, 0)),
                 pl.BlockSpec((1, d), lambda i, k: (0, 0))]
    args += [x_arr, g, scale]
    out_specs, out_shape = [], []
    if has_dx:
        in_specs.append(pl.BlockSpec((tm, d), lambda i, k: (i, 0)))
        args.append(resid)
        out_specs.append(pl.BlockSpec((tm, d), lambda i, k: (i, 0)))
        out_shape.append(SDS((m_rows, d), F32))
    out_specs.append(pl.BlockSpec((8, d), lambda i, k: (0, 0)))
    out_shape.append(SDS((8, d), F32))
    return pl.pallas_call(
        body, name=name, grid=(ni, total), in_specs=in_specs, out_specs=out_specs, out_shape=out_shape,
        scratch_shapes=[pltpu.VMEM((tm, d), F32)], compiler_params=_cp("arbitrary", "arbitrary"))(*args)


def _wgrad(a, b, k_rows, name):
    m, n = a.shape[1], b.shape[1]
    tm = _pick(m, (1408, 1024, 512, 384, 256, 128))
    tn = _pick(n, (1408, 1024, 768, 512, 384, 256, 128))
    tk = _pick(k_rows, (1280, 1024, 256))

    def body(a_ref, b_ref, o_ref):
        @pl.when(pl.program_id(2) == 0)
        def _():
            o_ref[...] = jnp.zeros_like(o_ref)

        o_ref[...] += _dot_tn(a_ref[...], b_ref[...])

    return pl.pallas_call(
        body, name=name, grid=(m // tm, n // tn, k_rows // tk),
        in_specs=[pl.BlockSpec((tk, tm), lambda i, j, k: (k, i)), pl.BlockSpec((tk, tn), lambda i, j, k: (k, j))],
        out_specs=pl.BlockSpec((tm, tn), lambda i, j, k: (i, j)), out_shape=SDS((m, n), F32),
        compiler_params=_cp("arbitrary", "arbitrary", "arbitrary"))(a, b)


def _lane_put(col, lane_idx):
    lane = lax.broadcasted_iota(jnp.int32, (1, LANES), 1)
    return jnp.where(lane == lane_idx, col, 0.0)


def _mixer_bwd(dh1, out, hf, hb, z_main, pm, ps, hg, lng, lnb, w_s, b_st, wbm, wbs, wout, mx2, t_rows, nh):
    d = dh1.shape[1]
    ng, sc = w_s.shape[0], w_s.shape[1]
    dh, gd = d // nh, d // ng
    tb = _pick(t_rows, (256,))

    def body(dh1_ref, out_ref, hf_ref, hb_ref, zo, zu, zv, zgm, zgg, pm_ref, ps_ref, hg_ref, lng_ref, lnb_ref, ws_ref, bs_ref,
             wbm_ref, wbs_ref, wo_ref, mx2_ref, dz_ref, dhs_ref, dout_ref, dpm_ref, dps_ref, st_ref, dws_ref, dbs_ref):
        i = pl.program_id(0)

        @pl.when(i == 0)
        def _():
            st_ref[...] = jnp.zeros_like(st_ref)
            dws_ref[...] = jnp.zeros_like(dws_ref)
            dbs_ref[...] = jnp.zeros_like(dbs_ref)

        dh1v = dh1_ref[...]
        doutb = _bf(dh1v * mx2_ref[...])
        dout_ref[...] = doutb
        d_mx2 = jnp.sum(dh1v * out_ref[...].astype(F32), axis=0, keepdims=True)
        dy = _dot_nt(doutb, wo_ref[...])
        sgm, sgg = _sigmoid(zgm[...].astype(F32)), _sigmoid(zgg[...].astype(F32))
        dpmb, dpsb = _bf(dy * sgm), _bf(dy * sgg)
        dpm_ref[...] = dpmb
        dps_ref[...] = dpsb
        dz_ref[:, 3 * d:4 * d] = _bf(dy * pm_ref[...].astype(F32) * sgm * (1.0 - sgm))
        dz_ref[:, 4 * d:5 * d] = _bf(dy * ps_ref[...].astype(F32) * sgg * (1.0 - sgg))
        dym = _dot_nt(dpmb, wbm_ref[...])
        dys = _dot_nt(dpsb, wbs_ref[...])
        hs = hf_ref[...].astype(F32) + hb_ref[...].astype(F32)
        hn, scales = _head_rms(hs, nh, dh)
        so = _sigmoid(zo[...].astype(F32))
        dz_ref[:, 0:d] = _bf(dym * (hn * hg_ref[...]) * so * (1.0 - so))
        dhmn = dym * so
        d_hg = jnp.sum(dhmn * hn, axis=0, keepdims=True)
        dhn = dhmn * hg_ref[...]
        for h in range(nh):
            sl = slice(h * dh, (h + 1) * dh)
            dhs_ref[:, sl] = _bf(scales[h] * (dhn[:, sl] - hn[:, sl] * jnp.mean(dhn[:, sl] * hn[:, sl], axis=-1, keepdims=True)))
        zuv, zvv = zu[...].astype(F32), zv[...].astype(F32)
        u = _gelu(zuv)
        vhat, rstd = _layer_norm(_gelu(zvv))
        vnb = _bf(vhat * lng_ref[...] + lnb_ref[...])
        mixed = _sgu_mix(vnb, ws_ref, bs_ref, tb, ng, gd, sc)
        dz_ref[:, d:2 * d] = _bf(dys * mixed * _gelu_grad(zuv))
        dmix = dys * u
        rows = []
        dbs = jnp.zeros((sc, LANES), F32)
        for ch in range(tb // sc):
            cols = []
            for g in range(ng):
                dm = dmix[ch * sc:(ch + 1) * sc, g * gd:(g + 1) * gd]
                dmb = _bf(dm)
                dws_ref[g] += _dot_nt(dmb, vnb[ch * sc:(ch + 1) * sc, g * gd:(g + 1) * gd])
                dbs = dbs + _lane_put(jnp.sum(dm, axis=1, keepdims=True), g)
                cols.append(_dot_tn(_bf(ws_ref[g]), dmb))
            rows.append(jnp.concatenate(cols, axis=1))
        dbs_ref[...] += dbs
        dvn = jnp.concatenate(rows, axis=0)
        d_lng = jnp.sum(dvn * vhat, axis=0, keepdims=True)
        d_lnb = jnp.sum(dvn, axis=0, keepdims=True)
        dvh = dvn * lng_ref[...]
        dvg = rstd * (dvh - jnp.mean(dvh, axis=-1, keepdims=True) - vhat * jnp.mean(dvh * vhat, axis=-1, keepdims=True))
        dz_ref[:, 2 * d:3 * d] = _bf(dvg * _gelu_grad(zvv))
        st_ref[...] += jnp.concatenate([d_mx2, d_hg, d_lng, d_lnb, jnp.zeros((4, d), F32)], axis=0)

    def tok(col):
        return pl.BlockSpec((tb, d), lambda i: (i, col))

    def full(shape):
        return pl.BlockSpec(shape, lambda i: (0,) * len(shape))

    return pl.pallas_call(
        body, name="mixer_bwd", grid=(t_rows // tb,),
        in_specs=[tok(0), tok(0), tok(0), tok(0), tok(3), tok(4), tok(5), tok(6), tok(7), tok(0), tok(0), full((1, d)),
                  full((1, d)), full((1, d)), full((ng, sc, sc)), full((sc, LANES)), full((d, d)), full((d, d)), full((d, d)),
                  full((1, d))],
        out_specs=[pl.BlockSpec((tb, 5 * d), lambda i: (i, 0)), tok(0), tok(0), tok(0), tok(0), full((8, d)), full((ng, sc, sc)),
                   full((sc, LANES))],
        out_shape=[SDS((t_rows, 5 * d), BF16)] + [SDS((t_rows, d), BF16)] * 4 + [SDS((8, d), F32), SDS((ng, sc, sc), F32),
                                                                                SDS((sc, LANES), F32)],
        compiler_params=_cp("arbitrary"))(dh1, out, hf, hb, z_main, z_main, z_main, z_main, z_main, pm, ps, hg, lng, lnb, w_s,
                                          b_st, wbm, wbs, wout, mx2)


def _mlstm_bwd(qk, z_main, zg, bias, dhs, states_f, states_b, nh, t_rows):
    s_rows = qk.shape[0]
    md = qk.shape[1] // 2
    dh = md // nh
    nc = s_rows // LCH
    nx = t_rows // LCH
    ln = LCH

    def chunk_f(i):
        return jnp.where(i == nc - 1, nc - 1, nc - 2 - i)

    def chunk_b(i):
        return jnp.where(i == nc - 1, nc - 1, i)

    def body(qf, kf, vf, gf, dhf, cf, nf, mf_, qb, kb, vb, gb, dhb, cb, nb, mb_, bias_ref, dqkvf_ref, dgf_ref, dqkvb_ref, dgb_ref,
             dc_sc, dn_sc):
        i = pl.program_id(0)
        is_ctx = i == nc - 1

        @pl.when(i == 0)
        def _():
            dc_sc[...] = jnp.zeros_like(dc_sc)
            dn_sc[...] = jnp.zeros_like(dn_sc)

        for dr, (q_ref, k_ref, v_ref, g_ref, dh_ref, c_ref, n_ref, m_ref, dqkv_ref, dg_ref) in enumerate(
                ((qf, kf, vf, gf, dhf, cf, nf, mf_, dqkvf_ref, dgf_ref), (qb, kb, vb, gb, dhb, cb, nb, mb_, dqkvb_ref, dgb_ref))):
            gz, b_all, b_t, g_t, g_all, mask, mfl = _chunk_gates(g_ref[...], bias_ref[...], dr == 1)
            x1 = jnp.zeros((ln, LANES), F32)
            x2 = jnp.zeros((ln, LANES), F32)
            dig = jnp.zeros((ln, LANES), F32)
            e_row = jnp.zeros((1, LANES), F32)
            for h in range(nh):
                ci, cfl = 2 * dr * nh + h, (2 * dr + 1) * nh + h
                sl = slice(h * dh, (h + 1) * dh)
                q, k, v = q_ref[:, sl], k_ref[:, sl], v_ref[:, sl]
                qf32, kf32 = q.astype(F32), k.astype(F32)
                dhv = jnp.where(is_ctx, 0.0, dh_ref[:, sl].astype(F32))
                c_in, n_in, m_in = c_ref[sl, :], n_ref[0:1, sl], m_ref[h, 0:1, 0:1]
                b_col, b_row, i_col, i_row = b_all[:, cfl:cfl + 1], b_t[cfl:cfl + 1, :], gz[:, ci:ci + 1], g_t[ci:ci + 1, :]
                g = g_all[:, cfl:cfl + 1]
                w, w_int, m_row = _head_weights(b_col, b_row, i_row, m_in, mask)
                s_mat = _dot_nt(q, k) * w
                sb, cb16 = _bf(s_mat), _bf(c_in)
                num = _dot(sb, v) + w_int * _dot(q, cb16)
                den = jnp.sum(s_mat, axis=1, keepdims=True) + w_int * jnp.sum(qf32 * n_in, axis=1, keepdims=True)
                e_m = jnp.exp(-m_row)
                dnm = jnp.maximum(jnp.abs(den), e_m)
                dnum = dhv / dnm
                hdh = jnp.sum((num / dnm) * dhv, axis=1, keepdims=True)
                dden = jnp.where(jnp.abs(den) > e_m, -(hdh / dnm) * jnp.sign(den), 0.0)
                dnum_b = _bf(dnum)
                ds = _dot_nt(dnum_b, v) + dden
                pb = _bf(w * ds)
                gmat = s_mat * ds
                a_old, coef, _ = _head_state_coeffs(g, b_col, i_col, m_in)
                dc_new, dn_new = dc_sc[dr, h], dn_sc[dr, h, 0:1, :]
                dcb = _bf(dc_new)
                dv = _dot_tn(sb, dnum_b) + _dot(_bf(kf32 * coef), dcb)
                dq_inter = w_int * (_dot_nt(dnum_b, cb16) + dden * n_in)
                dq = _dot(pb, k) + dq_inter
                dk_state = coef * (_dot_nt(v, dcb) + dn_new)
                dk = _dot_tn(pb, q) + dk_state
                dqkv_ref[:, sl] = _bf(dq)
                dqkv_ref[:, md + h * dh:md + (h + 1) * dh] = _bf(dk)
                dqkv_ref[:, 2 * md + h * dh:2 * md + (h + 1) * dh] = _bf(dv)
                row_intra = jnp.sum(gmat, axis=1, keepdims=True)
                col_intra = jnp.sum(gmat.T, axis=1, keepdims=True)
                row_inter = jnp.sum(qf32 * dq_inter, axis=1, keepdims=True)
                col_inter = jnp.sum(kf32 * dk_state, axis=1, keepdims=True)
                e_old = a_old * (jnp.sum(jnp.sum(dc_new * c_in, axis=1, keepdims=True), axis=0, keepdims=True)
                                 + jnp.sum(dn_new * n_in, axis=1, keepdims=True))
                x1 = x1 + _lane_put(row_intra - col_intra + row_inter, cfl)
                x2 = x2 + _lane_put(col_inter, cfl)
                e_row = e_row + _lane_put(e_old, cfl)
                dig = dig + _lane_put(col_intra + col_inter, ci)
                dc_sc[dr, h] = a_old * dc_new + _dot_tn(_bf(qf32 * w_int), dnum_b)
                dn_sc[dr, h] = jnp.broadcast_to(a_old * dn_new + jnp.sum(qf32 * (w_int * dden), axis=0, keepdims=True), (8, dh))
            dlogf = (lax.dot_general(mfl, x1, (((0,), (0,)), ((), ())), preferred_element_type=F32, precision=HI)
                     + jnp.dot(mfl, x2, preferred_element_type=F32, precision=HI) - x2 + e_row)
            dg_ref[...] = dig + dlogf * _sigmoid(-gz)

    def tok(cfn, col):
        return pl.BlockSpec((ln, md), lambda i: (cfn(i), col))

    def dht(cfn):
        return pl.BlockSpec((ln, md), lambda i: (jnp.minimum(cfn(i), nx - 1), 0))

    def gat(cfn):
        return pl.BlockSpec((ln, LANES), lambda i: (cfn(i), 0))

    def st(cfn, shape):
        return pl.BlockSpec((None,) + shape, lambda i: (cfn(i),) + (0,) * len(shape))

    st_shapes = ((nh * dh, dh), (8, md), (nh, 8, LANES))

    def side(cfn):
        return [tok(cfn, 0), tok(cfn, 1), tok(cfn, 2), gat(cfn), dht(cfn)] + [st(cfn, s) for s in st_shapes]

    def outs(cfn):
        return [pl.BlockSpec((ln, 3 * md), lambda i: (cfn(i), 0)), gat(cfn)]

    return pl.pallas_call(
        body, name="mlstm_bwd", grid=(nc,),
        in_specs=side(chunk_f) + side(chunk_b) + [pl.BlockSpec((1, LANES), lambda i: (0, 0))],
        out_specs=outs(chunk_f) + outs(chunk_b),
        out_shape=[SDS((s_rows, 3 * md), BF16), SDS((s_rows, LANES), F32)] * 2,
        scratch_shapes=[pltpu.VMEM((2, nh, dh, dh), F32), pltpu.VMEM((2, nh, 8, dh), F32)],
        compiler_params=_cp("arbitrary"))(qk, qk, z_main, zg, dhs, *states_f, qk, qk, z_main, zg, dhs, *states_b, bias)


def _qkv_conv_bwd(dqkv_f, dqkv_b, z_main, conv_w, t_rows, md, qscale):
    s_rows = z_main.shape[0]
    tb = _pick(s_rows, (1280, 1024, 256))
    cb = _pick(md, (512, 256, 128))
    ni, nj, ncq = s_rows // tb, 3 * md // cb, 2 * md // cb
    nb8 = tb // 8
    n_ext = tb + 16

    def body(fm, fp, fn, bm, bp, bn, zm, zp, zn, w_ref, dz_ref, gw_ref):
        j, i = pl.program_id(0), pl.program_id(1)

        @pl.when(j < ncq)
        def _():
            z = jnp.concatenate([zp[...], zm[...], zn[...]], axis=0).astype(F32)
            dqk = (jnp.concatenate([fp[...], fm[...], fn[...]], axis=0).astype(F32)
                   + jnp.concatenate([bp[...], bm[...], bn[...]], axis=0).astype(F32)) * jnp.where(j * cb < md, qscale, 1.0)
            row = i * tb - 8 + lax.broadcasted_iota(jnp.int32, (n_ext, 1), 0)
            prev_ok, next_ok = _seg_masks(row, t_rows, s_rows)
            zprev = jnp.where(prev_ok, pltpu.roll(z, 1, 0), 0.0)
            znext = jnp.where(next_ok, pltpu.roll(z, n_ext - 1, 0), 0.0)
            pre = w_ref[0:1, :] * zprev + w_ref[1:2, :] * z + w_ref[2:3, :] * znext
            sg = _sigmoid(pre)
            dpre = dqk * (sg * (1.0 + pre * (1.0 - sg)))
            dz = (w_ref[1:2, :] * dpre + w_ref[0:1, :] * jnp.where(next_ok, pltpu.roll(dpre, n_ext - 1, 0), 0.0)
                  + w_ref[2:3, :] * jnp.where(prev_ok, pltpu.roll(dpre, 1, 0), 0.0))
            dz_ref[...] = _bf(dz[8:8 + tb])
            dm = dpre[8:8 + tb]

            @pl.when(i == 0)
            def _():
                gw_ref[...] = jnp.zeros_like(gw_ref)

            gw_ref[...] += jnp.concatenate(
                [jnp.sum(dm * zprev[8:8 + tb], axis=0, keepdims=True), jnp.sum(dm * z[8:8 + tb], axis=0, keepdims=True),
                 jnp.sum(dm * znext[8:8 + tb], axis=0, keepdims=True), jnp.zeros((5, cb), F32)], axis=0)

        @pl.when(j >= ncq)
        def _():
            dz_ref[...] = _bf(fm[...].astype(F32) + bm[...].astype(F32))

    def halo(clampj):
        def cj(j):
            return jnp.minimum(j, ncq - 1) if clampj else j
        return [pl.BlockSpec((tb, cb), lambda j, i: (i, cj(j))),
                pl.BlockSpec((8, cb), lambda j, i: (jnp.maximum(i * nb8 - 1, 0), cj(j))),
                pl.BlockSpec((8, cb), lambda j, i: (jnp.minimum((i + 1) * nb8, s_rows // 8 - 1), cj(j)))]

    return pl.pallas_call(
        body, name="qkv_conv_bwd", grid=(nj, ni),
        in_specs=halo(False) + halo(False) + halo(True) + [pl.BlockSpec((8, cb), lambda j, i: (0, jnp.minimum(j, ncq - 1)))],
        out_specs=[pl.BlockSpec((tb, cb), lambda j, i: (i, j)), pl.BlockSpec((8, cb), lambda j, i: (0, jnp.minimum(j, ncq - 1)))],
        out_shape=[SDS((s_rows, 3 * md), BF16), SDS((8, 2 * md), F32)],
        compiler_params=_cp("arbitrary", "arbitrary"))(dqkv_f, dqkv_f, dqkv_f, dqkv_b, dqkv_b, dqkv_b, z_main, z_main, z_main, conv_w)


def _gate_grad_sum(dg_f, dg_b):
    s_rows = dg_f.shape[0]
    tb = _pick(s_rows, (1280, 1024, 256))

    def body(a_ref, b_ref, o_ref, st_ref):
        @pl.when(pl.program_id(0) == 0)
        def _():
            st_ref[...] = jnp.zeros_like(st_ref)

        s = a_ref[...] + b_ref[...]
        o_ref[...] = _bf(s)
        st_ref[...] += jnp.concatenate([jnp.sum(s, axis=0, keepdims=True), jnp.zeros((7, LANES), F32)], axis=0)

    blk = pl.BlockSpec((tb, LANES), lambda i: (i, 0))
    return pl.pallas_call(
        body, name="gate_grad_sum", grid=(s_rows // tb,), in_specs=[blk, blk],
        out_specs=[blk, pl.BlockSpec((8, LANES), lambda i: (0, 0))],
        out_shape=[SDS((s_rows, LANES), BF16), SDS((8, LANES), F32)], compiler_params=_cp("arbitrary"))(dg_f, dg_b)


def _mod_grads(silu_slots, dmx_sh, dmx_slots, dmc_tot, dmc_sh, silu_cctx, c_ctx, w_mod_c):
    d = silu_slots.shape[1]
    ncol, n6 = dmx_sh.shape[1], dmx_slots.shape[1]

    def body(ss_ref, dsh_ref, dsl_ref, dct_ref, dcs_ref, sc_ref, c_ref, w_ref, gw_ref, gb_ref, gc_ref):
        a = jnp.concatenate([ss_ref[...], sc_ref[...], jnp.zeros((7, d), F32)], axis=0)
        b = jnp.concatenate([dsh_ref[...], dcs_ref[...], jnp.zeros((7, ncol), F32)], axis=0)
        gw_ref[...] = lax.dot_general(a, b, (((0,), (0,)), ((), ())), preferred_element_type=F32, precision=HI)
        dct = dct_ref[...]
        gb_ref[...] = jnp.sum(dsl_ref[...], axis=0, keepdims=True) + jnp.concatenate(
            [dct, jnp.zeros((1, n6 - dct.shape[1]), F32)], axis=1)
        t = _dot_nt(_bf(jnp.broadcast_to(dct, (8, dct.shape[1]))), w_ref[...])
        cv = c_ref[...]
        s = _sigmoid(cv)
        gc_ref[...] = t[0:1, :] * (s * (1.0 + cv * (1.0 - s)))

    return pl.pallas_call(body, name="mod_grads", out_shape=[SDS((d, ncol), F32), SDS((1, n6), F32), SDS((1, d), F32)],
                          compiler_params=_cp())(silu_slots, dmx_sh, dmx_slots, dmc_tot, dmc_sh, silu_cctx, c_ctx, w_mod_c)


def _slot_sum(slots):
    ns, r = slots.shape[0], slots.shape[1]
    tb = _pick(r, (1024, 512, 256, 128, 64, 32, 16, 8))

    def body(s_ref, o_ref):
        acc = s_ref[0]
        for k in range(1, ns):
            acc = acc + s_ref[k]
        o_ref[...] = acc

    return pl.pallas_call(
        body, name="slot_sum", grid=(r // tb,), in_specs=[pl.BlockSpec((ns, tb, LANES), lambda i: (0, i, 0))],
        out_specs=pl.BlockSpec((tb, LANES), lambda i: (i, 0)), out_shape=SDS((r, LANES), F32),
        compiler_params=_cp("arbitrary"))(slots)


def _adamw(w, gslots, m, v, name):
    r, cdim = w.shape
    ns, rg = gslots.shape[0], gslots.shape[1]
    tb = r if (rg != r or r % 8) else _pick(r, (128, 64, 32, 16, 8))
    bc1, bc2 = 1.0 - ADAM_B1 ** ADAM_STEP, 1.0 - ADAM_B2 ** ADAM_STEP

    def body(w_ref, g_ref, m_ref, v_ref, go_ref, d_ref, mo_ref, vo_ref):
        g = g_ref[0, 0:tb, :]
        for k in range(1, ns):
            g = g + g_ref[k, 0:tb, :]
        mn = ADAM_B1 * m_ref[...] + (1.0 - ADAM_B1) * g
        vn = ADAM_B2 * v_ref[...] + (1.0 - ADAM_B2) * (g * g)
        go_ref[...] = g
        mo_ref[...] = mn
        vo_ref[...] = vn
        d_ref[...] = -ADAM_LR * ((mn / bc1) / (jnp.sqrt(vn / bc2) + ADAM_EPS) + ADAM_WD * w_ref[...])

    blk = pl.BlockSpec((tb, cdim), lambda i: (i, 0))
    gblk = pl.BlockSpec((ns, tb if rg == r else rg, cdim), lambda i: (0, i, 0))
    return pl.pallas_call(
        body, name=name, grid=(r // tb,), in_specs=[blk, gblk, blk, blk],
        out_specs=[blk] * 4, out_shape=[SDS((r, cdim), F32)] * 4, compiler_params=_cp("arbitrary"))(w, gslots, m, v)


def _pack(parts, row_mult):
    flat = jnp.concatenate([p.reshape(-1) for p in parts])
    n = flat.shape[0]
    rows = -(-n // LANES)
    rows = -(-rows // row_mult) * row_mult
    return jnp.pad(flat, (0, rows * LANES - n)).reshape(rows, LANES)


def _unpack(buf, shapes):
    flat = buf.reshape(-1)
    out, off = [], 0
    for s in shapes:
        n = math.prod(s)
        out.append(flat[off:off + n].reshape(s))
        off += n
    return out


def _pad_cols(a, width):
    return jnp.pad(a, ((0, 0), (0, width - a.shape[1])))


def _pad_lanes(a):
    return _pad_cols(a, LANES)


def _up128(n):
    return -(-n // LANES) * LANES


def kernel(x, c, ctx, c_ctx, w_mod, b_mod, norm1_g, w_in, b_gate, conv_qk, head_norm_g, sgu_ln_g, sgu_ln_b, w_s, b_s, w_branch_mlstm, w_branch_sgu, w_out, norm2_g, w_up, w_ffn_conv, w_down, final_g, loss_target, m_c_ctx, m_w_mod, m_b_mod, m_norm1_g, m_w_in, m_b_gate, m_conv_qk, m_head_norm_g, m_sgu_ln_g, m_sgu_ln_b, m_w_s, m_b_s, m_w_branch_mlstm, m_w_branch_sgu, m_w_out, m_norm2_g, m_w_up, m_w_ffn_conv, m_w_down, m_final_g, v_c_ctx, v_w_mod, v_b_mod, v_norm1_g, v_w_in, v_b_gate, v_conv_qk, v_head_norm_g, v_sgu_ln_g, v_sgu_ln_b, v_w_s, v_b_s, v_w_branch_mlstm, v_w_branch_sgu, v_w_out, v_norm2_g, v_w_up, v_w_ffn_conv, v_w_down, v_final_g):
    t, d = x.shape[1], x.shape[2]
    n_ctx = ctx.shape[1]
    s_rows = t + n_ctx
    nh = b_gate.shape[1] // 4
    md = head_norm_g.shape[1]
    dh = md // nh
    ng, sc = w_s.shape[1], w_s.shape[2]
    dff = w_down.shape[1] * N_DEV
    n_in = w_in.shape[2] * N_DEV
    assert md == d and sgu_ln_g.shape[1] == d and n_ctx == LCH and t % LCH == 0 and t % (8 * GRID_W) == 0
    assert n_in == 8 * d + 4 * nh and 4 * nh <= LANES
    me = 4 * lax.axis_index("x") + 2 * lax.axis_index("y") + lax.axis_index("c")

    n_mod, n_insh, n_upsh = w_mod.shape[2], w_in.shape[2], w_up.shape[2]
    p_mod, p_in, p_up = _up128(n_mod), _up128(n_insh), _up128(n_upsh)
    nq, nf = conv_qk.shape[2], w_ffn_conv.shape[3]
    ffn9 = w_ffn_conv[0].reshape(9, nf)
    colpack = jnp.concatenate([_pad_cols(_bf(w_mod[0]), p_mod), _pad_cols(_bf(w_in[0]), p_in), _pad_cols(_bf(w_up[0]), p_up)], axis=1)
    convpack = jnp.concatenate([jnp.pad(conv_qk[0], ((0, 13), (0, 0))), jnp.pad(ffn9, ((0, 7), (0, 0)))], axis=1)
    g_col, g_bm, g_bs, g_out, g_down, g_conv = _allgather(
        [colpack, _bf(w_branch_mlstm[0]), _bf(w_branch_sgu[0]), _bf(w_out[0]), _bf(w_down[0]), convpack])
    w_mod_f, w_main, w_gate, w_up_f = _assemble_cols(
        g_col, [(0, n_mod, [(0, 0, N_DEV * n_mod, 0)]),
                (p_mod, n_insh, [(1, 0, 3 * md, 0), (2, 3 * md, 4 * nh, 0), (1, 3 * md + 4 * nh, 5 * d, 3 * md)]),
                (p_mod + p_in, n_upsh, [(3, 0, 2 * dff, 0)])],
        [N_MOD * d, 8 * d, LANES, 2 * dff], "assemble_weights")
    wbm_f, wbs_f, wout_f = (g.reshape(d, d) for g in (g_bm, g_bs, g_out))
    w_down_f = g_down.reshape(dff, d)
    convw, wconv9 = _assemble_cols(g_conv, [(0, nq, [(0, 0, N_DEV * nq, 0)]), (nq, nf, [(1, 0, N_DEV * nf, 0)])],
                                   [N_DEV * nq, N_DEV * nf], "assemble_conv_weights")

    cvec = jnp.concatenate([c, c_ctx[None], jnp.zeros((6, d), F32)], axis=0)
    silu_v, mod = _modulation(cvec, w_mod_f, b_mod)
    mx = [mod[0:1, k * d:(k + 1) * d] for k in range(N_MOD)]
    mc = [mod[1:2, k * d:(k + 1) * d] for k in range(2)]
    xs = jnp.concatenate([x[0], ctx[0]], axis=0)
    hn, z_main, zg = _norm_mod_proj(xs, norm1_g, jnp.concatenate([mx[0], mx[1], mc[0], mc[1]], axis=0), w_main, w_gate, t, "in_proj")
    qscale = dh ** -0.5
    qk = _qk_conv(z_main, convw, t, md, qscale)
    bias = _pad_lanes(b_gate)
    fwd = _mlstm_fwd(qk, z_main, zg, bias, nh)
    hf, hb, states_f, states_b = fwd[0], fwd[1], fwd[2:5], fwd[5:8]
    b_st = _pad_lanes(b_s[0].T)
    h1, ym, ys, pm, ps, y, out = _mixer_fwd(hf, hb, z_main, xs, head_norm_g, sgu_ln_g, sgu_ln_b, w_s[0], b_st, wbm_f, wbs_f,
                                            wout_f, mx[2], t, nh)
    hn2, ab = _norm_mod_proj(h1, norm2_g, jnp.concatenate([mx[3], mx[4], mx[3], mx[4]], axis=0), w_up_f, None, t, "up_proj")
    aconv, f, dh2, dffn, st_tail = _ffn_tail(ab, wconv9, w_down_f, h1, mx[5], final_g[None], loss_target[0], dff)

    db, dac = _ffn_bwd_gate(dffn, w_down_f, aconv, ab, dff)
    da, g_wconv9 = _ffn_conv_bwd(dac, ab, wconv9, dff)
    g_wdown = _wgrad(f, dffn, t, "wgrad_down")
    gwup_slots = _scatter_cols([_wgrad(hn2, da, t, "wgrad_up_a"), _wgrad(hn2, db, t, "wgrad_up_b")],
                               [(0, 0, dff, 0), (1, dff, dff, 0)], n_upsh, "scatter_grad_w_up")
    tkf = _pick(dff, (1408, 704, 384, 128))
    dh1, st_n2 = _proj_norm_bwd([(da, 0, w_up_f, 0, dff, tkf), (db, 0, w_up_f, dff, dff, tkf)], h1, 0, norm2_g, mx[4], dh2, t,
                                "up_proj_bwd")
    dz_rest, dhs, dout, dpm, dps, st_mix, g_ws, g_bst = _mixer_bwd(dh1, out, hf, hb, z_main, pm, ps, head_norm_g, sgu_ln_g,
                                                                    sgu_ln_b, w_s[0], b_st, wbm_f, wbs_f, wout_f, mx[2], t, nh)
    g_wout = _wgrad(y, dout, t, "wgrad_out")
    g_wbm = _wgrad(ym, dpm, t, "wgrad_branch_mlstm")
    g_wbs = _wgrad(ys, dps, t, "wgrad_branch_sgu")
    dqkv_f, dg_f, dqkv_b, dg_b = _mlstm_bwd(qk, z_main, zg, bias, dhs, states_f, states_b, nh, t)
    dz_qkv, g_convqk = _qkv_conv_bwd(dqkv_f, dqkv_b, z_main, convw, t, md, qscale)
    dz_g, st_gate = _gate_grad_sum(dg_f, dg_b)
    gwin_slots = _scatter_cols(
        [_wgrad(hn, dz_qkv, s_rows, "wgrad_in_qkv"), _wgrad(hn, dz_g, s_rows, "wgrad_in_gate"), _wgrad(hn, dz_rest, t, "wgrad_in_rest")],
        [(0, 0, 3 * md, 0), (1, 3 * md, 4 * nh, 0), (2, 3 * md + 4 * nh, 5 * d, 0)], n_insh, "scatter_grad_w_in")
    tk = _pick(md, (1024, 512, 256))
    grad_x, st_n1x = _proj_norm_bwd(
        [(dz_qkv, 0, w_main, 0, 3 * md, tk), (dz_rest, 0, w_main, 3 * md, 5 * d, tk), (dz_g, 0, w_gate, 0, LANES, LANES)],
        xs, 0, norm1_g, mx[1], dh1, t, "in_proj_bwd")
    (st_n1c,) = _proj_norm_bwd([(dz_qkv, t, w_main, 0, 3 * md, tk), (dz_g, t, w_gate, 0, LANES, LANES)],
                               xs, t, norm1_g, mc[1], None, n_ctx, "in_proj_bwd_ctx")

    gcq_slots = _scatter_cols([g_convqk], [(0, 0, 2 * md, 0)], nq, "scatter_grad_conv_qk")
    gcf_slots = _scatter_cols([g_wconv9], [(0, 0, dff, 0)], nf, "scatter_grad_ffn_conv")
    per_dest = [gwin_slots, gwup_slots, g_wbm.reshape(N_DEV, d // N_DEV, d), g_wbs.reshape(N_DEV, d // N_DEV, d),
                g_wout.reshape(N_DEV, d // N_DEV, d), g_wdown.reshape(N_DEV, dff // N_DEV, d), gcq_slots, gcf_slots]
    small_parts = [st_n1x[1], st_n1x[2], st_mix[0], st_n2[1], st_n2[2], st_tail[1],
                   st_n1c[1], st_n1c[2],
                   silu_v[0], st_n1x[0] + st_n1c[0], st_gate[0], st_mix[1], st_mix[2], st_mix[3],
                   g_ws.reshape(-1), g_bst[:, :ng].T.reshape(-1), st_n2[0], st_tail[0]]
    gsmall = _pack(small_parts, 8)
    recv = _grad_exchange(per_dest, [gsmall])
    recv_small = recv[-1]
    small_sum = _slot_sum(recv_small).reshape(-1)
    small_slots = recv_small.reshape(N_DEV, -1)
    o_silu, o_n1 = 8 * d, 9 * d
    ncol = N_MOD * d // N_DEV
    dmc_tot = small_sum[6 * d:8 * d][None]
    dmc_pad = jnp.concatenate([dmc_tot, jnp.zeros((1, 4 * d), F32)], axis=1)
    g_wmod, g_bmod, g_cctx = _mod_grads(
        small_slots[:, o_silu:o_silu + d], lax.dynamic_slice_in_dim(small_slots[:, :6 * d], me * ncol, ncol, axis=1),
        small_slots[:, :6 * d], dmc_tot, lax.dynamic_slice_in_dim(dmc_pad, me * ncol, ncol, axis=1), silu_v[1:2], c_ctx[None],
        w_mod_f[:, :2 * d])

    shard_w = (w_in, w_up, w_branch_mlstm, w_branch_sgu, w_out, w_down, conv_qk)
    shard_m = (m_w_in, m_w_up, m_w_branch_mlstm, m_w_branch_sgu, m_w_out, m_w_down, m_conv_qk)
    shard_v = (v_w_in, v_w_up, v_w_branch_mlstm, v_w_branch_sgu, v_w_out, v_w_down, v_conv_qk)
    shard_names = ("w_in", "w_up", "w_branch_mlstm", "w_branch_sgu", "w_out", "w_down", "conv_qk")
    shard_out = [[b[None] for b in _adamw(wa[0], recv[k], ma[0], va[0], "adamw_" + nm)]
                 for k, (wa, ma, va, nm) in enumerate(zip(shard_w, shard_m, shard_v, shard_names))]
    shard_out.append([b.reshape(w_ffn_conv.shape) for b in
                      _adamw(ffn9, recv[7], m_w_ffn_conv[0].reshape(9, nf), v_w_ffn_conv[0].reshape(9, nf), "adamw_w_ffn_conv")])
    mod_out = [b[None] for b in _adamw(w_mod[0], g_wmod[None], m_w_mod[0], v_w_mod[0], "adamw_w_mod")]

    def rep(cc, bm, n1, bg, hg, lg, lb, ws, bs, n2, fg):
        return [cc.reshape(-1), bm.reshape(-1), n1.reshape(-1), _pad_lanes(bg.reshape(1, -1)).reshape(-1), hg.reshape(-1),
                lg.reshape(-1), lb.reshape(-1), ws.reshape(-1), bs.reshape(-1), n2.reshape(-1), fg.reshape(-1)]

    o = o_n1
    g_rep_parts = [g_cctx, g_bmod]
    for n in (d, LANES, d, d, d, ng * sc * sc, ng * sc, d, d):
        g_rep_parts.append(small_sum[o:o + n])
        o += n
    rep_shapes = [(d,), (1, N_MOD * d), (1, d), (1, LANES), (1, d), (1, d), (1, d), (1, ng, sc, sc), (1, ng, sc), (1, d), (d,)]
    rep_out = _adamw(
        _pack(rep(c_ctx, b_mod, norm1_g, b_gate, head_norm_g, sgu_ln_g, sgu_ln_b, w_s, b_s, norm2_g, final_g), 8),
        _pack(g_rep_parts, 8)[None],
        _pack(rep(m_c_ctx, m_b_mod, m_norm1_g, m_b_gate, m_head_norm_g, m_sgu_ln_g, m_sgu_ln_b, m_w_s, m_b_s, m_norm2_g, m_final_g), 8),
        _pack(rep(v_c_ctx, v_b_mod, v_norm1_g, v_b_gate, v_head_norm_g, v_sgu_ln_g, v_sgu_ln_b, v_w_s, v_b_s, v_norm2_g, v_final_g), 8),
        "adamw_replicated")

    def assemble(k):
        r = _unpack(rep_out[k], rep_shapes)
        s = [o[k] for o in shard_out]
        return [r[0], mod_out[k], r[1], r[2], s[0], r[3][:, :4 * nh], s[6], r[4], r[5], r[6], r[7], r[8], s[2], s[3], s[4], r[9],
                s[1], s[7], s[5], r[10]]

    loss = lax.psum(st_tail[2, 0], ("x", "y", "c"))
    outs = [loss, grad_x[None]]
    for k in range(4):
        outs += assemble(k)
    return tuple(outs)
```

```python
import functools
import math

import jax
import jax.numpy as jnp
from jax import lax
from jax.experimental import pallas as pl
from jax.experimental.pallas import tpu as pltpu

F32, BF16 = jnp.float32, jnp.bfloat16
EPS = 1e-6
M_INIT = -1e30
NEG = -1e30
GRID_W = 64
LCH = 256
N_MOD = 6
N_DEV = 8
LANES = 128
ADAM_LR, ADAM_B1, ADAM_B2, ADAM_EPS, ADAM_WD, ADAM_STEP = 0.001, 0.9, 0.999, 1e-08, 0.01, 10
GELU_C = math.sqrt(2.0 / math.pi)
GELU_A = 0.044715
VMEM_LIMIT = 56 * 1024 * 1024
HI = lax.Precision.HIGHEST
SDS = jax.ShapeDtypeStruct
MESH_ID = pl.DeviceIdType.MESH


def _pick(n, cands):
    for c in cands:
        if n % c == 0:
            return c
    raise ValueError(f"no block size for {n} in {cands}")


def _cp(*sem):
    return pltpu.CompilerParams(dimension_semantics=sem if sem else None, vmem_limit_bytes=VMEM_LIMIT)


def _sigmoid(x):
    return 0.5 * jnp.tanh(0.5 * x) + 0.5


def _split3(x):
    hi = x.astype(BF16)
    r = x - hi.astype(F32)
    mid = r.astype(BF16)
    return hi, mid, (r - mid.astype(F32)).astype(BF16)


def _mask_dot(mask_b, x):
    hi, mid, lo = _split3(x)
    return (_dot(mask_b, lo) + _dot(mask_b, mid)) + _dot(mask_b, hi)


def _mask_dot_t(mask_b, x):
    hi, mid, lo = _split3(x)
    return (_dot_tn(mask_b, lo) + _dot_tn(mask_b, mid)) + _dot_tn(mask_b, hi)


def _gelu(x):
    return 0.5 * x * (1.0 + jnp.tanh(GELU_C * (x + GELU_A * x * x * x)))


def _gelu_grad(x):
    t = jnp.tanh(GELU_C * (x + GELU_A * x * x * x))
    return 0.5 * (1.0 + t) + 0.5 * x * (1.0 - t * t) * GELU_C * (1.0 + 3.0 * GELU_A * x * x)


def _log_sigmoid(x):
    return jnp.minimum(x, 0.0) - jnp.log(1.0 + jnp.exp(-jnp.abs(x)))


def _dot(a, b):
    return jnp.dot(a, b, preferred_element_type=F32)


def _dot_nt(a, b):
    return lax.dot_general(a, b, (((1,), (1,)), ((), ())), preferred_element_type=F32)


def _dot_tn(a, b):
    return lax.dot_general(a, b, (((0,), (0,)), ((), ())), preferred_element_type=F32)


def _bf(x):
    return x.astype(BF16)


def _allgather(arrs):
    na = len(arrs)

    def body(*refs):
        x_refs, o_refs = refs[:na], refs[na:2 * na]
        send_sems, recv_sems, local_sems = refs[2 * na:]
        x, y, c = lax.axis_index("x"), lax.axis_index("y"), lax.axis_index("c")
        me, sibling = (x, y, c), (x, y, 1 - c)
        chips = [(1 - x, y), (x, 1 - y), (1 - x, 1 - y)]

        def copy(a, k, block, to, src=None):
            slot = o_refs[a].at[4 * block[0] + 2 * block[1] + block[2]]
            return pltpu.make_async_remote_copy(
                src_ref=slot if src is None else src, dst_ref=slot, send_sem=send_sems.at[7 * a + k],
                recv_sem=recv_sems.at[7 * a + k], device_id=to, device_id_type=MESH_ID)

        mine = [pltpu.make_async_copy(x_refs[a], o_refs[a].at[4 * x + 2 * y + c], local_sems.at[a]) for a in range(na)]
        for cp in mine:
            cp.start()
        first = []
        for a in range(na):
            first.append(copy(a, 0, me, sibling, src=x_refs[a]))
            first += [copy(a, 1 + j, me, (*chip, c), src=x_refs[a]) for j, chip in enumerate(chips)]
        for cp in first:
            cp.start()
        passed = []
        for j, chip in enumerate(chips):
            for a in range(na):
                copy(a, 1 + j, (*chip, c), me).wait_recv()
                passed.append(copy(a, 4 + j, (*chip, c), sibling))
                passed[-1].start()
        for a in range(na):
            copy(a, 0, sibling, me).wait_recv()
            for j, chip in enumerate(chips):
                copy(a, 4 + j, (*chip, 1 - c), me).wait_recv()
        for cp in first + passed:
            cp.wait_send()
        for cp in mine:
            cp.wait()

    anyspec = pl.BlockSpec(memory_space=pl.ANY)
    return pl.pallas_call(
        body, name="weights_allgather",
        out_shape=[SDS((N_DEV,) + a.shape, a.dtype) for a in arrs],
        in_specs=[anyspec] * na, out_specs=[anyspec] * na,
        scratch_shapes=[pltpu.SemaphoreType.DMA((7 * na,)), pltpu.SemaphoreType.DMA((7 * na,)), pltpu.SemaphoreType.DMA((na,))],
    )(*arrs)


def _grad_exchange(per_dest, shared):
    nd, ns = len(per_dest), len(shared)
    na = nd + ns

    def body(*refs):
        in_refs, out_refs = refs[:na], refs[na:2 * na]
        send_sems, recv_sems, local_sems = refs[2 * na:]
        x, y, c = lax.axis_index("x"), lax.axis_index("y"), lax.axis_index("c")
        me = 4 * x + 2 * y + c

        def src(a, idx):
            return in_refs[a].at[idx] if a < nd else in_refs[a]

        loc = [pltpu.make_async_copy(src(a, me), out_refs[a].at[me], local_sems.at[a]) for a in range(na)]
        for cp in loc:
            cp.start()
        sends, recvs = [], []
        for k in range(1, N_DEV):
            px = 1 - x if k & 4 else x
            py = 1 - y if k & 2 else y
            pc = 1 - c if k & 1 else c
            peer, pidx = (px, py, pc), 4 * px + 2 * py + pc
            for a in range(na):
                sem = 7 * a + k - 1
                sends.append(pltpu.make_async_remote_copy(
                    src_ref=src(a, pidx), dst_ref=out_refs[a].at[me], send_sem=send_sems.at[sem],
                    recv_sem=recv_sems.at[sem], device_id=peer, device_id_type=MESH_ID))
                recvs.append(pltpu.make_async_remote_copy(
                    src_ref=src(a, pidx), dst_ref=out_refs[a].at[pidx], send_sem=send_sems.at[sem],
                    recv_sem=recv_sems.at[sem], device_id=peer, device_id_type=MESH_ID))
        for cp in sends:
            cp.start()
        for cp in recvs:
            cp.wait_recv()
        for cp in sends:
            cp.wait_send()
        for cp in loc:
            cp.wait()

    anyspec = pl.BlockSpec(memory_space=pl.ANY)
    return pl.pallas_call(
        body, name="grad_exchange",
        out_shape=[SDS(a.shape, a.dtype) for a in per_dest] + [SDS((N_DEV,) + a.shape, a.dtype) for a in shared],
        in_specs=[anyspec] * na, out_specs=[anyspec] * na,
        scratch_shapes=[pltpu.SemaphoreType.DMA((7 * na,)), pltpu.SemaphoreType.DMA((7 * na,)), pltpu.SemaphoreType.DMA((na,))],
    )(*per_dest, *shared)


def _col_pieces(n, segments):
    out = []
    for j in range(N_DEV):
        lo, hi = j * n, (j + 1) * n
        for (k, s0, w, c0) in segments:
            a, b = max(lo, s0), min(hi, s0 + w)
            if a < b:
                out.append((j, a - lo, b - lo, k, c0 + a - s0, c0 + b - s0))
    return out


def _assemble_cols(slots, groups, out_widths, name):
    r, p = slots.shape[1], slots.shape[2]
    tb = _pick(r, (128, 64, 32, 16, 8))
    covered = [0] * len(out_widths)
    for (_, n, segs) in groups:
        for (k, _, w, _) in segs:
            covered[k] += w

    def body(s_ref, *o_refs):
        for k, wd in enumerate(out_widths):
            if covered[k] < wd:
                o_refs[k][...] = jnp.zeros_like(o_refs[k])
        for (off, n, segs) in groups:
            for (j, a0, a1, k, d0, d1) in _col_pieces(n, segs):
                o_refs[k][:, d0:d1] = s_ref[j, :, off + a0:off + a1]

    return pl.pallas_call(
        body, name=name, grid=(r // tb,), in_specs=[pl.BlockSpec((N_DEV, tb, p), lambda i: (0, i, 0))],
        out_specs=[pl.BlockSpec((tb, w), lambda i: (i, 0)) for w in out_widths],
        out_shape=[SDS((r, w), slots.dtype) for w in out_widths], compiler_params=_cp("arbitrary"))(slots)


def _scatter_cols(pieces, segments, n, name):
    r = pieces[0].shape[0]
    tb = _pick(r, (128, 64, 32, 16, 8))

    def body(*refs):
        p_refs, o_ref = refs[:-1], refs[-1]
        for (j, a0, a1, k, d0, d1) in _col_pieces(n, segments):
            o_ref[j, :, a0:a1] = p_refs[k][:, d0:d1]

    return pl.pallas_call(
        body, name=name, grid=(r // tb,), in_specs=[pl.BlockSpec((tb, a.shape[1]), lambda i: (i, 0)) for a in pieces],
        out_specs=pl.BlockSpec((N_DEV, tb, n), lambda i: (0, i, 0)), out_shape=SDS((N_DEV, r, n), pieces[0].dtype),
        compiler_params=_cp("arbitrary"))(*pieces)


def _modulation(cvec, w_mod, b_mod):
    d, n = w_mod.shape

    def body(c_ref, w_ref, b_ref, s_ref, o_ref):
        cv = c_ref[...]
        s = cv * _sigmoid(cv)
        s_ref[...] = s
        o_ref[...] = _dot(_bf(s), w_ref[...]) + b_ref[...]

    return pl.pallas_call(body, name="modulation", out_shape=(SDS((8, d), F32), SDS((8, n), F32)),
                          compiler_params=_cp())(cvec, w_mod, b_mod)


def _norm_mod_proj(xs, g, shsc, w_main, w_gate, t_rows, name):
    s_rows, d = xs.shape
    n = w_main.shape[1]
    tb = _pick(s_rows, (1280, 1024, 256))
    cb = _pick(n, (1408, 1024, 768, 512, 384, 256, 128))
    gate = w_gate is not None

    def body(*refs):
        if gate:
            x_ref, g_ref, ss_ref, wm_ref, wg_ref, hn_ref, z_ref, zg_ref, hn_sc = refs
        else:
            x_ref, g_ref, ss_ref, wm_ref, hn_ref, z_ref, hn_sc = refs
        i, j = pl.program_id(0), pl.program_id(1)

        @pl.when(j == 0)
        def _():
            x = x_ref[...]
            r = lax.rsqrt(jnp.mean(x * x, axis=-1, keepdims=True) + EPS)
            row = i * tb + lax.broadcasted_iota(jnp.int32, (tb, 1), 0)
            isx = row < t_rows
            sh = jnp.where(isx, ss_ref[0:1, :], ss_ref[2:3, :])
            sc = jnp.where(isx, ss_ref[1:2, :], ss_ref[3:4, :])
            hb = _bf((x * r * g_ref[...]) * (1.0 + sc) + sh)
            hn_sc[...] = hb
            hn_ref[...] = hb
            if gate:
                zg_ref[...] = _dot(hb, wg_ref[...])

        z_ref[...] = _bf(_dot(hn_sc[...], wm_ref[...]))

    in_specs = [pl.BlockSpec((tb, d), lambda i, j: (i, 0)), pl.BlockSpec((1, d), lambda i, j: (0, 0)),
                pl.BlockSpec((4, d), lambda i, j: (0, 0)), pl.BlockSpec((d, cb), lambda i, j: (0, j))]
    out_specs = [pl.BlockSpec((tb, d), lambda i, j: (i, 0)), pl.BlockSpec((tb, cb), lambda i, j: (i, j))]
    out_shape = [SDS((s_rows, d), BF16), SDS((s_rows, n), BF16)]
    args = [xs, g, shsc, w_main]
    if gate:
        in_specs.append(pl.BlockSpec((d, LANES), lambda i, j: (0, 0)))
        out_specs.append(pl.BlockSpec((tb, LANES), lambda i, j: (i, 0)))
        out_shape.append(SDS((s_rows, LANES), F32))
        args.append(w_gate)
    return pl.pallas_call(
        body, name=name, grid=(s_rows // tb, n // cb), in_specs=in_specs, out_specs=out_specs, out_shape=out_shape,
        scratch_shapes=[pltpu.VMEM((tb, d), BF16)], compiler_params=_cp("arbitrary", "arbitrary"))(*args)


def _seg_masks(row, t_rows, s_rows):
    prev_ok = (row != 0) & (row != t_rows)
    next_ok = (row != t_rows - 1) & (row != s_rows - 1)
    return prev_ok, next_ok


def _shift_rows(z, halo_prev, halo_next, tb):
    loc = lax.broadcasted_iota(jnp.int32, (tb, 1), 0)
    zp = jnp.where(loc == 0, halo_prev, pltpu.roll(z, 1, 0))
    zn = jnp.where(loc == tb - 1, halo_next, pltpu.roll(z, tb - 1, 0))
    return zp, zn


def _qk_conv(z_main, conv_w, t_rows, md, qscale):
    s_rows = z_main.shape[0]
    tb = _pick(s_rows, (1280, 1024, 256))
    cb = _pick(md, (512, 256, 128))
    nb8 = tb // 8

    def body(zm, zp, zn, w_ref, o_ref):
        i, j = pl.program_id(0), pl.program_id(1)
        z = zm[...].astype(F32)
        zprev, znext = _shift_rows(z, zp[7:8, :].astype(F32), zn[0:1, :].astype(F32), tb)
        row = i * tb + lax.broadcasted_iota(jnp.int32, (tb, 1), 0)
        prev_ok, next_ok = _seg_masks(row, t_rows, s_rows)
        pre = (w_ref[0:1, :] * jnp.where(prev_ok, zprev, 0.0) + w_ref[1:2, :] * z
               + w_ref[2:3, :] * jnp.where(next_ok, znext, 0.0))
        scale = jnp.where(j * cb < md, qscale, 1.0)
        o_ref[...] = _bf(pre * _sigmoid(pre) * scale)

    return pl.pallas_call(
        body, name="qk_conv", grid=(s_rows // tb, 2 * md // cb),
        in_specs=[pl.BlockSpec((tb, cb), lambda i, j: (i, j)),
                  pl.BlockSpec((8, cb), lambda i, j: (jnp.maximum(i * nb8 - 1, 0), j)),
                  pl.BlockSpec((8, cb), lambda i, j: (jnp.minimum((i + 1) * nb8, s_rows // 8 - 1), j)),
                  pl.BlockSpec((8, cb), lambda i, j: (0, j))],
        out_specs=pl.BlockSpec((tb, cb), lambda i, j: (i, j)),
        out_shape=SDS((s_rows, 2 * md), BF16), compiler_params=_cp("arbitrary", "arbitrary"))(z_main, z_main, z_main, conv_w)


def _chunk_gates(gates, bias, rev):
    ln = gates.shape[0]
    gz = gates + bias
    logf = _log_sigmoid(gz)
    r_id = lax.broadcasted_iota(jnp.int32, (ln, ln), 0)
    c_id = lax.broadcasted_iota(jnp.int32, (ln, ln), 1)
    mask = (c_id >= r_id) if rev else (c_id <= r_id)
    mb = mask.astype(F32).astype(BF16)
    b_all = _mask_dot(mb, logf)
    g_all = jnp.sum(logf, axis=0, keepdims=True)
    return gz, b_all, b_all.T, gz.T, g_all, mask, mb


def _head_weights(b_col, b_row, i_row, m_in, mask):
    d = jnp.where(mask, b_col - b_row + i_row, NEG)
    inter = b_col + m_in
    m_row = jnp.maximum(inter, jnp.max(d, axis=1, keepdims=True))
    return jnp.exp(d - m_row), jnp.exp(inter - m_row), m_row


def _head_state_coeffs(g, b_col, i_col, m_in):
    a = g - b_col + i_col
    m_new = jnp.maximum(g + m_in, jnp.max(a, axis=0, keepdims=True))
    return jnp.exp(g + m_in - m_new), jnp.exp(a - m_new), m_new


def _mlstm_fwd(qk, z_main, zg, bias, nh):
    s_rows = qk.shape[0]
    md = qk.shape[1] // 2
    dh = md // nh
    nc = s_rows // LCH
    ln = LCH

    def chunk_f(i):
        return jnp.where(i == 0, nc - 1, i - 1)

    def chunk_b(i):
        return jnp.where(i == 0, nc - 1, nc - 1 - i)

    def body(qf, kf, vf, gf, qb, kb, vb, gb, bias_ref, hf_ref, hb_ref, cf_ref, nf_ref, mf_ref, cb_ref, nb_ref, mb_ref,
             c_sc, n_sc, m_sc):
        i = pl.program_id(0)

        @pl.when(i == 0)
        def _():
            c_sc[...] = jnp.zeros_like(c_sc)
            n_sc[...] = jnp.zeros_like(n_sc)
            m_sc[...] = jnp.full(m_sc.shape, M_INIT, F32)

        for dr, (q_ref, k_ref, v_ref, g_ref, h_ref, c_out, n_out, m_out) in enumerate(
                ((qf, kf, vf, gf, hf_ref, cf_ref, nf_ref, mf_ref), (qb, kb, vb, gb, hb_ref, cb_ref, nb_ref, mb_ref))):
            gz, b_all, b_t, g_t, g_all, mask, _ = _chunk_gates(g_ref[...], bias_ref[...], dr == 1)
            for h in range(nh):
                ci, cf = 2 * dr * nh + h, (2 * dr + 1) * nh + h
                sl = slice(h * dh, (h + 1) * dh)
                q, k, v = q_ref[:, sl], k_ref[:, sl], v_ref[:, sl]
                c_in, n_in, m_in = c_sc[dr, h], n_sc[dr, h, 0:1, :], m_sc[dr, h, 0:1, 0:1]
                c_out[sl, :] = c_in
                n_out[:, sl] = n_sc[dr, h]
                m_out[h] = m_sc[dr, h]
                b_col, b_row, i_col, i_row = b_all[:, cf:cf + 1], b_t[cf:cf + 1, :], gz[:, ci:ci + 1], g_t[ci:ci + 1, :]
                g = g_all[:, cf:cf + 1]
                w, w_int, m_row = _head_weights(b_col, b_row, i_row, m_in, mask)
                s_mat = _dot_nt(q, k) * w
                num = _dot(_bf(s_mat), v) + w_int * _dot(q, _bf(c_in))
                den = jnp.sum(s_mat, axis=1, keepdims=True) + w_int * jnp.sum(q.astype(F32) * n_in, axis=1, keepdims=True)
                h_ref[:, sl] = _bf(num / jnp.maximum(jnp.abs(den), jnp.exp(-m_row)))
                a_old, coef, m_new = _head_state_coeffs(g, b_col, i_col, m_in)
                kw = k.astype(F32) * coef
                c_sc[dr, h] = a_old * c_in + _dot_tn(_bf(kw), v)
                n_sc[dr, h] = jnp.broadcast_to(a_old * n_in + jnp.sum(kw, axis=0, keepdims=True), (8, dh))
                m_sc[dr, h] = jnp.broadcast_to(m_new, (8, LANES))

    def tok(cfn, col):
        return pl.BlockSpec((ln, md), lambda i: (cfn(i), col))

    def gat(cfn):
        return pl.BlockSpec((ln, LANES), lambda i: (cfn(i), 0))

    def st(cfn, shape):
        return pl.BlockSpec((None,) + shape, lambda i: (cfn(i),) + (0,) * len(shape))

    st_shapes = ((nh * dh, dh), (8, md), (nh, 8, LANES))
    return pl.pallas_call(
        body, name="mlstm_fwd", grid=(nc,),
        in_specs=[tok(chunk_f, 0), tok(chunk_f, 1), tok(chunk_f, 2), gat(chunk_f),
                  tok(chunk_b, 0), tok(chunk_b, 1), tok(chunk_b, 2), gat(chunk_b),
                  pl.BlockSpec((1, LANES), lambda i: (0, 0))],
        out_specs=[tok(chunk_f, 0), tok(chunk_b, 0)] + [st(chunk_f, s) for s in st_shapes] + [st(chunk_b, s) for s in st_shapes],
        out_shape=[SDS((s_rows, md), BF16)] * 2 + [SDS((nc,) + s, F32) for s in st_shapes] * 2,
        scratch_shapes=[pltpu.VMEM((2, nh, dh, dh), F32), pltpu.VMEM((2, nh, 8, dh), F32), pltpu.VMEM((2, nh, 8, LANES), F32)],
        compiler_params=_cp("arbitrary"))(qk, qk, z_main, zg, qk, qk, z_main, zg, bias)


def _head_rms(hs, nh, dh):
    parts, scales = [], []
    for h in range(nh):
        hh = hs[:, h * dh:(h + 1) * dh]
        r = lax.rsqrt(jnp.mean(hh * hh, axis=-1, keepdims=True) + EPS)
        parts.append(hh * r)
        scales.append(r)
    return jnp.concatenate(parts, axis=1), scales


def _layer_norm(v):
    vc = v - jnp.mean(v, axis=-1, keepdims=True)
    r = lax.rsqrt(jnp.mean(vc * vc, axis=-1, keepdims=True) + EPS)
    return vc * r, r


def _sgu_mix(vnb, ws_ref, bs_ref, tb, ng, gd, sc):
    rows = []
    for ch in range(tb // sc):
        cols = []
        for g in range(ng):
            blk = vnb[ch * sc:(ch + 1) * sc, g * gd:(g + 1) * gd]
            cols.append(_dot(_bf(ws_ref[g]), blk) + bs_ref[:, g:g + 1])
        rows.append(jnp.concatenate(cols, axis=1))
    return jnp.concatenate(rows, axis=0)


def _mixer_fwd(hf, hb, z_main, xs, hg, lng, lnb, w_s, b_st, wbm, wbs, wout, mx2, t_rows, nh):
    d = xs.shape[1]
    ng, sc = w_s.shape[0], w_s.shape[1]
    dh, gd = d // nh, d // ng
    tb = _pick(t_rows, (256,))

    def body(hf_ref, hb_ref, zo, zu, zv, zgm, zgg, x_ref, hg_ref, lng_ref, lnb_ref, ws_ref, bs_ref, wbm_ref, wbs_ref,
             wo_ref, mx2_ref, h1_ref, ym_ref, ys_ref, pm_ref, ps_ref, y_ref, out_ref):
        hs = hf_ref[...].astype(F32) + hb_ref[...].astype(F32)
        hn, _ = _head_rms(hs, nh, dh)
        ym = _bf(_sigmoid(zo[...].astype(F32)) * (hn * hg_ref[...]))
        ym_ref[...] = ym
        vhat, _ = _layer_norm(_gelu(zv[...].astype(F32)))
        vnb = _bf(vhat * lng_ref[...] + lnb_ref[...])
        ys = _bf(_gelu(zu[...].astype(F32)) * _sgu_mix(vnb, ws_ref, bs_ref, tb, ng, gd, sc))
        ys_ref[...] = ys
        pm = _dot(ym, wbm_ref[...])
        ps = _dot(ys, wbs_ref[...])
        pm_ref[...] = _bf(pm)
        ps_ref[...] = _bf(ps)
        y = _bf(_sigmoid(zgm[...].astype(F32)) * pm + _sigmoid(zgg[...].astype(F32)) * ps)
        y_ref[...] = y
        out = _dot(y, wo_ref[...])
        out_ref[...] = _bf(out)
        h1_ref[...] = x_ref[...] + mx2_ref[...] * out

    def tok(col):
        return pl.BlockSpec((tb, d), lambda i: (i, col))

    def full(shape):
        return pl.BlockSpec(shape, lambda i: (0,) * len(shape))

    return pl.pallas_call(
        body, name="mixer_fwd", grid=(t_rows // tb,),
        in_specs=[tok(0), tok(0), tok(3), tok(4), tok(5), tok(6), tok(7), tok(0), full((1, d)), full((1, d)), full((1, d)),
                  full((ng, sc, sc)), full((sc, LANES)), full((d, d)), full((d, d)), full((d, d)), full((1, d))],
        out_specs=[tok(0)] * 7,
        out_shape=[SDS((t_rows, d), F32)] + [SDS((t_rows, d), BF16)] * 6,
        compiler_params=_cp("arbitrary"))(hf, hb, z_main, z_main, z_main, z_main, z_main, xs, hg, lng, lnb, w_s, b_st,
                                          wbm, wbs, wout, mx2)


def _grid_taps(a_ext, n_ext):
    col = lax.broadcasted_iota(jnp.int32, (n_ext, 1), 0) % GRID_W
    left = jnp.where(col != 0, pltpu.roll(a_ext, 1, 0), 0.0)
    right = jnp.where(col != GRID_W - 1, pltpu.roll(a_ext, n_ext - 1, 0), 0.0)
    return left, right


def _with_halo(prev, main, nxt, i, ni, tb):
    ext = jnp.concatenate([prev, main, nxt], axis=0).astype(F32)
    pos = lax.broadcasted_iota(jnp.int32, (tb + 2 * GRID_W, 1), 0)
    inside = ((pos >= GRID_W) | (i > 0)) & ((pos < tb + GRID_W) | (i < ni - 1))
    return jnp.where(inside, ext, 0.0)


def _halo_specs(tb, cb, t_rows, col0=0):
    nh64 = tb // GRID_W
    return [pl.BlockSpec((tb, cb), lambda i, j: (i, col0 + j)),
            pl.BlockSpec((GRID_W, cb), lambda i, j: (jnp.maximum(i * nh64 - 1, 0), col0 + j)),
            pl.BlockSpec((GRID_W, cb), lambda i, j: (jnp.minimum((i + 1) * nh64, t_rows // GRID_W - 1), col0 + j))]


def _ffn_tail(ab, w_conv9, w_down, h1, mx5, gfin, target, dff):
    t_rows, d = h1.shape
    tb = _pick(t_rows, (512,))
    cb = _pick(dff, (256, 128))
    ni, nj = t_rows // tb, dff // cb
    n_ext = tb + 2 * GRID_W

    def body(am, ap, an, b_ref, wc_ref, wd_ref, h1_ref, mx5_ref, gf_ref, tg_ref, ac_ref, f_ref, dh2_ref, dffn_ref, st_ref, acc):
        i, j = pl.program_id(0), pl.program_id(1)
        a_ext = _with_halo(ap[...], am[...], an[...], i, ni, tb)
        left, right = _grid_taps(a_ext, n_ext)
        conv = jnp.zeros((tb, cb), F32)
        for di in range(3):
            o = di * GRID_W
            conv = conv + (wc_ref[3 * di:3 * di + 1, :] * left[o:o + tb] + wc_ref[3 * di + 1:3 * di + 2, :] * a_ext[o:o + tb]
                           + wc_ref[3 * di + 2:3 * di + 3, :] * right[o:o + tb])
        ac_ref[...] = _bf(conv)
        fb = _bf(conv * _sigmoid(conv) * b_ref[...].astype(F32))
        f_ref[...] = fb

        @pl.when(j == 0)
        def _():
            acc[...] = jnp.zeros_like(acc)

        @pl.when((i == 0) & (j == 0))
        def _():
            st_ref[...] = jnp.zeros_like(st_ref)

        acc[...] += _dot(fb, wd_ref[...])

        @pl.when(j == nj - 1)
        def _():
            ffn = acc[...]
            h2 = h1_ref[...] + mx5_ref[...] * ffn
            r = lax.rsqrt(jnp.mean(h2 * h2, axis=-1, keepdims=True) + EPS)
            xn = h2 * r
            e = xn * gf_ref[...] - tg_ref[...]
            loss = 0.5 * jnp.sum(jnp.sum(e * e, axis=1, keepdims=True), axis=0, keepdims=True) / d
            dy = e * (1.0 / d)
            dxn = dy * gf_ref[...]
            dh2 = r * (dxn - xn * jnp.mean(dxn * xn, axis=-1, keepdims=True))
            dh2_ref[...] = dh2
            dffn_ref[...] = _bf(dh2 * mx5_ref[...])
            st_ref[...] += jnp.concatenate(
                [jnp.sum(dy * xn, axis=0, keepdims=True), jnp.sum(dh2 * ffn, axis=0, keepdims=True),
                 jnp.broadcast_to(loss, (1, d)), jnp.zeros((5, d), F32)], axis=0)

    def tokd():
        return pl.BlockSpec((tb, d), lambda i, j: (i, 0))

    def rowd():
        return pl.BlockSpec((1, d), lambda i, j: (0, 0))

    return pl.pallas_call(
        body, name="ffn_tail", grid=(ni, nj),
        in_specs=_halo_specs(tb, cb, t_rows) + [pl.BlockSpec((tb, cb), lambda i, j: (i, nj + j)),
                                                pl.BlockSpec((16, cb), lambda i, j: (0, j)),
                                                pl.BlockSpec((cb, d), lambda i, j: (j, 0)), tokd(), rowd(), rowd(), tokd()],
        out_specs=[pl.BlockSpec((tb, cb), lambda i, j: (i, j)), pl.BlockSpec((tb, cb), lambda i, j: (i, j)), tokd(), tokd(),
                   pl.BlockSpec((8, d), lambda i, j: (0, 0))],
        out_shape=[SDS((t_rows, dff), BF16), SDS((t_rows, dff), BF16), SDS((t_rows, d), F32), SDS((t_rows, d), BF16),
                   SDS((8, d), F32)],
        scratch_shapes=[pltpu.VMEM((tb, d), F32)],
        compiler_params=_cp("arbitrary", "arbitrary"))(ab, ab, ab, ab, w_conv9, w_down, h1, mx5, gfin, target)


def _ffn_bwd_gate(dffn, w_down, aconv, ab, dff):
    t_rows, d = dffn.shape
    tb = _pick(t_rows, (512,))
    cb = _pick(dff, (256, 128))
    nj = dff // cb

    def body(g_ref, wd_ref, ac_ref, b_ref, db_ref, dac_ref):
        df = _dot_nt(g_ref[...], wd_ref[...])
        ac = ac_ref[...].astype(F32)
        sa = _sigmoid(ac)
        db_ref[...] = _bf(df * ac * sa)
        dac_ref[...] = _bf(df * b_ref[...].astype(F32) * (sa * (1.0 + ac * (1.0 - sa))))

    blk = pl.BlockSpec((tb, cb), lambda i, j: (i, j))
    return pl.pallas_call(
        body, name="ffn_bwd_gate", grid=(t_rows // tb, nj),
        in_specs=[pl.BlockSpec((tb, d), lambda i, j: (i, 0)), pl.BlockSpec((cb, d), lambda i, j: (j, 0)), blk,
                  pl.BlockSpec((tb, cb), lambda i, j: (i, nj + j))],
        out_specs=[blk, blk], out_shape=[SDS((t_rows, dff), BF16)] * 2,
        compiler_params=_cp("arbitrary", "arbitrary"))(dffn, w_down, aconv, ab)


def _ffn_conv_bwd(dac, ab, w_conv9, dff):
    t_rows = dac.shape[0]
    tb = _pick(t_rows, (512,))
    cb = _pick(dff, (256, 128))
    ni, nj = t_rows // tb, dff // cb
    n_ext = tb + 2 * GRID_W
    nh64 = tb // GRID_W

    def body(dm, dp, dn, am, ap, an, wc_ref, da_ref, gw_ref):
        i = pl.program_id(1)
        d_ext = _with_halo(dp[...], dm[...], dn[...], i, ni, tb)
        a_ext = _with_halo(ap[...], am[...], an[...], i, ni, tb)
        d_left, d_right = _grid_taps(d_ext, n_ext)
        a_left, a_right = _grid_taps(a_ext, n_ext)
        dmain = d_ext[GRID_W:GRID_W + tb]
        da = jnp.zeros((tb, cb), F32)
        rows = []
        for di in range(3):
            o = (2 - di) * GRID_W
            da = da + (wc_ref[3 * di:3 * di + 1, :] * d_right[o:o + tb] + wc_ref[3 * di + 1:3 * di + 2, :] * d_ext[o:o + tb]
                       + wc_ref[3 * di + 2:3 * di + 3, :] * d_left[o:o + tb])
            o = di * GRID_W
            for tap in (a_left, a_ext, a_right):
                rows.append(jnp.sum(dmain * tap[o:o + tb], axis=0, keepdims=True))
        da_ref[...] = _bf(da)

        @pl.when(i == 0)
        def _():
            gw_ref[...] = jnp.zeros_like(gw_ref)

        gw_ref[...] += jnp.concatenate(rows + [jnp.zeros((7, cb), F32)], axis=0)

    def halo(col0):
        return [pl.BlockSpec((tb, cb), lambda j, i: (i, col0 + j)),
                pl.BlockSpec((GRID_W, cb), lambda j, i: (jnp.maximum(i * nh64 - 1, 0), col0 + j)),
                pl.BlockSpec((GRID_W, cb), lambda j, i: (jnp.minimum((i + 1) * nh64, t_rows // GRID_W - 1), col0 + j))]

    return pl.pallas_call(
        body, name="ffn_conv_bwd", grid=(nj, ni),
        in_specs=halo(0) + halo(0) + [pl.BlockSpec((16, cb), lambda j, i: (0, j))],
        out_specs=[pl.BlockSpec((tb, cb), lambda j, i: (i, j)), pl.BlockSpec((16, cb), lambda j, i: (0, j))],
        out_shape=[SDS((t_rows, dff), BF16), SDS((16, dff), F32)],
        compiler_params=_cp("arbitrary", "arbitrary"))(dac, dac, dac, ab, ab, ab, w_conv9)


def _proj_norm_bwd(pairs, x_arr, x_row0, g, scale, resid, m_rows, name):
    d = x_arr.shape[1]
    tm = _pick(m_rows, (512, 256))
    ni = m_rows // tm
    starts, total = [], 0
    for (_, _, _, _, k_p, tk_p) in pairs:
        starts.append(total)
        total += k_p // tk_p
    npairs = len(pairs)
    has_dx = resid is not None

    def body(*refs):
        a_refs, b_refs = refs[0:2 * npairs:2], refs[1:2 * npairs:2]
        rest = refs[2 * npairs:]
        if has_dx:
            x_ref, g_ref, sc_ref, r_ref, dx_ref, st_ref, acc = rest
        else:
            x_ref, g_ref, sc_ref, st_ref, acc = rest
        i, k = pl.program_id(0), pl.program_id(1)

        @pl.when(k == 0)
        def _():
            acc[...] = jnp.zeros_like(acc)

        @pl.when((i == 0) & (k == 0))
        def _():
            st_ref[...] = jnp.zeros_like(st_ref)

        for p in range(npairs):
            nk = pairs[p][4] // pairs[p][5]

            @pl.when((k >= starts[p]) & (k < starts[p] + nk))
            def _(p=p):
                acc[...] += _dot_nt(a_refs[p][...], b_refs[p][...])

        @pl.when(k == total - 1)
        def _():
            dhn = acc[...]
            x = x_ref[...]
            r = lax.rsqrt(jnp.mean(x * x, axis=-1, keepdims=True) + EPS)
            xn = x * r
            dmod = dhn * (1.0 + sc_ref[...])
            dxn = dmod * g_ref[...]
            if has_dx:
                dx_ref[...] = r * (dxn - xn * jnp.mean(dxn * xn, axis=-1, keepdims=True)) + r_ref[...]
            st_ref[...] += jnp.concatenate(
                [jnp.sum(dmod * xn, axis=0, keepdims=True), jnp.sum(dhn, axis=0, keepdims=True),
                 jnp.sum(dhn * (xn * g_ref[...]), axis=0, keepdims=True), jnp.zeros((5, d), F32)], axis=0)

    in_specs, args = [], []
    for p, (a, a_row0, b, b_col0, k_p, tk_p) in enumerate(pairs):
        nk, s0, ar, bc = k_p // tk_p, starts[p], a_row0 // tm, b_col0 // tk_p

        def kk(k, s0=s0, nk=nk):
            return jnp.clip(k - s0, 0, nk - 1)

        in_specs.append(pl.BlockSpec((tm, tk_p), lambda i, k, ar=ar, kk=kk: (ar + i, kk(k))))
        in_specs.append(pl.BlockSpec((d, tk_p), lambda i, k, bc=bc, kk=kk: (0, bc + kk(k))))
        args += [a, b]
    xr = x_row0 // tm
    in_specs += [pl.BlockSpec((tm, d), lambda i, k: (xr + i, 0)), pl.BlockSpec((1, d), lambda i, k: (0, 0)),
                 pl.BlockSpec((1, d), lambda i, k: (0, 0))]
    args += [x_arr, g, scale]
    out_specs, out_shape = [], []
    if has_dx:
        in_specs.append(pl.BlockSpec((tm, d), lambda i, k: (i, 0)))
        args.append(resid)
        out_specs.append(pl.BlockSpec((tm, d), lambda i, k: (i, 0)))
        out_shape.append(SDS((m_rows, d), F32))
    out_specs.append(pl.BlockSpec((8, d), lambda i, k: (0, 0)))
    out_shape.append(SDS((8, d), F32))
    return pl.pallas_call(
        body, name=name, grid=(ni, total), in_specs=in_specs, out_specs=out_specs, out_shape=out_shape,
        scratch_shapes=[pltpu.VMEM((tm, d), F32)], compiler_params=_cp("arbitrary", "arbitrary"))(*args)


def _wgrad(a, b, k_rows, name):
    m, n = a.shape[1], b.shape[1]
    tm = _pick(m, (1408, 1024, 512, 384, 256, 128))
    tn = _pick(n, (1408, 1024, 768, 512, 384, 256, 128))
    tk = _pick(k_rows, (1280, 1024, 256))
    nk = k_rows // tk

    def body(a_ref, b_ref, o_ref, acc):
        k = pl.program_id(2)

        @pl.when(k == 0)
        def _():
            acc[...] = jnp.zeros_like(acc)

        acc[...] += _dot_tn(a_ref[...], b_ref[...])

        @pl.when(k == nk - 1)
        def _():
            o_ref[...] = _bf(acc[...])

    return pl.pallas_call(
        body, name=name, grid=(m // tm, n // tn, nk),
        in_specs=[pl.BlockSpec((tk, tm), lambda i, j, k: (k, i)), pl.BlockSpec((tk, tn), lambda i, j, k: (k, j))],
        out_specs=pl.BlockSpec((tm, tn), lambda i, j, k: (i, j)), out_shape=SDS((m, n), BF16),
        scratch_shapes=[pltpu.VMEM((tm, tn), F32)],
        compiler_params=_cp("arbitrary", "arbitrary", "arbitrary"))(a, b)


def _lane_put(col, lane_idx):
    lane = lax.broadcasted_iota(jnp.int32, (1, LANES), 1)
    return jnp.where(lane == lane_idx, col, 0.0)


def _mixer_bwd(dh1, out, hf, hb, z_main, pm, ps, hg, lng, lnb, w_s, b_st, wbm, wbs, wout, mx2, t_rows, nh):
    d = dh1.shape[1]
    ng, sc = w_s.shape[0], w_s.shape[1]
    dh, gd = d // nh, d // ng
    tb = _pick(t_rows, (256,))

    def body(dh1_ref, out_ref, hf_ref, hb_ref, zo, zu, zv, zgm, zgg, pm_ref, ps_ref, hg_ref, lng_ref, lnb_ref, ws_ref, bs_ref,
             wbm_ref, wbs_ref, wo_ref, mx2_ref, dz_ref, dhs_ref, dout_ref, dpm_ref, dps_ref, st_ref, dws_ref, dbs_ref):
        i = pl.program_id(0)

        @pl.when(i == 0)
        def _():
            st_ref[...] = jnp.zeros_like(st_ref)
            dws_ref[...] = jnp.zeros_like(dws_ref)
            dbs_ref[...] = jnp.zeros_like(dbs_ref)

        dh1v = dh1_ref[...]
        doutb = _bf(dh1v * mx2_ref[...])
        dout_ref[...] = doutb
        d_mx2 = jnp.sum(dh1v * out_ref[...].astype(F32), axis=0, keepdims=True)
        dy = _dot_nt(doutb, wo_ref[...])
        sgm, sgg = _sigmoid(zgm[...].astype(F32)), _sigmoid(zgg[...].astype(F32))
        dpmb, dpsb = _bf(dy * sgm), _bf(dy * sgg)
        dpm_ref[...] = dpmb
        dps_ref[...] = dpsb
        dz_ref[:, 3 * d:4 * d] = _bf(dy * pm_ref[...].astype(F32) * sgm * (1.0 - sgm))
        dz_ref[:, 4 * d:5 * d] = _bf(dy * ps_ref[...].astype(F32) * sgg * (1.0 - sgg))
        dym = _dot_nt(dpmb, wbm_ref[...])
        dys = _dot_nt(dpsb, wbs_ref[...])
        hs = hf_ref[...].astype(F32) + hb_ref[...].astype(F32)
        hn, scales = _head_rms(hs, nh, dh)
        so = _sigmoid(zo[...].astype(F32))
        dz_ref[:, 0:d] = _bf(dym * (hn * hg_ref[...]) * so * (1.0 - so))
        dhmn = dym * so
        d_hg = jnp.sum(dhmn * hn, axis=0, keepdims=True)
        dhn = dhmn * hg_ref[...]
        for h in range(nh):
            sl = slice(h * dh, (h + 1) * dh)
            dhs_ref[:, sl] = _bf(scales[h] * (dhn[:, sl] - hn[:, sl] * jnp.mean(dhn[:, sl] * hn[:, sl], axis=-1, keepdims=True)))
        zuv, zvv = zu[...].astype(F32), zv[...].astype(F32)
        u = _gelu(zuv)
        vhat, rstd = _layer_norm(_gelu(zvv))
        vnb = _bf(vhat * lng_ref[...] + lnb_ref[...])
        mixed = _sgu_mix(vnb, ws_ref, bs_ref, tb, ng, gd, sc)
        dz_ref[:, d:2 * d] = _bf(dys * mixed * _gelu_grad(zuv))
        dmix = dys * u
        rows = []
        dbs = jnp.zeros((sc, LANES), F32)
        for ch in range(tb // sc):
            cols = []
            for g in range(ng):
                dm = dmix[ch * sc:(ch + 1) * sc, g * gd:(g + 1) * gd]
                dmb = _bf(dm)
                dws_ref[g] += _dot_nt(dmb, vnb[ch * sc:(ch + 1) * sc, g * gd:(g + 1) * gd])
                dbs = dbs + _lane_put(jnp.sum(dm, axis=1, keepdims=True), g)
                cols.append(_dot_tn(_bf(ws_ref[g]), dmb))
            rows.append(jnp.concatenate(cols, axis=1))
        dbs_ref[...] += dbs
        dvn = jnp.concatenate(rows, axis=0)
        d_lng = jnp.sum(dvn * vhat, axis=0, keepdims=True)
        d_lnb = jnp.sum(dvn, axis=0, keepdims=True)
        dvh = dvn * lng_ref[...]
        dvg = rstd * (dvh - jnp.mean(dvh, axis=-1, keepdims=True) - vhat * jnp.mean(dvh * vhat, axis=-1, keepdims=True))
        dz_ref[:, 2 * d:3 * d] = _bf(dvg * _gelu_grad(zvv))
        st_ref[...] += jnp.concatenate([d_mx2, d_hg, d_lng, d_lnb, jnp.zeros((4, d), F32)], axis=0)

    def tok(col):
        return pl.BlockSpec((tb, d), lambda i: (i, col))

    def full(shape):
        return pl.BlockSpec(shape, lambda i: (0,) * len(shape))

    return pl.pallas_call(
        body, name="mixer_bwd", grid=(t_rows // tb,),
        in_specs=[tok(0), tok(0), tok(0), tok(0), tok(3), tok(4), tok(5), tok(6), tok(7), tok(0), tok(0), full((1, d)),
                  full((1, d)), full((1, d)), full((ng, sc, sc)), full((sc, LANES)), full((d, d)), full((d, d)), full((d, d)),
                  full((1, d))],
        out_specs=[pl.BlockSpec((tb, 5 * d), lambda i: (i, 0)), tok(0), tok(0), tok(0), tok(0), full((8, d)), full((ng, sc, sc)),
                   full((sc, LANES))],
        out_shape=[SDS((t_rows, 5 * d), BF16)] + [SDS((t_rows, d), BF16)] * 4 + [SDS((8, d), F32), SDS((ng, sc, sc), F32),
                                                                                SDS((sc, LANES), F32)],
        compiler_params=_cp("arbitrary"))(dh1, out, hf, hb, z_main, z_main, z_main, z_main, z_main, pm, ps, hg, lng, lnb, w_s,
                                          b_st, wbm, wbs, wout, mx2)


def _mlstm_bwd(qk, z_main, zg, bias, dhs, states_f, states_b, nh, t_rows):
    s_rows = qk.shape[0]
    md = qk.shape[1] // 2
    dh = md // nh
    nc = s_rows // LCH
    nx = t_rows // LCH
    ln = LCH

    def chunk_f(i):
        return jnp.where(i == nc - 1, nc - 1, nc - 2 - i)

    def chunk_b(i):
        return jnp.where(i == nc - 1, nc - 1, i)

    def body(qf, kf, vf, gf, dhf, cf, nf, mf_, qb, kb, vb, gb, dhb, cb, nb, mb_, bias_ref, dqkvf_ref, dgf_ref, dqkvb_ref, dgb_ref,
             dc_sc, dn_sc):
        i = pl.program_id(0)
        is_ctx = i == nc - 1

        @pl.when(i == 0)
        def _():
            dc_sc[...] = jnp.zeros_like(dc_sc)
            dn_sc[...] = jnp.zeros_like(dn_sc)

        for dr, (q_ref, k_ref, v_ref, g_ref, dh_ref, c_ref, n_ref, m_ref, dqkv_ref, dg_ref) in enumerate(
                ((qf, kf, vf, gf, dhf, cf, nf, mf_, dqkvf_ref, dgf_ref), (qb, kb, vb, gb, dhb, cb, nb, mb_, dqkvb_ref, dgb_ref))):
            gz, b_all, b_t, g_t, g_all, mask, mfl = _chunk_gates(g_ref[...], bias_ref[...], dr == 1)
            x1 = jnp.zeros((ln, LANES), F32)
            x2 = jnp.zeros((ln, LANES), F32)
            dig = jnp.zeros((ln, LANES), F32)
            e_row = jnp.zeros((1, LANES), F32)
            for h in range(nh):
                ci, cfl = 2 * dr * nh + h, (2 * dr + 1) * nh + h
                sl = slice(h * dh, (h + 1) * dh)
                q, k, v = q_ref[:, sl], k_ref[:, sl], v_ref[:, sl]
                qf32, kf32 = q.astype(F32), k.astype(F32)
                dhv = jnp.where(is_ctx, 0.0, dh_ref[:, sl].astype(F32))
                c_in, n_in, m_in = c_ref[sl, :], n_ref[0:1, sl], m_ref[h, 0:1, 0:1]
                b_col, b_row, i_col, i_row = b_all[:, cfl:cfl + 1], b_t[cfl:cfl + 1, :], gz[:, ci:ci + 1], g_t[ci:ci + 1, :]
                g = g_all[:, cfl:cfl + 1]
                w, w_int, m_row = _head_weights(b_col, b_row, i_row, m_in, mask)
                s_mat = _dot_nt(q, k) * w
                sb, cb16 = _bf(s_mat), _bf(c_in)
                num = _dot(sb, v) + w_int * _dot(q, cb16)
                den = jnp.sum(s_mat, axis=1, keepdims=True) + w_int * jnp.sum(qf32 * n_in, axis=1, keepdims=True)
                e_m = jnp.exp(-m_row)
                dnm = jnp.maximum(jnp.abs(den), e_m)
                dnum = dhv / dnm
                hdh = jnp.sum((num / dnm) * dhv, axis=1, keepdims=True)
                dden = jnp.where(jnp.abs(den) > e_m, -(hdh / dnm) * jnp.sign(den), 0.0)
                dnum_b = _bf(dnum)
                ds = _dot_nt(dnum_b, v) + dden
                pb = _bf(w * ds)
                gmat = s_mat * ds
                a_old, coef, _ = _head_state_coeffs(g, b_col, i_col, m_in)
                dc_new, dn_new = dc_sc[dr, h], dn_sc[dr, h, 0:1, :]
                dcb = _bf(dc_new)
                dv = _dot_tn(sb, dnum_b) + _dot(_bf(kf32 * coef), dcb)
                dq_inter = w_int * (_dot_nt(dnum_b, cb16) + dden * n_in)
                dq = _dot(pb, k) + dq_inter
                dk_state = coef * (_dot_nt(v, dcb) + dn_new)
                dk = _dot_tn(pb, q) + dk_state
                dqkv_ref[:, sl] = _bf(dq)
                dqkv_ref[:, md + h * dh:md + (h + 1) * dh] = _bf(dk)
                dqkv_ref[:, 2 * md + h * dh:2 * md + (h + 1) * dh] = _bf(dv)
                row_intra = jnp.sum(gmat, axis=1, keepdims=True)
                col_intra = jnp.sum(gmat.T, axis=1, keepdims=True)
                row_inter = jnp.sum(qf32 * dq_inter, axis=1, keepdims=True)
                col_inter = jnp.sum(kf32 * dk_state, axis=1, keepdims=True)
                e_old = a_old * (jnp.sum(jnp.sum(dc_new * c_in, axis=1, keepdims=True), axis=0, keepdims=True)
                                 + jnp.sum(dn_new * n_in, axis=1, keepdims=True))
                x1 = x1 + _lane_put(row_intra - col_intra + row_inter, cfl)
                x2 = x2 + _lane_put(col_inter, cfl)
                e_row = e_row + _lane_put(e_old, cfl)
                dig = dig + _lane_put(col_intra + col_inter, ci)
                dc_sc[dr, h] = a_old * dc_new + _dot_tn(_bf(qf32 * w_int), dnum_b)
                dn_sc[dr, h] = jnp.broadcast_to(a_old * dn_new + jnp.sum(qf32 * (w_int * dden), axis=0, keepdims=True), (8, dh))
            dlogf = _mask_dot_t(mfl, x1) + _mask_dot(mfl, x2) - x2 + e_row
            dg_ref[...] = dig + dlogf * _sigmoid(-gz)

    def tok(cfn, col):
        return pl.BlockSpec((ln, md), lambda i: (cfn(i), col))

    def dht(cfn):
        return pl.BlockSpec((ln, md), lambda i: (jnp.minimum(cfn(i), nx - 1), 0))

    def gat(cfn):
        return pl.BlockSpec((ln, LANES), lambda i: (cfn(i), 0))

    def st(cfn, shape):
        return pl.BlockSpec((None,) + shape, lambda i: (cfn(i),) + (0,) * len(shape))

    st_shapes = ((nh * dh, dh), (8, md), (nh, 8, LANES))

    def side(cfn):
        return [tok(cfn, 0), tok(cfn, 1), tok(cfn, 2), gat(cfn), dht(cfn)] + [st(cfn, s) for s in st_shapes]

    def outs(cfn):
        return [pl.BlockSpec((ln, 3 * md), lambda i: (cfn(i), 0)), gat(cfn)]

    return pl.pallas_call(
        body, name="mlstm_bwd", grid=(nc,),
        in_specs=side(chunk_f) + side(chunk_b) + [pl.BlockSpec((1, LANES), lambda i: (0, 0))],
        out_specs=outs(chunk_f) + outs(chunk_b),
        out_shape=[SDS((s_rows, 3 * md), BF16), SDS((s_rows, LANES), F32)] * 2,
        scratch_shapes=[pltpu.VMEM((2, nh, dh, dh), F32), pltpu.VMEM((2, nh, 8, dh), F32)],
        compiler_params=_cp("arbitrary"))(qk, qk, z_main, zg, dhs, *states_f, qk, qk, z_main, zg, dhs, *states_b, bias)


def _qkv_conv_bwd(dqkv_f, dqkv_b, z_main, conv_w, t_rows, md, qscale):
    s_rows = z_main.shape[0]
    tb = _pick(s_rows, (1280, 1024, 256))
    cb = _pick(md, (512, 256, 128))
    ni, nj, ncq = s_rows // tb, 3 * md // cb, 2 * md // cb
    nb8 = tb // 8
    n_ext = tb + 16

    def body(fm, fp, fn, bm, bp, bn, zm, zp, zn, w_ref, dz_ref, gw_ref):
        j, i = pl.program_id(0), pl.program_id(1)

        @pl.when(j < ncq)
        def _():
            z = jnp.concatenate([zp[...], zm[...], zn[...]], axis=0).astype(F32)
            dqk = (jnp.concatenate([fp[...], fm[...], fn[...]], axis=0).astype(F32)
                   + jnp.concatenate([bp[...], bm[...], bn[...]], axis=0).astype(F32)) * jnp.where(j * cb < md, qscale, 1.0)
            row = i * tb - 8 + lax.broadcasted_iota(jnp.int32, (n_ext, 1), 0)
            prev_ok, next_ok = _seg_masks(row, t_rows, s_rows)
            zprev = jnp.where(prev_ok, pltpu.roll(z, 1, 0), 0.0)
            znext = jnp.where(next_ok, pltpu.roll(z, n_ext - 1, 0), 0.0)
            pre = w_ref[0:1, :] * zprev + w_ref[1:2, :] * z + w_ref[2:3, :] * znext
            sg = _sigmoid(pre)
            dpre = dqk * (sg * (1.0 + pre * (1.0 - sg)))
            dz = (w_ref[1:2, :] * dpre + w_ref[0:1, :] * jnp.where(next_ok, pltpu.roll(dpre, n_ext - 1, 0), 0.0)
                  + w_ref[2:3, :] * jnp.where(prev_ok, pltpu.roll(dpre, 1, 0), 0.0))
            dz_ref[...] = _bf(dz[8:8 + tb])
            dm = dpre[8:8 + tb]

            @pl.when(i == 0)
            def _():
                gw_ref[...] = jnp.zeros_like(gw_ref)

            gw_ref[...] += jnp.concatenate(
                [jnp.sum(dm * zprev[8:8 + tb], axis=0, keepdims=True), jnp.sum(dm * z[8:8 + tb], axis=0, keepdims=True),
                 jnp.sum(dm * znext[8:8 + tb], axis=0, keepdims=True), jnp.zeros((5, cb), F32)], axis=0)

        @pl.when(j >= ncq)
        def _():
            dz_ref[...] = _bf(fm[...].astype(F32) + bm[...].astype(F32))

    def halo(clampj):
        def cj(j):
            return jnp.minimum(j, ncq - 1) if clampj else j
        return [pl.BlockSpec((tb, cb), lambda j, i: (i, cj(j))),
                pl.BlockSpec((8, cb), lambda j, i: (jnp.maximum(i * nb8 - 1, 0), cj(j))),
                pl.BlockSpec((8, cb), lambda j, i: (jnp.minimum((i + 1) * nb8, s_rows // 8 - 1), cj(j)))]

    return pl.pallas_call(
        body, name="qkv_conv_bwd", grid=(nj, ni),
        in_specs=halo(False) + halo(False) + halo(True) + [pl.BlockSpec((8, cb), lambda j, i: (0, jnp.minimum(j, ncq - 1)))],
        out_specs=[pl.BlockSpec((tb, cb), lambda j, i: (i, j)), pl.BlockSpec((8, cb), lambda j, i: (0, jnp.minimum(j, ncq - 1)))],
        out_shape=[SDS((s_rows, 3 * md), BF16), SDS((8, 2 * md), F32)],
        compiler_params=_cp("arbitrary", "arbitrary"))(dqkv_f, dqkv_f, dqkv_f, dqkv_b, dqkv_b, dqkv_b, z_main, z_main, z_main, conv_w)


def _gate_grad_sum(dg_f, dg_b):
    s_rows = dg_f.shape[0]
    tb = _pick(s_rows, (1280, 1024, 256))

    def body(a_ref, b_ref, o_ref, st_ref):
        @pl.when(pl.program_id(0) == 0)
        def _():
            st_ref[...] = jnp.zeros_like(st_ref)

        s = a_ref[...] + b_ref[...]
        o_ref[...] = _bf(s)
        st_ref[...] += jnp.concatenate([jnp.sum(s, axis=0, keepdims=True), jnp.zeros((7, LANES), F32)], axis=0)

    blk = pl.BlockSpec((tb, LANES), lambda i: (i, 0))
    return pl.pallas_call(
        body, name="gate_grad_sum", grid=(s_rows // tb,), in_specs=[blk, blk],
        out_specs=[blk, pl.BlockSpec((8, LANES), lambda i: (0, 0))],
        out_shape=[SDS((s_rows, LANES), BF16), SDS((8, LANES), F32)], compiler_params=_cp("arbitrary"))(dg_f, dg_b)


def _mod_grads(silu_slots, dmx_sh, dmx_slots, dmc_tot, dmc_sh, silu_cctx, c_ctx, w_mod_c):
    d = silu_slots.shape[1]
    ncol, n6 = dmx_sh.shape[1], dmx_slots.shape[1]

    def body(ss_ref, dsh_ref, dsl_ref, dct_ref, dcs_ref, sc_ref, c_ref, w_ref, gw_ref, gb_ref, gc_ref):
        a = jnp.concatenate([ss_ref[...], sc_ref[...], jnp.zeros((7, d), F32)], axis=0)
        b = jnp.concatenate([dsh_ref[...], dcs_ref[...], jnp.zeros((7, ncol), F32)], axis=0)
        gw_ref[...] = lax.dot_general(a, b, (((0,), (0,)), ((), ())), preferred_element_type=F32, precision=HI)
        dct = dct_ref[...]
        gb_ref[...] = jnp.sum(dsl_ref[...], axis=0, keepdims=True) + jnp.concatenate(
            [dct, jnp.zeros((1, n6 - dct.shape[1]), F32)], axis=1)
        t = _dot_nt(_bf(jnp.broadcast_to(dct, (8, dct.shape[1]))), w_ref[...])
        cv = c_ref[...]
        s = _sigmoid(cv)
        gc_ref[...] = t[0:1, :] * (s * (1.0 + cv * (1.0 - s)))

    return pl.pallas_call(body, name="mod_grads", out_shape=[SDS((d, ncol), F32), SDS((1, n6), F32), SDS((1, d), F32)],
                          compiler_params=_cp())(silu_slots, dmx_sh, dmx_slots, dmc_tot, dmc_sh, silu_cctx, c_ctx, w_mod_c)


def _slot_sum(slots):
    ns, r = slots.shape[0], slots.shape[1]
    tb = _pick(r, (1024, 512, 256, 128, 64, 32, 16, 8))

    def body(s_ref, o_ref):
        acc = s_ref[0]
        for k in range(1, ns):
            acc = acc + s_ref[k]
        o_ref[...] = acc

    return pl.pallas_call(
        body, name="slot_sum", grid=(r // tb,), in_specs=[pl.BlockSpec((ns, tb, LANES), lambda i: (0, i, 0))],
        out_specs=pl.BlockSpec((tb, LANES), lambda i: (i, 0)), out_shape=SDS((r, LANES), F32),
        compiler_params=_cp("arbitrary"))(slots)


def _adamw(w, gslots, m, v, name):
    r, cdim = w.shape
    ns, rg = gslots.shape[0], gslots.shape[1]
    tb = r if (rg != r or r % 8) else _pick(r, (128, 64, 32, 16, 8))
    bc1, bc2 = 1.0 - ADAM_B1 ** ADAM_STEP, 1.0 - ADAM_B2 ** ADAM_STEP

    def body(w_ref, g_ref, m_ref, v_ref, go_ref, d_ref, mo_ref, vo_ref):
        g = g_ref[0, 0:tb, :].astype(F32)
        for k in range(1, ns):
            g = g + g_ref[k, 0:tb, :].astype(F32)
        mn = ADAM_B1 * m_ref[...] + (1.0 - ADAM_B1) * g
        vn = ADAM_B2 * v_ref[...] + (1.0 - ADAM_B2) * (g * g)
        go_ref[...] = g
        mo_ref[...] = mn
        vo_ref[...] = vn
        d_ref[...] = -ADAM_LR * ((mn / bc1) / (jnp.sqrt(vn / bc2) + ADAM_EPS) + ADAM_WD * w_ref[...])

    blk = pl.BlockSpec((tb, cdim), lambda i: (i, 0))
    gblk = pl.BlockSpec((ns, tb if rg == r else rg, cdim), lambda i: (0, i, 0))
    return pl.pallas_call(
        body, name=name, grid=(r // tb,), in_specs=[blk, gblk, blk, blk],
        out_specs=[blk] * 4, out_shape=[SDS((r, cdim), F32)] * 4, compiler_params=_cp("arbitrary"))(w, gslots, m, v)


def _pack(parts, row_mult):
    flat = jnp.concatenate([p.reshape(-1) for p in parts])
    n = flat.shape[0]
    rows = -(-n // LANES)
    rows = -(-rows // row_mult) * row_mult
    return jnp.pad(flat, (0, rows * LANES - n)).reshape(rows, LANES)


def _unpack(buf, shapes):
    flat = buf.reshape(-1)
    out, off = [], 0
    for s in shapes:
        n = math.prod(s)
        out.append(flat[off:off + n].reshape(s))
        off += n
    return out


def _pad_cols(a, width):
    return jnp.pad(a, ((0, 0), (0, width - a.shape[1])))


def _pad_lanes(a):
    return _pad_cols(a, LANES)


def _up128(n):
    return -(-n // LANES) * LANES


def kernel(x, c, ctx, c_ctx, w_mod, b_mod, norm1_g, w_in, b_gate, conv_qk, head_norm_g, sgu_ln_g, sgu_ln_b, w_s, b_s, w_branch_mlstm, w_branch_sgu, w_out, norm2_g, w_up, w_ffn_conv, w_down, final_g, loss_target, m_c_ctx, m_w_mod, m_b_mod, m_norm1_g, m_w_in, m_b_gate, m_conv_qk, m_head_norm_g, m_sgu_ln_g, m_sgu_ln_b, m_w_s, m_b_s, m_w_branch_mlstm, m_w_branch_sgu, m_w_out, m_norm2_g, m_w_up, m_w_ffn_conv, m_w_down, m_final_g, v_c_ctx, v_w_mod, v_b_mod, v_norm1_g, v_w_in, v_b_gate, v_conv_qk, v_head_norm_g, v_sgu_ln_g, v_sgu_ln_b, v_w_s, v_b_s, v_w_branch_mlstm, v_w_branch_sgu, v_w_out, v_norm2_g, v_w_up, v_w_ffn_conv, v_w_down, v_final_g):
    t, d = x.shape[1], x.shape[2]
    n_ctx = ctx.shape[1]
    s_rows = t + n_ctx
    nh = b_gate.shape[1] // 4
    md = head_norm_g.shape[1]
    dh = md // nh
    ng, sc = w_s.shape[1], w_s.shape[2]
    dff = w_down.shape[1] * N_DEV
    n_in = w_in.shape[2] * N_DEV
    assert md == d and sgu_ln_g.shape[1] == d and n_ctx == LCH and t % LCH == 0 and t % (8 * GRID_W) == 0
    assert n_in == 8 * d + 4 * nh and 4 * nh <= LANES
    me = 4 * lax.axis_index("x") + 2 * lax.axis_index("y") + lax.axis_index("c")

    n_mod, n_insh, n_upsh = w_mod.shape[2], w_in.shape[2], w_up.shape[2]
    p_mod, p_in, p_up = _up128(n_mod), _up128(n_insh), _up128(n_upsh)
    nq, nf = conv_qk.shape[2], w_ffn_conv.shape[3]
    ffn9 = w_ffn_conv[0].reshape(9, nf)
    colpack = jnp.concatenate([_pad_cols(_bf(w_mod[0]), p_mod), _pad_cols(_bf(w_in[0]), p_in), _pad_cols(_bf(w_up[0]), p_up)], axis=1)
    convpack = jnp.concatenate([jnp.pad(conv_qk[0], ((0, 13), (0, 0))), jnp.pad(ffn9, ((0, 7), (0, 0)))], axis=1)
    g_col, g_bm, g_bs, g_out, g_down, g_conv = _allgather(
        [colpack, _bf(w_branch_mlstm[0]), _bf(w_branch_sgu[0]), _bf(w_out[0]), _bf(w_down[0]), convpack])
    w_mod_f, w_main, w_gate, w_up_f = _assemble_cols(
        g_col, [(0, n_mod, [(0, 0, N_DEV * n_mod, 0)]),
                (p_mod, n_insh, [(1, 0, 3 * md, 0), (2, 3 * md, 4 * nh, 0), (1, 3 * md + 4 * nh, 5 * d, 3 * md)]),
                (p_mod + p_in, n_upsh, [(3, 0, 2 * dff, 0)])],
        [N_MOD * d, 8 * d, LANES, 2 * dff], "assemble_weights")
    wbm_f, wbs_f, wout_f = (g.reshape(d, d) for g in (g_bm, g_bs, g_out))
    w_down_f = g_down.reshape(dff, d)
    convw, wconv9 = _assemble_cols(g_conv, [(0, nq, [(0, 0, N_DEV * nq, 0)]), (nq, nf, [(1, 0, N_DEV * nf, 0)])],
                                   [N_DEV * nq, N_DEV * nf], "assemble_conv_weights")

    cvec = jnp.concatenate([c, c_ctx[None], jnp.zeros((6, d), F32)], axis=0)
    silu_v, mod = _modulation(cvec, w_mod_f, b_mod)
    mx = [mod[0:1, k * d:(k + 1) * d] for k in range(N_MOD)]
    mc = [mod[1:2, k * d:(k + 1) * d] for k in range(2)]
    xs = jnp.concatenate([x[0], ctx[0]], axis=0)
    hn, z_main, zg = _norm_mod_proj(xs, norm1_g, jnp.concatenate([mx[0], mx[1], mc[0], mc[1]], axis=0), w_main, w_gate, t, "in_proj")
    qscale = dh ** -0.5
    qk = _qk_conv(z_main, convw, t, md, qscale)
    bias = _pad_lanes(b_gate)
    fwd = _mlstm_fwd(qk, z_main, zg, bias, nh)
    hf, hb, states_f, states_b = fwd[0], fwd[1], fwd[2:5], fwd[5:8]
    b_st = _pad_lanes(b_s[0].T)
    h1, ym, ys, pm, ps, y, out = _mixer_fwd(hf, hb, z_main, xs, head_norm_g, sgu_ln_g, sgu_ln_b, w_s[0], b_st, wbm_f, wbs_f,
                                            wout_f, mx[2], t, nh)
    hn2, ab = _norm_mod_proj(h1, norm2_g, jnp.concatenate([mx[3], mx[4], mx[3], mx[4]], axis=0), w_up_f, None, t, "up_proj")
    aconv, f, dh2, dffn, st_tail = _ffn_tail(ab, wconv9, w_down_f, h1, mx[5], final_g[None], loss_target[0], dff)

    db, dac = _ffn_bwd_gate(dffn, w_down_f, aconv, ab, dff)
    da, g_wconv9 = _ffn_conv_bwd(dac, ab, wconv9, dff)
    g_wdown = _wgrad(f, dffn, t, "wgrad_down")
    gwup_slots = _scatter_cols([_wgrad(hn2, da, t, "wgrad_up_a"), _wgrad(hn2, db, t, "wgrad_up_b")],
                               [(0, 0, dff, 0), (1, dff, dff, 0)], n_upsh, "scatter_grad_w_up")
    tkf = _pick(dff, (1408, 704, 384, 128))
    dh1, st_n2 = _proj_norm_bwd([(da, 0, w_up_f, 0, dff, tkf), (db, 0, w_up_f, dff, dff, tkf)], h1, 0, norm2_g, mx[4], dh2, t,
                                "up_proj_bwd")
    dz_rest, dhs, dout, dpm, dps, st_mix, g_ws, g_bst = _mixer_bwd(dh1, out, hf, hb, z_main, pm, ps, head_norm_g, sgu_ln_g,
                                                                    sgu_ln_b, w_s[0], b_st, wbm_f, wbs_f, wout_f, mx[2], t, nh)
    g_wout = _wgrad(y, dout, t, "wgrad_out")
    g_wbm = _wgrad(ym, dpm, t, "wgrad_branch_mlstm")
    g_wbs = _wgrad(ys, dps, t, "wgrad_branch_sgu")
    dqkv_f, dg_f, dqkv_b, dg_b = _mlstm_bwd(qk, z_main, zg, bias, dhs, states_f, states_b, nh, t)
    dz_qkv, g_convqk = _qkv_conv_bwd(dqkv_f, dqkv_b, z_main, convw, t, md, qscale)
    dz_g, st_gate = _gate_grad_sum(dg_f, dg_b)
    gwin_slots = _scatter_cols(
        [_wgrad(hn, dz_qkv, s_rows, "wgrad_in_qkv"), _wgrad(hn, dz_g, s_rows, "wgrad_in_gate"), _wgrad(hn, dz_rest, t, "wgrad_in_rest")],
        [(0, 0, 3 * md, 0), (1, 3 * md, 4 * nh, 0), (2, 3 * md + 4 * nh, 5 * d, 0)], n_insh, "scatter_grad_w_in")
    tk = _pick(md, (1024, 512, 256))
    grad_x, st_n1x = _proj_norm_bwd(
        [(dz_qkv, 0, w_main, 0, 3 * md, tk), (dz_rest, 0, w_main, 3 * md, 5 * d, tk), (dz_g, 0, w_gate, 0, LANES, LANES)],
        xs, 0, norm1_g, mx[1], dh1, t, "in_proj_bwd")
    (st_n1c,) = _proj_norm_bwd([(dz_qkv, t, w_main, 0, 3 * md, tk), (dz_g, t, w_gate, 0, LANES, LANES)],
                               xs, t, norm1_g, mc[1], None, n_ctx, "in_proj_bwd_ctx")

    gcq_slots = _scatter_cols([g_convqk], [(0, 0, 2 * md, 0)], nq, "scatter_grad_conv_qk")
    gcf_slots = _scatter_cols([g_wconv9], [(0, 0, dff, 0)], nf, "scatter_grad_ffn_conv")
    per_dest = [gwin_slots, gwup_slots, g_wbm.reshape(N_DEV, d // N_DEV, d), g_wbs.reshape(N_DEV, d // N_DEV, d),
                g_wout.reshape(N_DEV, d // N_DEV, d), g_wdown.reshape(N_DEV, dff // N_DEV, d), gcq_slots, gcf_slots]
    small_parts = [st_n1x[1], st_n1x[2], st_mix[0], st_n2[1], st_n2[2], st_tail[1],
                   st_n1c[1], st_n1c[2],
                   silu_v[0], st_n1x[0] + st_n1c[0], st_gate[0], st_mix[1], st_mix[2], st_mix[3],
                   g_ws.reshape(-1), g_bst[:, :ng].T.reshape(-1), st_n2[0], st_tail[0]]
    gsmall = _pack(small_parts, 8)
    recv = _grad_exchange(per_dest, [gsmall])
    recv_small = recv[-1]
    small_sum = _slot_sum(recv_small).reshape(-1)
    small_slots = recv_small.reshape(N_DEV, -1)
    o_silu, o_n1 = 8 * d, 9 * d
    ncol = N_MOD * d // N_DEV
    dmc_tot = small_sum[6 * d:8 * d][None]
    dmc_pad = jnp.concatenate([dmc_tot, jnp.zeros((1, 4 * d), F32)], axis=1)
    g_wmod, g_bmod, g_cctx = _mod_grads(
        small_slots[:, o_silu:o_silu + d], lax.dynamic_slice_in_dim(small_slots[:, :6 * d], me * ncol, ncol, axis=1),
        small_slots[:, :6 * d], dmc_tot, lax.dynamic_slice_in_dim(dmc_pad, me * ncol, ncol, axis=1), silu_v[1:2], c_ctx[None],
        w_mod_f[:, :2 * d])

    shard_w = (w_in, w_up, w_branch_mlstm, w_branch_sgu, w_out, w_down, conv_qk)
    shard_m = (m_w_in, m_w_up, m_w_branch_mlstm, m_w_branch_sgu, m_w_out, m_w_down, m_conv_qk)
    shard_v = (v_w_in, v_w_up, v_w_branch_mlstm, v_w_branch_sgu, v_w_out, v_w_down, v_conv_qk)
    shard_names = ("w_in", "w_up", "w_branch_mlstm", "w_branch_sgu", "w_out", "w_down", "conv_qk")
    shard_out = [[b[None] for b in _adamw(wa[0], recv[k], ma[0], va[0], "adamw_" + nm)]
                 for k, (wa, ma, va, nm) in enumerate(zip(shard_w, shard_m, shard_v, shard_names))]
    shard_out.append([b.reshape(w_ffn_conv.shape) for b in
                      _adamw(ffn9, recv[7], m_w_ffn_conv[0].reshape(9, nf), v_w_ffn_conv[0].reshape(9, nf), "adamw_w_ffn_conv")])
    mod_out = [b[None] for b in _adamw(w_mod[0], g_wmod[None], m_w_mod[0], v_w_mod[0], "adamw_w_mod")]

    def rep(cc, bm, n1, bg, hg, lg, lb, ws, bs, n2, fg):
        return [cc.reshape(-1), bm.reshape(-1), n1.reshape(-1), _pad_lanes(bg.reshape(1, -1)).reshape(-1), hg.reshape(-1),
                lg.reshape(-1), lb.reshape(-1), ws.reshape(-1), bs.reshape(-1), n2.reshape(-1), fg.reshape(-1)]

    o = o_n1
    g_rep_parts = [g_cctx, g_bmod]
    for n in (d, LANES, d, d, d, ng * sc * sc, ng * sc, d, d):
        g_rep_parts.append(small_sum[o:o + n])
        o += n
    rep_shapes = [(d,), (1, N_MOD * d), (1, d), (1, LANES), (1, d), (1, d), (1, d), (1, ng, sc, sc), (1, ng, sc), (1, d), (d,)]
    rep_out = _adamw(
        _pack(rep(c_ctx, b_mod, norm1_g, b_gate, head_norm_g, sgu_ln_g, sgu_ln_b, w_s, b_s, norm2_g, final_g), 8),
        _pack(g_rep_parts, 8)[None],
        _pack(rep(m_c_ctx, m_b_mod, m_norm1_g, m_b_gate, m_head_norm_g, m_sgu_ln_g, m_sgu_ln_b, m_w_s, m_b_s, m_norm2_g, m_final_g), 8),
        _pack(rep(v_c_ctx, v_b_mod, v_norm1_g, v_b_gate, v_head_norm_g, v_sgu_ln_g, v_sgu_ln_b, v_w_s, v_b_s, v_norm2_g, v_final_g), 8),
        "adamw_replicated")

    def assemble(k):
        r = _unpack(rep_out[k], rep_shapes)
        s = [o[k] for o in shard_out]
        return [r[0], mod_out[k], r[1], r[2], s[0], r[3][:, :4 * nh], s[6], r[4], r[5], r[6], r[7], r[8], s[2], s[3], s[4], r[9],
                s[1], s[7], s[5], r[10]]

    loss = lax.psum(st_tail[2, 0], ("x", "y", "c"))
    outs = [loss, grad_x[None]]
    for k in range(4):
        outs += assemble(k)
    return tuple(outs)
```

```python
import functools
import math

import jax
import jax.numpy as jnp
from jax import lax
from jax.experimental import pallas as pl
from jax.experimental.pallas import tpu as pltpu

F32, BF16 = jnp.float32, jnp.bfloat16
EPS = 1e-6
M_INIT = -1e30
NEG = -1e30
GRID_W = 64
LCH = 256
N_MOD = 6
N_DEV = 8
LANES = 128
ADAM_LR, ADAM_B1, ADAM_B2, ADAM_EPS, ADAM_WD, ADAM_STEP = 0.001, 0.9, 0.999, 1e-08, 0.01, 10
GELU_C = math.sqrt(2.0 / math.pi)
GELU_A = 0.044715
VMEM_LIMIT = 56 * 1024 * 1024
HI = lax.Precision.HIGHEST
SDS = jax.ShapeDtypeStruct
MESH_ID = pl.DeviceIdType.MESH


def _pick(n, cands):
    for c in cands:
        if n % c == 0:
            return c
    raise ValueError(f"no block size for {n} in {cands}")


def _cp(*sem):
    return pltpu.CompilerParams(dimension_semantics=sem if sem else None, vmem_limit_bytes=VMEM_LIMIT)


def _sigmoid(x):
    return 0.5 * jnp.tanh(0.5 * x) + 0.5


def _split3(x):
    hi = x.astype(BF16)
    r = x - hi.astype(F32)
    mid = r.astype(BF16)
    return hi, mid, (r - mid.astype(F32)).astype(BF16)


def _mask_dot(mask_b, x):
    hi, mid, lo = _split3(x)
    return (_dot(mask_b, lo) + _dot(mask_b, mid)) + _dot(mask_b, hi)


def _mask_dot_t(mask_b, x):
    hi, mid, lo = _split3(x)
    return (_dot_tn(mask_b, lo) + _dot_tn(mask_b, mid)) + _dot_tn(mask_b, hi)


def _gelu(x):
    return 0.5 * x * (1.0 + jnp.tanh(GELU_C * (x + GELU_A * x * x * x)))


def _gelu_grad(x):
    t = jnp.tanh(GELU_C * (x + GELU_A * x * x * x))
    return 0.5 * (1.0 + t) + 0.5 * x * (1.0 - t * t) * GELU_C * (1.0 + 3.0 * GELU_A * x * x)


def _log_sigmoid(x):
    return jnp.minimum(x, 0.0) - jnp.log(1.0 + jnp.exp(-jnp.abs(x)))


def _dot(a, b):
    return jnp.dot(a, b, preferred_element_type=F32)


def _dot_nt(a, b):
    return lax.dot_general(a, b, (((1,), (1,)), ((), ())), preferred_element_type=F32)


def _dot_tn(a, b):
    return lax.dot_general(a, b, (((0,), (0,)), ((), ())), preferred_element_type=F32)


def _bf(x):
    return x.astype(BF16)


def _allgather(arrs):
    na = len(arrs)

    def body(*refs):
        x_refs, o_refs = refs[:na], refs[na:2 * na]
        send_sems, recv_sems, local_sems = refs[2 * na:]
        x, y, c = lax.axis_index("x"), lax.axis_index("y"), lax.axis_index("c")
        me, sibling = (x, y, c), (x, y, 1 - c)
        chips = [(1 - x, y), (x, 1 - y), (1 - x, 1 - y)]

        def copy(a, k, block, to, src=None):
            slot = o_refs[a].at[4 * block[0] + 2 * block[1] + block[2]]
            return pltpu.make_async_remote_copy(
                src_ref=slot if src is None else src, dst_ref=slot, send_sem=send_sems.at[7 * a + k],
                recv_sem=recv_sems.at[7 * a + k], device_id=to, device_id_type=MESH_ID)

        mine = [pltpu.make_async_copy(x_refs[a], o_refs[a].at[4 * x + 2 * y + c], local_sems.at[a]) for a in range(na)]
        for cp in mine:
            cp.start()
        first = []
        for a in range(na):
            first.append(copy(a, 0, me, sibling, src=x_refs[a]))
            first += [copy(a, 1 + j, me, (*chip, c), src=x_refs[a]) for j, chip in enumerate(chips)]
        for cp in first:
            cp.start()
        passed = []
        for j, chip in enumerate(chips):
            for a in range(na):
                copy(a, 1 + j, (*chip, c), me).wait_recv()
                passed.append(copy(a, 4 + j, (*chip, c), sibling))
                passed[-1].start()
        for a in range(na):
            copy(a, 0, sibling, me).wait_recv()
            for j, chip in enumerate(chips):
                copy(a, 4 + j, (*chip, 1 - c), me).wait_recv()
        for cp in first + passed:
            cp.wait_send()
        for cp in mine:
            cp.wait()

    anyspec = pl.BlockSpec(memory_space=pl.ANY)
    return pl.pallas_call(
        body, name="weights_allgather",
        out_shape=[SDS((N_DEV,) + a.shape, a.dtype) for a in arrs],
        in_specs=[anyspec] * na, out_specs=[anyspec] * na,
        scratch_shapes=[pltpu.SemaphoreType.DMA((7 * na,)), pltpu.SemaphoreType.DMA((7 * na,)), pltpu.SemaphoreType.DMA((na,))],
    )(*arrs)


def _grad_exchange(per_dest, shared):
    nd, ns = len(per_dest), len(shared)
    na = nd + ns

    def body(*refs):
        in_refs, out_refs = refs[:na], refs[na:2 * na]
        send_sems, recv_sems, local_sems = refs[2 * na:]
        x, y, c = lax.axis_index("x"), lax.axis_index("y"), lax.axis_index("c")
        me = 4 * x + 2 * y + c

        def src(a, idx):
            return in_refs[a].at[idx] if a < nd else in_refs[a]

        loc = [pltpu.make_async_copy(src(a, me), out_refs[a].at[me], local_sems.at[a]) for a in range(na)]
        for cp in loc:
            cp.start()
        sends, recvs = [], []
        for k in range(1, N_DEV):
            px = 1 - x if k & 4 else x
            py = 1 - y if k & 2 else y
            pc = 1 - c if k & 1 else c
            peer, pidx = (px, py, pc), 4 * px + 2 * py + pc
            for a in range(na):
                sem = 7 * a + k - 1
                sends.append(pltpu.make_async_remote_copy(
                    src_ref=src(a, pidx), dst_ref=out_refs[a].at[me], send_sem=send_sems.at[sem],
                    recv_sem=recv_sems.at[sem], device_id=peer, device_id_type=MESH_ID))
                recvs.append(pltpu.make_async_remote_copy(
                    src_ref=src(a, pidx), dst_ref=out_refs[a].at[pidx], send_sem=send_sems.at[sem],
                    recv_sem=recv_sems.at[sem], device_id=peer, device_id_type=MESH_ID))
        for cp in sends:
            cp.start()
        for cp in recvs:
            cp.wait_recv()
        for cp in sends:
            cp.wait_send()
        for cp in loc:
            cp.wait()

    anyspec = pl.BlockSpec(memory_space=pl.ANY)
    return pl.pallas_call(
        body, name="grad_exchange",
        out_shape=[SDS(a.shape, a.dtype) for a in per_dest] + [SDS((N_DEV,) + a.shape, a.dtype) for a in shared],
        in_specs=[anyspec] * na, out_specs=[anyspec] * na,
        scratch_shapes=[pltpu.SemaphoreType.DMA((7 * na,)), pltpu.SemaphoreType.DMA((7 * na,)), pltpu.SemaphoreType.DMA((na,))],
    )(*per_dest, *shared)


_HBM_SPEC = pl.BlockSpec(memory_space=pltpu.HBM)
_SEM_SPEC = pl.BlockSpec(memory_space=pltpu.SEMAPHORE)
_EFFECT = pltpu.SideEffectType.DATAFLOW_SIDE_EFFECTING


def _peer_list(x, y, c):
    out = []
    for k in range(1, N_DEV):
        px = 1 - x if k & 4 else x
        py = 1 - y if k & 2 else y
        pc = 1 - c if k & 1 else c
        out.append(((px, py, pc), 4 * px + 2 * py + pc))
    return out


def _split_copies(src, land, send_sems, recv_sems, per_dest, receive):
    x, y, c = lax.axis_index("x"), lax.axis_index("y"), lax.axis_index("c")
    me = 4 * x + 2 * y + c
    out = []
    for k, (peer, pidx) in enumerate(_peer_list(x, y, c)):
        for a in range(len(src)):
            out.append(pltpu.make_async_remote_copy(
                src_ref=src[a].at[pidx] if per_dest else src[a], dst_ref=land[a].at[pidx if receive else me],
                send_sem=send_sems.at[7 * a + k], recv_sem=recv_sems.at[7 * a + k], device_id=peer, device_id_type=MESH_ID))
    return out


def _exchange_start(arrs, per_dest, name):
    na = len(arrs)
    land_shapes = [a.shape if per_dest else (N_DEV,) + a.shape for a in arrs]
    lands = [pltpu.with_memory_space_constraint(lax.empty(s, a.dtype), pltpu.HBM) for s, a in zip(land_shapes, arrs)]

    def body(*refs):
        src, land = refs[:na], refs[na:2 * na]
        send_sems, recv_sems, token = refs[2 * na], refs[2 * na + 1], refs[-1]
        for cp in _split_copies(src, land, send_sems, recv_sems, per_dest, False):
            cp.start()
        token[...] = jnp.zeros_like(token)

    outs = pl.pallas_call(
        body, name=name,
        out_shape=[pltpu.SemaphoreType.DMA((7 * na,)), pltpu.SemaphoreType.DMA((7 * na,))]
        + [pltpu.HBM(a.shape, a.dtype) for a in arrs] + [pltpu.HBM(s, a.dtype) for s, a in zip(land_shapes, arrs)]
        + [SDS((8, LANES), F32)],
        in_specs=[_HBM_SPEC] * (2 * na),
        out_specs=[_SEM_SPEC, _SEM_SPEC] + [_HBM_SPEC] * (2 * na) + [pl.BlockSpec(memory_space=pltpu.VMEM)],
        input_output_aliases={k: 2 + k for k in range(2 * na)},
        compiler_params=pltpu.CompilerParams(has_side_effects=_EFFECT),
    )(*[pltpu.with_memory_space_constraint(a, pltpu.HBM) for a in arrs], *lands)
    return (na, per_dest, outs[:-1]), outs[-1]


def _exchange_wait(state, after, name):
    na, per_dest, started = state

    def body(*refs):
        src, land = refs[:na], refs[na:2 * na]
        send_sems, recv_sems = refs[2 * na], refs[2 * na + 1]
        for cp in _split_copies(src, land, send_sems, recv_sems, per_dest, True):
            cp.wait_send()
            cp.wait_recv()

    bufs = started[2:]
    outs = pl.pallas_call(
        body, name=name,
        out_shape=[pltpu.HBM(b.shape, b.dtype) for b in bufs],
        in_specs=[_HBM_SPEC] * (2 * na) + [_SEM_SPEC, _SEM_SPEC, pl.BlockSpec(memory_space=pl.ANY)],
        out_specs=[_HBM_SPEC] * (2 * na),
        input_output_aliases={k: k for k in range(2 * na)},
        compiler_params=pltpu.CompilerParams(has_side_effects=_EFFECT),
    )(*bufs, started[0], started[1], after)
    return outs[:na], outs[na:]


def _fill_own(land, own, me):
    return lax.dynamic_update_index_in_dim(land, own, me, 0)


def _col_pieces(n, segments):
    out = []
    for j in range(N_DEV):
        lo, hi = j * n, (j + 1) * n
        for (k, s0, w, c0) in segments:
            a, b = max(lo, s0), min(hi, s0 + w)
            if a < b:
                out.append((j, a - lo, b - lo, k, c0 + a - s0, c0 + b - s0))
    return out


def _assemble_cols(slots, groups, out_widths, name):
    r, p = slots.shape[1], slots.shape[2]
    tb = _pick(r, (128, 64, 32, 16, 8))
    covered = [0] * len(out_widths)
    for (_, n, segs) in groups:
        for (k, _, w, _) in segs:
            covered[k] += w

    def body(s_ref, *o_refs):
        for k, wd in enumerate(out_widths):
            if covered[k] < wd:
                o_refs[k][...] = jnp.zeros_like(o_refs[k])
        for (off, n, segs) in groups:
            for (j, a0, a1, k, d0, d1) in _col_pieces(n, segs):
                o_refs[k][:, d0:d1] = s_ref[j, :, off + a0:off + a1]

    return pl.pallas_call(
        body, name=name, grid=(r // tb,), in_specs=[pl.BlockSpec((N_DEV, tb, p), lambda i: (0, i, 0))],
        out_specs=[pl.BlockSpec((tb, w), lambda i: (i, 0)) for w in out_widths],
        out_shape=[SDS((r, w), slots.dtype) for w in out_widths], compiler_params=_cp("arbitrary"))(slots)


def _scatter_cols(pieces, segments, n, name):
    r = pieces[0].shape[0]
    tb = _pick(r, (128, 64, 32, 16, 8))

    def body(*refs):
        p_refs, o_ref = refs[:-1], refs[-1]
        for (j, a0, a1, k, d0, d1) in _col_pieces(n, segments):
            o_ref[j, :, a0:a1] = p_refs[k][:, d0:d1]

    return pl.pallas_call(
        body, name=name, grid=(r // tb,), in_specs=[pl.BlockSpec((tb, a.shape[1]), lambda i: (i, 0)) for a in pieces],
        out_specs=pl.BlockSpec((N_DEV, tb, n), lambda i: (0, i, 0)), out_shape=SDS((N_DEV, r, n), pieces[0].dtype),
        compiler_params=_cp("arbitrary"))(*pieces)


def _modulation(cvec, w_mod, b_mod):
    d, n = w_mod.shape

    def body(c_ref, w_ref, b_ref, s_ref, o_ref):
        cv = c_ref[...]
        s = cv * _sigmoid(cv)
        s_ref[...] = s
        o_ref[...] = _dot(_bf(s), w_ref[...]) + b_ref[...]

    return pl.pallas_call(body, name="modulation", out_shape=(SDS((8, d), F32), SDS((8, n), F32)),
                          compiler_params=_cp())(cvec, w_mod, b_mod)


def _norm_mod_proj(xs, g, shsc, w_main, w_gate, t_rows, name):
    s_rows, d = xs.shape
    n = w_main.shape[1]
    tb = _pick(s_rows, (1280, 1024, 256))
    cb = _pick(n, (1408, 1024, 768, 512, 384, 256, 128))
    gate = w_gate is not None

    def body(*refs):
        if gate:
            x_ref, g_ref, ss_ref, wm_ref, wg_ref, hn_ref, z_ref, zg_ref, hn_sc = refs
        else:
            x_ref, g_ref, ss_ref, wm_ref, hn_ref, z_ref, hn_sc = refs
        i, j = pl.program_id(0), pl.program_id(1)

        @pl.when(j == 0)
        def _():
            x = x_ref[...]
            r = lax.rsqrt(jnp.mean(x * x, axis=-1, keepdims=True) + EPS)
            row = i * tb + lax.broadcasted_iota(jnp.int32, (tb, 1), 0)
            isx = row < t_rows
            sh = jnp.where(isx, ss_ref[0:1, :], ss_ref[2:3, :])
            sc = jnp.where(isx, ss_ref[1:2, :], ss_ref[3:4, :])
            hb = _bf((x * r * g_ref[...]) * (1.0 + sc) + sh)
            hn_sc[...] = hb
            hn_ref[...] = hb
            if gate:
                zg_ref[...] = _dot(hb, wg_ref[...])

        z_ref[...] = _bf(_dot(hn_sc[...], wm_ref[...]))

    in_specs = [pl.BlockSpec((tb, d), lambda i, j: (i, 0)), pl.BlockSpec((1, d), lambda i, j: (0, 0)),
                pl.BlockSpec((4, d), lambda i, j: (0, 0)), pl.BlockSpec((d, cb), lambda i, j: (0, j))]
    out_specs = [pl.BlockSpec((tb, d), lambda i, j: (i, 0)), pl.BlockSpec((tb, cb), lambda i, j: (i, j))]
    out_shape = [SDS((s_rows, d), BF16), SDS((s_rows, n), BF16)]
    args = [xs, g, shsc, w_main]
    if gate:
        in_specs.append(pl.BlockSpec((d, LANES), lambda i, j: (0, 0)))
        out_specs.append(pl.BlockSpec((tb, LANES), lambda i, j: (i, 0)))
        out_shape.append(SDS((s_rows, LANES), F32))
        args.append(w_gate)
    return pl.pallas_call(
        body, name=name, grid=(s_rows // tb, n // cb), in_specs=in_specs, out_specs=out_specs, out_shape=out_shape,
        scratch_shapes=[pltpu.VMEM((tb, d), BF16)], compiler_params=_cp("arbitrary", "arbitrary"))(*args)


def _seg_masks(row, t_rows, s_rows):
    prev_ok = (row != 0) & (row != t_rows)
    next_ok = (row != t_rows - 1) & (row != s_rows - 1)
    return prev_ok, next_ok


def _shift_rows(z, halo_prev, halo_next, tb):
    loc = lax.broadcasted_iota(jnp.int32, (tb, 1), 0)
    zp = jnp.where(loc == 0, halo_prev, pltpu.roll(z, 1, 0))
    zn = jnp.where(loc == tb - 1, halo_next, pltpu.roll(z, tb - 1, 0))
    return zp, zn


def _qk_conv(z_main, conv_w, t_rows, md, qscale):
    s_rows = z_main.shape[0]
    tb = _pick(s_rows, (1280, 1024, 256))
    cb = _pick(md, (512, 256, 128))
    nb8 = tb // 8

    def body(zm, zp, zn, w_ref, o_ref):
        i, j = pl.program_id(0), pl.program_id(1)
        z = zm[...].astype(F32)
        zprev, znext = _shift_rows(z, zp[7:8, :].astype(F32), zn[0:1, :].astype(F32), tb)
        row = i * tb + lax.broadcasted_iota(jnp.int32, (tb, 1), 0)
        prev_ok, next_ok = _seg_masks(row, t_rows, s_rows)
        pre = (w_ref[0:1, :] * jnp.where(prev_ok, zprev, 0.0) + w_ref[1:2, :] * z
               + w_ref[2:3, :] * jnp.where(next_ok, znext, 0.0))
        scale = jnp.where(j * cb < md, qscale, 1.0)
        o_ref[...] = _bf(pre * _sigmoid(pre) * scale)

    return pl.pallas_call(
        body, name="qk_conv", grid=(s_rows // tb, 2 * md // cb),
        in_specs=[pl.BlockSpec((tb, cb), lambda i, j: (i, j)),
                  pl.BlockSpec((8, cb), lambda i, j: (jnp.maximum(i * nb8 - 1, 0), j)),
                  pl.BlockSpec((8, cb), lambda i, j: (jnp.minimum((i + 1) * nb8, s_rows // 8 - 1), j)),
                  pl.BlockSpec((8, cb), lambda i, j: (0, j))],
        out_specs=pl.BlockSpec((tb, cb), lambda i, j: (i, j)),
        out_shape=SDS((s_rows, 2 * md), BF16), compiler_params=_cp("arbitrary", "arbitrary"))(z_main, z_main, z_main, conv_w)


def _chunk_gates(gates, bias, rev):
    ln = gates.shape[0]
    gz = gates + bias
    logf = _log_sigmoid(gz)
    r_id = lax.broadcasted_iota(jnp.int32, (ln, ln), 0)
    c_id = lax.broadcasted_iota(jnp.int32, (ln, ln), 1)
    mask = (c_id >= r_id) if rev else (c_id <= r_id)
    mb = mask.astype(F32).astype(BF16)
    b_all = _mask_dot(mb, logf)
    g_all = jnp.sum(logf, axis=0, keepdims=True)
    return gz, b_all, b_all.T, gz.T, g_all, mask, mb


def _head_weights(b_col, b_row, i_row, m_in, mask):
    d = jnp.where(mask, b_col - b_row + i_row, NEG)
    inter = b_col + m_in
    m_row = jnp.maximum(inter, jnp.max(d, axis=1, keepdims=True))
    return jnp.exp(d - m_row), jnp.exp(inter - m_row), m_row


def _head_state_coeffs(g, b_col, i_col, m_in):
    a = g - b_col + i_col
    m_new = jnp.maximum(g + m_in, jnp.max(a, axis=0, keepdims=True))
    return jnp.exp(g + m_in - m_new), jnp.exp(a - m_new), m_new


def _mlstm_fwd(qk, z_main, zg, bias, nh):
    s_rows = qk.shape[0]
    md = qk.shape[1] // 2
    dh = md // nh
    nc = s_rows // LCH
    ln = LCH

    def chunk_f(i):
        return jnp.where(i == 0, nc - 1, i - 1)

    def chunk_b(i):
        return jnp.where(i == 0, nc - 1, nc - 1 - i)

    def body(qf, kf, vf, gf, qb, kb, vb, gb, bias_ref, hf_ref, hb_ref, cf_ref, nf_ref, mf_ref, cb_ref, nb_ref, mb_ref,
             c_sc, n_sc, m_sc):
        i = pl.program_id(0)

        @pl.when(i == 0)
        def _():
            c_sc[...] = jnp.zeros_like(c_sc)
            n_sc[...] = jnp.zeros_like(n_sc)
            m_sc[...] = jnp.full(m_sc.shape, M_INIT, F32)

        for dr, (q_ref, k_ref, v_ref, g_ref, h_ref, c_out, n_out, m_out) in enumerate(
                ((qf, kf, vf, gf, hf_ref, cf_ref, nf_ref, mf_ref), (qb, kb, vb, gb, hb_ref, cb_ref, nb_ref, mb_ref))):
            gz, b_all, b_t, g_t, g_all, mask, _ = _chunk_gates(g_ref[...], bias_ref[...], dr == 1)
            for h in range(nh):
                ci, cf = 2 * dr * nh + h, (2 * dr + 1) * nh + h
                sl = slice(h * dh, (h + 1) * dh)
                q, k, v = q_ref[:, sl], k_ref[:, sl], v_ref[:, sl]
                c_in, n_in, m_in = c_sc[dr, h], n_sc[dr, h, 0:1, :], m_sc[dr, h, 0:1, 0:1]
                c_out[sl, :] = c_in
                n_out[:, sl] = n_sc[dr, h]
                m_out[h] = m_sc[dr, h]
                b_col, b_row, i_col, i_row = b_all[:, cf:cf + 1], b_t[cf:cf + 1, :], gz[:, ci:ci + 1], g_t[ci:ci + 1, :]
                g = g_all[:, cf:cf + 1]
                w, w_int, m_row = _head_weights(b_col, b_row, i_row, m_in, mask)
                s_mat = _dot_nt(q, k) * w
                num = _dot(_bf(s_mat), v) + w_int * _dot(q, _bf(c_in))
                den = jnp.sum(s_mat, axis=1, keepdims=True) + w_int * jnp.sum(q.astype(F32) * n_in, axis=1, keepdims=True)
                h_ref[:, sl] = _bf(num / jnp.maximum(jnp.abs(den), jnp.exp(-m_row)))
                a_old, coef, m_new = _head_state_coeffs(g, b_col, i_col, m_in)
                kw = k.astype(F32) * coef
                c_sc[dr, h] = a_old * c_in + _dot_tn(_bf(kw), v)
                n_sc[dr, h] = jnp.broadcast_to(a_old * n_in + jnp.sum(kw, axis=0, keepdims=True), (8, dh))
                m_sc[dr, h] = jnp.broadcast_to(m_new, (8, LANES))

    def tok(cfn, col):
        return pl.BlockSpec((ln, md), lambda i: (cfn(i), col))

    def gat(cfn):
        return pl.BlockSpec((ln, LANES), lambda i: (cfn(i), 0))

    def st(cfn, shape):
        return pl.BlockSpec((None,) + shape, lambda i: (cfn(i),) + (0,) * len(shape))

    st_shapes = ((nh * dh, dh), (8, md), (nh, 8, LANES))
    return pl.pallas_call(
        body, name="mlstm_fwd", grid=(nc,),
        in_specs=[tok(chunk_f, 0), tok(chunk_f, 1), tok(chunk_f, 2), gat(chunk_f),
                  tok(chunk_b, 0), tok(chunk_b, 1), tok(chunk_b, 2), gat(chunk_b),
                  pl.BlockSpec((1, LANES), lambda i: (0, 0))],
        out_specs=[tok(chunk_f, 0), tok(chunk_b, 0)] + [st(chunk_f, s) for s in st_shapes] + [st(chunk_b, s) for s in st_shapes],
        out_shape=[SDS((s_rows, md), BF16)] * 2 + [SDS((nc,) + s, F32) for s in st_shapes] * 2,
        scratch_shapes=[pltpu.VMEM((2, nh, dh, dh), F32), pltpu.VMEM((2, nh, 8, dh), F32), pltpu.VMEM((2, nh, 8, LANES), F32)],
        compiler_params=_cp("arbitrary"))(qk, qk, z_main, zg, qk, qk, z_main, zg, bias)


def _head_rms(hs, nh, dh):
    parts, scales = [], []
    for h in range(nh):
        hh = hs[:, h * dh:(h + 1) * dh]
        r = lax.rsqrt(jnp.mean(hh * hh, axis=-1, keepdims=True) + EPS)
        parts.append(hh * r)
        scales.append(r)
    return jnp.concatenate(parts, axis=1), scales


def _layer_norm(v):
    vc = v - jnp.mean(v, axis=-1, keepdims=True)
    r = lax.rsqrt(jnp.mean(vc * vc, axis=-1, keepdims=True) + EPS)
    return vc * r, r


def _sgu_mix(vnb, ws_ref, bs_ref, tb, ng, gd, sc):
    rows = []
    for ch in range(tb // sc):
        cols = []
        for g in range(ng):
            blk = vnb[ch * sc:(ch + 1) * sc, g * gd:(g + 1) * gd]
            cols.append(_dot(_bf(ws_ref[g]), blk) + bs_ref[:, g:g + 1])
        rows.append(jnp.concatenate(cols, axis=1))
    return jnp.concatenate(rows, axis=0)


def _mixer_fwd(hf, hb, z_main, xs, hg, lng, lnb, w_s, b_st, wbm, wbs, wout, mx2, t_rows, nh):
    d = xs.shape[1]
    ng, sc = w_s.shape[0], w_s.shape[1]
    dh, gd = d // nh, d // ng
    tb = _pick(t_rows, (256,))

    def body(hf_ref, hb_ref, zo, zu, zv, zgm, zgg, x_ref, hg_ref, lng_ref, lnb_ref, ws_ref, bs_ref, wbm_ref, wbs_ref,
             wo_ref, mx2_ref, h1_ref, ym_ref, ys_ref, pm_ref, ps_ref, y_ref, out_ref):
        hs = hf_ref[...].astype(F32) + hb_ref[...].astype(F32)
        hn, _ = _head_rms(hs, nh, dh)
        ym = _bf(_sigmoid(zo[...].astype(F32)) * (hn * hg_ref[...]))
        ym_ref[...] = ym
        vhat, _ = _layer_norm(_gelu(zv[...].astype(F32)))
        vnb = _bf(vhat * lng_ref[...] + lnb_ref[...])
        ys = _bf(_gelu(zu[...].astype(F32)) * _sgu_mix(vnb, ws_ref, bs_ref, tb, ng, gd, sc))
        ys_ref[...] = ys
        pm = _dot(ym, wbm_ref[...])
        ps = _dot(ys, wbs_ref[...])
        pm_ref[...] = _bf(pm)
        ps_ref[...] = _bf(ps)
        y = _bf(_sigmoid(zgm[...].astype(F32)) * pm + _sigmoid(zgg[...].astype(F32)) * ps)
        y_ref[...] = y
        out = _dot(y, wo_ref[...])
        out_ref[...] = _bf(out)
        h1_ref[...] = x_ref[...] + mx2_ref[...] * out

    def tok(col):
        return pl.BlockSpec((tb, d), lambda i: (i, col))

    def full(shape):
        return pl.BlockSpec(shape, lambda i: (0,) * len(shape))

    return pl.pallas_call(
        body, name="mixer_fwd", grid=(t_rows // tb,),
        in_specs=[tok(0), tok(0), tok(3), tok(4), tok(5), tok(6), tok(7), tok(0), full((1, d)), full((1, d)), full((1, d)),
                  full((ng, sc, sc)), full((sc, LANES)), full((d, d)), full((d, d)), full((d, d)), full((1, d))],
        out_specs=[tok(0)] * 7,
        out_shape=[SDS((t_rows, d), F32)] + [SDS((t_rows, d), BF16)] * 6,
        compiler_params=_cp("arbitrary"))(hf, hb, z_main, z_main, z_main, z_main, z_main, xs, hg, lng, lnb, w_s, b_st,
                                          wbm, wbs, wout, mx2)


def _grid_taps(a_ext, n_ext):
    col = lax.broadcasted_iota(jnp.int32, (n_ext, 1), 0) % GRID_W
    left = jnp.where(col != 0, pltpu.roll(a_ext, 1, 0), 0.0)
    right = jnp.where(col != GRID_W - 1, pltpu.roll(a_ext, n_ext - 1, 0), 0.0)
    return left, right


def _with_halo(prev, main, nxt, i, ni, tb):
    ext = jnp.concatenate([prev, main, nxt], axis=0).astype(F32)
    pos = lax.broadcasted_iota(jnp.int32, (tb + 2 * GRID_W, 1), 0)
    inside = ((pos >= GRID_W) | (i > 0)) & ((pos < tb + GRID_W) | (i < ni - 1))
    return jnp.where(inside, ext, 0.0)


def _halo_specs(tb, cb, t_rows, col0=0):
    nh64 = tb // GRID_W
    return [pl.BlockSpec((tb, cb), lambda i, j: (i, col0 + j)),
            pl.BlockSpec((GRID_W, cb), lambda i, j: (jnp.maximum(i * nh64 - 1, 0), col0 + j)),
            pl.BlockSpec((GRID_W, cb), lambda i, j: (jnp.minimum((i + 1) * nh64, t_rows // GRID_W - 1), col0 + j))]


def _ffn_tail(ab, w_conv9, w_down, h1, mx5, gfin, target, dff):
    t_rows, d = h1.shape
    tb = _pick(t_rows, (512,))
    cb = _pick(dff, (256, 128))
    ni, nj = t_rows // tb, dff // cb
    n_ext = tb + 2 * GRID_W

    def body(am, ap, an, b_ref, wc_ref, wd_ref, h1_ref, mx5_ref, gf_ref, tg_ref, ac_ref, f_ref, dh2_ref, dffn_ref, st_ref, acc):
        i, j = pl.program_id(0), pl.program_id(1)
        a_ext = _with_halo(ap[...], am[...], an[...], i, ni, tb)
        left, right = _grid_taps(a_ext, n_ext)
        conv = jnp.zeros((tb, cb), F32)
        for di in range(3):
            o = di * GRID_W
            conv = conv + (wc_ref[3 * di:3 * di + 1, :] * left[o:o + tb] + wc_ref[3 * di + 1:3 * di + 2, :] * a_ext[o:o + tb]
                           + wc_ref[3 * di + 2:3 * di + 3, :] * right[o:o + tb])
        ac_ref[...] = _bf(conv)
        fb = _bf(conv * _sigmoid(conv) * b_ref[...].astype(F32))
        f_ref[...] = fb

        @pl.when(j == 0)
        def _():
            acc[...] = jnp.zeros_like(acc)

        @pl.when((i == 0) & (j == 0))
        def _():
            st_ref[...] = jnp.zeros_like(st_ref)

        acc[...] += _dot(fb, wd_ref[...])

        @pl.when(j == nj - 1)
        def _():
            ffn = acc[...]
            h2 = h1_ref[...] + mx5_ref[...] * ffn
            r = lax.rsqrt(jnp.mean(h2 * h2, axis=-1, keepdims=True) + EPS)
            xn = h2 * r
            e = xn * gf_ref[...] - tg_ref[...]
            loss = 0.5 * jnp.sum(jnp.sum(e * e, axis=1, keepdims=True), axis=0, keepdims=True) / d
            dy = e * (1.0 / d)
            dxn = dy * gf_ref[...]
            dh2 = r * (dxn - xn * jnp.mean(dxn * xn, axis=-1, keepdims=True))
            dh2_ref[...] = dh2
            dffn_ref[...] = _bf(dh2 * mx5_ref[...])
            st_ref[...] += jnp.concatenate(
                [jnp.sum(dy * xn, axis=0, keepdims=True), jnp.sum(dh2 * ffn, axis=0, keepdims=True),
                 jnp.broadcast_to(loss, (1, d)), jnp.zeros((5, d), F32)], axis=0)

    def tokd():
        return pl.BlockSpec((tb, d), lambda i, j: (i, 0))

    def rowd():
        return pl.BlockSpec((1, d), lambda i, j: (0, 0))

    return pl.pallas_call(
        body, name="ffn_tail", grid=(ni, nj),
        in_specs=_halo_specs(tb, cb, t_rows) + [pl.BlockSpec((tb, cb), lambda i, j: (i, nj + j)),
                                                pl.BlockSpec((16, cb), lambda i, j: (0, j)),
                                                pl.BlockSpec((cb, d), lambda i, j: (j, 0)), tokd(), rowd(), rowd(), tokd()],
        out_specs=[pl.BlockSpec((tb, cb), lambda i, j: (i, j)), pl.BlockSpec((tb, cb), lambda i, j: (i, j)), tokd(), tokd(),
                   pl.BlockSpec((8, d), lambda i, j: (0, 0))],
        out_shape=[SDS((t_rows, dff), BF16), SDS((t_rows, dff), BF16), SDS((t_rows, d), F32), SDS((t_rows, d), BF16),
                   SDS((8, d), F32)],
        scratch_shapes=[pltpu.VMEM((tb, d), F32)],
        compiler_params=_cp("arbitrary", "arbitrary"))(ab, ab, ab, ab, w_conv9, w_down, h1, mx5, gfin, target)


def _ffn_bwd_gate(dffn, w_down, aconv, ab, dff):
    t_rows, d = dffn.shape
    tb = _pick(t_rows, (512,))
    cb = _pick(dff, (256, 128))
    nj = dff // cb

    def body(g_ref, wd_ref, ac_ref, b_ref, db_ref, dac_ref):
        df = _dot_nt(g_ref[...], wd_ref[...])
        ac = ac_ref[...].astype(F32)
        sa = _sigmoid(ac)
        db_ref[...] = _bf(df * ac * sa)
        dac_ref[...] = _bf(df * b_ref[...].astype(F32) * (sa * (1.0 + ac * (1.0 - sa))))

    blk = pl.BlockSpec((tb, cb), lambda i, j: (i, j))
    return pl.pallas_call(
        body, name="ffn_bwd_gate", grid=(t_rows // tb, nj),
        in_specs=[pl.BlockSpec((tb, d), lambda i, j: (i, 0)), pl.BlockSpec((cb, d), lambda i, j: (j, 0)), blk,
                  pl.BlockSpec((tb, cb), lambda i, j: (i, nj + j))],
        out_specs=[blk, blk], out_shape=[SDS((t_rows, dff), BF16)] * 2,
        compiler_params=_cp("arbitrary", "arbitrary"))(dffn, w_down, aconv, ab)


def _ffn_conv_bwd(dac, ab, w_conv9, dff):
    t_rows = dac.shape[0]
    tb = _pick(t_rows, (512,))
    cb = _pick(dff, (256, 128))
    ni, nj = t_rows // tb, dff // cb
    n_ext = tb + 2 * GRID_W
    nh64 = tb // GRID_W

    def body(dm, dp, dn, am, ap, an, wc_ref, da_ref, gw_ref):
        i = pl.program_id(1)
        d_ext = _with_halo(dp[...], dm[...], dn[...], i, ni, tb)
        a_ext = _with_halo(ap[...], am[...], an[...], i, ni, tb)
        d_left, d_right = _grid_taps(d_ext, n_ext)
        a_left, a_right = _grid_taps(a_ext, n_ext)
        dmain = d_ext[GRID_W:GRID_W + tb]
        da = jnp.zeros((tb, cb), F32)
        rows = []
        for di in range(3):
            o = (2 - di) * GRID_W
            da = da + (wc_ref[3 * di:3 * di + 1, :] * d_right[o:o + tb] + wc_ref[3 * di + 1:3 * di + 2, :] * d_ext[o:o + tb]
                       + wc_ref[3 * di + 2:3 * di + 3, :] * d_left[o:o + tb])
            o = di * GRID_W
            for tap in (a_left, a_ext, a_right):
                rows.append(jnp.sum(dmain * tap[o:o + tb], axis=0, keepdims=True))
        da_ref[...] = _bf(da)

        @pl.when(i == 0)
        def _():
            gw_ref[...] = jnp.zeros_like(gw_ref)

        gw_ref[...] += jnp.concatenate(rows + [jnp.zeros((7, cb), F32)], axis=0)

    def halo(col0):
        return [pl.BlockSpec((tb, cb), lambda j, i: (i, col0 + j)),
                pl.BlockSpec((GRID_W, cb), lambda j, i: (jnp.maximum(i * nh64 - 1, 0), col0 + j)),
                pl.BlockSpec((GRID_W, cb), lambda j, i: (jnp.minimum((i + 1) * nh64, t_rows // GRID_W - 1), col0 + j))]

    return pl.pallas_call(
        body, name="ffn_conv_bwd", grid=(nj, ni),
        in_specs=halo(0) + halo(0) + [pl.BlockSpec((16, cb), lambda j, i: (0, j))],
        out_specs=[pl.BlockSpec((tb, cb), lambda j, i: (i, j)), pl.BlockSpec((16, cb), lambda j, i: (0, j))],
        out_shape=[SDS((t_rows, dff), BF16), SDS((16, dff), F32)],
        compiler_params=_cp("arbitrary", "arbitrary"))(dac, dac, dac, ab, ab, ab, w_conv9)


def _proj_norm_bwd(pairs, x_arr, x_row0, g, scale, resid, m_rows, name):
    d = x_arr.shape[1]
    tm = _pick(m_rows, (512, 256))
    ni = m_rows // tm
    starts, total = [], 0
    for (_, _, _, _, k_p, tk_p) in pairs:
        starts.append(total)
        total += k_p // tk_p
    npairs = len(pairs)
    has_dx = resid is not None

    def body(*refs):
        a_refs, b_refs = refs[0:2 * npairs:2], refs[1:2 * npairs:2]
        rest = refs[2 * npairs:]
        if has_dx:
            x_ref, g_ref, sc_ref, r_ref, dx_ref, st_ref, acc = rest
        else:
            x_ref, g_ref, sc_ref, st_ref, acc = rest
        i, k = pl.program_id(0), pl.program_id(1)

        @pl.when(k == 0)
        def _():
            acc[...] = jnp.zeros_like(acc)

        @pl.when((i == 0) & (k == 0))
        def _():
            st_ref[...] = jnp.zeros_like(st_ref)

        for p in range(npairs):
            nk = pairs[p][4] // pairs[p][5]

            @pl.when((k >= starts[p]) & (k < starts[p] + nk))
            def _(p=p):
                acc[...] += _dot_nt(a_refs[p][...], b_refs[p][...])

        @pl.when(k == total - 1)
        def _():
            dhn = acc[...]
            x = x_ref[...]
            r = lax.rsqrt(jnp.mean(x * x, axis=-1, keepdims=True) + EPS)
            xn = x * r
            dmod = dhn * (1.0 + sc_ref[...])
            dxn = dmod * g_ref[...]
            if has_dx:
                dx_ref[...] = r * (dxn - xn * jnp.mean(dxn * xn, axis=-1, keepdims=True)) + r_ref[...]
            st_ref[...] += jnp.concatenate(
                [jnp.sum(dmod * xn, axis=0, keepdims=True), jnp.sum(dhn, axis=0, keepdims=True),
                 jnp.sum(dhn * (xn * g_ref[...]), axis=0, keepdims=True), jnp.zeros((5, d), F32)], axis=0)

    in_specs, args = [], []
    for p, (a, a_row0, b, b_col0, k_p, tk_p) in enumerate(pairs):
        nk, s0, ar, bc = k_p // tk_p, starts[p], a_row0 // tm, b_col0 // tk_p

        def kk(k, s0=s0, nk=nk):
            return jnp.clip(k - s0, 0, nk - 1)

        in_specs.append(pl.BlockSpec((tm, tk_p), lambda i, k, ar=ar, kk=kk: (ar + i, kk(k))))
        in_specs.append(pl.BlockSpec((d, tk_p), lambda i, k, bc=bc, kk=kk: (0, bc + kk(k))))
        args += [a, b]
    xr = x_row0 // tm
    in_specs += [pl.BlockSpec((tm, d), lambda i, k: (xr + i, 0)), pl.BlockSpec((1, d), lambda i, k: (0, 0)),
                 pl.BlockSpec((1, d), lambda i, k: (0, 0))]
    args += [x_arr, g, scale]
    out_specs, out_shape = [], []
    if has_dx:
        in_specs.append(pl.BlockSpec((tm, d), lambda i, k: (i, 0)))
        args.append(resid)
        out_specs.append(pl.BlockSpec((tm, d), lambda i, k: (i, 0)))
        out_shape.append(SDS((m_rows, d), F32))
    out_specs.append(pl.BlockSpec((8, d), lambda i, k: (0, 0)))
    out_shape.append(SDS((8, d), F32))
    return pl.pallas_call(
        body, name=name, grid=(ni, total), in_specs=in_specs, out_specs=out_specs, out_shape=out_shape,
        scratch_shapes=[pltpu.VMEM((tm, d), F32)], compiler_params=_cp("arbitrary", "arbitrary"))(*args)


def _wgrad(a, b, k_rows, name):
    m, n = a.shape[1], b.shape[1]
    tm = _pick(m, (1408, 1024, 512, 384, 256, 128))
    tn = _pick(n, (1408, 1024, 768, 512, 384, 256, 128))
    tk = _pick(k_rows, (1280, 1024, 256))
    nk = k_rows // tk

    def body(a_ref, b_ref, o_ref, acc):
        k = pl.program_id(2)

        @pl.when(k == 0)
        def _():
            acc[...] = jnp.zeros_like(acc)

        acc[...] += _dot_tn(a_ref[...], b_ref[...])

        @pl.when(k == nk - 1)
        def _():
            o_ref[...] = _bf(acc[...])

    return pl.pallas_call(
        body, name=name, grid=(m // tm, n // tn, nk),
        in_specs=[pl.BlockSpec((tk, tm), lambda i, j, k: (k, i)), pl.BlockSpec((tk, tn), lambda i, j, k: (k, j))],
        out_specs=pl.BlockSpec((tm, tn), lambda i, j, k: (i, j)), out_shape=SDS((m, n), BF16),
        scratch_shapes=[pltpu.VMEM((tm, tn), F32)],
        compiler_params=_cp("arbitrary", "arbitrary", "arbitrary"))(a, b)


def _lane_put(col, lane_idx):
    lane = lax.broadcasted_iota(jnp.int32, (1, LANES), 1)
    return jnp.where(lane == lane_idx, col, 0.0)


def _mixer_bwd(dh1, out, hf, hb, z_main, pm, ps, hg, lng, lnb, w_s, b_st, wbm, wbs, wout, mx2, t_rows, nh):
    d = dh1.shape[1]
    ng, sc = w_s.shape[0], w_s.shape[1]
    dh, gd = d // nh, d // ng
    tb = _pick(t_rows, (256,))

    def body(dh1_ref, out_ref, hf_ref, hb_ref, zo, zu, zv, zgm, zgg, pm_ref, ps_ref, hg_ref, lng_ref, lnb_ref, ws_ref, bs_ref,
             wbm_ref, wbs_ref, wo_ref, mx2_ref, dz_ref, dhs_ref, dout_ref, dpm_ref, dps_ref, st_ref, dws_ref, dbs_ref):
        i = pl.program_id(0)

        @pl.when(i == 0)
        def _():
            st_ref[...] = jnp.zeros_like(st_ref)
            dws_ref[...] = jnp.zeros_like(dws_ref)
            dbs_ref[...] = jnp.zeros_like(dbs_ref)

        dh1v = dh1_ref[...]
        doutb = _bf(dh1v * mx2_ref[...])
        dout_ref[...] = doutb
        d_mx2 = jnp.sum(dh1v * out_ref[...].astype(F32), axis=0, keepdims=True)
        dy = _dot_nt(doutb, wo_ref[...])
        sgm, sgg = _sigmoid(zgm[...].astype(F32)), _sigmoid(zgg[...].astype(F32))
        dpmb, dpsb = _bf(dy * sgm), _bf(dy * sgg)
        dpm_ref[...] = dpmb
        dps_ref[...] = dpsb
        dz_ref[:, 3 * d:4 * d] = _bf(dy * pm_ref[...].astype(F32) * sgm * (1.0 - sgm))
        dz_ref[:, 4 * d:5 * d] = _bf(dy * ps_ref[...].astype(F32) * sgg * (1.0 - sgg))
        dym = _dot_nt(dpmb, wbm_ref[...])
        dys = _dot_nt(dpsb, wbs_ref[...])
        hs = hf_ref[...].astype(F32) + hb_ref[...].astype(F32)
        hn, scales = _head_rms(hs, nh, dh)
        so = _sigmoid(zo[...].astype(F32))
        dz_ref[:, 0:d] = _bf(dym * (hn * hg_ref[...]) * so * (1.0 - so))
        dhmn = dym * so
        d_hg = jnp.sum(dhmn * hn, axis=0, keepdims=True)
        dhn = dhmn * hg_ref[...]
        for h in range(nh):
            sl = slice(h * dh, (h + 1) * dh)
            dhs_ref[:, sl] = _bf(scales[h] * (dhn[:, sl] - hn[:, sl] * jnp.mean(dhn[:, sl] * hn[:, sl], axis=-1, keepdims=True)))
        zuv, zvv = zu[...].astype(F32), zv[...].astype(F32)
        u = _gelu(zuv)
        vhat, rstd = _layer_norm(_gelu(zvv))
        vnb = _bf(vhat * lng_ref[...] + lnb_ref[...])
        mixed = _sgu_mix(vnb, ws_ref, bs_ref, tb, ng, gd, sc)
        dz_ref[:, d:2 * d] = _bf(dys * mixed * _gelu_grad(zuv))
        dmix = dys * u
        rows = []
        dbs = jnp.zeros((sc, LANES), F32)
        for ch in range(tb // sc):
            cols = []
            for g in range(ng):
                dm = dmix[ch * sc:(ch + 1) * sc, g * gd:(g + 1) * gd]
                dmb = _bf(dm)
                dws_ref[g] += _dot_nt(dmb, vnb[ch * sc:(ch + 1) * sc, g * gd:(g + 1) * gd])
                dbs = dbs + _lane_put(jnp.sum(dm, axis=1, keepdims=True), g)
                cols.append(_dot_tn(_bf(ws_ref[g]), dmb))
            rows.append(jnp.concatenate(cols, axis=1))
        dbs_ref[...] += dbs
        dvn = jnp.concatenate(rows, axis=0)
        d_lng = jnp.sum(dvn * vhat, axis=0, keepdims=True)
        d_lnb = jnp.sum(dvn, axis=0, keepdims=True)
        dvh = dvn * lng_ref[...]
        dvg = rstd * (dvh - jnp.mean(dvh, axis=-1, keepdims=True) - vhat * jnp.mean(dvh * vhat, axis=-1, keepdims=True))
        dz_ref[:, 2 * d:3 * d] = _bf(dvg * _gelu_grad(zvv))
        st_ref[...] += jnp.concatenate([d_mx2, d_hg, d_lng, d_lnb, jnp.zeros((4, d), F32)], axis=0)

    def tok(col):
        return pl.BlockSpec((tb, d), lambda i: (i, col))

    def full(shape):
        return pl.BlockSpec(shape, lambda i: (0,) * len(shape))

    return pl.pallas_call(
        body, name="mixer_bwd", grid=(t_rows // tb,),
        in_specs=[tok(0), tok(0), tok(0), tok(0), tok(3), tok(4), tok(5), tok(6), tok(7), tok(0), tok(0), full((1, d)),
                  full((1, d)), full((1, d)), full((ng, sc, sc)), full((sc, LANES)), full((d, d)), full((d, d)), full((d, d)),
                  full((1, d))],
        out_specs=[pl.BlockSpec((tb, 5 * d), lambda i: (i, 0)), tok(0), tok(0), tok(0), tok(0), full((8, d)), full((ng, sc, sc)),
                   full((sc, LANES))],
        out_shape=[SDS((t_rows, 5 * d), BF16)] + [SDS((t_rows, d), BF16)] * 4 + [SDS((8, d), F32), SDS((ng, sc, sc), F32),
                                                                                SDS((sc, LANES), F32)],
        compiler_params=_cp("arbitrary"))(dh1, out, hf, hb, z_main, z_main, z_main, z_main, z_main, pm, ps, hg, lng, lnb, w_s,
                                          b_st, wbm, wbs, wout, mx2)


def _mlstm_bwd(qk, z_main, zg, bias, dhs, states_f, states_b, nh, t_rows):
    s_rows = qk.shape[0]
    md = qk.shape[1] // 2
    dh = md // nh
    nc = s_rows // LCH
    nx = t_rows // LCH
    ln = LCH

    def chunk_f(i):
        return jnp.where(i == nc - 1, nc - 1, nc - 2 - i)

    def chunk_b(i):
        return jnp.where(i == nc - 1, nc - 1, i)

    def body(qf, kf, vf, gf, dhf, cf, nf, mf_, qb, kb, vb, gb, dhb, cb, nb, mb_, bias_ref, dqkvf_ref, dgf_ref, dqkvb_ref, dgb_ref,
             dc_sc, dn_sc):
        i = pl.program_id(0)
        is_ctx = i == nc - 1

        @pl.when(i == 0)
        def _():
            dc_sc[...] = jnp.zeros_like(dc_sc)
            dn_sc[...] = jnp.zeros_like(dn_sc)

        for dr, (q_ref, k_ref, v_ref, g_ref, dh_ref, c_ref, n_ref, m_ref, dqkv_ref, dg_ref) in enumerate(
                ((qf, kf, vf, gf, dhf, cf, nf, mf_, dqkvf_ref, dgf_ref), (qb, kb, vb, gb, dhb, cb, nb, mb_, dqkvb_ref, dgb_ref))):
            gz, b_all, b_t, g_t, g_all, mask, mfl = _chunk_gates(g_ref[...], bias_ref[...], dr == 1)
            x1 = jnp.zeros((ln, LANES), F32)
            x2 = jnp.zeros((ln, LANES), F32)
            dig = jnp.zeros((ln, LANES), F32)
            e_row = jnp.zeros((1, LANES), F32)
            for h in range(nh):
                ci, cfl = 2 * dr * nh + h, (2 * dr + 1) * nh + h
                sl = slice(h * dh, (h + 1) * dh)
                q, k, v = q_ref[:, sl], k_ref[:, sl], v_ref[:, sl]
                qf32, kf32 = q.astype(F32), k.astype(F32)
                dhv = jnp.where(is_ctx, 0.0, dh_ref[:, sl].astype(F32))
                c_in, n_in, m_in = c_ref[sl, :], n_ref[0:1, sl], m_ref[h, 0:1, 0:1]
                b_col, b_row, i_col, i_row = b_all[:, cfl:cfl + 1], b_t[cfl:cfl + 1, :], gz[:, ci:ci + 1], g_t[ci:ci + 1, :]
                g = g_all[:, cfl:cfl + 1]
                w, w_int, m_row = _head_weights(b_col, b_row, i_row, m_in, mask)
                s_mat = _dot_nt(q, k) * w
                sb, cb16 = _bf(s_mat), _bf(c_in)
                num = _dot(sb, v) + w_int * _dot(q, cb16)
                den = jnp.sum(s_mat, axis=1, keepdims=True) + w_int * jnp.sum(qf32 * n_in, axis=1, keepdims=True)
                e_m = jnp.exp(-m_row)
                dnm = jnp.maximum(jnp.abs(den), e_m)
                dnum = dhv / dnm
                hdh = jnp.sum((num / dnm) * dhv, axis=1, keepdims=True)
                dden = jnp.where(jnp.abs(den) > e_m, -(hdh / dnm) * jnp.sign(den), 0.0)
                dnum_b = _bf(dnum)
                ds = _dot_nt(dnum_b, v) + dden
                pb = _bf(w * ds)
                gmat = s_mat * ds
                a_old, coef, _ = _head_state_coeffs(g, b_col, i_col, m_in)
                dc_new, dn_new = dc_sc[dr, h], dn_sc[dr, h, 0:1, :]
                dcb = _bf(dc_new)
                dv = _dot_tn(sb, dnum_b) + _dot(_bf(kf32 * coef), dcb)
                dq_inter = w_int * (_dot_nt(dnum_b, cb16) + dden * n_in)
                dq = _dot(pb, k) + dq_inter
                dk_state = coef * (_dot_nt(v, dcb) + dn_new)
                dk = _dot_tn(pb, q) + dk_state
                dqkv_ref[:, sl] = _bf(dq)
                dqkv_ref[:, md + h * dh:md + (h + 1) * dh] = _bf(dk)
                dqkv_ref[:, 2 * md + h * dh:2 * md + (h + 1) * dh] = _bf(dv)
                row_intra = jnp.sum(gmat, axis=1, keepdims=True)
                col_intra = jnp.sum(gmat.T, axis=1, keepdims=True)
                row_inter = jnp.sum(qf32 * dq_inter, axis=1, keepdims=True)
                col_inter = jnp.sum(kf32 * dk_state, axis=1, keepdims=True)
                e_old = a_old * (jnp.sum(jnp.sum(dc_new * c_in, axis=1, keepdims=True), axis=0, keepdims=True)
                                 + jnp.sum(dn_new * n_in, axis=1, keepdims=True))
                x1 = x1 + _lane_put(row_intra - col_intra + row_inter, cfl)
                x2 = x2 + _lane_put(col_inter, cfl)
                e_row = e_row + _lane_put(e_old, cfl)
                dig = dig + _lane_put(col_intra + col_inter, ci)
                dc_sc[dr, h] = a_old * dc_new + _dot_tn(_bf(qf32 * w_int), dnum_b)
                dn_sc[dr, h] = jnp.broadcast_to(a_old * dn_new + jnp.sum(qf32 * (w_int * dden), axis=0, keepdims=True), (8, dh))
            dlogf = _mask_dot_t(mfl, x1) + _mask_dot(mfl, x2) - x2 + e_row
            dg_ref[...] = dig + dlogf / (1.0 + jnp.exp(gz))

    def tok(cfn, col):
        return pl.BlockSpec((ln, md), lambda i: (cfn(i), col))

    def dht(cfn):
        return pl.BlockSpec((ln, md), lambda i: (jnp.minimum(cfn(i), nx - 1), 0))

    def gat(cfn):
        return pl.BlockSpec((ln, LANES), lambda i: (cfn(i), 0))

    def st(cfn, shape):
        return pl.BlockSpec((None,) + shape, lambda i: (cfn(i),) + (0,) * len(shape))

    st_shapes = ((nh * dh, dh), (8, md), (nh, 8, LANES))

    def side(cfn):
        return [tok(cfn, 0), tok(cfn, 1), tok(cfn, 2), gat(cfn), dht(cfn)] + [st(cfn, s) for s in st_shapes]

    def outs(cfn):
        return [pl.BlockSpec((ln, 3 * md), lambda i: (cfn(i), 0)), gat(cfn)]

    return pl.pallas_call(
        body, name="mlstm_bwd", grid=(nc,),
        in_specs=side(chunk_f) + side(chunk_b) + [pl.BlockSpec((1, LANES), lambda i: (0, 0))],
        out_specs=outs(chunk_f) + outs(chunk_b),
        out_shape=[SDS((s_rows, 3 * md), BF16), SDS((s_rows, LANES), F32)] * 2,
        scratch_shapes=[pltpu.VMEM((2, nh, dh, dh), F32), pltpu.VMEM((2, nh, 8, dh), F32)],
        compiler_params=_cp("arbitrary"))(qk, qk, z_main, zg, dhs, *states_f, qk, qk, z_main, zg, dhs, *states_b, bias)


def _qkv_conv_bwd(dqkv_f, dqkv_b, z_main, conv_w, t_rows, md, qscale):
    s_rows = z_main.shape[0]
    tb = _pick(s_rows, (1280, 1024, 256))
    cb = _pick(md, (512, 256, 128))
    ni, nj, ncq = s_rows // tb, 3 * md // cb, 2 * md // cb
    nb8 = tb // 8
    n_ext = tb + 16

    def body(fm, fp, fn, bm, bp, bn, zm, zp, zn, w_ref, dz_ref, gw_ref):
        j, i = pl.program_id(0), pl.program_id(1)

        @pl.when(j < ncq)
        def _():
            z = jnp.concatenate([zp[...], zm[...], zn[...]], axis=0).astype(F32)
            dqk = (jnp.concatenate([fp[...], fm[...], fn[...]], axis=0).astype(F32)
                   + jnp.concatenate([bp[...], bm[...], bn[...]], axis=0).astype(F32)) * jnp.where(j * cb < md, qscale, 1.0)
            row = i * tb - 8 + lax.broadcasted_iota(jnp.int32, (n_ext, 1), 0)
            prev_ok, next_ok = _seg_masks(row, t_rows, s_rows)
            zprev = jnp.where(prev_ok, pltpu.roll(z, 1, 0), 0.0)
            znext = jnp.where(next_ok, pltpu.roll(z, n_ext - 1, 0), 0.0)
            pre = w_ref[0:1, :] * zprev + w_ref[1:2, :] * z + w_ref[2:3, :] * znext
            sg = _sigmoid(pre)
            dpre = dqk * (sg * (1.0 + pre * (1.0 - sg)))
            dz = (w_ref[1:2, :] * dpre + w_ref[0:1, :] * jnp.where(next_ok, pltpu.roll(dpre, n_ext - 1, 0), 0.0)
                  + w_ref[2:3, :] * jnp.where(prev_ok, pltpu.roll(dpre, 1, 0), 0.0))
            dz_ref[...] = _bf(dz[8:8 + tb])
            dm = dpre[8:8 + tb]

            @pl.when(i == 0)
            def _():
                gw_ref[...] = jnp.zeros_like(gw_ref)

            gw_ref[...] += jnp.concatenate(
                [jnp.sum(dm * zprev[8:8 + tb], axis=0, keepdims=True), jnp.sum(dm * z[8:8 + tb], axis=0, keepdims=True),
                 jnp.sum(dm * znext[8:8 + tb], axis=0, keepdims=True), jnp.zeros((5, cb), F32)], axis=0)

        @pl.when(j >= ncq)
        def _():
            dz_ref[...] = _bf(fm[...].astype(F32) + bm[...].astype(F32))

    def halo(clampj):
        def cj(j):
            return jnp.minimum(j, ncq - 1) if clampj else j
        return [pl.BlockSpec((tb, cb), lambda j, i: (i, cj(j))),
                pl.BlockSpec((8, cb), lambda j, i: (jnp.maximum(i * nb8 - 1, 0), cj(j))),
                pl.BlockSpec((8, cb), lambda j, i: (jnp.minimum((i + 1) * nb8, s_rows // 8 - 1), cj(j)))]

    return pl.pallas_call(
        body, name="qkv_conv_bwd", grid=(nj, ni),
        in_specs=halo(False) + halo(False) + halo(True) + [pl.BlockSpec((8, cb), lambda j, i: (0, jnp.minimum(j, ncq - 1)))],
        out_specs=[pl.BlockSpec((tb, cb), lambda j, i: (i, j)), pl.BlockSpec((8, cb), lambda j, i: (0, jnp.minimum(j, ncq - 1)))],
        out_shape=[SDS((s_rows, 3 * md), BF16), SDS((8, 2 * md), F32)],
        compiler_params=_cp("arbitrary", "arbitrary"))(dqkv_f, dqkv_f, dqkv_f, dqkv_b, dqkv_b, dqkv_b, z_main, z_main, z_main, conv_w)


def _gate_grad_sum(dg_f, dg_b):
    s_rows = dg_f.shape[0]
    tb = _pick(s_rows, (1280, 1024, 256))

    def body(a_ref, b_ref, o_ref, st_ref):
        @pl.when(pl.program_id(0) == 0)
        def _():
            st_ref[...] = jnp.zeros_like(st_ref)

        s = a_ref[...] + b_ref[...]
        o_ref[...] = _bf(s)
        st_ref[...] += jnp.concatenate([jnp.sum(s, axis=0, keepdims=True), jnp.zeros((7, LANES), F32)], axis=0)

    blk = pl.BlockSpec((tb, LANES), lambda i: (i, 0))
    return pl.pallas_call(
        body, name="gate_grad_sum", grid=(s_rows // tb,), in_specs=[blk, blk],
        out_specs=[blk, pl.BlockSpec((8, LANES), lambda i: (0, 0))],
        out_shape=[SDS((s_rows, LANES), BF16), SDS((8, LANES), F32)], compiler_params=_cp("arbitrary"))(dg_f, dg_b)


def _mod_grads(silu_slots, dmx_sh, dmx_slots, dmc_tot, dmc_sh, silu_cctx, c_ctx, w_mod_c):
    d = silu_slots.shape[1]
    ncol, n6 = dmx_sh.shape[1], dmx_slots.shape[1]

    def body(ss_ref, dsh_ref, dsl_ref, dct_ref, dcs_ref, sc_ref, c_ref, w_ref, gw_ref, gb_ref, gc_ref):
        a = jnp.concatenate([ss_ref[...], sc_ref[...], jnp.zeros((7, d), F32)], axis=0)
        b = jnp.concatenate([dsh_ref[...], dcs_ref[...], jnp.zeros((7, ncol), F32)], axis=0)
        gw_ref[...] = lax.dot_general(a, b, (((0,), (0,)), ((), ())), preferred_element_type=F32, precision=HI)
        dct = dct_ref[...]
        gb_ref[...] = jnp.sum(dsl_ref[...], axis=0, keepdims=True) + jnp.concatenate(
            [dct, jnp.zeros((1, n6 - dct.shape[1]), F32)], axis=1)
        t = _dot_nt(_bf(jnp.broadcast_to(dct, (8, dct.shape[1]))), w_ref[...])
        cv = c_ref[...]
        s = _sigmoid(cv)
        gc_ref[...] = t[0:1, :] * (s * (1.0 + cv * (1.0 - s)))

    return pl.pallas_call(body, name="mod_grads", out_shape=[SDS((d, ncol), F32), SDS((1, n6), F32), SDS((1, d), F32)],
                          compiler_params=_cp())(silu_slots, dmx_sh, dmx_slots, dmc_tot, dmc_sh, silu_cctx, c_ctx, w_mod_c)


def _slot_sum(slots):
    ns, r = slots.shape[0], slots.shape[1]
    tb = _pick(r, (1024, 512, 256, 128, 64, 32, 16, 8))

    def body(s_ref, o_ref):
        acc = s_ref[0]
        for k in range(1, ns):
            acc = acc + s_ref[k]
        o_ref[...] = acc

    return pl.pallas_call(
        body, name="slot_sum", grid=(r // tb,), in_specs=[pl.BlockSpec((ns, tb, LANES), lambda i: (0, i, 0))],
        out_specs=pl.BlockSpec((tb, LANES), lambda i: (i, 0)), out_shape=SDS((r, LANES), F32),
        compiler_params=_cp("arbitrary"))(slots)


def _adamw(w, gslots, m, v, name):
    r, cdim = w.shape
    ns, rg = gslots.shape[0], gslots.shape[1]
    tb = r if (rg != r or r % 8) else _pick(r, (128, 64, 32, 16, 8))
    bc1, bc2 = 1.0 - ADAM_B1 ** ADAM_STEP, 1.0 - ADAM_B2 ** ADAM_STEP

    def body(w_ref, g_ref, m_ref, v_ref, go_ref, d_ref, mo_ref, vo_ref):
        g = g_ref[0, 0:tb, :].astype(F32)
        for k in range(1, ns):
            g = g + g_ref[k, 0:tb, :].astype(F32)
        mn = ADAM_B1 * m_ref[...] + (1.0 - ADAM_B1) * g
        vn = ADAM_B2 * v_ref[...] + (1.0 - ADAM_B2) * (g * g)
        go_ref[...] = g
        mo_ref[...] = mn
        vo_ref[...] = vn
        d_ref[...] = -ADAM_LR * ((mn / bc1) / (jnp.sqrt(vn / bc2) + ADAM_EPS) + ADAM_WD * w_ref[...])

    blk = pl.BlockSpec((tb, cdim), lambda i: (i, 0))
    gblk = pl.BlockSpec((ns, tb if rg == r else rg, cdim), lambda i: (0, i, 0))
    return pl.pallas_call(
        body, name=name, grid=(r // tb,), in_specs=[blk, gblk, blk, blk],
        out_specs=[blk] * 4, out_shape=[SDS((r, cdim), F32)] * 4, compiler_params=_cp("arbitrary"))(w, gslots, m, v)


def _pack(parts, row_mult):
    flat = jnp.concatenate([p.reshape(-1) for p in parts])
    n = flat.shape[0]
    rows = -(-n // LANES)
    rows = -(-rows // row_mult) * row_mult
    return jnp.pad(flat, (0, rows * LANES - n)).reshape(rows, LANES)


def _unpack(buf, shapes):
    flat = buf.reshape(-1)
    out, off = [], 0
    for s in shapes:
        n = math.prod(s)
        out.append(flat[off:off + n].reshape(s))
        off += n
    return out


def _pad_cols(a, width):
    return jnp.pad(a, ((0, 0), (0, width - a.shape[1])))


def _pad_lanes(a):
    return _pad_cols(a, LANES)


def _up128(n):
    return -(-n // LANES) * LANES


def kernel(x, c, ctx, c_ctx, w_mod, b_mod, norm1_g, w_in, b_gate, conv_qk, head_norm_g, sgu_ln_g, sgu_ln_b, w_s, b_s, w_branch_mlstm, w_branch_sgu, w_out, norm2_g, w_up, w_ffn_conv, w_down, final_g, loss_target, m_c_ctx, m_w_mod, m_b_mod, m_norm1_g, m_w_in, m_b_gate, m_conv_qk, m_head_norm_g, m_sgu_ln_g, m_sgu_ln_b, m_w_s, m_b_s, m_w_branch_mlstm, m_w_branch_sgu, m_w_out, m_norm2_g, m_w_up, m_w_ffn_conv, m_w_down, m_final_g, v_c_ctx, v_w_mod, v_b_mod, v_norm1_g, v_w_in, v_b_gate, v_conv_qk, v_head_norm_g, v_sgu_ln_g, v_sgu_ln_b, v_w_s, v_b_s, v_w_branch_mlstm, v_w_branch_sgu, v_w_out, v_norm2_g, v_w_up, v_w_ffn_conv, v_w_down, v_final_g):
    t, d = x.shape[1], x.shape[2]
    n_ctx = ctx.shape[1]
    s_rows = t + n_ctx
    nh = b_gate.shape[1] // 4
    md = head_norm_g.shape[1]
    dh = md // nh
    ng, sc = w_s.shape[1], w_s.shape[2]
    dff = w_down.shape[1] * N_DEV
    n_in = w_in.shape[2] * N_DEV
    assert md == d and sgu_ln_g.shape[1] == d and n_ctx == LCH and t % LCH == 0 and t % (8 * GRID_W) == 0
    assert n_in == 8 * d + 4 * nh and 4 * nh <= LANES
    me = 4 * lax.axis_index("x") + 2 * lax.axis_index("y") + lax.axis_index("c")

    n_mod, n_insh, n_upsh = w_mod.shape[2], w_in.shape[2], w_up.shape[2]
    p_mod, p_in, p_up = _up128(n_mod), _up128(n_insh), _up128(n_upsh)
    nq, nf = conv_qk.shape[2], w_ffn_conv.shape[3]
    ffn9 = w_ffn_conv[0].reshape(9, nf)
    colpack = jnp.concatenate([_pad_cols(_bf(w_mod[0]), p_mod), _pad_cols(_bf(w_in[0]), p_in)], axis=1)
    convpack = jnp.concatenate([jnp.pad(conv_qk[0], ((0, 13), (0, 0))), jnp.pad(ffn9, ((0, 7), (0, 0)))], axis=1)
    g_col, g_conv = _allgather([colpack, convpack])
    w_mod_f, w_main, w_gate = _assemble_cols(
        g_col, [(0, n_mod, [(0, 0, N_DEV * n_mod, 0)]),
                (p_mod, n_insh, [(1, 0, 3 * md, 0), (2, 3 * md, 4 * nh, 0), (1, 3 * md + 4 * nh, 5 * d, 3 * md)])],
        [N_MOD * d, 8 * d, LANES], "assemble_weights")
    convw, wconv9 = _assemble_cols(g_conv, [(0, nq, [(0, 0, N_DEV * nq, 0)]), (nq, nf, [(1, 0, N_DEV * nf, 0)])],
                                   [N_DEV * nq, N_DEV * nf], "assemble_conv_weights")
    zero = jnp.minimum(jnp.abs(g_conv[0, 0, 0]), 0.0)
    late_w = [_pad_cols(_bf(w_up[0] + zero), p_up), _bf(w_branch_mlstm[0]), _bf(w_branch_sgu[0]), _bf(w_out[0]), _bf(w_down[0])]
    late_state, late_tok = _exchange_start(late_w, False, "late_weights_start")

    cvec = jnp.concatenate([c, c_ctx[None], jnp.zeros((6, d), F32)], axis=0) + late_tok[0:1, 0:1]
    silu_v, mod = _modulation(cvec, w_mod_f, b_mod)
    mx = [mod[0:1, k * d:(k + 1) * d] for k in range(N_MOD)]
    mc = [mod[1:2, k * d:(k + 1) * d] for k in range(2)]
    xs = jnp.concatenate([x[0], ctx[0]], axis=0)
    hn, z_main, zg = _norm_mod_proj(xs, norm1_g, jnp.concatenate([mx[0], mx[1], mc[0], mc[1]], axis=0), w_main, w_gate, t, "in_proj")
    qscale = dh ** -0.5
    qk = _qk_conv(z_main, convw, t, md, qscale)
    bias = _pad_lanes(b_gate)
    fwd = _mlstm_fwd(qk, z_main, zg, bias, nh)
    hf, hb, states_f, states_b = fwd[0], fwd[1], fwd[2:5], fwd[5:8]
    late_own, late_land = _exchange_wait(late_state, fwd[4], "late_weights_wait")
    g_up, g_bm, g_bs, g_out, g_down = (_fill_own(ld, ow, me) for ld, ow in zip(late_land, late_own))
    (w_up_f,) = _assemble_cols(g_up, [(0, n_upsh, [(0, 0, 2 * dff, 0)])], [2 * dff], "assemble_w_up")
    wbm_f, wbs_f, wout_f = (g.reshape(d, d) for g in (g_bm, g_bs, g_out))
    w_down_f = g_down.reshape(dff, d)
    b_st = _pad_lanes(b_s[0].T)
    h1, ym, ys, pm, ps, y, out = _mixer_fwd(hf, hb, z_main, xs, head_norm_g, sgu_ln_g, sgu_ln_b, w_s[0], b_st, wbm_f, wbs_f,
                                            wout_f, mx[2], t, nh)
    hn2, ab = _norm_mod_proj(h1, norm2_g, jnp.concatenate([mx[3], mx[4], mx[3], mx[4]], axis=0), w_up_f, None, t, "up_proj")
    aconv, f, dh2, dffn, st_tail = _ffn_tail(ab, wconv9, w_down_f, h1, mx[5], final_g[None], loss_target[0], dff)

    db, dac = _ffn_bwd_gate(dffn, w_down_f, aconv, ab, dff)
    da, g_wconv9 = _ffn_conv_bwd(dac, ab, wconv9, dff)
    g_wdown = _wgrad(f, dffn, t, "wgrad_down")
    gwup_slots = _scatter_cols([_wgrad(hn2, da, t, "wgrad_up_a"), _wgrad(hn2, db, t, "wgrad_up_b")],
                               [(0, 0, dff, 0), (1, dff, dff, 0)], n_upsh, "scatter_grad_w_up")
    tkf = _pick(dff, (1408, 704, 384, 128))
    dh1, st_n2 = _proj_norm_bwd([(da, 0, w_up_f, 0, dff, tkf), (db, 0, w_up_f, dff, dff, tkf)], h1, 0, norm2_g, mx[4], dh2, t,
                                "up_proj_bwd")
    dz_rest, dhs, dout, dpm, dps, st_mix, g_ws, g_bst = _mixer_bwd(dh1, out, hf, hb, z_main, pm, ps, head_norm_g, sgu_ln_g,
                                                                    sgu_ln_b, w_s[0], b_st, wbm_f, wbs_f, wout_f, mx[2], t, nh)
    g_wout = _wgrad(y, dout, t, "wgrad_out")
    g_wbm = _wgrad(ym, dpm, t, "wgrad_branch_mlstm")
    g_wbs = _wgrad(ys, dps, t, "wgrad_branch_sgu")
    ex_a = [gwup_slots, g_wdown.reshape(N_DEV, dff // N_DEV, d), g_wbm.reshape(N_DEV, d // N_DEV, d),
            g_wbs.reshape(N_DEV, d // N_DEV, d), g_wout.reshape(N_DEV, d // N_DEV, d)]
    ex_a_state, ex_a_tok = _exchange_start(ex_a, True, "grad_exchange_a_start")
    dqkv_f, dg_f, dqkv_b, dg_b = _mlstm_bwd(qk, z_main, zg, bias + ex_a_tok[0:1, :], dhs, states_f, states_b, nh, t)
    dz_qkv, g_convqk = _qkv_conv_bwd(dqkv_f, dqkv_b, z_main, convw, t, md, qscale)
    dz_g, st_gate = _gate_grad_sum(dg_f, dg_b)
    gwin_slots = _scatter_cols(
        [_wgrad(hn, dz_qkv, s_rows, "wgrad_in_qkv"), _wgrad(hn, dz_g, s_rows, "wgrad_in_gate"), _wgrad(hn, dz_rest, t, "wgrad_in_rest")],
        [(0, 0, 3 * md, 0), (1, 3 * md, 4 * nh, 0), (2, 3 * md + 4 * nh, 5 * d, 0)], n_insh, "scatter_grad_w_in")
    gcq_slots = _scatter_cols([g_convqk], [(0, 0, 2 * md, 0)], nq, "scatter_grad_conv_qk")
    gcf_slots = _scatter_cols([g_wconv9], [(0, 0, dff, 0)], nf, "scatter_grad_ffn_conv")
    ex_b_state, ex_b_tok = _exchange_start([gwin_slots, gcq_slots, gcf_slots], True, "grad_exchange_b_start")
    tk = _pick(md, (1024, 512, 256))
    grad_x, st_n1x = _proj_norm_bwd(
        [(dz_qkv, 0, w_main, 0, 3 * md, tk), (dz_rest, 0, w_main, 3 * md, 5 * d, tk), (dz_g, 0, w_gate, 0, LANES, LANES)],
        xs, 0, norm1_g, mx[1] + ex_b_tok[0:1, 0:1], dh1, t, "in_proj_bwd")
    (st_n1c,) = _proj_norm_bwd([(dz_qkv, t, w_main, 0, 3 * md, tk), (dz_g, t, w_gate, 0, LANES, LANES)],
                               xs, t, norm1_g, mc[1] + ex_b_tok[0:1, 0:1], None, n_ctx, "in_proj_bwd_ctx")

    def received(state, name):
        sent, land = _exchange_wait(state, st_n1c, name)
        return [_fill_own(ld, lax.dynamic_index_in_dim(sn, me, 0, keepdims=False), me) for ld, sn in zip(land, sent)]

    rx_a = received(ex_a_state, "grad_exchange_a_wait")
    rx_b = received(ex_b_state, "grad_exchange_b_wait")
    recv = [rx_b[0], rx_a[0], rx_a[2], rx_a[3], rx_a[4], rx_a[1], rx_b[1], rx_b[2]]
    small_parts = [st_n1x[1], st_n1x[2], st_mix[0], st_n2[1], st_n2[2], st_tail[1],
                   st_n1c[1], st_n1c[2],
                   silu_v[0], st_n1x[0] + st_n1c[0], st_gate[0], st_mix[1], st_mix[2], st_mix[3],
                   g_ws.reshape(-1), g_bst[:, :ng].T.reshape(-1), st_n2[0], st_tail[0]]
    gsmall = _pack(small_parts, 8)
    (recv_small,) = _grad_exchange([], [gsmall])
    small_sum = _slot_sum(recv_small).reshape(-1)
    small_slots = recv_small.reshape(N_DEV, -1)
    o_silu, o_n1 = 8 * d, 9 * d
    ncol = N_MOD * d // N_DEV
    dmc_tot = small_sum[6 * d:8 * d][None]
    dmc_pad = jnp.concatenate([dmc_tot, jnp.zeros((1, 4 * d), F32)], axis=1)
    g_wmod, g_bmod, g_cctx = _mod_grads(
        small_slots[:, o_silu:o_silu + d], lax.dynamic_slice_in_dim(small_slots[:, :6 * d], me * ncol, ncol, axis=1),
        small_slots[:, :6 * d], dmc_tot, lax.dynamic_slice_in_dim(dmc_pad, me * ncol, ncol, axis=1), silu_v[1:2], c_ctx[None],
        w_mod_f[:, :2 * d])

    shard_w = (w_in, w_up, w_branch_mlstm, w_branch_sgu, w_out, w_down, conv_qk)
    shard_m = (m_w_in, m_w_up, m_w_branch_mlstm, m_w_branch_sgu, m_w_out, m_w_down, m_conv_qk)
    shard_v = (v_w_in, v_w_up, v_w_branch_mlstm, v_w_branch_sgu, v_w_out, v_w_down, v_conv_qk)
    shard_names = ("w_in", "w_up", "w_branch_mlstm", "w_branch_sgu", "w_out", "w_down", "conv_qk")
    shard_out = [[b[None] for b in _adamw(wa[0], recv[k], ma[0], va[0], "adamw_" + nm)]
                 for k, (wa, ma, va, nm) in enumerate(zip(shard_w, shard_m, shard_v, shard_names))]
    shard_out.append([b.reshape(w_ffn_conv.shape) for b in
                      _adamw(ffn9, recv[7], m_w_ffn_conv[0].reshape(9, nf), v_w_ffn_conv[0].reshape(9, nf), "adamw_w_ffn_conv")])
    mod_out = [b[None] for b in _adamw(w_mod[0], g_wmod[None], m_w_mod[0], v_w_mod[0], "adamw_w_mod")]

    def rep(cc, bm, n1, bg, hg, lg, lb, ws, bs, n2, fg):
        return [cc.reshape(-1), bm.reshape(-1), n1.reshape(-1), _pad_lanes(bg.reshape(1, -1)).reshape(-1), hg.reshape(-1),
                lg.reshape(-1), lb.reshape(-1), ws.reshape(-1), bs.reshape(-1), n2.reshape(-1), fg.reshape(-1)]

    o = o_n1
    g_rep_parts = [g_cctx, g_bmod]
    for n in (d, LANES, d, d, d, ng * sc * sc, ng * sc, d, d):
        g_rep_parts.append(small_sum[o:o + n])
        o += n
    rep_shapes = [(d,), (1, N_MOD * d), (1, d), (1, LANES), (1, d), (1, d), (1, d), (1, ng, sc, sc), (1, ng, sc), (1, d), (d,)]
    rep_out = _adamw(
        _pack(rep(c_ctx, b_mod, norm1_g, b_gate, head_norm_g, sgu_ln_g, sgu_ln_b, w_s, b_s, norm2_g, final_g), 8),
        _pack(g_rep_parts, 8)[None],
        _pack(rep(m_c_ctx, m_b_mod, m_norm1_g, m_b_gate, m_head_norm_g, m_sgu_ln_g, m_sgu_ln_b, m_w_s, m_b_s, m_norm2_g, m_final_g), 8),
        _pack(rep(v_c_ctx, v_b_mod, v_norm1_g, v_b_gate, v_head_norm_g, v_sgu_ln_g, v_sgu_ln_b, v_w_s, v_b_s, v_norm2_g, v_final_g), 8),
        "adamw_replicated")

    def assemble(k):
        r = _unpack(rep_out[k], rep_shapes)
        s = [o[k] for o in shard_out]
        return [r[0], mod_out[k], r[1], r[2], s[0], r[3][:, :4 * nh], s[6], r[4], r[5], r[6], r[7], r[8], s[2], s[3], s[4], r[9],
                s[1], s[7], s[5], r[10]]

    loss = lax.psum(st_tail[2, 0], ("x", "y", "c"))
    outs = [loss, grad_x[None]]
    for k in range(4):
        outs += assemble(k)
    return tuple(outs)
```

```python
import functools
import math

import jax
import jax.numpy as jnp
from jax import lax
from jax.experimental import pallas as pl
from jax.experimental.pallas import tpu as pltpu

F32, BF16 = jnp.float32, jnp.bfloat16
EPS = 1e-6
M_INIT = -1e30
NEG = -1e30
GRID_W = 64
LCH = 256
N_MOD = 6
N_DEV = 8
LANES = 128
ADAM_LR, ADAM_B1, ADAM_B2, ADAM_EPS, ADAM_WD, ADAM_STEP = 0.001, 0.9, 0.999, 1e-08, 0.01, 10
GELU_C = math.sqrt(2.0 / math.pi)
GELU_A = 0.044715
VMEM_LIMIT = 56 * 1024 * 1024
HI = lax.Precision.HIGHEST
SDS = jax.ShapeDtypeStruct
MESH_ID = pl.DeviceIdType.MESH


def _pick(n, cands):
    for c in cands:
        if n % c == 0:
            return c
    raise ValueError(f"no block size for {n} in {cands}")


def _cp(*sem):
    return pltpu.CompilerParams(dimension_semantics=sem if sem else None, vmem_limit_bytes=VMEM_LIMIT)


def _sigmoid(x):
    return 0.5 * jnp.tanh(0.5 * x) + 0.5


def _split3(x):
    hi = x.astype(BF16)
    r = x - hi.astype(F32)
    mid = r.astype(BF16)
    return hi, mid, (r - mid.astype(F32)).astype(BF16)


def _mask_dot(mask_b, x):
    hi, mid, lo = _split3(x)
    return (_dot(mask_b, lo) + _dot(mask_b, mid)) + _dot(mask_b, hi)


def _mask_dot_t(mask_b, x):
    hi, mid, lo = _split3(x)
    return (_dot_tn(mask_b, lo) + _dot_tn(mask_b, mid)) + _dot_tn(mask_b, hi)


def _gelu(x):
    return 0.5 * x * (1.0 + jnp.tanh(GELU_C * (x + GELU_A * x * x * x)))


def _gelu_grad(x):
    t = jnp.tanh(GELU_C * (x + GELU_A * x * x * x))
    return 0.5 * (1.0 + t) + 0.5 * x * (1.0 - t * t) * GELU_C * (1.0 + 3.0 * GELU_A * x * x)


def _log_sigmoid(x):
    return jnp.minimum(x, 0.0) - jnp.log(1.0 + jnp.exp(-jnp.abs(x)))


def _dot(a, b):
    return jnp.dot(a, b, preferred_element_type=F32)


def _dot_nt(a, b):
    return lax.dot_general(a, b, (((1,), (1,)), ((), ())), preferred_element_type=F32)


def _dot_tn(a, b):
    return lax.dot_general(a, b, (((0,), (0,)), ((), ())), preferred_element_type=F32)


def _bf(x):
    return x.astype(BF16)


def _allgather(arrs):
    na = len(arrs)

    def body(*refs):
        x_refs, o_refs = refs[:na], refs[na:2 * na]
        send_sems, recv_sems, local_sems = refs[2 * na:]
        x, y, c = lax.axis_index("x"), lax.axis_index("y"), lax.axis_index("c")
        me, sibling = (x, y, c), (x, y, 1 - c)
        chips = [(1 - x, y), (x, 1 - y), (1 - x, 1 - y)]

        def copy(a, k, block, to, src=None):
            slot = o_refs[a].at[4 * block[0] + 2 * block[1] + block[2]]
            return pltpu.make_async_remote_copy(
                src_ref=slot if src is None else src, dst_ref=slot, send_sem=send_sems.at[7 * a + k],
                recv_sem=recv_sems.at[7 * a + k], device_id=to, device_id_type=MESH_ID)

        mine = [pltpu.make_async_copy(x_refs[a], o_refs[a].at[4 * x + 2 * y + c], local_sems.at[a]) for a in range(na)]
        for cp in mine:
            cp.start()
        first = []
        for a in range(na):
            first.append(copy(a, 0, me, sibling, src=x_refs[a]))
            first += [copy(a, 1 + j, me, (*chip, c), src=x_refs[a]) for j, chip in enumerate(chips)]
        for cp in first:
            cp.start()
        passed = []
        for j, chip in enumerate(chips):
            for a in range(na):
                copy(a, 1 + j, (*chip, c), me).wait_recv()
                passed.append(copy(a, 4 + j, (*chip, c), sibling))
                passed[-1].start()
        for a in range(na):
            copy(a, 0, sibling, me).wait_recv()
            for j, chip in enumerate(chips):
                copy(a, 4 + j, (*chip, 1 - c), me).wait_recv()
        for cp in first + passed:
            cp.wait_send()
        for cp in mine:
            cp.wait()

    anyspec = pl.BlockSpec(memory_space=pl.ANY)
    return pl.pallas_call(
        body, name="weights_allgather",
        out_shape=[SDS((N_DEV,) + a.shape, a.dtype) for a in arrs],
        in_specs=[anyspec] * na, out_specs=[anyspec] * na,
        scratch_shapes=[pltpu.SemaphoreType.DMA((7 * na,)), pltpu.SemaphoreType.DMA((7 * na,)), pltpu.SemaphoreType.DMA((na,))],
    )(*arrs)


def _grad_exchange(per_dest, shared):
    nd, ns = len(per_dest), len(shared)
    na = nd + ns

    def body(*refs):
        in_refs, out_refs = refs[:na], refs[na:2 * na]
        send_sems, recv_sems, local_sems = refs[2 * na:]
        x, y, c = lax.axis_index("x"), lax.axis_index("y"), lax.axis_index("c")
        me = 4 * x + 2 * y + c

        def src(a, idx):
            return in_refs[a].at[idx] if a < nd else in_refs[a]

        loc = [pltpu.make_async_copy(src(a, me), out_refs[a].at[me], local_sems.at[a]) for a in range(na)]
        for cp in loc:
            cp.start()
        sends, recvs = [], []
        for k in range(1, N_DEV):
            px = 1 - x if k & 4 else x
            py = 1 - y if k & 2 else y
            pc = 1 - c if k & 1 else c
            peer, pidx = (px, py, pc), 4 * px + 2 * py + pc
            for a in range(na):
                sem = 7 * a + k - 1
                sends.append(pltpu.make_async_remote_copy(
                    src_ref=src(a, pidx), dst_ref=out_refs[a].at[me], send_sem=send_sems.at[sem],
                    recv_sem=recv_sems.at[sem], device_id=peer, device_id_type=MESH_ID))
                recvs.append(pltpu.make_async_remote_copy(
                    src_ref=src(a, pidx), dst_ref=out_refs[a].at[pidx], send_sem=send_sems.at[sem],
                    recv_sem=recv_sems.at[sem], device_id=peer, device_id_type=MESH_ID))
        for cp in sends:
            cp.start()
        for cp in recvs:
            cp.wait_recv()
        for cp in sends:
            cp.wait_send()
        for cp in loc:
            cp.wait()

    anyspec = pl.BlockSpec(memory_space=pl.ANY)
    return pl.pallas_call(
        body, name="grad_exchange",
        out_shape=[SDS(a.shape, a.dtype) for a in per_dest] + [SDS((N_DEV,) + a.shape, a.dtype) for a in shared],
        in_specs=[anyspec] * na, out_specs=[anyspec] * na,
        scratch_shapes=[pltpu.SemaphoreType.DMA((7 * na,)), pltpu.SemaphoreType.DMA((7 * na,)), pltpu.SemaphoreType.DMA((na,))],
    )(*per_dest, *shared)


_HBM_SPEC = pl.BlockSpec(memory_space=pltpu.HBM)
_SEM_SPEC = pl.BlockSpec(memory_space=pltpu.SEMAPHORE)
_EFFECT = pltpu.SideEffectType.DATAFLOW_SIDE_EFFECTING


def _peer_list(x, y, c):
    out = []
    for k in range(1, N_DEV):
        px = 1 - x if k & 4 else x
        py = 1 - y if k & 2 else y
        pc = 1 - c if k & 1 else c
        out.append(((px, py, pc), 4 * px + 2 * py + pc))
    return out


def _split_copies(src, land, send_sems, recv_sems, per_dest, receive):
    x, y, c = lax.axis_index("x"), lax.axis_index("y"), lax.axis_index("c")
    me = 4 * x + 2 * y + c
    out = []
    for k, (peer, pidx) in enumerate(_peer_list(x, y, c)):
        for a in range(len(src)):
            out.append(pltpu.make_async_remote_copy(
                src_ref=src[a].at[pidx] if per_dest else src[a], dst_ref=land[a].at[pidx if receive else me],
                send_sem=send_sems.at[7 * a + k], recv_sem=recv_sems.at[7 * a + k], device_id=peer, device_id_type=MESH_ID))
    return out


def _own_copies(src, land, own_sems, per_dest):
    me = 4 * lax.axis_index("x") + 2 * lax.axis_index("y") + lax.axis_index("c")
    return [pltpu.make_async_copy(src[a].at[me] if per_dest else src[a], land[a].at[me], own_sems.at[a]) for a in range(len(src))]


def _exchange_start(arrs, per_dest, name):
    na = len(arrs)
    land_shapes = [a.shape if per_dest else (N_DEV,) + a.shape for a in arrs]
    lands = [pltpu.with_memory_space_constraint(lax.empty(s, a.dtype), pltpu.HBM) for s, a in zip(land_shapes, arrs)]

    def body(*refs):
        src, land = refs[:na], refs[na:2 * na]
        send_sems, recv_sems, own_sems, token = refs[2 * na], refs[2 * na + 1], refs[2 * na + 2], refs[-1]
        for cp in _split_copies(src, land, send_sems, recv_sems, per_dest, False) + _own_copies(src, land, own_sems, per_dest):
            cp.start()
        token[...] = jnp.zeros_like(token)

    outs = pl.pallas_call(
        body, name=name,
        out_shape=[pltpu.SemaphoreType.DMA((7 * na,)), pltpu.SemaphoreType.DMA((7 * na,)), pltpu.SemaphoreType.DMA((na,))]
        + [pltpu.HBM(a.shape, a.dtype) for a in arrs] + [pltpu.HBM(s, a.dtype) for s, a in zip(land_shapes, arrs)]
        + [SDS((8, LANES), F32)],
        in_specs=[_HBM_SPEC] * (2 * na),
        out_specs=[_SEM_SPEC] * 3 + [_HBM_SPEC] * (2 * na) + [pl.BlockSpec(memory_space=pltpu.VMEM)],
        input_output_aliases={k: 3 + k for k in range(2 * na)},
        compiler_params=pltpu.CompilerParams(has_side_effects=_EFFECT),
    )(*[pltpu.with_memory_space_constraint(a, pltpu.HBM) for a in arrs], *lands)
    return (na, per_dest, outs[:-1]), outs[-1]


def _exchange_wait(state, after, name):
    na, per_dest, started = state

    def body(*refs):
        src, land = refs[:na], refs[na:2 * na]
        send_sems, recv_sems, own_sems = refs[2 * na], refs[2 * na + 1], refs[2 * na + 2]
        for cp in _split_copies(src, land, send_sems, recv_sems, per_dest, True):
            cp.wait_send()
            cp.wait_recv()
        for cp in _own_copies(src, land, own_sems, per_dest):
            cp.wait()

    bufs = started[3:]
    outs = pl.pallas_call(
        body, name=name,
        out_shape=[pltpu.HBM(b.shape, b.dtype) for b in bufs],
        in_specs=[_HBM_SPEC] * (2 * na) + [_SEM_SPEC] * 3 + [pl.BlockSpec(memory_space=pl.ANY)],
        out_specs=[_HBM_SPEC] * (2 * na),
        input_output_aliases={k: k for k in range(2 * na)},
        compiler_params=pltpu.CompilerParams(has_side_effects=_EFFECT),
    )(*bufs, started[0], started[1], started[2], after)
    return outs[na:]


def _col_pieces(n, segments):
    out = []
    for j in range(N_DEV):
        lo, hi = j * n, (j + 1) * n
        for (k, s0, w, c0) in segments:
            a, b = max(lo, s0), min(hi, s0 + w)
            if a < b:
                out.append((j, a - lo, b - lo, k, c0 + a - s0, c0 + b - s0))
    return out


def _assemble_cols(slots, groups, out_widths, name):
    r, p = slots.shape[1], slots.shape[2]
    tb = _pick(r, (128, 64, 32, 16, 8))
    covered = [0] * len(out_widths)
    for (_, n, segs) in groups:
        for (k, _, w, _) in segs:
            covered[k] += w

    def body(s_ref, *o_refs):
        for k, wd in enumerate(out_widths):
            if covered[k] < wd:
                o_refs[k][...] = jnp.zeros_like(o_refs[k])
        for (off, n, segs) in groups:
            for (j, a0, a1, k, d0, d1) in _col_pieces(n, segs):
                o_refs[k][:, d0:d1] = s_ref[j, :, off + a0:off + a1]

    return pl.pallas_call(
        body, name=name, grid=(r // tb,), in_specs=[pl.BlockSpec((N_DEV, tb, p), lambda i: (0, i, 0))],
        out_specs=[pl.BlockSpec((tb, w), lambda i: (i, 0)) for w in out_widths],
        out_shape=[SDS((r, w), slots.dtype) for w in out_widths], compiler_params=_cp("arbitrary"))(slots)


def _scatter_cols(pieces, segments, n, name):
    r = pieces[0].shape[0]
    tb = _pick(r, (128, 64, 32, 16, 8))

    def body(*refs):
        p_refs, o_ref = refs[:-1], refs[-1]
        for (j, a0, a1, k, d0, d1) in _col_pieces(n, segments):
            o_ref[j, :, a0:a1] = p_refs[k][:, d0:d1]

    return pl.pallas_call(
        body, name=name, grid=(r // tb,), in_specs=[pl.BlockSpec((tb, a.shape[1]), lambda i: (i, 0)) for a in pieces],
        out_specs=pl.BlockSpec((N_DEV, tb, n), lambda i: (0, i, 0)), out_shape=SDS((N_DEV, r, n), pieces[0].dtype),
        compiler_params=_cp("arbitrary"))(*pieces)


def _modulation(cvec, w_mod, b_mod):
    d, n = w_mod.shape

    def body(c_ref, w_ref, b_ref, s_ref, o_ref):
        cv = c_ref[...]
        s = cv * _sigmoid(cv)
        s_ref[...] = s
        o_ref[...] = _dot(_bf(s), w_ref[...]) + b_ref[...]

    return pl.pallas_call(body, name="modulation", out_shape=(SDS((8, d), F32), SDS((8, n), F32)),
                          compiler_params=_cp())(cvec, w_mod, b_mod)


def _norm_mod_proj(xs, g, shsc, w_main, w_gate, t_rows, name):
    s_rows, d = xs.shape
    n = w_main.shape[1]
    tb = _pick(s_rows, (1280, 1024, 256))
    cb = _pick(n, (1408, 1024, 768, 512, 384, 256, 128))
    gate = w_gate is not None

    def body(*refs):
        if gate:
            x_ref, g_ref, ss_ref, wm_ref, wg_ref, hn_ref, z_ref, zg_ref, hn_sc = refs
        else:
            x_ref, g_ref, ss_ref, wm_ref, hn_ref, z_ref, hn_sc = refs
        i, j = pl.program_id(0), pl.program_id(1)

        @pl.when(j == 0)
        def _():
            x = x_ref[...]
            r = lax.rsqrt(jnp.mean(x * x, axis=-1, keepdims=True) + EPS)
            row = i * tb + lax.broadcasted_iota(jnp.int32, (tb, 1), 0)
            isx = row < t_rows
            sh = jnp.where(isx, ss_ref[0:1, :], ss_ref[2:3, :])
            sc = jnp.where(isx, ss_ref[1:2, :], ss_ref[3:4, :])
            hb = _bf((x * r * g_ref[...]) * (1.0 + sc) + sh)
            hn_sc[...] = hb
            hn_ref[...] = hb
            if gate:
                zg_ref[...] = _dot(hb, wg_ref[...])

        z_ref[...] = _bf(_dot(hn_sc[...], wm_ref[...]))

    in_specs = [pl.BlockSpec((tb, d), lambda i, j: (i, 0)), pl.BlockSpec((1, d), lambda i, j: (0, 0)),
                pl.BlockSpec((4, d), lambda i, j: (0, 0)), pl.BlockSpec((d, cb), lambda i, j: (0, j))]
    out_specs = [pl.BlockSpec((tb, d), lambda i, j: (i, 0)), pl.BlockSpec((tb, cb), lambda i, j: (i, j))]
    out_shape = [SDS((s_rows, d), BF16), SDS((s_rows, n), BF16)]
    args = [xs, g, shsc, w_main]
    if gate:
        in_specs.append(pl.BlockSpec((d, LANES), lambda i, j: (0, 0)))
        out_specs.append(pl.BlockSpec((tb, LANES), lambda i, j: (i, 0)))
        out_shape.append(SDS((s_rows, LANES), F32))
        args.append(w_gate)
    return pl.pallas_call(
        body, name=name, grid=(s_rows // tb, n // cb), in_specs=in_specs, out_specs=out_specs, out_shape=out_shape,
        scratch_shapes=[pltpu.VMEM((tb, d), BF16)], compiler_params=_cp("arbitrary", "arbitrary"))(*args)


def _seg_masks(row, t_rows, s_rows):
    prev_ok = (row != 0) & (row != t_rows)
    next_ok = (row != t_rows - 1) & (row != s_rows - 1)
    return prev_ok, next_ok


def _shift_rows(z, halo_prev, halo_next, tb):
    loc = lax.broadcasted_iota(jnp.int32, (tb, 1), 0)
    zp = jnp.where(loc == 0, halo_prev, pltpu.roll(z, 1, 0))
    zn = jnp.where(loc == tb - 1, halo_next, pltpu.roll(z, tb - 1, 0))
    return zp, zn


def _qk_conv(z_main, conv_w, t_rows, md, qscale):
    s_rows = z_main.shape[0]
    tb = _pick(s_rows, (1280, 1024, 256))
    cb = _pick(md, (512, 256, 128))
    nb8 = tb // 8

    def body(zm, zp, zn, w_ref, o_ref):
        i, j = pl.program_id(0), pl.program_id(1)
        z = zm[...].astype(F32)
        zprev, znext = _shift_rows(z, zp[7:8, :].astype(F32), zn[0:1, :].astype(F32), tb)
        row = i * tb + lax.broadcasted_iota(jnp.int32, (tb, 1), 0)
        prev_ok, next_ok = _seg_masks(row, t_rows, s_rows)
        pre = (w_ref[0:1, :] * jnp.where(prev_ok, zprev, 0.0) + w_ref[1:2, :] * z
               + w_ref[2:3, :] * jnp.where(next_ok, znext, 0.0))
        scale = jnp.where(j * cb < md, qscale, 1.0)
        o_ref[...] = _bf(pre * _sigmoid(pre) * scale)

    return pl.pallas_call(
        body, name="qk_conv", grid=(s_rows // tb, 2 * md // cb),
        in_specs=[pl.BlockSpec((tb, cb), lambda i, j: (i, j)),
                  pl.BlockSpec((8, cb), lambda i, j: (jnp.maximum(i * nb8 - 1, 0), j)),
                  pl.BlockSpec((8, cb), lambda i, j: (jnp.minimum((i + 1) * nb8, s_rows // 8 - 1), j)),
                  pl.BlockSpec((8, cb), lambda i, j: (0, j))],
        out_specs=pl.BlockSpec((tb, cb), lambda i, j: (i, j)),
        out_shape=SDS((s_rows, 2 * md), BF16), compiler_params=_cp("arbitrary", "arbitrary"))(z_main, z_main, z_main, conv_w)


def _chunk_gates(gates, bias, rev):
    ln = gates.shape[0]
    gz = gates + bias
    logf = _log_sigmoid(gz)
    r_id = lax.broadcasted_iota(jnp.int32, (ln, ln), 0)
    c_id = lax.broadcasted_iota(jnp.int32, (ln, ln), 1)
    mask = (c_id >= r_id) if rev else (c_id <= r_id)
    mb = mask.astype(F32).astype(BF16)
    b_all = _mask_dot(mb, logf)
    g_all = jnp.sum(logf, axis=0, keepdims=True)
    return gz, b_all, b_all.T, gz.T, g_all, mask, mb


def _head_weights(b_col, b_row, i_row, m_in, mask):
    d = jnp.where(mask, b_col - b_row + i_row, NEG)
    inter = b_col + m_in
    m_row = jnp.maximum(inter, jnp.max(d, axis=1, keepdims=True))
    return jnp.exp(d - m_row), jnp.exp(inter - m_row), m_row


def _head_state_coeffs(g, b_col, i_col, m_in):
    a = g - b_col + i_col
    m_new = jnp.maximum(g + m_in, jnp.max(a, axis=0, keepdims=True))
    return jnp.exp(g + m_in - m_new), jnp.exp(a - m_new), m_new


def _mlstm_fwd(qk, z_main, zg, bias, nh):
    s_rows = qk.shape[0]
    md = qk.shape[1] // 2
    dh = md // nh
    nc = s_rows // LCH
    ln = LCH

    def chunk_f(i):
        return jnp.where(i == 0, nc - 1, i - 1)

    def chunk_b(i):
        return jnp.where(i == 0, nc - 1, nc - 1 - i)

    def body(qf, kf, vf, gf, qb, kb, vb, gb, bias_ref, hf_ref, hb_ref, cf_ref, nf_ref, mf_ref, cb_ref, nb_ref, mb_ref,
             c_sc, n_sc, m_sc):
        i = pl.program_id(0)

        @pl.when(i == 0)
        def _():
            c_sc[...] = jnp.zeros_like(c_sc)
            n_sc[...] = jnp.zeros_like(n_sc)
            m_sc[...] = jnp.full(m_sc.shape, M_INIT, F32)

        for dr, (q_ref, k_ref, v_ref, g_ref, h_ref, c_out, n_out, m_out) in enumerate(
                ((qf, kf, vf, gf, hf_ref, cf_ref, nf_ref, mf_ref), (qb, kb, vb, gb, hb_ref, cb_ref, nb_ref, mb_ref))):
            gz, b_all, b_t, g_t, g_all, mask, _ = _chunk_gates(g_ref[...], bias_ref[...], dr == 1)
            for h in range(nh):
                ci, cf = 2 * dr * nh + h, (2 * dr + 1) * nh + h
                sl = slice(h * dh, (h + 1) * dh)
                q, k, v = q_ref[:, sl], k_ref[:, sl], v_ref[:, sl]
                c_in, n_in, m_in = c_sc[dr, h], n_sc[dr, h, 0:1, :], m_sc[dr, h, 0:1, 0:1]
                c_out[sl, :] = c_in
                n_out[:, sl] = n_sc[dr, h]
                m_out[h] = m_sc[dr, h]
                b_col, b_row, i_col, i_row = b_all[:, cf:cf + 1], b_t[cf:cf + 1, :], gz[:, ci:ci + 1], g_t[ci:ci + 1, :]
                g = g_all[:, cf:cf + 1]
                w, w_int, m_row = _head_weights(b_col, b_row, i_row, m_in, mask)
                s_mat = _dot_nt(q, k) * w
                num = _dot(_bf(s_mat), v) + w_int * _dot(q, _bf(c_in))
                den = jnp.sum(s_mat, axis=1, keepdims=True) + w_int * jnp.sum(q.astype(F32) * n_in, axis=1, keepdims=True)
                h_ref[:, sl] = _bf(num / jnp.maximum(jnp.abs(den), jnp.exp(-m_row)))
                a_old, coef, m_new = _head_state_coeffs(g, b_col, i_col, m_in)
                kw = k.astype(F32) * coef
                c_sc[dr, h] = a_old * c_in + _dot_tn(_bf(kw), v)
                n_sc[dr, h] = jnp.broadcast_to(a_old * n_in + jnp.sum(kw, axis=0, keepdims=True), (8, dh))
                m_sc[dr, h] = jnp.broadcast_to(m_new, (8, LANES))

    def tok(cfn, col):
        return pl.BlockSpec((ln, md), lambda i: (cfn(i), col))

    def gat(cfn):
        return pl.BlockSpec((ln, LANES), lambda i: (cfn(i), 0))

    def st(cfn, shape):
        return pl.BlockSpec((None,) + shape, lambda i: (cfn(i),) + (0,) * len(shape))

    st_shapes = ((nh * dh, dh), (8, md), (nh, 8, LANES))
    return pl.pallas_call(
        body, name="mlstm_fwd", grid=(nc,),
        in_specs=[tok(chunk_f, 0), tok(chunk_f, 1), tok(chunk_f, 2), gat(chunk_f),
                  tok(chunk_b, 0), tok(chunk_b, 1), tok(chunk_b, 2), gat(chunk_b),
                  pl.BlockSpec((1, LANES), lambda i: (0, 0))],
        out_specs=[tok(chunk_f, 0), tok(chunk_b, 0)] + [st(chunk_f, s) for s in st_shapes] + [st(chunk_b, s) for s in st_shapes],
        out_shape=[SDS((s_rows, md), BF16)] * 2 + [SDS((nc,) + s, F32) for s in st_shapes] * 2,
        scratch_shapes=[pltpu.VMEM((2, nh, dh, dh), F32), pltpu.VMEM((2, nh, 8, dh), F32), pltpu.VMEM((2, nh, 8, LANES), F32)],
        compiler_params=_cp("arbitrary"))(qk, qk, z_main, zg, qk, qk, z_main, zg, bias)


def _head_rms(hs, nh, dh):
    parts, scales = [], []
    for h in range(nh):
        hh = hs[:, h * dh:(h + 1) * dh]
        r = lax.rsqrt(jnp.mean(hh * hh, axis=-1, keepdims=True) + EPS)
        parts.append(hh * r)
        scales.append(r)
    return jnp.concatenate(parts, axis=1), scales


def _layer_norm(v):
    vc = v - jnp.mean(v, axis=-1, keepdims=True)
    r = lax.rsqrt(jnp.mean(vc * vc, axis=-1, keepdims=True) + EPS)
    return vc * r, r


def _sgu_mix(vnb, ws_ref, bs_ref, tb, ng, gd, sc):
    rows = []
    for ch in range(tb // sc):
        cols = []
        for g in range(ng):
            blk = vnb[ch * sc:(ch + 1) * sc, g * gd:(g + 1) * gd]
            cols.append(_dot(_bf(ws_ref[g]), blk) + bs_ref[:, g:g + 1])
        rows.append(jnp.concatenate(cols, axis=1))
    return jnp.concatenate(rows, axis=0)


def _mixer_fwd(hf, hb, z_main, xs, hg, lng, lnb, w_s, b_st, wbm, wbs, wout, mx2, t_rows, nh):
    d = xs.shape[1]
    ng, sc = w_s.shape[0], w_s.shape[1]
    dh, gd = d // nh, d // ng
    tb = _pick(t_rows, (256,))

    def body(hf_ref, hb_ref, zo, zu, zv, zgm, zgg, x_ref, hg_ref, lng_ref, lnb_ref, ws_ref, bs_ref, wbm_ref, wbs_ref,
             wo_ref, mx2_ref, h1_ref, ym_ref, ys_ref, pm_ref, ps_ref, y_ref, out_ref):
        hs = hf_ref[...].astype(F32) + hb_ref[...].astype(F32)
        hn, _ = _head_rms(hs, nh, dh)
        ym = _bf(_sigmoid(zo[...].astype(F32)) * (hn * hg_ref[...]))
        ym_ref[...] = ym
        vhat, _ = _layer_norm(_gelu(zv[...].astype(F32)))
        vnb = _bf(vhat * lng_ref[...] + lnb_ref[...])
        ys = _bf(_gelu(zu[...].astype(F32)) * _sgu_mix(vnb, ws_ref, bs_ref, tb, ng, gd, sc))
        ys_ref[...] = ys
        pm = _dot(ym, wbm_ref[...])
        ps = _dot(ys, wbs_ref[...])
        pm_ref[...] = _bf(pm)
        ps_ref[...] = _bf(ps)
        y = _bf(_sigmoid(zgm[...].astype(F32)) * pm + _sigmoid(zgg[...].astype(F32)) * ps)
        y_ref[...] = y
        out = _dot(y, wo_ref[...])
        out_ref[...] = _bf(out)
        h1_ref[...] = x_ref[...] + mx2_ref[...] * out

    def tok(col):
        return pl.BlockSpec((tb, d), lambda i: (i, col))

    def full(shape):
        return pl.BlockSpec(shape, lambda i: (0,) * len(shape))

    return pl.pallas_call(
        body, name="mixer_fwd", grid=(t_rows // tb,),
        in_specs=[tok(0), tok(0), tok(3), tok(4), tok(5), tok(6), tok(7), tok(0), full((1, d)), full((1, d)), full((1, d)),
                  full((ng, sc, sc)), full((sc, LANES)), full((d, d)), full((d, d)), full((d, d)), full((1, d))],
        out_specs=[tok(0)] * 7,
        out_shape=[SDS((t_rows, d), F32)] + [SDS((t_rows, d), BF16)] * 6,
        compiler_params=_cp("arbitrary"))(hf, hb, z_main, z_main, z_main, z_main, z_main, xs, hg, lng, lnb, w_s, b_st,
                                          wbm, wbs, wout, mx2)


def _grid_taps(a_ext, n_ext):
    col = lax.broadcasted_iota(jnp.int32, (n_ext, 1), 0) % GRID_W
    left = jnp.where(col != 0, pltpu.roll(a_ext, 1, 0), 0.0)
    right = jnp.where(col != GRID_W - 1, pltpu.roll(a_ext, n_ext - 1, 0), 0.0)
    return left, right


def _with_halo(prev, main, nxt, i, ni, tb):
    ext = jnp.concatenate([prev, main, nxt], axis=0).astype(F32)
    pos = lax.broadcasted_iota(jnp.int32, (tb + 2 * GRID_W, 1), 0)
    inside = ((pos >= GRID_W) | (i > 0)) & ((pos < tb + GRID_W) | (i < ni - 1))
    return jnp.where(inside, ext, 0.0)


def _halo_specs(tb, cb, t_rows, col0=0):
    nh64 = tb // GRID_W
    return [pl.BlockSpec((tb, cb), lambda i, j: (i, col0 + j)),
            pl.BlockSpec((GRID_W, cb), lambda i, j: (jnp.maximum(i * nh64 - 1, 0), col0 + j)),
            pl.BlockSpec((GRID_W, cb), lambda i, j: (jnp.minimum((i + 1) * nh64, t_rows // GRID_W - 1), col0 + j))]


def _ffn_tail(ab, w_conv9, w_down, h1, mx5, gfin, target, dff):
    t_rows, d = h1.shape
    tb = _pick(t_rows, (256,))
    cb = _pick(dff, (1408, 256, 128))
    ni, nj = t_rows // tb, dff // cb
    n_ext = tb + 2 * GRID_W

    def body(am, ap, an, b_ref, wc_ref, wd_ref, h1_ref, mx5_ref, gf_ref, tg_ref, ac_ref, f_ref, dh2_ref, dffn_ref, st_ref, acc):
        i, j = pl.program_id(0), pl.program_id(1)
        a_ext = _with_halo(ap[...], am[...], an[...], i, ni, tb)
        left, right = _grid_taps(a_ext, n_ext)
        conv = jnp.zeros((tb, cb), F32)
        for di in range(3):
            o = di * GRID_W
            conv = conv + (wc_ref[3 * di:3 * di + 1, :] * left[o:o + tb] + wc_ref[3 * di + 1:3 * di + 2, :] * a_ext[o:o + tb]
                           + wc_ref[3 * di + 2:3 * di + 3, :] * right[o:o + tb])
        ac_ref[...] = _bf(conv)
        fb = _bf(conv * _sigmoid(conv) * b_ref[...].astype(F32))
        f_ref[...] = fb

        @pl.when(j == 0)
        def _():
            acc[...] = jnp.zeros_like(acc)

        @pl.when((i == 0) & (j == 0))
        def _():
            st_ref[...] = jnp.zeros_like(st_ref)

        acc[...] += _dot(fb, wd_ref[...])

        @pl.when(j == nj - 1)
        def _():
            ffn = acc[...]
            h2 = h1_ref[...] + mx5_ref[...] * ffn
            r = lax.rsqrt(jnp.mean(h2 * h2, axis=-1, keepdims=True) + EPS)
            xn = h2 * r
            e = xn * gf_ref[...] - tg_ref[...]
            loss = 0.5 * jnp.sum(jnp.sum(e * e, axis=1, keepdims=True), axis=0, keepdims=True) / d
            dy = e * (1.0 / d)
            dxn = dy * gf_ref[...]
            dh2 = r * (dxn - xn * jnp.mean(dxn * xn, axis=-1, keepdims=True))
            dh2_ref[...] = dh2
            dffn_ref[...] = _bf(dh2 * mx5_ref[...])
            st_ref[...] += jnp.concatenate(
                [jnp.sum(dy * xn, axis=0, keepdims=True), jnp.sum(dh2 * ffn, axis=0, keepdims=True),
                 jnp.broadcast_to(loss, (1, d)), jnp.zeros((5, d), F32)], axis=0)

    def tokd():
        return pl.BlockSpec((tb, d), lambda i, j: (i, 0))

    def rowd():
        return pl.BlockSpec((1, d), lambda i, j: (0, 0))

    return pl.pallas_call(
        body, name="ffn_tail", grid=(ni, nj),
        in_specs=_halo_specs(tb, cb, t_rows) + [pl.BlockSpec((tb, cb), lambda i, j: (i, nj + j)),
                                                pl.BlockSpec((16, cb), lambda i, j: (0, j)),
                                                pl.BlockSpec((cb, d), lambda i, j: (j, 0)), tokd(), rowd(), rowd(), tokd()],
        out_specs=[pl.BlockSpec((tb, cb), lambda i, j: (i, j)), pl.BlockSpec((tb, cb), lambda i, j: (i, j)), tokd(), tokd(),
                   pl.BlockSpec((8, d), lambda i, j: (0, 0))],
        out_shape=[SDS((t_rows, dff), BF16), SDS((t_rows, dff), BF16), SDS((t_rows, d), F32), SDS((t_rows, d), BF16),
                   SDS((8, d), F32)],
        scratch_shapes=[pltpu.VMEM((tb, d), F32)],
        compiler_params=_cp("arbitrary", "arbitrary"))(ab, ab, ab, ab, w_conv9, w_down, h1, mx5, gfin, target)


def _ffn_bwd_gate(dffn, w_down, aconv, ab, dff):
    t_rows, d = dffn.shape
    tb = _pick(t_rows, (256,))
    cb = _pick(dff, (1408, 256, 128))
    nj = dff // cb

    def body(g_ref, wd_ref, ac_ref, b_ref, db_ref, dac_ref):
        df = _dot_nt(g_ref[...], wd_ref[...])
        ac = ac_ref[...].astype(F32)
        sa = _sigmoid(ac)
        db_ref[...] = _bf(df * ac * sa)
        dac_ref[...] = _bf(df * b_ref[...].astype(F32) * (sa * (1.0 + ac * (1.0 - sa))))

    blk = pl.BlockSpec((tb, cb), lambda i, j: (i, j))
    return pl.pallas_call(
        body, name="ffn_bwd_gate", grid=(t_rows // tb, nj),
        in_specs=[pl.BlockSpec((tb, d), lambda i, j: (i, 0)), pl.BlockSpec((cb, d), lambda i, j: (j, 0)), blk,
                  pl.BlockSpec((tb, cb), lambda i, j: (i, nj + j))],
        out_specs=[blk, blk], out_shape=[SDS((t_rows, dff), BF16)] * 2,
        compiler_params=_cp("arbitrary", "arbitrary"))(dffn, w_down, aconv, ab)


def _ffn_conv_bwd(dac, ab, w_conv9, dff):
    t_rows = dac.shape[0]
    tb = _pick(t_rows, (256,))
    cb = _pick(dff, (1408, 256, 128))
    ni, nj = t_rows // tb, dff // cb
    n_ext = tb + 2 * GRID_W
    nh64 = tb // GRID_W

    def body(dm, dp, dn, am, ap, an, wc_ref, da_ref, gw_ref):
        i = pl.program_id(1)
        d_ext = _with_halo(dp[...], dm[...], dn[...], i, ni, tb)
        a_ext = _with_halo(ap[...], am[...], an[...], i, ni, tb)
        d_left, d_right = _grid_taps(d_ext, n_ext)
        a_left, a_right = _grid_taps(a_ext, n_ext)
        dmain = d_ext[GRID_W:GRID_W + tb]
        da = jnp.zeros((tb, cb), F32)
        rows = []
        for di in range(3):
            o = (2 - di) * GRID_W
            da = da + (wc_ref[3 * di:3 * di + 1, :] * d_right[o:o + tb] + wc_ref[3 * di + 1:3 * di + 2, :] * d_ext[o:o + tb]
                       + wc_ref[3 * di + 2:3 * di + 3, :] * d_left[o:o + tb])
            o = di * GRID_W
            for tap in (a_left, a_ext, a_right):
                rows.append(jnp.sum(dmain * tap[o:o + tb], axis=0, keepdims=True))
        da_ref[...] = _bf(da)

        @pl.when(i == 0)
        def _():
            gw_ref[...] = jnp.zeros_like(gw_ref)

        gw_ref[...] += jnp.concatenate(rows + [jnp.zeros((7, cb), F32)], axis=0)

    def halo(col0):
        return [pl.BlockSpec((tb, cb), lambda j, i: (i, col0 + j)),
                pl.BlockSpec((GRID_W, cb), lambda j, i: (jnp.maximum(i * nh64 - 1, 0), col0 + j)),
                pl.BlockSpec((GRID_W, cb), lambda j, i: (jnp.minimum((i + 1) * nh64, t_rows // GRID_W - 1), col0 + j))]

    return pl.pallas_call(
        body, name="ffn_conv_bwd", grid=(nj, ni),
        in_specs=halo(0) + halo(0) + [pl.BlockSpec((16, cb), lambda j, i: (0, j))],
        out_specs=[pl.BlockSpec((tb, cb), lambda j, i: (i, j)), pl.BlockSpec((16, cb), lambda j, i: (0, j))],
        out_shape=[SDS((t_rows, dff), BF16), SDS((16, dff), F32)],
        compiler_params=_cp("arbitrary", "arbitrary"))(dac, dac, dac, ab, ab, ab, w_conv9)


def _proj_norm_bwd(pairs, x_arr, x_row0, g, scale, resid, m_rows, name):
    d = x_arr.shape[1]
    tm = _pick(m_rows, (512, 256))
    ni = m_rows // tm
    starts, total = [], 0
    for (_, _, _, _, k_p, tk_p) in pairs:
        starts.append(total)
        total += k_p // tk_p
    npairs = len(pairs)
    has_dx = resid is not None

    def body(*refs):
        a_refs, b_refs = refs[0:2 * npairs:2], refs[1:2 * npairs:2]
        rest = refs[2 * npairs:]
        if has_dx:
            x_ref, g_ref, sc_ref, r_ref, dx_ref, st_ref, acc = rest
        else:
            x_ref, g_ref, sc_ref, st_ref, acc = rest
        i, k = pl.program_id(0), pl.program_id(1)

        @pl.when(k == 0)
        def _():
            acc[...] = jnp.zeros_like(acc)

        @pl.when((i == 0) & (k == 0))
        def _():
            st_ref[...] = jnp.zeros_like(st_ref)

        for p in range(npairs):
            nk = pairs[p][4] // pairs[p][5]

            @pl.when((k >= starts[p]) & (k < starts[p] + nk))
            def _(p=p):
                acc[...] += _dot_nt(a_refs[p][...], b_refs[p][...])

        @pl.when(k == total - 1)
        def _():
            dhn = acc[...]
            x = x_ref[...]
            r = lax.rsqrt(jnp.mean(x * x, axis=-1, keepdims=True) + EPS)
            xn = x * r
            dmod = dhn * (1.0 + sc_ref[...])
            dxn = dmod * g_ref[...]
            if has_dx:
                dx_ref[...] = r * (dxn - xn * jnp.mean(dxn * xn, axis=-1, keepdims=True)) + r_ref[...]
            st_ref[...] += jnp.concatenate(
                [jnp.sum(dmod * xn, axis=0, keepdims=True), jnp.sum(dhn, axis=0, keepdims=True),
                 jnp.sum(dhn * (xn * g_ref[...]), axis=0, keepdims=True), jnp.zeros((5, d), F32)], axis=0)

    in_specs, args = [], []
    for p, (a, a_row0, b, b_col0, k_p, tk_p) in enumerate(pairs):
        nk, s0, ar, bc = k_p // tk_p, starts[p], a_row0 // tm, b_col0 // tk_p

        def kk(k, s0=s0, nk=nk):
            return jnp.clip(k - s0, 0, nk - 1)

        in_specs.append(pl.BlockSpec((tm, tk_p), lambda i, k, ar=ar, kk=kk: (ar + i, kk(k))))
        in_specs.append(pl.BlockSpec((d, tk_p), lambda i, k, bc=bc, kk=kk: (0, bc + kk(k))))
        args += [a, b]
    xr = x_row0 // tm
    in_specs += [pl.BlockSpec((tm, d), lambda i, k: (xr + i, 0)), pl.BlockSpec((1, d), lambda i, k: (0, 0)),
                 pl.BlockSpec((1, d), lambda i, k: (0, 0))]
    args += [x_arr, g, scale]
    out_specs, out_shape = [], []
    if has_dx:
        in_specs.append(pl.BlockSpec((tm, d), lambda i, k: (i, 0)))
        args.append(resid)
        out_specs.append(pl.BlockSpec((tm, d), lambda i, k: (i, 0)))
        out_shape.append(SDS((m_rows, d), F32))
    out_specs.append(pl.BlockSpec((8, d), lambda i, k: (0, 0)))
    out_shape.append(SDS((8, d), F32))
    return pl.pallas_call(
        body, name=name, grid=(ni, total), in_specs=in_specs, out_specs=out_specs, out_shape=out_shape,
        scratch_shapes=[pltpu.VMEM((tm, d), F32)], compiler_params=_cp("arbitrary", "arbitrary"))(*args)


def _wgrad(a, b, k_rows, name):
    m, n = a.shape[1], b.shape[1]
    tm = _pick(m, (1408, 1024, 512, 384, 256, 128))
    tn = _pick(n, (1408, 1024, 768, 512, 384, 256, 128))
    tk = _pick(k_rows, (1280, 1024, 256))
    nk = k_rows // tk

    def body(a_ref, b_ref, o_ref, acc):
        k = pl.program_id(2)

        @pl.when(k == 0)
        def _():
            acc[...] = jnp.zeros_like(acc)

        acc[...] += _dot_tn(a_ref[...], b_ref[...])

        @pl.when(k == nk - 1)
        def _():
            o_ref[...] = _bf(acc[...])

    return pl.pallas_call(
        body, name=name, grid=(m // tm, n // tn, nk),
        in_specs=[pl.BlockSpec((tk, tm), lambda i, j, k: (k, i)), pl.BlockSpec((tk, tn), lambda i, j, k: (k, j))],
        out_specs=pl.BlockSpec((tm, tn), lambda i, j, k: (i, j)), out_shape=SDS((m, n), BF16),
        scratch_shapes=[pltpu.VMEM((tm, tn), F32)],
        compiler_params=_cp("arbitrary", "arbitrary", "arbitrary"))(a, b)


def _lane_put(col, lane_idx):
    lane = lax.broadcasted_iota(jnp.int32, (1, LANES), 1)
    return jnp.where(lane == lane_idx, col, 0.0)


def _mixer_bwd(dh1, out, hf, hb, z_main, pm, ps, hg, lng, lnb, w_s, b_st, wbm, wbs, wout, mx2, t_rows, nh):
    d = dh1.shape[1]
    ng, sc = w_s.shape[0], w_s.shape[1]
    dh, gd = d // nh, d // ng
    tb = _pick(t_rows, (256,))

    def body(dh1_ref, out_ref, hf_ref, hb_ref, zo, zu, zv, zgm, zgg, pm_ref, ps_ref, hg_ref, lng_ref, lnb_ref, ws_ref, bs_ref,
             wbm_ref, wbs_ref, wo_ref, mx2_ref, dz_ref, dhs_ref, dout_ref, dpm_ref, dps_ref, st_ref, dws_ref, dbs_ref):
        i = pl.program_id(0)

        @pl.when(i == 0)
        def _():
            st_ref[...] = jnp.zeros_like(st_ref)
            dws_ref[...] = jnp.zeros_like(dws_ref)
            dbs_ref[...] = jnp.zeros_like(dbs_ref)

        dh1v = dh1_ref[...]
        doutb = _bf(dh1v * mx2_ref[...])
        dout_ref[...] = doutb
        d_mx2 = jnp.sum(dh1v * out_ref[...].astype(F32), axis=0, keepdims=True)
        dy = _dot_nt(doutb, wo_ref[...])
        sgm, sgg = _sigmoid(zgm[...].astype(F32)), _sigmoid(zgg[...].astype(F32))
        dpmb, dpsb = _bf(dy * sgm), _bf(dy * sgg)
        dpm_ref[...] = dpmb
        dps_ref[...] = dpsb
        dz_ref[:, 3 * d:4 * d] = _bf(dy * pm_ref[...].astype(F32) * sgm * (1.0 - sgm))
        dz_ref[:, 4 * d:5 * d] = _bf(dy * ps_ref[...].astype(F32) * sgg * (1.0 - sgg))
        dym = _dot_nt(dpmb, wbm_ref[...])
        dys = _dot_nt(dpsb, wbs_ref[...])
        hs = hf_ref[...].astype(F32) + hb_ref[...].astype(F32)
        hn, scales = _head_rms(hs, nh, dh)
        so = _sigmoid(zo[...].astype(F32))
        dz_ref[:, 0:d] = _bf(dym * (hn * hg_ref[...]) * so * (1.0 - so))
        dhmn = dym * so
        d_hg = jnp.sum(dhmn * hn, axis=0, keepdims=True)
        dhn = dhmn * hg_ref[...]
        for h in range(nh):
            sl = slice(h * dh, (h + 1) * dh)
            dhs_ref[:, sl] = _bf(scales[h] * (dhn[:, sl] - hn[:, sl] * jnp.mean(dhn[:, sl] * hn[:, sl], axis=-1, keepdims=True)))
        zuv, zvv = zu[...].astype(F32), zv[...].astype(F32)
        u = _gelu(zuv)
        vhat, rstd = _layer_norm(_gelu(zvv))
        vnb = _bf(vhat * lng_ref[...] + lnb_ref[...])
        mixed = _sgu_mix(vnb, ws_ref, bs_ref, tb, ng, gd, sc)
        dz_ref[:, d:2 * d] = _bf(dys * mixed * _gelu_grad(zuv))
        dmix = dys * u
        rows = []
        dbs = jnp.zeros((sc, LANES), F32)
        for ch in range(tb // sc):
            cols = []
            for g in range(ng):
                dm = dmix[ch * sc:(ch + 1) * sc, g * gd:(g + 1) * gd]
                dmb = _bf(dm)
                dws_ref[g] += _dot_nt(dmb, vnb[ch * sc:(ch + 1) * sc, g * gd:(g + 1) * gd])
                dbs = dbs + _lane_put(jnp.sum(dm, axis=1, keepdims=True), g)
                cols.append(_dot_tn(_bf(ws_ref[g]), dmb))
            rows.append(jnp.concatenate(cols, axis=1))
        dbs_ref[...] += dbs
        dvn = jnp.concatenate(rows, axis=0)
        d_lng = jnp.sum(dvn * vhat, axis=0, keepdims=True)
        d_lnb = jnp.sum(dvn, axis=0, keepdims=True)
        dvh = dvn * lng_ref[...]
        dvg = rstd * (dvh - jnp.mean(dvh, axis=-1, keepdims=True) - vhat * jnp.mean(dvh * vhat, axis=-1, keepdims=True))
        dz_ref[:, 2 * d:3 * d] = _bf(dvg * _gelu_grad(zvv))
        st_ref[...] += jnp.concatenate([d_mx2, d_hg, d_lng, d_lnb, jnp.zeros((4, d), F32)], axis=0)

    def tok(col):
        return pl.BlockSpec((tb, d), lambda i: (i, col))

    def full(shape):
        return pl.BlockSpec(shape, lambda i: (0,) * len(shape))

    return pl.pallas_call(
        body, name="mixer_bwd", grid=(t_rows // tb,),
        in_specs=[tok(0), tok(0), tok(0), tok(0), tok(3), tok(4), tok(5), tok(6), tok(7), tok(0), tok(0), full((1, d)),
                  full((1, d)), full((1, d)), full((ng, sc, sc)), full((sc, LANES)), full((d, d)), full((d, d)), full((d, d)),
                  full((1, d))],
        out_specs=[pl.BlockSpec((tb, 5 * d), lambda i: (i, 0)), tok(0), tok(0), tok(0), tok(0), full((8, d)), full((ng, sc, sc)),
                   full((sc, LANES))],
        out_shape=[SDS((t_rows, 5 * d), BF16)] + [SDS((t_rows, d), BF16)] * 4 + [SDS((8, d), F32), SDS((ng, sc, sc), F32),
                                                                                SDS((sc, LANES), F32)],
        compiler_params=_cp("arbitrary"))(dh1, out, hf, hb, z_main, z_main, z_main, z_main, z_main, pm, ps, hg, lng, lnb, w_s,
                                          b_st, wbm, wbs, wout, mx2)


def _mlstm_bwd(qk, z_main, zg, bias, dhs, states_f, states_b, nh, t_rows):
    s_rows = qk.shape[0]
    md = qk.shape[1] // 2
    dh = md // nh
    nc = s_rows // LCH
    nx = t_rows // LCH
    ln = LCH

    def chunk_f(i):
        return jnp.where(i == nc - 1, nc - 1, nc - 2 - i)

    def chunk_b(i):
        return jnp.where(i == nc - 1, nc - 1, i)

    def body(qf, kf, vf, gf, dhf, cf, nf, mf_, qb, kb, vb, gb, dhb, cb, nb, mb_, bias_ref, dqkvf_ref, dgf_ref, dqkvb_ref, dgb_ref,
             dc_sc, dn_sc):
        i = pl.program_id(0)
        is_ctx = i == nc - 1

        @pl.when(i == 0)
        def _():
            dc_sc[...] = jnp.zeros_like(dc_sc)
            dn_sc[...] = jnp.zeros_like(dn_sc)

        for dr, (q_ref, k_ref, v_ref, g_ref, dh_ref, c_ref, n_ref, m_ref, dqkv_ref, dg_ref) in enumerate(
                ((qf, kf, vf, gf, dhf, cf, nf, mf_, dqkvf_ref, dgf_ref), (qb, kb, vb, gb, dhb, cb, nb, mb_, dqkvb_ref, dgb_ref))):
            gz, b_all, b_t, g_t, g_all, mask, mfl = _chunk_gates(g_ref[...], bias_ref[...], dr == 1)
            x1 = jnp.zeros((ln, LANES), F32)
            x2 = jnp.zeros((ln, LANES), F32)
            dig = jnp.zeros((ln, LANES), F32)
            e_row = jnp.zeros((1, LANES), F32)
            for h in range(nh):
                ci, cfl = 2 * dr * nh + h, (2 * dr + 1) * nh + h
                sl = slice(h * dh, (h + 1) * dh)
                q, k, v = q_ref[:, sl], k_ref[:, sl], v_ref[:, sl]
                qf32, kf32 = q.astype(F32), k.astype(F32)
                dhv = jnp.where(is_ctx, 0.0, dh_ref[:, sl].astype(F32))
                c_in, n_in, m_in = c_ref[sl, :], n_ref[0:1, sl], m_ref[h, 0:1, 0:1]
                b_col, b_row, i_col, i_row = b_all[:, cfl:cfl + 1], b_t[cfl:cfl + 1, :], gz[:, ci:ci + 1], g_t[ci:ci + 1, :]
                g = g_all[:, cfl:cfl + 1]
                w, w_int, m_row = _head_weights(b_col, b_row, i_row, m_in, mask)
                s_mat = _dot_nt(q, k) * w
                sb, cb16 = _bf(s_mat), _bf(c_in)
                num = _dot(sb, v) + w_int * _dot(q, cb16)
                den = jnp.sum(s_mat, axis=1, keepdims=True) + w_int * jnp.sum(qf32 * n_in, axis=1, keepdims=True)
                e_m = jnp.exp(-m_row)
                dnm = jnp.maximum(jnp.abs(den), e_m)
                dnum = dhv / dnm
                hdh = jnp.sum((num / dnm) * dhv, axis=1, keepdims=True)
                dden = jnp.where(jnp.abs(den) > e_m, -(hdh / dnm) * jnp.sign(den), 0.0)
                dnum_b = _bf(dnum)
                ds = _dot_nt(dnum_b, v) + dden
                pb = _bf(w * ds)
                gmat = s_mat * ds
                a_old, coef, _ = _head_state_coeffs(g, b_col, i_col, m_in)
                dc_new, dn_new = dc_sc[dr, h], dn_sc[dr, h, 0:1, :]
                dcb = _bf(dc_new)
                dv = _dot_tn(sb, dnum_b) + _dot(_bf(kf32 * coef), dcb)
                dq_inter = w_int * (_dot_nt(dnum_b, cb16) + dden * n_in)
                dq = _dot(pb, k) + dq_inter
                dk_state = coef * (_dot_nt(v, dcb) + dn_new)
                dk = _dot_tn(pb, q) + dk_state
                dqkv_ref[:, sl] = _bf(dq)
                dqkv_ref[:, md + h * dh:md + (h + 1) * dh] = _bf(dk)
                dqkv_ref[:, 2 * md + h * dh:2 * md + (h + 1) * dh] = _bf(dv)
                row_intra = jnp.sum(gmat, axis=1, keepdims=True)
                col_intra = jnp.sum(gmat.T, axis=1, keepdims=True)
                row_inter = jnp.sum(qf32 * dq_inter, axis=1, keepdims=True)
                col_inter = jnp.sum(kf32 * dk_state, axis=1, keepdims=True)
                e_old = a_old * (jnp.sum(jnp.sum(dc_new * c_in, axis=1, keepdims=True), axis=0, keepdims=True)
                                 + jnp.sum(dn_new * n_in, axis=1, keepdims=True))
                x1 = x1 + _lane_put(row_intra - col_intra + row_inter, cfl)
                x2 = x2 + _lane_put(col_inter, cfl)
                e_row = e_row + _lane_put(e_old, cfl)
                dig = dig + _lane_put(col_intra + col_inter, ci)
                dc_sc[dr, h] = a_old * dc_new + _dot_tn(_bf(qf32 * w_int), dnum_b)
                dn_sc[dr, h] = jnp.broadcast_to(a_old * dn_new + jnp.sum(qf32 * (w_int * dden), axis=0, keepdims=True), (8, dh))
            dlogf = _mask_dot_t(mfl, x1) + _mask_dot(mfl, x2) - x2 + e_row
            dg_ref[...] = dig + dlogf / (1.0 + jnp.exp(gz))

    def tok(cfn, col):
        return pl.BlockSpec((ln, md), lambda i: (cfn(i), col))

    def dht(cfn):
        return pl.BlockSpec((ln, md), lambda i: (jnp.minimum(cfn(i), nx - 1), 0))

    def gat(cfn):
        return pl.BlockSpec((ln, LANES), lambda i: (cfn(i), 0))

    def st(cfn, shape):
        return pl.BlockSpec((None,) + shape, lambda i: (cfn(i),) + (0,) * len(shape))

    st_shapes = ((nh * dh, dh), (8, md), (nh, 8, LANES))

    def side(cfn):
        return [tok(cfn, 0), tok(cfn, 1), tok(cfn, 2), gat(cfn), dht(cfn)] + [st(cfn, s) for s in st_shapes]

    def outs(cfn):
        return [pl.BlockSpec((ln, 3 * md), lambda i: (cfn(i), 0)), gat(cfn)]

    return pl.pallas_call(
        body, name="mlstm_bwd", grid=(nc,),
        in_specs=side(chunk_f) + side(chunk_b) + [pl.BlockSpec((1, LANES), lambda i: (0, 0))],
        out_specs=outs(chunk_f) + outs(chunk_b),
        out_shape=[SDS((s_rows, 3 * md), BF16), SDS((s_rows, LANES), F32)] * 2,
        scratch_shapes=[pltpu.VMEM((2, nh, dh, dh), F32), pltpu.VMEM((2, nh, 8, dh), F32)],
        compiler_params=_cp("arbitrary"))(qk, qk, z_main, zg, dhs, *states_f, qk, qk, z_main, zg, dhs, *states_b, bias)


def _qkv_conv_bwd(dqkv_f, dqkv_b, z_main, conv_w, t_rows, md, qscale):
    s_rows = z_main.shape[0]
    tb = _pick(s_rows, (1280, 1024, 256))
    cb = _pick(md, (512, 256, 128))
    ni, nj, ncq = s_rows // tb, 3 * md // cb, 2 * md // cb
    nb8 = tb // 8
    n_ext = tb + 16

    def body(fm, fp, fn, bm, bp, bn, zm, zp, zn, w_ref, dz_ref, gw_ref):
        j, i = pl.program_id(0), pl.program_id(1)

        @pl.when(j < ncq)
        def _():
            z = jnp.concatenate([zp[...], zm[...], zn[...]], axis=0).astype(F32)
            dqk = (jnp.concatenate([fp[...], fm[...], fn[...]], axis=0).astype(F32)
                   + jnp.concatenate([bp[...], bm[...], bn[...]], axis=0).astype(F32)) * jnp.where(j * cb < md, qscale, 1.0)
            row = i * tb - 8 + lax.broadcasted_iota(jnp.int32, (n_ext, 1), 0)
            prev_ok, next_ok = _seg_masks(row, t_rows, s_rows)
            zprev = jnp.where(prev_ok, pltpu.roll(z, 1, 0), 0.0)
            znext = jnp.where(next_ok, pltpu.roll(z, n_ext - 1, 0), 0.0)
            pre = w_ref[0:1, :] * zprev + w_ref[1:2, :] * z + w_ref[2:3, :] * znext
            sg = _sigmoid(pre)
            dpre = dqk * (sg * (1.0 + pre * (1.0 - sg)))
            dz = (w_ref[1:2, :] * dpre + w_ref[0:1, :] * jnp.where(next_ok, pltpu.roll(dpre, n_ext - 1, 0), 0.0)
                  + w_ref[2:3, :] * jnp.where(prev_ok, pltpu.roll(dpre, 1, 0), 0.0))
            dz_ref[...] = _bf(dz[8:8 + tb])
            dm = dpre[8:8 + tb]

            @pl.when(i == 0)
            def _():
                gw_ref[...] = jnp.zeros_like(gw_ref)

            gw_ref[...] += jnp.concatenate(
                [jnp.sum(dm * zprev[8:8 + tb], axis=0, keepdims=True), jnp.sum(dm * z[8:8 + tb], axis=0, keepdims=True),
                 jnp.sum(dm * znext[8:8 + tb], axis=0, keepdims=True), jnp.zeros((5, cb), F32)], axis=0)

        @pl.when(j >= ncq)
        def _():
            dz_ref[...] = _bf(fm[...].astype(F32) + bm[...].astype(F32))

    def halo(clampj):
        def cj(j):
            return jnp.minimum(j, ncq - 1) if clampj else j
        return [pl.BlockSpec((tb, cb), lambda j, i: (i, cj(j))),
                pl.BlockSpec((8, cb), lambda j, i: (jnp.maximum(i * nb8 - 1, 0), cj(j))),
                pl.BlockSpec((8, cb), lambda j, i: (jnp.minimum((i + 1) * nb8, s_rows // 8 - 1), cj(j)))]

    return pl.pallas_call(
        body, name="qkv_conv_bwd", grid=(nj, ni),
        in_specs=halo(False) + halo(False) + halo(True) + [pl.BlockSpec((8, cb), lambda j, i: (0, jnp.minimum(j, ncq - 1)))],
        out_specs=[pl.BlockSpec((tb, cb), lambda j, i: (i, j)), pl.BlockSpec((8, cb), lambda j, i: (0, jnp.minimum(j, ncq - 1)))],
        out_shape=[SDS((s_rows, 3 * md), BF16), SDS((8, 2 * md), F32)],
        compiler_params=_cp("arbitrary", "arbitrary"))(dqkv_f, dqkv_f, dqkv_f, dqkv_b, dqkv_b, dqkv_b, z_main, z_main, z_main, conv_w)


def _gate_grad_sum(dg_f, dg_b):
    s_rows = dg_f.shape[0]
    tb = _pick(s_rows, (1280, 1024, 256))

    def body(a_ref, b_ref, o_ref, st_ref):
        @pl.when(pl.program_id(0) == 0)
        def _():
            st_ref[...] = jnp.zeros_like(st_ref)

        s = a_ref[...] + b_ref[...]
        o_ref[...] = _bf(s)
        st_ref[...] += jnp.concatenate([jnp.sum(s, axis=0, keepdims=True), jnp.zeros((7, LANES), F32)], axis=0)

    blk = pl.BlockSpec((tb, LANES), lambda i: (i, 0))
    return pl.pallas_call(
        body, name="gate_grad_sum", grid=(s_rows // tb,), in_specs=[blk, blk],
        out_specs=[blk, pl.BlockSpec((8, LANES), lambda i: (0, 0))],
        out_shape=[SDS((s_rows, LANES), BF16), SDS((8, LANES), F32)], compiler_params=_cp("arbitrary"))(dg_f, dg_b)


def _mod_grads(silu_slots, dmx_sh, dmx_slots, dmc_tot, dmc_sh, silu_cctx, c_ctx, w_mod_c):
    d = silu_slots.shape[1]
    ncol, n6 = dmx_sh.shape[1], dmx_slots.shape[1]

    def body(ss_ref, dsh_ref, dsl_ref, dct_ref, dcs_ref, sc_ref, c_ref, w_ref, gw_ref, gb_ref, gc_ref):
        a = jnp.concatenate([ss_ref[...], sc_ref[...], jnp.zeros((7, d), F32)], axis=0)
        b = jnp.concatenate([dsh_ref[...], dcs_ref[...], jnp.zeros((7, ncol), F32)], axis=0)
        gw_ref[...] = lax.dot_general(a, b, (((0,), (0,)), ((), ())), preferred_element_type=F32, precision=HI)
        dct = dct_ref[...]
        gb_ref[...] = jnp.sum(dsl_ref[...], axis=0, keepdims=True) + jnp.concatenate(
            [dct, jnp.zeros((1, n6 - dct.shape[1]), F32)], axis=1)
        t = _dot_nt(_bf(jnp.broadcast_to(dct, (8, dct.shape[1]))), w_ref[...])
        cv = c_ref[...]
        s = _sigmoid(cv)
        gc_ref[...] = t[0:1, :] * (s * (1.0 + cv * (1.0 - s)))

    return pl.pallas_call(body, name="mod_grads", out_shape=[SDS((d, ncol), F32), SDS((1, n6), F32), SDS((1, d), F32)],
                          compiler_params=_cp())(silu_slots, dmx_sh, dmx_slots, dmc_tot, dmc_sh, silu_cctx, c_ctx, w_mod_c)


def _slot_sum(slots):
    ns, r = slots.shape[0], slots.shape[1]
    tb = _pick(r, (1024, 512, 256, 128, 64, 32, 16, 8))

    def body(s_ref, o_ref):
        acc = s_ref[0]
        for k in range(1, ns):
            acc = acc + s_ref[k]
        o_ref[...] = acc

    return pl.pallas_call(
        body, name="slot_sum", grid=(r // tb,), in_specs=[pl.BlockSpec((ns, tb, LANES), lambda i: (0, i, 0))],
        out_specs=pl.BlockSpec((tb, LANES), lambda i: (i, 0)), out_shape=SDS((r, LANES), F32),
        compiler_params=_cp("arbitrary"))(slots)


def _adamw(w, gslots, m, v, name):
    r, cdim = w.shape
    ns, rg = gslots.shape[0], gslots.shape[1]
    tb = r if (rg != r or r % 8) else _pick(r, (128, 64, 32, 16, 8))
    bc1, bc2 = 1.0 - ADAM_B1 ** ADAM_STEP, 1.0 - ADAM_B2 ** ADAM_STEP

    def body(w_ref, g_ref, m_ref, v_ref, go_ref, d_ref, mo_ref, vo_ref):
        g = g_ref[0, 0:tb, :].astype(F32)
        for k in range(1, ns):
            g = g + g_ref[k, 0:tb, :].astype(F32)
        mn = ADAM_B1 * m_ref[...] + (1.0 - ADAM_B1) * g
        vn = ADAM_B2 * v_ref[...] + (1.0 - ADAM_B2) * (g * g)
        go_ref[...] = g
        mo_ref[...] = mn
        vo_ref[...] = vn
        d_ref[...] = -ADAM_LR * ((mn / bc1) / (jnp.sqrt(vn / bc2) + ADAM_EPS) + ADAM_WD * w_ref[...])

    blk = pl.BlockSpec((tb, cdim), lambda i: (i, 0))
    gblk = pl.BlockSpec((ns, tb if rg == r else rg, cdim), lambda i: (0, i, 0))
    return pl.pallas_call(
        body, name=name, grid=(r // tb,), in_specs=[blk, gblk, blk, blk],
        out_specs=[blk] * 4, out_shape=[SDS((r, cdim), F32)] * 4, compiler_params=_cp("arbitrary"))(w, gslots, m, v)


def _pack(parts, row_mult):
    flat = jnp.concatenate([p.reshape(-1) for p in parts])
    n = flat.shape[0]
    rows = -(-n // LANES)
    rows = -(-rows // row_mult) * row_mult
    return jnp.pad(flat, (0, rows * LANES - n)).reshape(rows, LANES)


def _unpack(buf, shapes):
    flat = buf.reshape(-1)
    out, off = [], 0
    for s in shapes:
        n = math.prod(s)
        out.append(flat[off:off + n].reshape(s))
        off += n
    return out


def _pad_cols(a, width):
    return jnp.pad(a, ((0, 0), (0, width - a.shape[1])))


def _pad_lanes(a):
    return _pad_cols(a, LANES)


def _up128(n):
    return -(-n // LANES) * LANES


def kernel(x, c, ctx, c_ctx, w_mod, b_mod, norm1_g, w_in, b_gate, conv_qk, head_norm_g, sgu_ln_g, sgu_ln_b, w_s, b_s, w_branch_mlstm, w_branch_sgu, w_out, norm2_g, w_up, w_ffn_conv, w_down, final_g, loss_target, m_c_ctx, m_w_mod, m_b_mod, m_norm1_g, m_w_in, m_b_gate, m_conv_qk, m_head_norm_g, m_sgu_ln_g, m_sgu_ln_b, m_w_s, m_b_s, m_w_branch_mlstm, m_w_branch_sgu, m_w_out, m_norm2_g, m_w_up, m_w_ffn_conv, m_w_down, m_final_g, v_c_ctx, v_w_mod, v_b_mod, v_norm1_g, v_w_in, v_b_gate, v_conv_qk, v_head_norm_g, v_sgu_ln_g, v_sgu_ln_b, v_w_s, v_b_s, v_w_branch_mlstm, v_w_branch_sgu, v_w_out, v_norm2_g, v_w_up, v_w_ffn_conv, v_w_down, v_final_g):
    t, d = x.shape[1], x.shape[2]
    n_ctx = ctx.shape[1]
    s_rows = t + n_ctx
    nh = b_gate.shape[1] // 4
    md = head_norm_g.shape[1]
    dh = md // nh
    ng, sc = w_s.shape[1], w_s.shape[2]
    dff = w_down.shape[1] * N_DEV
    n_in = w_in.shape[2] * N_DEV
    assert md == d and sgu_ln_g.shape[1] == d and n_ctx == LCH and t % LCH == 0 and t % (8 * GRID_W) == 0
    assert n_in == 8 * d + 4 * nh and 4 * nh <= LANES
    me = 4 * lax.axis_index("x") + 2 * lax.axis_index("y") + lax.axis_index("c")

    n_mod, n_insh, n_upsh = w_mod.shape[2], w_in.shape[2], w_up.shape[2]
    p_mod, p_in, p_up = _up128(n_mod), _up128(n_insh), _up128(n_upsh)
    nq, nf = conv_qk.shape[2], w_ffn_conv.shape[3]
    ffn9 = w_ffn_conv[0].reshape(9, nf)
    colpack = jnp.concatenate([_pad_cols(_bf(w_mod[0]), p_mod), _pad_cols(_bf(w_in[0]), p_in)], axis=1)
    convpack = jnp.concatenate([jnp.pad(conv_qk[0], ((0, 13), (0, 0))), jnp.pad(ffn9, ((0, 7), (0, 0)))], axis=1)
    g_col, g_conv = _allgather([colpack, convpack])
    w_mod_f, w_main, w_gate = _assemble_cols(
        g_col, [(0, n_mod, [(0, 0, N_DEV * n_mod, 0)]),
                (p_mod, n_insh, [(1, 0, 3 * md, 0), (2, 3 * md, 4 * nh, 0), (1, 3 * md + 4 * nh, 5 * d, 3 * md)])],
        [N_MOD * d, 8 * d, LANES], "assemble_weights")
    convw, wconv9 = _assemble_cols(g_conv, [(0, nq, [(0, 0, N_DEV * nq, 0)]), (nq, nf, [(1, 0, N_DEV * nf, 0)])],
                                   [N_DEV * nq, N_DEV * nf], "assemble_conv_weights")
    zero = jnp.minimum(jnp.abs(g_conv[0, 0, 0]), 0.0)
    late_w = [_pad_cols(_bf(w_up[0] + zero), p_up), _bf(w_branch_mlstm[0]), _bf(w_branch_sgu[0]), _bf(w_out[0]), _bf(w_down[0])]
    late_state, late_tok = _exchange_start(late_w, False, "late_weights_start")

    cvec = jnp.concatenate([c, c_ctx[None], jnp.zeros((6, d), F32)], axis=0) + late_tok[0:1, 0:1]
    silu_v, mod = _modulation(cvec, w_mod_f, b_mod)
    mx = [mod[0:1, k * d:(k + 1) * d] for k in range(N_MOD)]
    mc = [mod[1:2, k * d:(k + 1) * d] for k in range(2)]
    xs = jnp.concatenate([x[0], ctx[0]], axis=0)
    hn, z_main, zg = _norm_mod_proj(xs, norm1_g, jnp.concatenate([mx[0], mx[1], mc[0], mc[1]], axis=0), w_main, w_gate, t, "in_proj")
    qscale = dh ** -0.5
    qk = _qk_conv(z_main, convw, t, md, qscale)
    bias = _pad_lanes(b_gate)
    fwd = _mlstm_fwd(qk, z_main, zg, bias, nh)
    hf, hb, states_f, states_b = fwd[0], fwd[1], fwd[2:5], fwd[5:8]
    g_up, g_bm, g_bs, g_out, g_down = _exchange_wait(late_state, fwd[4], "late_weights_wait")
    (w_up_f,) = _assemble_cols(g_up, [(0, n_upsh, [(0, 0, 2 * dff, 0)])], [2 * dff], "assemble_w_up")
    wbm_f, wbs_f, wout_f = (g.reshape(d, d) for g in (g_bm, g_bs, g_out))
    w_down_f = g_down.reshape(dff, d)
    b_st = _pad_lanes(b_s[0].T)
    h1, ym, ys, pm, ps, y, out = _mixer_fwd(hf, hb, z_main, xs, head_norm_g, sgu_ln_g, sgu_ln_b, w_s[0], b_st, wbm_f, wbs_f,
                                            wout_f, mx[2], t, nh)
    hn2, ab = _norm_mod_proj(h1, norm2_g, jnp.concatenate([mx[3], mx[4], mx[3], mx[4]], axis=0), w_up_f, None, t, "up_proj")
    aconv, f, dh2, dffn, st_tail = _ffn_tail(ab, wconv9, w_down_f, h1, mx[5], final_g[None], loss_target[0], dff)

    db, dac = _ffn_bwd_gate(dffn, w_down_f, aconv, ab, dff)
    da, g_wconv9 = _ffn_conv_bwd(dac, ab, wconv9, dff)
    g_wdown = _wgrad(f, dffn, t, "wgrad_down")
    gwup_slots = _scatter_cols([_wgrad(hn2, da, t, "wgrad_up_a"), _wgrad(hn2, db, t, "wgrad_up_b")],
                               [(0, 0, dff, 0), (1, dff, dff, 0)], n_upsh, "scatter_grad_w_up")
    tkf = _pick(dff, (1408, 704, 384, 128))
    dh1, st_n2 = _proj_norm_bwd([(da, 0, w_up_f, 0, dff, tkf), (db, 0, w_up_f, dff, dff, tkf)], h1, 0, norm2_g, mx[4], dh2, t,
                                "up_proj_bwd")
    dz_rest, dhs, dout, dpm, dps, st_mix, g_ws, g_bst = _mixer_bwd(dh1, out, hf, hb, z_main, pm, ps, head_norm_g, sgu_ln_g,
                                                                    sgu_ln_b, w_s[0], b_st, wbm_f, wbs_f, wout_f, mx[2], t, nh)
    g_wout = _wgrad(y, dout, t, "wgrad_out")
    g_wbm = _wgrad(ym, dpm, t, "wgrad_branch_mlstm")
    g_wbs = _wgrad(ys, dps, t, "wgrad_branch_sgu")
    ex_a = [gwup_slots, g_wdown.reshape(N_DEV, dff // N_DEV, d), g_wbm.reshape(N_DEV, d // N_DEV, d),
            g_wbs.reshape(N_DEV, d // N_DEV, d), g_wout.reshape(N_DEV, d // N_DEV, d)]
    ex_a_state, ex_a_tok = _exchange_start(ex_a, True, "grad_exchange_a_start")
    dqkv_f, dg_f, dqkv_b, dg_b = _mlstm_bwd(qk, z_main, zg, bias + ex_a_tok[0:1, :], dhs, states_f, states_b, nh, t)
    dz_qkv, g_convqk = _qkv_conv_bwd(dqkv_f, dqkv_b, z_main, convw, t, md, qscale)
    dz_g, st_gate = _gate_grad_sum(dg_f, dg_b)
    gwin_slots = _scatter_cols(
        [_wgrad(hn, dz_qkv, s_rows, "wgrad_in_qkv"), _wgrad(hn, dz_g, s_rows, "wgrad_in_gate"), _wgrad(hn, dz_rest, t, "wgrad_in_rest")],
        [(0, 0, 3 * md, 0), (1, 3 * md, 4 * nh, 0), (2, 3 * md + 4 * nh, 5 * d, 0)], n_insh, "scatter_grad_w_in")
    gcq_slots = _scatter_cols([g_convqk], [(0, 0, 2 * md, 0)], nq, "scatter_grad_conv_qk")
    gcf_slots = _scatter_cols([g_wconv9], [(0, 0, dff, 0)], nf, "scatter_grad_ffn_conv")
    ex_b_state, ex_b_tok = _exchange_start([gwin_slots, gcq_slots, gcf_slots], True, "grad_exchange_b_start")
    tk = _pick(md, (1024, 512, 256))
    grad_x, st_n1x = _proj_norm_bwd(
        [(dz_qkv, 0, w_main, 0, 3 * md, tk), (dz_rest, 0, w_main, 3 * md, 5 * d, tk), (dz_g, 0, w_gate, 0, LANES, LANES)],
        xs, 0, norm1_g, mx[1] + ex_b_tok[0:1, 0:1], dh1, t, "in_proj_bwd")
    (st_n1c,) = _proj_norm_bwd([(dz_qkv, t, w_main, 0, 3 * md, tk), (dz_g, t, w_gate, 0, LANES, LANES)],
                               xs, t, norm1_g, mc[1] + ex_b_tok[0:1, 0:1], None, n_ctx, "in_proj_bwd_ctx")

    rx_a = _exchange_wait(ex_a_state, st_n1c, "grad_exchange_a_wait")
    rx_b = _exchange_wait(ex_b_state, st_n1c, "grad_exchange_b_wait")
    recv = [rx_b[0], rx_a[0], rx_a[2], rx_a[3], rx_a[4], rx_a[1], rx_b[1], rx_b[2]]
    small_parts = [st_n1x[1], st_n1x[2], st_mix[0], st_n2[1], st_n2[2], st_tail[1],
                   st_n1c[1], st_n1c[2],
                   silu_v[0], st_n1x[0] + st_n1c[0], st_gate[0], st_mix[1], st_mix[2], st_mix[3],
                   g_ws.reshape(-1), g_bst[:, :ng].T.reshape(-1), st_n2[0], st_tail[0]]
    gsmall = _pack(small_parts, 8)
    (recv_small,) = _grad_exchange([], [gsmall])
    small_sum = _slot_sum(recv_small).reshape(-1)
    small_slots = recv_small.reshape(N_DEV, -1)
    o_silu, o_n1 = 8 * d, 9 * d
    ncol = N_MOD * d // N_DEV
    dmc_tot = small_sum[6 * d:8 * d][None]
    dmc_pad = jnp.concatenate([dmc_tot, jnp.zeros((1, 4 * d), F32)], axis=1)
    g_wmod, g_bmod, g_cctx = _mod_grads(
        small_slots[:, o_silu:o_silu + d], lax.dynamic_slice_in_dim(small_slots[:, :6 * d], me * ncol, ncol, axis=1),
        small_slots[:, :6 * d], dmc_tot, lax.dynamic_slice_in_dim(dmc_pad, me * ncol, ncol, axis=1), silu_v[1:2], c_ctx[None],
        w_mod_f[:, :2 * d])

    shard_w = (w_in, w_up, w_branch_mlstm, w_branch_sgu, w_out, w_down, conv_qk)
    shard_m = (m_w_in, m_w_up, m_w_branch_mlstm, m_w_branch_sgu, m_w_out, m_w_down, m_conv_qk)
    shard_v = (v_w_in, v_w_up, v_w_branch_mlstm, v_w_branch_sgu, v_w_out, v_w_down, v_conv_qk)
    shard_names = ("w_in", "w_up", "w_branch_mlstm", "w_branch_sgu", "w_out", "w_down", "conv_qk")
    shard_out = [[b[None] for b in _adamw(wa[0], recv[k], ma[0], va[0], "adamw_" + nm)]
                 for k, (wa, ma, va, nm) in enumerate(zip(shard_w, shard_m, shard_v, shard_names))]
    shard_out.append([b.reshape(w_ffn_conv.shape) for b in
                      _adamw(ffn9, recv[7], m_w_ffn_conv[0].reshape(9, nf), v_w_ffn_conv[0].reshape(9, nf), "adamw_w_ffn_conv")])
    mod_out = [b[None] for b in _adamw(w_mod[0], g_wmod[None], m_w_mod[0], v_w_mod[0], "adamw_w_mod")]

    def rep(cc, bm, n1, bg, hg, lg, lb, ws, bs, n2, fg):
        return [cc.reshape(-1), bm.reshape(-1), n1.reshape(-1), _pad_lanes(bg.reshape(1, -1)).reshape(-1), hg.reshape(-1),
                lg.reshape(-1), lb.reshape(-1), ws.reshape(-1), bs.reshape(-1), n2.reshape(-1), fg.reshape(-1)]

    o = o_n1
    g_rep_parts = [g_cctx, g_bmod]
    for n in (d, LANES, d, d, d, ng * sc * sc, ng * sc, d, d):
        g_rep_parts.append(small_sum[o:o + n])
        o += n
    rep_shapes = [(d,), (1, N_MOD * d), (1, d), (1, LANES), (1, d), (1, d), (1, d), (1, ng, sc, sc), (1, ng, sc), (1, d), (d,)]
    rep_out = _adamw(
        _pack(rep(c_ctx, b_mod, norm1_g, b_gate, head_norm_g, sgu_ln_g, sgu_ln_b, w_s, b_s, norm2_g, final_g), 8),
        _pack(g_rep_parts, 8)[None],
        _pack(rep(m_c_ctx, m_b_mod, m_norm1_g, m_b_gate, m_head_norm_g, m_sgu_ln_g, m_sgu_ln_b, m_w_s, m_b_s, m_norm2_g, m_final_g), 8),
        _pack(rep(v_c_ctx, v_b_mod, v_norm1_g, v_b_gate, v_head_norm_g, v_sgu_ln_g, v_sgu_ln_b, v_w_s, v_b_s, v_norm2_g, v_final_g), 8),
        "adamw_replicated")

    def assemble(k):
        r = _unpack(rep_out[k], rep_shapes)
        s = [o[k] for o in shard_out]
        return [r[0], mod_out[k], r[1], r[2], s[0], r[3][:, :4 * nh], s[6], r[4], r[5], r[6], r[7], r[8], s[2], s[3], s[4], r[9],
                s[1], s[7], s[5], r[10]]

    loss = lax.psum(st_tail[2, 0], ("x", "y", "c"))
    outs = [loss, grad_x[None]]
    for k in range(4):
        outs += assemble(k)
    return tuple(outs)
```

```python
import functools
import math

import jax
import jax.numpy as jnp
from jax import lax
from jax.experimental import pallas as pl
from jax.experimental.pallas import tpu as pltpu

F32, BF16 = jnp.float32, jnp.bfloat16
EPS = 1e-6
M_INIT = -1e30
NEG = -1e30
GRID_W = 64
LCH = 256
N_MOD = 6
N_DEV = 8
LANES = 128
ADAM_LR, ADAM_B1, ADAM_B2, ADAM_EPS, ADAM_WD, ADAM_STEP = 0.001, 0.9, 0.999, 1e-08, 0.01, 10
GELU_C = math.sqrt(2.0 / math.pi)
GELU_A = 0.044715
VMEM_LIMIT = 56 * 1024 * 1024
HI = lax.Precision.HIGHEST
SDS = jax.ShapeDtypeStruct
MESH_ID = pl.DeviceIdType.MESH


def _pick(n, cands):
    for c in cands:
        if n % c == 0:
            return c
    raise ValueError(f"no block size for {n} in {cands}")


def _cp(*sem):
    return pltpu.CompilerParams(dimension_semantics=sem if sem else None, vmem_limit_bytes=VMEM_LIMIT)


def _sigmoid(x):
    return 0.5 * jnp.tanh(0.5 * x) + 0.5


def _split3(x):
    hi = x.astype(BF16)
    r = x - hi.astype(F32)
    mid = r.astype(BF16)
    return hi, mid, (r - mid.astype(F32)).astype(BF16)


def _mask_dot(mask_b, x):
    hi, mid, lo = _split3(x)
    return (_dot(mask_b, lo) + _dot(mask_b, mid)) + _dot(mask_b, hi)


def _mask_dot_t(mask_b, x):
    hi, mid, lo = _split3(x)
    return (_dot_tn(mask_b, lo) + _dot_tn(mask_b, mid)) + _dot_tn(mask_b, hi)


def _gelu(x):
    return 0.5 * x * (1.0 + jnp.tanh(GELU_C * (x + GELU_A * x * x * x)))


def _gelu_grad(x):
    t = jnp.tanh(GELU_C * (x + GELU_A * x * x * x))
    return 0.5 * (1.0 + t) + 0.5 * x * (1.0 - t * t) * GELU_C * (1.0 + 3.0 * GELU_A * x * x)


def _log_sigmoid(x):
    return jnp.minimum(x, 0.0) - jnp.log(1.0 + jnp.exp(-jnp.abs(x)))


def _dot(a, b):
    return jnp.dot(a, b, preferred_element_type=F32)


def _dot_nt(a, b):
    return lax.dot_general(a, b, (((1,), (1,)), ((), ())), preferred_element_type=F32)


def _dot_tn(a, b):
    return lax.dot_general(a, b, (((0,), (0,)), ((), ())), preferred_element_type=F32)


def _bf(x):
    return x.astype(BF16)


def _allgather(arrs):
    na = len(arrs)

    def body(*refs):
        x_refs, o_refs = refs[:na], refs[na:2 * na]
        send_sems, recv_sems, local_sems = refs[2 * na:]
        x, y, c = lax.axis_index("x"), lax.axis_index("y"), lax.axis_index("c")
        me, sibling = (x, y, c), (x, y, 1 - c)
        chips = [(1 - x, y), (x, 1 - y), (1 - x, 1 - y)]

        def copy(a, k, block, to, src=None):
            slot = o_refs[a].at[4 * block[0] + 2 * block[1] + block[2]]
            return pltpu.make_async_remote_copy(
                src_ref=slot if src is None else src, dst_ref=slot, send_sem=send_sems.at[7 * a + k],
                recv_sem=recv_sems.at[7 * a + k], device_id=to, device_id_type=MESH_ID)

        mine = [pltpu.make_async_copy(x_refs[a], o_refs[a].at[4 * x + 2 * y + c], local_sems.at[a]) for a in range(na)]
        for cp in mine:
            cp.start()
        first = []
        for a in range(na):
            first.append(copy(a, 0, me, sibling, src=x_refs[a]))
            first += [copy(a, 1 + j, me, (*chip, c), src=x_refs[a]) for j, chip in enumerate(chips)]
        for cp in first:
            cp.start()
        passed = []
        for j, chip in enumerate(chips):
            for a in range(na):
                copy(a, 1 + j, (*chip, c), me).wait_recv()
                passed.append(copy(a, 4 + j, (*chip, c), sibling))
                passed[-1].start()
        for a in range(na):
            copy(a, 0, sibling, me).wait_recv()
            for j, chip in enumerate(chips):
                copy(a, 4 + j, (*chip, 1 - c), me).wait_recv()
        for cp in first + passed:
            cp.wait_send()
        for cp in mine:
            cp.wait()

    anyspec = pl.BlockSpec(memory_space=pl.ANY)
    return pl.pallas_call(
        body, name="weights_allgather",
        out_shape=[SDS((N_DEV,) + a.shape, a.dtype) for a in arrs],
        in_specs=[anyspec] * na, out_specs=[anyspec] * na,
        scratch_shapes=[pltpu.SemaphoreType.DMA((7 * na,)), pltpu.SemaphoreType.DMA((7 * na,)), pltpu.SemaphoreType.DMA((na,))],
    )(*arrs)


def _grad_exchange(per_dest, shared):
    nd, ns = len(per_dest), len(shared)
    na = nd + ns

    def body(*refs):
        in_refs, out_refs = refs[:na], refs[na:2 * na]
        send_sems, recv_sems, local_sems = refs[2 * na:]
        x, y, c = lax.axis_index("x"), lax.axis_index("y"), lax.axis_index("c")
        me = 4 * x + 2 * y + c

        def src(a, idx):
            return in_refs[a].at[idx] if a < nd else in_refs[a]

        loc = [pltpu.make_async_copy(src(a, me), out_refs[a].at[me], local_sems.at[a]) for a in range(na)]
        for cp in loc:
            cp.start()
        sends, recvs = [], []
        for k in range(1, N_DEV):
            px = 1 - x if k & 4 else x
            py = 1 - y if k & 2 else y
            pc = 1 - c if k & 1 else c
            peer, pidx = (px, py, pc), 4 * px + 2 * py + pc
            for a in range(na):
                sem = 7 * a + k - 1
                sends.append(pltpu.make_async_remote_copy(
                    src_ref=src(a, pidx), dst_ref=out_refs[a].at[me], send_sem=send_sems.at[sem],
                    recv_sem=recv_sems.at[sem], device_id=peer, device_id_type=MESH_ID))
                recvs.append(pltpu.make_async_remote_copy(
                    src_ref=src(a, pidx), dst_ref=out_refs[a].at[pidx], send_sem=send_sems.at[sem],
                    recv_sem=recv_sems.at[sem], device_id=peer, device_id_type=MESH_ID))
        for cp in sends:
            cp.start()
        for cp in recvs:
            cp.wait_recv()
        for cp in sends:
            cp.wait_send()
        for cp in loc:
            cp.wait()

    anyspec = pl.BlockSpec(memory_space=pl.ANY)
    return pl.pallas_call(
        body, name="grad_exchange",
        out_shape=[SDS(a.shape, a.dtype) for a in per_dest] + [SDS((N_DEV,) + a.shape, a.dtype) for a in shared],
        in_specs=[anyspec] * na, out_specs=[anyspec] * na,
        scratch_shapes=[pltpu.SemaphoreType.DMA((7 * na,)), pltpu.SemaphoreType.DMA((7 * na,)), pltpu.SemaphoreType.DMA((na,))],
    )(*per_dest, *shared)


_HBM_SPEC = pl.BlockSpec(memory_space=pltpu.HBM)
_SEM_SPEC = pl.BlockSpec(memory_space=pltpu.SEMAPHORE)
_EFFECT = pltpu.SideEffectType.DATAFLOW_SIDE_EFFECTING


def _peer_list(x, y, c):
    out = []
    for k in range(1, N_DEV):
        px = 1 - x if k & 4 else x
        py = 1 - y if k & 2 else y
        pc = 1 - c if k & 1 else c
        out.append(((px, py, pc), 4 * px + 2 * py + pc))
    return out


def _split_copies(src, land, send_sems, recv_sems, per_dest, receive):
    x, y, c = lax.axis_index("x"), lax.axis_index("y"), lax.axis_index("c")
    me = 4 * x + 2 * y + c
    out = []
    for k, (peer, pidx) in enumerate(_peer_list(x, y, c)):
        for a in range(len(src)):
            out.append(pltpu.make_async_remote_copy(
                src_ref=src[a].at[pidx] if per_dest else src[a], dst_ref=land[a].at[pidx if receive else me],
                send_sem=send_sems.at[7 * a + k], recv_sem=recv_sems.at[7 * a + k], device_id=peer, device_id_type=MESH_ID))
    return out


def _own_copies(src, land, own_sems, per_dest):
    me = 4 * lax.axis_index("x") + 2 * lax.axis_index("y") + lax.axis_index("c")
    return [pltpu.make_async_copy(src[a].at[me] if per_dest else src[a], land[a].at[me], own_sems.at[a]) for a in range(len(src))]


def _exchange_start(arrs, per_dest, name):
    na = len(arrs)
    land_shapes = [a.shape if per_dest else (N_DEV,) + a.shape for a in arrs]
    lands = [pltpu.with_memory_space_constraint(lax.empty(s, a.dtype), pltpu.HBM) for s, a in zip(land_shapes, arrs)]

    def body(*refs):
        src, land = refs[:na], refs[na:2 * na]
        send_sems, recv_sems, own_sems, token = refs[2 * na], refs[2 * na + 1], refs[2 * na + 2], refs[-1]
        for cp in _split_copies(src, land, send_sems, recv_sems, per_dest, False) + _own_copies(src, land, own_sems, per_dest):
            cp.start()
        token[...] = jnp.zeros_like(token)

    outs = pl.pallas_call(
        body, name=name,
        out_shape=[pltpu.SemaphoreType.DMA((7 * na,)), pltpu.SemaphoreType.DMA((7 * na,)), pltpu.SemaphoreType.DMA((na,))]
        + [pltpu.HBM(a.shape, a.dtype) for a in arrs] + [pltpu.HBM(s, a.dtype) for s, a in zip(land_shapes, arrs)]
        + [SDS((8, LANES), F32)],
        in_specs=[_HBM_SPEC] * (2 * na),
        out_specs=[_SEM_SPEC] * 3 + [_HBM_SPEC] * (2 * na) + [pl.BlockSpec(memory_space=pltpu.VMEM)],
        input_output_aliases={k: 3 + k for k in range(2 * na)},
        compiler_params=pltpu.CompilerParams(has_side_effects=_EFFECT),
    )(*[pltpu.with_memory_space_constraint(a, pltpu.HBM) for a in arrs], *lands)
    return (na, per_dest, outs[:-1]), outs[-1]


def _exchange_wait(state, after, name):
    na, per_dest, started = state

    def body(*refs):
        src, land = refs[:na], refs[na:2 * na]
        send_sems, recv_sems, own_sems = refs[2 * na], refs[2 * na + 1], refs[2 * na + 2]
        for cp in _split_copies(src, land, send_sems, recv_sems, per_dest, True):
            cp.wait_send()
            cp.wait_recv()
        for cp in _own_copies(src, land, own_sems, per_dest):
            cp.wait()

    bufs = started[3:]
    outs = pl.pallas_call(
        body, name=name,
        out_shape=[pltpu.HBM(b.shape, b.dtype) for b in bufs],
        in_specs=[_HBM_SPEC] * (2 * na) + [_SEM_SPEC] * 3 + [pl.BlockSpec(memory_space=pl.ANY)],
        out_specs=[_HBM_SPEC] * (2 * na),
        input_output_aliases={k: k for k in range(2 * na)},
        compiler_params=pltpu.CompilerParams(has_side_effects=_EFFECT),
    )(*bufs, started[0], started[1], started[2], after)
    return outs[na:]


def _col_pieces(n, segments):
    out = []
    for j in range(N_DEV):
        lo, hi = j * n, (j + 1) * n
        for (k, s0, w, c0) in segments:
            a, b = max(lo, s0), min(hi, s0 + w)
            if a < b:
                out.append((j, a - lo, b - lo, k, c0 + a - s0, c0 + b - s0))
    return out


def _assemble_cols(slots, groups, out_widths, name):
    r, p = slots.shape[1], slots.shape[2]
    tb = _pick(r, (128, 64, 32, 16, 8))
    covered = [0] * len(out_widths)
    for (_, n, segs) in groups:
        for (k, _, w, _) in segs:
            covered[k] += w

    def body(s_ref, *o_refs):
        for k, wd in enumerate(out_widths):
            if covered[k] < wd:
                o_refs[k][...] = jnp.zeros_like(o_refs[k])
        for (off, n, segs) in groups:
            for (j, a0, a1, k, d0, d1) in _col_pieces(n, segs):
                o_refs[k][:, d0:d1] = s_ref[j, :, off + a0:off + a1]

    return pl.pallas_call(
        body, name=name, grid=(r // tb,), in_specs=[pl.BlockSpec((N_DEV, tb, p), lambda i: (0, i, 0))],
        out_specs=[pl.BlockSpec((tb, w), lambda i: (i, 0)) for w in out_widths],
        out_shape=[SDS((r, w), slots.dtype) for w in out_widths], compiler_params=_cp("arbitrary"))(slots)


def _scatter_cols(pieces, segments, n, name):
    r = pieces[0].shape[0]
    tb = _pick(r, (128, 64, 32, 16, 8))

    def body(*refs):
        p_refs, o_ref = refs[:-1], refs[-1]
        for (j, a0, a1, k, d0, d1) in _col_pieces(n, segments):
            o_ref[j, :, a0:a1] = p_refs[k][:, d0:d1]

    return pl.pallas_call(
        body, name=name, grid=(r // tb,), in_specs=[pl.BlockSpec((tb, a.shape[1]), lambda i: (i, 0)) for a in pieces],
        out_specs=pl.BlockSpec((N_DEV, tb, n), lambda i: (0, i, 0)), out_shape=SDS((N_DEV, r, n), pieces[0].dtype),
        compiler_params=_cp("arbitrary"))(*pieces)


def _modulation(cvec, w_mod, b_mod):
    d, n = w_mod.shape

    def body(c_ref, w_ref, b_ref, s_ref, o_ref):
        cv = c_ref[...]
        s = cv * _sigmoid(cv)
        s_ref[...] = s
        o_ref[...] = _dot(_bf(s), w_ref[...]) + b_ref[...]

    return pl.pallas_call(body, name="modulation", out_shape=(SDS((8, d), F32), SDS((8, n), F32)),
                          compiler_params=_cp())(cvec, w_mod, b_mod)


def _norm_mod_proj(xs, g, shsc, w_main, w_gate, t_rows, name):
    s_rows, d = xs.shape
    n = w_main.shape[1]
    tb = _pick(s_rows, (1280, 1024, 256))
    cb = _pick(n, (1408, 1024, 768, 512, 384, 256, 128))
    gate = w_gate is not None

    def body(*refs):
        if gate:
            x_ref, g_ref, ss_ref, wm_ref, wg_ref, hn_ref, z_ref, zg_ref, hn_sc = refs
        else:
            x_ref, g_ref, ss_ref, wm_ref, hn_ref, z_ref, hn_sc = refs
        i, j = pl.program_id(0), pl.program_id(1)

        @pl.when(j == 0)
        def _():
            x = x_ref[...]
            r = lax.rsqrt(jnp.mean(x * x, axis=-1, keepdims=True) + EPS)
            row = i * tb + lax.broadcasted_iota(jnp.int32, (tb, 1), 0)
            isx = row < t_rows
            sh = jnp.where(isx, ss_ref[0:1, :], ss_ref[2:3, :])
            sc = jnp.where(isx, ss_ref[1:2, :], ss_ref[3:4, :])
            hb = _bf((x * r * g_ref[...]) * (1.0 + sc) + sh)
            hn_sc[...] = hb
            hn_ref[...] = hb
            if gate:
                zg_ref[...] = _dot(hb, wg_ref[...])

        z_ref[...] = _bf(_dot(hn_sc[...], wm_ref[...]))

    in_specs = [pl.BlockSpec((tb, d), lambda i, j: (i, 0)), pl.BlockSpec((1, d), lambda i, j: (0, 0)),
                pl.BlockSpec((4, d), lambda i, j: (0, 0)), pl.BlockSpec((d, cb), lambda i, j: (0, j))]
    out_specs = [pl.BlockSpec((tb, d), lambda i, j: (i, 0)), pl.BlockSpec((tb, cb), lambda i, j: (i, j))]
    out_shape = [SDS((s_rows, d), BF16), SDS((s_rows, n), BF16)]
    args = [xs, g, shsc, w_main]
    if gate:
        in_specs.append(pl.BlockSpec((d, LANES), lambda i, j: (0, 0)))
        out_specs.append(pl.BlockSpec((tb, LANES), lambda i, j: (i, 0)))
        out_shape.append(SDS((s_rows, LANES), F32))
        args.append(w_gate)
    return pl.pallas_call(
        body, name=name, grid=(s_rows // tb, n // cb), in_specs=in_specs, out_specs=out_specs, out_shape=out_shape,
        scratch_shapes=[pltpu.VMEM((tb, d), BF16)], compiler_params=_cp("arbitrary", "arbitrary"))(*args)


def _seg_masks(row, t_rows, s_rows):
    prev_ok = (row != 0) & (row != t_rows)
    next_ok = (row != t_rows - 1) & (row != s_rows - 1)
    return prev_ok, next_ok


def _shift_rows(z, halo_prev, halo_next, tb):
    loc = lax.broadcasted_iota(jnp.int32, (tb, 1), 0)
    zp = jnp.where(loc == 0, halo_prev, pltpu.roll(z, 1, 0))
    zn = jnp.where(loc == tb - 1, halo_next, pltpu.roll(z, tb - 1, 0))
    return zp, zn


def _qk_conv(z_main, conv_w, t_rows, md, qscale):
    s_rows = z_main.shape[0]
    tb = _pick(s_rows, (1280, 1024, 256))
    cb = _pick(md, (512, 256, 128))
    nb8 = tb // 8

    def body(zm, zp, zn, w_ref, o_ref):
        i, j = pl.program_id(0), pl.program_id(1)
        z = zm[...].astype(F32)
        zprev, znext = _shift_rows(z, zp[7:8, :].astype(F32), zn[0:1, :].astype(F32), tb)
        row = i * tb + lax.broadcasted_iota(jnp.int32, (tb, 1), 0)
        prev_ok, next_ok = _seg_masks(row, t_rows, s_rows)
        pre = (w_ref[0:1, :] * jnp.where(prev_ok, zprev, 0.0) + w_ref[1:2, :] * z
               + w_ref[2:3, :] * jnp.where(next_ok, znext, 0.0))
        scale = jnp.where(j * cb < md, qscale, 1.0)
        o_ref[...] = _bf(pre * _sigmoid(pre) * scale)

    return pl.pallas_call(
        body, name="qk_conv", grid=(s_rows // tb, 2 * md // cb),
        in_specs=[pl.BlockSpec((tb, cb), lambda i, j: (i, j)),
                  pl.BlockSpec((8, cb), lambda i, j: (jnp.maximum(i * nb8 - 1, 0), j)),
                  pl.BlockSpec((8, cb), lambda i, j: (jnp.minimum((i + 1) * nb8, s_rows // 8 - 1), j)),
                  pl.BlockSpec((8, cb), lambda i, j: (0, j))],
        out_specs=pl.BlockSpec((tb, cb), lambda i, j: (i, j)),
        out_shape=SDS((s_rows, 2 * md), BF16), compiler_params=_cp("arbitrary", "arbitrary"))(z_main, z_main, z_main, conv_w)


def _chunk_gates(gates, bias, rev):
    ln = gates.shape[0]
    gz = gates + bias
    logf = _log_sigmoid(gz)
    r_id = lax.broadcasted_iota(jnp.int32, (ln, ln), 0)
    c_id = lax.broadcasted_iota(jnp.int32, (ln, ln), 1)
    mask = (c_id >= r_id) if rev else (c_id <= r_id)
    mb = mask.astype(F32).astype(BF16)
    b_all = _mask_dot(mb, logf)
    g_all = jnp.sum(logf, axis=0, keepdims=True)
    return gz, b_all, b_all.T, gz.T, g_all, mask, mb


def _head_weights(b_col, b_row, i_row, m_in, mask):
    d = jnp.where(mask, b_col - b_row + i_row, NEG)
    inter = b_col + m_in
    m_row = jnp.maximum(inter, jnp.max(d, axis=1, keepdims=True))
    return jnp.exp(d - m_row), jnp.exp(inter - m_row), m_row


def _head_state_coeffs(g, b_col, i_col, m_in):
    a = g - b_col + i_col
    m_new = jnp.maximum(g + m_in, jnp.max(a, axis=0, keepdims=True))
    return jnp.exp(g + m_in - m_new), jnp.exp(a - m_new), m_new


def _mlstm_fwd(qk, z_main, zg, bias, nh):
    s_rows = qk.shape[0]
    md = qk.shape[1] // 2
    dh = md // nh
    nc = s_rows // LCH
    ln = LCH

    def chunk_f(i):
        return jnp.where(i == 0, nc - 1, i - 1)

    def chunk_b(i):
        return jnp.where(i == 0, nc - 1, nc - 1 - i)

    def body(qf, kf, vf, gf, qb, kb, vb, gb, bias_ref, hf_ref, hb_ref, cf_ref, nf_ref, mf_ref, cb_ref, nb_ref, mb_ref,
             c_sc, n_sc, m_sc):
        i = pl.program_id(0)

        @pl.when(i == 0)
        def _():
            c_sc[...] = jnp.zeros_like(c_sc)
            n_sc[...] = jnp.zeros_like(n_sc)
            m_sc[...] = jnp.full(m_sc.shape, M_INIT, F32)

        for dr, (q_ref, k_ref, v_ref, g_ref, h_ref, c_out, n_out, m_out) in enumerate(
                ((qf, kf, vf, gf, hf_ref, cf_ref, nf_ref, mf_ref), (qb, kb, vb, gb, hb_ref, cb_ref, nb_ref, mb_ref))):
            gz, b_all, b_t, g_t, g_all, mask, _ = _chunk_gates(g_ref[...], bias_ref[...], dr == 1)
            for h in range(nh):
                ci, cf = 2 * dr * nh + h, (2 * dr + 1) * nh + h
                sl = slice(h * dh, (h + 1) * dh)
                q, k, v = q_ref[:, sl], k_ref[:, sl], v_ref[:, sl]
                c_in, n_in, m_in = c_sc[dr, h], n_sc[dr, h, 0:1, :], m_sc[dr, h, 0:1, 0:1]
                c_out[sl, :] = c_in
                n_out[:, sl] = n_sc[dr, h]
                m_out[h] = m_sc[dr, h]
                b_col, b_row, i_col, i_row = b_all[:, cf:cf + 1], b_t[cf:cf + 1, :], gz[:, ci:ci + 1], g_t[ci:ci + 1, :]
                g = g_all[:, cf:cf + 1]
                w, w_int, m_row = _head_weights(b_col, b_row, i_row, m_in, mask)
                s_mat = _dot_nt(q, k) * w
                num = _dot(_bf(s_mat), v) + w_int * _dot(q, _bf(c_in))
                den = jnp.sum(s_mat, axis=1, keepdims=True) + w_int * jnp.sum(q.astype(F32) * n_in, axis=1, keepdims=True)
                h_ref[:, sl] = _bf(num / jnp.maximum(jnp.abs(den), jnp.exp(-m_row)))
                a_old, coef, m_new = _head_state_coeffs(g, b_col, i_col, m_in)
                kw = k.astype(F32) * coef
                c_sc[dr, h] = a_old * c_in + _dot_tn(_bf(kw), v)
                n_sc[dr, h] = jnp.broadcast_to(a_old * n_in + jnp.sum(kw, axis=0, keepdims=True), (8, dh))
                m_sc[dr, h] = jnp.broadcast_to(m_new, (8, LANES))

    def tok(cfn, col):
        return pl.BlockSpec((ln, md), lambda i: (cfn(i), col))

    def gat(cfn):
        return pl.BlockSpec((ln, LANES), lambda i: (cfn(i), 0))

    def st(cfn, shape):
        return pl.BlockSpec((None,) + shape, lambda i: (cfn(i),) + (0,) * len(shape))

    st_shapes = ((nh * dh, dh), (8, md), (nh, 8, LANES))
    return pl.pallas_call(
        body, name="mlstm_fwd", grid=(nc,),
        in_specs=[tok(chunk_f, 0), tok(chunk_f, 1), tok(chunk_f, 2), gat(chunk_f),
                  tok(chunk_b, 0), tok(chunk_b, 1), tok(chunk_b, 2), gat(chunk_b),
                  pl.BlockSpec((1, LANES), lambda i: (0, 0))],
        out_specs=[tok(chunk_f, 0), tok(chunk_b, 0)] + [st(chunk_f, s) for s in st_shapes] + [st(chunk_b, s) for s in st_shapes],
        out_shape=[SDS((s_rows, md), BF16)] * 2 + [SDS((nc,) + s, F32) for s in st_shapes] * 2,
        scratch_shapes=[pltpu.VMEM((2, nh, dh, dh), F32), pltpu.VMEM((2, nh, 8, dh), F32), pltpu.VMEM((2, nh, 8, LANES), F32)],
        compiler_params=_cp("arbitrary"))(qk, qk, z_main, zg, qk, qk, z_main, zg, bias)


def _head_rms(hs, nh, dh):
    parts, scales = [], []
    for h in range(nh):
        hh = hs[:, h * dh:(h + 1) * dh]
        r = lax.rsqrt(jnp.mean(hh * hh, axis=-1, keepdims=True) + EPS)
        parts.append(hh * r)
        scales.append(r)
    return jnp.concatenate(parts, axis=1), scales


def _layer_norm(v):
    vc = v - jnp.mean(v, axis=-1, keepdims=True)
    r = lax.rsqrt(jnp.mean(vc * vc, axis=-1, keepdims=True) + EPS)
    return vc * r, r


def _sgu_mix(vnb, ws_ref, bs_ref, tb, ng, gd, sc):
    rows = []
    for ch in range(tb // sc):
        cols = []
        for g in range(ng):
            blk = vnb[ch * sc:(ch + 1) * sc, g * gd:(g + 1) * gd]
            cols.append(_dot(_bf(ws_ref[g]), blk) + bs_ref[:, g:g + 1])
        rows.append(jnp.concatenate(cols, axis=1))
    return jnp.concatenate(rows, axis=0)


def _mixer_fwd(hf, hb, z_main, xs, hg, lng, lnb, w_s, b_st, wbm, wbs, wout, mx2, t_rows, nh):
    d = xs.shape[1]
    ng, sc = w_s.shape[0], w_s.shape[1]
    dh, gd = d // nh, d // ng
    tb = _pick(t_rows, (256,))

    def body(hf_ref, hb_ref, zo, zu, zv, zgm, zgg, x_ref, hg_ref, lng_ref, lnb_ref, ws_ref, bs_ref, wbm_ref, wbs_ref,
             wo_ref, mx2_ref, h1_ref, ym_ref, ys_ref, pm_ref, ps_ref, y_ref, out_ref):
        hs = hf_ref[...].astype(F32) + hb_ref[...].astype(F32)
        hn, _ = _head_rms(hs, nh, dh)
        ym = _bf(_sigmoid(zo[...].astype(F32)) * (hn * hg_ref[...]))
        ym_ref[...] = ym
        vhat, _ = _layer_norm(_gelu(zv[...].astype(F32)))
        vnb = _bf(vhat * lng_ref[...] + lnb_ref[...])
        ys = _bf(_gelu(zu[...].astype(F32)) * _sgu_mix(vnb, ws_ref, bs_ref, tb, ng, gd, sc))
        ys_ref[...] = ys
        pm = _dot(ym, wbm_ref[...])
        ps = _dot(ys, wbs_ref[...])
        pm_ref[...] = _bf(pm)
        ps_ref[...] = _bf(ps)
        y = _bf(_sigmoid(zgm[...].astype(F32)) * pm + _sigmoid(zgg[...].astype(F32)) * ps)
        y_ref[...] = y
        out = _dot(y, wo_ref[...])
        out_ref[...] = _bf(out)
        h1_ref[...] = x_ref[...] + mx2_ref[...] * out

    def tok(col):
        return pl.BlockSpec((tb, d), lambda i: (i, col))

    def full(shape):
        return pl.BlockSpec(shape, lambda i: (0,) * len(shape))

    return pl.pallas_call(
        body, name="mixer_fwd", grid=(t_rows // tb,),
        in_specs=[tok(0), tok(0), tok(3), tok(4), tok(5), tok(6), tok(7), tok(0), full((1, d)), full((1, d)), full((1, d)),
                  full((ng, sc, sc)), full((sc, LANES)), full((d, d)), full((d, d)), full((d, d)), full((1, d))],
        out_specs=[tok(0)] * 7,
        out_shape=[SDS((t_rows, d), F32)] + [SDS((t_rows, d), BF16)] * 6,
        compiler_params=_cp("arbitrary"))(hf, hb, z_main, z_main, z_main, z_main, z_main, xs, hg, lng, lnb, w_s, b_st,
                                          wbm, wbs, wout, mx2)


def _grid_taps(a_ext, n_ext):
    col = lax.broadcasted_iota(jnp.int32, (n_ext, 1), 0) % GRID_W
    left = jnp.where(col != 0, pltpu.roll(a_ext, 1, 0), 0.0)
    right = jnp.where(col != GRID_W - 1, pltpu.roll(a_ext, n_ext - 1, 0), 0.0)
    return left, right


def _with_halo(prev, main, nxt, i, ni, tb):
    ext = jnp.concatenate([prev, main, nxt], axis=0).astype(F32)
    pos = lax.broadcasted_iota(jnp.int32, (tb + 2 * GRID_W, 1), 0)
    inside = ((pos >= GRID_W) | (i > 0)) & ((pos < tb + GRID_W) | (i < ni - 1))
    return jnp.where(inside, ext, 0.0)


def _halo_specs(tb, cb, t_rows, col0=0):
    nh64 = tb // GRID_W
    return [pl.BlockSpec((tb, cb), lambda i, j: (i, col0 + j)),
            pl.BlockSpec((GRID_W, cb), lambda i, j: (jnp.maximum(i * nh64 - 1, 0), col0 + j)),
            pl.BlockSpec((GRID_W, cb), lambda i, j: (jnp.minimum((i + 1) * nh64, t_rows // GRID_W - 1), col0 + j))]


def _ffn_tail(ab, w_conv9, w_down, h1, mx5, gfin, target, dff):
    t_rows, d = h1.shape
    tb = _pick(t_rows, (256,))
    cb = _pick(dff, (1408, 256, 128))
    ni, nj = t_rows // tb, dff // cb
    n_ext = tb + 2 * GRID_W

    def body(am, ap, an, b_ref, wc_ref, wd_ref, h1_ref, mx5_ref, gf_ref, tg_ref, ac_ref, f_ref, dh2_ref, dffn_ref, st_ref, acc):
        i, j = pl.program_id(0), pl.program_id(1)
        a_ext = _with_halo(ap[...], am[...], an[...], i, ni, tb)
        left, right = _grid_taps(a_ext, n_ext)
        conv = jnp.zeros((tb, cb), F32)
        for di in range(3):
            o = di * GRID_W
            conv = conv + (wc_ref[3 * di:3 * di + 1, :] * left[o:o + tb] + wc_ref[3 * di + 1:3 * di + 2, :] * a_ext[o:o + tb]
                           + wc_ref[3 * di + 2:3 * di + 3, :] * right[o:o + tb])
        ac_ref[...] = _bf(conv)
        fb = _bf(conv * _sigmoid(conv) * b_ref[...].astype(F32))
        f_ref[...] = fb

        @pl.when(j == 0)
        def _():
            acc[...] = jnp.zeros_like(acc)

        @pl.when((i == 0) & (j == 0))
        def _():
            st_ref[...] = jnp.zeros_like(st_ref)

        acc[...] += _dot(fb, wd_ref[...])

        @pl.when(j == nj - 1)
        def _():
            ffn = acc[...]
            h2 = h1_ref[...] + mx5_ref[...] * ffn
            r = lax.rsqrt(jnp.mean(h2 * h2, axis=-1, keepdims=True) + EPS)
            xn = h2 * r
            e = xn * gf_ref[...] - tg_ref[...]
            loss = 0.5 * jnp.sum(jnp.sum(e * e, axis=1, keepdims=True), axis=0, keepdims=True) / d
            dy = e * (1.0 / d)
            dxn = dy * gf_ref[...]
            dh2 = r * (dxn - xn * jnp.mean(dxn * xn, axis=-1, keepdims=True))
            dh2_ref[...] = dh2
            dffn_ref[...] = _bf(dh2 * mx5_ref[...])
            st_ref[...] += jnp.concatenate(
                [jnp.sum(dy * xn, axis=0, keepdims=True), jnp.sum(dh2 * ffn, axis=0, keepdims=True),
                 jnp.broadcast_to(loss, (1, d)), jnp.zeros((5, d), F32)], axis=0)

    def tokd():
        return pl.BlockSpec((tb, d), lambda i, j: (i, 0))

    def rowd():
        return pl.BlockSpec((1, d), lambda i, j: (0, 0))

    return pl.pallas_call(
        body, name="ffn_tail", grid=(ni, nj),
        in_specs=_halo_specs(tb, cb, t_rows) + [pl.BlockSpec((tb, cb), lambda i, j: (i, nj + j)),
                                                pl.BlockSpec((16, cb), lambda i, j: (0, j)),
                                                pl.BlockSpec((cb, d), lambda i, j: (j, 0)), tokd(), rowd(), rowd(), tokd()],
        out_specs=[pl.BlockSpec((tb, cb), lambda i, j: (i, j)), pl.BlockSpec((tb, cb), lambda i, j: (i, j)), tokd(), tokd(),
                   pl.BlockSpec((8, d), lambda i, j: (0, 0))],
        out_shape=[SDS((t_rows, dff), BF16), SDS((t_rows, dff), BF16), SDS((t_rows, d), F32), SDS((t_rows, d), BF16),
                   SDS((8, d), F32)],
        scratch_shapes=[pltpu.VMEM((tb, d), F32)],
        compiler_params=_cp("arbitrary", "arbitrary"))(ab, ab, ab, ab, w_conv9, w_down, h1, mx5, gfin, target)


def _ffn_bwd_gate(dffn, w_down, aconv, ab, dff):
    t_rows, d = dffn.shape
    tb = _pick(t_rows, (512,))
    cb = _pick(dff, (1408, 256, 128))
    nj = dff // cb

    def body(g_ref, wd_ref, ac_ref, b_ref, db_ref, dac_ref):
        df = _dot_nt(g_ref[...], wd_ref[...])
        ac = ac_ref[...].astype(F32)
        sa = _sigmoid(ac)
        db_ref[...] = _bf(df * ac * sa)
        dac_ref[...] = _bf(df * b_ref[...].astype(F32) * (sa * (1.0 + ac * (1.0 - sa))))

    blk = pl.BlockSpec((tb, cb), lambda i, j: (i, j))
    return pl.pallas_call(
        body, name="ffn_bwd_gate", grid=(t_rows // tb, nj),
        in_specs=[pl.BlockSpec((tb, d), lambda i, j: (i, 0)), pl.BlockSpec((cb, d), lambda i, j: (j, 0)), blk,
                  pl.BlockSpec((tb, cb), lambda i, j: (i, nj + j))],
        out_specs=[blk, blk], out_shape=[SDS((t_rows, dff), BF16)] * 2,
        compiler_params=_cp("arbitrary", "arbitrary"))(dffn, w_down, aconv, ab)


def _ffn_conv_bwd(dac, ab, w_conv9, dff):
    t_rows = dac.shape[0]
    tb = _pick(t_rows, (256,))
    cb = _pick(dff, (1408, 256, 128))
    ni, nj = t_rows // tb, dff // cb
    n_ext = tb + 2 * GRID_W
    nh64 = tb // GRID_W

    def body(dm, dp, dn, am, ap, an, wc_ref, da_ref, gw_ref):
        i = pl.program_id(1)
        d_ext = _with_halo(dp[...], dm[...], dn[...], i, ni, tb)
        a_ext = _with_halo(ap[...], am[...], an[...], i, ni, tb)
        d_left, d_right = _grid_taps(d_ext, n_ext)
        a_left, a_right = _grid_taps(a_ext, n_ext)
        dmain = d_ext[GRID_W:GRID_W + tb]
        da = jnp.zeros((tb, cb), F32)
        rows = []
        for di in range(3):
            o = (2 - di) * GRID_W
            da = da + (wc_ref[3 * di:3 * di + 1, :] * d_right[o:o + tb] + wc_ref[3 * di + 1:3 * di + 2, :] * d_ext[o:o + tb]
                       + wc_ref[3 * di + 2:3 * di + 3, :] * d_left[o:o + tb])
            o = di * GRID_W
            for tap in (a_left, a_ext, a_right):
                rows.append(jnp.sum(dmain * tap[o:o + tb], axis=0, keepdims=True))
        da_ref[...] = _bf(da)

        @pl.when(i == 0)
        def _():
            gw_ref[...] = jnp.zeros_like(gw_ref)

        gw_ref[...] += jnp.concatenate(rows + [jnp.zeros((7, cb), F32)], axis=0)

    def halo(col0):
        return [pl.BlockSpec((tb, cb), lambda j, i: (i, col0 + j)),
                pl.BlockSpec((GRID_W, cb), lambda j, i: (jnp.maximum(i * nh64 - 1, 0), col0 + j)),
                pl.BlockSpec((GRID_W, cb), lambda j, i: (jnp.minimum((i + 1) * nh64, t_rows // GRID_W - 1), col0 + j))]

    return pl.pallas_call(
        body, name="ffn_conv_bwd", grid=(nj, ni),
        in_specs=halo(0) + halo(0) + [pl.BlockSpec((16, cb), lambda j, i: (0, j))],
        out_specs=[pl.BlockSpec((tb, cb), lambda j, i: (i, j)), pl.BlockSpec((16, cb), lambda j, i: (0, j))],
        out_shape=[SDS((t_rows, dff), BF16), SDS((16, dff), F32)],
        compiler_params=_cp("arbitrary", "arbitrary"))(dac, dac, dac, ab, ab, ab, w_conv9)


def _proj_norm_bwd(pairs, x_arr, x_row0, g, scale, resid, m_rows, name):
    d = x_arr.shape[1]
    tm = _pick(m_rows, (1024, 256))
    te = 256
    ni = m_rows // tm
    once = pl.Buffered(1)
    starts, total = [], 0
    for (_, _, _, _, k_p, tk_p) in pairs:
        starts.append(total)
        total += k_p // tk_p
    npairs = len(pairs)
    has_dx = resid is not None

    def body(*refs):
        a_refs, b_refs = refs[0:2 * npairs:2], refs[1:2 * npairs:2]
        rest = refs[2 * npairs:]
        if has_dx:
            x_ref, g_ref, sc_ref, r_ref, dx_ref, st_ref, acc = rest
        else:
            x_ref, g_ref, sc_ref, st_ref, acc = rest
        i, k = pl.program_id(0), pl.program_id(1)

        @pl.when(k == 0)
        def _():
            acc[...] = jnp.zeros_like(acc)

        @pl.when((i == 0) & (k == 0))
        def _():
            st_ref[...] = jnp.zeros_like(st_ref)

        for p in range(npairs):
            nk = pairs[p][4] // pairs[p][5]

            @pl.when((k >= starts[p]) & (k < starts[p] + nk))
            def _(p=p):
                acc[...] += _dot_nt(a_refs[p][...], b_refs[p][...])

        @pl.when(k == total - 1)
        def _():
            sums = [jnp.zeros((1, d), F32)] * 3
            for r0 in range(0, tm, te):
                rows = slice(r0, r0 + te)
                dhn = acc[rows, :]
                x = x_ref[rows, :]
                r = lax.rsqrt(jnp.mean(x * x, axis=-1, keepdims=True) + EPS)
                xn = x * r
                dmod = dhn * (1.0 + sc_ref[...])
                dxn = dmod * g_ref[...]
                if has_dx:
                    dx_ref[rows, :] = r * (dxn - xn * jnp.mean(dxn * xn, axis=-1, keepdims=True)) + r_ref[rows, :]
                sums = [sums[0] + jnp.sum(dmod * xn, axis=0, keepdims=True), sums[1] + jnp.sum(dhn, axis=0, keepdims=True),
                        sums[2] + jnp.sum(dhn * (xn * g_ref[...]), axis=0, keepdims=True)]
            st_ref[...] += jnp.concatenate(sums + [jnp.zeros((5, d), F32)], axis=0)

    in_specs, args = [], []
    for p, (a, a_row0, b, b_col0, k_p, tk_p) in enumerate(pairs):
        nk, s0, ar, bc = k_p // tk_p, starts[p], a_row0 // tm, b_col0 // tk_p

        def kk(k, s0=s0, nk=nk):
            return jnp.clip(k - s0, 0, nk - 1)

        in_specs.append(pl.BlockSpec((tm, tk_p), lambda i, k, ar=ar, kk=kk: (ar + i, kk(k))))
        in_specs.append(pl.BlockSpec((d, tk_p), lambda i, k, bc=bc, kk=kk: (0, bc + kk(k))))
        args += [a, b]
    xr = x_row0 // tm
    in_specs += [pl.BlockSpec((tm, d), lambda i, k: (xr + i, 0), pipeline_mode=once), pl.BlockSpec((1, d), lambda i, k: (0, 0)),
                 pl.BlockSpec((1, d), lambda i, k: (0, 0))]
    args += [x_arr, g, scale]
    out_specs, out_shape = [], []
    if has_dx:
        in_specs.append(pl.BlockSpec((tm, d), lambda i, k: (i, 0), pipeline_mode=once))
        args.append(resid)
        out_specs.append(pl.BlockSpec((tm, d), lambda i, k: (i, 0)))
        out_shape.append(SDS((m_rows, d), F32))
    out_specs.append(pl.BlockSpec((8, d), lambda i, k: (0, 0)))
    out_shape.append(SDS((8, d), F32))
    return pl.pallas_call(
        body, name=name, grid=(ni, total), in_specs=in_specs, out_specs=out_specs, out_shape=out_shape,
        scratch_shapes=[pltpu.VMEM((tm, d), F32)], compiler_params=_cp("arbitrary", "arbitrary"))(*args)


def _wgrad(a, b, k_rows, name):
    m, n = a.shape[1], b.shape[1]
    tm = _pick(m, (1408, 1024, 512, 384, 256, 128))
    tn = _pick(n, (1408, 1024, 768, 512, 384, 256, 128))
    tk = _pick(k_rows, (1280, 1024, 256))
    nk = k_rows // tk

    def body(a_ref, b_ref, o_ref, acc):
        k = pl.program_id(2)

        @pl.when(k == 0)
        def _():
            acc[...] = jnp.zeros_like(acc)

        acc[...] += _dot_tn(a_ref[...], b_ref[...])

        @pl.when(k == nk - 1)
        def _():
            o_ref[...] = _bf(acc[...])

    return pl.pallas_call(
        body, name=name, grid=(m // tm, n // tn, nk),
        in_specs=[pl.BlockSpec((tk, tm), lambda i, j, k: (k, i)), pl.BlockSpec((tk, tn), lambda i, j, k: (k, j))],
        out_specs=pl.BlockSpec((tm, tn), lambda i, j, k: (i, j)), out_shape=SDS((m, n), BF16),
        scratch_shapes=[pltpu.VMEM((tm, tn), F32)],
        compiler_params=_cp("arbitrary", "arbitrary", "arbitrary"))(a, b)


def _lane_put(col, lane_idx):
    lane = lax.broadcasted_iota(jnp.int32, (1, LANES), 1)
    return jnp.where(lane == lane_idx, col, 0.0)


def _mixer_bwd(dh1, out, hf, hb, z_main, pm, ps, hg, lng, lnb, w_s, b_st, wbm, wbs, wout, mx2, t_rows, nh):
    d = dh1.shape[1]
    ng, sc = w_s.shape[0], w_s.shape[1]
    dh, gd = d // nh, d // ng
    tb = _pick(t_rows, (256,))

    def body(dh1_ref, out_ref, hf_ref, hb_ref, zo, zu, zv, zgm, zgg, pm_ref, ps_ref, hg_ref, lng_ref, lnb_ref, ws_ref, bs_ref,
             wbm_ref, wbs_ref, wo_ref, mx2_ref, dz_ref, dhs_ref, dout_ref, dpm_ref, dps_ref, st_ref, dws_ref, dbs_ref):
        i = pl.program_id(0)

        @pl.when(i == 0)
        def _():
            st_ref[...] = jnp.zeros_like(st_ref)
            dws_ref[...] = jnp.zeros_like(dws_ref)
            dbs_ref[...] = jnp.zeros_like(dbs_ref)

        dh1v = dh1_ref[...]
        doutb = _bf(dh1v * mx2_ref[...])
        dout_ref[...] = doutb
        d_mx2 = jnp.sum(dh1v * out_ref[...].astype(F32), axis=0, keepdims=True)
        dy = _dot_nt(doutb, wo_ref[...])
        sgm, sgg = _sigmoid(zgm[...].astype(F32)), _sigmoid(zgg[...].astype(F32))
        dpmb, dpsb = _bf(dy * sgm), _bf(dy * sgg)
        dpm_ref[...] = dpmb
        dps_ref[...] = dpsb
        dz_ref[:, 3 * d:4 * d] = _bf(dy * pm_ref[...].astype(F32) * sgm * (1.0 - sgm))
        dz_ref[:, 4 * d:5 * d] = _bf(dy * ps_ref[...].astype(F32) * sgg * (1.0 - sgg))
        dym = _dot_nt(dpmb, wbm_ref[...])
        dys = _dot_nt(dpsb, wbs_ref[...])
        hs = hf_ref[...].astype(F32) + hb_ref[...].astype(F32)
        hn, scales = _head_rms(hs, nh, dh)
        so = _sigmoid(zo[...].astype(F32))
        dz_ref[:, 0:d] = _bf(dym * (hn * hg_ref[...]) * so * (1.0 - so))
        dhmn = dym * so
        d_hg = jnp.sum(dhmn * hn, axis=0, keepdims=True)
        dhn = dhmn * hg_ref[...]
        for h in range(nh):
            sl = slice(h * dh, (h + 1) * dh)
            dhs_ref[:, sl] = _bf(scales[h] * (dhn[:, sl] - hn[:, sl] * jnp.mean(dhn[:, sl] * hn[:, sl], axis=-1, keepdims=True)))
        zuv, zvv = zu[...].astype(F32), zv[...].astype(F32)
        u = _gelu(zuv)
        vhat, rstd = _layer_norm(_gelu(zvv))
        vnb = _bf(vhat * lng_ref[...] + lnb_ref[...])
        mixed = _sgu_mix(vnb, ws_ref, bs_ref, tb, ng, gd, sc)
        dz_ref[:, d:2 * d] = _bf(dys * mixed * _gelu_grad(zuv))
        dmix = dys * u
        rows = []
        dbs = jnp.zeros((sc, LANES), F32)
        for ch in range(tb // sc):
            cols = []
            for g in range(ng):
                dm = dmix[ch * sc:(ch + 1) * sc, g * gd:(g + 1) * gd]
                dmb = _bf(dm)
                dws_ref[g] += _dot_nt(dmb, vnb[ch * sc:(ch + 1) * sc, g * gd:(g + 1) * gd])
                dbs = dbs + _lane_put(jnp.sum(dm, axis=1, keepdims=True), g)
                cols.append(_dot_tn(_bf(ws_ref[g]), dmb))
            rows.append(jnp.concatenate(cols, axis=1))
        dbs_ref[...] += dbs
        dvn = jnp.concatenate(rows, axis=0)
        d_lng = jnp.sum(dvn * vhat, axis=0, keepdims=True)
        d_lnb = jnp.sum(dvn, axis=0, keepdims=True)
        dvh = dvn * lng_ref[...]
        dvg = rstd * (dvh - jnp.mean(dvh, axis=-1, keepdims=True) - vhat * jnp.mean(dvh * vhat, axis=-1, keepdims=True))
        dz_ref[:, 2 * d:3 * d] = _bf(dvg * _gelu_grad(zvv))
        st_ref[...] += jnp.concatenate([d_mx2, d_hg, d_lng, d_lnb, jnp.zeros((4, d), F32)], axis=0)

    def tok(col):
        return pl.BlockSpec((tb, d), lambda i: (i, col))

    def full(shape):
        return pl.BlockSpec(shape, lambda i: (0,) * len(shape))

    return pl.pallas_call(
        body, name="mixer_bwd", grid=(t_rows // tb,),
        in_specs=[tok(0), tok(0), tok(0), tok(0), tok(3), tok(4), tok(5), tok(6), tok(7), tok(0), tok(0), full((1, d)),
                  full((1, d)), full((1, d)), full((ng, sc, sc)), full((sc, LANES)), full((d, d)), full((d, d)), full((d, d)),
                  full((1, d))],
        out_specs=[pl.BlockSpec((tb, 5 * d), lambda i: (i, 0)), tok(0), tok(0), tok(0), tok(0), full((8, d)), full((ng, sc, sc)),
                   full((sc, LANES))],
        out_shape=[SDS((t_rows, 5 * d), BF16)] + [SDS((t_rows, d), BF16)] * 4 + [SDS((8, d), F32), SDS((ng, sc, sc), F32),
                                                                                SDS((sc, LANES), F32)],
        compiler_params=_cp("arbitrary"))(dh1, out, hf, hb, z_main, z_main, z_main, z_main, z_main, pm, ps, hg, lng, lnb, w_s,
                                          b_st, wbm, wbs, wout, mx2)


def _mlstm_bwd(qk, z_main, zg, bias, dhs, states_f, states_b, nh, t_rows):
    s_rows = qk.shape[0]
    md = qk.shape[1] // 2
    dh = md // nh
    nc = s_rows // LCH
    nx = t_rows // LCH
    ln = LCH

    def chunk_f(i):
        return jnp.where(i == nc - 1, nc - 1, nc - 2 - i)

    def chunk_b(i):
        return jnp.where(i == nc - 1, nc - 1, i)

    def body(qf, kf, vf, gf, dhf, cf, nf, mf_, qb, kb, vb, gb, dhb, cb, nb, mb_, bias_ref, dqkvf_ref, dgf_ref, dqkvb_ref, dgb_ref,
             dc_sc, dn_sc):
        i = pl.program_id(0)
        is_ctx = i == nc - 1

        @pl.when(i == 0)
        def _():
            dc_sc[...] = jnp.zeros_like(dc_sc)
            dn_sc[...] = jnp.zeros_like(dn_sc)

        for dr, (q_ref, k_ref, v_ref, g_ref, dh_ref, c_ref, n_ref, m_ref, dqkv_ref, dg_ref) in enumerate(
                ((qf, kf, vf, gf, dhf, cf, nf, mf_, dqkvf_ref, dgf_ref), (qb, kb, vb, gb, dhb, cb, nb, mb_, dqkvb_ref, dgb_ref))):
            gz, b_all, b_t, g_t, g_all, mask, mfl = _chunk_gates(g_ref[...], bias_ref[...], dr == 1)
            x1 = jnp.zeros((ln, LANES), F32)
            x2 = jnp.zeros((ln, LANES), F32)
            dig = jnp.zeros((ln, LANES), F32)
            e_row = jnp.zeros((1, LANES), F32)
            for h in range(nh):
                ci, cfl = 2 * dr * nh + h, (2 * dr + 1) * nh + h
                sl = slice(h * dh, (h + 1) * dh)
                q, k, v = q_ref[:, sl], k_ref[:, sl], v_ref[:, sl]
                qf32, kf32 = q.astype(F32), k.astype(F32)
                dhv = jnp.where(is_ctx, 0.0, dh_ref[:, sl].astype(F32))
                c_in, n_in, m_in = c_ref[sl, :], n_ref[0:1, sl], m_ref[h, 0:1, 0:1]
                b_col, b_row, i_col, i_row = b_all[:, cfl:cfl + 1], b_t[cfl:cfl + 1, :], gz[:, ci:ci + 1], g_t[ci:ci + 1, :]
                g = g_all[:, cfl:cfl + 1]
                w, w_int, m_row = _head_weights(b_col, b_row, i_row, m_in, mask)
                s_mat = _dot_nt(q, k) * w
                sb, cb16 = _bf(s_mat), _bf(c_in)
                num = _dot(sb, v) + w_int * _dot(q, cb16)
                den = jnp.sum(s_mat, axis=1, keepdims=True) + w_int * jnp.sum(qf32 * n_in, axis=1, keepdims=True)
                e_m = jnp.exp(-m_row)
                dnm = jnp.maximum(jnp.abs(den), e_m)
                dnum = dhv / dnm
                hdh = jnp.sum((num / dnm) * dhv, axis=1, keepdims=True)
                dden = jnp.where(jnp.abs(den) > e_m, -(hdh / dnm) * jnp.sign(den), 0.0)
                dnum_b = _bf(dnum)
                ds = _dot_nt(dnum_b, v) + dden
                pb = _bf(w * ds)
                gmat = s_mat * ds
                a_old, coef, _ = _head_state_coeffs(g, b_col, i_col, m_in)
                dc_new, dn_new = dc_sc[dr, h], dn_sc[dr, h, 0:1, :]
                dcb = _bf(dc_new)
                dv = _dot_tn(sb, dnum_b) + _dot(_bf(kf32 * coef), dcb)
                dq_inter = w_int * (_dot_nt(dnum_b, cb16) + dden * n_in)
                dq = _dot(pb, k) + dq_inter
                dk_state = coef * (_dot_nt(v, dcb) + dn_new)
                dk = _dot_tn(pb, q) + dk_state
                dqkv_ref[:, sl] = _bf(dq)
                dqkv_ref[:, md + h * dh:md + (h + 1) * dh] = _bf(dk)
                dqkv_ref[:, 2 * md + h * dh:2 * md + (h + 1) * dh] = _bf(dv)
                row_intra = jnp.sum(gmat, axis=1, keepdims=True)
                col_intra = jnp.sum(gmat.T, axis=1, keepdims=True)
                row_inter = jnp.sum(qf32 * dq_inter, axis=1, keepdims=True)
                col_inter = jnp.sum(kf32 * dk_state, axis=1, keepdims=True)
                e_old = a_old * (jnp.sum(jnp.sum(dc_new * c_in, axis=1, keepdims=True), axis=0, keepdims=True)
                                 + jnp.sum(dn_new * n_in, axis=1, keepdims=True))
                x1 = x1 + _lane_put(row_intra - col_intra + row_inter, cfl)
                x2 = x2 + _lane_put(col_inter, cfl)
                e_row = e_row + _lane_put(e_old, cfl)
                dig = dig + _lane_put(col_intra + col_inter, ci)
                dc_sc[dr, h] = a_old * dc_new + _dot_tn(_bf(qf32 * w_int), dnum_b)
                dn_sc[dr, h] = jnp.broadcast_to(a_old * dn_new + jnp.sum(qf32 * (w_int * dden), axis=0, keepdims=True), (8, dh))
            dlogf = _mask_dot_t(mfl, x1) + _mask_dot(mfl, x2) - x2 + e_row
            dg_ref[...] = dig + dlogf / (1.0 + jnp.exp(gz))

    def tok(cfn, col):
        return pl.BlockSpec((ln, md), lambda i: (cfn(i), col))

    def dht(cfn):
        return pl.BlockSpec((ln, md), lambda i: (jnp.minimum(cfn(i), nx - 1), 0))

    def gat(cfn):
        return pl.BlockSpec((ln, LANES), lambda i: (cfn(i), 0))

    def st(cfn, shape):
        return pl.BlockSpec((None,) + shape, lambda i: (cfn(i),) + (0,) * len(shape))

    st_shapes = ((nh * dh, dh), (8, md), (nh, 8, LANES))

    def side(cfn):
        return [tok(cfn, 0), tok(cfn, 1), tok(cfn, 2), gat(cfn), dht(cfn)] + [st(cfn, s) for s in st_shapes]

    def outs(cfn):
        return [pl.BlockSpec((ln, 3 * md), lambda i: (cfn(i), 0)), gat(cfn)]

    return pl.pallas_call(
        body, name="mlstm_bwd", grid=(nc,),
        in_specs=side(chunk_f) + side(chunk_b) + [pl.BlockSpec((1, LANES), lambda i: (0, 0))],
        out_specs=outs(chunk_f) + outs(chunk_b),
        out_shape=[SDS((s_rows, 3 * md), BF16), SDS((s_rows, LANES), F32)] * 2,
        scratch_shapes=[pltpu.VMEM((2, nh, dh, dh), F32), pltpu.VMEM((2, nh, 8, dh), F32)],
        compiler_params=_cp("arbitrary"))(qk, qk, z_main, zg, dhs, *states_f, qk, qk, z_main, zg, dhs, *states_b, bias)


def _qkv_conv_bwd(dqkv_f, dqkv_b, z_main, conv_w, t_rows, md, qscale):
    s_rows = z_main.shape[0]
    tb = _pick(s_rows, (1280, 1024, 256))
    cb = _pick(md, (512, 256, 128))
    ni, nj, ncq = s_rows // tb, 3 * md // cb, 2 * md // cb
    nb8 = tb // 8
    n_ext = tb + 16

    def body(fm, fp, fn, bm, bp, bn, zm, zp, zn, w_ref, dz_ref, gw_ref):
        j, i = pl.program_id(0), pl.program_id(1)

        @pl.when(j < ncq)
        def _():
            z = jnp.concatenate([zp[...], zm[...], zn[...]], axis=0).astype(F32)
            dqk = (jnp.concatenate([fp[...], fm[...], fn[...]], axis=0).astype(F32)
                   + jnp.concatenate([bp[...], bm[...], bn[...]], axis=0).astype(F32)) * jnp.where(j * cb < md, qscale, 1.0)
            row = i * tb - 8 + lax.broadcasted_iota(jnp.int32, (n_ext, 1), 0)
            prev_ok, next_ok = _seg_masks(row, t_rows, s_rows)
            zprev = jnp.where(prev_ok, pltpu.roll(z, 1, 0), 0.0)
            znext = jnp.where(next_ok, pltpu.roll(z, n_ext - 1, 0), 0.0)
            pre = w_ref[0:1, :] * zprev + w_ref[1:2, :] * z + w_ref[2:3, :] * znext
            sg = _sigmoid(pre)
            dpre = dqk * (sg * (1.0 + pre * (1.0 - sg)))
            dz = (w_ref[1:2, :] * dpre + w_ref[0:1, :] * jnp.where(next_ok, pltpu.roll(dpre, n_ext - 1, 0), 0.0)
                  + w_ref[2:3, :] * jnp.where(prev_ok, pltpu.roll(dpre, 1, 0), 0.0))
            dz_ref[...] = _bf(dz[8:8 + tb])
            dm = dpre[8:8 + tb]

            @pl.when(i == 0)
            def _():
                gw_ref[...] = jnp.zeros_like(gw_ref)

            gw_ref[...] += jnp.concatenate(
                [jnp.sum(dm * zprev[8:8 + tb], axis=0, keepdims=True), jnp.sum(dm * z[8:8 + tb], axis=0, keepdims=True),
                 jnp.sum(dm * znext[8:8 + tb], axis=0, keepdims=True), jnp.zeros((5, cb), F32)], axis=0)

        @pl.when(j >= ncq)
        def _():
            dz_ref[...] = _bf(fm[...].astype(F32) + bm[...].astype(F32))

    def halo(clampj):
        def cj(j):
            return jnp.minimum(j, ncq - 1) if clampj else j
        return [pl.BlockSpec((tb, cb), lambda j, i: (i, cj(j))),
                pl.BlockSpec((8, cb), lambda j, i: (jnp.maximum(i * nb8 - 1, 0), cj(j))),
                pl.BlockSpec((8, cb), lambda j, i: (jnp.minimum((i + 1) * nb8, s_rows // 8 - 1), cj(j)))]

    return pl.pallas_call(
        body, name="qkv_conv_bwd", grid=(nj, ni),
        in_specs=halo(False) + halo(False) + halo(True) + [pl.BlockSpec((8, cb), lambda j, i: (0, jnp.minimum(j, ncq - 1)))],
        out_specs=[pl.BlockSpec((tb, cb), lambda j, i: (i, j)), pl.BlockSpec((8, cb), lambda j, i: (0, jnp.minimum(j, ncq - 1)))],
        out_shape=[SDS((s_rows, 3 * md), BF16), SDS((8, 2 * md), F32)],
        compiler_params=_cp("arbitrary", "arbitrary"))(dqkv_f, dqkv_f, dqkv_f, dqkv_b, dqkv_b, dqkv_b, z_main, z_main, z_main, conv_w)


def _gate_grad_sum(dg_f, dg_b):
    s_rows = dg_f.shape[0]
    tb = _pick(s_rows, (1280, 1024, 256))

    def body(a_ref, b_ref, o_ref, st_ref):
        @pl.when(pl.program_id(0) == 0)
        def _():
            st_ref[...] = jnp.zeros_like(st_ref)

        s = a_ref[...] + b_ref[...]
        o_ref[...] = _bf(s)
        st_ref[...] += jnp.concatenate([jnp.sum(s, axis=0, keepdims=True), jnp.zeros((7, LANES), F32)], axis=0)

    blk = pl.BlockSpec((tb, LANES), lambda i: (i, 0))
    return pl.pallas_call(
        body, name="gate_grad_sum", grid=(s_rows // tb,), in_specs=[blk, blk],
        out_specs=[blk, pl.BlockSpec((8, LANES), lambda i: (0, 0))],
        out_shape=[SDS((s_rows, LANES), BF16), SDS((8, LANES), F32)], compiler_params=_cp("arbitrary"))(dg_f, dg_b)


def _mod_grads(silu_slots, dmx_sh, dmx_slots, dmc_tot, dmc_sh, silu_cctx, c_ctx, w_mod_c):
    d = silu_slots.shape[1]
    ncol, n6 = dmx_sh.shape[1], dmx_slots.shape[1]

    def body(ss_ref, dsh_ref, dsl_ref, dct_ref, dcs_ref, sc_ref, c_ref, w_ref, gw_ref, gb_ref, gc_ref):
        a = jnp.concatenate([ss_ref[...], sc_ref[...], jnp.zeros((7, d), F32)], axis=0)
        b = jnp.concatenate([dsh_ref[...], dcs_ref[...], jnp.zeros((7, ncol), F32)], axis=0)
        gw_ref[...] = lax.dot_general(a, b, (((0,), (0,)), ((), ())), preferred_element_type=F32, precision=HI)
        dct = dct_ref[...]
        gb_ref[...] = jnp.sum(dsl_ref[...], axis=0, keepdims=True) + jnp.concatenate(
            [dct, jnp.zeros((1, n6 - dct.shape[1]), F32)], axis=1)
        t = _dot_nt(_bf(jnp.broadcast_to(dct, (8, dct.shape[1]))), w_ref[...])
        cv = c_ref[...]
        s = _sigmoid(cv)
        gc_ref[...] = t[0:1, :] * (s * (1.0 + cv * (1.0 - s)))

    return pl.pallas_call(body, name="mod_grads", out_shape=[SDS((d, ncol), F32), SDS((1, n6), F32), SDS((1, d), F32)],
                          compiler_params=_cp())(silu_slots, dmx_sh, dmx_slots, dmc_tot, dmc_sh, silu_cctx, c_ctx, w_mod_c)


def _slot_sum(slots):
    ns, r = slots.shape[0], slots.shape[1]
    tb = _pick(r, (1024, 512, 256, 128, 64, 32, 16, 8))

    def body(s_ref, o_ref):
        acc = s_ref[0]
        for k in range(1, ns):
            acc = acc + s_ref[k]
        o_ref[...] = acc

    return pl.pallas_call(
        body, name="slot_sum", grid=(r // tb,), in_specs=[pl.BlockSpec((ns, tb, LANES), lambda i: (0, i, 0))],
        out_specs=pl.BlockSpec((tb, LANES), lambda i: (i, 0)), out_shape=SDS((r, LANES), F32),
        compiler_params=_cp("arbitrary"))(slots)


def _adamw(w, gslots, m, v, name):
    r, cdim = w.shape
    ns, rg = gslots.shape[0], gslots.shape[1]
    tb = r if (rg != r or r % 8) else _pick(r, (128, 64, 32, 16, 8))
    bc1, bc2 = 1.0 - ADAM_B1 ** ADAM_STEP, 1.0 - ADAM_B2 ** ADAM_STEP

    def body(w_ref, g_ref, m_ref, v_ref, go_ref, d_ref, mo_ref, vo_ref):
        g = g_ref[0, 0:tb, :].astype(F32)
        for k in range(1, ns):
            g = g + g_ref[k, 0:tb, :].astype(F32)
        mn = ADAM_B1 * m_ref[...] + (1.0 - ADAM_B1) * g
        vn = ADAM_B2 * v_ref[...] + (1.0 - ADAM_B2) * (g * g)
        go_ref[...] = g
        mo_ref[...] = mn
        vo_ref[...] = vn
        d_ref[...] = -ADAM_LR * ((mn / bc1) / (jnp.sqrt(vn / bc2) + ADAM_EPS) + ADAM_WD * w_ref[...])

    blk = pl.BlockSpec((tb, cdim), lambda i: (i, 0))
    gblk = pl.BlockSpec((ns, tb if rg == r else rg, cdim), lambda i: (0, i, 0))
    return pl.pallas_call(
        body, name=name, grid=(r // tb,), in_specs=[blk, gblk, blk, blk],
        out_specs=[blk] * 4, out_shape=[SDS((r, cdim), F32)] * 4, compiler_params=_cp("arbitrary"))(w, gslots, m, v)


def _pack(parts, row_mult):
    flat = jnp.concatenate([p.reshape(-1) for p in parts])
    n = flat.shape[0]
    rows = -(-n // LANES)
    rows = -(-rows // row_mult) * row_mult
    return jnp.pad(flat, (0, rows * LANES - n)).reshape(rows, LANES)


def _unpack(buf, shapes):
    flat = buf.reshape(-1)
    out, off = [], 0
    for s in shapes:
        n = math.prod(s)
        out.append(flat[off:off + n].reshape(s))
        off += n
    return out


def _pad_cols(a, width):
    return jnp.pad(a, ((0, 0), (0, width - a.shape[1])))


def _pad_lanes(a):
    return _pad_cols(a, LANES)


def _up128(n):
    return -(-n // LANES) * LANES


def kernel(x, c, ctx, c_ctx, w_mod, b_mod, norm1_g, w_in, b_gate, conv_qk, head_norm_g, sgu_ln_g, sgu_ln_b, w_s, b_s, w_branch_mlstm, w_branch_sgu, w_out, norm2_g, w_up, w_ffn_conv, w_down, final_g, loss_target, m_c_ctx, m_w_mod, m_b_mod, m_norm1_g, m_w_in, m_b_gate, m_conv_qk, m_head_norm_g, m_sgu_ln_g, m_sgu_ln_b, m_w_s, m_b_s, m_w_branch_mlstm, m_w_branch_sgu, m_w_out, m_norm2_g, m_w_up, m_w_ffn_conv, m_w_down, m_final_g, v_c_ctx, v_w_mod, v_b_mod, v_norm1_g, v_w_in, v_b_gate, v_conv_qk, v_head_norm_g, v_sgu_ln_g, v_sgu_ln_b, v_w_s, v_b_s, v_w_branch_mlstm, v_w_branch_sgu, v_w_out, v_norm2_g, v_w_up, v_w_ffn_conv, v_w_down, v_final_g):
    t, d = x.shape[1], x.shape[2]
    n_ctx = ctx.shape[1]
    s_rows = t + n_ctx
    nh = b_gate.shape[1] // 4
    md = head_norm_g.shape[1]
    dh = md // nh
    ng, sc = w_s.shape[1], w_s.shape[2]
    dff = w_down.shape[1] * N_DEV
    n_in = w_in.shape[2] * N_DEV
    assert md == d and sgu_ln_g.shape[1] == d and n_ctx == LCH and t % LCH == 0 and t % (8 * GRID_W) == 0
    assert n_in == 8 * d + 4 * nh and 4 * nh <= LANES
    me = 4 * lax.axis_index("x") + 2 * lax.axis_index("y") + lax.axis_index("c")

    n_mod, n_insh, n_upsh = w_mod.shape[2], w_in.shape[2], w_up.shape[2]
    p_mod, p_in, p_up = _up128(n_mod), _up128(n_insh), _up128(n_upsh)
    nq, nf = conv_qk.shape[2], w_ffn_conv.shape[3]
    ffn9 = w_ffn_conv[0].reshape(9, nf)
    colpack = jnp.concatenate([_pad_cols(_bf(w_mod[0]), p_mod), _pad_cols(_bf(w_in[0]), p_in)], axis=1)
    convpack = jnp.concatenate([jnp.pad(conv_qk[0], ((0, 13), (0, 0))), jnp.pad(ffn9, ((0, 7), (0, 0)))], axis=1)
    g_col, g_conv = _allgather([colpack, convpack])
    w_mod_f, w_main, w_gate = _assemble_cols(
        g_col, [(0, n_mod, [(0, 0, N_DEV * n_mod, 0)]),
                (p_mod, n_insh, [(1, 0, 3 * md, 0), (2, 3 * md, 4 * nh, 0), (1, 3 * md + 4 * nh, 5 * d, 3 * md)])],
        [N_MOD * d, 8 * d, LANES], "assemble_weights")
    convw, wconv9 = _assemble_cols(g_conv, [(0, nq, [(0, 0, N_DEV * nq, 0)]), (nq, nf, [(1, 0, N_DEV * nf, 0)])],
                                   [N_DEV * nq, N_DEV * nf], "assemble_conv_weights")
    zero = jnp.minimum(jnp.abs(g_conv[0, 0, 0]), 0.0)
    late_w = [_pad_cols(_bf(w_up[0] + zero), p_up), _bf(w_branch_mlstm[0]), _bf(w_branch_sgu[0]), _bf(w_out[0]), _bf(w_down[0])]
    late_state, late_tok = _exchange_start(late_w, False, "late_weights_start")

    cvec = jnp.concatenate([c, c_ctx[None], jnp.zeros((6, d), F32)], axis=0) + late_tok[0:1, 0:1]
    silu_v, mod = _modulation(cvec, w_mod_f, b_mod)
    mx = [mod[0:1, k * d:(k + 1) * d] for k in range(N_MOD)]
    mc = [mod[1:2, k * d:(k + 1) * d] for k in range(2)]
    xs = jnp.concatenate([x[0], ctx[0]], axis=0)
    hn, z_main, zg = _norm_mod_proj(xs, norm1_g, jnp.concatenate([mx[0], mx[1], mc[0], mc[1]], axis=0), w_main, w_gate, t, "in_proj")
    qscale = dh ** -0.5
    qk = _qk_conv(z_main, convw, t, md, qscale)
    bias = _pad_lanes(b_gate)
    fwd = _mlstm_fwd(qk, z_main, zg, bias, nh)
    hf, hb, states_f, states_b = fwd[0], fwd[1], fwd[2:5], fwd[5:8]
    g_up, g_bm, g_bs, g_out, g_down = _exchange_wait(late_state, fwd[4], "late_weights_wait")
    (w_up_f,) = _assemble_cols(g_up, [(0, n_upsh, [(0, 0, 2 * dff, 0)])], [2 * dff], "assemble_w_up")
    wbm_f, wbs_f, wout_f = (g.reshape(d, d) for g in (g_bm, g_bs, g_out))
    w_down_f = g_down.reshape(dff, d)
    b_st = _pad_lanes(b_s[0].T)
    h1, ym, ys, pm, ps, y, out = _mixer_fwd(hf, hb, z_main, xs, head_norm_g, sgu_ln_g, sgu_ln_b, w_s[0], b_st, wbm_f, wbs_f,
                                            wout_f, mx[2], t, nh)
    hn2, ab = _norm_mod_proj(h1, norm2_g, jnp.concatenate([mx[3], mx[4], mx[3], mx[4]], axis=0), w_up_f, None, t, "up_proj")
    aconv, f, dh2, dffn, st_tail = _ffn_tail(ab, wconv9, w_down_f, h1, mx[5], final_g[None], loss_target[0], dff)

    db, dac = _ffn_bwd_gate(dffn, w_down_f, aconv, ab, dff)
    da, g_wconv9 = _ffn_conv_bwd(dac, ab, wconv9, dff)
    g_wdown = _wgrad(f, dffn, t, "wgrad_down")
    gwup_slots = _scatter_cols([_wgrad(hn2, da, t, "wgrad_up_a"), _wgrad(hn2, db, t, "wgrad_up_b")],
                               [(0, 0, dff, 0), (1, dff, dff, 0)], n_upsh, "scatter_grad_w_up")
    tkf = _pick(dff, (1408, 704, 384, 128))
    dh1, st_n2 = _proj_norm_bwd([(da, 0, w_up_f, 0, dff, tkf), (db, 0, w_up_f, dff, dff, tkf)], h1, 0, norm2_g, mx[4], dh2, t,
                                "up_proj_bwd")
    dz_rest, dhs, dout, dpm, dps, st_mix, g_ws, g_bst = _mixer_bwd(dh1, out, hf, hb, z_main, pm, ps, head_norm_g, sgu_ln_g,
                                                                    sgu_ln_b, w_s[0], b_st, wbm_f, wbs_f, wout_f, mx[2], t, nh)
    g_wout = _wgrad(y, dout, t, "wgrad_out")
    g_wbm = _wgrad(ym, dpm, t, "wgrad_branch_mlstm")
    g_wbs = _wgrad(ys, dps, t, "wgrad_branch_sgu")
    ex_a = [gwup_slots, g_wdown.reshape(N_DEV, dff // N_DEV, d), g_wbm.reshape(N_DEV, d // N_DEV, d),
            g_wbs.reshape(N_DEV, d // N_DEV, d), g_wout.reshape(N_DEV, d // N_DEV, d)]
    ex_a_state, ex_a_tok = _exchange_start(ex_a, True, "grad_exchange_a_start")
    dqkv_f, dg_f, dqkv_b, dg_b = _mlstm_bwd(qk, z_main, zg, bias + ex_a_tok[0:1, :], dhs, states_f, states_b, nh, t)
    dz_qkv, g_convqk = _qkv_conv_bwd(dqkv_f, dqkv_b, z_main, convw, t, md, qscale)
    dz_g, st_gate = _gate_grad_sum(dg_f, dg_b)
    gwin_slots = _scatter_cols(
        [_wgrad(hn, dz_qkv, s_rows, "wgrad_in_qkv"), _wgrad(hn, dz_g, s_rows, "wgrad_in_gate"), _wgrad(hn, dz_rest, t, "wgrad_in_rest")],
        [(0, 0, 3 * md, 0), (1, 3 * md, 4 * nh, 0), (2, 3 * md + 4 * nh, 5 * d, 0)], n_insh, "scatter_grad_w_in")
    gcq_slots = _scatter_cols([g_convqk], [(0, 0, 2 * md, 0)], nq, "scatter_grad_conv_qk")
    gcf_slots = _scatter_cols([g_wconv9], [(0, 0, dff, 0)], nf, "scatter_grad_ffn_conv")
    ex_b_state, ex_b_tok = _exchange_start([gwin_slots, gcq_slots, gcf_slots], True, "grad_exchange_b_start")
    tk = _pick(md, (1024, 512, 256))
    grad_x, st_n1x = _proj_norm_bwd(
        [(dz_qkv, 0, w_main, 0, 3 * md, tk), (dz_rest, 0, w_main, 3 * md, 5 * d, tk), (dz_g, 0, w_gate, 0, LANES, LANES)],
        xs, 0, norm1_g, mx[1] + ex_b_tok[0:1, 0:1], dh1, t, "in_proj_bwd")
    (st_n1c,) = _proj_norm_bwd([(dz_qkv, t, w_main, 0, 3 * md, tk), (dz_g, t, w_gate, 0, LANES, LANES)],
                               xs, t, norm1_g, mc[1] + ex_b_tok[0:1, 0:1], None, n_ctx, "in_proj_bwd_ctx")

    rx_a = _exchange_wait(ex_a_state, st_n1c, "grad_exchange_a_wait")
    rx_b = _exchange_wait(ex_b_state, st_n1c, "grad_exchange_b_wait")
    recv = [rx_b[0], rx_a[0], rx_a[2], rx_a[3], rx_a[4], rx_a[1], rx_b[1], rx_b[2]]
    small_parts = [st_n1x[1], st_n1x[2], st_mix[0], st_n2[1], st_n2[2], st_tail[1],
                   st_n1c[1], st_n1c[2],
                   silu_v[0], st_n1x[0] + st_n1c[0], st_gate[0], st_mix[1], st_mix[2], st_mix[3],
                   g_ws.reshape(-1), g_bst[:, :ng].T.reshape(-1), st_n2[0], st_tail[0]]
    gsmall = _pack(small_parts, 8)
    (recv_small,) = _grad_exchange([], [gsmall])
    small_sum = _slot_sum(recv_small).reshape(-1)
    small_slots = recv_small.reshape(N_DEV, -1)
    o_silu, o_n1 = 8 * d, 9 * d
    ncol = N_MOD * d // N_DEV
    dmc_tot = small_sum[6 * d:8 * d][None]
    dmc_pad = jnp.concatenate([dmc_tot, jnp.zeros((1, 4 * d), F32)], axis=1)
    g_wmod, g_bmod, g_cctx = _mod_grads(
        small_slots[:, o_silu:o_silu + d], lax.dynamic_slice_in_dim(small_slots[:, :6 * d], me * ncol, ncol, axis=1),
        small_slots[:, :6 * d], dmc_tot, lax.dynamic_slice_in_dim(dmc_pad, me * ncol, ncol, axis=1), silu_v[1:2], c_ctx[None],
        w_mod_f[:, :2 * d])

    shard_w = (w_in, w_up, w_branch_mlstm, w_branch_sgu, w_out, w_down, conv_qk)
    shard_m = (m_w_in, m_w_up, m_w_branch_mlstm, m_w_branch_sgu, m_w_out, m_w_down, m_conv_qk)
    shard_v = (v_w_in, v_w_up, v_w_branch_mlstm, v_w_branch_sgu, v_w_out, v_w_down, v_conv_qk)
    shard_names = ("w_in", "w_up", "w_branch_mlstm", "w_branch_sgu", "w_out", "w_down", "conv_qk")
    shard_out = [[b[None] for b in _adamw(wa[0], recv[k], ma[0], va[0], "adamw_" + nm)]
                 for k, (wa, ma, va, nm) in enumerate(zip(shard_w, shard_m, shard_v, shard_names))]
    shard_out.append([b.reshape(w_ffn_conv.shape) for b in
                      _adamw(ffn9, recv[7], m_w_ffn_conv[0].reshape(9, nf), v_w_ffn_conv[0].reshape(9, nf), "adamw_w_ffn_conv")])
    mod_out = [b[None] for b in _adamw(w_mod[0], g_wmod[None], m_w_mod[0], v_w_mod[0], "adamw_w_mod")]

    def rep(cc, bm, n1, bg, hg, lg, lb, ws, bs, n2, fg):
        return [cc.reshape(-1), bm.reshape(-1), n1.reshape(-1), _pad_lanes(bg.reshape(1, -1)).reshape(-1), hg.reshape(-1),
                lg.reshape(-1), lb.reshape(-1), ws.reshape(-1), bs.reshape(-1), n2.reshape(-1), fg.reshape(-1)]

    o = o_n1
    g_rep_parts = [g_cctx, g_bmod]
    for n in (d, LANES, d, d, d, ng * sc * sc, ng * sc, d, d):
        g_rep_parts.append(small_sum[o:o + n])
        o += n
    rep_shapes = [(d,), (1, N_MOD * d), (1, d), (1, LANES), (1, d), (1, d), (1, d), (1, ng, sc, sc), (1, ng, sc), (1, d), (d,)]
    rep_out = _adamw(
        _pack(rep(c_ctx, b_mod, norm1_g, b_gate, head_norm_g, sgu_ln_g, sgu_ln_b, w_s, b_s, norm2_g, final_g), 8),
        _pack(g_rep_parts, 8)[None],
        _pack(rep(m_c_ctx, m_b_mod, m_norm1_g, m_b_gate, m_head_norm_g, m_sgu_ln_g, m_sgu_ln_b, m_w_s, m_b_s, m_norm2_g, m_final_g), 8),
        _pack(rep(v_c_ctx, v_b_mod, v_norm1_g, v_b_gate, v_head_norm_g, v_sgu_ln_g, v_sgu_ln_b, v_w_s, v_b_s, v_norm2_g, v_final_g), 8),
        "adamw_replicated")

    def assemble(k):
        r = _unpack(rep_out[k], rep_shapes)
        s = [o[k] for o in shard_out]
        return [r[0], mod_out[k], r[1], r[2], s[0], r[3][:, :4 * nh], s[6], r[4], r[5], r[6], r[7], r[8], s[2], s[3], s[4], r[9],
                s[1], s[7], s[5], r[10]]

    loss = lax.psum(st_tail[2, 0], ("x", "y", "c"))
    outs = [loss, grad_x[None]]
    for k in range(4):
        outs += assemble(k)
    return tuple(outs)
```

```python
import functools
import math

import jax
import jax.numpy as jnp
from jax import lax
from jax.experimental import pallas as pl
from jax.experimental.pallas import tpu as pltpu

F32, BF16 = jnp.float32, jnp.bfloat16
EPS = 1e-6
M_INIT = -1e30
NEG = -1e30
GRID_W = 64
LCH = 256
N_MOD = 6
N_DEV = 8
LANES = 128
ADAM_LR, ADAM_B1, ADAM_B2, ADAM_EPS, ADAM_WD, ADAM_STEP = 0.001, 0.9, 0.999, 1e-08, 0.01, 10
GELU_C = math.sqrt(2.0 / math.pi)
GELU_A = 0.044715
VMEM_LIMIT = 56 * 1024 * 1024
HI = lax.Precision.HIGHEST
SDS = jax.ShapeDtypeStruct
MESH_ID = pl.DeviceIdType.MESH


def _pick(n, cands):
    for c in cands:
        if n % c == 0:
            return c
    raise ValueError(f"no block size for {n} in {cands}")


def _cp(*sem):
    return pltpu.CompilerParams(dimension_semantics=sem if sem else None, vmem_limit_bytes=VMEM_LIMIT)


def _sigmoid(x):
    return 0.5 * jnp.tanh(0.5 * x) + 0.5


def _split3(x):
    hi = x.astype(BF16)
    r = x - hi.astype(F32)
    mid = r.astype(BF16)
    return hi, mid, (r - mid.astype(F32)).astype(BF16)


def _mask_dot(mask_b, x):
    hi, mid, lo = _split3(x)
    return (_dot(mask_b, lo) + _dot(mask_b, mid)) + _dot(mask_b, hi)


def _mask_dot_t(mask_b, x):
    hi, mid, lo = _split3(x)
    return (_dot_tn(mask_b, lo) + _dot_tn(mask_b, mid)) + _dot_tn(mask_b, hi)


def _gelu(x):
    return x * (0.5 * (1.0 + jnp.tanh(GELU_C * x * (1.0 + GELU_A * (x * x)))))


def _gelu_and_grad(x):
    x2 = x * x
    t = jnp.tanh(GELU_C * x * (1.0 + GELU_A * x2))
    half = 0.5 * (1.0 + t)
    return x * half, half + (0.5 * GELU_C) * x * (1.0 - t * t) * (1.0 + 3.0 * GELU_A * x2)


def _log_sigmoid(x):
    return jnp.minimum(x, 0.0) - jnp.log(1.0 + jnp.exp(-jnp.abs(x)))


def _dot(a, b):
    return jnp.dot(a, b, preferred_element_type=F32)


def _dot_nt(a, b):
    return lax.dot_general(a, b, (((1,), (1,)), ((), ())), preferred_element_type=F32)


def _dot_tn(a, b):
    return lax.dot_general(a, b, (((0,), (0,)), ((), ())), preferred_element_type=F32)


def _bf(x):
    return x.astype(BF16)


def _allgather(arrs):
    na = len(arrs)

    def body(*refs):
        x_refs, o_refs = refs[:na], refs[na:2 * na]
        send_sems, recv_sems, local_sems = refs[2 * na:]
        x, y, c = lax.axis_index("x"), lax.axis_index("y"), lax.axis_index("c")
        me, sibling = (x, y, c), (x, y, 1 - c)
        chips = [(1 - x, y), (x, 1 - y), (1 - x, 1 - y)]

        def copy(a, k, block, to, src=None):
            slot = o_refs[a].at[4 * block[0] + 2 * block[1] + block[2]]
            return pltpu.make_async_remote_copy(
                src_ref=slot if src is None else src, dst_ref=slot, send_sem=send_sems.at[7 * a + k],
                recv_sem=recv_sems.at[7 * a + k], device_id=to, device_id_type=MESH_ID)

        mine = [pltpu.make_async_copy(x_refs[a], o_refs[a].at[4 * x + 2 * y + c], local_sems.at[a]) for a in range(na)]
        for cp in mine:
            cp.start()
        first = []
        for a in range(na):
            first.append(copy(a, 0, me, sibling, src=x_refs[a]))
            first += [copy(a, 1 + j, me, (*chip, c), src=x_refs[a]) for j, chip in enumerate(chips)]
        for cp in first:
            cp.start()
        passed = []
        for j, chip in enumerate(chips):
            for a in range(na):
                copy(a, 1 + j, (*chip, c), me).wait_recv()
                passed.append(copy(a, 4 + j, (*chip, c), sibling))
                passed[-1].start()
        for a in range(na):
            copy(a, 0, sibling, me).wait_recv()
            for j, chip in enumerate(chips):
                copy(a, 4 + j, (*chip, 1 - c), me).wait_recv()
        for cp in first + passed:
            cp.wait_send()
        for cp in mine:
            cp.wait()

    anyspec = pl.BlockSpec(memory_space=pl.ANY)
    return pl.pallas_call(
        body, name="weights_allgather",
        out_shape=[SDS((N_DEV,) + a.shape, a.dtype) for a in arrs],
        in_specs=[anyspec] * na, out_specs=[anyspec] * na,
        scratch_shapes=[pltpu.SemaphoreType.DMA((7 * na,)), pltpu.SemaphoreType.DMA((7 * na,)), pltpu.SemaphoreType.DMA((na,))],
    )(*arrs)


def _grad_exchange(per_dest, shared):
    nd, ns = len(per_dest), len(shared)
    na = nd + ns

    def body(*refs):
        in_refs, out_refs = refs[:na], refs[na:2 * na]
        send_sems, recv_sems, local_sems = refs[2 * na:]
        x, y, c = lax.axis_index("x"), lax.axis_index("y"), lax.axis_index("c")
        me = 4 * x + 2 * y + c

        def src(a, idx):
            return in_refs[a].at[idx] if a < nd else in_refs[a]

        loc = [pltpu.make_async_copy(src(a, me), out_refs[a].at[me], local_sems.at[a]) for a in range(na)]
        for cp in loc:
            cp.start()
        sends, recvs = [], []
        for k in range(1, N_DEV):
            px = 1 - x if k & 4 else x
            py = 1 - y if k & 2 else y
            pc = 1 - c if k & 1 else c
            peer, pidx = (px, py, pc), 4 * px + 2 * py + pc
            for a in range(na):
                sem = 7 * a + k - 1
                sends.append(pltpu.make_async_remote_copy(
                    src_ref=src(a, pidx), dst_ref=out_refs[a].at[me], send_sem=send_sems.at[sem],
                    recv_sem=recv_sems.at[sem], device_id=peer, device_id_type=MESH_ID))
                recvs.append(pltpu.make_async_remote_copy(
                    src_ref=src(a, pidx), dst_ref=out_refs[a].at[pidx], send_sem=send_sems.at[sem],
                    recv_sem=recv_sems.at[sem], device_id=peer, device_id_type=MESH_ID))
        for cp in sends:
            cp.start()
        for cp in recvs:
            cp.wait_recv()
        for cp in sends:
            cp.wait_send()
        for cp in loc:
            cp.wait()

    anyspec = pl.BlockSpec(memory_space=pl.ANY)
    return pl.pallas_call(
        body, name="grad_exchange",
        out_shape=[SDS(a.shape, a.dtype) for a in per_dest] + [SDS((N_DEV,) + a.shape, a.dtype) for a in shared],
        in_specs=[anyspec] * na, out_specs=[anyspec] * na,
        scratch_shapes=[pltpu.SemaphoreType.DMA((7 * na,)), pltpu.SemaphoreType.DMA((7 * na,)), pltpu.SemaphoreType.DMA((na,))],
    )(*per_dest, *shared)


_HBM_SPEC = pl.BlockSpec(memory_space=pltpu.HBM)
_SEM_SPEC = pl.BlockSpec(memory_space=pltpu.SEMAPHORE)
_EFFECT = pltpu.SideEffectType.DATAFLOW_SIDE_EFFECTING


def _peer_list(x, y, c):
    out = []
    for k in range(1, N_DEV):
        px = 1 - x if k & 4 else x
        py = 1 - y if k & 2 else y
        pc = 1 - c if k & 1 else c
        out.append(((px, py, pc), 4 * px + 2 * py + pc))
    return out


def _split_copies(src, land, send_sems, recv_sems, per_dest, receive):
    x, y, c = lax.axis_index("x"), lax.axis_index("y"), lax.axis_index("c")
    me = 4 * x + 2 * y + c
    out = []
    for k, (peer, pidx) in enumerate(_peer_list(x, y, c)):
        for a in range(len(src)):
            out.append(pltpu.make_async_remote_copy(
                src_ref=src[a].at[pidx] if per_dest else src[a], dst_ref=land[a].at[pidx if receive else me],
                send_sem=send_sems.at[7 * a + k], recv_sem=recv_sems.at[7 * a + k], device_id=peer, device_id_type=MESH_ID))
    return out


def _own_copies(src, land, own_sems, per_dest):
    me = 4 * lax.axis_index("x") + 2 * lax.axis_index("y") + lax.axis_index("c")
    return [pltpu.make_async_copy(src[a].at[me] if per_dest else src[a], land[a].at[me], own_sems.at[a]) for a in range(len(src))]


def _exchange_start(arrs, per_dest, name):
    na = len(arrs)
    land_shapes = [a.shape if per_dest else (N_DEV,) + a.shape for a in arrs]
    lands = [pltpu.with_memory_space_constraint(lax.empty(s, a.dtype), pltpu.HBM) for s, a in zip(land_shapes, arrs)]

    def body(*refs):
        src, land = refs[:na], refs[na:2 * na]
        send_sems, recv_sems, own_sems, token = refs[2 * na], refs[2 * na + 1], refs[2 * na + 2], refs[-1]
        for cp in _split_copies(src, land, send_sems, recv_sems, per_dest, False) + _own_copies(src, land, own_sems, per_dest):
            cp.start()
        token[...] = jnp.zeros_like(token)

    outs = pl.pallas_call(
        body, name=name,
        out_shape=[pltpu.SemaphoreType.DMA((7 * na,)), pltpu.SemaphoreType.DMA((7 * na,)), pltpu.SemaphoreType.DMA((na,))]
        + [pltpu.HBM(a.shape, a.dtype) for a in arrs] + [pltpu.HBM(s, a.dtype) for s, a in zip(land_shapes, arrs)]
        + [SDS((8, LANES), F32)],
        in_specs=[_HBM_SPEC] * (2 * na),
        out_specs=[_SEM_SPEC] * 3 + [_HBM_SPEC] * (2 * na) + [pl.BlockSpec(memory_space=pltpu.VMEM)],
        input_output_aliases={k: 3 + k for k in range(2 * na)},
        compiler_params=pltpu.CompilerParams(has_side_effects=_EFFECT),
    )(*[pltpu.with_memory_space_constraint(a, pltpu.HBM) for a in arrs], *lands)
    return (na, per_dest, outs[:-1]), outs[-1]


def _exchange_wait(state, after, name):
    na, per_dest, started = state

    def body(*refs):
        src, land = refs[:na], refs[na:2 * na]
        send_sems, recv_sems, own_sems = refs[2 * na], refs[2 * na + 1], refs[2 * na + 2]
        for cp in _split_copies(src, land, send_sems, recv_sems, per_dest, True):
            cp.wait_send()
            cp.wait_recv()
        for cp in _own_copies(src, land, own_sems, per_dest):
            cp.wait()

    bufs = started[3:]
    outs = pl.pallas_call(
        body, name=name,
        out_shape=[pltpu.HBM(b.shape, b.dtype) for b in bufs],
        in_specs=[_HBM_SPEC] * (2 * na) + [_SEM_SPEC] * 3 + [pl.BlockSpec(memory_space=pl.ANY)],
        out_specs=[_HBM_SPEC] * (2 * na),
        input_output_aliases={k: k for k in range(2 * na)},
        compiler_params=pltpu.CompilerParams(has_side_effects=_EFFECT),
    )(*bufs, started[0], started[1], started[2], after)
    return outs[na:]


def _col_pieces(n, segments):
    out = []
    for j in range(N_DEV):
        lo, hi = j * n, (j + 1) * n
        for (k, s0, w, c0) in segments:
            a, b = max(lo, s0), min(hi, s0 + w)
            if a < b:
                out.append((j, a - lo, b - lo, k, c0 + a - s0, c0 + b - s0))
    return out


def _assemble_cols(slots, groups, out_widths, name):
    r, p = slots.shape[1], slots.shape[2]
    tb = _pick(r, (128, 64, 32, 16, 8))
    covered = [0] * len(out_widths)
    for (_, n, segs) in groups:
        for (k, _, w, _) in segs:
            covered[k] += w

    def body(s_ref, *o_refs):
        for k, wd in enumerate(out_widths):
            if covered[k] < wd:
                o_refs[k][...] = jnp.zeros_like(o_refs[k])
        for (off, n, segs) in groups:
            for (j, a0, a1, k, d0, d1) in _col_pieces(n, segs):
                o_refs[k][:, d0:d1] = s_ref[j, :, off + a0:off + a1]

    return pl.pallas_call(
        body, name=name, grid=(r // tb,), in_specs=[pl.BlockSpec((N_DEV, tb, p), lambda i: (0, i, 0))],
        out_specs=[pl.BlockSpec((tb, w), lambda i: (i, 0)) for w in out_widths],
        out_shape=[SDS((r, w), slots.dtype) for w in out_widths], compiler_params=_cp("arbitrary"))(slots)


def _scatter_cols(pieces, segments, n, name):
    r = pieces[0].shape[0]
    tb = _pick(r, (128, 64, 32, 16, 8))

    def body(*refs):
        p_refs, o_ref = refs[:-1], refs[-1]
        for (j, a0, a1, k, d0, d1) in _col_pieces(n, segments):
            o_ref[j, :, a0:a1] = p_refs[k][:, d0:d1]

    return pl.pallas_call(
        body, name=name, grid=(r // tb,), in_specs=[pl.BlockSpec((tb, a.shape[1]), lambda i: (i, 0)) for a in pieces],
        out_specs=pl.BlockSpec((N_DEV, tb, n), lambda i: (0, i, 0)), out_shape=SDS((N_DEV, r, n), pieces[0].dtype),
        compiler_params=_cp("arbitrary"))(*pieces)


def _modulation(cvec, w_mod, b_mod):
    d, n = w_mod.shape

    def body(c_ref, w_ref, b_ref, s_ref, o_ref):
        cv = c_ref[...]
        s = cv * _sigmoid(cv)
        s_ref[...] = s
        o_ref[...] = _dot(_bf(s), w_ref[...]) + b_ref[...]

    return pl.pallas_call(body, name="modulation", out_shape=(SDS((8, d), F32), SDS((8, n), F32)),
                          compiler_params=_cp())(cvec, w_mod, b_mod)


def _norm_mod_proj(x_arr, g, shsc, w_main, w_gate, rows_total, row0, filled, name):
    m_rows, d = x_arr.shape
    n = w_main.shape[1]
    tb = _pick(m_rows, (1024, 256))
    cb = _pick(n, (1408, 1024, 768, 512, 384, 256, 128))
    gate = w_gate is not None
    nout = 3 if gate else 2
    nin = 5 if gate else 4
    rb = row0 // tb

    def body(*refs):
        x_ref, g_ref, ss_ref, wm_ref = refs[:4]
        wg_ref = refs[4] if gate else None
        outs = refs[len(refs) - 1 - nout:len(refs) - 1]
        hn_ref, z_ref = outs[0], outs[1]
        hn_sc = refs[-1]

        @pl.when(pl.program_id(1) == 0)
        def _():
            x = x_ref[...]
            r = lax.rsqrt(jnp.mean(x * x, axis=-1, keepdims=True) + EPS)
            hb = _bf((x * r * g_ref[...]) * (1.0 + ss_ref[1:2, :]) + ss_ref[0:1, :])
            hn_sc[...] = hb
            hn_ref[...] = hb
            if gate:
                outs[2][...] = _dot(hb, wg_ref[...])

        z_ref[...] = _bf(_dot(hn_sc[...], wm_ref[...]))

    in_specs = [pl.BlockSpec((tb, d), lambda i, j: (i, 0)), pl.BlockSpec((1, d), lambda i, j: (0, 0)),
                pl.BlockSpec((2, d), lambda i, j: (0, 0)), pl.BlockSpec((d, cb), lambda i, j: (0, j))]
    out_specs = [pl.BlockSpec((tb, d), lambda i, j: (rb + i, 0)), pl.BlockSpec((tb, cb), lambda i, j: (rb + i, j))]
    out_shape = [SDS((rows_total, d), BF16), SDS((rows_total, n), BF16)]
    args = [x_arr, g, shsc, w_main]
    if gate:
        in_specs.append(pl.BlockSpec((d, LANES), lambda i, j: (0, 0)))
        out_specs.append(pl.BlockSpec((tb, LANES), lambda i, j: (rb + i, 0)))
        out_shape.append(SDS((rows_total, LANES), F32))
        args.append(w_gate)
    aliases = {}
    if filled is not None:
        in_specs += [pl.BlockSpec(memory_space=pl.ANY)] * nout
        args += list(filled)
        aliases = {nin + k: k for k in range(nout)}
    return pl.pallas_call(
        body, name=name, grid=(m_rows // tb, n // cb), in_specs=in_specs, out_specs=out_specs, out_shape=out_shape,
        input_output_aliases=aliases, scratch_shapes=[pltpu.VMEM((tb, d), BF16)],
        compiler_params=_cp("arbitrary", "arbitrary"))(*args)


def _seg_masks(row, t_rows, s_rows):
    prev_ok = (row != 0) & (row != t_rows)
    next_ok = (row != t_rows - 1) & (row != s_rows - 1)
    return prev_ok, next_ok


def _shift_rows(z, halo_prev, halo_next, tb):
    loc = lax.broadcasted_iota(jnp.int32, (tb, 1), 0)
    zp = jnp.where(loc == 0, halo_prev, pltpu.roll(z, 1, 0))
    zn = jnp.where(loc == tb - 1, halo_next, pltpu.roll(z, tb - 1, 0))
    return zp, zn


def _qk_conv(z_main, conv_w, t_rows, md, qscale):
    s_rows = z_main.shape[0]
    tb = _pick(s_rows, (1280, 1024, 256))
    cb = _pick(md, (512, 256, 128))
    nb8 = tb // 8

    def body(zm, zp, zn, w_ref, o_ref):
        i, j = pl.program_id(0), pl.program_id(1)
        z = zm[...].astype(F32)
        zprev, znext = _shift_rows(z, zp[7:8, :].astype(F32), zn[0:1, :].astype(F32), tb)
        row = i * tb + lax.broadcasted_iota(jnp.int32, (tb, 1), 0)
        prev_ok, next_ok = _seg_masks(row, t_rows, s_rows)
        pre = (w_ref[0:1, :] * jnp.where(prev_ok, zprev, 0.0) + w_ref[1:2, :] * z
               + w_ref[2:3, :] * jnp.where(next_ok, znext, 0.0))
        scale = jnp.where(j * cb < md, qscale, 1.0)
        o_ref[...] = _bf(pre * _sigmoid(pre) * scale)

    return pl.pallas_call(
        body, name="qk_conv", grid=(s_rows // tb, 2 * md // cb),
        in_specs=[pl.BlockSpec((tb, cb), lambda i, j: (i, j)),
                  pl.BlockSpec((8, cb), lambda i, j: (jnp.maximum(i * nb8 - 1, 0), j)),
                  pl.BlockSpec((8, cb), lambda i, j: (jnp.minimum((i + 1) * nb8, s_rows // 8 - 1), j)),
                  pl.BlockSpec((8, cb), lambda i, j: (0, j))],
        out_specs=pl.BlockSpec((tb, cb), lambda i, j: (i, j)),
        out_shape=SDS((s_rows, 2 * md), BF16), compiler_params=_cp("arbitrary", "arbitrary"))(z_main, z_main, z_main, conv_w)


def _chunk_gates(gates, bias, rev):
    ln = gates.shape[0]
    gz = gates + bias
    logf = _log_sigmoid(gz)
    r_id = lax.broadcasted_iota(jnp.int32, (ln, ln), 0)
    c_id = lax.broadcasted_iota(jnp.int32, (ln, ln), 1)
    mask = (c_id >= r_id) if rev else (c_id <= r_id)
    mb = mask.astype(F32).astype(BF16)
    b_all = _mask_dot(mb, logf)
    g_all = jnp.sum(logf, axis=0, keepdims=True)
    return gz, b_all, b_all.T, gz.T, g_all, mask, mb


def _head_weights(b_col, b_row, i_row, m_in, mask):
    d = jnp.where(mask, b_col - b_row + i_row, NEG)
    inter = b_col + m_in
    m_row = jnp.maximum(inter, jnp.max(d, axis=1, keepdims=True))
    return jnp.exp(d - m_row), jnp.exp(inter - m_row), m_row


def _head_state_coeffs(g, b_col, i_col, m_in):
    a = g - b_col + i_col
    m_new = jnp.maximum(g + m_in, jnp.max(a, axis=0, keepdims=True))
    return jnp.exp(g + m_in - m_new), jnp.exp(a - m_new), m_new


def _mlstm_fwd(qk, z_main, zg, bias, nh):
    s_rows = qk.shape[0]
    md = qk.shape[1] // 2
    dh = md // nh
    nc = s_rows // LCH
    ln = LCH

    def chunk_f(i):
        return jnp.where(i == 0, nc - 1, i - 1)

    def chunk_b(i):
        return jnp.where(i == 0, nc - 1, nc - 1 - i)

    def body(qf, kf, vf, gf, qb, kb, vb, gb, bias_ref, hf_ref, hb_ref, cf_ref, nf_ref, mf_ref, cb_ref, nb_ref, mb_ref,
             c_sc, n_sc, m_sc):
        i = pl.program_id(0)

        @pl.when(i == 0)
        def _():
            c_sc[...] = jnp.zeros_like(c_sc)
            n_sc[...] = jnp.zeros_like(n_sc)
            m_sc[...] = jnp.full(m_sc.shape, M_INIT, F32)

        for dr, (q_ref, k_ref, v_ref, g_ref, h_ref, c_out, n_out, m_out) in enumerate(
                ((qf, kf, vf, gf, hf_ref, cf_ref, nf_ref, mf_ref), (qb, kb, vb, gb, hb_ref, cb_ref, nb_ref, mb_ref))):
            gz, b_all, b_t, g_t, g_all, mask, _ = _chunk_gates(g_ref[...], bias_ref[...], dr == 1)
            for h in range(nh):
                ci, cf = 2 * dr * nh + h, (2 * dr + 1) * nh + h
                sl = slice(h * dh, (h + 1) * dh)
                q, k, v = q_ref[:, sl], k_ref[:, sl], v_ref[:, sl]
                c_in, n_in, m_in = c_sc[dr, h], n_sc[dr, h, 0:1, :], m_sc[dr, h, 0:1, 0:1]
                c_out[sl, :] = c_in
                n_out[:, sl] = n_sc[dr, h]
                m_out[h] = m_sc[dr, h]
                b_col, b_row, i_col, i_row = b_all[:, cf:cf + 1], b_t[cf:cf + 1, :], gz[:, ci:ci + 1], g_t[ci:ci + 1, :]
                g = g_all[:, cf:cf + 1]
                w, w_int, m_row = _head_weights(b_col, b_row, i_row, m_in, mask)
                s_mat = _dot_nt(q, k) * w
                num = _dot(_bf(s_mat), v) + w_int * _dot(q, _bf(c_in))
                den = jnp.sum(s_mat, axis=1, keepdims=True) + w_int * jnp.sum(q.astype(F32) * n_in, axis=1, keepdims=True)
                h_ref[:, sl] = _bf(num / jnp.maximum(jnp.abs(den), jnp.exp(-m_row)))
                a_old, coef, m_new = _head_state_coeffs(g, b_col, i_col, m_in)
                kw = k.astype(F32) * coef
                c_sc[dr, h] = a_old * c_in + _dot_tn(_bf(kw), v)
                n_sc[dr, h] = jnp.broadcast_to(a_old * n_in + jnp.sum(kw, axis=0, keepdims=True), (8, dh))
                m_sc[dr, h] = jnp.broadcast_to(m_new, (8, LANES))

    def tok(cfn, col):
        return pl.BlockSpec((ln, md), lambda i: (cfn(i), col))

    def gat(cfn):
        return pl.BlockSpec((ln, LANES), lambda i: (cfn(i), 0))

    def st(cfn, shape):
        return pl.BlockSpec((None,) + shape, lambda i: (cfn(i),) + (0,) * len(shape))

    st_shapes = ((nh * dh, dh), (8, md), (nh, 8, LANES))
    return pl.pallas_call(
        body, name="mlstm_fwd", grid=(nc,),
        in_specs=[tok(chunk_f, 0), tok(chunk_f, 1), tok(chunk_f, 2), gat(chunk_f),
                  tok(chunk_b, 0), tok(chunk_b, 1), tok(chunk_b, 2), gat(chunk_b),
                  pl.BlockSpec((1, LANES), lambda i: (0, 0))],
        out_specs=[tok(chunk_f, 0), tok(chunk_b, 0)] + [st(chunk_f, s) for s in st_shapes] + [st(chunk_b, s) for s in st_shapes],
        out_shape=[SDS((s_rows, md), BF16)] * 2 + [SDS((nc,) + s, F32) for s in st_shapes] * 2,
        scratch_shapes=[pltpu.VMEM((2, nh, dh, dh), F32), pltpu.VMEM((2, nh, 8, dh), F32), pltpu.VMEM((2, nh, 8, LANES), F32)],
        compiler_params=_cp("arbitrary"))(qk, qk, z_main, zg, qk, qk, z_main, zg, bias)


def _head_rms(hs, nh, dh):
    parts, scales = [], []
    for h in range(nh):
        hh = hs[:, h * dh:(h + 1) * dh]
        r = lax.rsqrt(jnp.mean(hh * hh, axis=-1, keepdims=True) + EPS)
        parts.append(hh * r)
        scales.append(r)
    return jnp.concatenate(parts, axis=1), scales


def _layer_norm(v):
    vc = v - jnp.mean(v, axis=-1, keepdims=True)
    r = lax.rsqrt(jnp.mean(vc * vc, axis=-1, keepdims=True) + EPS)
    return vc * r, r


def _sgu_mix(vnb, ws_ref, bs_ref, tb, ng, gd, sc):
    rows = []
    for ch in range(tb // sc):
        cols = []
        for g in range(ng):
            blk = vnb[ch * sc:(ch + 1) * sc, g * gd:(g + 1) * gd]
            cols.append(_dot(_bf(ws_ref[g]), blk) + bs_ref[:, g:g + 1])
        rows.append(jnp.concatenate(cols, axis=1))
    return jnp.concatenate(rows, axis=0)


def _mixer_fwd(hf, hb, z_main, xs, hg, lng, lnb, w_s, b_st, wbm, wbs, wout, mx2, t_rows, nh):
    d = xs.shape[1]
    ng, sc = w_s.shape[0], w_s.shape[1]
    dh, gd = d // nh, d // ng
    tb = _pick(t_rows, (256,))

    def body(hf_ref, hb_ref, zo, zu, zv, zgm, zgg, x_ref, hg_ref, lng_ref, lnb_ref, ws_ref, bs_ref, wbm_ref, wbs_ref,
             wo_ref, mx2_ref, h1_ref, ym_ref, ys_ref, pm_ref, ps_ref, y_ref, out_ref):
        hs = hf_ref[...].astype(F32) + hb_ref[...].astype(F32)
        hn, _ = _head_rms(hs, nh, dh)
        ym = _bf(_sigmoid(zo[...].astype(F32)) * (hn * hg_ref[...]))
        ym_ref[...] = ym
        vhat, _ = _layer_norm(_gelu(zv[...].astype(F32)))
        vnb = _bf(vhat * lng_ref[...] + lnb_ref[...])
        ys = _bf(_gelu(zu[...].astype(F32)) * _sgu_mix(vnb, ws_ref, bs_ref, tb, ng, gd, sc))
        ys_ref[...] = ys
        pm = _dot(ym, wbm_ref[...])
        ps = _dot(ys, wbs_ref[...])
        pm_ref[...] = _bf(pm)
        ps_ref[...] = _bf(ps)
        y = _bf(_sigmoid(zgm[...].astype(F32)) * pm + _sigmoid(zgg[...].astype(F32)) * ps)
        y_ref[...] = y
        out = _dot(y, wo_ref[...])
        out_ref[...] = _bf(out)
        h1_ref[...] = x_ref[...] + mx2_ref[...] * out

    def tok(col):
        return pl.BlockSpec((tb, d), lambda i: (i, col))

    def full(shape):
        return pl.BlockSpec(shape, lambda i: (0,) * len(shape))

    return pl.pallas_call(
        body, name="mixer_fwd", grid=(t_rows // tb,),
        in_specs=[tok(0), tok(0), tok(3), tok(4), tok(5), tok(6), tok(7), tok(0), full((1, d)), full((1, d)), full((1, d)),
                  full((ng, sc, sc)), full((sc, LANES)), full((d, d)), full((d, d)), full((d, d)), full((1, d))],
        out_specs=[tok(0)] * 7,
        out_shape=[SDS((t_rows, d), F32)] + [SDS((t_rows, d), BF16)] * 6,
        compiler_params=_cp("arbitrary"))(hf, hb, z_main, z_main, z_main, z_main, z_main, xs, hg, lng, lnb, w_s, b_st,
                                          wbm, wbs, wout, mx2)


def _grid_taps(a_ext, n_ext):
    col = lax.broadcasted_iota(jnp.int32, (n_ext, 1), 0) % GRID_W
    left = jnp.where(col != 0, pltpu.roll(a_ext, 1, 0), 0.0)
    right = jnp.where(col != GRID_W - 1, pltpu.roll(a_ext, n_ext - 1, 0), 0.0)
    return left, right


def _with_halo(prev, main, nxt, i, ni, tb):
    ext = jnp.concatenate([prev, main, nxt], axis=0).astype(F32)
    pos = lax.broadcasted_iota(jnp.int32, (tb + 2 * GRID_W, 1), 0)
    inside = ((pos >= GRID_W) | (i > 0)) & ((pos < tb + GRID_W) | (i < ni - 1))
    return jnp.where(inside, ext, 0.0)


def _halo_specs(tb, cb, t_rows, col0=0):
    nh64 = tb // GRID_W
    return [pl.BlockSpec((tb, cb), lambda i, j: (i, col0 + j)),
            pl.BlockSpec((GRID_W, cb), lambda i, j: (jnp.maximum(i * nh64 - 1, 0), col0 + j)),
            pl.BlockSpec((GRID_W, cb), lambda i, j: (jnp.minimum((i + 1) * nh64, t_rows // GRID_W - 1), col0 + j))]


def _ffn_tail(ab, w_conv9, w_down, h1, mx5, gfin, target, dff):
    t_rows, d = h1.shape
    tb = _pick(t_rows, (256,))
    cb = _pick(dff, (1408, 256, 128))
    ni, nj = t_rows // tb, dff // cb
    n_ext = tb + 2 * GRID_W

    def body(am, ap, an, b_ref, wc_ref, wd_ref, h1_ref, mx5_ref, gf_ref, tg_ref, ac_ref, f_ref, dh2_ref, dffn_ref, st_ref, acc):
        i, j = pl.program_id(0), pl.program_id(1)
        a_ext = _with_halo(ap[...], am[...], an[...], i, ni, tb)
        left, right = _grid_taps(a_ext, n_ext)
        conv = jnp.zeros((tb, cb), F32)
        for di in range(3):
            o = di * GRID_W
            conv = conv + (wc_ref[3 * di:3 * di + 1, :] * left[o:o + tb] + wc_ref[3 * di + 1:3 * di + 2, :] * a_ext[o:o + tb]
                           + wc_ref[3 * di + 2:3 * di + 3, :] * right[o:o + tb])
        ac_ref[...] = _bf(conv)
        fb = _bf(conv * _sigmoid(conv) * b_ref[...].astype(F32))
        f_ref[...] = fb

        @pl.when(j == 0)
        def _():
            acc[...] = jnp.zeros_like(acc)

        @pl.when((i == 0) & (j == 0))
        def _():
            st_ref[...] = jnp.zeros_like(st_ref)

        acc[...] += _dot(fb, wd_ref[...])

        @pl.when(j == nj - 1)
        def _():
            ffn = acc[...]
            h2 = h1_ref[...] + mx5_ref[...] * ffn
            r = lax.rsqrt(jnp.mean(h2 * h2, axis=-1, keepdims=True) + EPS)
            xn = h2 * r
            e = xn * gf_ref[...] - tg_ref[...]
            loss = 0.5 * jnp.sum(jnp.sum(e * e, axis=1, keepdims=True), axis=0, keepdims=True) / d
            dy = e * (1.0 / d)
            dxn = dy * gf_ref[...]
            dh2 = r * (dxn - xn * jnp.mean(dxn * xn, axis=-1, keepdims=True))
            dh2_ref[...] = dh2
            dffn_ref[...] = _bf(dh2 * mx5_ref[...])
            st_ref[...] += jnp.concatenate(
                [jnp.sum(dy * xn, axis=0, keepdims=True), jnp.sum(dh2 * ffn, axis=0, keepdims=True),
                 jnp.broadcast_to(loss, (1, d)), jnp.zeros((5, d), F32)], axis=0)

    def tokd():
        return pl.BlockSpec((tb, d), lambda i, j: (i, 0))

    def rowd():
        return pl.BlockSpec((1, d), lambda i, j: (0, 0))

    return pl.pallas_call(
        body, name="ffn_tail", grid=(ni, nj),
        in_specs=_halo_specs(tb, cb, t_rows) + [pl.BlockSpec((tb, cb), lambda i, j: (i, nj + j)),
                                                pl.BlockSpec((16, cb), lambda i, j: (0, j)),
                                                pl.BlockSpec((cb, d), lambda i, j: (j, 0)), tokd(), rowd(), rowd(), tokd()],
        out_specs=[pl.BlockSpec((tb, cb), lambda i, j: (i, j)), pl.BlockSpec((tb, cb), lambda i, j: (i, j)), tokd(), tokd(),
                   pl.BlockSpec((8, d), lambda i, j: (0, 0))],
        out_shape=[SDS((t_rows, dff), BF16), SDS((t_rows, dff), BF16), SDS((t_rows, d), F32), SDS((t_rows, d), BF16),
                   SDS((8, d), F32)],
        scratch_shapes=[pltpu.VMEM((tb, d), F32)],
        compiler_params=_cp("arbitrary", "arbitrary"))(ab, ab, ab, ab, w_conv9, w_down, h1, mx5, gfin, target)


def _ffn_bwd_gate(dffn, w_down, aconv, ab, dff):
    t_rows, d = dffn.shape
    tb = _pick(t_rows, (512,))
    cb = _pick(dff, (1408, 256, 128))
    nj = dff // cb

    def body(g_ref, wd_ref, ac_ref, b_ref, db_ref, dac_ref):
        df = _dot_nt(g_ref[...], wd_ref[...])
        ac = ac_ref[...].astype(F32)
        sa = _sigmoid(ac)
        db_ref[...] = _bf(df * ac * sa)
        dac_ref[...] = _bf(df * b_ref[...].astype(F32) * (sa * (1.0 + ac * (1.0 - sa))))

    blk = pl.BlockSpec((tb, cb), lambda i, j: (i, j))
    return pl.pallas_call(
        body, name="ffn_bwd_gate", grid=(t_rows // tb, nj),
        in_specs=[pl.BlockSpec((tb, d), lambda i, j: (i, 0)), pl.BlockSpec((cb, d), lambda i, j: (j, 0)), blk,
                  pl.BlockSpec((tb, cb), lambda i, j: (i, nj + j))],
        out_specs=[blk, blk], out_shape=[SDS((t_rows, dff), BF16)] * 2,
        compiler_params=_cp("arbitrary", "arbitrary"))(dffn, w_down, aconv, ab)


def _ffn_conv_bwd(dac, ab, w_conv9, dff):
    t_rows = dac.shape[0]
    tb = _pick(t_rows, (256,))
    cb = _pick(dff, (1408, 256, 128))
    ni, nj = t_rows // tb, dff // cb
    n_ext = tb + 2 * GRID_W
    nh64 = tb // GRID_W

    def body(dm, dp, dn, am, ap, an, wc_ref, da_ref, gw_ref):
        i = pl.program_id(1)
        d_ext = _with_halo(dp[...], dm[...], dn[...], i, ni, tb)
        a_ext = _with_halo(ap[...], am[...], an[...], i, ni, tb)
        d_left, d_right = _grid_taps(d_ext, n_ext)
        a_left, a_right = _grid_taps(a_ext, n_ext)
        dmain = d_ext[GRID_W:GRID_W + tb]
        da = jnp.zeros((tb, cb), F32)
        rows = []
        for di in range(3):
            o = (2 - di) * GRID_W
            da = da + (wc_ref[3 * di:3 * di + 1, :] * d_right[o:o + tb] + wc_ref[3 * di + 1:3 * di + 2, :] * d_ext[o:o + tb]
                       + wc_ref[3 * di + 2:3 * di + 3, :] * d_left[o:o + tb])
            o = di * GRID_W
            for tap in (a_left, a_ext, a_right):
                rows.append(jnp.sum(dmain * tap[o:o + tb], axis=0, keepdims=True))
        da_ref[...] = _bf(da)

        @pl.when(i == 0)
        def _():
            gw_ref[...] = jnp.zeros_like(gw_ref)

        gw_ref[...] += jnp.concatenate(rows + [jnp.zeros((7, cb), F32)], axis=0)

    def halo(col0):
        return [pl.BlockSpec((tb, cb), lambda j, i: (i, col0 + j)),
                pl.BlockSpec((GRID_W, cb), lambda j, i: (jnp.maximum(i * nh64 - 1, 0), col0 + j)),
                pl.BlockSpec((GRID_W, cb), lambda j, i: (jnp.minimum((i + 1) * nh64, t_rows // GRID_W - 1), col0 + j))]

    return pl.pallas_call(
        body, name="ffn_conv_bwd", grid=(nj, ni),
        in_specs=halo(0) + halo(0) + [pl.BlockSpec((16, cb), lambda j, i: (0, j))],
        out_specs=[pl.BlockSpec((tb, cb), lambda j, i: (i, j)), pl.BlockSpec((16, cb), lambda j, i: (0, j))],
        out_shape=[SDS((t_rows, dff), BF16), SDS((16, dff), F32)],
        compiler_params=_cp("arbitrary", "arbitrary"))(dac, dac, dac, ab, ab, ab, w_conv9)


def _proj_norm_bwd(pairs, x_arr, x_row0, g, scale, resid, m_rows, name, row_blocks=(1024, 256)):
    d = x_arr.shape[1]
    tm = _pick(m_rows, row_blocks)
    te = 256
    ni = m_rows // tm
    once = pl.Buffered(1)
    starts, total = [], 0
    for (_, _, _, _, k_p, tk_p) in pairs:
        starts.append(total)
        total += k_p // tk_p
    npairs = len(pairs)
    has_dx = resid is not None

    def body(*refs):
        a_refs, b_refs = refs[0:2 * npairs:2], refs[1:2 * npairs:2]
        rest = refs[2 * npairs:]
        if has_dx:
            x_ref, g_ref, sc_ref, r_ref, dx_ref, st_ref, acc = rest
        else:
            x_ref, g_ref, sc_ref, st_ref, acc = rest
        i, k = pl.program_id(0), pl.program_id(1)

        @pl.when(k == 0)
        def _():
            acc[...] = jnp.zeros_like(acc)

        @pl.when((i == 0) & (k == 0))
        def _():
            st_ref[...] = jnp.zeros_like(st_ref)

        for p in range(npairs):
            nk = pairs[p][4] // pairs[p][5]

            @pl.when((k >= starts[p]) & (k < starts[p] + nk))
            def _(p=p):
                acc[...] += _dot_nt(a_refs[p][...], b_refs[p][...])

        @pl.when(k == total - 1)
        def _():
            sums = [jnp.zeros((1, d), F32)] * 3
            for r0 in range(0, tm, te):
                rows = slice(r0, r0 + te)
                dhn = acc[rows, :]
                x = x_ref[rows, :]
                r = lax.rsqrt(jnp.mean(x * x, axis=-1, keepdims=True) + EPS)
                xn = x * r
                dmod = dhn * (1.0 + sc_ref[...])
                dxn = dmod * g_ref[...]
                if has_dx:
                    dx_ref[rows, :] = r * (dxn - xn * jnp.mean(dxn * xn, axis=-1, keepdims=True)) + r_ref[rows, :]
                sums = [sums[0] + jnp.sum(dmod * xn, axis=0, keepdims=True), sums[1] + jnp.sum(dhn, axis=0, keepdims=True),
                        sums[2] + jnp.sum(dhn * (xn * g_ref[...]), axis=0, keepdims=True)]
            st_ref[...] += jnp.concatenate(sums + [jnp.zeros((5, d), F32)], axis=0)

    in_specs, args = [], []
    for p, (a, a_row0, b, b_col0, k_p, tk_p) in enumerate(pairs):
        nk, s0, ar, bc = k_p // tk_p, starts[p], a_row0 // tm, b_col0 // tk_p

        def kk(k, s0=s0, nk=nk):
            return jnp.clip(k - s0, 0, nk - 1)

        in_specs.append(pl.BlockSpec((tm, tk_p), lambda i, k, ar=ar, kk=kk: (ar + i, kk(k))))
        in_specs.append(pl.BlockSpec((d, tk_p), lambda i, k, bc=bc, kk=kk: (0, bc + kk(k))))
        args += [a, b]
    xr = x_row0 // tm
    in_specs += [pl.BlockSpec((tm, d), lambda i, k: (xr + i, 0), pipeline_mode=once), pl.BlockSpec((1, d), lambda i, k: (0, 0)),
                 pl.BlockSpec((1, d), lambda i, k: (0, 0))]
    args += [x_arr, g, scale]
    out_specs, out_shape = [], []
    if has_dx:
        in_specs.append(pl.BlockSpec((tm, d), lambda i, k: (i, 0), pipeline_mode=once))
        args.append(resid)
        out_specs.append(pl.BlockSpec((tm, d), lambda i, k: (i, 0)))
        out_shape.append(SDS((m_rows, d), F32))
    out_specs.append(pl.BlockSpec((8, d), lambda i, k: (0, 0)))
    out_shape.append(SDS((8, d), F32))
    return pl.pallas_call(
        body, name=name, grid=(ni, total), in_specs=in_specs, out_specs=out_specs, out_shape=out_shape,
        scratch_shapes=[pltpu.VMEM((tm, d), F32)], compiler_params=_cp("arbitrary", "arbitrary"))(*args)


def _wgrad(a, b, k_rows, name):
    m, n = a.shape[1], b.shape[1]
    tm = _pick(m, (1408, 1024, 512, 384, 256, 128))
    tn = _pick(n, (1408, 1024, 768, 512, 384, 256, 128))
    tk = _pick(k_rows, (1280, 1024, 256))
    nk = k_rows // tk

    def body(a_ref, b_ref, o_ref, acc):
        k = pl.program_id(2)

        @pl.when(k == 0)
        def _():
            acc[...] = jnp.zeros_like(acc)

        acc[...] += _dot_tn(a_ref[...], b_ref[...])

        @pl.when(k == nk - 1)
        def _():
            o_ref[...] = _bf(acc[...])

    return pl.pallas_call(
        body, name=name, grid=(m // tm, n // tn, nk),
        in_specs=[pl.BlockSpec((tk, tm), lambda i, j, k: (k, i)), pl.BlockSpec((tk, tn), lambda i, j, k: (k, j))],
        out_specs=pl.BlockSpec((tm, tn), lambda i, j, k: (i, j)), out_shape=SDS((m, n), BF16),
        scratch_shapes=[pltpu.VMEM((tm, tn), F32)],
        compiler_params=_cp("arbitrary", "arbitrary", "arbitrary"))(a, b)


def _lane_put(col, lane_idx):
    lane = lax.broadcasted_iota(jnp.int32, (1, LANES), 1)
    return jnp.where(lane == lane_idx, col, 0.0)


def _mixer_bwd(dh1, out, hf, hb, z_main, pm, ps, hg, lng, lnb, w_s, b_st, wbm, wbs, wout, mx2, t_rows, nh):
    d = dh1.shape[1]
    ng, sc = w_s.shape[0], w_s.shape[1]
    dh, gd = d // nh, d // ng
    tb = _pick(t_rows, (256,))

    def body(dh1_ref, out_ref, hf_ref, hb_ref, zo, zu, zv, zgm, zgg, pm_ref, ps_ref, hg_ref, lng_ref, lnb_ref, ws_ref, bs_ref,
             wbm_ref, wbs_ref, wo_ref, mx2_ref, dz_ref, dhs_ref, dout_ref, dpm_ref, dps_ref, st_ref, dws_ref, dbs_ref):
        i = pl.program_id(0)

        @pl.when(i == 0)
        def _():
            st_ref[...] = jnp.zeros_like(st_ref)
            dws_ref[...] = jnp.zeros_like(dws_ref)
            dbs_ref[...] = jnp.zeros_like(dbs_ref)

        dh1v = dh1_ref[...]
        doutb = _bf(dh1v * mx2_ref[...])
        dout_ref[...] = doutb
        d_mx2 = jnp.sum(dh1v * out_ref[...].astype(F32), axis=0, keepdims=True)
        dy = _dot_nt(doutb, wo_ref[...])
        sgm, sgg = _sigmoid(zgm[...].astype(F32)), _sigmoid(zgg[...].astype(F32))
        dpmb, dpsb = _bf(dy * sgm), _bf(dy * sgg)
        dpm_ref[...] = dpmb
        dps_ref[...] = dpsb
        dz_ref[:, 3 * d:4 * d] = _bf(dy * pm_ref[...].astype(F32) * sgm * (1.0 - sgm))
        dz_ref[:, 4 * d:5 * d] = _bf(dy * ps_ref[...].astype(F32) * sgg * (1.0 - sgg))
        dym = _dot_nt(dpmb, wbm_ref[...])
        dys = _dot_nt(dpsb, wbs_ref[...])
        hs = hf_ref[...].astype(F32) + hb_ref[...].astype(F32)
        hn, scales = _head_rms(hs, nh, dh)
        so = _sigmoid(zo[...].astype(F32))
        dz_ref[:, 0:d] = _bf(dym * (hn * hg_ref[...]) * so * (1.0 - so))
        dhmn = dym * so
        d_hg = jnp.sum(dhmn * hn, axis=0, keepdims=True)
        dhn = dhmn * hg_ref[...]
        for h in range(nh):
            sl = slice(h * dh, (h + 1) * dh)
            dhs_ref[:, sl] = _bf(scales[h] * (dhn[:, sl] - hn[:, sl] * jnp.mean(dhn[:, sl] * hn[:, sl], axis=-1, keepdims=True)))
        zuv, zvv = zu[...].astype(F32), zv[...].astype(F32)
        u, du_dz = _gelu_and_grad(zuv)
        vg, dvg_dz = _gelu_and_grad(zvv)
        vhat, rstd = _layer_norm(vg)
        vnb = _bf(vhat * lng_ref[...] + lnb_ref[...])
        mixed = _sgu_mix(vnb, ws_ref, bs_ref, tb, ng, gd, sc)
        dz_ref[:, d:2 * d] = _bf(dys * mixed * du_dz)
        dmix = dys * u
        rows = []
        dbs = jnp.zeros((sc, LANES), F32)
        for ch in range(tb // sc):
            cols = []
            for g in range(ng):
                dm = dmix[ch * sc:(ch + 1) * sc, g * gd:(g + 1) * gd]
                dmb = _bf(dm)
                dws_ref[g] += _dot_nt(dmb, vnb[ch * sc:(ch + 1) * sc, g * gd:(g + 1) * gd])
                dbs = dbs + _lane_put(jnp.sum(dm, axis=1, keepdims=True), g)
                cols.append(_dot_tn(_bf(ws_ref[g]), dmb))
            rows.append(jnp.concatenate(cols, axis=1))
        dbs_ref[...] += dbs
        dvn = jnp.concatenate(rows, axis=0)
        d_lng = jnp.sum(dvn * vhat, axis=0, keepdims=True)
        d_lnb = jnp.sum(dvn, axis=0, keepdims=True)
        dvh = dvn * lng_ref[...]
        dvg = rstd * (dvh - jnp.mean(dvh, axis=-1, keepdims=True) - vhat * jnp.mean(dvh * vhat, axis=-1, keepdims=True))
        dz_ref[:, 2 * d:3 * d] = _bf(dvg * dvg_dz)
        st_ref[...] += jnp.concatenate([d_mx2, d_hg, d_lng, d_lnb, jnp.zeros((4, d), F32)], axis=0)

    def tok(col):
        return pl.BlockSpec((tb, d), lambda i: (i, col))

    def full(shape):
        return pl.BlockSpec(shape, lambda i: (0,) * len(shape))

    return pl.pallas_call(
        body, name="mixer_bwd", grid=(t_rows // tb,),
        in_specs=[tok(0), tok(0), tok(0), tok(0), tok(3), tok(4), tok(5), tok(6), tok(7), tok(0), tok(0), full((1, d)),
                  full((1, d)), full((1, d)), full((ng, sc, sc)), full((sc, LANES)), full((d, d)), full((d, d)), full((d, d)),
                  full((1, d))],
        out_specs=[pl.BlockSpec((tb, 5 * d), lambda i: (i, 0)), tok(0), tok(0), tok(0), tok(0), full((8, d)), full((ng, sc, sc)),
                   full((sc, LANES))],
        out_shape=[SDS((t_rows, 5 * d), BF16)] + [SDS((t_rows, d), BF16)] * 4 + [SDS((8, d), F32), SDS((ng, sc, sc), F32),
                                                                                SDS((sc, LANES), F32)],
        compiler_params=_cp("arbitrary"))(dh1, out, hf, hb, z_main, z_main, z_main, z_main, z_main, pm, ps, hg, lng, lnb, w_s,
                                          b_st, wbm, wbs, wout, mx2)


def _mlstm_bwd(qk, z_main, zg, bias, dhs, states_f, states_b, nh, t_rows):
    s_rows = qk.shape[0]
    md = qk.shape[1] // 2
    dh = md // nh
    nc = s_rows // LCH
    nx = t_rows // LCH
    ln = LCH

    def chunk_f(i):
        return jnp.where(i == nc - 1, nc - 1, nc - 2 - i)

    def chunk_b(i):
        return jnp.where(i == nc - 1, nc - 1, i)

    def body(qf, kf, vf, gf, dhf, cf, nf, mf_, qb, kb, vb, gb, dhb, cb, nb, mb_, bias_ref, dqkvf_ref, dgf_ref, dqkvb_ref, dgb_ref,
             dc_sc, dn_sc):
        i = pl.program_id(0)
        is_ctx = i == nc - 1

        @pl.when(i == 0)
        def _():
            dc_sc[...] = jnp.zeros_like(dc_sc)
            dn_sc[...] = jnp.zeros_like(dn_sc)

        for dr, (q_ref, k_ref, v_ref, g_ref, dh_ref, c_ref, n_ref, m_ref, dqkv_ref, dg_ref) in enumerate(
                ((qf, kf, vf, gf, dhf, cf, nf, mf_, dqkvf_ref, dgf_ref), (qb, kb, vb, gb, dhb, cb, nb, mb_, dqkvb_ref, dgb_ref))):
            gz, b_all, b_t, g_t, g_all, mask, mfl = _chunk_gates(g_ref[...], bias_ref[...], dr == 1)
            x1 = jnp.zeros((ln, LANES), F32)
            x2 = jnp.zeros((ln, LANES), F32)
            dig = jnp.zeros((ln, LANES), F32)
            e_row = jnp.zeros((1, LANES), F32)
            for h in range(nh):
                ci, cfl = 2 * dr * nh + h, (2 * dr + 1) * nh + h
                sl = slice(h * dh, (h + 1) * dh)
                q, k, v = q_ref[:, sl], k_ref[:, sl], v_ref[:, sl]
                qf32, kf32 = q.astype(F32), k.astype(F32)
                dhv = jnp.where(is_ctx, 0.0, dh_ref[:, sl].astype(F32))
                c_in, n_in, m_in = c_ref[sl, :], n_ref[0:1, sl], m_ref[h, 0:1, 0:1]
                b_col, b_row, i_col, i_row = b_all[:, cfl:cfl + 1], b_t[cfl:cfl + 1, :], gz[:, ci:ci + 1], g_t[ci:ci + 1, :]
                g = g_all[:, cfl:cfl + 1]
                w, w_int, m_row = _head_weights(b_col, b_row, i_row, m_in, mask)
                s_mat = _dot_nt(q, k) * w
                sb, cb16 = _bf(s_mat), _bf(c_in)
                num = _dot(sb, v) + w_int * _dot(q, cb16)
                den = jnp.sum(s_mat, axis=1, keepdims=True) + w_int * jnp.sum(qf32 * n_in, axis=1, keepdims=True)
                e_m = jnp.exp(-m_row)
                dnm = jnp.maximum(jnp.abs(den), e_m)
                dnum = dhv / dnm
                hdh = jnp.sum((num / dnm) * dhv, axis=1, keepdims=True)
                dden = jnp.where(jnp.abs(den) > e_m, -(hdh / dnm) * jnp.sign(den), 0.0)
                dnum_b = _bf(dnum)
                ds = _dot_nt(dnum_b, v) + dden
                pb = _bf(w * ds)
                gmat = s_mat * ds
                a_old, coef, _ = _head_state_coeffs(g, b_col, i_col, m_in)
                dc_new, dn_new = dc_sc[dr, h], dn_sc[dr, h, 0:1, :]
                dcb = _bf(dc_new)
                dv = _dot_tn(sb, dnum_b) + _dot(_bf(kf32 * coef), dcb)
                dq_inter = w_int * (_dot_nt(dnum_b, cb16) + dden * n_in)
                dq = _dot(pb, k) + dq_inter
                dk_state = coef * (_dot_nt(v, dcb) + dn_new)
                dk = _dot_tn(pb, q) + dk_state
                dqkv_ref[:, sl] = _bf(dq)
                dqkv_ref[:, md + h * dh:md + (h + 1) * dh] = _bf(dk)
                dqkv_ref[:, 2 * md + h * dh:2 * md + (h + 1) * dh] = _bf(dv)
                row_intra = jnp.sum(gmat, axis=1, keepdims=True)
                col_intra = jnp.sum(gmat.T, axis=1, keepdims=True)
                row_inter = jnp.sum(qf32 * dq_inter, axis=1, keepdims=True)
                col_inter = jnp.sum(kf32 * dk_state, axis=1, keepdims=True)
                e_old = a_old * (jnp.sum(jnp.sum(dc_new * c_in, axis=1, keepdims=True), axis=0, keepdims=True)
                                 + jnp.sum(dn_new * n_in, axis=1, keepdims=True))
                x1 = x1 + _lane_put(row_intra - col_intra + row_inter, cfl)
                x2 = x2 + _lane_put(col_inter, cfl)
                e_row = e_row + _lane_put(e_old, cfl)
                dig = dig + _lane_put(col_intra + col_inter, ci)
                dc_sc[dr, h] = a_old * dc_new + _dot_tn(_bf(qf32 * w_int), dnum_b)
                dn_sc[dr, h] = jnp.broadcast_to(a_old * dn_new + jnp.sum(qf32 * (w_int * dden), axis=0, keepdims=True), (8, dh))
            dlogf = _mask_dot_t(mfl, x1) + _mask_dot(mfl, x2) - x2 + e_row
            dg_ref[...] = dig + dlogf / (1.0 + jnp.exp(gz))

    def tok(cfn, col):
        return pl.BlockSpec((ln, md), lambda i: (cfn(i), col))

    def dht(cfn):
        return pl.BlockSpec((ln, md), lambda i: (jnp.minimum(cfn(i), nx - 1), 0))

    def gat(cfn):
        return pl.BlockSpec((ln, LANES), lambda i: (cfn(i), 0))

    def st(cfn, shape):
        return pl.BlockSpec((None,) + shape, lambda i: (cfn(i),) + (0,) * len(shape))

    st_shapes = ((nh * dh, dh), (8, md), (nh, 8, LANES))

    def side(cfn):
        return [tok(cfn, 0), tok(cfn, 1), tok(cfn, 2), gat(cfn), dht(cfn)] + [st(cfn, s) for s in st_shapes]

    def outs(cfn):
        return [pl.BlockSpec((ln, 3 * md), lambda i: (cfn(i), 0)), gat(cfn)]

    return pl.pallas_call(
        body, name="mlstm_bwd", grid=(nc,),
        in_specs=side(chunk_f) + side(chunk_b) + [pl.BlockSpec((1, LANES), lambda i: (0, 0))],
        out_specs=outs(chunk_f) + outs(chunk_b),
        out_shape=[SDS((s_rows, 3 * md), BF16), SDS((s_rows, LANES), F32)] * 2,
        scratch_shapes=[pltpu.VMEM((2, nh, dh, dh), F32), pltpu.VMEM((2, nh, 8, dh), F32)],
        compiler_params=_cp("arbitrary"))(qk, qk, z_main, zg, dhs, *states_f, qk, qk, z_main, zg, dhs, *states_b, bias)


def _qkv_conv_bwd(dqkv_f, dqkv_b, z_main, conv_w, t_rows, md, qscale):
    s_rows = z_main.shape[0]
    tb = _pick(s_rows, (1280, 1024, 256))
    cb = _pick(md, (512, 256, 128))
    ni, nj, ncq = s_rows // tb, 3 * md // cb, 2 * md // cb
    nb8 = tb // 8
    n_ext = tb + 16

    def body(fm, fp, fn, bm, bp, bn, zm, zp, zn, w_ref, dz_ref, gw_ref):
        j, i = pl.program_id(0), pl.program_id(1)

        @pl.when(j < ncq)
        def _():
            z = jnp.concatenate([zp[...], zm[...], zn[...]], axis=0).astype(F32)
            dqk = (jnp.concatenate([fp[...], fm[...], fn[...]], axis=0).astype(F32)
                   + jnp.concatenate([bp[...], bm[...], bn[...]], axis=0).astype(F32)) * jnp.where(j * cb < md, qscale, 1.0)
            row = i * tb - 8 + lax.broadcasted_iota(jnp.int32, (n_ext, 1), 0)
            prev_ok, next_ok = _seg_masks(row, t_rows, s_rows)
            zprev = jnp.where(prev_ok, pltpu.roll(z, 1, 0), 0.0)
            znext = jnp.where(next_ok, pltpu.roll(z, n_ext - 1, 0), 0.0)
            pre = w_ref[0:1, :] * zprev + w_ref[1:2, :] * z + w_ref[2:3, :] * znext
            sg = _sigmoid(pre)
            dpre = dqk * (sg * (1.0 + pre * (1.0 - sg)))
            dz = (w_ref[1:2, :] * dpre + w_ref[0:1, :] * jnp.where(next_ok, pltpu.roll(dpre, n_ext - 1, 0), 0.0)
                  + w_ref[2:3, :] * jnp.where(prev_ok, pltpu.roll(dpre, 1, 0), 0.0))
            dz_ref[...] = _bf(dz[8:8 + tb])
            dm = dpre[8:8 + tb]

            @pl.when(i == 0)
            def _():
                gw_ref[...] = jnp.zeros_like(gw_ref)

            gw_ref[...] += jnp.concatenate(
                [jnp.sum(dm * zprev[8:8 + tb], axis=0, keepdims=True), jnp.sum(dm * z[8:8 + tb], axis=0, keepdims=True),
                 jnp.sum(dm * znext[8:8 + tb], axis=0, keepdims=True), jnp.zeros((5, cb), F32)], axis=0)

        @pl.when(j >= ncq)
        def _():
            dz_ref[...] = _bf(fm[...].astype(F32) + bm[...].astype(F32))

    def halo(clampj):
        def cj(j):
            return jnp.minimum(j, ncq - 1) if clampj else j
        return [pl.BlockSpec((tb, cb), lambda j, i: (i, cj(j))),
                pl.BlockSpec((8, cb), lambda j, i: (jnp.maximum(i * nb8 - 1, 0), cj(j))),
                pl.BlockSpec((8, cb), lambda j, i: (jnp.minimum((i + 1) * nb8, s_rows // 8 - 1), cj(j)))]

    return pl.pallas_call(
        body, name="qkv_conv_bwd", grid=(nj, ni),
        in_specs=halo(False) + halo(False) + halo(True) + [pl.BlockSpec((8, cb), lambda j, i: (0, jnp.minimum(j, ncq - 1)))],
        out_specs=[pl.BlockSpec((tb, cb), lambda j, i: (i, j)), pl.BlockSpec((8, cb), lambda j, i: (0, jnp.minimum(j, ncq - 1)))],
        out_shape=[SDS((s_rows, 3 * md), BF16), SDS((8, 2 * md), F32)],
        compiler_params=_cp("arbitrary", "arbitrary"))(dqkv_f, dqkv_f, dqkv_f, dqkv_b, dqkv_b, dqkv_b, z_main, z_main, z_main, conv_w)


def _gate_grad_sum(dg_f, dg_b):
    s_rows = dg_f.shape[0]
    tb = _pick(s_rows, (1280, 1024, 256))

    def body(a_ref, b_ref, o_ref, st_ref):
        @pl.when(pl.program_id(0) == 0)
        def _():
            st_ref[...] = jnp.zeros_like(st_ref)

        s = a_ref[...] + b_ref[...]
        o_ref[...] = _bf(s)
        st_ref[...] += jnp.concatenate([jnp.sum(s, axis=0, keepdims=True), jnp.zeros((7, LANES), F32)], axis=0)

    blk = pl.BlockSpec((tb, LANES), lambda i: (i, 0))
    return pl.pallas_call(
        body, name="gate_grad_sum", grid=(s_rows // tb,), in_specs=[blk, blk],
        out_specs=[blk, pl.BlockSpec((8, LANES), lambda i: (0, 0))],
        out_shape=[SDS((s_rows, LANES), BF16), SDS((8, LANES), F32)], compiler_params=_cp("arbitrary"))(dg_f, dg_b)


def _mod_grads(silu_slots, dmx_sh, dmx_slots, dmc_tot, dmc_sh, silu_cctx, c_ctx, w_mod_c):
    d = silu_slots.shape[1]
    ncol, n6 = dmx_sh.shape[1], dmx_slots.shape[1]

    def body(ss_ref, dsh_ref, dsl_ref, dct_ref, dcs_ref, sc_ref, c_ref, w_ref, gw_ref, gb_ref, gc_ref):
        a = jnp.concatenate([ss_ref[...], sc_ref[...], jnp.zeros((7, d), F32)], axis=0)
        b = jnp.concatenate([dsh_ref[...], dcs_ref[...], jnp.zeros((7, ncol), F32)], axis=0)
        gw_ref[0] = lax.dot_general(a, b, (((0,), (0,)), ((), ())), preferred_element_type=F32, precision=HI)
        dct = dct_ref[...]
        gb_ref[...] = jnp.sum(dsl_ref[...], axis=0, keepdims=True) + jnp.concatenate(
            [dct, jnp.zeros((1, n6 - dct.shape[1]), F32)], axis=1)
        t = _dot_nt(_bf(jnp.broadcast_to(dct, (8, dct.shape[1]))), w_ref[...])
        cv = c_ref[...]
        s = _sigmoid(cv)
        gc_ref[...] = t[0:1, :] * (s * (1.0 + cv * (1.0 - s)))

    return pl.pallas_call(body, name="mod_grads", out_shape=[SDS((1, d, ncol), F32), SDS((1, n6), F32), SDS((1, d), F32)],
                          compiler_params=_cp())(silu_slots, dmx_sh, dmx_slots, dmc_tot, dmc_sh, silu_cctx, c_ctx, w_mod_c)


def _slot_sum(slots):
    ns, r = slots.shape[0], slots.shape[1]
    tb = _pick(r, (1024, 512, 256, 128, 64, 32, 16, 8))

    def body(s_ref, o_ref):
        acc = s_ref[0]
        for k in range(1, ns):
            acc = acc + s_ref[k]
        o_ref[...] = acc

    return pl.pallas_call(
        body, name="slot_sum", grid=(r // tb,), in_specs=[pl.BlockSpec((ns, tb, LANES), lambda i: (0, i, 0))],
        out_specs=pl.BlockSpec((tb, LANES), lambda i: (i, 0)), out_shape=SDS((r, LANES), F32),
        compiler_params=_cp("arbitrary"))(slots)


def _adamw(w, gslots, m, v, name):
    lead = ((None,), (0,)) if w.ndim == 3 else ((), ())
    r, cdim = w.shape[-2:]
    ns, rg = gslots.shape[0], gslots.shape[1]
    tb = r if (rg != r or r % 8) else _pick(r, (128, 64, 32, 16, 8))
    bc1, bc2 = 1.0 - ADAM_B1 ** ADAM_STEP, 1.0 - ADAM_B2 ** ADAM_STEP

    def body(w_ref, g_ref, m_ref, v_ref, go_ref, d_ref, mo_ref, vo_ref):
        g = g_ref[0, 0:tb, :].astype(F32)
        for k in range(1, ns):
            g = g + g_ref[k, 0:tb, :].astype(F32)
        mn = ADAM_B1 * m_ref[...] + (1.0 - ADAM_B1) * g
        vn = ADAM_B2 * v_ref[...] + (1.0 - ADAM_B2) * (g * g)
        go_ref[...] = g
        mo_ref[...] = mn
        vo_ref[...] = vn
        d_ref[...] = -ADAM_LR * ((mn / bc1) / (jnp.sqrt(vn / bc2) + ADAM_EPS) + ADAM_WD * w_ref[...])

    blk = pl.BlockSpec(lead[0] + (tb, cdim), lambda i: lead[1] + (i, 0))
    gblk = pl.BlockSpec((ns, tb if rg == r else rg, cdim), lambda i: (0, i, 0))
    return pl.pallas_call(
        body, name=name, grid=(r // tb,), in_specs=[blk, gblk, blk, blk],
        out_specs=[blk] * 4, out_shape=[SDS(w.shape, F32)] * 4, compiler_params=_cp("arbitrary"))(w, gslots, m, v)


def _pack(parts, row_mult):
    flat = jnp.concatenate([p.reshape(-1) for p in parts])
    n = flat.shape[0]
    rows = -(-n // LANES)
    rows = -(-rows // row_mult) * row_mult
    return jnp.pad(flat, (0, rows * LANES - n)).reshape(rows, LANES)


def _unpack(buf, shapes):
    flat = buf.reshape(-1)
    out, off = [], 0
    for s in shapes:
        n = math.prod(s)
        out.append(flat[off:off + n].reshape(s))
        off += n
    return out


def _pad_cols(a, width):
    return jnp.pad(a, ((0, 0), (0, width - a.shape[1])))


def _pad_lanes(a):
    return _pad_cols(a, LANES)


def _up128(n):
    return -(-n // LANES) * LANES


def kernel(x, c, ctx, c_ctx, w_mod, b_mod, norm1_g, w_in, b_gate, conv_qk, head_norm_g, sgu_ln_g, sgu_ln_b, w_s, b_s, w_branch_mlstm, w_branch_sgu, w_out, norm2_g, w_up, w_ffn_conv, w_down, final_g, loss_target, m_c_ctx, m_w_mod, m_b_mod, m_norm1_g, m_w_in, m_b_gate, m_conv_qk, m_head_norm_g, m_sgu_ln_g, m_sgu_ln_b, m_w_s, m_b_s, m_w_branch_mlstm, m_w_branch_sgu, m_w_out, m_norm2_g, m_w_up, m_w_ffn_conv, m_w_down, m_final_g, v_c_ctx, v_w_mod, v_b_mod, v_norm1_g, v_w_in, v_b_gate, v_conv_qk, v_head_norm_g, v_sgu_ln_g, v_sgu_ln_b, v_w_s, v_b_s, v_w_branch_mlstm, v_w_branch_sgu, v_w_out, v_norm2_g, v_w_up, v_w_ffn_conv, v_w_down, v_final_g):
    t, d = x.shape[1], x.shape[2]
    n_ctx = ctx.shape[1]
    s_rows = t + n_ctx
    nh = b_gate.shape[1] // 4
    md = head_norm_g.shape[1]
    dh = md // nh
    ng, sc = w_s.shape[1], w_s.shape[2]
    dff = w_down.shape[1] * N_DEV
    n_in = w_in.shape[2] * N_DEV
    assert md == d and sgu_ln_g.shape[1] == d and n_ctx == LCH and t % LCH == 0 and t % (8 * GRID_W) == 0
    assert n_in == 8 * d + 4 * nh and 4 * nh <= LANES
    me = 4 * lax.axis_index("x") + 2 * lax.axis_index("y") + lax.axis_index("c")

    n_mod, n_insh, n_upsh = w_mod.shape[2], w_in.shape[2], w_up.shape[2]
    p_mod, p_in, p_up = _up128(n_mod), _up128(n_insh), _up128(n_upsh)
    nq, nf = conv_qk.shape[2], w_ffn_conv.shape[3]
    ffn9 = w_ffn_conv[0].reshape(9, nf)
    colpack = jnp.concatenate([_pad_cols(_bf(w_mod[0]), p_mod), _pad_cols(_bf(w_in[0]), p_in)], axis=1)
    convpack = jnp.concatenate([jnp.pad(conv_qk[0], ((0, 13), (0, 0))), jnp.pad(ffn9, ((0, 7), (0, 0)))], axis=1)
    g_col, g_conv = _allgather([colpack, convpack])
    w_mod_f, w_main, w_gate = _assemble_cols(
        g_col, [(0, n_mod, [(0, 0, N_DEV * n_mod, 0)]),
                (p_mod, n_insh, [(1, 0, 3 * md, 0), (2, 3 * md, 4 * nh, 0), (1, 3 * md + 4 * nh, 5 * d, 3 * md)])],
        [N_MOD * d, 8 * d, LANES], "assemble_weights")
    convw, wconv9 = _assemble_cols(g_conv, [(0, nq, [(0, 0, N_DEV * nq, 0)]), (nq, nf, [(1, 0, N_DEV * nf, 0)])],
                                   [N_DEV * nq, N_DEV * nf], "assemble_conv_weights")
    zero = jnp.minimum(jnp.abs(g_conv[0, 0, 0]), 0.0)
    late_w = [_pad_cols(_bf(w_up[0] + zero), p_up), _bf(w_branch_mlstm[0]), _bf(w_branch_sgu[0]), _bf(w_out[0]), _bf(w_down[0])]
    late_state, late_tok = _exchange_start(late_w, False, "late_weights_start")

    cvec = jnp.concatenate([c, c_ctx[None], jnp.zeros((6, d), F32)], axis=0) + late_tok[0:1, 0:1]
    silu_v, mod = _modulation(cvec, w_mod_f, b_mod)
    mx = [mod[0:1, k * d:(k + 1) * d] for k in range(N_MOD)]
    mc = [mod[1:2, k * d:(k + 1) * d] for k in range(2)]
    x2, ctx2 = x[0], ctx[0]
    in_x = _norm_mod_proj(x2, norm1_g, jnp.concatenate([mx[0], mx[1]], axis=0), w_main, w_gate, s_rows, 0, None, "in_proj")
    hn, z_main, zg = _norm_mod_proj(ctx2, norm1_g, jnp.concatenate([mc[0], mc[1]], axis=0), w_main, w_gate, s_rows, t, in_x,
                                    "in_proj_ctx")
    qscale = dh ** -0.5
    qk = _qk_conv(z_main, convw, t, md, qscale)
    bias = _pad_lanes(b_gate)
    fwd = _mlstm_fwd(qk, z_main, zg, bias, nh)
    hf, hb, states_f, states_b = fwd[0], fwd[1], fwd[2:5], fwd[5:8]
    g_up, g_bm, g_bs, g_out, g_down = _exchange_wait(late_state, fwd[4], "late_weights_wait")
    (w_up_f,) = _assemble_cols(g_up, [(0, n_upsh, [(0, 0, 2 * dff, 0)])], [2 * dff], "assemble_w_up")
    wbm_f, wbs_f, wout_f = (g.reshape(d, d) for g in (g_bm, g_bs, g_out))
    w_down_f = g_down.reshape(dff, d)
    b_st = _pad_lanes(b_s[0].T)
    h1, ym, ys, pm, ps, y, out = _mixer_fwd(hf, hb, z_main, x2, head_norm_g, sgu_ln_g, sgu_ln_b, w_s[0], b_st, wbm_f, wbs_f,
                                            wout_f, mx[2], t, nh)
    hn2, ab = _norm_mod_proj(h1, norm2_g, jnp.concatenate([mx[3], mx[4]], axis=0), w_up_f, None, t, 0, None, "up_proj")
    aconv, f, dh2, dffn, st_tail = _ffn_tail(ab, wconv9, w_down_f, h1, mx[5], final_g[None], loss_target[0], dff)

    db, dac = _ffn_bwd_gate(dffn, w_down_f, aconv, ab, dff)
    da, g_wconv9 = _ffn_conv_bwd(dac, ab, wconv9, dff)
    g_wdown = _wgrad(f, dffn, t, "wgrad_down")
    gwup_slots = _scatter_cols([_wgrad(hn2, da, t, "wgrad_up_a"), _wgrad(hn2, db, t, "wgrad_up_b")],
                               [(0, 0, dff, 0), (1, dff, dff, 0)], n_upsh, "scatter_grad_w_up")
    tkf = _pick(dff, (1408, 704, 384, 128))
    dh1, st_n2 = _proj_norm_bwd([(da, 0, w_up_f, 0, dff, tkf), (db, 0, w_up_f, dff, dff, tkf)], h1, 0, norm2_g, mx[4], dh2, t,
                                "up_proj_bwd", (512, 256))
    dz_rest, dhs, dout, dpm, dps, st_mix, g_ws, g_bst = _mixer_bwd(dh1, out, hf, hb, z_main, pm, ps, head_norm_g, sgu_ln_g,
                                                                    sgu_ln_b, w_s[0], b_st, wbm_f, wbs_f, wout_f, mx[2], t, nh)
    g_wout = _wgrad(y, dout, t, "wgrad_out")
    g_wbm = _wgrad(ym, dpm, t, "wgrad_branch_mlstm")
    g_wbs = _wgrad(ys, dps, t, "wgrad_branch_sgu")
    ex_a = [gwup_slots, g_wdown.reshape(N_DEV, dff // N_DEV, d), g_wbm.reshape(N_DEV, d // N_DEV, d),
            g_wbs.reshape(N_DEV, d // N_DEV, d), g_wout.reshape(N_DEV, d // N_DEV, d)]
    ex_a_state, ex_a_tok = _exchange_start(ex_a, True, "grad_exchange_a_start")
    dqkv_f, dg_f, dqkv_b, dg_b = _mlstm_bwd(qk, z_main, zg, bias + ex_a_tok[0:1, :], dhs, states_f, states_b, nh, t)
    dz_qkv, g_convqk = _qkv_conv_bwd(dqkv_f, dqkv_b, z_main, convw, t, md, qscale)
    dz_g, st_gate = _gate_grad_sum(dg_f, dg_b)
    gwin_slots = _scatter_cols(
        [_wgrad(hn, dz_qkv, s_rows, "wgrad_in_qkv"), _wgrad(hn, dz_g, s_rows, "wgrad_in_gate"), _wgrad(hn, dz_rest, t, "wgrad_in_rest")],
        [(0, 0, 3 * md, 0), (1, 3 * md, 4 * nh, 0), (2, 3 * md + 4 * nh, 5 * d, 0)], n_insh, "scatter_grad_w_in")
    gcq_slots = _scatter_cols([g_convqk], [(0, 0, 2 * md, 0)], nq, "scatter_grad_conv_qk")
    gcf_slots = _scatter_cols([g_wconv9], [(0, 0, dff, 0)], nf, "scatter_grad_ffn_conv")
    ex_b_state, ex_b_tok = _exchange_start([gwin_slots, gcq_slots, gcf_slots], True, "grad_exchange_b_start")
    tk = _pick(md, (1024, 512, 256))
    grad_x, st_n1x = _proj_norm_bwd(
        [(dz_qkv, 0, w_main, 0, 3 * md, tk), (dz_rest, 0, w_main, 3 * md, 5 * d, tk), (dz_g, 0, w_gate, 0, LANES, LANES)],
        x2, 0, norm1_g, mx[1] + ex_b_tok[0:1, 0:1], dh1, t, "in_proj_bwd")
    (st_n1c,) = _proj_norm_bwd([(dz_qkv, t, w_main, 0, 3 * md, tk), (dz_g, t, w_gate, 0, LANES, LANES)],
                               ctx2, 0, norm1_g, mc[1] + ex_b_tok[0:1, 0:1], None, n_ctx, "in_proj_bwd_ctx")

    rx_a = _exchange_wait(ex_a_state, st_n1c, "grad_exchange_a_wait")
    rx_b = _exchange_wait(ex_b_state, st_n1c, "grad_exchange_b_wait")
    recv = [rx_b[0], rx_a[0], rx_a[2], rx_a[3], rx_a[4], rx_a[1], rx_b[1], rx_b[2]]
    small_parts = [st_n1x[1], st_n1x[2], st_mix[0], st_n2[1], st_n2[2], st_tail[1],
                   st_n1c[1], st_n1c[2],
                   silu_v[0], st_n1x[0] + st_n1c[0], st_gate[0], st_mix[1], st_mix[2], st_mix[3],
                   g_ws.reshape(-1), g_bst[:, :ng].T.reshape(-1), st_n2[0], st_tail[0]]
    gsmall = _pack(small_parts, 8)
    (recv_small,) = _grad_exchange([], [gsmall])
    small_sum = _slot_sum(recv_small).reshape(-1)
    small_slots = recv_small.reshape(N_DEV, -1)
    o_silu, o_n1 = 8 * d, 9 * d
    ncol = N_MOD * d // N_DEV
    dmc_tot = small_sum[6 * d:8 * d][None]
    dmc_pad = jnp.concatenate([dmc_tot, jnp.zeros((1, 4 * d), F32)], axis=1)
    g_wmod, g_bmod, g_cctx = _mod_grads(
        small_slots[:, o_silu:o_silu + d], lax.dynamic_slice_in_dim(small_slots[:, :6 * d], me * ncol, ncol, axis=1),
        small_slots[:, :6 * d], dmc_tot, lax.dynamic_slice_in_dim(dmc_pad, me * ncol, ncol, axis=1), silu_v[1:2], c_ctx[None],
        w_mod_f[:, :2 * d])

    shard_w = (w_in, w_up, w_branch_mlstm, w_branch_sgu, w_out, w_down, conv_qk)
    shard_m = (m_w_in, m_w_up, m_w_branch_mlstm, m_w_branch_sgu, m_w_out, m_w_down, m_conv_qk)
    shard_v = (v_w_in, v_w_up, v_w_branch_mlstm, v_w_branch_sgu, v_w_out, v_w_down, v_conv_qk)
    shard_names = ("w_in", "w_up", "w_branch_mlstm", "w_branch_sgu", "w_out", "w_down", "conv_qk")
    shard_out = [_adamw(wa, recv[k], ma, va, "adamw_" + nm)
                 for k, (wa, ma, va, nm) in enumerate(zip(shard_w, shard_m, shard_v, shard_names))]
    shard_out.append([b.reshape(w_ffn_conv.shape) for b in
                      _adamw(ffn9, recv[7], m_w_ffn_conv[0].reshape(9, nf), v_w_ffn_conv[0].reshape(9, nf), "adamw_w_ffn_conv")])
    mod_out = _adamw(w_mod, g_wmod, m_w_mod, v_w_mod, "adamw_w_mod")

    def rep(cc, bm, n1, bg, hg, lg, lb, ws, bs, n2, fg):
        return [cc.reshape(-1), bm.reshape(-1), n1.reshape(-1), _pad_lanes(bg.reshape(1, -1)).reshape(-1), hg.reshape(-1),
                lg.reshape(-1), lb.reshape(-1), ws.reshape(-1), bs.reshape(-1), n2.reshape(-1), fg.reshape(-1)]

    o = o_n1
    g_rep_parts = [g_cctx, g_bmod]
    for n in (d, LANES, d, d, d, ng * sc * sc, ng * sc, d, d):
        g_rep_parts.append(small_sum[o:o + n])
        o += n
    rep_shapes = [(d,), (1, N_MOD * d), (1, d), (1, LANES), (1, d), (1, d), (1, d), (1, ng, sc, sc), (1, ng, sc), (1, d), (d,)]
    rep_out = _adamw(
        _pack(rep(c_ctx, b_mod, norm1_g, b_gate, head_norm_g, sgu_ln_g, sgu_ln_b, w_s, b_s, norm2_g, final_g), 8),
        _pack(g_rep_parts, 8)[None],
        _pack(rep(m_c_ctx, m_b_mod, m_norm1_g, m_b_gate, m_head_norm_g, m_sgu_ln_g, m_sgu_ln_b, m_w_s, m_b_s, m_norm2_g, m_final_g), 8),
        _pack(rep(v_c_ctx, v_b_mod, v_norm1_g, v_b_gate, v_head_norm_g, v_sgu_ln_g, v_sgu_ln_b, v_w_s, v_b_s, v_norm2_g, v_final_g), 8),
        "adamw_replicated")

    def assemble(k):
        r = _unpack(rep_out[k], rep_shapes)
        s = [o[k] for o in shard_out]
        return [r[0], mod_out[k], r[1], r[2], s[0], r[3][:, :4 * nh], s[6], r[4], r[5], r[6], r[7], r[8], s[2], s[3], s[4], r[9],
                s[1], s[7], s[5], r[10]]

    loss = lax.psum(st_tail[2, 0], ("x", "y", "c"))
    outs = [loss, grad_x[None]]
    for k in range(4):
        outs += assemble(k)
    return tuple(outs)
```

```python
import functools
import math

import jax
import jax.numpy as jnp
from jax import lax
from jax.experimental import pallas as pl
from jax.experimental.pallas import tpu as pltpu

F32, BF16 = jnp.float32, jnp.bfloat16
EPS = 1e-6
M_INIT = -1e30
NEG = -1e30
GRID_W = 64
LCH = 256
N_MOD = 6
N_DEV = 8
LANES = 128
ADAM_LR, ADAM_B1, ADAM_B2, ADAM_EPS, ADAM_WD, ADAM_STEP = 0.001, 0.9, 0.999, 1e-08, 0.01, 10
GELU_C = math.sqrt(2.0 / math.pi)
GELU_A = 0.044715
VMEM_LIMIT = 56 * 1024 * 1024
HI = lax.Precision.HIGHEST
SDS = jax.ShapeDtypeStruct
MESH_ID = pl.DeviceIdType.MESH


def _pick(n, cands):
    for c in cands:
        if n % c == 0:
            return c
    raise ValueError(f"no block size for {n} in {cands}")


def _cp(*sem):
    return pltpu.CompilerParams(dimension_semantics=sem if sem else None, vmem_limit_bytes=VMEM_LIMIT)


def _sigmoid(x):
    return 0.5 * jnp.tanh(0.5 * x) + 0.5


def _split3(x):
    hi = x.astype(BF16)
    r = x - hi.astype(F32)
    mid = r.astype(BF16)
    return hi, mid, (r - mid.astype(F32)).astype(BF16)


def _mask_dot(mask_b, x):
    hi, mid, lo = _split3(x)
    return (_dot(mask_b, lo) + _dot(mask_b, mid)) + _dot(mask_b, hi)


def _mask_dot_t(mask_b, x):
    hi, mid, lo = _split3(x)
    return (_dot_tn(mask_b, lo) + _dot_tn(mask_b, mid)) + _dot_tn(mask_b, hi)


def _gelu(x):
    return x * (0.5 * (1.0 + jnp.tanh(GELU_C * x * (1.0 + GELU_A * (x * x)))))


def _gelu_and_grad(x):
    x2 = x * x
    t = jnp.tanh(GELU_C * x * (1.0 + GELU_A * x2))
    half = 0.5 * (1.0 + t)
    return x * half, half + (0.5 * GELU_C) * x * (1.0 - t * t) * (1.0 + 3.0 * GELU_A * x2)


def _log_sigmoid(x):
    return jnp.minimum(x, 0.0) - jnp.log(1.0 + jnp.exp(-jnp.abs(x)))


def _dot(a, b):
    return jnp.dot(a, b, preferred_element_type=F32)


def _dot_nt(a, b):
    return lax.dot_general(a, b, (((1,), (1,)), ((), ())), preferred_element_type=F32)


def _dot_tn(a, b):
    return lax.dot_general(a, b, (((0,), (0,)), ((), ())), preferred_element_type=F32)


def _bf(x):
    return x.astype(BF16)


def _allgather(arrs):
    na = len(arrs)

    def body(*refs):
        x_refs, o_refs = refs[:na], refs[na:2 * na]
        send_sems, recv_sems, local_sems = refs[2 * na:]
        x, y, c = lax.axis_index("x"), lax.axis_index("y"), lax.axis_index("c")
        me, sibling = (x, y, c), (x, y, 1 - c)
        chips = [(1 - x, y), (x, 1 - y), (1 - x, 1 - y)]

        def copy(a, k, block, to, src=None):
            slot = o_refs[a].at[4 * block[0] + 2 * block[1] + block[2]]
            return pltpu.make_async_remote_copy(
                src_ref=slot if src is None else src, dst_ref=slot, send_sem=send_sems.at[7 * a + k],
                recv_sem=recv_sems.at[7 * a + k], device_id=to, device_id_type=MESH_ID)

        mine = [pltpu.make_async_copy(x_refs[a], o_refs[a].at[4 * x + 2 * y + c], local_sems.at[a]) for a in range(na)]
        for cp in mine:
            cp.start()
        first = []
        for a in range(na):
            first.append(copy(a, 0, me, sibling, src=x_refs[a]))
            first += [copy(a, 1 + j, me, (*chip, c), src=x_refs[a]) for j, chip in enumerate(chips)]
        for cp in first:
            cp.start()
        passed = []
        for j, chip in enumerate(chips):
            for a in range(na):
                copy(a, 1 + j, (*chip, c), me).wait_recv()
                passed.append(copy(a, 4 + j, (*chip, c), sibling))
                passed[-1].start()
        for a in range(na):
            copy(a, 0, sibling, me).wait_recv()
            for j, chip in enumerate(chips):
                copy(a, 4 + j, (*chip, 1 - c), me).wait_recv()
        for cp in first + passed:
            cp.wait_send()
        for cp in mine:
            cp.wait()

    anyspec = pl.BlockSpec(memory_space=pl.ANY)
    return pl.pallas_call(
        body, name="weights_allgather",
        out_shape=[SDS((N_DEV,) + a.shape, a.dtype) for a in arrs],
        in_specs=[anyspec] * na, out_specs=[anyspec] * na,
        scratch_shapes=[pltpu.SemaphoreType.DMA((7 * na,)), pltpu.SemaphoreType.DMA((7 * na,)), pltpu.SemaphoreType.DMA((na,))],
    )(*arrs)


def _grad_exchange(per_dest, shared):
    nd, ns = len(per_dest), len(shared)
    na = nd + ns

    def body(*refs):
        in_refs, out_refs = refs[:na], refs[na:2 * na]
        send_sems, recv_sems, local_sems = refs[2 * na:]
        x, y, c = lax.axis_index("x"), lax.axis_index("y"), lax.axis_index("c")
        me = 4 * x + 2 * y + c

        def src(a, idx):
            return in_refs[a].at[idx] if a < nd else in_refs[a]

        loc = [pltpu.make_async_copy(src(a, me), out_refs[a].at[me], local_sems.at[a]) for a in range(na)]
        for cp in loc:
            cp.start()
        sends, recvs = [], []
        for k in range(1, N_DEV):
            px = 1 - x if k & 4 else x
            py = 1 - y if k & 2 else y
            pc = 1 - c if k & 1 else c
            peer, pidx = (px, py, pc), 4 * px + 2 * py + pc
            for a in range(na):
                sem = 7 * a + k - 1
                sends.append(pltpu.make_async_remote_copy(
                    src_ref=src(a, pidx), dst_ref=out_refs[a].at[me], send_sem=send_sems.at[sem],
                    recv_sem=recv_sems.at[sem], device_id=peer, device_id_type=MESH_ID))
                recvs.append(pltpu.make_async_remote_copy(
                    src_ref=src(a, pidx), dst_ref=out_refs[a].at[pidx], send_sem=send_sems.at[sem],
                    recv_sem=recv_sems.at[sem], device_id=peer, device_id_type=MESH_ID))
        for cp in sends:
            cp.start()
        for cp in recvs:
            cp.wait_recv()
        for cp in sends:
            cp.wait_send()
        for cp in loc:
            cp.wait()

    anyspec = pl.BlockSpec(memory_space=pl.ANY)
    return pl.pallas_call(
        body, name="grad_exchange",
        out_shape=[SDS(a.shape, a.dtype) for a in per_dest] + [SDS((N_DEV,) + a.shape, a.dtype) for a in shared],
        in_specs=[anyspec] * na, out_specs=[anyspec] * na,
        scratch_shapes=[pltpu.SemaphoreType.DMA((7 * na,)), pltpu.SemaphoreType.DMA((7 * na,)), pltpu.SemaphoreType.DMA((na,))],
    )(*per_dest, *shared)


_HBM_SPEC = pl.BlockSpec(memory_space=pltpu.HBM)
_SEM_SPEC = pl.BlockSpec(memory_space=pltpu.SEMAPHORE)
_EFFECT = pltpu.SideEffectType.DATAFLOW_SIDE_EFFECTING


def _peer_list(x, y, c):
    out = []
    for k in range(1, N_DEV):
        px = 1 - x if k & 4 else x
        py = 1 - y if k & 2 else y
        pc = 1 - c if k & 1 else c
        out.append(((px, py, pc), 4 * px + 2 * py + pc))
    return out


def _split_copies(src, land, send_sems, recv_sems, per_dest, receive):
    x, y, c = lax.axis_index("x"), lax.axis_index("y"), lax.axis_index("c")
    me = 4 * x + 2 * y + c
    out = []
    for k, (peer, pidx) in enumerate(_peer_list(x, y, c)):
        for a in range(len(src)):
            out.append(pltpu.make_async_remote_copy(
                src_ref=src[a].at[pidx] if per_dest else src[a], dst_ref=land[a].at[pidx if receive else me],
                send_sem=send_sems.at[7 * a + k], recv_sem=recv_sems.at[7 * a + k], device_id=peer, device_id_type=MESH_ID))
    return out


def _own_copies(src, land, own_sems, per_dest):
    me = 4 * lax.axis_index("x") + 2 * lax.axis_index("y") + lax.axis_index("c")
    return [pltpu.make_async_copy(src[a].at[me] if per_dest else src[a], land[a].at[me], own_sems.at[a]) for a in range(len(src))]


def _exchange_start(arrs, per_dest, name):
    na = len(arrs)
    land_shapes = [a.shape if per_dest else (N_DEV,) + a.shape for a in arrs]
    lands = [pltpu.with_memory_space_constraint(lax.empty(s, a.dtype), pltpu.HBM) for s, a in zip(land_shapes, arrs)]

    def body(*refs):
        src, land = refs[:na], refs[na:2 * na]
        send_sems, recv_sems, own_sems, token = refs[2 * na], refs[2 * na + 1], refs[2 * na + 2], refs[-1]
        for cp in _split_copies(src, land, send_sems, recv_sems, per_dest, False) + _own_copies(src, land, own_sems, per_dest):
            cp.start()
        token[...] = jnp.zeros_like(token)

    outs = pl.pallas_call(
        body, name=name,
        out_shape=[pltpu.SemaphoreType.DMA((7 * na,)), pltpu.SemaphoreType.DMA((7 * na,)), pltpu.SemaphoreType.DMA((na,))]
        + [pltpu.HBM(a.shape, a.dtype) for a in arrs] + [pltpu.HBM(s, a.dtype) for s, a in zip(land_shapes, arrs)]
        + [SDS((8, LANES), F32)],
        in_specs=[_HBM_SPEC] * (2 * na),
        out_specs=[_SEM_SPEC] * 3 + [_HBM_SPEC] * (2 * na) + [pl.BlockSpec(memory_space=pltpu.VMEM)],
        input_output_aliases={k: 3 + k for k in range(2 * na)},
        compiler_params=pltpu.CompilerParams(has_side_effects=_EFFECT),
    )(*[pltpu.with_memory_space_constraint(a, pltpu.HBM) for a in arrs], *lands)
    return (na, per_dest, outs[:-1]), outs[-1]


def _exchange_wait(state, after, name):
    na, per_dest, started = state

    def body(*refs):
        src, land = refs[:na], refs[na:2 * na]
        send_sems, recv_sems, own_sems = refs[2 * na], refs[2 * na + 1], refs[2 * na + 2]
        for cp in _split_copies(src, land, send_sems, recv_sems, per_dest, True):
            cp.wait_send()
            cp.wait_recv()
        for cp in _own_copies(src, land, own_sems, per_dest):
            cp.wait()

    bufs = started[3:]
    outs = pl.pallas_call(
        body, name=name,
        out_shape=[pltpu.HBM(b.shape, b.dtype) for b in bufs],
        in_specs=[_HBM_SPEC] * (2 * na) + [_SEM_SPEC] * 3 + [pl.BlockSpec(memory_space=pl.ANY)],
        out_specs=[_HBM_SPEC] * (2 * na),
        input_output_aliases={k: k for k in range(2 * na)},
        compiler_params=pltpu.CompilerParams(has_side_effects=_EFFECT),
    )(*bufs, started[0], started[1], started[2], after)
    return outs[na:]


def _col_pieces(n, segments):
    out = []
    for j in range(N_DEV):
        lo, hi = j * n, (j + 1) * n
        for (k, s0, w, c0) in segments:
            a, b = max(lo, s0), min(hi, s0 + w)
            if a < b:
                out.append((j, a - lo, b - lo, k, c0 + a - s0, c0 + b - s0))
    return out


def _assemble_cols(slots, groups, out_widths, name):
    r, p = slots.shape[1], slots.shape[2]
    tb = _pick(r, (128, 64, 32, 16, 8))
    covered = [0] * len(out_widths)
    for (_, n, segs) in groups:
        for (k, _, w, _) in segs:
            covered[k] += w

    def body(s_ref, *o_refs):
        for k, wd in enumerate(out_widths):
            if covered[k] < wd:
                o_refs[k][...] = jnp.zeros_like(o_refs[k])
        for (off, n, segs) in groups:
            for (j, a0, a1, k, d0, d1) in _col_pieces(n, segs):
                o_refs[k][:, d0:d1] = s_ref[j, :, off + a0:off + a1]

    return pl.pallas_call(
        body, name=name, grid=(r // tb,), in_specs=[pl.BlockSpec((N_DEV, tb, p), lambda i: (0, i, 0))],
        out_specs=[pl.BlockSpec((tb, w), lambda i: (i, 0)) for w in out_widths],
        out_shape=[SDS((r, w), slots.dtype) for w in out_widths], compiler_params=_cp("arbitrary"))(slots)


def _scatter_cols(pieces, segments, n, name):
    r = pieces[0].shape[0]
    tb = _pick(r, (128, 64, 32, 16, 8))

    def body(*refs):
        p_refs, o_ref = refs[:-1], refs[-1]
        for (j, a0, a1, k, d0, d1) in _col_pieces(n, segments):
            o_ref[j, :, a0:a1] = p_refs[k][:, d0:d1]

    return pl.pallas_call(
        body, name=name, grid=(r // tb,), in_specs=[pl.BlockSpec((tb, a.shape[1]), lambda i: (i, 0)) for a in pieces],
        out_specs=pl.BlockSpec((N_DEV, tb, n), lambda i: (0, i, 0)), out_shape=SDS((N_DEV, r, n), pieces[0].dtype),
        compiler_params=_cp("arbitrary"))(*pieces)


def _modulation(cvec, w_mod, b_mod):
    d, n = w_mod.shape

    def body(c_ref, w_ref, b_ref, s_ref, o_ref):
        cv = c_ref[...]
        s = cv * _sigmoid(cv)
        s_ref[...] = s
        o_ref[...] = _dot(_bf(s), w_ref[...]) + b_ref[...]

    return pl.pallas_call(body, name="modulation", out_shape=(SDS((8, d), F32), SDS((8, n), F32)),
                          compiler_params=_cp())(cvec, w_mod, b_mod)


def _norm_mod_proj(x_arr, g, shsc, w_main, w_gate, rows_total, row0, filled, name):
    m_rows, d = x_arr.shape
    n = w_main.shape[1]
    tb = _pick(m_rows, (1024, 256))
    cb = _pick(n, (1408, 1024, 768, 512, 384, 256, 128))
    gate = w_gate is not None
    nout = 3 if gate else 2
    nin = 5 if gate else 4
    rb = row0 // tb

    def body(*refs):
        x_ref, g_ref, ss_ref, wm_ref = refs[:4]
        wg_ref = refs[4] if gate else None
        outs = refs[len(refs) - 1 - nout:len(refs) - 1]
        hn_ref, z_ref = outs[0], outs[1]
        hn_sc = refs[-1]

        @pl.when(pl.program_id(1) == 0)
        def _():
            x = x_ref[...]
            r = lax.rsqrt(jnp.mean(x * x, axis=-1, keepdims=True) + EPS)
            hb = _bf((x * r * g_ref[...]) * (1.0 + ss_ref[1:2, :]) + ss_ref[0:1, :])
            hn_sc[...] = hb
            hn_ref[...] = hb
            if gate:
                outs[2][...] = _dot(hb, wg_ref[...])

        z_ref[...] = _bf(_dot(hn_sc[...], wm_ref[...]))

    in_specs = [pl.BlockSpec((tb, d), lambda i, j: (i, 0)), pl.BlockSpec((1, d), lambda i, j: (0, 0)),
                pl.BlockSpec((2, d), lambda i, j: (0, 0)), pl.BlockSpec((d, cb), lambda i, j: (0, j))]
    out_specs = [pl.BlockSpec((tb, d), lambda i, j: (rb + i, 0)), pl.BlockSpec((tb, cb), lambda i, j: (rb + i, j))]
    out_shape = [SDS((rows_total, d), BF16), SDS((rows_total, n), BF16)]
    args = [x_arr, g, shsc, w_main]
    if gate:
        in_specs.append(pl.BlockSpec((d, LANES), lambda i, j: (0, 0)))
        out_specs.append(pl.BlockSpec((tb, LANES), lambda i, j: (rb + i, 0)))
        out_shape.append(SDS((rows_total, LANES), F32))
        args.append(w_gate)
    aliases = {}
    if filled is not None:
        in_specs += [pl.BlockSpec(memory_space=pl.ANY)] * nout
        args += list(filled)
        aliases = {nin + k: k for k in range(nout)}
    return pl.pallas_call(
        body, name=name, grid=(m_rows // tb, n // cb), in_specs=in_specs, out_specs=out_specs, out_shape=out_shape,
        input_output_aliases=aliases, scratch_shapes=[pltpu.VMEM((tb, d), BF16)],
        compiler_params=_cp("arbitrary", "arbitrary"))(*args)


def _seg_masks(row, t_rows, s_rows):
    prev_ok = (row != 0) & (row != t_rows)
    next_ok = (row != t_rows - 1) & (row != s_rows - 1)
    return prev_ok, next_ok


def _shift_rows(z, halo_prev, halo_next, tb):
    loc = lax.broadcasted_iota(jnp.int32, (tb, 1), 0)
    zp = jnp.where(loc == 0, halo_prev, pltpu.roll(z, 1, 0))
    zn = jnp.where(loc == tb - 1, halo_next, pltpu.roll(z, tb - 1, 0))
    return zp, zn


def _qk_conv(z_main, conv_w, t_rows, md, qscale):
    s_rows = z_main.shape[0]
    tb = _pick(s_rows, (1280, 1024, 256))
    cb = _pick(md, (512, 256, 128))
    nb8 = tb // 8

    def body(zm, zp, zn, w_ref, o_ref):
        i, j = pl.program_id(0), pl.program_id(1)
        z = zm[...].astype(F32)
        zprev, znext = _shift_rows(z, zp[7:8, :].astype(F32), zn[0:1, :].astype(F32), tb)
        row = i * tb + lax.broadcasted_iota(jnp.int32, (tb, 1), 0)
        prev_ok, next_ok = _seg_masks(row, t_rows, s_rows)
        pre = (w_ref[0:1, :] * jnp.where(prev_ok, zprev, 0.0) + w_ref[1:2, :] * z
               + w_ref[2:3, :] * jnp.where(next_ok, znext, 0.0))
        scale = jnp.where(j * cb < md, qscale, 1.0)
        o_ref[...] = _bf(pre * _sigmoid(pre) * scale)

    return pl.pallas_call(
        body, name="qk_conv", grid=(s_rows // tb, 2 * md // cb),
        in_specs=[pl.BlockSpec((tb, cb), lambda i, j: (i, j)),
                  pl.BlockSpec((8, cb), lambda i, j: (jnp.maximum(i * nb8 - 1, 0), j)),
                  pl.BlockSpec((8, cb), lambda i, j: (jnp.minimum((i + 1) * nb8, s_rows // 8 - 1), j)),
                  pl.BlockSpec((8, cb), lambda i, j: (0, j))],
        out_specs=pl.BlockSpec((tb, cb), lambda i, j: (i, j)),
        out_shape=SDS((s_rows, 2 * md), BF16), compiler_params=_cp("arbitrary", "arbitrary"))(z_main, z_main, z_main, conv_w)


def _chunk_gates(gates, bias, rev):
    ln = gates.shape[0]
    gz = gates + bias
    logf = _log_sigmoid(gz)
    r_id = lax.broadcasted_iota(jnp.int32, (ln, ln), 0)
    c_id = lax.broadcasted_iota(jnp.int32, (ln, ln), 1)
    mask = (c_id >= r_id) if rev else (c_id <= r_id)
    mb = mask.astype(F32).astype(BF16)
    b_all = _mask_dot(mb, logf)
    g_all = jnp.sum(logf, axis=0, keepdims=True)
    return gz, b_all, b_all.T, gz.T, g_all, mask, mb


def _head_weights(b_col, b_row, i_row, m_in, mask):
    d = jnp.where(mask, b_col - b_row + i_row, NEG)
    inter = b_col + m_in
    m_row = jnp.maximum(inter, jnp.max(d, axis=1, keepdims=True))
    return jnp.exp(d - m_row), jnp.exp(inter - m_row), m_row


def _head_state_coeffs(g, b_col, i_col, m_in):
    a = g - b_col + i_col
    m_new = jnp.maximum(g + m_in, jnp.max(a, axis=0, keepdims=True))
    return jnp.exp(g + m_in - m_new), jnp.exp(a - m_new), m_new


def _mlstm_fwd(qk, z_main, zg, bias, nh):
    s_rows = qk.shape[0]
    md = qk.shape[1] // 2
    dh = md // nh
    nc = s_rows // LCH
    ln = LCH

    def chunk_f(i):
        return jnp.where(i == 0, nc - 1, i - 1)

    def chunk_b(i):
        return jnp.where(i == 0, nc - 1, nc - 1 - i)

    def body(qf, kf, vf, gf, qb, kb, vb, gb, bias_ref, hf_ref, hb_ref, cf_ref, nf_ref, mf_ref, cb_ref, nb_ref, mb_ref,
             c_sc, n_sc, m_sc):
        i = pl.program_id(0)

        @pl.when(i == 0)
        def _():
            c_sc[...] = jnp.zeros_like(c_sc)
            n_sc[...] = jnp.zeros_like(n_sc)
            m_sc[...] = jnp.full(m_sc.shape, M_INIT, F32)

        for dr, (q_ref, k_ref, v_ref, g_ref, h_ref, c_out, n_out, m_out) in enumerate(
                ((qf, kf, vf, gf, hf_ref, cf_ref, nf_ref, mf_ref), (qb, kb, vb, gb, hb_ref, cb_ref, nb_ref, mb_ref))):
            gz, b_all, b_t, g_t, g_all, mask, _ = _chunk_gates(g_ref[...], bias_ref[...], dr == 1)
            for h in range(nh):
                ci, cf = 2 * dr * nh + h, (2 * dr + 1) * nh + h
                sl = slice(h * dh, (h + 1) * dh)
                q, k, v = q_ref[:, sl], k_ref[:, sl], v_ref[:, sl]
                c_in, n_in, m_in = c_sc[dr, h], n_sc[dr, h, 0:1, :], m_sc[dr, h, 0:1, 0:1]
                c_out[sl, :] = c_in
                n_out[:, sl] = n_sc[dr, h]
                m_out[h] = m_sc[dr, h]
                b_col, b_row, i_col, i_row = b_all[:, cf:cf + 1], b_t[cf:cf + 1, :], gz[:, ci:ci + 1], g_t[ci:ci + 1, :]
                g = g_all[:, cf:cf + 1]
                w, w_int, m_row = _head_weights(b_col, b_row, i_row, m_in, mask)
                s_mat = _dot_nt(q, k) * w
                num = _dot(_bf(s_mat), v) + w_int * _dot(q, _bf(c_in))
                den = jnp.sum(s_mat, axis=1, keepdims=True) + w_int * jnp.sum(q.astype(F32) * n_in, axis=1, keepdims=True)
                h_ref[:, sl] = _bf(num / jnp.maximum(jnp.abs(den), jnp.exp(-m_row)))
                a_old, coef, m_new = _head_state_coeffs(g, b_col, i_col, m_in)
                kw = k.astype(F32) * coef
                c_sc[dr, h] = a_old * c_in + _dot_tn(_bf(kw), v)
                n_sc[dr, h] = jnp.broadcast_to(a_old * n_in + jnp.sum(kw, axis=0, keepdims=True), (8, dh))
                m_sc[dr, h] = jnp.broadcast_to(m_new, (8, LANES))

    def tok(cfn, col):
        return pl.BlockSpec((ln, md), lambda i: (cfn(i), col))

    def gat(cfn):
        return pl.BlockSpec((ln, LANES), lambda i: (cfn(i), 0))

    def st(cfn, shape):
        return pl.BlockSpec((None,) + shape, lambda i: (cfn(i),) + (0,) * len(shape))

    st_shapes = ((nh * dh, dh), (8, md), (nh, 8, LANES))
    return pl.pallas_call(
        body, name="mlstm_fwd", grid=(nc,),
        in_specs=[tok(chunk_f, 0), tok(chunk_f, 1), tok(chunk_f, 2), gat(chunk_f),
                  tok(chunk_b, 0), tok(chunk_b, 1), tok(chunk_b, 2), gat(chunk_b),
                  pl.BlockSpec((1, LANES), lambda i: (0, 0))],
        out_specs=[tok(chunk_f, 0), tok(chunk_b, 0)] + [st(chunk_f, s) for s in st_shapes] + [st(chunk_b, s) for s in st_shapes],
        out_shape=[SDS((s_rows, md), BF16)] * 2 + [SDS((nc,) + s, F32) for s in st_shapes] * 2,
        scratch_shapes=[pltpu.VMEM((2, nh, dh, dh), F32), pltpu.VMEM((2, nh, 8, dh), F32), pltpu.VMEM((2, nh, 8, LANES), F32)],
        compiler_params=_cp("arbitrary"))(qk, qk, z_main, zg, qk, qk, z_main, zg, bias)


def _head_rms(hs, nh, dh):
    parts, scales = [], []
    for h in range(nh):
        hh = hs[:, h * dh:(h + 1) * dh]
        r = lax.rsqrt(jnp.mean(hh * hh, axis=-1, keepdims=True) + EPS)
        parts.append(hh * r)
        scales.append(r)
    return jnp.concatenate(parts, axis=1), scales


def _layer_norm(v):
    vc = v - jnp.mean(v, axis=-1, keepdims=True)
    r = lax.rsqrt(jnp.mean(vc * vc, axis=-1, keepdims=True) + EPS)
    return vc * r, r


def _sgu_mix(vnb, ws_ref, bs_ref, tb, ng, gd, sc):
    rows = []
    for ch in range(tb // sc):
        cols = []
        for g in range(ng):
            blk = vnb[ch * sc:(ch + 1) * sc, g * gd:(g + 1) * gd]
            cols.append(_dot(_bf(ws_ref[g]), blk) + bs_ref[:, g:g + 1])
        rows.append(jnp.concatenate(cols, axis=1))
    return jnp.concatenate(rows, axis=0)


def _mixer_fwd(hf, hb, z_main, xs, hg, lng, lnb, w_s, b_st, wbm, wbs, wout, mx2, t_rows, nh):
    d = xs.shape[1]
    ng, sc = w_s.shape[0], w_s.shape[1]
    dh, gd = d // nh, d // ng
    tb = _pick(t_rows, (256,))

    def body(hf_ref, hb_ref, zo, zu, zv, zgm, zgg, x_ref, hg_ref, lng_ref, lnb_ref, ws_ref, bs_ref, wbm_ref, wbs_ref,
             wo_ref, mx2_ref, h1_ref, ym_ref, ys_ref, pm_ref, ps_ref, y_ref, out_ref):
        hs = hf_ref[...].astype(F32) + hb_ref[...].astype(F32)
        hn, _ = _head_rms(hs, nh, dh)
        ym = _bf(_sigmoid(zo[...].astype(F32)) * (hn * hg_ref[...]))
        ym_ref[...] = ym
        vhat, _ = _layer_norm(_gelu(zv[...].astype(F32)))
        vnb = _bf(vhat * lng_ref[...] + lnb_ref[...])
        ys = _bf(_gelu(zu[...].astype(F32)) * _sgu_mix(vnb, ws_ref, bs_ref, tb, ng, gd, sc))
        ys_ref[...] = ys
        pm = _dot(ym, wbm_ref[...])
        ps = _dot(ys, wbs_ref[...])
        pm_ref[...] = _bf(pm)
        ps_ref[...] = _bf(ps)
        y = _bf(_sigmoid(zgm[...].astype(F32)) * pm + _sigmoid(zgg[...].astype(F32)) * ps)
        y_ref[...] = y
        out = _dot(y, wo_ref[...])
        out_ref[...] = _bf(out)
        h1_ref[...] = x_ref[...] + mx2_ref[...] * out

    def tok(col):
        return pl.BlockSpec((tb, d), lambda i: (i, col))

    def full(shape):
        return pl.BlockSpec(shape, lambda i: (0,) * len(shape))

    return pl.pallas_call(
        body, name="mixer_fwd", grid=(t_rows // tb,),
        in_specs=[tok(0), tok(0), tok(3), tok(4), tok(5), tok(6), tok(7), tok(0), full((1, d)), full((1, d)), full((1, d)),
                  full((ng, sc, sc)), full((sc, LANES)), full((d, d)), full((d, d)), full((d, d)), full((1, d))],
        out_specs=[tok(0)] * 7,
        out_shape=[SDS((t_rows, d), F32)] + [SDS((t_rows, d), BF16)] * 6,
        compiler_params=_cp("arbitrary"))(hf, hb, z_main, z_main, z_main, z_main, z_main, xs, hg, lng, lnb, w_s, b_st,
                                          wbm, wbs, wout, mx2)


def _grid_taps(a_ext, n_ext):
    col = lax.broadcasted_iota(jnp.int32, (n_ext, 1), 0) % GRID_W
    left = jnp.where(col != 0, pltpu.roll(a_ext, 1, 0), 0.0)
    right = jnp.where(col != GRID_W - 1, pltpu.roll(a_ext, n_ext - 1, 0), 0.0)
    return left, right


def _with_halo(prev, main, nxt, i, ni, tb):
    ext = jnp.concatenate([prev, main, nxt], axis=0).astype(F32)
    pos = lax.broadcasted_iota(jnp.int32, (tb + 2 * GRID_W, 1), 0)
    inside = ((pos >= GRID_W) | (i > 0)) & ((pos < tb + GRID_W) | (i < ni - 1))
    return jnp.where(inside, ext, 0.0)


def _halo_specs(tb, cb, t_rows, col0=0):
    nh64 = tb // GRID_W
    return [pl.BlockSpec((tb, cb), lambda i, j: (i, col0 + j)),
            pl.BlockSpec((GRID_W, cb), lambda i, j: (jnp.maximum(i * nh64 - 1, 0), col0 + j)),
            pl.BlockSpec((GRID_W, cb), lambda i, j: (jnp.minimum((i + 1) * nh64, t_rows // GRID_W - 1), col0 + j))]


def _ffn_tail(ab, w_conv9, w_down, h1, mx5, gfin, target, dff):
    t_rows, d = h1.shape
    tb = _pick(t_rows, (256,))
    cb = _pick(dff, (1408, 256, 128))
    ni, nj = t_rows // tb, dff // cb
    n_ext = tb + 2 * GRID_W

    def body(am, ap, an, b_ref, wc_ref, wd_ref, h1_ref, mx5_ref, gf_ref, tg_ref, ac_ref, f_ref, dh2_ref, dffn_ref, st_ref, acc):
        i, j = pl.program_id(0), pl.program_id(1)
        a_ext = _with_halo(ap[...], am[...], an[...], i, ni, tb)
        left, right = _grid_taps(a_ext, n_ext)
        conv = jnp.zeros((tb, cb), F32)
        for di in range(3):
            o = di * GRID_W
            conv = conv + (wc_ref[3 * di:3 * di + 1, :] * left[o:o + tb] + wc_ref[3 * di + 1:3 * di + 2, :] * a_ext[o:o + tb]
                           + wc_ref[3 * di + 2:3 * di + 3, :] * right[o:o + tb])
        ac_ref[...] = _bf(conv)
        fb = _bf(conv * _sigmoid(conv) * b_ref[...].astype(F32))
        f_ref[...] = fb

        @pl.when(j == 0)
        def _():
            acc[...] = jnp.zeros_like(acc)

        @pl.when((i == 0) & (j == 0))
        def _():
            st_ref[...] = jnp.zeros_like(st_ref)

        acc[...] += _dot(fb, wd_ref[...])

        @pl.when(j == nj - 1)
        def _():
            ffn = acc[...]
            h2 = h1_ref[...] + mx5_ref[...] * ffn
            r = lax.rsqrt(jnp.mean(h2 * h2, axis=-1, keepdims=True) + EPS)
            xn = h2 * r
            e = xn * gf_ref[...] - tg_ref[...]
            loss = 0.5 * jnp.sum(jnp.sum(e * e, axis=1, keepdims=True), axis=0, keepdims=True) / d
            dy = e * (1.0 / d)
            dxn = dy * gf_ref[...]
            dh2 = r * (dxn - xn * jnp.mean(dxn * xn, axis=-1, keepdims=True))
            dh2_ref[...] = dh2
            dffn_ref[...] = _bf(dh2 * mx5_ref[...])
            st_ref[...] += jnp.concatenate(
                [jnp.sum(dy * xn, axis=0, keepdims=True), jnp.sum(dh2 * ffn, axis=0, keepdims=True),
                 jnp.broadcast_to(loss, (1, d)), jnp.zeros((5, d), F32)], axis=0)

    def tokd():
        return pl.BlockSpec((tb, d), lambda i, j: (i, 0))

    def rowd():
        return pl.BlockSpec((1, d), lambda i, j: (0, 0))

    return pl.pallas_call(
        body, name="ffn_tail", grid=(ni, nj),
        in_specs=_halo_specs(tb, cb, t_rows) + [pl.BlockSpec((tb, cb), lambda i, j: (i, nj + j)),
                                                pl.BlockSpec((16, cb), lambda i, j: (0, j)),
                                                pl.BlockSpec((cb, d), lambda i, j: (j, 0)), tokd(), rowd(), rowd(), tokd()],
        out_specs=[pl.BlockSpec((tb, cb), lambda i, j: (i, j)), pl.BlockSpec((tb, cb), lambda i, j: (i, j)), tokd(), tokd(),
                   pl.BlockSpec((8, d), lambda i, j: (0, 0))],
        out_shape=[SDS((t_rows, dff), BF16), SDS((t_rows, dff), BF16), SDS((t_rows, d), F32), SDS((t_rows, d), BF16),
                   SDS((8, d), F32)],
        scratch_shapes=[pltpu.VMEM((tb, d), F32)],
        compiler_params=_cp("arbitrary", "arbitrary"))(ab, ab, ab, ab, w_conv9, w_down, h1, mx5, gfin, target)


def _ffn_bwd_gate(dffn, w_down, aconv, ab, dff):
    t_rows, d = dffn.shape
    tb = _pick(t_rows, (512,))
    cb = _pick(dff, (1408, 256, 128))
    nj = dff // cb

    def body(g_ref, wd_ref, ac_ref, b_ref, db_ref, dac_ref):
        df = _dot_nt(g_ref[...], wd_ref[...])
        ac = ac_ref[...].astype(F32)
        sa = _sigmoid(ac)
        db_ref[...] = _bf(df * ac * sa)
        dac_ref[...] = _bf(df * b_ref[...].astype(F32) * (sa * (1.0 + ac * (1.0 - sa))))

    blk = pl.BlockSpec((tb, cb), lambda i, j: (i, j))
    return pl.pallas_call(
        body, name="ffn_bwd_gate", grid=(t_rows // tb, nj),
        in_specs=[pl.BlockSpec((tb, d), lambda i, j: (i, 0)), pl.BlockSpec((cb, d), lambda i, j: (j, 0)), blk,
                  pl.BlockSpec((tb, cb), lambda i, j: (i, nj + j))],
        out_specs=[blk, blk], out_shape=[SDS((t_rows, dff), BF16)] * 2,
        compiler_params=_cp("arbitrary", "arbitrary"))(dffn, w_down, aconv, ab)


def _ffn_conv_bwd(dac, ab, w_conv9, dff):
    t_rows = dac.shape[0]
    tb = _pick(t_rows, (256,))
    cb = _pick(dff, (1408, 256, 128))
    ni, nj = t_rows // tb, dff // cb
    n_ext = tb + 2 * GRID_W
    nh64 = tb // GRID_W

    def body(dm, dp, dn, am, ap, an, wc_ref, da_ref, gw_ref):
        i = pl.program_id(1)
        d_ext = _with_halo(dp[...], dm[...], dn[...], i, ni, tb)
        a_ext = _with_halo(ap[...], am[...], an[...], i, ni, tb)
        d_left, d_right = _grid_taps(d_ext, n_ext)
        a_left, a_right = _grid_taps(a_ext, n_ext)
        dmain = d_ext[GRID_W:GRID_W + tb]
        da = jnp.zeros((tb, cb), F32)
        rows = []
        for di in range(3):
            o = (2 - di) * GRID_W
            da = da + (wc_ref[3 * di:3 * di + 1, :] * d_right[o:o + tb] + wc_ref[3 * di + 1:3 * di + 2, :] * d_ext[o:o + tb]
                       + wc_ref[3 * di + 2:3 * di + 3, :] * d_left[o:o + tb])
            o = di * GRID_W
            for tap in (a_left, a_ext, a_right):
                rows.append(jnp.sum(dmain * tap[o:o + tb], axis=0, keepdims=True))
        da_ref[...] = _bf(da)

        @pl.when(i == 0)
        def _():
            gw_ref[...] = jnp.zeros_like(gw_ref)

        gw_ref[...] += jnp.concatenate(rows + [jnp.zeros((7, cb), F32)], axis=0)

    def halo(col0):
        return [pl.BlockSpec((tb, cb), lambda j, i: (i, col0 + j)),
                pl.BlockSpec((GRID_W, cb), lambda j, i: (jnp.maximum(i * nh64 - 1, 0), col0 + j)),
                pl.BlockSpec((GRID_W, cb), lambda j, i: (jnp.minimum((i + 1) * nh64, t_rows // GRID_W - 1), col0 + j))]

    return pl.pallas_call(
        body, name="ffn_conv_bwd", grid=(nj, ni),
        in_specs=halo(0) + halo(0) + [pl.BlockSpec((16, cb), lambda j, i: (0, j))],
        out_specs=[pl.BlockSpec((tb, cb), lambda j, i: (i, j)), pl.BlockSpec((16, cb), lambda j, i: (0, j))],
        out_shape=[SDS((t_rows, dff), BF16), SDS((16, dff), F32)],
        compiler_params=_cp("arbitrary", "arbitrary"))(dac, dac, dac, ab, ab, ab, w_conv9)


def _proj_norm_bwd(pairs, x_arr, x_row0, g, scale, resid, m_rows, name, row_blocks=(1024, 256)):
    d = x_arr.shape[1]
    tm = _pick(m_rows, row_blocks)
    te = 256
    ni = m_rows // tm
    starts, total = [], 0
    for (_, _, _, _, k_p, tk_p) in pairs:
        starts.append(total)
        total += k_p // tk_p
    npairs = len(pairs)
    has_dx = resid is not None

    def body(*refs):
        a_refs, b_refs = refs[0:2 * npairs:2], refs[1:2 * npairs:2]
        rest = refs[2 * npairs:]
        if has_dx:
            x_ref, g_ref, sc_ref, r_ref, dx_ref, st_ref, acc = rest
        else:
            x_ref, g_ref, sc_ref, st_ref, acc = rest
        i, k = pl.program_id(0), pl.program_id(1)

        @pl.when(k == 0)
        def _():
            acc[...] = jnp.zeros_like(acc)

        @pl.when((i == 0) & (k == 0))
        def _():
            st_ref[...] = jnp.zeros_like(st_ref)

        for p in range(npairs):
            nk = pairs[p][4] // pairs[p][5]

            @pl.when((k >= starts[p]) & (k < starts[p] + nk))
            def _(p=p):
                acc[...] += _dot_nt(a_refs[p][...], b_refs[p][...])

        @pl.when(k == total - 1)
        def _():
            sums = [jnp.zeros((1, d), F32)] * 3
            for r0 in range(0, tm, te):
                rows = slice(r0, r0 + te)
                dhn = acc[rows, :]
                x = x_ref[rows, :]
                r = lax.rsqrt(jnp.mean(x * x, axis=-1, keepdims=True) + EPS)
                xn = x * r
                dmod = dhn * (1.0 + sc_ref[...])
                dxn = dmod * g_ref[...]
                if has_dx:
                    dx_ref[rows, :] = r * (dxn - xn * jnp.mean(dxn * xn, axis=-1, keepdims=True)) + r_ref[rows, :]
                sums = [sums[0] + jnp.sum(dmod * xn, axis=0, keepdims=True), sums[1] + jnp.sum(dhn, axis=0, keepdims=True),
                        sums[2] + jnp.sum(dhn * (xn * g_ref[...]), axis=0, keepdims=True)]
            st_ref[...] += jnp.concatenate(sums + [jnp.zeros((5, d), F32)], axis=0)

    in_specs, args = [], []
    for p, (a, a_row0, b, b_col0, k_p, tk_p) in enumerate(pairs):
        nk, s0, ar, bc = k_p // tk_p, starts[p], a_row0 // tm, b_col0 // tk_p

        def kk(k, s0=s0, nk=nk):
            return jnp.clip(k - s0, 0, nk - 1)

        in_specs.append(pl.BlockSpec((tm, tk_p), lambda i, k, ar=ar, kk=kk: (ar + i, kk(k))))
        in_specs.append(pl.BlockSpec((d, tk_p), lambda i, k, bc=bc, kk=kk: (0, bc + kk(k))))
        args += [a, b]
    xr = x_row0 // tm
    in_specs += [pl.BlockSpec((tm, d), lambda i, k: (xr + i, 0)), pl.BlockSpec((1, d), lambda i, k: (0, 0)),
                 pl.BlockSpec((1, d), lambda i, k: (0, 0))]
    args += [x_arr, g, scale]
    out_specs, out_shape = [], []
    if has_dx:
        in_specs.append(pl.BlockSpec((tm, d), lambda i, k: (i, 0)))
        args.append(resid)
        out_specs.append(pl.BlockSpec((tm, d), lambda i, k: (i, 0)))
        out_shape.append(SDS((m_rows, d), F32))
    out_specs.append(pl.BlockSpec((8, d), lambda i, k: (0, 0)))
    out_shape.append(SDS((8, d), F32))
    return pl.pallas_call(
        body, name=name, grid=(ni, total), in_specs=in_specs, out_specs=out_specs, out_shape=out_shape,
        scratch_shapes=[pltpu.VMEM((tm, d), F32)], compiler_params=_cp("arbitrary", "arbitrary"))(*args)


def _wgrad(a, b, k_rows, name):
    m, n = a.shape[1], b.shape[1]
    tm = _pick(m, (1408, 1024, 512, 384, 256, 128))
    tn = _pick(n, (1408, 1024, 768, 512, 384, 256, 128))
    tk = _pick(k_rows, (1280, 1024, 256))
    nk = k_rows // tk

    def body(a_ref, b_ref, o_ref, acc):
        k = pl.program_id(2)

        @pl.when(k == 0)
        def _():
            acc[...] = jnp.zeros_like(acc)

        acc[...] += _dot_tn(a_ref[...], b_ref[...])

        @pl.when(k == nk - 1)
        def _():
            o_ref[...] = _bf(acc[...])

    return pl.pallas_call(
        body, name=name, grid=(m // tm, n // tn, nk),
        in_specs=[pl.BlockSpec((tk, tm), lambda i, j, k: (k, i)), pl.BlockSpec((tk, tn), lambda i, j, k: (k, j))],
        out_specs=pl.BlockSpec((tm, tn), lambda i, j, k: (i, j)), out_shape=SDS((m, n), BF16),
        scratch_shapes=[pltpu.VMEM((tm, tn), F32)],
        compiler_params=_cp("arbitrary", "arbitrary", "arbitrary"))(a, b)


def _lane_put(col, lane_idx):
    lane = lax.broadcasted_iota(jnp.int32, (1, LANES), 1)
    return jnp.where(lane == lane_idx, col, 0.0)


def _mixer_bwd(dh1, out, hf, hb, z_main, pm, ps, hg, lng, lnb, w_s, b_st, wbm, wbs, wout, mx2, t_rows, nh):
    d = dh1.shape[1]
    ng, sc = w_s.shape[0], w_s.shape[1]
    dh, gd = d // nh, d // ng
    tb = _pick(t_rows, (256,))

    def body(dh1_ref, out_ref, hf_ref, hb_ref, zo, zu, zv, zgm, zgg, pm_ref, ps_ref, hg_ref, lng_ref, lnb_ref, ws_ref, bs_ref,
             wbm_ref, wbs_ref, wo_ref, mx2_ref, dz_ref, dhs_ref, dout_ref, dpm_ref, dps_ref, st_ref, dws_ref, dbs_ref):
        i = pl.program_id(0)

        @pl.when(i == 0)
        def _():
            st_ref[...] = jnp.zeros_like(st_ref)
            dws_ref[...] = jnp.zeros_like(dws_ref)
            dbs_ref[...] = jnp.zeros_like(dbs_ref)

        dh1v = dh1_ref[...]
        doutb = _bf(dh1v * mx2_ref[...])
        dout_ref[...] = doutb
        d_mx2 = jnp.sum(dh1v * out_ref[...].astype(F32), axis=0, keepdims=True)
        dy = _dot_nt(doutb, wo_ref[...])
        sgm, sgg = _sigmoid(zgm[...].astype(F32)), _sigmoid(zgg[...].astype(F32))
        dpmb, dpsb = _bf(dy * sgm), _bf(dy * sgg)
        dpm_ref[...] = dpmb
        dps_ref[...] = dpsb
        dz_ref[:, 3 * d:4 * d] = _bf(dy * pm_ref[...].astype(F32) * sgm * (1.0 - sgm))
        dz_ref[:, 4 * d:5 * d] = _bf(dy * ps_ref[...].astype(F32) * sgg * (1.0 - sgg))
        dym = _dot_nt(dpmb, wbm_ref[...])
        dys = _dot_nt(dpsb, wbs_ref[...])
        hs = hf_ref[...].astype(F32) + hb_ref[...].astype(F32)
        hn, scales = _head_rms(hs, nh, dh)
        so = _sigmoid(zo[...].astype(F32))
        dz_ref[:, 0:d] = _bf(dym * (hn * hg_ref[...]) * so * (1.0 - so))
        dhmn = dym * so
        d_hg = jnp.sum(dhmn * hn, axis=0, keepdims=True)
        dhn = dhmn * hg_ref[...]
        for h in range(nh):
            sl = slice(h * dh, (h + 1) * dh)
            dhs_ref[:, sl] = _bf(scales[h] * (dhn[:, sl] - hn[:, sl] * jnp.mean(dhn[:, sl] * hn[:, sl], axis=-1, keepdims=True)))
        zuv, zvv = zu[...].astype(F32), zv[...].astype(F32)
        u, du_dz = _gelu_and_grad(zuv)
        vg, dvg_dz = _gelu_and_grad(zvv)
        vhat, rstd = _layer_norm(vg)
        vnb = _bf(vhat * lng_ref[...] + lnb_ref[...])
        mixed = _sgu_mix(vnb, ws_ref, bs_ref, tb, ng, gd, sc)
        dz_ref[:, d:2 * d] = _bf(dys * mixed * du_dz)
        dmix = dys * u
        rows = []
        dbs = jnp.zeros((sc, LANES), F32)
        for ch in range(tb // sc):
            cols = []
            for g in range(ng):
                dm = dmix[ch * sc:(ch + 1) * sc, g * gd:(g + 1) * gd]
                dmb = _bf(dm)
                dws_ref[g] += _dot_nt(dmb, vnb[ch * sc:(ch + 1) * sc, g * gd:(g + 1) * gd])
                dbs = dbs + _lane_put(jnp.sum(dm, axis=1, keepdims=True), g)
                cols.append(_dot_tn(_bf(ws_ref[g]), dmb))
            rows.append(jnp.concatenate(cols, axis=1))
        dbs_ref[...] += dbs
        dvn = jnp.concatenate(rows, axis=0)
        d_lng = jnp.sum(dvn * vhat, axis=0, keepdims=True)
        d_lnb = jnp.sum(dvn, axis=0, keepdims=True)
        dvh = dvn * lng_ref[...]
        dvg = rstd * (dvh - jnp.mean(dvh, axis=-1, keepdims=True) - vhat * jnp.mean(dvh * vhat, axis=-1, keepdims=True))
        dz_ref[:, 2 * d:3 * d] = _bf(dvg * dvg_dz)
        st_ref[...] += jnp.concatenate([d_mx2, d_hg, d_lng, d_lnb, jnp.zeros((4, d), F32)], axis=0)

    def tok(col):
        return pl.BlockSpec((tb, d), lambda i: (i, col))

    def full(shape):
        return pl.BlockSpec(shape, lambda i: (0,) * len(shape))

    return pl.pallas_call(
        body, name="mixer_bwd", grid=(t_rows // tb,),
        in_specs=[tok(0), tok(0), tok(0), tok(0), tok(3), tok(4), tok(5), tok(6), tok(7), tok(0), tok(0), full((1, d)),
                  full((1, d)), full((1, d)), full((ng, sc, sc)), full((sc, LANES)), full((d, d)), full((d, d)), full((d, d)),
                  full((1, d))],
        out_specs=[pl.BlockSpec((tb, 5 * d), lambda i: (i, 0)), tok(0), tok(0), tok(0), tok(0), full((8, d)), full((ng, sc, sc)),
                   full((sc, LANES))],
        out_shape=[SDS((t_rows, 5 * d), BF16)] + [SDS((t_rows, d), BF16)] * 4 + [SDS((8, d), F32), SDS((ng, sc, sc), F32),
                                                                                SDS((sc, LANES), F32)],
        compiler_params=_cp("arbitrary"))(dh1, out, hf, hb, z_main, z_main, z_main, z_main, z_main, pm, ps, hg, lng, lnb, w_s,
                                          b_st, wbm, wbs, wout, mx2)


def _mlstm_bwd(qk, z_main, zg, bias, dhs, hf, hb, states_f, states_b, nh, t_rows):
    s_rows = qk.shape[0]
    md = qk.shape[1] // 2
    dh = md // nh
    nc = s_rows // LCH
    nx = t_rows // LCH
    ln = LCH

    def chunk_f(i):
        return jnp.where(i == nc - 1, nc - 1, nc - 2 - i)

    def chunk_b(i):
        return jnp.where(i == nc - 1, nc - 1, i)

    def body(qf, kf, vf, gf, dhf, hsf, cf, nf, mf_, qb, kb, vb, gb, dhb, hsb, cb, nb, mb_, bias_ref, dqkvf_ref, dgf_ref, dqkvb_ref,
             dgb_ref, dc_sc, dn_sc):
        i = pl.program_id(0)
        is_ctx = i == nc - 1

        @pl.when(i == 0)
        def _():
            dc_sc[...] = jnp.zeros_like(dc_sc)
            dn_sc[...] = jnp.zeros_like(dn_sc)

        for dr, (q_ref, k_ref, v_ref, g_ref, dh_ref, hs_ref, c_ref, n_ref, m_ref, dqkv_ref, dg_ref) in enumerate(
                ((qf, kf, vf, gf, dhf, hsf, cf, nf, mf_, dqkvf_ref, dgf_ref),
                 (qb, kb, vb, gb, dhb, hsb, cb, nb, mb_, dqkvb_ref, dgb_ref))):
            gz, b_all, b_t, g_t, g_all, mask, mfl = _chunk_gates(g_ref[...], bias_ref[...], dr == 1)
            x1 = jnp.zeros((ln, LANES), F32)
            x2 = jnp.zeros((ln, LANES), F32)
            dig = jnp.zeros((ln, LANES), F32)
            e_row = jnp.zeros((1, LANES), F32)
            for h in range(nh):
                ci, cfl = 2 * dr * nh + h, (2 * dr + 1) * nh + h
                sl = slice(h * dh, (h + 1) * dh)
                q, k, v = q_ref[:, sl], k_ref[:, sl], v_ref[:, sl]
                qf32, kf32 = q.astype(F32), k.astype(F32)
                dhv = jnp.where(is_ctx, 0.0, dh_ref[:, sl].astype(F32))
                c_in, n_in, m_in = c_ref[sl, :], n_ref[0:1, sl], m_ref[h, 0:1, 0:1]
                b_col, b_row, i_col, i_row = b_all[:, cfl:cfl + 1], b_t[cfl:cfl + 1, :], gz[:, ci:ci + 1], g_t[ci:ci + 1, :]
                g = g_all[:, cfl:cfl + 1]
                w, w_int, m_row = _head_weights(b_col, b_row, i_row, m_in, mask)
                s_mat = _dot_nt(q, k) * w
                sb, cb16 = _bf(s_mat), _bf(c_in)
                den = jnp.sum(s_mat, axis=1, keepdims=True) + w_int * jnp.sum(qf32 * n_in, axis=1, keepdims=True)
                e_m = jnp.exp(-m_row)
                dnm = jnp.maximum(jnp.abs(den), e_m)
                dnum = dhv / dnm
                hdh = jnp.sum(hs_ref[:, sl].astype(F32) * dhv, axis=1, keepdims=True)
                dden = jnp.where(jnp.abs(den) > e_m, -(hdh / dnm) * jnp.sign(den), 0.0)
                dnum_b = _bf(dnum)
                ds = _dot_nt(dnum_b, v) + dden
                pb = _bf(w * ds)
                gmat = s_mat * ds
                a_old, coef, _ = _head_state_coeffs(g, b_col, i_col, m_in)
                dc_new, dn_new = dc_sc[dr, h], dn_sc[dr, h, 0:1, :]
                dcb = _bf(dc_new)
                dv = _dot_tn(sb, dnum_b) + _dot(_bf(kf32 * coef), dcb)
                dq_inter = w_int * (_dot_nt(dnum_b, cb16) + dden * n_in)
                dq = _dot(pb, k) + dq_inter
                dk_state = coef * (_dot_nt(v, dcb) + dn_new)
                dk = _dot_tn(pb, q) + dk_state
                dqkv_ref[:, sl] = _bf(dq)
                dqkv_ref[:, md + h * dh:md + (h + 1) * dh] = _bf(dk)
                dqkv_ref[:, 2 * md + h * dh:2 * md + (h + 1) * dh] = _bf(dv)
                row_intra = jnp.sum(gmat, axis=1, keepdims=True)
                col_intra = jnp.sum(gmat.T, axis=1, keepdims=True)
                row_inter = jnp.sum(qf32 * dq_inter, axis=1, keepdims=True)
                col_inter = jnp.sum(kf32 * dk_state, axis=1, keepdims=True)
                e_old = a_old * (jnp.sum(jnp.sum(dc_new * c_in, axis=1, keepdims=True), axis=0, keepdims=True)
                                 + jnp.sum(dn_new * n_in, axis=1, keepdims=True))
                x1 = x1 + _lane_put(row_intra - col_intra + row_inter, cfl)
                x2 = x2 + _lane_put(col_inter, cfl)
                e_row = e_row + _lane_put(e_old, cfl)
                dig = dig + _lane_put(col_intra + col_inter, ci)
                dc_sc[dr, h] = a_old * dc_new + _dot_tn(_bf(qf32 * w_int), dnum_b)
                dn_sc[dr, h] = jnp.broadcast_to(a_old * dn_new + jnp.sum(qf32 * (w_int * dden), axis=0, keepdims=True), (8, dh))
            dlogf = _mask_dot_t(mfl, x1) + _mask_dot(mfl, x2) - x2 + e_row
            dg_ref[...] = dig + dlogf / (1.0 + jnp.exp(gz))

    def tok(cfn, col):
        return pl.BlockSpec((ln, md), lambda i: (cfn(i), col))

    def dht(cfn):
        return pl.BlockSpec((ln, md), lambda i: (jnp.minimum(cfn(i), nx - 1), 0))

    def gat(cfn):
        return pl.BlockSpec((ln, LANES), lambda i: (cfn(i), 0))

    def st(cfn, shape):
        return pl.BlockSpec((None,) + shape, lambda i: (cfn(i),) + (0,) * len(shape))

    st_shapes = ((nh * dh, dh), (8, md), (nh, 8, LANES))

    def side(cfn):
        return [tok(cfn, 0), tok(cfn, 1), tok(cfn, 2), gat(cfn), dht(cfn), tok(cfn, 0)] + [st(cfn, s) for s in st_shapes]

    def outs(cfn):
        return [pl.BlockSpec((ln, 3 * md), lambda i: (cfn(i), 0)), gat(cfn)]

    return pl.pallas_call(
        body, name="mlstm_bwd", grid=(nc,),
        in_specs=side(chunk_f) + side(chunk_b) + [pl.BlockSpec((1, LANES), lambda i: (0, 0))],
        out_specs=outs(chunk_f) + outs(chunk_b),
        out_shape=[SDS((s_rows, 3 * md), BF16), SDS((s_rows, LANES), F32)] * 2,
        scratch_shapes=[pltpu.VMEM((2, nh, dh, dh), F32), pltpu.VMEM((2, nh, 8, dh), F32)],
        compiler_params=_cp("arbitrary"))(qk, qk, z_main, zg, dhs, hf, *states_f, qk, qk, z_main, zg, dhs, hb, *states_b, bias)


def _qkv_conv_bwd(dqkv_f, dqkv_b, z_main, conv_w, t_rows, md, qscale):
    s_rows = z_main.shape[0]
    tb = _pick(s_rows, (1280, 1024, 256))
    cb = _pick(md, (512, 256, 128))
    ni, nj, ncq = s_rows // tb, 3 * md // cb, 2 * md // cb
    nb8 = tb // 8
    n_ext = tb + 16

    def body(fm, fp, fn, bm, bp, bn, zm, zp, zn, w_ref, dz_ref, gw_ref):
        j, i = pl.program_id(0), pl.program_id(1)

        @pl.when(j < ncq)
        def _():
            z = jnp.concatenate([zp[...], zm[...], zn[...]], axis=0).astype(F32)
            dqk = (jnp.concatenate([fp[...], fm[...], fn[...]], axis=0).astype(F32)
                   + jnp.concatenate([bp[...], bm[...], bn[...]], axis=0).astype(F32)) * jnp.where(j * cb < md, qscale, 1.0)
            row = i * tb - 8 + lax.broadcasted_iota(jnp.int32, (n_ext, 1), 0)
            prev_ok, next_ok = _seg_masks(row, t_rows, s_rows)
            zprev = jnp.where(prev_ok, pltpu.roll(z, 1, 0), 0.0)
            znext = jnp.where(next_ok, pltpu.roll(z, n_ext - 1, 0), 0.0)
            pre = w_ref[0:1, :] * zprev + w_ref[1:2, :] * z + w_ref[2:3, :] * znext
            sg = _sigmoid(pre)
            dpre = dqk * (sg * (1.0 + pre * (1.0 - sg)))
            dz = (w_ref[1:2, :] * dpre + w_ref[0:1, :] * jnp.where(next_ok, pltpu.roll(dpre, n_ext - 1, 0), 0.0)
                  + w_ref[2:3, :] * jnp.where(prev_ok, pltpu.roll(dpre, 1, 0), 0.0))
            dz_ref[...] = _bf(dz[8:8 + tb])
            dm = dpre[8:8 + tb]

            @pl.when(i == 0)
            def _():
                gw_ref[...] = jnp.zeros_like(gw_ref)

            gw_ref[...] += jnp.concatenate(
                [jnp.sum(dm * zprev[8:8 + tb], axis=0, keepdims=True), jnp.sum(dm * z[8:8 + tb], axis=0, keepdims=True),
                 jnp.sum(dm * znext[8:8 + tb], axis=0, keepdims=True), jnp.zeros((5, cb), F32)], axis=0)

        @pl.when(j >= ncq)
        def _():
            dz_ref[...] = _bf(fm[...].astype(F32) + bm[...].astype(F32))

    def halo(clampj):
        def cj(j):
            return jnp.minimum(j, ncq - 1) if clampj else j
        return [pl.BlockSpec((tb, cb), lambda j, i: (i, cj(j))),
                pl.BlockSpec((8, cb), lambda j, i: (jnp.maximum(i * nb8 - 1, 0), cj(j))),
                pl.BlockSpec((8, cb), lambda j, i: (jnp.minimum((i + 1) * nb8, s_rows // 8 - 1), cj(j)))]

    return pl.pallas_call(
        body, name="qkv_conv_bwd", grid=(nj, ni),
        in_specs=halo(False) + halo(False) + halo(True) + [pl.BlockSpec((8, cb), lambda j, i: (0, jnp.minimum(j, ncq - 1)))],
        out_specs=[pl.BlockSpec((tb, cb), lambda j, i: (i, j)), pl.BlockSpec((8, cb), lambda j, i: (0, jnp.minimum(j, ncq - 1)))],
        out_shape=[SDS((s_rows, 3 * md), BF16), SDS((8, 2 * md), F32)],
        compiler_params=_cp("arbitrary", "arbitrary"))(dqkv_f, dqkv_f, dqkv_f, dqkv_b, dqkv_b, dqkv_b, z_main, z_main, z_main, conv_w)


def _gate_grad_sum(dg_f, dg_b):
    s_rows = dg_f.shape[0]
    tb = _pick(s_rows, (1280, 1024, 256))

    def body(a_ref, b_ref, o_ref, st_ref):
        @pl.when(pl.program_id(0) == 0)
        def _():
            st_ref[...] = jnp.zeros_like(st_ref)

        s = a_ref[...] + b_ref[...]
        o_ref[...] = _bf(s)
        st_ref[...] += jnp.concatenate([jnp.sum(s, axis=0, keepdims=True), jnp.zeros((7, LANES), F32)], axis=0)

    blk = pl.BlockSpec((tb, LANES), lambda i: (i, 0))
    return pl.pallas_call(
        body, name="gate_grad_sum", grid=(s_rows // tb,), in_specs=[blk, blk],
        out_specs=[blk, pl.BlockSpec((8, LANES), lambda i: (0, 0))],
        out_shape=[SDS((s_rows, LANES), BF16), SDS((8, LANES), F32)], compiler_params=_cp("arbitrary"))(dg_f, dg_b)


def _mod_grads(silu_slots, dmx_sh, dmx_slots, dmc_tot, dmc_sh, silu_cctx, c_ctx, w_mod_c):
    d = silu_slots.shape[1]
    ncol, n6 = dmx_sh.shape[1], dmx_slots.shape[1]

    def body(ss_ref, dsh_ref, dsl_ref, dct_ref, dcs_ref, sc_ref, c_ref, w_ref, gw_ref, gb_ref, gc_ref):
        a = jnp.concatenate([ss_ref[...], sc_ref[...], jnp.zeros((7, d), F32)], axis=0)
        b = jnp.concatenate([dsh_ref[...], dcs_ref[...], jnp.zeros((7, ncol), F32)], axis=0)
        gw_ref[0] = lax.dot_general(a, b, (((0,), (0,)), ((), ())), preferred_element_type=F32, precision=HI)
        dct = dct_ref[...]
        gb_ref[...] = jnp.sum(dsl_ref[...], axis=0, keepdims=True) + jnp.concatenate(
            [dct, jnp.zeros((1, n6 - dct.shape[1]), F32)], axis=1)
        t = _dot_nt(_bf(jnp.broadcast_to(dct, (8, dct.shape[1]))), w_ref[...])
        cv = c_ref[...]
        s = _sigmoid(cv)
        gc_ref[...] = t[0:1, :] * (s * (1.0 + cv * (1.0 - s)))

    return pl.pallas_call(body, name="mod_grads", out_shape=[SDS((1, d, ncol), F32), SDS((1, n6), F32), SDS((1, d), F32)],
                          compiler_params=_cp())(silu_slots, dmx_sh, dmx_slots, dmc_tot, dmc_sh, silu_cctx, c_ctx, w_mod_c)


def _slot_sum(slots):
    ns, r = slots.shape[0], slots.shape[1]
    tb = _pick(r, (1024, 512, 256, 128, 64, 32, 16, 8))

    def body(s_ref, o_ref):
        acc = s_ref[0]
        for k in range(1, ns):
            acc = acc + s_ref[k]
        o_ref[...] = acc

    return pl.pallas_call(
        body, name="slot_sum", grid=(r // tb,), in_specs=[pl.BlockSpec((ns, tb, LANES), lambda i: (0, i, 0))],
        out_specs=pl.BlockSpec((tb, LANES), lambda i: (i, 0)), out_shape=SDS((r, LANES), F32),
        compiler_params=_cp("arbitrary"))(slots)


def _adamw(w, gslots, m, v, name):
    lead = ((None,), (0,)) if w.ndim == 3 else ((), ())
    r, cdim = w.shape[-2:]
    ns, rg = gslots.shape[0], gslots.shape[1]
    tb = r if (rg != r or r % 8) else _pick(r, (128, 64, 32, 16, 8))
    bc1, bc2 = 1.0 - ADAM_B1 ** ADAM_STEP, 1.0 - ADAM_B2 ** ADAM_STEP

    def body(w_ref, g_ref, m_ref, v_ref, go_ref, d_ref, mo_ref, vo_ref):
        g = g_ref[0, 0:tb, :].astype(F32)
        for k in range(1, ns):
            g = g + g_ref[k, 0:tb, :].astype(F32)
        mn = ADAM_B1 * m_ref[...] + (1.0 - ADAM_B1) * g
        vn = ADAM_B2 * v_ref[...] + (1.0 - ADAM_B2) * (g * g)
        go_ref[...] = g
        mo_ref[...] = mn
        vo_ref[...] = vn
        d_ref[...] = -ADAM_LR * ((mn / bc1) / (jnp.sqrt(vn / bc2) + ADAM_EPS) + ADAM_WD * w_ref[...])

    blk = pl.BlockSpec(lead[0] + (tb, cdim), lambda i: lead[1] + (i, 0))
    gblk = pl.BlockSpec((ns, tb if rg == r else rg, cdim), lambda i: (0, i, 0))
    return pl.pallas_call(
        body, name=name, grid=(r // tb,), in_specs=[blk, gblk, blk, blk],
        out_specs=[blk] * 4, out_shape=[SDS(w.shape, F32)] * 4, compiler_params=_cp("arbitrary"))(w, gslots, m, v)


def _pack(parts, row_mult):
    flat = jnp.concatenate([p.reshape(-1) for p in parts])
    n = flat.shape[0]
    rows = -(-n // LANES)
    rows = -(-rows // row_mult) * row_mult
    return jnp.pad(flat, (0, rows * LANES - n)).reshape(rows, LANES)


def _unpack(buf, shapes):
    flat = buf.reshape(-1)
    out, off = [], 0
    for s in shapes:
        n = math.prod(s)
        out.append(flat[off:off + n].reshape(s))
        off += n
    return out


def _pad_cols(a, width):
    return jnp.pad(a, ((0, 0), (0, width - a.shape[1])))


def _pad_lanes(a):
    return _pad_cols(a, LANES)


def _up128(n):
    return -(-n // LANES) * LANES


def kernel(x, c, ctx, c_ctx, w_mod, b_mod, norm1_g, w_in, b_gate, conv_qk, head_norm_g, sgu_ln_g, sgu_ln_b, w_s, b_s, w_branch_mlstm, w_branch_sgu, w_out, norm2_g, w_up, w_ffn_conv, w_down, final_g, loss_target, m_c_ctx, m_w_mod, m_b_mod, m_norm1_g, m_w_in, m_b_gate, m_conv_qk, m_head_norm_g, m_sgu_ln_g, m_sgu_ln_b, m_w_s, m_b_s, m_w_branch_mlstm, m_w_branch_sgu, m_w_out, m_norm2_g, m_w_up, m_w_ffn_conv, m_w_down, m_final_g, v_c_ctx, v_w_mod, v_b_mod, v_norm1_g, v_w_in, v_b_gate, v_conv_qk, v_head_norm_g, v_sgu_ln_g, v_sgu_ln_b, v_w_s, v_b_s, v_w_branch_mlstm, v_w_branch_sgu, v_w_out, v_norm2_g, v_w_up, v_w_ffn_conv, v_w_down, v_final_g):
    t, d = x.shape[1], x.shape[2]
    n_ctx = ctx.shape[1]
    s_rows = t + n_ctx
    nh = b_gate.shape[1] // 4
    md = head_norm_g.shape[1]
    dh = md // nh
    ng, sc = w_s.shape[1], w_s.shape[2]
    dff = w_down.shape[1] * N_DEV
    n_in = w_in.shape[2] * N_DEV
    assert md == d and sgu_ln_g.shape[1] == d and n_ctx == LCH and t % LCH == 0 and t % (8 * GRID_W) == 0
    assert n_in == 8 * d + 4 * nh and 4 * nh <= LANES
    me = 4 * lax.axis_index("x") + 2 * lax.axis_index("y") + lax.axis_index("c")

    n_mod, n_insh, n_upsh = w_mod.shape[2], w_in.shape[2], w_up.shape[2]
    p_mod, p_in, p_up = _up128(n_mod), _up128(n_insh), _up128(n_upsh)
    nq, nf = conv_qk.shape[2], w_ffn_conv.shape[3]
    ffn9 = w_ffn_conv[0].reshape(9, nf)
    colpack = jnp.concatenate([_pad_cols(_bf(w_mod[0]), p_mod), _pad_cols(_bf(w_in[0]), p_in)], axis=1)
    convpack = jnp.concatenate([jnp.pad(conv_qk[0], ((0, 13), (0, 0))), jnp.pad(ffn9, ((0, 7), (0, 0)))], axis=1)
    g_col, g_conv = _allgather([colpack, convpack])
    w_mod_f, w_main, w_gate = _assemble_cols(
        g_col, [(0, n_mod, [(0, 0, N_DEV * n_mod, 0)]),
                (p_mod, n_insh, [(1, 0, 3 * md, 0), (2, 3 * md, 4 * nh, 0), (1, 3 * md + 4 * nh, 5 * d, 3 * md)])],
        [N_MOD * d, 8 * d, LANES], "assemble_weights")
    convw, wconv9 = _assemble_cols(g_conv, [(0, nq, [(0, 0, N_DEV * nq, 0)]), (nq, nf, [(1, 0, N_DEV * nf, 0)])],
                                   [N_DEV * nq, N_DEV * nf], "assemble_conv_weights")
    zero = jnp.minimum(jnp.abs(g_conv[0, 0, 0]), 0.0)
    late_w = [_pad_cols(_bf(w_up[0] + zero), p_up), _bf(w_branch_mlstm[0]), _bf(w_branch_sgu[0]), _bf(w_out[0]), _bf(w_down[0])]
    late_state, late_tok = _exchange_start(late_w, False, "late_weights_start")

    cvec = jnp.concatenate([c, c_ctx[None], jnp.zeros((6, d), F32)], axis=0) + late_tok[0:1, 0:1]
    silu_v, mod = _modulation(cvec, w_mod_f, b_mod)
    mx = [mod[0:1, k * d:(k + 1) * d] for k in range(N_MOD)]
    mc = [mod[1:2, k * d:(k + 1) * d] for k in range(2)]
    x2, ctx2 = x[0], ctx[0]
    in_x = _norm_mod_proj(x2, norm1_g, jnp.concatenate([mx[0], mx[1]], axis=0), w_main, w_gate, s_rows, 0, None, "in_proj")
    hn, z_main, zg = _norm_mod_proj(ctx2, norm1_g, jnp.concatenate([mc[0], mc[1]], axis=0), w_main, w_gate, s_rows, t, in_x,
                                    "in_proj_ctx")
    qscale = dh ** -0.5
    qk = _qk_conv(z_main, convw, t, md, qscale)
    bias = _pad_lanes(b_gate)
    fwd = _mlstm_fwd(qk, z_main, zg, bias, nh)
    hf, hb, states_f, states_b = fwd[0], fwd[1], fwd[2:5], fwd[5:8]
    g_up, g_bm, g_bs, g_out, g_down = _exchange_wait(late_state, fwd[4], "late_weights_wait")
    (w_up_f,) = _assemble_cols(g_up, [(0, n_upsh, [(0, 0, 2 * dff, 0)])], [2 * dff], "assemble_w_up")
    wbm_f, wbs_f, wout_f = (g.reshape(d, d) for g in (g_bm, g_bs, g_out))
    w_down_f = g_down.reshape(dff, d)
    b_st = _pad_lanes(b_s[0].T)
    h1, ym, ys, pm, ps, y, out = _mixer_fwd(hf, hb, z_main, x2, head_norm_g, sgu_ln_g, sgu_ln_b, w_s[0], b_st, wbm_f, wbs_f,
                                            wout_f, mx[2], t, nh)
    hn2, ab = _norm_mod_proj(h1, norm2_g, jnp.concatenate([mx[3], mx[4]], axis=0), w_up_f, None, t, 0, None, "up_proj")
    aconv, f, dh2, dffn, st_tail = _ffn_tail(ab, wconv9, w_down_f, h1, mx[5], final_g[None], loss_target[0], dff)

    db, dac = _ffn_bwd_gate(dffn, w_down_f, aconv, ab, dff)
    da, g_wconv9 = _ffn_conv_bwd(dac, ab, wconv9, dff)
    g_wdown = _wgrad(f, dffn, t, "wgrad_down")
    gwup_slots = _scatter_cols([_wgrad(hn2, da, t, "wgrad_up_a"), _wgrad(hn2, db, t, "wgrad_up_b")],
                               [(0, 0, dff, 0), (1, dff, dff, 0)], n_upsh, "scatter_grad_w_up")
    tkf = _pick(dff, (1408, 704, 384, 128))
    dh1, st_n2 = _proj_norm_bwd([(da, 0, w_up_f, 0, dff, tkf), (db, 0, w_up_f, dff, dff, tkf)], h1, 0, norm2_g, mx[4], dh2, t,
                                "up_proj_bwd", (512, 256))
    dz_rest, dhs, dout, dpm, dps, st_mix, g_ws, g_bst = _mixer_bwd(dh1, out, hf, hb, z_main, pm, ps, head_norm_g, sgu_ln_g,
                                                                    sgu_ln_b, w_s[0], b_st, wbm_f, wbs_f, wout_f, mx[2], t, nh)
    g_wout = _wgrad(y, dout, t, "wgrad_out")
    g_wbm = _wgrad(ym, dpm, t, "wgrad_branch_mlstm")
    g_wbs = _wgrad(ys, dps, t, "wgrad_branch_sgu")
    ex_a = [gwup_slots, g_wdown.reshape(N_DEV, dff // N_DEV, d), g_wbm.reshape(N_DEV, d // N_DEV, d),
            g_wbs.reshape(N_DEV, d // N_DEV, d), g_wout.reshape(N_DEV, d // N_DEV, d)]
    ex_a_state, ex_a_tok = _exchange_start(ex_a, True, "grad_exchange_a_start")
    dqkv_f, dg_f, dqkv_b, dg_b = _mlstm_bwd(qk, z_main, zg, bias + ex_a_tok[0:1, :], dhs, hf, hb, states_f, states_b, nh, t)
    dz_qkv, g_convqk = _qkv_conv_bwd(dqkv_f, dqkv_b, z_main, convw, t, md, qscale)
    dz_g, st_gate = _gate_grad_sum(dg_f, dg_b)
    gwin_slots = _scatter_cols(
        [_wgrad(hn, dz_qkv, s_rows, "wgrad_in_qkv"), _wgrad(hn, dz_g, s_rows, "wgrad_in_gate"), _wgrad(hn, dz_rest, t, "wgrad_in_rest")],
        [(0, 0, 3 * md, 0), (1, 3 * md, 4 * nh, 0), (2, 3 * md + 4 * nh, 5 * d, 0)], n_insh, "scatter_grad_w_in")
    gcq_slots = _scatter_cols([g_convqk], [(0, 0, 2 * md, 0)], nq, "scatter_grad_conv_qk")
    gcf_slots = _scatter_cols([g_wconv9], [(0, 0, dff, 0)], nf, "scatter_grad_ffn_conv")
    ex_b_state, ex_b_tok = _exchange_start([gwin_slots, gcq_slots, gcf_slots], True, "grad_exchange_b_start")
    tk = _pick(md, (1024, 512, 256))
    grad_x, st_n1x = _proj_norm_bwd(
        [(dz_qkv, 0, w_main, 0, 3 * md, tk), (dz_rest, 0, w_main, 3 * md, 5 * d, tk), (dz_g, 0, w_gate, 0, LANES, LANES)],
        x2, 0, norm1_g, mx[1] + ex_b_tok[0:1, 0:1], dh1, t, "in_proj_bwd")
    (st_n1c,) = _proj_norm_bwd([(dz_qkv, t, w_main, 0, 3 * md, tk), (dz_g, t, w_gate, 0, LANES, LANES)],
                               ctx2, 0, norm1_g, mc[1] + ex_b_tok[0:1, 0:1], None, n_ctx, "in_proj_bwd_ctx")

    rx_a = _exchange_wait(ex_a_state, st_n1c, "grad_exchange_a_wait")
    rx_b = _exchange_wait(ex_b_state, st_n1c, "grad_exchange_b_wait")
    recv = [rx_b[0], rx_a[0], rx_a[2], rx_a[3], rx_a[4], rx_a[1], rx_b[1], rx_b[2]]
    small_parts = [st_n1x[1], st_n1x[2], st_mix[0], st_n2[1], st_n2[2], st_tail[1],
                   st_n1c[1], st_n1c[2],
                   silu_v[0], st_n1x[0] + st_n1c[0], st_gate[0], st_mix[1], st_mix[2], st_mix[3],
                   g_ws.reshape(-1), g_bst[:, :ng].T.reshape(-1), st_n2[0], st_tail[0]]
    gsmall = _pack(small_parts, 8)
    (recv_small,) = _grad_exchange([], [gsmall])
    small_sum = _slot_sum(recv_small).reshape(-1)
    small_slots = recv_small.reshape(N_DEV, -1)
    o_silu, o_n1 = 8 * d, 9 * d
    ncol = N_MOD * d // N_DEV
    dmc_tot = small_sum[6 * d:8 * d][None]
    dmc_pad = jnp.concatenate([dmc_tot, jnp.zeros((1, 4 * d), F32)], axis=1)
    g_wmod, g_bmod, g_cctx = _mod_grads(
        small_slots[:, o_silu:o_silu + d], lax.dynamic_slice_in_dim(small_slots[:, :6 * d], me * ncol, ncol, axis=1),
        small_slots[:, :6 * d], dmc_tot, lax.dynamic_slice_in_dim(dmc_pad, me * ncol, ncol, axis=1), silu_v[1:2], c_ctx[None],
        w_mod_f[:, :2 * d])

    shard_w = (w_in, w_up, w_branch_mlstm, w_branch_sgu, w_out, w_down, conv_qk)
    shard_m = (m_w_in, m_w_up, m_w_branch_mlstm, m_w_branch_sgu, m_w_out, m_w_down, m_conv_qk)
    shard_v = (v_w_in, v_w_up, v_w_branch_mlstm, v_w_branch_sgu, v_w_out, v_w_down, v_conv_qk)
    shard_names = ("w_in", "w_up", "w_branch_mlstm", "w_branch_sgu", "w_out", "w_down", "conv_qk")
    shard_out = [_adamw(wa, recv[k], ma, va, "adamw_" + nm)
                 for k, (wa, ma, va, nm) in enumerate(zip(shard_w, shard_m, shard_v, shard_names))]
    shard_out.append([b.reshape(w_ffn_conv.shape) for b in
                      _adamw(ffn9, recv[7], m_w_ffn_conv[0].reshape(9, nf), v_w_ffn_conv[0].reshape(9, nf), "adamw_w_ffn_conv")])
    mod_out = _adamw(w_mod, g_wmod, m_w_mod, v_w_mod, "adamw_w_mod")

    def rep(cc, bm, n1, bg, hg, lg, lb, ws, bs, n2, fg):
        return [cc.reshape(-1), bm.reshape(-1), n1.reshape(-1), _pad_lanes(bg.reshape(1, -1)).reshape(-1), hg.reshape(-1),
                lg.reshape(-1), lb.reshape(-1), ws.reshape(-1), bs.reshape(-1), n2.reshape(-1), fg.reshape(-1)]

    o = o_n1
    g_rep_parts = [g_cctx, g_bmod]
    for n in (d, LANES, d, d, d, ng * sc * sc, ng * sc, d, d):
        g_rep_parts.append(small_sum[o:o + n])
        o += n
    rep_shapes = [(d,), (1, N_MOD * d), (1, d), (1, LANES), (1, d), (1, d), (1, d), (1, ng, sc, sc), (1, ng, sc), (1, d), (d,)]
    rep_out = _adamw(
        _pack(rep(c_ctx, b_mod, norm1_g, b_gate, head_norm_g, sgu_ln_g, sgu_ln_b, w_s, b_s, norm2_g, final_g), 8),
        _pack(g_rep_parts, 8)[None],
        _pack(rep(m_c_ctx, m_b_mod, m_norm1_g, m_b_gate, m_head_norm_g, m_sgu_ln_g, m_sgu_ln_b, m_w_s, m_b_s, m_norm2_g, m_final_g), 8),
        _pack(rep(v_c_ctx, v_b_mod, v_norm1_g, v_b_gate, v_head_norm_g, v_sgu_ln_g, v_sgu_ln_b, v_w_s, v_b_s, v_norm2_g, v_final_g), 8),
        "adamw_replicated")

    def assemble(k):
        r = _unpack(rep_out[k], rep_shapes)
        s = [o[k] for o in shard_out]
        return [r[0], mod_out[k], r[1], r[2], s[0], r[3][:, :4 * nh], s[6], r[4], r[5], r[6], r[7], r[8], s[2], s[3], s[4], r[9],
                s[1], s[7], s[5], r[10]]

    loss = lax.psum(st_tail[2, 0], ("x", "y", "c"))
    outs = [loss, grad_x[None]]
    for k in range(4):
        outs += assemble(k)
    return tuple(outs)
```

```python
import functools
import math

import jax
import jax.numpy as jnp
from jax import lax
from jax.experimental import pallas as pl
from jax.experimental.pallas import tpu as pltpu

F32, BF16 = jnp.float32, jnp.bfloat16
EPS = 1e-6
M_INIT = -1e30
NEG = -1e30
GRID_W = 64
LCH = 256
N_MOD = 6
N_DEV = 8
LANES = 128
ADAM_LR, ADAM_B1, ADAM_B2, ADAM_EPS, ADAM_WD, ADAM_STEP = 0.001, 0.9, 0.999, 1e-08, 0.01, 10
GELU_C = math.sqrt(2.0 / math.pi)
GELU_A = 0.044715
VMEM_LIMIT = 56 * 1024 * 1024
HI = lax.Precision.HIGHEST
SDS = jax.ShapeDtypeStruct
MESH_ID = pl.DeviceIdType.MESH


def _pick(n, cands):
    for c in cands:
        if n % c == 0:
            return c
    raise ValueError(f"no block size for {n} in {cands}")


def _cp(*sem):
    return pltpu.CompilerParams(dimension_semantics=sem if sem else None, vmem_limit_bytes=VMEM_LIMIT)


def _sigmoid(x):
    return 0.5 * jnp.tanh(0.5 * x) + 0.5


def _split3(x):
    hi = x.astype(BF16)
    r = x - hi.astype(F32)
    mid = r.astype(BF16)
    return hi, mid, (r - mid.astype(F32)).astype(BF16)


def _mask_dot(mask_b, x):
    hi, mid, lo = _split3(x)
    return (_dot(mask_b, lo) + _dot(mask_b, mid)) + _dot(mask_b, hi)


def _mask_dot_t(mask_b, x):
    hi, mid, lo = _split3(x)
    return (_dot_tn(mask_b, lo) + _dot_tn(mask_b, mid)) + _dot_tn(mask_b, hi)


def _gelu(x):
    return x * (0.5 * (1.0 + jnp.tanh(GELU_C * x * (1.0 + GELU_A * (x * x)))))


def _gelu_and_grad(x):
    x2 = x * x
    t = jnp.tanh(GELU_C * x * (1.0 + GELU_A * x2))
    half = 0.5 * (1.0 + t)
    return x * half, half + (0.5 * GELU_C) * x * (1.0 - t * t) * (1.0 + 3.0 * GELU_A * x2)


def _log_sigmoid(x):
    return jnp.minimum(x, 0.0) - jnp.log(1.0 + jnp.exp(-jnp.abs(x)))


def _dot(a, b):
    return jnp.dot(a, b, preferred_element_type=F32)


def _dot_nt(a, b):
    return lax.dot_general(a, b, (((1,), (1,)), ((), ())), preferred_element_type=F32)


def _dot_tn(a, b):
    return lax.dot_general(a, b, (((0,), (0,)), ((), ())), preferred_element_type=F32)


def _bf(x):
    return x.astype(BF16)


def _allgather(arrs):
    na = len(arrs)

    def body(*refs):
        x_refs, o_refs = refs[:na], refs[na:2 * na]
        send_sems, recv_sems, local_sems = refs[2 * na:]
        x, y, c = lax.axis_index("x"), lax.axis_index("y"), lax.axis_index("c")
        me, sibling = (x, y, c), (x, y, 1 - c)
        chips = [(1 - x, y), (x, 1 - y), (1 - x, 1 - y)]

        def copy(a, k, block, to, src=None):
            slot = o_refs[a].at[4 * block[0] + 2 * block[1] + block[2]]
            return pltpu.make_async_remote_copy(
                src_ref=slot if src is None else src, dst_ref=slot, send_sem=send_sems.at[7 * a + k],
                recv_sem=recv_sems.at[7 * a + k], device_id=to, device_id_type=MESH_ID)

        mine = [pltpu.make_async_copy(x_refs[a], o_refs[a].at[4 * x + 2 * y + c], local_sems.at[a]) for a in range(na)]
        for cp in mine:
            cp.start()
        first = []
        for a in range(na):
            first.append(copy(a, 0, me, sibling, src=x_refs[a]))
            first += [copy(a, 1 + j, me, (*chip, c), src=x_refs[a]) for j, chip in enumerate(chips)]
        for cp in first:
            cp.start()
        passed = []
        for j, chip in enumerate(chips):
            for a in range(na):
                copy(a, 1 + j, (*chip, c), me).wait_recv()
                passed.append(copy(a, 4 + j, (*chip, c), sibling))
                passed[-1].start()
        for a in range(na):
            copy(a, 0, sibling, me).wait_recv()
            for j, chip in enumerate(chips):
                copy(a, 4 + j, (*chip, 1 - c), me).wait_recv()
        for cp in first + passed:
            cp.wait_send()
        for cp in mine:
            cp.wait()

    anyspec = pl.BlockSpec(memory_space=pl.ANY)
    return pl.pallas_call(
        body, name="weights_allgather",
        out_shape=[SDS((N_DEV,) + a.shape, a.dtype) for a in arrs],
        in_specs=[anyspec] * na, out_specs=[anyspec] * na,
        scratch_shapes=[pltpu.SemaphoreType.DMA((7 * na,)), pltpu.SemaphoreType.DMA((7 * na,)), pltpu.SemaphoreType.DMA((na,))],
    )(*arrs)


def _grad_exchange(per_dest, shared):
    nd, ns = len(per_dest), len(shared)
    na = nd + ns

    def body(*refs):
        in_refs, out_refs = refs[:na], refs[na:2 * na]
        send_sems, recv_sems, local_sems = refs[2 * na:]
        x, y, c = lax.axis_index("x"), lax.axis_index("y"), lax.axis_index("c")
        me = 4 * x + 2 * y + c

        def src(a, idx):
            return in_refs[a].at[idx] if a < nd else in_refs[a]

        loc = [pltpu.make_async_copy(src(a, me), out_refs[a].at[me], local_sems.at[a]) for a in range(na)]
        for cp in loc:
            cp.start()
        sends, recvs = [], []
        for k in range(1, N_DEV):
            px = 1 - x if k & 4 else x
            py = 1 - y if k & 2 else y
            pc = 1 - c if k & 1 else c
            peer, pidx = (px, py, pc), 4 * px + 2 * py + pc
            for a in range(na):
                sem = 7 * a + k - 1
                sends.append(pltpu.make_async_remote_copy(
                    src_ref=src(a, pidx), dst_ref=out_refs[a].at[me], send_sem=send_sems.at[sem],
                    recv_sem=recv_sems.at[sem], device_id=peer, device_id_type=MESH_ID))
                recvs.append(pltpu.make_async_remote_copy(
                    src_ref=src(a, pidx), dst_ref=out_refs[a].at[pidx], send_sem=send_sems.at[sem],
                    recv_sem=recv_sems.at[sem], device_id=peer, device_id_type=MESH_ID))
        for cp in sends:
            cp.start()
        for cp in recvs:
            cp.wait_recv()
        for cp in sends:
            cp.wait_send()
        for cp in loc:
            cp.wait()

    anyspec = pl.BlockSpec(memory_space=pl.ANY)
    return pl.pallas_call(
        body, name="grad_exchange",
        out_shape=[SDS(a.shape, a.dtype) for a in per_dest] + [SDS((N_DEV,) + a.shape, a.dtype) for a in shared],
        in_specs=[anyspec] * na, out_specs=[anyspec] * na,
        scratch_shapes=[pltpu.SemaphoreType.DMA((7 * na,)), pltpu.SemaphoreType.DMA((7 * na,)), pltpu.SemaphoreType.DMA((na,))],
    )(*per_dest, *shared)


_HBM_SPEC = pl.BlockSpec(memory_space=pltpu.HBM)
_SEM_SPEC = pl.BlockSpec(memory_space=pltpu.SEMAPHORE)
_EFFECT = pltpu.SideEffectType.DATAFLOW_SIDE_EFFECTING


def _peer_list(x, y, c):
    out = []
    for k in range(1, N_DEV):
        px = 1 - x if k & 4 else x
        py = 1 - y if k & 2 else y
        pc = 1 - c if k & 1 else c
        out.append(((px, py, pc), 4 * px + 2 * py + pc))
    return out


def _split_copies(src, land, send_sems, recv_sems, per_dest, receive):
    x, y, c = lax.axis_index("x"), lax.axis_index("y"), lax.axis_index("c")
    me = 4 * x + 2 * y + c
    out = []
    for k, (peer, pidx) in enumerate(_peer_list(x, y, c)):
        for a in range(len(src)):
            out.append(pltpu.make_async_remote_copy(
                src_ref=src[a].at[pidx] if per_dest else src[a], dst_ref=land[a].at[pidx if receive else me],
                send_sem=send_sems.at[7 * a + k], recv_sem=recv_sems.at[7 * a + k], device_id=peer, device_id_type=MESH_ID))
    return out


def _own_copies(src, land, own_sems, per_dest):
    me = 4 * lax.axis_index("x") + 2 * lax.axis_index("y") + lax.axis_index("c")
    return [pltpu.make_async_copy(src[a].at[me] if per_dest else src[a], land[a].at[me], own_sems.at[a]) for a in range(len(src))]


def _exchange_start(arrs, per_dest, name):
    na = len(arrs)
    land_shapes = [a.shape if per_dest else (N_DEV,) + a.shape for a in arrs]
    lands = [pltpu.with_memory_space_constraint(lax.empty(s, a.dtype), pltpu.HBM) for s, a in zip(land_shapes, arrs)]

    def body(*refs):
        src, land = refs[:na], refs[na:2 * na]
        send_sems, recv_sems, own_sems, token = refs[2 * na], refs[2 * na + 1], refs[2 * na + 2], refs[-1]
        for cp in _split_copies(src, land, send_sems, recv_sems, per_dest, False) + _own_copies(src, land, own_sems, per_dest):
            cp.start()
        token[...] = jnp.zeros_like(token)

    outs = pl.pallas_call(
        body, name=name,
        out_shape=[pltpu.SemaphoreType.DMA((7 * na,)), pltpu.SemaphoreType.DMA((7 * na,)), pltpu.SemaphoreType.DMA((na,))]
        + [pltpu.HBM(a.shape, a.dtype) for a in arrs] + [pltpu.HBM(s, a.dtype) for s, a in zip(land_shapes, arrs)]
        + [SDS((8, LANES), F32)],
        in_specs=[_HBM_SPEC] * (2 * na),
        out_specs=[_SEM_SPEC] * 3 + [_HBM_SPEC] * (2 * na) + [pl.BlockSpec(memory_space=pltpu.VMEM)],
        input_output_aliases={k: 3 + k for k in range(2 * na)},
        compiler_params=pltpu.CompilerParams(has_side_effects=_EFFECT),
    )(*[pltpu.with_memory_space_constraint(a, pltpu.HBM) for a in arrs], *lands)
    return (na, per_dest, outs[:-1]), outs[-1]


def _exchange_wait(state, after, name):
    na, per_dest, started = state

    def body(*refs):
        src, land = refs[:na], refs[na:2 * na]
        send_sems, recv_sems, own_sems = refs[2 * na], refs[2 * na + 1], refs[2 * na + 2]
        for cp in _split_copies(src, land, send_sems, recv_sems, per_dest, True):
            cp.wait_send()
            cp.wait_recv()
        for cp in _own_copies(src, land, own_sems, per_dest):
            cp.wait()

    bufs = started[3:]
    outs = pl.pallas_call(
        body, name=name,
        out_shape=[pltpu.HBM(b.shape, b.dtype) for b in bufs],
        in_specs=[_HBM_SPEC] * (2 * na) + [_SEM_SPEC] * 3 + [pl.BlockSpec(memory_space=pl.ANY)],
        out_specs=[_HBM_SPEC] * (2 * na),
        input_output_aliases={k: k for k in range(2 * na)},
        compiler_params=pltpu.CompilerParams(has_side_effects=_EFFECT),
    )(*bufs, started[0], started[1], started[2], after)
    return outs[na:]


def _col_pieces(n, segments):
    out = []
    for j in range(N_DEV):
        lo, hi = j * n, (j + 1) * n
        for (k, s0, w, c0) in segments:
            a, b = max(lo, s0), min(hi, s0 + w)
            if a < b:
                out.append((j, a - lo, b - lo, k, c0 + a - s0, c0 + b - s0))
    return out


def _assemble_cols(slots, groups, out_widths, name):
    r, p = slots.shape[1], slots.shape[2]
    tb = _pick(r, (128, 64, 32, 16, 8))
    covered = [0] * len(out_widths)
    for (_, n, segs) in groups:
        for (k, _, w, _) in segs:
            covered[k] += w

    def body(s_ref, *o_refs):
        for k, wd in enumerate(out_widths):
            if covered[k] < wd:
                o_refs[k][...] = jnp.zeros_like(o_refs[k])
        for (off, n, segs) in groups:
            for (j, a0, a1, k, d0, d1) in _col_pieces(n, segs):
                o_refs[k][:, d0:d1] = s_ref[j, :, off + a0:off + a1]

    return pl.pallas_call(
        body, name=name, grid=(r // tb,), in_specs=[pl.BlockSpec((N_DEV, tb, p), lambda i: (0, i, 0))],
        out_specs=[pl.BlockSpec((tb, w), lambda i: (i, 0)) for w in out_widths],
        out_shape=[SDS((r, w), slots.dtype) for w in out_widths], compiler_params=_cp("arbitrary"))(slots)


def _scatter_cols(pieces, segments, n, name):
    r = pieces[0].shape[0]
    tb = _pick(r, (128, 64, 32, 16, 8))

    def body(*refs):
        p_refs, o_ref = refs[:-1], refs[-1]
        for (j, a0, a1, k, d0, d1) in _col_pieces(n, segments):
            o_ref[j, :, a0:a1] = p_refs[k][:, d0:d1]

    return pl.pallas_call(
        body, name=name, grid=(r // tb,), in_specs=[pl.BlockSpec((tb, a.shape[1]), lambda i: (i, 0)) for a in pieces],
        out_specs=pl.BlockSpec((N_DEV, tb, n), lambda i: (0, i, 0)), out_shape=SDS((N_DEV, r, n), pieces[0].dtype),
        compiler_params=_cp("arbitrary"))(*pieces)


def _modulation(cvec, w_mod, b_mod):
    d, n = w_mod.shape

    def body(c_ref, w_ref, b_ref, s_ref, o_ref):
        cv = c_ref[...]
        s = cv * _sigmoid(cv)
        s_ref[...] = s
        o_ref[...] = _dot(_bf(s), w_ref[...]) + b_ref[...]

    return pl.pallas_call(body, name="modulation", out_shape=(SDS((8, d), F32), SDS((8, n), F32)),
                          compiler_params=_cp())(cvec, w_mod, b_mod)


def _norm_mod_proj(x_arr, g, shsc, w_main, w_gate, rows_total, row0, filled, name):
    m_rows, d = x_arr.shape
    n = w_main.shape[1]
    tb = _pick(m_rows, (1024, 256))
    cb = _pick(n, (1408, 1024, 768, 512, 384, 256, 128))
    gate = w_gate is not None
    nout = 3 if gate else 2
    nin = 5 if gate else 4
    rb = row0 // tb

    def body(*refs):
        x_ref, g_ref, ss_ref, wm_ref = refs[:4]
        wg_ref = refs[4] if gate else None
        outs = refs[len(refs) - 1 - nout:len(refs) - 1]
        hn_ref, z_ref = outs[0], outs[1]
        hn_sc = refs[-1]

        @pl.when(pl.program_id(1) == 0)
        def _():
            x = x_ref[...]
            r = lax.rsqrt(jnp.mean(x * x, axis=-1, keepdims=True) + EPS)
            hb = _bf((x * r * g_ref[...]) * (1.0 + ss_ref[1:2, :]) + ss_ref[0:1, :])
            hn_sc[...] = hb
            hn_ref[...] = hb
            if gate:
                outs[2][...] = _dot(hb, wg_ref[...])

        z_ref[...] = _bf(_dot(hn_sc[...], wm_ref[...]))

    in_specs = [pl.BlockSpec((tb, d), lambda i, j: (i, 0)), pl.BlockSpec((1, d), lambda i, j: (0, 0)),
                pl.BlockSpec((2, d), lambda i, j: (0, 0)), pl.BlockSpec((d, cb), lambda i, j: (0, j))]
    out_specs = [pl.BlockSpec((tb, d), lambda i, j: (rb + i, 0)), pl.BlockSpec((tb, cb), lambda i, j: (rb + i, j))]
    out_shape = [SDS((rows_total, d), BF16), SDS((rows_total, n), BF16)]
    args = [x_arr, g, shsc, w_main]
    if gate:
        in_specs.append(pl.BlockSpec((d, LANES), lambda i, j: (0, 0)))
        out_specs.append(pl.BlockSpec((tb, LANES), lambda i, j: (rb + i, 0)))
        out_shape.append(SDS((rows_total, LANES), F32))
        args.append(w_gate)
    aliases = {}
    if filled is not None:
        in_specs += [pl.BlockSpec(memory_space=pl.ANY)] * nout
        args += list(filled)
        aliases = {nin + k: k for k in range(nout)}
    return pl.pallas_call(
        body, name=name, grid=(m_rows // tb, n // cb), in_specs=in_specs, out_specs=out_specs, out_shape=out_shape,
        input_output_aliases=aliases, scratch_shapes=[pltpu.VMEM((tb, d), BF16)],
        compiler_params=_cp("arbitrary", "arbitrary"))(*args)


def _seg_masks(row, t_rows, s_rows):
    prev_ok = (row != 0) & (row != t_rows)
    next_ok = (row != t_rows - 1) & (row != s_rows - 1)
    return prev_ok, next_ok


def _shift_rows(z, halo_prev, halo_next, tb):
    loc = lax.broadcasted_iota(jnp.int32, (tb, 1), 0)
    zp = jnp.where(loc == 0, halo_prev, pltpu.roll(z, 1, 0))
    zn = jnp.where(loc == tb - 1, halo_next, pltpu.roll(z, tb - 1, 0))
    return zp, zn


def _qk_conv(z_main, conv_w, t_rows, md, qscale):
    s_rows = z_main.shape[0]
    tb = _pick(s_rows, (1280, 1024, 256))
    cb = _pick(md, (512, 256, 128))
    nb8 = tb // 8

    def body(zm, zp, zn, w_ref, o_ref):
        i, j = pl.program_id(0), pl.program_id(1)
        z = zm[...].astype(F32)
        zprev, znext = _shift_rows(z, zp[7:8, :].astype(F32), zn[0:1, :].astype(F32), tb)
        row = i * tb + lax.broadcasted_iota(jnp.int32, (tb, 1), 0)
        prev_ok, next_ok = _seg_masks(row, t_rows, s_rows)
        pre = (w_ref[0:1, :] * jnp.where(prev_ok, zprev, 0.0) + w_ref[1:2, :] * z
               + w_ref[2:3, :] * jnp.where(next_ok, znext, 0.0))
        scale = jnp.where(j * cb < md, qscale, 1.0)
        o_ref[...] = _bf(pre * _sigmoid(pre) * scale)

    return pl.pallas_call(
        body, name="qk_conv", grid=(s_rows // tb, 2 * md // cb),
        in_specs=[pl.BlockSpec((tb, cb), lambda i, j: (i, j)),
                  pl.BlockSpec((8, cb), lambda i, j: (jnp.maximum(i * nb8 - 1, 0), j)),
                  pl.BlockSpec((8, cb), lambda i, j: (jnp.minimum((i + 1) * nb8, s_rows // 8 - 1), j)),
                  pl.BlockSpec((8, cb), lambda i, j: (0, j))],
        out_specs=pl.BlockSpec((tb, cb), lambda i, j: (i, j)),
        out_shape=SDS((s_rows, 2 * md), BF16), compiler_params=_cp("arbitrary", "arbitrary"))(z_main, z_main, z_main, conv_w)


def _chunk_gates(gates, bias, rev):
    ln = gates.shape[0]
    gz = gates + bias
    logf = _log_sigmoid(gz)
    r_id = lax.broadcasted_iota(jnp.int32, (ln, ln), 0)
    c_id = lax.broadcasted_iota(jnp.int32, (ln, ln), 1)
    mask = (c_id >= r_id) if rev else (c_id <= r_id)
    mb = mask.astype(F32).astype(BF16)
    b_all = _mask_dot(mb, logf)
    g_all = jnp.sum(logf, axis=0, keepdims=True)
    return gz, b_all, b_all.T, gz.T, g_all, mask, mb


def _head_weights(b_col, b_row, i_row, m_in, mask):
    d = jnp.where(mask, b_col - b_row + i_row, NEG)
    inter = b_col + m_in
    m_row = jnp.maximum(inter, jnp.max(d, axis=1, keepdims=True))
    return jnp.exp(d - m_row), jnp.exp(inter - m_row), m_row


def _head_state_coeffs(g, b_col, i_col, m_in):
    a = g - b_col + i_col
    m_new = jnp.maximum(g + m_in, jnp.max(a, axis=0, keepdims=True))
    return jnp.exp(g + m_in - m_new), jnp.exp(a - m_new), m_new


def _mlstm_fwd(qk, z_main, zg, bias, nh):
    s_rows = qk.shape[0]
    md = qk.shape[1] // 2
    dh = md // nh
    nc = s_rows // LCH
    ln = LCH

    def chunk_f(i):
        return jnp.where(i == 0, nc - 1, i - 1)

    def chunk_b(i):
        return jnp.where(i == 0, nc - 1, nc - 1 - i)

    def body(qf, kf, vf, gf, qb, kb, vb, gb, bias_ref, hf_ref, hb_ref, cf_ref, nf_ref, mf_ref, cb_ref, nb_ref, mb_ref,
             c_sc, n_sc, m_sc):
        i = pl.program_id(0)

        @pl.when(i == 0)
        def _():
            c_sc[...] = jnp.zeros_like(c_sc)
            n_sc[...] = jnp.zeros_like(n_sc)
            m_sc[...] = jnp.full(m_sc.shape, M_INIT, F32)

        sides = ((qf, kf, vf, gf, hf_ref, cf_ref, nf_ref, mf_ref), (qb, kb, vb, gb, hb_ref, cb_ref, nb_ref, mb_ref))
        gates = [_chunk_gates(s[3][...], bias_ref[...], dr == 1) for dr, s in enumerate(sides)]
        units = []
        for dr, (q_ref, k_ref, v_ref, _, h_ref, c_out, n_out, m_out) in enumerate(sides):
            gz, b_all, b_t, g_t, g_all, mask, _ = gates[dr]
            for h in range(nh):
                ci, cf = 2 * dr * nh + h, (2 * dr + 1) * nh + h
                sl = slice(h * dh, (h + 1) * dh)
                u = dict(dr=dr, h=h, sl=sl, h_ref=h_ref, q=q_ref[:, sl], k=k_ref[:, sl], v=v_ref[:, sl],
                         c_in=c_sc[dr, h], n_in=n_sc[dr, h, 0:1, :], m_in=m_sc[dr, h, 0:1, 0:1],
                         b_col=b_all[:, cf:cf + 1], i_col=gz[:, ci:ci + 1], g=g_all[:, cf:cf + 1])
                c_out[sl, :] = u["c_in"]
                n_out[:, sl] = n_sc[dr, h]
                m_out[h] = m_sc[dr, h]
                u["w"], u["w_int"], u["m_row"] = _head_weights(u["b_col"], b_t[cf:cf + 1, :], g_t[ci:ci + 1, :], u["m_in"], mask)
                u["qk"] = _dot_nt(u["q"], u["k"])
                units.append(u)
        for u in units:
            u["s_mat"] = u["qk"] * u["w"]
            u["qc"] = _dot(u["q"], _bf(u["c_in"]))
            u["a_old"], u["coef"], u["m_new"] = _head_state_coeffs(u["g"], u["b_col"], u["i_col"], u["m_in"])
            u["kw"] = u["k"].astype(F32) * u["coef"]
        for u in units:
            u["sv"] = _dot(_bf(u["s_mat"]), u["v"])
            u["kv"] = _dot_tn(_bf(u["kw"]), u["v"])
        for u in units:
            dr, h = u["dr"], u["h"]
            num = u["sv"] + u["w_int"] * u["qc"]
            den = (jnp.sum(u["s_mat"], axis=1, keepdims=True)
                   + u["w_int"] * jnp.sum(u["q"].astype(F32) * u["n_in"], axis=1, keepdims=True))
            u["h_ref"][:, u["sl"]] = _bf(num / jnp.maximum(jnp.abs(den), jnp.exp(-u["m_row"])))
            c_sc[dr, h] = u["a_old"] * u["c_in"] + u["kv"]
            n_sc[dr, h] = jnp.broadcast_to(u["a_old"] * u["n_in"] + jnp.sum(u["kw"], axis=0, keepdims=True), (8, dh))
            m_sc[dr, h] = jnp.broadcast_to(u["m_new"], (8, LANES))

    def tok(cfn, col):
        return pl.BlockSpec((ln, md), lambda i: (cfn(i), col))

    def gat(cfn):
        return pl.BlockSpec((ln, LANES), lambda i: (cfn(i), 0))

    def st(cfn, shape):
        return pl.BlockSpec((None,) + shape, lambda i: (cfn(i),) + (0,) * len(shape))

    st_shapes = ((nh * dh, dh), (8, md), (nh, 8, LANES))
    return pl.pallas_call(
        body, name="mlstm_fwd", grid=(nc,),
        in_specs=[tok(chunk_f, 0), tok(chunk_f, 1), tok(chunk_f, 2), gat(chunk_f),
                  tok(chunk_b, 0), tok(chunk_b, 1), tok(chunk_b, 2), gat(chunk_b),
                  pl.BlockSpec((1, LANES), lambda i: (0, 0))],
        out_specs=[tok(chunk_f, 0), tok(chunk_b, 0)] + [st(chunk_f, s) for s in st_shapes] + [st(chunk_b, s) for s in st_shapes],
        out_shape=[SDS((s_rows, md), BF16)] * 2 + [SDS((nc,) + s, F32) for s in st_shapes] * 2,
        scratch_shapes=[pltpu.VMEM((2, nh, dh, dh), F32), pltpu.VMEM((2, nh, 8, dh), F32), pltpu.VMEM((2, nh, 8, LANES), F32)],
        compiler_params=_cp("arbitrary"))(qk, qk, z_main, zg, qk, qk, z_main, zg, bias)


def _head_rms(hs, nh, dh):
    parts, scales = [], []
    for h in range(nh):
        hh = hs[:, h * dh:(h + 1) * dh]
        r = lax.rsqrt(jnp.mean(hh * hh, axis=-1, keepdims=True) + EPS)
        parts.append(hh * r)
        scales.append(r)
    return jnp.concatenate(parts, axis=1), scales


def _layer_norm(v):
    vc = v - jnp.mean(v, axis=-1, keepdims=True)
    r = lax.rsqrt(jnp.mean(vc * vc, axis=-1, keepdims=True) + EPS)
    return vc * r, r


def _sgu_mix(vnb, ws_ref, bs_ref, tb, ng, gd, sc):
    rows = []
    for ch in range(tb // sc):
        cols = []
        for g in range(ng):
            blk = vnb[ch * sc:(ch + 1) * sc, g * gd:(g + 1) * gd]
            cols.append(_dot(_bf(ws_ref[g]), blk) + bs_ref[:, g:g + 1])
        rows.append(jnp.concatenate(cols, axis=1))
    return jnp.concatenate(rows, axis=0)


def _mixer_fwd(hf, hb, z_main, xs, hg, lng, lnb, w_s, b_st, wbm, wbs, wout, mx2, t_rows, nh):
    d = xs.shape[1]
    ng, sc = w_s.shape[0], w_s.shape[1]
    dh, gd = d // nh, d // ng
    tb = _pick(t_rows, (256,))

    def body(hf_ref, hb_ref, zo, zu, zv, zgm, zgg, x_ref, hg_ref, lng_ref, lnb_ref, ws_ref, bs_ref, wbm_ref, wbs_ref,
             wo_ref, mx2_ref, h1_ref, ym_ref, ys_ref, pm_ref, ps_ref, y_ref, out_ref):
        hs = hf_ref[...].astype(F32) + hb_ref[...].astype(F32)
        hn, _ = _head_rms(hs, nh, dh)
        ym = _bf(_sigmoid(zo[...].astype(F32)) * (hn * hg_ref[...]))
        ym_ref[...] = ym
        vhat, _ = _layer_norm(_gelu(zv[...].astype(F32)))
        vnb = _bf(vhat * lng_ref[...] + lnb_ref[...])
        ys = _bf(_gelu(zu[...].astype(F32)) * _sgu_mix(vnb, ws_ref, bs_ref, tb, ng, gd, sc))
        ys_ref[...] = ys
        pm = _dot(ym, wbm_ref[...])
        ps = _dot(ys, wbs_ref[...])
        pm_ref[...] = _bf(pm)
        ps_ref[...] = _bf(ps)
        y = _bf(_sigmoid(zgm[...].astype(F32)) * pm + _sigmoid(zgg[...].astype(F32)) * ps)
        y_ref[...] = y
        out = _dot(y, wo_ref[...])
        out_ref[...] = _bf(out)
        h1_ref[...] = x_ref[...] + mx2_ref[...] * out

    def tok(col):
        return pl.BlockSpec((tb, d), lambda i: (i, col))

    def full(shape):
        return pl.BlockSpec(shape, lambda i: (0,) * len(shape))

    return pl.pallas_call(
        body, name="mixer_fwd", grid=(t_rows // tb,),
        in_specs=[tok(0), tok(0), tok(3), tok(4), tok(5), tok(6), tok(7), tok(0), full((1, d)), full((1, d)), full((1, d)),
                  full((ng, sc, sc)), full((sc, LANES)), full((d, d)), full((d, d)), full((d, d)), full((1, d))],
        out_specs=[tok(0)] * 7,
        out_shape=[SDS((t_rows, d), F32)] + [SDS((t_rows, d), BF16)] * 6,
        compiler_params=_cp("arbitrary"))(hf, hb, z_main, z_main, z_main, z_main, z_main, xs, hg, lng, lnb, w_s, b_st,
                                          wbm, wbs, wout, mx2)


def _grid_taps(a_ext, n_ext):
    col = lax.broadcasted_iota(jnp.int32, (n_ext, 1), 0) % GRID_W
    left = jnp.where(col != 0, pltpu.roll(a_ext, 1, 0), 0.0)
    right = jnp.where(col != GRID_W - 1, pltpu.roll(a_ext, n_ext - 1, 0), 0.0)
    return left, right


def _with_halo(prev, main, nxt, i, ni, tb):
    ext = jnp.concatenate([prev, main, nxt], axis=0).astype(F32)
    pos = lax.broadcasted_iota(jnp.int32, (tb + 2 * GRID_W, 1), 0)
    inside = ((pos >= GRID_W) | (i > 0)) & ((pos < tb + GRID_W) | (i < ni - 1))
    return jnp.where(inside, ext, 0.0)


def _halo_specs(tb, cb, t_rows, col0=0):
    nh64 = tb // GRID_W
    return [pl.BlockSpec((tb, cb), lambda i, j: (i, col0 + j)),
            pl.BlockSpec((GRID_W, cb), lambda i, j: (jnp.maximum(i * nh64 - 1, 0), col0 + j)),
            pl.BlockSpec((GRID_W, cb), lambda i, j: (jnp.minimum((i + 1) * nh64, t_rows // GRID_W - 1), col0 + j))]


def _ffn_tail(ab, w_conv9, w_down, h1, mx5, gfin, target, dff):
    t_rows, d = h1.shape
    tb = _pick(t_rows, (256,))
    cb = _pick(dff, (1408, 256, 128))
    ni, nj = t_rows // tb, dff // cb
    n_ext = tb + 2 * GRID_W

    def body(am, ap, an, b_ref, wc_ref, wd_ref, h1_ref, mx5_ref, gf_ref, tg_ref, ac_ref, f_ref, dh2_ref, dffn_ref, st_ref, acc):
        i, j = pl.program_id(0), pl.program_id(1)
        a_ext = _with_halo(ap[...], am[...], an[...], i, ni, tb)
        left, right = _grid_taps(a_ext, n_ext)
        conv = jnp.zeros((tb, cb), F32)
        for di in range(3):
            o = di * GRID_W
            conv = conv + (wc_ref[3 * di:3 * di + 1, :] * left[o:o + tb] + wc_ref[3 * di + 1:3 * di + 2, :] * a_ext[o:o + tb]
                           + wc_ref[3 * di + 2:3 * di + 3, :] * right[o:o + tb])
        ac_ref[...] = _bf(conv)
        fb = _bf(conv * _sigmoid(conv) * b_ref[...].astype(F32))
        f_ref[...] = fb

        @pl.when(j == 0)
        def _():
            acc[...] = jnp.zeros_like(acc)

        @pl.when((i == 0) & (j == 0))
        def _():
            st_ref[...] = jnp.zeros_like(st_ref)

        acc[...] += _dot(fb, wd_ref[...])

        @pl.when(j == nj - 1)
        def _():
            ffn = acc[...]
            h2 = h1_ref[...] + mx5_ref[...] * ffn
            r = lax.rsqrt(jnp.mean(h2 * h2, axis=-1, keepdims=True) + EPS)
            xn = h2 * r
            e = xn * gf_ref[...] - tg_ref[...]
            loss = 0.5 * jnp.sum(jnp.sum(e * e, axis=1, keepdims=True), axis=0, keepdims=True) / d
            dy = e * (1.0 / d)
            dxn = dy * gf_ref[...]
            dh2 = r * (dxn - xn * jnp.mean(dxn * xn, axis=-1, keepdims=True))
            dh2_ref[...] = dh2
            dffn_ref[...] = _bf(dh2 * mx5_ref[...])
            st_ref[...] += jnp.concatenate(
                [jnp.sum(dy * xn, axis=0, keepdims=True), jnp.sum(dh2 * ffn, axis=0, keepdims=True),
                 jnp.broadcast_to(loss, (1, d)), jnp.zeros((5, d), F32)], axis=0)

    def tokd():
        return pl.BlockSpec((tb, d), lambda i, j: (i, 0))

    def rowd():
        return pl.BlockSpec((1, d), lambda i, j: (0, 0))

    return pl.pallas_call(
        body, name="ffn_tail", grid=(ni, nj),
        in_specs=_halo_specs(tb, cb, t_rows) + [pl.BlockSpec((tb, cb), lambda i, j: (i, nj + j)),
                                                pl.BlockSpec((16, cb), lambda i, j: (0, j)),
                                                pl.BlockSpec((cb, d), lambda i, j: (j, 0)), tokd(), rowd(), rowd(), tokd()],
        out_specs=[pl.BlockSpec((tb, cb), lambda i, j: (i, j)), pl.BlockSpec((tb, cb), lambda i, j: (i, j)), tokd(), tokd(),
                   pl.BlockSpec((8, d), lambda i, j: (0, 0))],
        out_shape=[SDS((t_rows, dff), BF16), SDS((t_rows, dff), BF16), SDS((t_rows, d), F32), SDS((t_rows, d), BF16),
                   SDS((8, d), F32)],
        scratch_shapes=[pltpu.VMEM((tb, d), F32)],
        compiler_params=_cp("arbitrary", "arbitrary"))(ab, ab, ab, ab, w_conv9, w_down, h1, mx5, gfin, target)


def _ffn_bwd_gate(dffn, w_down, aconv, ab, dff):
    t_rows, d = dffn.shape
    tb = _pick(t_rows, (512,))
    cb = _pick(dff, (1408, 256, 128))
    nj = dff // cb

    def body(g_ref, wd_ref, ac_ref, b_ref, db_ref, dac_ref):
        df = _dot_nt(g_ref[...], wd_ref[...])
        ac = ac_ref[...].astype(F32)
        sa = _sigmoid(ac)
        db_ref[...] = _bf(df * ac * sa)
        dac_ref[...] = _bf(df * b_ref[...].astype(F32) * (sa * (1.0 + ac * (1.0 - sa))))

    blk = pl.BlockSpec((tb, cb), lambda i, j: (i, j))
    return pl.pallas_call(
        body, name="ffn_bwd_gate", grid=(t_rows // tb, nj),
        in_specs=[pl.BlockSpec((tb, d), lambda i, j: (i, 0)), pl.BlockSpec((cb, d), lambda i, j: (j, 0)), blk,
                  pl.BlockSpec((tb, cb), lambda i, j: (i, nj + j))],
        out_specs=[blk, blk], out_shape=[SDS((t_rows, dff), BF16)] * 2,
        compiler_params=_cp("arbitrary", "arbitrary"))(dffn, w_down, aconv, ab)


def _ffn_conv_bwd(dac, ab, w_conv9, dff):
    t_rows = dac.shape[0]
    tb = _pick(t_rows, (256,))
    cb = _pick(dff, (1408, 256, 128))
    ni, nj = t_rows // tb, dff // cb
    n_ext = tb + 2 * GRID_W
    nh64 = tb // GRID_W

    def body(dm, dp, dn, am, ap, an, wc_ref, da_ref, gw_ref):
        i = pl.program_id(1)
        d_ext = _with_halo(dp[...], dm[...], dn[...], i, ni, tb)
        a_ext = _with_halo(ap[...], am[...], an[...], i, ni, tb)
        d_left, d_right = _grid_taps(d_ext, n_ext)
        a_left, a_right = _grid_taps(a_ext, n_ext)
        dmain = d_ext[GRID_W:GRID_W + tb]
        da = jnp.zeros((tb, cb), F32)
        rows = []
        for di in range(3):
            o = (2 - di) * GRID_W
            da = da + (wc_ref[3 * di:3 * di + 1, :] * d_right[o:o + tb] + wc_ref[3 * di + 1:3 * di + 2, :] * d_ext[o:o + tb]
                       + wc_ref[3 * di + 2:3 * di + 3, :] * d_left[o:o + tb])
            o = di * GRID_W
            for tap in (a_left, a_ext, a_right):
                rows.append(jnp.sum(dmain * tap[o:o + tb], axis=0, keepdims=True))
        da_ref[...] = _bf(da)

        @pl.when(i == 0)
        def _():
            gw_ref[...] = jnp.zeros_like(gw_ref)

        gw_ref[...] += jnp.concatenate(rows + [jnp.zeros((7, cb), F32)], axis=0)

    def halo(col0):
        return [pl.BlockSpec((tb, cb), lambda j, i: (i, col0 + j)),
                pl.BlockSpec((GRID_W, cb), lambda j, i: (jnp.maximum(i * nh64 - 1, 0), col0 + j)),
                pl.BlockSpec((GRID_W, cb), lambda j, i: (jnp.minimum((i + 1) * nh64, t_rows // GRID_W - 1), col0 + j))]

    return pl.pallas_call(
        body, name="ffn_conv_bwd", grid=(nj, ni),
        in_specs=halo(0) + halo(0) + [pl.BlockSpec((16, cb), lambda j, i: (0, j))],
        out_specs=[pl.BlockSpec((tb, cb), lambda j, i: (i, j)), pl.BlockSpec((16, cb), lambda j, i: (0, j))],
        out_shape=[SDS((t_rows, dff), BF16), SDS((16, dff), F32)],
        compiler_params=_cp("arbitrary", "arbitrary"))(dac, dac, dac, ab, ab, ab, w_conv9)


def _proj_norm_bwd(pairs, x_arr, x_row0, g, scale, resid, m_rows, name, row_blocks=(1024, 256)):
    d = x_arr.shape[1]
    tm = _pick(m_rows, row_blocks)
    te = 256
    ni = m_rows // tm
    starts, total = [], 0
    for (_, _, _, _, k_p, tk_p) in pairs:
        starts.append(total)
        total += k_p // tk_p
    npairs = len(pairs)
    has_dx = resid is not None

    def body(*refs):
        a_refs, b_refs = refs[0:2 * npairs:2], refs[1:2 * npairs:2]
        rest = refs[2 * npairs:]
        if has_dx:
            x_ref, g_ref, sc_ref, r_ref, dx_ref, st_ref, acc = rest
        else:
            x_ref, g_ref, sc_ref, st_ref, acc = rest
        i, k = pl.program_id(0), pl.program_id(1)

        @pl.when(k == 0)
        def _():
            acc[...] = jnp.zeros_like(acc)

        @pl.when((i == 0) & (k == 0))
        def _():
            st_ref[...] = jnp.zeros_like(st_ref)

        for p in range(npairs):
            nk = pairs[p][4] // pairs[p][5]

            @pl.when((k >= starts[p]) & (k < starts[p] + nk))
            def _(p=p):
                acc[...] += _dot_nt(a_refs[p][...], b_refs[p][...])

        @pl.when(k == total - 1)
        def _():
            sums = [jnp.zeros((1, d), F32)] * 3
            for r0 in range(0, tm, te):
                rows = slice(r0, r0 + te)
                dhn = acc[rows, :]
                x = x_ref[rows, :]
                r = lax.rsqrt(jnp.mean(x * x, axis=-1, keepdims=True) + EPS)
                xn = x * r
                dmod = dhn * (1.0 + sc_ref[...])
                dxn = dmod * g_ref[...]
                if has_dx:
                    dx_ref[rows, :] = r * (dxn - xn * jnp.mean(dxn * xn, axis=-1, keepdims=True)) + r_ref[rows, :]
                sums = [sums[0] + jnp.sum(dmod * xn, axis=0, keepdims=True), sums[1] + jnp.sum(dhn, axis=0, keepdims=True),
                        sums[2] + jnp.sum(dhn * (xn * g_ref[...]), axis=0, keepdims=True)]
            st_ref[...] += jnp.concatenate(sums + [jnp.zeros((5, d), F32)], axis=0)

    in_specs, args = [], []
    for p, (a, a_row0, b, b_col0, k_p, tk_p) in enumerate(pairs):
        nk, s0, ar, bc = k_p // tk_p, starts[p], a_row0 // tm, b_col0 // tk_p

        def kk(k, s0=s0, nk=nk):
            return jnp.clip(k - s0, 0, nk - 1)

        in_specs.append(pl.BlockSpec((tm, tk_p), lambda i, k, ar=ar, kk=kk: (ar + i, kk(k))))
        in_specs.append(pl.BlockSpec((d, tk_p), lambda i, k, bc=bc, kk=kk: (0, bc + kk(k))))
        args += [a, b]
    xr = x_row0 // tm
    in_specs += [pl.BlockSpec((tm, d), lambda i, k: (xr + i, 0)), pl.BlockSpec((1, d), lambda i, k: (0, 0)),
                 pl.BlockSpec((1, d), lambda i, k: (0, 0))]
    args += [x_arr, g, scale]
    out_specs, out_shape = [], []
    if has_dx:
        in_specs.append(pl.BlockSpec((tm, d), lambda i, k: (i, 0)))
        args.append(resid)
        out_specs.append(pl.BlockSpec((tm, d), lambda i, k: (i, 0)))
        out_shape.append(SDS((m_rows, d), F32))
    out_specs.append(pl.BlockSpec((8, d), lambda i, k: (0, 0)))
    out_shape.append(SDS((8, d), F32))
    return pl.pallas_call(
        body, name=name, grid=(ni, total), in_specs=in_specs, out_specs=out_specs, out_shape=out_shape,
        scratch_shapes=[pltpu.VMEM((tm, d), F32)], compiler_params=_cp("arbitrary", "arbitrary"))(*args)


def _wgrad(a, b, k_rows, name):
    m, n = a.shape[1], b.shape[1]
    tm = _pick(m, (1408, 1024, 512, 384, 256, 128))
    tn = _pick(n, (1408, 1024, 768, 512, 384, 256, 128))
    tk = _pick(k_rows, (1280, 1024, 256))
    nk = k_rows // tk

    def body(a_ref, b_ref, o_ref, acc):
        k = pl.program_id(2)

        @pl.when(k == 0)
        def _():
            acc[...] = jnp.zeros_like(acc)

        acc[...] += _dot_tn(a_ref[...], b_ref[...])

        @pl.when(k == nk - 1)
        def _():
            o_ref[...] = _bf(acc[...])

    return pl.pallas_call(
        body, name=name, grid=(m // tm, n // tn, nk),
        in_specs=[pl.BlockSpec((tk, tm), lambda i, j, k: (k, i)), pl.BlockSpec((tk, tn), lambda i, j, k: (k, j))],
        out_specs=pl.BlockSpec((tm, tn), lambda i, j, k: (i, j)), out_shape=SDS((m, n), BF16),
        scratch_shapes=[pltpu.VMEM((tm, tn), F32)],
        compiler_params=_cp("arbitrary", "arbitrary", "arbitrary"))(a, b)


def _lane_put(col, lane_idx):
    lane = lax.broadcasted_iota(jnp.int32, (1, LANES), 1)
    return jnp.where(lane == lane_idx, col, 0.0)


def _mixer_bwd(dh1, out, hf, hb, z_main, pm, ps, hg, lng, lnb, w_s, b_st, wbm, wbs, wout, mx2, t_rows, nh):
    d = dh1.shape[1]
    ng, sc = w_s.shape[0], w_s.shape[1]
    dh, gd = d // nh, d // ng
    tb = _pick(t_rows, (256,))

    def body(dh1_ref, out_ref, hf_ref, hb_ref, zo, zu, zv, zgm, zgg, pm_ref, ps_ref, hg_ref, lng_ref, lnb_ref, ws_ref, bs_ref,
             wbm_ref, wbs_ref, wo_ref, mx2_ref, dz_ref, dhs_ref, dout_ref, dpm_ref, dps_ref, st_ref, dws_ref, dbs_ref):
        i = pl.program_id(0)

        @pl.when(i == 0)
        def _():
            st_ref[...] = jnp.zeros_like(st_ref)
            dws_ref[...] = jnp.zeros_like(dws_ref)
            dbs_ref[...] = jnp.zeros_like(dbs_ref)

        dh1v = dh1_ref[...]
        doutb = _bf(dh1v * mx2_ref[...])
        dout_ref[...] = doutb
        d_mx2 = jnp.sum(dh1v * out_ref[...].astype(F32), axis=0, keepdims=True)
        dy = _dot_nt(doutb, wo_ref[...])
        sgm, sgg = _sigmoid(zgm[...].astype(F32)), _sigmoid(zgg[...].astype(F32))
        dpmb, dpsb = _bf(dy * sgm), _bf(dy * sgg)
        dpm_ref[...] = dpmb
        dps_ref[...] = dpsb
        dz_ref[:, 3 * d:4 * d] = _bf(dy * pm_ref[...].astype(F32) * sgm * (1.0 - sgm))
        dz_ref[:, 4 * d:5 * d] = _bf(dy * ps_ref[...].astype(F32) * sgg * (1.0 - sgg))
        dym = _dot_nt(dpmb, wbm_ref[...])
        dys = _dot_nt(dpsb, wbs_ref[...])
        hs = hf_ref[...].astype(F32) + hb_ref[...].astype(F32)
        hn, scales = _head_rms(hs, nh, dh)
        so = _sigmoid(zo[...].astype(F32))
        dz_ref[:, 0:d] = _bf(dym * (hn * hg_ref[...]) * so * (1.0 - so))
        dhmn = dym * so
        d_hg = jnp.sum(dhmn * hn, axis=0, keepdims=True)
        dhn = dhmn * hg_ref[...]
        for h in range(nh):
            sl = slice(h * dh, (h + 1) * dh)
            dhs_ref[:, sl] = _bf(scales[h] * (dhn[:, sl] - hn[:, sl] * jnp.mean(dhn[:, sl] * hn[:, sl], axis=-1, keepdims=True)))
        zuv, zvv = zu[...].astype(F32), zv[...].astype(F32)
        u, du_dz = _gelu_and_grad(zuv)
        vg, dvg_dz = _gelu_and_grad(zvv)
        vhat, rstd = _layer_norm(vg)
        vnb = _bf(vhat * lng_ref[...] + lnb_ref[...])
        mixed = _sgu_mix(vnb, ws_ref, bs_ref, tb, ng, gd, sc)
        dz_ref[:, d:2 * d] = _bf(dys * mixed * du_dz)
        dmix = dys * u
        rows = []
        dbs = jnp.zeros((sc, LANES), F32)
        for ch in range(tb // sc):
            cols = []
            for g in range(ng):
                dm = dmix[ch * sc:(ch + 1) * sc, g * gd:(g + 1) * gd]
                dmb = _bf(dm)
                dws_ref[g] += _dot_nt(dmb, vnb[ch * sc:(ch + 1) * sc, g * gd:(g + 1) * gd])
                dbs = dbs + _lane_put(jnp.sum(dm, axis=1, keepdims=True), g)
                cols.append(_dot_tn(_bf(ws_ref[g]), dmb))
            rows.append(jnp.concatenate(cols, axis=1))
        dbs_ref[...] += dbs
        dvn = jnp.concatenate(rows, axis=0)
        d_lng = jnp.sum(dvn * vhat, axis=0, keepdims=True)
        d_lnb = jnp.sum(dvn, axis=0, keepdims=True)
        dvh = dvn * lng_ref[...]
        dvg = rstd * (dvh - jnp.mean(dvh, axis=-1, keepdims=True) - vhat * jnp.mean(dvh * vhat, axis=-1, keepdims=True))
        dz_ref[:, 2 * d:3 * d] = _bf(dvg * dvg_dz)
        st_ref[...] += jnp.concatenate([d_mx2, d_hg, d_lng, d_lnb, jnp.zeros((4, d), F32)], axis=0)

    def tok(col):
        return pl.BlockSpec((tb, d), lambda i: (i, col))

    def full(shape):
        return pl.BlockSpec(shape, lambda i: (0,) * len(shape))

    return pl.pallas_call(
        body, name="mixer_bwd", grid=(t_rows // tb,),
        in_specs=[tok(0), tok(0), tok(0), tok(0), tok(3), tok(4), tok(5), tok(6), tok(7), tok(0), tok(0), full((1, d)),
                  full((1, d)), full((1, d)), full((ng, sc, sc)), full((sc, LANES)), full((d, d)), full((d, d)), full((d, d)),
                  full((1, d))],
        out_specs=[pl.BlockSpec((tb, 5 * d), lambda i: (i, 0)), tok(0), tok(0), tok(0), tok(0), full((8, d)), full((ng, sc, sc)),
                   full((sc, LANES))],
        out_shape=[SDS((t_rows, 5 * d), BF16)] + [SDS((t_rows, d), BF16)] * 4 + [SDS((8, d), F32), SDS((ng, sc, sc), F32),
                                                                                SDS((sc, LANES), F32)],
        compiler_params=_cp("arbitrary"))(dh1, out, hf, hb, z_main, z_main, z_main, z_main, z_main, pm, ps, hg, lng, lnb, w_s,
                                          b_st, wbm, wbs, wout, mx2)


def _mlstm_bwd(qk, z_main, zg, bias, dhs, hf, hb, states_f, states_b, nh, t_rows):
    s_rows = qk.shape[0]
    md = qk.shape[1] // 2
    dh = md // nh
    nc = s_rows // LCH
    nx = t_rows // LCH
    ln = LCH

    def chunk_f(i):
        return jnp.where(i == nc - 1, nc - 1, nc - 2 - i)

    def chunk_b(i):
        return jnp.where(i == nc - 1, nc - 1, i)

    def body(qf, kf, vf, gf, dhf, hsf, cf, nf, mf_, qb, kb, vb, gb, dhb, hsb, cb, nb, mb_, bias_ref, dqkvf_ref, dgf_ref, dqkvb_ref,
             dgb_ref, dc_sc, dn_sc):
        i = pl.program_id(0)
        is_ctx = i == nc - 1

        @pl.when(i == 0)
        def _():
            dc_sc[...] = jnp.zeros_like(dc_sc)
            dn_sc[...] = jnp.zeros_like(dn_sc)

        sides = ((qf, kf, vf, gf, dhf, hsf, cf, nf, mf_, dqkvf_ref, dgf_ref), (qb, kb, vb, gb, dhb, hsb, cb, nb, mb_, dqkvb_ref, dgb_ref))
        gates = [_chunk_gates(s[3][...], bias_ref[...], dr == 1) for dr, s in enumerate(sides)]
        units = []
        for dr, (q_ref, k_ref, v_ref, _, dh_ref, hs_ref, c_ref, n_ref, m_ref, dqkv_ref, _) in enumerate(sides):
            gz, b_all, b_t, g_t, g_all, mask, _ = gates[dr]
            for h in range(nh):
                ci, cfl = 2 * dr * nh + h, (2 * dr + 1) * nh + h
                sl = slice(h * dh, (h + 1) * dh)
                u = dict(dr=dr, h=h, sl=sl, ci=ci, cfl=cfl, dqkv_ref=dqkv_ref, q=q_ref[:, sl], k=k_ref[:, sl], v=v_ref[:, sl],
                         dhv=jnp.where(is_ctx, 0.0, dh_ref[:, sl].astype(F32)), hs=hs_ref[:, sl].astype(F32),
                         c_in=c_ref[sl, :], n_in=n_ref[0:1, sl], m_in=m_ref[h, 0:1, 0:1],
                         b_col=b_all[:, cfl:cfl + 1], i_col=gz[:, ci:ci + 1], g=g_all[:, cfl:cfl + 1],
                         dc_new=dc_sc[dr, h], dn_new=dn_sc[dr, h, 0:1, :])
                u["qf32"], u["kf32"] = u["q"].astype(F32), u["k"].astype(F32)
                u["w"], u["w_int"], u["m_row"] = _head_weights(u["b_col"], b_t[cfl:cfl + 1, :], g_t[ci:ci + 1, :], u["m_in"], mask)
                u["qk"] = _dot_nt(u["q"], u["k"])
                units.append(u)
        for u in units:
            s_mat = u["qk"] * u["w"]
            u["s_mat"], u["sb"], u["cb16"], u["dcb"] = s_mat, _bf(s_mat), _bf(u["c_in"]), _bf(u["dc_new"])
            den = jnp.sum(s_mat, axis=1, keepdims=True) + u["w_int"] * jnp.sum(u["qf32"] * u["n_in"], axis=1, keepdims=True)
            e_m = jnp.exp(-u["m_row"])
            dnm = jnp.maximum(jnp.abs(den), e_m)
            hdh = jnp.sum(u["hs"] * u["dhv"], axis=1, keepdims=True)
            u["dden"] = jnp.where(jnp.abs(den) > e_m, -(hdh / dnm) * jnp.sign(den), 0.0)
            u["dnum_b"] = _bf(u["dhv"] / dnm)
            u["a_old"], u["coef"], _ = _head_state_coeffs(u["g"], u["b_col"], u["i_col"], u["m_in"])
            u["dsm"] = _dot_nt(u["dnum_b"], u["v"])
            u["qct"] = _dot_nt(u["dnum_b"], u["cb16"])
            u["vdc"] = _dot_nt(u["v"], u["dcb"])
        for u in units:
            ds = u["dsm"] + u["dden"]
            u["pb"] = _bf(u["w"] * ds)
            u["gmat"] = u["s_mat"] * ds
            u["dv1"] = _dot_tn(u["sb"], u["dnum_b"])
            u["dv2"] = _dot(_bf(u["kf32"] * u["coef"]), u["dcb"])
            u["dcu"] = _dot_tn(_bf(u["qf32"] * u["w_int"]), u["dnum_b"])
        for u in units:
            u["dq1"] = _dot(u["pb"], u["k"])
            u["dk1"] = _dot_tn(u["pb"], u["q"])
        acc = [dict(x1=jnp.zeros((ln, LANES), F32), x2=jnp.zeros((ln, LANES), F32), dig=jnp.zeros((ln, LANES), F32),
                    e_row=jnp.zeros((1, LANES), F32)) for _ in range(2)]
        for u in units:
            dr, h, sl, a = u["dr"], u["h"], u["sl"], acc[u["dr"]]
            dq_inter = u["w_int"] * (u["qct"] + u["dden"] * u["n_in"])
            dk_state = u["coef"] * (u["vdc"] + u["dn_new"])
            u["dqkv_ref"][:, sl] = _bf(u["dq1"] + dq_inter)
            u["dqkv_ref"][:, md + h * dh:md + (h + 1) * dh] = _bf(u["dk1"] + dk_state)
            u["dqkv_ref"][:, 2 * md + h * dh:2 * md + (h + 1) * dh] = _bf(u["dv1"] + u["dv2"])
            row_intra = jnp.sum(u["gmat"], axis=1, keepdims=True)
            col_intra = jnp.sum(u["gmat"].T, axis=1, keepdims=True)
            row_inter = jnp.sum(u["qf32"] * dq_inter, axis=1, keepdims=True)
            col_inter = jnp.sum(u["kf32"] * dk_state, axis=1, keepdims=True)
            e_old = u["a_old"] * (jnp.sum(jnp.sum(u["dc_new"] * u["c_in"], axis=1, keepdims=True), axis=0, keepdims=True)
                                  + jnp.sum(u["dn_new"] * u["n_in"], axis=1, keepdims=True))
            a["x1"] = a["x1"] + _lane_put(row_intra - col_intra + row_inter, u["cfl"])
            a["x2"] = a["x2"] + _lane_put(col_inter, u["cfl"])
            a["e_row"] = a["e_row"] + _lane_put(e_old, u["cfl"])
            a["dig"] = a["dig"] + _lane_put(col_intra + col_inter, u["ci"])
            dc_sc[dr, h] = u["a_old"] * u["dc_new"] + u["dcu"]
            dn_sc[dr, h] = jnp.broadcast_to(
                u["a_old"] * u["dn_new"] + jnp.sum(u["qf32"] * (u["w_int"] * u["dden"]), axis=0, keepdims=True), (8, dh))
        for dr, s in enumerate(sides):
            gz, mfl, a = gates[dr][0], gates[dr][6], acc[dr]
            dlogf = _mask_dot_t(mfl, a["x1"]) + _mask_dot(mfl, a["x2"]) - a["x2"] + a["e_row"]
            s[10][...] = a["dig"] + dlogf / (1.0 + jnp.exp(gz))

    def tok(cfn, col):
        return pl.BlockSpec((ln, md), lambda i: (cfn(i), col))

    def dht(cfn):
        return pl.BlockSpec((ln, md), lambda i: (jnp.minimum(cfn(i), nx - 1), 0))

    def gat(cfn):
        return pl.BlockSpec((ln, LANES), lambda i: (cfn(i), 0))

    def st(cfn, shape):
        return pl.BlockSpec((None,) + shape, lambda i: (cfn(i),) + (0,) * len(shape))

    st_shapes = ((nh * dh, dh), (8, md), (nh, 8, LANES))

    def side(cfn):
        return [tok(cfn, 0), tok(cfn, 1), tok(cfn, 2), gat(cfn), dht(cfn), tok(cfn, 0)] + [st(cfn, s) for s in st_shapes]

    def outs(cfn):
        return [pl.BlockSpec((ln, 3 * md), lambda i: (cfn(i), 0)), gat(cfn)]

    return pl.pallas_call(
        body, name="mlstm_bwd", grid=(nc,),
        in_specs=side(chunk_f) + side(chunk_b) + [pl.BlockSpec((1, LANES), lambda i: (0, 0))],
        out_specs=outs(chunk_f) + outs(chunk_b),
        out_shape=[SDS((s_rows, 3 * md), BF16), SDS((s_rows, LANES), F32)] * 2,
        scratch_shapes=[pltpu.VMEM((2, nh, dh, dh), F32), pltpu.VMEM((2, nh, 8, dh), F32)],
        compiler_params=_cp("arbitrary"))(qk, qk, z_main, zg, dhs, hf, *states_f, qk, qk, z_main, zg, dhs, hb, *states_b, bias)


def _qkv_conv_bwd(dqkv_f, dqkv_b, z_main, conv_w, t_rows, md, qscale):
    s_rows = z_main.shape[0]
    tb = _pick(s_rows, (1280, 1024, 256))
    cb = _pick(md, (512, 256, 128))
    ni, nj, ncq = s_rows // tb, 3 * md // cb, 2 * md // cb
    nb8 = tb // 8
    n_ext = tb + 16

    def body(fm, fp, fn, bm, bp, bn, zm, zp, zn, w_ref, dz_ref, gw_ref):
        j, i = pl.program_id(0), pl.program_id(1)

        @pl.when(j < ncq)
        def _():
            z = jnp.concatenate([zp[...], zm[...], zn[...]], axis=0).astype(F32)
            dqk = (jnp.concatenate([fp[...], fm[...], fn[...]], axis=0).astype(F32)
                   + jnp.concatenate([bp[...], bm[...], bn[...]], axis=0).astype(F32)) * jnp.where(j * cb < md, qscale, 1.0)
            row = i * tb - 8 + lax.broadcasted_iota(jnp.int32, (n_ext, 1), 0)
            prev_ok, next_ok = _seg_masks(row, t_rows, s_rows)
            zprev = jnp.where(prev_ok, pltpu.roll(z, 1, 0), 0.0)
            znext = jnp.where(next_ok, pltpu.roll(z, n_ext - 1, 0), 0.0)
            pre = w_ref[0:1, :] * zprev + w_ref[1:2, :] * z + w_ref[2:3, :] * znext
            sg = _sigmoid(pre)
            dpre = dqk * (sg * (1.0 + pre * (1.0 - sg)))
            dz = (w_ref[1:2, :] * dpre + w_ref[0:1, :] * jnp.where(next_ok, pltpu.roll(dpre, n_ext - 1, 0), 0.0)
                  + w_ref[2:3, :] * jnp.where(prev_ok, pltpu.roll(dpre, 1, 0), 0.0))
            dz_ref[...] = _bf(dz[8:8 + tb])
            dm = dpre[8:8 + tb]

            @pl.when(i == 0)
            def _():
                gw_ref[...] = jnp.zeros_like(gw_ref)

            gw_ref[...] += jnp.concatenate(
                [jnp.sum(dm * zprev[8:8 + tb], axis=0, keepdims=True), jnp.sum(dm * z[8:8 + tb], axis=0, keepdims=True),
                 jnp.sum(dm * znext[8:8 + tb], axis=0, keepdims=True), jnp.zeros((5, cb), F32)], axis=0)

        @pl.when(j >= ncq)
        def _():
            dz_ref[...] = _bf(fm[...].astype(F32) + bm[...].astype(F32))

    def halo(clampj):
        def cj(j):
            return jnp.minimum(j, ncq - 1) if clampj else j
        return [pl.BlockSpec((tb, cb), lambda j, i: (i, cj(j))),
                pl.BlockSpec((8, cb), lambda j, i: (jnp.maximum(i * nb8 - 1, 0), cj(j))),
                pl.BlockSpec((8, cb), lambda j, i: (jnp.minimum((i + 1) * nb8, s_rows // 8 - 1), cj(j)))]

    return pl.pallas_call(
        body, name="qkv_conv_bwd", grid=(nj, ni),
        in_specs=halo(False) + halo(False) + halo(True) + [pl.BlockSpec((8, cb), lambda j, i: (0, jnp.minimum(j, ncq - 1)))],
        out_specs=[pl.BlockSpec((tb, cb), lambda j, i: (i, j)), pl.BlockSpec((8, cb), lambda j, i: (0, jnp.minimum(j, ncq - 1)))],
        out_shape=[SDS((s_rows, 3 * md), BF16), SDS((8, 2 * md), F32)],
        compiler_params=_cp("arbitrary", "arbitrary"))(dqkv_f, dqkv_f, dqkv_f, dqkv_b, dqkv_b, dqkv_b, z_main, z_main, z_main, conv_w)


def _gate_grad_sum(dg_f, dg_b):
    s_rows = dg_f.shape[0]
    tb = _pick(s_rows, (1280, 1024, 256))

    def body(a_ref, b_ref, o_ref, st_ref):
        @pl.when(pl.program_id(0) == 0)
        def _():
            st_ref[...] = jnp.zeros_like(st_ref)

        s = a_ref[...] + b_ref[...]
        o_ref[...] = _bf(s)
        st_ref[...] += jnp.concatenate([jnp.sum(s, axis=0, keepdims=True), jnp.zeros((7, LANES), F32)], axis=0)

    blk = pl.BlockSpec((tb, LANES), lambda i: (i, 0))
    return pl.pallas_call(
        body, name="gate_grad_sum", grid=(s_rows // tb,), in_specs=[blk, blk],
        out_specs=[blk, pl.BlockSpec((8, LANES), lambda i: (0, 0))],
        out_shape=[SDS((s_rows, LANES), BF16), SDS((8, LANES), F32)], compiler_params=_cp("arbitrary"))(dg_f, dg_b)


def _mod_grads(silu_slots, dmx_sh, dmx_slots, dmc_tot, dmc_sh, silu_cctx, c_ctx, w_mod_c):
    d = silu_slots.shape[1]
    ncol, n6 = dmx_sh.shape[1], dmx_slots.shape[1]

    def body(ss_ref, dsh_ref, dsl_ref, dct_ref, dcs_ref, sc_ref, c_ref, w_ref, gw_ref, gb_ref, gc_ref):
        a = jnp.concatenate([ss_ref[...], sc_ref[...], jnp.zeros((7, d), F32)], axis=0)
        b = jnp.concatenate([dsh_ref[...], dcs_ref[...], jnp.zeros((7, ncol), F32)], axis=0)
        gw_ref[0] = lax.dot_general(a, b, (((0,), (0,)), ((), ())), preferred_element_type=F32, precision=HI)
        dct = dct_ref[...]
        gb_ref[...] = jnp.sum(dsl_ref[...], axis=0, keepdims=True) + jnp.concatenate(
            [dct, jnp.zeros((1, n6 - dct.shape[1]), F32)], axis=1)
        t = _dot_nt(_bf(jnp.broadcast_to(dct, (8, dct.shape[1]))), w_ref[...])
        cv = c_ref[...]
        s = _sigmoid(cv)
        gc_ref[...] = t[0:1, :] * (s * (1.0 + cv * (1.0 - s)))

    return pl.pallas_call(body, name="mod_grads", out_shape=[SDS((1, d, ncol), F32), SDS((1, n6), F32), SDS((1, d), F32)],
                          compiler_params=_cp())(silu_slots, dmx_sh, dmx_slots, dmc_tot, dmc_sh, silu_cctx, c_ctx, w_mod_c)


def _slot_sum(slots):
    ns, r = slots.shape[0], slots.shape[1]
    tb = _pick(r, (1024, 512, 256, 128, 64, 32, 16, 8))

    def body(s_ref, o_ref):
        acc = s_ref[0]
        for k in range(1, ns):
            acc = acc + s_ref[k]
        o_ref[...] = acc

    return pl.pallas_call(
        body, name="slot_sum", grid=(r // tb,), in_specs=[pl.BlockSpec((ns, tb, LANES), lambda i: (0, i, 0))],
        out_specs=pl.BlockSpec((tb, LANES), lambda i: (i, 0)), out_shape=SDS((r, LANES), F32),
        compiler_params=_cp("arbitrary"))(slots)


def _adamw(w, gslots, m, v, name):
    lead = ((None,), (0,)) if w.ndim == 3 else ((), ())
    r, cdim = w.shape[-2:]
    ns, rg = gslots.shape[0], gslots.shape[1]
    tb = r if (rg != r or r % 8) else _pick(r, (128, 64, 32, 16, 8))
    bc1, bc2 = 1.0 - ADAM_B1 ** ADAM_STEP, 1.0 - ADAM_B2 ** ADAM_STEP

    def body(w_ref, g_ref, m_ref, v_ref, go_ref, d_ref, mo_ref, vo_ref):
        g = g_ref[0, 0:tb, :].astype(F32)
        for k in range(1, ns):
            g = g + g_ref[k, 0:tb, :].astype(F32)
        mn = ADAM_B1 * m_ref[...] + (1.0 - ADAM_B1) * g
        vn = ADAM_B2 * v_ref[...] + (1.0 - ADAM_B2) * (g * g)
        go_ref[...] = g
        mo_ref[...] = mn
        vo_ref[...] = vn
        d_ref[...] = -ADAM_LR * ((mn / bc1) / (jnp.sqrt(vn / bc2) + ADAM_EPS) + ADAM_WD * w_ref[...])

    blk = pl.BlockSpec(lead[0] + (tb, cdim), lambda i: lead[1] + (i, 0))
    gblk = pl.BlockSpec((ns, tb if rg == r else rg, cdim), lambda i: (0, i, 0))
    return pl.pallas_call(
        body, name=name, grid=(r // tb,), in_specs=[blk, gblk, blk, blk],
        out_specs=[blk] * 4, out_shape=[SDS(w.shape, F32)] * 4, compiler_params=_cp("arbitrary"))(w, gslots, m, v)


def _pack(parts, row_mult):
    flat = jnp.concatenate([p.reshape(-1) for p in parts])
    n = flat.shape[0]
    rows = -(-n // LANES)
    rows = -(-rows // row_mult) * row_mult
    return jnp.pad(flat, (0, rows * LANES - n)).reshape(rows, LANES)


def _unpack(buf, shapes):
    flat = buf.reshape(-1)
    out, off = [], 0
    for s in shapes:
        n = math.prod(s)
        out.append(flat[off:off + n].reshape(s))
        off += n
    return out


def _pad_cols(a, width):
    return jnp.pad(a, ((0, 0), (0, width - a.shape[1])))


def _pad_lanes(a):
    return _pad_cols(a, LANES)


def _up128(n):
    return -(-n // LANES) * LANES


def kernel(x, c, ctx, c_ctx, w_mod, b_mod, norm1_g, w_in, b_gate, conv_qk, head_norm_g, sgu_ln_g, sgu_ln_b, w_s, b_s, w_branch_mlstm, w_branch_sgu, w_out, norm2_g, w_up, w_ffn_conv, w_down, final_g, loss_target, m_c_ctx, m_w_mod, m_b_mod, m_norm1_g, m_w_in, m_b_gate, m_conv_qk, m_head_norm_g, m_sgu_ln_g, m_sgu_ln_b, m_w_s, m_b_s, m_w_branch_mlstm, m_w_branch_sgu, m_w_out, m_norm2_g, m_w_up, m_w_ffn_conv, m_w_down, m_final_g, v_c_ctx, v_w_mod, v_b_mod, v_norm1_g, v_w_in, v_b_gate, v_conv_qk, v_head_norm_g, v_sgu_ln_g, v_sgu_ln_b, v_w_s, v_b_s, v_w_branch_mlstm, v_w_branch_sgu, v_w_out, v_norm2_g, v_w_up, v_w_ffn_conv, v_w_down, v_final_g):
    t, d = x.shape[1], x.shape[2]
    n_ctx = ctx.shape[1]
    s_rows = t + n_ctx
    nh = b_gate.shape[1] // 4
    md = head_norm_g.shape[1]
    dh = md // nh
    ng, sc = w_s.shape[1], w_s.shape[2]
    dff = w_down.shape[1] * N_DEV
    n_in = w_in.shape[2] * N_DEV
    assert md == d and sgu_ln_g.shape[1] == d and n_ctx == LCH and t % LCH == 0 and t % (8 * GRID_W) == 0
    assert n_in == 8 * d + 4 * nh and 4 * nh <= LANES
    me = 4 * lax.axis_index("x") + 2 * lax.axis_index("y") + lax.axis_index("c")

    n_mod, n_insh, n_upsh = w_mod.shape[2], w_in.shape[2], w_up.shape[2]
    p_mod, p_in, p_up = _up128(n_mod), _up128(n_insh), _up128(n_upsh)
    nq, nf = conv_qk.shape[2], w_ffn_conv.shape[3]
    ffn9 = w_ffn_conv[0].reshape(9, nf)
    colpack = jnp.concatenate([_pad_cols(_bf(w_mod[0]), p_mod), _pad_cols(_bf(w_in[0]), p_in)], axis=1)
    convpack = jnp.concatenate([jnp.pad(conv_qk[0], ((0, 13), (0, 0))), jnp.pad(ffn9, ((0, 7), (0, 0)))], axis=1)
    g_col, g_conv = _allgather([colpack, convpack])
    w_mod_f, w_main, w_gate = _assemble_cols(
        g_col, [(0, n_mod, [(0, 0, N_DEV * n_mod, 0)]),
                (p_mod, n_insh, [(1, 0, 3 * md, 0), (2, 3 * md, 4 * nh, 0), (1, 3 * md + 4 * nh, 5 * d, 3 * md)])],
        [N_MOD * d, 8 * d, LANES], "assemble_weights")
    convw, wconv9 = _assemble_cols(g_conv, [(0, nq, [(0, 0, N_DEV * nq, 0)]), (nq, nf, [(1, 0, N_DEV * nf, 0)])],
                                   [N_DEV * nq, N_DEV * nf], "assemble_conv_weights")
    zero = jnp.minimum(jnp.abs(g_conv[0, 0, 0]), 0.0)
    late_w = [_pad_cols(_bf(w_up[0] + zero), p_up), _bf(w_branch_mlstm[0]), _bf(w_branch_sgu[0]), _bf(w_out[0]), _bf(w_down[0])]
    late_state, late_tok = _exchange_start(late_w, False, "late_weights_start")

    cvec = jnp.concatenate([c, c_ctx[None], jnp.zeros((6, d), F32)], axis=0) + late_tok[0:1, 0:1]
    silu_v, mod = _modulation(cvec, w_mod_f, b_mod)
    mx = [mod[0:1, k * d:(k + 1) * d] for k in range(N_MOD)]
    mc = [mod[1:2, k * d:(k + 1) * d] for k in range(2)]
    x2, ctx2 = x[0], ctx[0]
    in_x = _norm_mod_proj(x2, norm1_g, jnp.concatenate([mx[0], mx[1]], axis=0), w_main, w_gate, s_rows, 0, None, "in_proj")
    hn, z_main, zg = _norm_mod_proj(ctx2, norm1_g, jnp.concatenate([mc[0], mc[1]], axis=0), w_main, w_gate, s_rows, t, in_x,
                                    "in_proj_ctx")
    qscale = dh ** -0.5
    qk = _qk_conv(z_main, convw, t, md, qscale)
    bias = _pad_lanes(b_gate)
    fwd = _mlstm_fwd(qk, z_main, zg, bias, nh)
    hf, hb, states_f, states_b = fwd[0], fwd[1], fwd[2:5], fwd[5:8]
    g_up, g_bm, g_bs, g_out, g_down = _exchange_wait(late_state, fwd[4], "late_weights_wait")
    (w_up_f,) = _assemble_cols(g_up, [(0, n_upsh, [(0, 0, 2 * dff, 0)])], [2 * dff], "assemble_w_up")
    wbm_f, wbs_f, wout_f = (g.reshape(d, d) for g in (g_bm, g_bs, g_out))
    w_down_f = g_down.reshape(dff, d)
    b_st = _pad_lanes(b_s[0].T)
    h1, ym, ys, pm, ps, y, out = _mixer_fwd(hf, hb, z_main, x2, head_norm_g, sgu_ln_g, sgu_ln_b, w_s[0], b_st, wbm_f, wbs_f,
                                            wout_f, mx[2], t, nh)
    hn2, ab = _norm_mod_proj(h1, norm2_g, jnp.concatenate([mx[3], mx[4]], axis=0), w_up_f, None, t, 0, None, "up_proj")
    aconv, f, dh2, dffn, st_tail = _ffn_tail(ab, wconv9, w_down_f, h1, mx[5], final_g[None], loss_target[0], dff)

    db, dac = _ffn_bwd_gate(dffn, w_down_f, aconv, ab, dff)
    da, g_wconv9 = _ffn_conv_bwd(dac, ab, wconv9, dff)
    g_wdown = _wgrad(f, dffn, t, "wgrad_down")
    gwup_slots = _scatter_cols([_wgrad(hn2, da, t, "wgrad_up_a"), _wgrad(hn2, db, t, "wgrad_up_b")],
                               [(0, 0, dff, 0), (1, dff, dff, 0)], n_upsh, "scatter_grad_w_up")
    tkf = _pick(dff, (1408, 704, 384, 128))
    dh1, st_n2 = _proj_norm_bwd([(da, 0, w_up_f, 0, dff, tkf), (db, 0, w_up_f, dff, dff, tkf)], h1, 0, norm2_g, mx[4], dh2, t,
                                "up_proj_bwd", (512, 256))
    dz_rest, dhs, dout, dpm, dps, st_mix, g_ws, g_bst = _mixer_bwd(dh1, out, hf, hb, z_main, pm, ps, head_norm_g, sgu_ln_g,
                                                                    sgu_ln_b, w_s[0], b_st, wbm_f, wbs_f, wout_f, mx[2], t, nh)
    g_wout = _wgrad(y, dout, t, "wgrad_out")
    g_wbm = _wgrad(ym, dpm, t, "wgrad_branch_mlstm")
    g_wbs = _wgrad(ys, dps, t, "wgrad_branch_sgu")
    ex_a = [gwup_slots, g_wdown.reshape(N_DEV, dff // N_DEV, d), g_wbm.reshape(N_DEV, d // N_DEV, d),
            g_wbs.reshape(N_DEV, d // N_DEV, d), g_wout.reshape(N_DEV, d // N_DEV, d)]
    ex_a_state, ex_a_tok = _exchange_start(ex_a, True, "grad_exchange_a_start")
    dqkv_f, dg_f, dqkv_b, dg_b = _mlstm_bwd(qk, z_main, zg, bias + ex_a_tok[0:1, :], dhs, hf, hb, states_f, states_b, nh, t)
    dz_qkv, g_convqk = _qkv_conv_bwd(dqkv_f, dqkv_b, z_main, convw, t, md, qscale)
    dz_g, st_gate = _gate_grad_sum(dg_f, dg_b)
    gwin_slots = _scatter_cols(
        [_wgrad(hn, dz_qkv, s_rows, "wgrad_in_qkv"), _wgrad(hn, dz_g, s_rows, "wgrad_in_gate"), _wgrad(hn, dz_rest, t, "wgrad_in_rest")],
        [(0, 0, 3 * md, 0), (1, 3 * md, 4 * nh, 0), (2, 3 * md + 4 * nh, 5 * d, 0)], n_insh, "scatter_grad_w_in")
    gcq_slots = _scatter_cols([g_convqk], [(0, 0, 2 * md, 0)], nq, "scatter_grad_conv_qk")
    gcf_slots = _scatter_cols([g_wconv9], [(0, 0, dff, 0)], nf, "scatter_grad_ffn_conv")
    ex_b_state, ex_b_tok = _exchange_start([gwin_slots, gcq_slots, gcf_slots], True, "grad_exchange_b_start")
    tk = _pick(md, (1024, 512, 256))
    grad_x, st_n1x = _proj_norm_bwd(
        [(dz_qkv, 0, w_main, 0, 3 * md, tk), (dz_rest, 0, w_main, 3 * md, 5 * d, tk), (dz_g, 0, w_gate, 0, LANES, LANES)],
        x2, 0, norm1_g, mx[1] + ex_b_tok[0:1, 0:1], dh1, t, "in_proj_bwd")
    (st_n1c,) = _proj_norm_bwd([(dz_qkv, t, w_main, 0, 3 * md, tk), (dz_g, t, w_gate, 0, LANES, LANES)],
                               ctx2, 0, norm1_g, mc[1] + ex_b_tok[0:1, 0:1], None, n_ctx, "in_proj_bwd_ctx")

    rx_a = _exchange_wait(ex_a_state, st_n1c, "grad_exchange_a_wait")
    rx_b = _exchange_wait(ex_b_state, st_n1c, "grad_exchange_b_wait")
    recv = [rx_b[0], rx_a[0], rx_a[2], rx_a[3], rx_a[4], rx_a[1], rx_b[1], rx_b[2]]
    small_parts = [st_n1x[1], st_n1x[2], st_mix[0], st_n2[1], st_n2[2], st_tail[1],
                   st_n1c[1], st_n1c[2],
                   silu_v[0], st_n1x[0] + st_n1c[0], st_gate[0], st_mix[1], st_mix[2], st_mix[3],
                   g_ws.reshape(-1), g_bst[:, :ng].T.reshape(-1), st_n2[0], st_tail[0]]
    gsmall = _pack(small_parts, 8)
    (recv_small,) = _grad_exchange([], [gsmall])
    small_sum = _slot_sum(recv_small).reshape(-1)
    small_slots = recv_small.reshape(N_DEV, -1)
    o_silu, o_n1 = 8 * d, 9 * d
    ncol = N_MOD * d // N_DEV
    dmc_tot = small_sum[6 * d:8 * d][None]
    dmc_pad = jnp.concatenate([dmc_tot, jnp.zeros((1, 4 * d), F32)], axis=1)
    g_wmod, g_bmod, g_cctx = _mod_grads(
        small_slots[:, o_silu:o_silu + d], lax.dynamic_slice_in_dim(small_slots[:, :6 * d], me * ncol, ncol, axis=1),
        small_slots[:, :6 * d], dmc_tot, lax.dynamic_slice_in_dim(dmc_pad, me * ncol, ncol, axis=1), silu_v[1:2], c_ctx[None],
        w_mod_f[:, :2 * d])

    shard_w = (w_in, w_up, w_branch_mlstm, w_branch_sgu, w_out, w_down, conv_qk)
    shard_m = (m_w_in, m_w_up, m_w_branch_mlstm, m_w_branch_sgu, m_w_out, m_w_down, m_conv_qk)
    shard_v = (v_w_in, v_w_up, v_w_branch_mlstm, v_w_branch_sgu, v_w_out, v_w_down, v_conv_qk)
    shard_names = ("w_in", "w_up", "w_branch_mlstm", "w_branch_sgu", "w_out", "w_down", "conv_qk")
    shard_out = [_adamw(wa, recv[k], ma, va, "adamw_" + nm)
                 for k, (wa, ma, va, nm) in enumerate(zip(shard_w, shard_m, shard_v, shard_names))]
    shard_out.append([b.reshape(w_ffn_conv.shape) for b in
                      _adamw(ffn9, recv[7], m_w_ffn_conv[0].reshape(9, nf), v_w_ffn_conv[0].reshape(9, nf), "adamw_w_ffn_conv")])
    mod_out = _adamw(w_mod, g_wmod, m_w_mod, v_w_mod, "adamw_w_mod")

    def rep(cc, bm, n1, bg, hg, lg, lb, ws, bs, n2, fg):
        return [cc.reshape(-1), bm.reshape(-1), n1.reshape(-1), _pad_lanes(bg.reshape(1, -1)).reshape(-1), hg.reshape(-1),
                lg.reshape(-1), lb.reshape(-1), ws.reshape(-1), bs.reshape(-1), n2.reshape(-1), fg.reshape(-1)]

    o = o_n1
    g_rep_parts = [g_cctx, g_bmod]
    for n in (d, LANES, d, d, d, ng * sc * sc, ng * sc, d, d):
        g_rep_parts.append(small_sum[o:o + n])
        o += n
    rep_shapes = [(d,), (1, N_MOD * d), (1, d), (1, LANES), (1, d), (1, d), (1, d), (1, ng, sc, sc), (1, ng, sc), (1, d), (d,)]
    rep_out = _adamw(
        _pack(rep(c_ctx, b_mod, norm1_g, b_gate, head_norm_g, sgu_ln_g, sgu_ln_b, w_s, b_s, norm2_g, final_g), 8),
        _pack(g_rep_parts, 8)[None],
        _pack(rep(m_c_ctx, m_b_mod, m_norm1_g, m_b_gate, m_head_norm_g, m_sgu_ln_g, m_sgu_ln_b, m_w_s, m_b_s, m_norm2_g, m_final_g), 8),
        _pack(rep(v_c_ctx, v_b_mod, v_norm1_g, v_b_gate, v_head_norm_g, v_sgu_ln_g, v_sgu_ln_b, v_w_s, v_b_s, v_norm2_g, v_final_g), 8),
        "adamw_replicated")

    def assemble(k):
        r = _unpack(rep_out[k], rep_shapes)
        s = [o[k] for o in shard_out]
        return [r[0], mod_out[k], r[1], r[2], s[0], r[3][:, :4 * nh], s[6], r[4], r[5], r[6], r[7], r[8], s[2], s[3], s[4], r[9],
                s[1], s[7], s[5], r[10]]

    loss = lax.psum(st_tail[2, 0], ("x", "y", "c"))
    outs = [loss, grad_x[None]]
    for k in range(4):
        outs += assemble(k)
    return tuple(outs)
```

```python
import math

import jax
import jax.numpy as jnp
from jax import lax
from jax.experimental import pallas as pl
from jax.experimental.pallas import tpu as pltpu

F32, BF16 = jnp.float32, jnp.bfloat16
EPS = 1e-6
M_INIT = -1e30
NEG = -1e30
GRID_W = 64
LCH = 256
N_MOD = 6
N_DEV = 8
LANES = 128
ADAM_LR, ADAM_B1, ADAM_B2, ADAM_EPS, ADAM_WD, ADAM_STEP = 0.001, 0.9, 0.999, 1e-08, 0.01, 10
GELU_C = math.sqrt(2.0 / math.pi)
GELU_A = 0.044715
VMEM_LIMIT = 56 * 1024 * 1024
HI = lax.Precision.HIGHEST
SDS = jax.ShapeDtypeStruct
MESH_ID = pl.DeviceIdType.MESH


def _pick(n, cands):
    for c in cands:
        if n % c == 0:
            return c
    raise ValueError(f"no block size for {n} in {cands}")


def _cp(*sem):
    return pltpu.CompilerParams(dimension_semantics=sem if sem else None, vmem_limit_bytes=VMEM_LIMIT)


def _sigmoid(x):
    return 0.5 * jnp.tanh(0.5 * x) + 0.5


def _split3(x):
    hi = x.astype(BF16)
    r = x - hi.astype(F32)
    mid = r.astype(BF16)
    return hi, mid, (r - mid.astype(F32)).astype(BF16)


def _mask_dot(mask_b, x):
    hi, mid, lo = _split3(x)
    return (_dot(mask_b, lo) + _dot(mask_b, mid)) + _dot(mask_b, hi)


def _mask_dot_t(mask_b, x):
    hi, mid, lo = _split3(x)
    return (_dot_tn(mask_b, lo) + _dot_tn(mask_b, mid)) + _dot_tn(mask_b, hi)


def _gelu(x):
    return x * (0.5 * (1.0 + jnp.tanh(GELU_C * x * (1.0 + GELU_A * (x * x)))))


def _gelu_and_grad(x):
    x2 = x * x
    t = jnp.tanh(GELU_C * x * (1.0 + GELU_A * x2))
    half = 0.5 * (1.0 + t)
    return x * half, half + (0.5 * GELU_C) * x * (1.0 - t * t) * (1.0 + 3.0 * GELU_A * x2)


def _log_sigmoid(x):
    return jnp.minimum(x, 0.0) - jnp.log(1.0 + jnp.exp(-jnp.abs(x)))


def _dot(a, b):
    return jnp.dot(a, b, preferred_element_type=F32)


def _dot_nt(a, b):
    return lax.dot_general(a, b, (((1,), (1,)), ((), ())), preferred_element_type=F32)


def _dot_tn(a, b):
    return lax.dot_general(a, b, (((0,), (0,)), ((), ())), preferred_element_type=F32)


def _bf(x):
    return x.astype(BF16)


def _allgather(arrs):
    na = len(arrs)

    def body(*refs):
        x_refs, o_refs = refs[:na], refs[na:2 * na]
        send_sems, recv_sems, local_sems = refs[2 * na:]
        x, y, c = lax.axis_index("x"), lax.axis_index("y"), lax.axis_index("c")
        me, sibling = (x, y, c), (x, y, 1 - c)
        chips = [(1 - x, y), (x, 1 - y), (1 - x, 1 - y)]

        def copy(a, k, block, to, src=None):
            slot = o_refs[a].at[4 * block[0] + 2 * block[1] + block[2]]
            return pltpu.make_async_remote_copy(
                src_ref=slot if src is None else src, dst_ref=slot, send_sem=send_sems.at[7 * a + k],
                recv_sem=recv_sems.at[7 * a + k], device_id=to, device_id_type=MESH_ID)

        mine = [pltpu.make_async_copy(x_refs[a], o_refs[a].at[4 * x + 2 * y + c], local_sems.at[a]) for a in range(na)]
        for cp in mine:
            cp.start()
        first = []
        for a in range(na):
            first.append(copy(a, 0, me, sibling, src=x_refs[a]))
            first += [copy(a, 1 + j, me, (*chip, c), src=x_refs[a]) for j, chip in enumerate(chips)]
        for cp in first:
            cp.start()
        passed = []
        for j, chip in enumerate(chips):
            for a in range(na):
                copy(a, 1 + j, (*chip, c), me).wait_recv()
                passed.append(copy(a, 4 + j, (*chip, c), sibling))
                passed[-1].start()
        for a in range(na):
            copy(a, 0, sibling, me).wait_recv()
            for j, chip in enumerate(chips):
                copy(a, 4 + j, (*chip, 1 - c), me).wait_recv()
        for cp in first + passed:
            cp.wait_send()
        for cp in mine:
            cp.wait()

    anyspec = pl.BlockSpec(memory_space=pl.ANY)
    return pl.pallas_call(
        body, name="weights_allgather",
        out_shape=[SDS((N_DEV,) + a.shape, a.dtype) for a in arrs],
        in_specs=[anyspec] * na, out_specs=[anyspec] * na,
        scratch_shapes=[pltpu.SemaphoreType.DMA((7 * na,)), pltpu.SemaphoreType.DMA((7 * na,)), pltpu.SemaphoreType.DMA((na,))],
    )(*arrs)


def _grad_exchange(per_dest, shared):
    nd, ns = len(per_dest), len(shared)
    na = nd + ns

    def body(*refs):
        in_refs, out_refs = refs[:na], refs[na:2 * na]
        send_sems, recv_sems, local_sems = refs[2 * na:]
        x, y, c = lax.axis_index("x"), lax.axis_index("y"), lax.axis_index("c")
        me = 4 * x + 2 * y + c

        def src(a, idx):
            return in_refs[a].at[idx] if a < nd else in_refs[a]

        loc = [pltpu.make_async_copy(src(a, me), out_refs[a].at[me], local_sems.at[a]) for a in range(na)]
        for cp in loc:
            cp.start()
        sends, recvs = [], []
        for k in range(1, N_DEV):
            px = 1 - x if k & 4 else x
            py = 1 - y if k & 2 else y
            pc = 1 - c if k & 1 else c
            peer, pidx = (px, py, pc), 4 * px + 2 * py + pc
            for a in range(na):
                sem = 7 * a + k - 1
                sends.append(pltpu.make_async_remote_copy(
                    src_ref=src(a, pidx), dst_ref=out_refs[a].at[me], send_sem=send_sems.at[sem],
                    recv_sem=recv_sems.at[sem], device_id=peer, device_id_type=MESH_ID))
                recvs.append(pltpu.make_async_remote_copy(
                    src_ref=src(a, pidx), dst_ref=out_refs[a].at[pidx], send_sem=send_sems.at[sem],
                    recv_sem=recv_sems.at[sem], device_id=peer, device_id_type=MESH_ID))
        for cp in sends:
            cp.start()
        for cp in recvs:
            cp.wait_recv()
        for cp in sends:
            cp.wait_send()
        for cp in loc:
            cp.wait()

    anyspec = pl.BlockSpec(memory_space=pl.ANY)
    return pl.pallas_call(
        body, name="grad_exchange",
        out_shape=[SDS(a.shape, a.dtype) for a in per_dest] + [SDS((N_DEV,) + a.shape, a.dtype) for a in shared],
        in_specs=[anyspec] * na, out_specs=[anyspec] * na,
        scratch_shapes=[pltpu.SemaphoreType.DMA((7 * na,)), pltpu.SemaphoreType.DMA((7 * na,)), pltpu.SemaphoreType.DMA((na,))],
    )(*per_dest, *shared)


_HBM_SPEC = pl.BlockSpec(memory_space=pltpu.HBM)
_SEM_SPEC = pl.BlockSpec(memory_space=pltpu.SEMAPHORE)
_EFFECT = pltpu.SideEffectType.DATAFLOW_SIDE_EFFECTING


def _peer_list(x, y, c):
    out = []
    for k in range(1, N_DEV):
        px = 1 - x if k & 4 else x
        py = 1 - y if k & 2 else y
        pc = 1 - c if k & 1 else c
        out.append(((px, py, pc), 4 * px + 2 * py + pc))
    return out


def _split_copies(src, land, send_sems, recv_sems, per_dest, receive):
    x, y, c = lax.axis_index("x"), lax.axis_index("y"), lax.axis_index("c")
    me = 4 * x + 2 * y + c
    out = []
    for k, (peer, pidx) in enumerate(_peer_list(x, y, c)):
        for a in range(len(src)):
            out.append(pltpu.make_async_remote_copy(
                src_ref=src[a].at[pidx] if per_dest else src[a], dst_ref=land[a].at[pidx if receive else me],
                send_sem=send_sems.at[7 * a + k], recv_sem=recv_sems.at[7 * a + k], device_id=peer, device_id_type=MESH_ID))
    return out


def _own_copies(src, land, own_sems, per_dest):
    me = 4 * lax.axis_index("x") + 2 * lax.axis_index("y") + lax.axis_index("c")
    return [pltpu.make_async_copy(src[a].at[me] if per_dest else src[a], land[a].at[me], own_sems.at[a]) for a in range(len(src))]


def _exchange_start(arrs, per_dest, name):
    na = len(arrs)
    land_shapes = [a.shape if per_dest else (N_DEV,) + a.shape for a in arrs]
    lands = [pltpu.with_memory_space_constraint(lax.empty(s, a.dtype), pltpu.HBM) for s, a in zip(land_shapes, arrs)]

    def body(*refs):
        src, land = refs[:na], refs[na:2 * na]
        send_sems, recv_sems, own_sems, token = refs[2 * na], refs[2 * na + 1], refs[2 * na + 2], refs[-1]
        for cp in _split_copies(src, land, send_sems, recv_sems, per_dest, False) + _own_copies(src, land, own_sems, per_dest):
            cp.start()
        token[...] = jnp.zeros_like(token)

    outs = pl.pallas_call(
        body, name=name,
        out_shape=[pltpu.SemaphoreType.DMA((7 * na,)), pltpu.SemaphoreType.DMA((7 * na,)), pltpu.SemaphoreType.DMA((na,))]
        + [pltpu.HBM(a.shape, a.dtype) for a in arrs] + [pltpu.HBM(s, a.dtype) for s, a in zip(land_shapes, arrs)]
        + [SDS((8, LANES), F32)],
        in_specs=[_HBM_SPEC] * (2 * na),
        out_specs=[_SEM_SPEC] * 3 + [_HBM_SPEC] * (2 * na) + [pl.BlockSpec(memory_space=pltpu.VMEM)],
        input_output_aliases={k: 3 + k for k in range(2 * na)},
        compiler_params=pltpu.CompilerParams(has_side_effects=_EFFECT),
    )(*[pltpu.with_memory_space_constraint(a, pltpu.HBM) for a in arrs], *lands)
    return (na, per_dest, outs[:-1]), outs[-1]


def _exchange_wait(state, after, name):
    na, per_dest, started = state

    def body(*refs):
        src, land = refs[:na], refs[na:2 * na]
        send_sems, recv_sems, own_sems = refs[2 * na], refs[2 * na + 1], refs[2 * na + 2]
        for cp in _split_copies(src, land, send_sems, recv_sems, per_dest, True):
            cp.wait_send()
            cp.wait_recv()
        for cp in _own_copies(src, land, own_sems, per_dest):
            cp.wait()

    bufs = started[3:]
    outs = pl.pallas_call(
        body, name=name,
        out_shape=[pltpu.HBM(b.shape, b.dtype) for b in bufs],
        in_specs=[_HBM_SPEC] * (2 * na) + [_SEM_SPEC] * 3 + [pl.BlockSpec(memory_space=pl.ANY)],
        out_specs=[_HBM_SPEC] * (2 * na),
        input_output_aliases={k: k for k in range(2 * na)},
        compiler_params=pltpu.CompilerParams(has_side_effects=_EFFECT),
    )(*bufs, started[0], started[1], started[2], after)
    return outs[na:]


def _col_pieces(n, segments):
    out = []
    for j in range(N_DEV):
        lo, hi = j * n, (j + 1) * n
        for (k, s0, w, c0) in segments:
            a, b = max(lo, s0), min(hi, s0 + w)
            if a < b:
                out.append((j, a - lo, b - lo, k, c0 + a - s0, c0 + b - s0))
    return out


def _assemble_cols(slots, groups, out_widths, name):
    r, p = slots.shape[1], slots.shape[2]
    tb = _pick(r, (128, 64, 32, 16, 8))
    covered = [0] * len(out_widths)
    for (_, n, segs) in groups:
        for (k, _, w, _) in segs:
            covered[k] += w

    def body(s_ref, *o_refs):
        for k, wd in enumerate(out_widths):
            if covered[k] < wd:
                o_refs[k][...] = jnp.zeros_like(o_refs[k])
        for (off, n, segs) in groups:
            for (j, a0, a1, k, d0, d1) in _col_pieces(n, segs):
                o_refs[k][:, d0:d1] = s_ref[j, :, off + a0:off + a1]

    return pl.pallas_call(
        body, name=name, grid=(r // tb,), in_specs=[pl.BlockSpec((N_DEV, tb, p), lambda i: (0, i, 0))],
        out_specs=[pl.BlockSpec((tb, w), lambda i: (i, 0)) for w in out_widths],
        out_shape=[SDS((r, w), slots.dtype) for w in out_widths], compiler_params=_cp("arbitrary"))(slots)


def _scatter_cols(pieces, segments, n, name):
    r = pieces[0].shape[0]
    tb = _pick(r, (128, 64, 32, 16, 8))

    def body(*refs):
        p_refs, o_ref = refs[:-1], refs[-1]
        for (j, a0, a1, k, d0, d1) in _col_pieces(n, segments):
            o_ref[j, :, a0:a1] = p_refs[k][:, d0:d1]

    return pl.pallas_call(
        body, name=name, grid=(r // tb,), in_specs=[pl.BlockSpec((tb, a.shape[1]), lambda i: (i, 0)) for a in pieces],
        out_specs=pl.BlockSpec((N_DEV, tb, n), lambda i: (0, i, 0)), out_shape=SDS((N_DEV, r, n), pieces[0].dtype),
        compiler_params=_cp("arbitrary"))(*pieces)


def _modulation(cvec, w_mod, b_mod):
    d, n = w_mod.shape

    def body(c_ref, w_ref, b_ref, s_ref, o_ref):
        cv = c_ref[...]
        s = cv * _sigmoid(cv)
        s_ref[...] = s
        o_ref[...] = _dot(_bf(s), w_ref[...]) + b_ref[...]

    return pl.pallas_call(body, name="modulation", out_shape=(SDS((8, d), F32), SDS((8, n), F32)),
                          compiler_params=_cp())(cvec, w_mod, b_mod)


def _norm_mod_proj(x_arr, g, shsc, w_main, w_gate, rows_total, row0, filled, name):
    m_rows, d = x_arr.shape
    n = w_main.shape[1]
    tb = _pick(m_rows, (1024, 256))
    cb = _pick(n, (2048, 1408, 1024, 768, 512, 384, 256, 128))
    gate = w_gate is not None
    nout = 3 if gate else 2
    nin = 5 if gate else 4
    rb = row0 // tb

    def body(*refs):
        x_ref, g_ref, ss_ref, wm_ref = refs[:4]
        wg_ref = refs[4] if gate else None
        outs = refs[len(refs) - 1 - nout:len(refs) - 1]
        hn_ref, z_ref = outs[0], outs[1]
        hn_sc = refs[-1]

        @pl.when(pl.program_id(1) == 0)
        def _():
            x = x_ref[...]
            r = lax.rsqrt(jnp.mean(x * x, axis=-1, keepdims=True) + EPS)
            hb = _bf((x * r * g_ref[...]) * (1.0 + ss_ref[1:2, :]) + ss_ref[0:1, :])
            hn_sc[...] = hb
            hn_ref[...] = hb
            if gate:
                outs[2][...] = _dot(hb, wg_ref[...])

        z_ref[...] = _bf(_dot(hn_sc[...], wm_ref[:, pl.ds(pl.multiple_of(pl.program_id(1) * cb, cb), cb)]))

    in_specs = [pl.BlockSpec((tb, d), lambda i, j: (i, 0)), pl.BlockSpec((1, d), lambda i, j: (0, 0)),
                pl.BlockSpec((2, d), lambda i, j: (0, 0)), _resident((d, n))]
    out_specs = [pl.BlockSpec((tb, d), lambda i, j: (rb + i, 0)), pl.BlockSpec((tb, cb), lambda i, j: (rb + i, j))]
    out_shape = [SDS((rows_total, d), BF16), SDS((rows_total, n), BF16)]
    args = [x_arr, g, shsc, w_main]
    if gate:
        in_specs.append(pl.BlockSpec((d, LANES), lambda i, j: (0, 0)))
        out_specs.append(pl.BlockSpec((tb, LANES), lambda i, j: (rb + i, 0)))
        out_shape.append(SDS((rows_total, LANES), F32))
        args.append(w_gate)
    aliases = {}
    if filled is not None:
        in_specs += [pl.BlockSpec(memory_space=pl.ANY)] * nout
        args += list(filled)
        aliases = {nin + k: k for k in range(nout)}
    return pl.pallas_call(
        body, name=name, grid=(m_rows // tb, n // cb), in_specs=in_specs, out_specs=out_specs, out_shape=out_shape,
        input_output_aliases=aliases, scratch_shapes=[pltpu.VMEM((tb, d), BF16)],
        compiler_params=_cp("arbitrary", "arbitrary"))(*args)


def _seg_masks(row, t_rows, s_rows):
    prev_ok = (row != 0) & (row != t_rows)
    next_ok = (row != t_rows - 1) & (row != s_rows - 1)
    return prev_ok, next_ok


def _shift_rows(z, halo_prev, halo_next, tb):
    loc = lax.broadcasted_iota(jnp.int32, (tb, 1), 0)
    zp = jnp.where(loc == 0, halo_prev, pltpu.roll(z, 1, 0))
    zn = jnp.where(loc == tb - 1, halo_next, pltpu.roll(z, tb - 1, 0))
    return zp, zn


def _qk_conv(z_main, conv_w, t_rows, md, qscale):
    s_rows = z_main.shape[0]
    tb = _pick(s_rows, (1280, 1024, 256))
    cb = _pick(md, (512, 256, 128))
    nb8 = tb // 8

    def body(zm, zp, zn, w_ref, o_ref):
        i, j = pl.program_id(0), pl.program_id(1)
        z = zm[...].astype(F32)
        zprev, znext = _shift_rows(z, zp[7:8, :].astype(F32), zn[0:1, :].astype(F32), tb)
        row = i * tb + lax.broadcasted_iota(jnp.int32, (tb, 1), 0)
        prev_ok, next_ok = _seg_masks(row, t_rows, s_rows)
        pre = (w_ref[0:1, :] * jnp.where(prev_ok, zprev, 0.0) + w_ref[1:2, :] * z
               + w_ref[2:3, :] * jnp.where(next_ok, znext, 0.0))
        scale = jnp.where(j * cb < md, qscale, 1.0)
        o_ref[...] = _bf(pre * _sigmoid(pre) * scale)

    return pl.pallas_call(
        body, name="qk_conv", grid=(s_rows // tb, 2 * md // cb),
        in_specs=[pl.BlockSpec((tb, cb), lambda i, j: (i, j)),
                  pl.BlockSpec((8, cb), lambda i, j: (jnp.maximum(i * nb8 - 1, 0), j)),
                  pl.BlockSpec((8, cb), lambda i, j: (jnp.minimum((i + 1) * nb8, s_rows // 8 - 1), j)),
                  pl.BlockSpec((8, cb), lambda i, j: (0, j))],
        out_specs=pl.BlockSpec((tb, cb), lambda i, j: (i, j)),
        out_shape=SDS((s_rows, 2 * md), BF16), compiler_params=_cp("arbitrary", "arbitrary"))(z_main, z_main, z_main, conv_w)


def _chunk_gates(gates, bias, rev):
    ln = gates.shape[0]
    gz = gates + bias
    logf = _log_sigmoid(gz)
    r_id = lax.broadcasted_iota(jnp.int32, (ln, ln), 0)
    c_id = lax.broadcasted_iota(jnp.int32, (ln, ln), 1)
    mask = (c_id >= r_id) if rev else (c_id <= r_id)
    mb = mask.astype(F32).astype(BF16)
    b_all = _mask_dot(mb, logf)
    g_all = jnp.sum(logf, axis=0, keepdims=True)
    return gz, b_all, b_all.T, gz.T, g_all, mask, mb


def _head_weights(b_col, b_row, i_row, m_in, mask):
    d = jnp.where(mask, b_col - b_row + i_row, NEG)
    inter = b_col + m_in
    m_row = jnp.maximum(inter, jnp.max(d, axis=1, keepdims=True))
    return jnp.exp(d - m_row), jnp.exp(inter - m_row), m_row


def _head_state_coeffs(g, b_col, i_col, m_in):
    a = g - b_col + i_col
    m_new = jnp.maximum(g + m_in, jnp.max(a, axis=0, keepdims=True))
    return jnp.exp(g + m_in - m_new), jnp.exp(a - m_new), m_new


def _mlstm_fwd(qk, z_main, zg, bias, nh):
    s_rows = qk.shape[0]
    md = qk.shape[1] // 2
    dh = md // nh
    nc = s_rows // LCH
    ln = LCH

    def chunk_f(i):
        return jnp.where(i == 0, nc - 1, i - 1)

    def chunk_b(i):
        return jnp.where(i == 0, nc - 1, nc - 1 - i)

    def body(qf, kf, vf, gf, qb, kb, vb, gb, bias_ref, hf_ref, hb_ref, cf_ref, nf_ref, mf_ref, cb_ref, nb_ref, mb_ref,
             c_sc, n_sc, m_sc):
        i = pl.program_id(0)

        @pl.when(i == 0)
        def _():
            c_sc[...] = jnp.zeros_like(c_sc)
            n_sc[...] = jnp.zeros_like(n_sc)
            m_sc[...] = jnp.full(m_sc.shape, M_INIT, F32)

        sides = ((qf, kf, vf, gf, hf_ref, cf_ref, nf_ref, mf_ref), (qb, kb, vb, gb, hb_ref, cb_ref, nb_ref, mb_ref))
        gates = [_chunk_gates(s[3][...], bias_ref[...], dr == 1) for dr, s in enumerate(sides)]
        units = []
        for dr, (q_ref, k_ref, v_ref, _, h_ref, c_out, n_out, m_out) in enumerate(sides):
            gz, b_all, b_t, g_t, g_all, mask, _ = gates[dr]
            for h in range(nh):
                ci, cf = 2 * dr * nh + h, (2 * dr + 1) * nh + h
                sl = slice(h * dh, (h + 1) * dh)
                u = dict(dr=dr, h=h, sl=sl, h_ref=h_ref, q=q_ref[:, sl], k=k_ref[:, sl], v=v_ref[:, sl],
                         c_in=c_sc[dr, h], n_in=n_sc[dr, h, 0:1, :], m_in=m_sc[dr, h, 0:1, 0:1],
                         b_col=b_all[:, cf:cf + 1], i_col=gz[:, ci:ci + 1], g=g_all[:, cf:cf + 1])
                c_out[sl, :] = u["c_in"]
                n_out[:, sl] = n_sc[dr, h]
                m_out[h] = m_sc[dr, h]
                u["w"], u["w_int"], u["m_row"] = _head_weights(u["b_col"], b_t[cf:cf + 1, :], g_t[ci:ci + 1, :], u["m_in"], mask)
                u["qk"] = _dot_nt(u["q"], u["k"])
                units.append(u)
        for u in units:
            u["s_mat"] = u["qk"] * u["w"]
            u["qc"] = _dot(u["q"], _bf(u["c_in"]))
            u["a_old"], u["coef"], u["m_new"] = _head_state_coeffs(u["g"], u["b_col"], u["i_col"], u["m_in"])
            u["kw"] = u["k"].astype(F32) * u["coef"]
        for u in units:
            u["sv"] = _dot(_bf(u["s_mat"]), u["v"])
            u["kv"] = _dot_tn(_bf(u["kw"]), u["v"])
        for u in units:
            dr, h = u["dr"], u["h"]
            num = u["sv"] + u["w_int"] * u["qc"]
            den = (jnp.sum(u["s_mat"], axis=1, keepdims=True)
                   + u["w_int"] * jnp.sum(u["q"].astype(F32) * u["n_in"], axis=1, keepdims=True))
            u["h_ref"][:, u["sl"]] = _bf(num / jnp.maximum(jnp.abs(den), jnp.exp(-u["m_row"])))
            c_sc[dr, h] = u["a_old"] * u["c_in"] + u["kv"]
            n_sc[dr, h] = jnp.broadcast_to(u["a_old"] * u["n_in"] + jnp.sum(u["kw"], axis=0, keepdims=True), (8, dh))
            m_sc[dr, h] = jnp.broadcast_to(u["m_new"], (8, LANES))

    def tok(cfn, col):
        return pl.BlockSpec((ln, md), lambda i: (cfn(i), col))

    def gat(cfn):
        return pl.BlockSpec((ln, LANES), lambda i: (cfn(i), 0))

    def st(cfn, shape):
        return pl.BlockSpec((None,) + shape, lambda i: (cfn(i),) + (0,) * len(shape))

    st_shapes = ((nh * dh, dh), (8, md), (nh, 8, LANES))
    return pl.pallas_call(
        body, name="mlstm_fwd", grid=(nc,),
        in_specs=[tok(chunk_f, 0), tok(chunk_f, 1), tok(chunk_f, 2), gat(chunk_f),
                  tok(chunk_b, 0), tok(chunk_b, 1), tok(chunk_b, 2), gat(chunk_b),
                  pl.BlockSpec((1, LANES), lambda i: (0, 0))],
        out_specs=[tok(chunk_f, 0), tok(chunk_b, 0)] + [st(chunk_f, s) for s in st_shapes] + [st(chunk_b, s) for s in st_shapes],
        out_shape=[SDS((s_rows, md), BF16)] * 2 + [SDS((nc,) + s, F32) for s in st_shapes] * 2,
        scratch_shapes=[pltpu.VMEM((2, nh, dh, dh), F32), pltpu.VMEM((2, nh, 8, dh), F32), pltpu.VMEM((2, nh, 8, LANES), F32)],
        compiler_params=_cp("arbitrary"))(qk, qk, z_main, zg, qk, qk, z_main, zg, bias)


def _head_rms(hs, nh, dh):
    parts, scales = [], []
    for h in range(nh):
        hh = hs[:, h * dh:(h + 1) * dh]
        r = lax.rsqrt(jnp.mean(hh * hh, axis=-1, keepdims=True) + EPS)
        parts.append(hh * r)
        scales.append(r)
    return jnp.concatenate(parts, axis=1), scales


def _layer_norm(v):
    vc = v - jnp.mean(v, axis=-1, keepdims=True)
    r = lax.rsqrt(jnp.mean(vc * vc, axis=-1, keepdims=True) + EPS)
    return vc * r, r


def _sgu_mix(vnb, ws_ref, bs_ref, tb, ng, gd, sc):
    rows = []
    for ch in range(tb // sc):
        cols = []
        for g in range(ng):
            blk = vnb[ch * sc:(ch + 1) * sc, g * gd:(g + 1) * gd]
            cols.append(_dot(_bf(ws_ref[g]), blk) + bs_ref[:, g:g + 1])
        rows.append(jnp.concatenate(cols, axis=1))
    return jnp.concatenate(rows, axis=0)


def _mixer_fwd(hf, hb, z_main, xs, hg, lng, lnb, w_s, b_st, wbm, wbs, wout, mx2, t_rows, nh):
    d = xs.shape[1]
    ng, sc = w_s.shape[0], w_s.shape[1]
    dh, gd = d // nh, d // ng
    tb = _pick(t_rows, (256,))

    def body(hf_ref, hb_ref, zo, zu, zv, zgm, zgg, x_ref, hg_ref, lng_ref, lnb_ref, ws_ref, bs_ref, wbm_ref, wbs_ref,
             wo_ref, mx2_ref, h1_ref, ym_ref, ys_ref, pm_ref, ps_ref, y_ref, out_ref):
        hs = hf_ref[...].astype(F32) + hb_ref[...].astype(F32)
        hn, _ = _head_rms(hs, nh, dh)
        ym = _bf(_sigmoid(zo[...].astype(F32)) * (hn * hg_ref[...]))
        ym_ref[...] = ym
        vhat, _ = _layer_norm(_gelu(zv[...].astype(F32)))
        vnb = _bf(vhat * lng_ref[...] + lnb_ref[...])
        ys = _bf(_gelu(zu[...].astype(F32)) * _sgu_mix(vnb, ws_ref, bs_ref, tb, ng, gd, sc))
        ys_ref[...] = ys
        pm = _dot(ym, wbm_ref[...])
        ps = _dot(ys, wbs_ref[...])
        pm_ref[...] = _bf(pm)
        ps_ref[...] = _bf(ps)
        y = _bf(_sigmoid(zgm[...].astype(F32)) * pm + _sigmoid(zgg[...].astype(F32)) * ps)
        y_ref[...] = y
        out = _dot(y, wo_ref[...])
        out_ref[...] = _bf(out)
        h1_ref[...] = x_ref[...] + mx2_ref[...] * out

    def tok(col):
        return pl.BlockSpec((tb, d), lambda i: (i, col))

    def full(shape):
        return pl.BlockSpec(shape, lambda i: (0,) * len(shape))

    return pl.pallas_call(
        body, name="mixer_fwd", grid=(t_rows // tb,),
        in_specs=[tok(0), tok(0), tok(3), tok(4), tok(5), tok(6), tok(7), tok(0), full((1, d)), full((1, d)), full((1, d)),
                  full((ng, sc, sc)), full((sc, LANES)), full((d, d)), full((d, d)), full((d, d)), full((1, d))],
        out_specs=[tok(0)] * 7,
        out_shape=[SDS((t_rows, d), F32)] + [SDS((t_rows, d), BF16)] * 6,
        compiler_params=_cp("arbitrary"))(hf, hb, z_main, z_main, z_main, z_main, z_main, xs, hg, lng, lnb, w_s, b_st,
                                          wbm, wbs, wout, mx2)


def _resident(shape):
    return pl.BlockSpec(shape, lambda *_: (0,) * len(shape), pipeline_mode=pl.Buffered(1))


def _grid_taps(a_ext, n_ext):
    col = lax.broadcasted_iota(jnp.int32, (n_ext, 1), 0) % GRID_W
    left = jnp.where(col != 0, pltpu.roll(a_ext, 1, 0), 0.0)
    right = jnp.where(col != GRID_W - 1, pltpu.roll(a_ext, n_ext - 1, 0), 0.0)
    return left, right


def _with_halo(prev, main, nxt, i, ni, tb):
    ext = jnp.concatenate([prev, main, nxt], axis=0).astype(F32)
    pos = lax.broadcasted_iota(jnp.int32, (tb + 2 * GRID_W, 1), 0)
    inside = ((pos >= GRID_W) | (i > 0)) & ((pos < tb + GRID_W) | (i < ni - 1))
    return jnp.where(inside, ext, 0.0)


def _halo_specs(tb, cb, t_rows, col0=0):
    nh64 = tb // GRID_W
    return [pl.BlockSpec((tb, cb), lambda i, j: (i, col0 + j)),
            pl.BlockSpec((GRID_W, cb), lambda i, j: (jnp.maximum(i * nh64 - 1, 0), col0 + j)),
            pl.BlockSpec((GRID_W, cb), lambda i, j: (jnp.minimum((i + 1) * nh64, t_rows // GRID_W - 1), col0 + j))]


def _ffn_tail(ab, w_conv9, w_down, h1, mx5, gfin, target, dff):
    t_rows, d = h1.shape
    tb = _pick(t_rows, (256,))
    cb = _pick(dff, (1408, 256, 128))
    ni, nj = t_rows // tb, dff // cb
    n_ext = tb + 2 * GRID_W

    def body(am, ap, an, b_ref, wc_ref, wd_ref, h1_ref, mx5_ref, gf_ref, tg_ref, ac_ref, f_ref, dh2_ref, dffn_ref, st_ref, acc):
        i, j = pl.program_id(0), pl.program_id(1)
        a_ext = _with_halo(ap[...], am[...], an[...], i, ni, tb)
        left, right = _grid_taps(a_ext, n_ext)
        conv = jnp.zeros((tb, cb), F32)
        for di in range(3):
            o = di * GRID_W
            conv = conv + (wc_ref[3 * di:3 * di + 1, :] * left[o:o + tb] + wc_ref[3 * di + 1:3 * di + 2, :] * a_ext[o:o + tb]
                           + wc_ref[3 * di + 2:3 * di + 3, :] * right[o:o + tb])
        ac_ref[...] = _bf(conv)
        fb = _bf(conv * _sigmoid(conv) * b_ref[...].astype(F32))
        f_ref[...] = fb

        @pl.when(j == 0)
        def _():
            acc[...] = jnp.zeros_like(acc)

        @pl.when((i == 0) & (j == 0))
        def _():
            st_ref[...] = jnp.zeros_like(st_ref)

        acc[...] += _dot(fb, wd_ref[pl.ds(pl.multiple_of(j * cb, cb), cb), :])

        @pl.when(j == nj - 1)
        def _():
            ffn = acc[...]
            h2 = h1_ref[...] + mx5_ref[...] * ffn
            r = lax.rsqrt(jnp.mean(h2 * h2, axis=-1, keepdims=True) + EPS)
            xn = h2 * r
            e = xn * gf_ref[...] - tg_ref[...]
            loss = 0.5 * jnp.sum(jnp.sum(e * e, axis=1, keepdims=True), axis=0, keepdims=True) / d
            dy = e * (1.0 / d)
            dxn = dy * gf_ref[...]
            dh2 = r * (dxn - xn * jnp.mean(dxn * xn, axis=-1, keepdims=True))
            dh2_ref[...] = dh2
            dffn_ref[...] = _bf(dh2 * mx5_ref[...])
            st_ref[...] += jnp.concatenate(
                [jnp.sum(dy * xn, axis=0, keepdims=True), jnp.sum(dh2 * ffn, axis=0, keepdims=True),
                 jnp.broadcast_to(loss, (1, d)), jnp.zeros((5, d), F32)], axis=0)

    def tokd():
        return pl.BlockSpec((tb, d), lambda i, j: (i, 0))

    def rowd():
        return pl.BlockSpec((1, d), lambda i, j: (0, 0))

    return pl.pallas_call(
        body, name="ffn_tail", grid=(ni, nj),
        in_specs=_halo_specs(tb, cb, t_rows) + [pl.BlockSpec((tb, cb), lambda i, j: (i, nj + j)),
                                                pl.BlockSpec((16, cb), lambda i, j: (0, j)),
                                                _resident((dff, d)), tokd(), rowd(), rowd(), tokd()],
        out_specs=[pl.BlockSpec((tb, cb), lambda i, j: (i, j)), pl.BlockSpec((tb, cb), lambda i, j: (i, j)), tokd(), tokd(),
                   pl.BlockSpec((8, d), lambda i, j: (0, 0))],
        out_shape=[SDS((t_rows, dff), BF16), SDS((t_rows, dff), BF16), SDS((t_rows, d), F32), SDS((t_rows, d), BF16),
                   SDS((8, d), F32)],
        scratch_shapes=[pltpu.VMEM((tb, d), F32)],
        compiler_params=_cp("arbitrary", "arbitrary"))(ab, ab, ab, ab, w_conv9, w_down, h1, mx5, gfin, target)


def _ffn_bwd_gate(dffn, w_down, aconv, ab, dff):
    t_rows, d = dffn.shape
    tb = _pick(t_rows, (512,))
    cb = _pick(dff, (1408, 256, 128))
    nj = dff // cb

    def body(g_ref, wd_ref, ac_ref, b_ref, db_ref, dac_ref):
        df = _dot_nt(g_ref[...], wd_ref[pl.ds(pl.multiple_of(pl.program_id(1) * cb, cb), cb), :])
        ac = ac_ref[...].astype(F32)
        sa = _sigmoid(ac)
        db_ref[...] = _bf(df * ac * sa)
        dac_ref[...] = _bf(df * b_ref[...].astype(F32) * (sa * (1.0 + ac * (1.0 - sa))))

    blk = pl.BlockSpec((tb, cb), lambda i, j: (i, j))
    return pl.pallas_call(
        body, name="ffn_bwd_gate", grid=(t_rows // tb, nj),
        in_specs=[pl.BlockSpec((tb, d), lambda i, j: (i, 0)), _resident((dff, d)), blk,
                  pl.BlockSpec((tb, cb), lambda i, j: (i, nj + j))],
        out_specs=[blk, blk], out_shape=[SDS((t_rows, dff), BF16)] * 2,
        compiler_params=_cp("arbitrary", "arbitrary"))(dffn, w_down, aconv, ab)


def _ffn_conv_bwd(dac, ab, w_conv9, dff):
    t_rows = dac.shape[0]
    tb = _pick(t_rows, (256,))
    cb = _pick(dff, (1408, 256, 128))
    ni, nj = t_rows // tb, dff // cb
    n_ext = tb + 2 * GRID_W
    nh64 = tb // GRID_W

    def body(dm, dp, dn, am, ap, an, wc_ref, da_ref, gw_ref):
        i = pl.program_id(1)
        d_ext = _with_halo(dp[...], dm[...], dn[...], i, ni, tb)
        a_ext = _with_halo(ap[...], am[...], an[...], i, ni, tb)
        d_left, d_right = _grid_taps(d_ext, n_ext)
        a_left, a_right = _grid_taps(a_ext, n_ext)
        dmain = d_ext[GRID_W:GRID_W + tb]
        da = jnp.zeros((tb, cb), F32)
        rows = []
        for di in range(3):
            o = (2 - di) * GRID_W
            da = da + (wc_ref[3 * di:3 * di + 1, :] * d_right[o:o + tb] + wc_ref[3 * di + 1:3 * di + 2, :] * d_ext[o:o + tb]
                       + wc_ref[3 * di + 2:3 * di + 3, :] * d_left[o:o + tb])
            o = di * GRID_W
            for tap in (a_left, a_ext, a_right):
                rows.append(jnp.sum(dmain * tap[o:o + tb], axis=0, keepdims=True))
        da_ref[...] = _bf(da)

        @pl.when(i == 0)
        def _():
            gw_ref[...] = jnp.zeros_like(gw_ref)

        gw_ref[...] += jnp.concatenate(rows + [jnp.zeros((7, cb), F32)], axis=0)

    def halo(col0):
        return [pl.BlockSpec((tb, cb), lambda j, i: (i, col0 + j)),
                pl.BlockSpec((GRID_W, cb), lambda j, i: (jnp.maximum(i * nh64 - 1, 0), col0 + j)),
                pl.BlockSpec((GRID_W, cb), lambda j, i: (jnp.minimum((i + 1) * nh64, t_rows // GRID_W - 1), col0 + j))]

    return pl.pallas_call(
        body, name="ffn_conv_bwd", grid=(nj, ni),
        in_specs=halo(0) + halo(0) + [pl.BlockSpec((16, cb), lambda j, i: (0, j))],
        out_specs=[pl.BlockSpec((tb, cb), lambda j, i: (i, j)), pl.BlockSpec((16, cb), lambda j, i: (0, j))],
        out_shape=[SDS((t_rows, dff), BF16), SDS((16, dff), F32)],
        compiler_params=_cp("arbitrary", "arbitrary"))(dac, dac, dac, ab, ab, ab, w_conv9)


def _proj_norm_bwd(pairs, x_arr, x_row0, g, scale, resid, m_rows, name, row_blocks=(1024, 256)):
    d = x_arr.shape[1]
    tm = _pick(m_rows, row_blocks)
    te = 256
    ni = m_rows // tm
    starts, total = [], 0
    for (_, _, _, _, k_p, tk_p) in pairs:
        starts.append(total)
        total += k_p // tk_p
    npairs = len(pairs)
    has_dx = resid is not None

    def body(*refs):
        a_refs, b_refs = refs[0:2 * npairs:2], refs[1:2 * npairs:2]
        rest = refs[2 * npairs:]
        if has_dx:
            x_ref, g_ref, sc_ref, r_ref, dx_ref, st_ref, acc = rest
        else:
            x_ref, g_ref, sc_ref, st_ref, acc = rest
        i, k = pl.program_id(0), pl.program_id(1)

        @pl.when(k == 0)
        def _():
            acc[...] = jnp.zeros_like(acc)

        @pl.when((i == 0) & (k == 0))
        def _():
            st_ref[...] = jnp.zeros_like(st_ref)

        for p in range(npairs):
            nk = pairs[p][4] // pairs[p][5]

            @pl.when((k >= starts[p]) & (k < starts[p] + nk))
            def _(p=p):
                acc[...] += _dot_nt(a_refs[p][...], b_refs[p][...])

        @pl.when(k == total - 1)
        def _():
            sums = [jnp.zeros((1, d), F32)] * 3
            for r0 in range(0, tm, te):
                rows = slice(r0, r0 + te)
                dhn = acc[rows, :]
                x = x_ref[rows, :]
                r = lax.rsqrt(jnp.mean(x * x, axis=-1, keepdims=True) + EPS)
                xn = x * r
                dmod = dhn * (1.0 + sc_ref[...])
                dxn = dmod * g_ref[...]
                if has_dx:
                    dx_ref[rows, :] = r * (dxn - xn * jnp.mean(dxn * xn, axis=-1, keepdims=True)) + r_ref[rows, :]
                sums = [sums[0] + jnp.sum(dmod * xn, axis=0, keepdims=True), sums[1] + jnp.sum(dhn, axis=0, keepdims=True),
                        sums[2] + jnp.sum(dhn * (xn * g_ref[...]), axis=0, keepdims=True)]
            st_ref[...] += jnp.concatenate(sums + [jnp.zeros((5, d), F32)], axis=0)

    in_specs, args = [], []
    for p, (a, a_row0, b, b_col0, k_p, tk_p) in enumerate(pairs):
        nk, s0, ar, bc = k_p // tk_p, starts[p], a_row0 // tm, b_col0 // tk_p

        def kk(k, s0=s0, nk=nk):
            return jnp.clip(k - s0, 0, nk - 1)

        in_specs.append(pl.BlockSpec((tm, tk_p), lambda i, k, ar=ar, kk=kk: (ar + i, kk(k))))
        in_specs.append(pl.BlockSpec((d, tk_p), lambda i, k, bc=bc, kk=kk: (0, bc + kk(k)),
                                     pipeline_mode=pl.Buffered(1 if nk == 1 else 2)))
        args += [a, b]
    xr = x_row0 // tm
    in_specs += [pl.BlockSpec((tm, d), lambda i, k: (xr + i, 0)), pl.BlockSpec((1, d), lambda i, k: (0, 0)),
                 pl.BlockSpec((1, d), lambda i, k: (0, 0))]
    args += [x_arr, g, scale]
    out_specs, out_shape = [], []
    if has_dx:
        in_specs.append(pl.BlockSpec((tm, d), lambda i, k: (i, 0)))
        args.append(resid)
        out_specs.append(pl.BlockSpec((tm, d), lambda i, k: (i, 0)))
        out_shape.append(SDS((m_rows, d), F32))
    out_specs.append(pl.BlockSpec((8, d), lambda i, k: (0, 0)))
    out_shape.append(SDS((8, d), F32))
    return pl.pallas_call(
        body, name=name, grid=(ni, total), in_specs=in_specs, out_specs=out_specs, out_shape=out_shape,
        scratch_shapes=[pltpu.VMEM((tm, d), F32)], compiler_params=_cp("arbitrary", "arbitrary"))(*args)


def _wgrad(a, b, k_rows, name):
    m, n = a.shape[1], b.shape[1]
    tm = _pick(m, (1408, 1024, 512, 384, 256, 128))
    tn = _pick(n, (1408, 1024, 768, 512, 384, 256, 128))
    tk = _pick(k_rows, (1280, 1024, 256))
    nk = k_rows // tk

    def body(a_ref, b_ref, o_ref, acc):
        k = pl.program_id(2)

        @pl.when(k == 0)
        def _():
            acc[...] = jnp.zeros_like(acc)

        acc[...] += _dot_tn(a_ref[...], b_ref[...])

        @pl.when(k == nk - 1)
        def _():
            o_ref[...] = _bf(acc[...])

    return pl.pallas_call(
        body, name=name, grid=(m // tm, n // tn, nk),
        in_specs=[pl.BlockSpec((tk, tm), lambda i, j, k: (k, i)), pl.BlockSpec((tk, tn), lambda i, j, k: (k, j))],
        out_specs=pl.BlockSpec((tm, tn), lambda i, j, k: (i, j)), out_shape=SDS((m, n), BF16),
        scratch_shapes=[pltpu.VMEM((tm, tn), F32)],
        compiler_params=_cp("arbitrary", "arbitrary", "arbitrary"))(a, b)


def _lane_put(col, lane_idx):
    lane = lax.broadcasted_iota(jnp.int32, (1, LANES), 1)
    return jnp.where(lane == lane_idx, col, 0.0)


def _mixer_bwd(dh1, out, hf, hb, z_main, pm, ps, hg, lng, lnb, w_s, b_st, wbm, wbs, wout, mx2, t_rows, nh):
    d = dh1.shape[1]
    ng, sc = w_s.shape[0], w_s.shape[1]
    dh, gd = d // nh, d // ng
    tb = _pick(t_rows, (256,))

    def body(dh1_ref, out_ref, hf_ref, hb_ref, zo, zu, zv, zgm, zgg, pm_ref, ps_ref, hg_ref, lng_ref, lnb_ref, ws_ref, bs_ref,
             wbm_ref, wbs_ref, wo_ref, mx2_ref, dz_ref, dhs_ref, dout_ref, dpm_ref, dps_ref, st_ref, dws_ref, dbs_ref):
        i = pl.program_id(0)

        @pl.when(i == 0)
        def _():
            st_ref[...] = jnp.zeros_like(st_ref)
            dws_ref[...] = jnp.zeros_like(dws_ref)
            dbs_ref[...] = jnp.zeros_like(dbs_ref)

        dh1v = dh1_ref[...]
        doutb = _bf(dh1v * mx2_ref[...])
        dout_ref[...] = doutb
        d_mx2 = jnp.sum(dh1v * out_ref[...].astype(F32), axis=0, keepdims=True)
        dy = _dot_nt(doutb, wo_ref[...])
        sgm, sgg = _sigmoid(zgm[...].astype(F32)), _sigmoid(zgg[...].astype(F32))
        dpmb, dpsb = _bf(dy * sgm), _bf(dy * sgg)
        dpm_ref[...] = dpmb
        dps_ref[...] = dpsb
        dz_ref[:, 3 * d:4 * d] = _bf(dy * pm_ref[...].astype(F32) * sgm * (1.0 - sgm))
        dz_ref[:, 4 * d:5 * d] = _bf(dy * ps_ref[...].astype(F32) * sgg * (1.0 - sgg))
        dym = _dot_nt(dpmb, wbm_ref[...])
        dys = _dot_nt(dpsb, wbs_ref[...])
        hs = hf_ref[...].astype(F32) + hb_ref[...].astype(F32)
        hn, scales = _head_rms(hs, nh, dh)
        so = _sigmoid(zo[...].astype(F32))
        dz_ref[:, 0:d] = _bf(dym * (hn * hg_ref[...]) * so * (1.0 - so))
        dhmn = dym * so
        d_hg = jnp.sum(dhmn * hn, axis=0, keepdims=True)
        dhn = dhmn * hg_ref[...]
        for h in range(nh):
            sl = slice(h * dh, (h + 1) * dh)
            dhs_ref[:, sl] = _bf(scales[h] * (dhn[:, sl] - hn[:, sl] * jnp.mean(dhn[:, sl] * hn[:, sl], axis=-1, keepdims=True)))
        zuv, zvv = zu[...].astype(F32), zv[...].astype(F32)
        u, du_dz = _gelu_and_grad(zuv)
        vg, dvg_dz = _gelu_and_grad(zvv)
        vhat, rstd = _layer_norm(vg)
        vnb = _bf(vhat * lng_ref[...] + lnb_ref[...])
        mixed = _sgu_mix(vnb, ws_ref, bs_ref, tb, ng, gd, sc)
        dz_ref[:, d:2 * d] = _bf(dys * mixed * du_dz)
        dmix = dys * u
        rows = []
        dbs = jnp.zeros((sc, LANES), F32)
        for ch in range(tb // sc):
            cols = []
            for g in range(ng):
                dm = dmix[ch * sc:(ch + 1) * sc, g * gd:(g + 1) * gd]
                dmb = _bf(dm)
                dws_ref[g] += _dot_nt(dmb, vnb[ch * sc:(ch + 1) * sc, g * gd:(g + 1) * gd])
                dbs = dbs + _lane_put(jnp.sum(dm, axis=1, keepdims=True), g)
                cols.append(_dot_tn(_bf(ws_ref[g]), dmb))
            rows.append(jnp.concatenate(cols, axis=1))
        dbs_ref[...] += dbs
        dvn = jnp.concatenate(rows, axis=0)
        d_lng = jnp.sum(dvn * vhat, axis=0, keepdims=True)
        d_lnb = jnp.sum(dvn, axis=0, keepdims=True)
        dvh = dvn * lng_ref[...]
        dvg = rstd * (dvh - jnp.mean(dvh, axis=-1, keepdims=True) - vhat * jnp.mean(dvh * vhat, axis=-1, keepdims=True))
        dz_ref[:, 2 * d:3 * d] = _bf(dvg * dvg_dz)
        st_ref[...] += jnp.concatenate([d_mx2, d_hg, d_lng, d_lnb, jnp.zeros((4, d), F32)], axis=0)

    def tok(col):
        return pl.BlockSpec((tb, d), lambda i: (i, col))

    def full(shape):
        return pl.BlockSpec(shape, lambda i: (0,) * len(shape))

    return pl.pallas_call(
        body, name="mixer_bwd", grid=(t_rows // tb,),
        in_specs=[tok(0), tok(0), tok(0), tok(0), tok(3), tok(4), tok(5), tok(6), tok(7), tok(0), tok(0), full((1, d)),
                  full((1, d)), full((1, d)), full((ng, sc, sc)), full((sc, LANES)), full((d, d)), full((d, d)), full((d, d)),
                  full((1, d))],
        out_specs=[pl.BlockSpec((tb, 5 * d), lambda i: (i, 0)), tok(0), tok(0), tok(0), tok(0), full((8, d)), full((ng, sc, sc)),
                   full((sc, LANES))],
        out_shape=[SDS((t_rows, 5 * d), BF16)] + [SDS((t_rows, d), BF16)] * 4 + [SDS((8, d), F32), SDS((ng, sc, sc), F32),
                                                                                SDS((sc, LANES), F32)],
        compiler_params=_cp("arbitrary"))(dh1, out, hf, hb, z_main, z_main, z_main, z_main, z_main, pm, ps, hg, lng, lnb, w_s,
                                          b_st, wbm, wbs, wout, mx2)


def _mlstm_bwd(qk, z_main, zg, bias, dhs, hf, hb, states_f, states_b, nh, t_rows):
    s_rows = qk.shape[0]
    md = qk.shape[1] // 2
    dh = md // nh
    nc = s_rows // LCH
    nx = t_rows // LCH
    ln = LCH

    def chunk_f(i):
        return jnp.where(i == nc - 1, nc - 1, nc - 2 - i)

    def chunk_b(i):
        return jnp.where(i == nc - 1, nc - 1, i)

    def body(qf, kf, vf, gf, dhf, hsf, cf, nf, mf_, qb, kb, vb, gb, dhb, hsb, cb, nb, mb_, bias_ref, dqkvf_ref, dgf_ref, dqkvb_ref,
             dgb_ref, dc_sc, dn_sc):
        i = pl.program_id(0)
        is_ctx = i == nc - 1

        @pl.when(i == 0)
        def _():
            dc_sc[...] = jnp.zeros_like(dc_sc)
            dn_sc[...] = jnp.zeros_like(dn_sc)

        sides = ((qf, kf, vf, gf, dhf, hsf, cf, nf, mf_, dqkvf_ref, dgf_ref), (qb, kb, vb, gb, dhb, hsb, cb, nb, mb_, dqkvb_ref, dgb_ref))
        gates = [_chunk_gates(s[3][...], bias_ref[...], dr == 1) for dr, s in enumerate(sides)]
        units = []
        for dr, (q_ref, k_ref, v_ref, _, dh_ref, hs_ref, c_ref, n_ref, m_ref, dqkv_ref, _) in enumerate(sides):
            gz, b_all, b_t, g_t, g_all, mask, _ = gates[dr]
            for h in range(nh):
                ci, cfl = 2 * dr * nh + h, (2 * dr + 1) * nh + h
                sl = slice(h * dh, (h + 1) * dh)
                u = dict(dr=dr, h=h, sl=sl, ci=ci, cfl=cfl, dqkv_ref=dqkv_ref, q=q_ref[:, sl], k=k_ref[:, sl], v=v_ref[:, sl],
                         dhv=jnp.where(is_ctx, 0.0, dh_ref[:, sl].astype(F32)), hs=hs_ref[:, sl].astype(F32),
                         c_in=c_ref[sl, :], n_in=n_ref[0:1, sl], m_in=m_ref[h, 0:1, 0:1],
                         b_col=b_all[:, cfl:cfl + 1], i_col=gz[:, ci:ci + 1], g=g_all[:, cfl:cfl + 1],
                         dc_new=dc_sc[dr, h], dn_new=dn_sc[dr, h, 0:1, :])
                u["qf32"], u["kf32"] = u["q"].astype(F32), u["k"].astype(F32)
                u["w"], u["w_int"], u["m_row"] = _head_weights(u["b_col"], b_t[cfl:cfl + 1, :], g_t[ci:ci + 1, :], u["m_in"], mask)
                u["qk"] = _dot_nt(u["q"], u["k"])
                units.append(u)
        for u in units:
            s_mat = u["qk"] * u["w"]
            u["s_mat"], u["sb"], u["cb16"], u["dcb"] = s_mat, _bf(s_mat), _bf(u["c_in"]), _bf(u["dc_new"])
            den = jnp.sum(s_mat, axis=1, keepdims=True) + u["w_int"] * jnp.sum(u["qf32"] * u["n_in"], axis=1, keepdims=True)
            e_m = jnp.exp(-u["m_row"])
            dnm = jnp.maximum(jnp.abs(den), e_m)
            hdh = jnp.sum(u["hs"] * u["dhv"], axis=1, keepdims=True)
            u["dden"] = jnp.where(jnp.abs(den) > e_m, -(hdh / dnm) * jnp.sign(den), 0.0)
            u["dnum_b"] = _bf(u["dhv"] / dnm)
            u["a_old"], u["coef"], _ = _head_state_coeffs(u["g"], u["b_col"], u["i_col"], u["m_in"])
            u["dsm"] = _dot_nt(u["dnum_b"], u["v"])
            u["qct"] = _dot_nt(u["dnum_b"], u["cb16"])
            u["vdc"] = _dot_nt(u["v"], u["dcb"])
        for u in units:
            ds = u["dsm"] + u["dden"]
            u["pb"] = _bf(u["w"] * ds)
            u["gmat"] = u["s_mat"] * ds
            u["dv1"] = _dot_tn(u["sb"], u["dnum_b"])
            u["dv2"] = _dot(_bf(u["kf32"] * u["coef"]), u["dcb"])
            u["dcu"] = _dot_tn(_bf(u["qf32"] * u["w_int"]), u["dnum_b"])
        for u in units:
            u["dq1"] = _dot(u["pb"], u["k"])
            u["dk1"] = _dot_tn(u["pb"], u["q"])
        acc = [dict(x1=jnp.zeros((ln, LANES), F32), x2=jnp.zeros((ln, LANES), F32), dig=jnp.zeros((ln, LANES), F32),
                    e_row=jnp.zeros((1, LANES), F32)) for _ in range(2)]
        for u in units:
            dr, h, sl, a = u["dr"], u["h"], u["sl"], acc[u["dr"]]
            dq_inter = u["w_int"] * (u["qct"] + u["dden"] * u["n_in"])
            dk_state = u["coef"] * (u["vdc"] + u["dn_new"])
            u["dqkv_ref"][:, sl] = _bf(u["dq1"] + dq_inter)
            u["dqkv_ref"][:, md + h * dh:md + (h + 1) * dh] = _bf(u["dk1"] + dk_state)
            u["dqkv_ref"][:, 2 * md + h * dh:2 * md + (h + 1) * dh] = _bf(u["dv1"] + u["dv2"])
            row_intra = jnp.sum(u["gmat"], axis=1, keepdims=True)
            col_intra = jnp.sum(u["gmat"].T, axis=1, keepdims=True)
            row_inter = jnp.sum(u["qf32"] * dq_inter, axis=1, keepdims=True)
            col_inter = jnp.sum(u["kf32"] * dk_state, axis=1, keepdims=True)
            e_old = u["a_old"] * (jnp.sum(jnp.sum(u["dc_new"] * u["c_in"], axis=1, keepdims=True), axis=0, keepdims=True)
                                  + jnp.sum(u["dn_new"] * u["n_in"], axis=1, keepdims=True))
            a["x1"] = a["x1"] + _lane_put(row_intra - col_intra + row_inter, u["cfl"])
            a["x2"] = a["x2"] + _lane_put(col_inter, u["cfl"])
            a["e_row"] = a["e_row"] + _lane_put(e_old, u["cfl"])
            a["dig"] = a["dig"] + _lane_put(col_intra + col_inter, u["ci"])
            dc_sc[dr, h] = u["a_old"] * u["dc_new"] + u["dcu"]
            dn_sc[dr, h] = jnp.broadcast_to(
                u["a_old"] * u["dn_new"] + jnp.sum(u["qf32"] * (u["w_int"] * u["dden"]), axis=0, keepdims=True), (8, dh))
        for dr, s in enumerate(sides):
            gz, mfl, a = gates[dr][0], gates[dr][6], acc[dr]
            dlogf = _mask_dot_t(mfl, a["x1"]) + _mask_dot(mfl, a["x2"]) - a["x2"] + a["e_row"]
            s[10][...] = a["dig"] + dlogf / (1.0 + jnp.exp(gz))

    def tok(cfn, col):
        return pl.BlockSpec((ln, md), lambda i: (cfn(i), col))

    def dht(cfn):
        return pl.BlockSpec((ln, md), lambda i: (jnp.minimum(cfn(i), nx - 1), 0))

    def gat(cfn):
        return pl.BlockSpec((ln, LANES), lambda i: (cfn(i), 0))

    def st(cfn, shape):
        return pl.BlockSpec((None,) + shape, lambda i: (cfn(i),) + (0,) * len(shape))

    st_shapes = ((nh * dh, dh), (8, md), (nh, 8, LANES))

    def side(cfn):
        return [tok(cfn, 0), tok(cfn, 1), tok(cfn, 2), gat(cfn), dht(cfn), tok(cfn, 0)] + [st(cfn, s) for s in st_shapes]

    def outs(cfn):
        return [pl.BlockSpec((ln, 3 * md), lambda i: (cfn(i), 0)), gat(cfn)]

    return pl.pallas_call(
        body, name="mlstm_bwd", grid=(nc,),
        in_specs=side(chunk_f) + side(chunk_b) + [pl.BlockSpec((1, LANES), lambda i: (0, 0))],
        out_specs=outs(chunk_f) + outs(chunk_b),
        out_shape=[SDS((s_rows, 3 * md), BF16), SDS((s_rows, LANES), F32)] * 2,
        scratch_shapes=[pltpu.VMEM((2, nh, dh, dh), F32), pltpu.VMEM((2, nh, 8, dh), F32)],
        compiler_params=_cp("arbitrary"))(qk, qk, z_main, zg, dhs, hf, *states_f, qk, qk, z_main, zg, dhs, hb, *states_b, bias)


def _qkv_conv_bwd(dqkv_f, dqkv_b, z_main, conv_w, t_rows, md, qscale):
    s_rows = z_main.shape[0]
    tb = _pick(s_rows, (1280, 1024, 256))
    cb = _pick(md, (512, 256, 128))
    ni, nj, ncq = s_rows // tb, 3 * md // cb, 2 * md // cb
    nb8 = tb // 8
    n_ext = tb + 16

    def body(fm, fp, fn, bm, bp, bn, zm, zp, zn, w_ref, dz_ref, gw_ref):
        j, i = pl.program_id(0), pl.program_id(1)

        @pl.when(j < ncq)
        def _():
            z = jnp.concatenate([zp[...], zm[...], zn[...]], axis=0).astype(F32)
            dqk = (jnp.concatenate([fp[...], fm[...], fn[...]], axis=0).astype(F32)
                   + jnp.concatenate([bp[...], bm[...], bn[...]], axis=0).astype(F32)) * jnp.where(j * cb < md, qscale, 1.0)
            row = i * tb - 8 + lax.broadcasted_iota(jnp.int32, (n_ext, 1), 0)
            prev_ok, next_ok = _seg_masks(row, t_rows, s_rows)
            zprev = jnp.where(prev_ok, pltpu.roll(z, 1, 0), 0.0)
            znext = jnp.where(next_ok, pltpu.roll(z, n_ext - 1, 0), 0.0)
            pre = w_ref[0:1, :] * zprev + w_ref[1:2, :] * z + w_ref[2:3, :] * znext
            sg = _sigmoid(pre)
            dpre = dqk * (sg * (1.0 + pre * (1.0 - sg)))
            dz = (w_ref[1:2, :] * dpre + w_ref[0:1, :] * jnp.where(next_ok, pltpu.roll(dpre, n_ext - 1, 0), 0.0)
                  + w_ref[2:3, :] * jnp.where(prev_ok, pltpu.roll(dpre, 1, 0), 0.0))
            dz_ref[...] = _bf(dz[8:8 + tb])
            dm = dpre[8:8 + tb]

            @pl.when(i == 0)
            def _():
                gw_ref[...] = jnp.zeros_like(gw_ref)

            gw_ref[...] += jnp.concatenate(
                [jnp.sum(dm * zprev[8:8 + tb], axis=0, keepdims=True), jnp.sum(dm * z[8:8 + tb], axis=0, keepdims=True),
                 jnp.sum(dm * znext[8:8 + tb], axis=0, keepdims=True), jnp.zeros((5, cb), F32)], axis=0)

        @pl.when(j >= ncq)
        def _():
            dz_ref[...] = _bf(fm[...].astype(F32) + bm[...].astype(F32))

    def halo(clampj):
        def cj(j):
            return jnp.minimum(j, ncq - 1) if clampj else j
        return [pl.BlockSpec((tb, cb), lambda j, i: (i, cj(j))),
                pl.BlockSpec((8, cb), lambda j, i: (jnp.maximum(i * nb8 - 1, 0), cj(j))),
                pl.BlockSpec((8, cb), lambda j, i: (jnp.minimum((i + 1) * nb8, s_rows // 8 - 1), cj(j)))]

    return pl.pallas_call(
        body, name="qkv_conv_bwd", grid=(nj, ni),
        in_specs=halo(False) + halo(False) + halo(True) + [pl.BlockSpec((8, cb), lambda j, i: (0, jnp.minimum(j, ncq - 1)))],
        out_specs=[pl.BlockSpec((tb, cb), lambda j, i: (i, j)), pl.BlockSpec((8, cb), lambda j, i: (0, jnp.minimum(j, ncq - 1)))],
        out_shape=[SDS((s_rows, 3 * md), BF16), SDS((8, 2 * md), F32)],
        compiler_params=_cp("arbitrary", "arbitrary"))(dqkv_f, dqkv_f, dqkv_f, dqkv_b, dqkv_b, dqkv_b, z_main, z_main, z_main, conv_w)


def _gate_grad_sum(dg_f, dg_b):
    s_rows = dg_f.shape[0]
    tb = _pick(s_rows, (1280, 1024, 256))

    def body(a_ref, b_ref, o_ref, st_ref):
        @pl.when(pl.program_id(0) == 0)
        def _():
            st_ref[...] = jnp.zeros_like(st_ref)

        s = a_ref[...] + b_ref[...]
        o_ref[...] = _bf(s)
        st_ref[...] += jnp.concatenate([jnp.sum(s, axis=0, keepdims=True), jnp.zeros((7, LANES), F32)], axis=0)

    blk = pl.BlockSpec((tb, LANES), lambda i: (i, 0))
    return pl.pallas_call(
        body, name="gate_grad_sum", grid=(s_rows // tb,), in_specs=[blk, blk],
        out_specs=[blk, pl.BlockSpec((8, LANES), lambda i: (0, 0))],
        out_shape=[SDS((s_rows, LANES), BF16), SDS((8, LANES), F32)], compiler_params=_cp("arbitrary"))(dg_f, dg_b)


def _mod_grads(silu_slots, dmx_sh, dmx_slots, dmc_tot, dmc_sh, silu_cctx, c_ctx, w_mod_c):
    d = silu_slots.shape[1]
    ncol, n6 = dmx_sh.shape[1], dmx_slots.shape[1]

    def body(ss_ref, dsh_ref, dsl_ref, dct_ref, dcs_ref, sc_ref, c_ref, w_ref, gw_ref, gb_ref, gc_ref):
        a = jnp.concatenate([ss_ref[...], sc_ref[...], jnp.zeros((7, d), F32)], axis=0)
        b = jnp.concatenate([dsh_ref[...], dcs_ref[...], jnp.zeros((7, ncol), F32)], axis=0)
        gw_ref[0] = lax.dot_general(a, b, (((0,), (0,)), ((), ())), preferred_element_type=F32, precision=HI)
        dct = dct_ref[...]
        gb_ref[...] = jnp.sum(dsl_ref[...], axis=0, keepdims=True) + jnp.concatenate(
            [dct, jnp.zeros((1, n6 - dct.shape[1]), F32)], axis=1)
        t = _dot_nt(_bf(jnp.broadcast_to(dct, (8, dct.shape[1]))), w_ref[...])
        cv = c_ref[...]
        s = _sigmoid(cv)
        gc_ref[...] = t[0:1, :] * (s * (1.0 + cv * (1.0 - s)))

    return pl.pallas_call(body, name="mod_grads", out_shape=[SDS((1, d, ncol), F32), SDS((1, n6), F32), SDS((1, d), F32)],
                          compiler_params=_cp())(silu_slots, dmx_sh, dmx_slots, dmc_tot, dmc_sh, silu_cctx, c_ctx, w_mod_c)


def _slot_sum(slots):
    ns, r = slots.shape[0], slots.shape[1]
    tb = _pick(r, (1024, 512, 256, 128, 64, 32, 16, 8))

    def body(s_ref, o_ref):
        acc = s_ref[0]
        for k in range(1, ns):
            acc = acc + s_ref[k]
        o_ref[...] = acc

    return pl.pallas_call(
        body, name="slot_sum", grid=(r // tb,), in_specs=[pl.BlockSpec((ns, tb, LANES), lambda i: (0, i, 0))],
        out_specs=pl.BlockSpec((tb, LANES), lambda i: (i, 0)), out_shape=SDS((r, LANES), F32),
        compiler_params=_cp("arbitrary"))(slots)


def _adamw(w, gslots, m, v, name):
    lead = ((None,), (0,)) if w.ndim == 3 else ((), ())
    r, cdim = w.shape[-2:]
    ns, rg = gslots.shape[0], gslots.shape[1]
    tb = r if (rg != r or r % 8) else _pick(r, (128, 64, 32, 16, 8))
    bc1, bc2 = 1.0 - ADAM_B1 ** ADAM_STEP, 1.0 - ADAM_B2 ** ADAM_STEP

    def body(w_ref, g_ref, m_ref, v_ref, go_ref, d_ref, mo_ref, vo_ref):
        g = g_ref[0, 0:tb, :].astype(F32)
        for k in range(1, ns):
            g = g + g_ref[k, 0:tb, :].astype(F32)
        mn = ADAM_B1 * m_ref[...] + (1.0 - ADAM_B1) * g
        vn = ADAM_B2 * v_ref[...] + (1.0 - ADAM_B2) * (g * g)
        go_ref[...] = g
        mo_ref[...] = mn
        vo_ref[...] = vn
        d_ref[...] = -ADAM_LR * ((mn / bc1) / (jnp.sqrt(vn / bc2) + ADAM_EPS) + ADAM_WD * w_ref[...])

    blk = pl.BlockSpec(lead[0] + (tb, cdim), lambda i: lead[1] + (i, 0))
    gblk = pl.BlockSpec((ns, tb if rg == r else rg, cdim), lambda i: (0, i, 0))
    return pl.pallas_call(
        body, name=name, grid=(r // tb,), in_specs=[blk, gblk, blk, blk],
        out_specs=[blk] * 4, out_shape=[SDS(w.shape, F32)] * 4, compiler_params=_cp("arbitrary"))(w, gslots, m, v)


def _pack(parts, row_mult):
    flat = jnp.concatenate([p.reshape(-1) for p in parts])
    n = flat.shape[0]
    rows = -(-n // LANES)
    rows = -(-rows // row_mult) * row_mult
    return jnp.pad(flat, (0, rows * LANES - n)).reshape(rows, LANES)


def _unpack(buf, shapes):
    flat = buf.reshape(-1)
    out, off = [], 0
    for s in shapes:
        n = math.prod(s)
        out.append(flat[off:off + n].reshape(s))
        off += n
    return out


def _pad_cols(a, width):
    return jnp.pad(a, ((0, 0), (0, width - a.shape[1])))


def _pad_lanes(a):
    return _pad_cols(a, LANES)


def _up128(n):
    return -(-n // LANES) * LANES


def kernel(x, c, ctx, c_ctx, w_mod, b_mod, norm1_g, w_in, b_gate, conv_qk, head_norm_g, sgu_ln_g, sgu_ln_b, w_s, b_s, w_branch_mlstm, w_branch_sgu, w_out, norm2_g, w_up, w_ffn_conv, w_down, final_g, loss_target, m_c_ctx, m_w_mod, m_b_mod, m_norm1_g, m_w_in, m_b_gate, m_conv_qk, m_head_norm_g, m_sgu_ln_g, m_sgu_ln_b, m_w_s, m_b_s, m_w_branch_mlstm, m_w_branch_sgu, m_w_out, m_norm2_g, m_w_up, m_w_ffn_conv, m_w_down, m_final_g, v_c_ctx, v_w_mod, v_b_mod, v_norm1_g, v_w_in, v_b_gate, v_conv_qk, v_head_norm_g, v_sgu_ln_g, v_sgu_ln_b, v_w_s, v_b_s, v_w_branch_mlstm, v_w_branch_sgu, v_w_out, v_norm2_g, v_w_up, v_w_ffn_conv, v_w_down, v_final_g):
    t, d = x.shape[1], x.shape[2]
    n_ctx = ctx.shape[1]
    s_rows = t + n_ctx
    nh = b_gate.shape[1] // 4
    md = head_norm_g.shape[1]
    dh = md // nh
    ng, sc = w_s.shape[1], w_s.shape[2]
    dff = w_down.shape[1] * N_DEV
    n_in = w_in.shape[2] * N_DEV
    assert md == d and sgu_ln_g.shape[1] == d and n_ctx == LCH and t % LCH == 0 and t % (8 * GRID_W) == 0
    assert n_in == 8 * d + 4 * nh and 4 * nh <= LANES
    me = 4 * lax.axis_index("x") + 2 * lax.axis_index("y") + lax.axis_index("c")

    n_mod, n_insh, n_upsh = w_mod.shape[2], w_in.shape[2], w_up.shape[2]
    p_mod, p_in, p_up = _up128(n_mod), _up128(n_insh), _up128(n_upsh)
    nq, nf = conv_qk.shape[2], w_ffn_conv.shape[3]
    ffn9 = w_ffn_conv[0].reshape(9, nf)
    colpack = jnp.concatenate([_pad_cols(_bf(w_mod[0]), p_mod), _pad_cols(_bf(w_in[0]), p_in)], axis=1)
    convpack = jnp.concatenate([jnp.pad(conv_qk[0], ((0, 13), (0, 0))), jnp.pad(ffn9, ((0, 7), (0, 0)))], axis=1)
    g_col, g_conv = _allgather([colpack, convpack])
    w_mod_f, w_main, w_gate = _assemble_cols(
        g_col, [(0, n_mod, [(0, 0, N_DEV * n_mod, 0)]),
                (p_mod, n_insh, [(1, 0, 3 * md, 0), (2, 3 * md, 4 * nh, 0), (1, 3 * md + 4 * nh, 5 * d, 3 * md)])],
        [N_MOD * d, 8 * d, LANES], "assemble_weights")
    convw, wconv9 = _assemble_cols(g_conv, [(0, nq, [(0, 0, N_DEV * nq, 0)]), (nq, nf, [(1, 0, N_DEV * nf, 0)])],
                                   [N_DEV * nq, N_DEV * nf], "assemble_conv_weights")
    zero = jnp.minimum(jnp.abs(g_conv[0, 0, 0]), 0.0)
    late_w = [_pad_cols(_bf(w_up[0] + zero), p_up), _bf(w_branch_mlstm[0]), _bf(w_branch_sgu[0]), _bf(w_out[0]), _bf(w_down[0])]
    late_state, late_tok = _exchange_start(late_w, False, "late_weights_start")

    cvec = jnp.concatenate([c, c_ctx[None], jnp.zeros((6, d), F32)], axis=0) + late_tok[0:1, 0:1]
    silu_v, mod = _modulation(cvec, w_mod_f, b_mod)
    mx = [mod[0:1, k * d:(k + 1) * d] for k in range(N_MOD)]
    mc = [mod[1:2, k * d:(k + 1) * d] for k in range(2)]
    x2, ctx2 = x[0], ctx[0]
    in_x = _norm_mod_proj(x2, norm1_g, jnp.concatenate([mx[0], mx[1]], axis=0), w_main, w_gate, s_rows, 0, None, "in_proj")
    hn, z_main, zg = _norm_mod_proj(ctx2, norm1_g, jnp.concatenate([mc[0], mc[1]], axis=0), w_main, w_gate, s_rows, t, in_x,
                                    "in_proj_ctx")
    qscale = dh ** -0.5
    qk = _qk_conv(z_main, convw, t, md, qscale)
    bias = _pad_lanes(b_gate)
    fwd = _mlstm_fwd(qk, z_main, zg, bias, nh)
    hf, hb, states_f, states_b = fwd[0], fwd[1], fwd[2:5], fwd[5:8]
    g_up, g_bm, g_bs, g_out, g_down = _exchange_wait(late_state, fwd[4], "late_weights_wait")
    (w_up_f,) = _assemble_cols(g_up, [(0, n_upsh, [(0, 0, 2 * dff, 0)])], [2 * dff], "assemble_w_up")
    wbm_f, wbs_f, wout_f = (g.reshape(d, d) for g in (g_bm, g_bs, g_out))
    w_down_f = g_down.reshape(dff, d)
    b_st = _pad_lanes(b_s[0].T)
    h1, ym, ys, pm, ps, y, out = _mixer_fwd(hf, hb, z_main, x2, head_norm_g, sgu_ln_g, sgu_ln_b, w_s[0], b_st, wbm_f, wbs_f,
                                            wout_f, mx[2], t, nh)
    hn2, ab = _norm_mod_proj(h1, norm2_g, jnp.concatenate([mx[3], mx[4]], axis=0), w_up_f, None, t, 0, None, "up_proj")
    aconv, f, dh2, dffn, st_tail = _ffn_tail(ab, wconv9, w_down_f, h1, mx[5], final_g[None], loss_target[0], dff)

    db, dac = _ffn_bwd_gate(dffn, w_down_f, aconv, ab, dff)
    da, g_wconv9 = _ffn_conv_bwd(dac, ab, wconv9, dff)
    g_wdown = _wgrad(f, dffn, t, "wgrad_down")
    gwup_slots = _scatter_cols([_wgrad(hn2, da, t, "wgrad_up_a"), _wgrad(hn2, db, t, "wgrad_up_b")],
                               [(0, 0, dff, 0), (1, dff, dff, 0)], n_upsh, "scatter_grad_w_up")
    dh1, st_n2 = _proj_norm_bwd([(da, 0, w_up_f, 0, dff, dff), (db, 0, w_up_f, dff, dff, dff)], h1, 0, norm2_g, mx[4], dh2, t,
                                "up_proj_bwd", (512, 256))
    dz_rest, dhs, dout, dpm, dps, st_mix, g_ws, g_bst = _mixer_bwd(dh1, out, hf, hb, z_main, pm, ps, head_norm_g, sgu_ln_g,
                                                                    sgu_ln_b, w_s[0], b_st, wbm_f, wbs_f, wout_f, mx[2], t, nh)
    g_wout = _wgrad(y, dout, t, "wgrad_out")
    g_wbm = _wgrad(ym, dpm, t, "wgrad_branch_mlstm")
    g_wbs = _wgrad(ys, dps, t, "wgrad_branch_sgu")
    ex_a = [gwup_slots, g_wdown.reshape(N_DEV, dff // N_DEV, d), g_wbm.reshape(N_DEV, d // N_DEV, d),
            g_wbs.reshape(N_DEV, d // N_DEV, d), g_wout.reshape(N_DEV, d // N_DEV, d)]
    ex_a_state, ex_a_tok = _exchange_start(ex_a, True, "grad_exchange_a_start")
    dqkv_f, dg_f, dqkv_b, dg_b = _mlstm_bwd(qk, z_main, zg, bias + ex_a_tok[0:1, :], dhs, hf, hb, states_f, states_b, nh, t)
    dz_qkv, g_convqk = _qkv_conv_bwd(dqkv_f, dqkv_b, z_main, convw, t, md, qscale)
    dz_g, st_gate = _gate_grad_sum(dg_f, dg_b)
    gwin_slots = _scatter_cols(
        [_wgrad(hn, dz_qkv, s_rows, "wgrad_in_qkv"), _wgrad(hn, dz_g, s_rows, "wgrad_in_gate"), _wgrad(hn, dz_rest, t, "wgrad_in_rest")],
        [(0, 0, 3 * md, 0), (1, 3 * md, 4 * nh, 0), (2, 3 * md + 4 * nh, 5 * d, 0)], n_insh, "scatter_grad_w_in")
    gcq_slots = _scatter_cols([g_convqk], [(0, 0, 2 * md, 0)], nq, "scatter_grad_conv_qk")
    gcf_slots = _scatter_cols([g_wconv9], [(0, 0, dff, 0)], nf, "scatter_grad_ffn_conv")
    ex_b_state, ex_b_tok = _exchange_start([gwin_slots, gcq_slots, gcf_slots], True, "grad_exchange_b_start")
    tk = _pick(md, (1024, 512, 256))
    grad_x, st_n1x = _proj_norm_bwd(
        [(dz_qkv, 0, w_main, 0, 3 * md, tk), (dz_rest, 0, w_main, 3 * md, 5 * d, tk), (dz_g, 0, w_gate, 0, LANES, LANES)],
        x2, 0, norm1_g, mx[1] + ex_b_tok[0:1, 0:1], dh1, t, "in_proj_bwd")
    (st_n1c,) = _proj_norm_bwd([(dz_qkv, t, w_main, 0, 3 * md, tk), (dz_g, t, w_gate, 0, LANES, LANES)],
                               ctx2, 0, norm1_g, mc[1] + ex_b_tok[0:1, 0:1], None, n_ctx, "in_proj_bwd_ctx")

    rx_a = _exchange_wait(ex_a_state, st_n1c, "grad_exchange_a_wait")
    rx_b = _exchange_wait(ex_b_state, st_n1c, "grad_exchange_b_wait")
    recv = [rx_b[0], rx_a[0], rx_a[2], rx_a[3], rx_a[4], rx_a[1], rx_b[1], rx_b[2]]
    small_parts = [st_n1x[1], st_n1x[2], st_mix[0], st_n2[1], st_n2[2], st_tail[1],
                   st_n1c[1], st_n1c[2],
                   silu_v[0], st_n1x[0] + st_n1c[0], st_gate[0], st_mix[1], st_mix[2], st_mix[3],
                   g_ws.reshape(-1), g_bst[:, :ng].T.reshape(-1), st_n2[0], st_tail[0]]
    gsmall = _pack(small_parts, 8)
    (recv_small,) = _grad_exchange([], [gsmall])
    small_sum = _slot_sum(recv_small).reshape(-1)
    small_slots = recv_small.reshape(N_DEV, -1)
    o_silu, o_n1 = 8 * d, 9 * d
    ncol = N_MOD * d // N_DEV
    dmc_tot = small_sum[6 * d:8 * d][None]
    dmc_pad = jnp.concatenate([dmc_tot, jnp.zeros((1, 4 * d), F32)], axis=1)
    g_wmod, g_bmod, g_cctx = _mod_grads(
        small_slots[:, o_silu:o_silu + d], lax.dynamic_slice_in_dim(small_slots[:, :6 * d], me * ncol, ncol, axis=1),
        small_slots[:, :6 * d], dmc_tot, lax.dynamic_slice_in_dim(dmc_pad, me * ncol, ncol, axis=1), silu_v[1:2], c_ctx[None],
        w_mod_f[:, :2 * d])

    shard_w = (w_in, w_up, w_branch_mlstm, w_branch_sgu, w_out, w_down, conv_qk)
    shard_m = (m_w_in, m_w_up, m_w_branch_mlstm, m_w_branch_sgu, m_w_out, m_w_down, m_conv_qk)
    shard_v = (v_w_in, v_w_up, v_w_branch_mlstm, v_w_branch_sgu, v_w_out, v_w_down, v_conv_qk)
    shard_names = ("w_in", "w_up", "w_branch_mlstm", "w_branch_sgu", "w_out", "w_down", "conv_qk")
    shard_out = [_adamw(wa, recv[k], ma, va, "adamw_" + nm)
                 for k, (wa, ma, va, nm) in enumerate(zip(shard_w, shard_m, shard_v, shard_names))]
    shard_out.append([b.reshape(w_ffn_conv.shape) for b in
                      _adamw(ffn9, recv[7], m_w_ffn_conv[0].reshape(9, nf), v_w_ffn_conv[0].reshape(9, nf), "adamw_w_ffn_conv")])
    mod_out = _adamw(w_mod, g_wmod, m_w_mod, v_w_mod, "adamw_w_mod")

    def rep(cc, bm, n1, bg, hg, lg, lb, ws, bs, n2, fg):
        return [cc.reshape(-1), bm.reshape(-1), n1.reshape(-1), _pad_lanes(bg.reshape(1, -1)).reshape(-1), hg.reshape(-1),
                lg.reshape(-1), lb.reshape(-1), ws.reshape(-1), bs.reshape(-1), n2.reshape(-1), fg.reshape(-1)]

    o = o_n1
    g_rep_parts = [g_cctx, g_bmod]
    for n in (d, LANES, d, d, d, ng * sc * sc, ng * sc, d, d):
        g_rep_parts.append(small_sum[o:o + n])
        o += n
    rep_shapes = [(d,), (1, N_MOD * d), (1, d), (1, LANES), (1, d), (1, d), (1, d), (1, ng, sc, sc), (1, ng, sc), (1, d), (d,)]
    rep_out = _adamw(
        _pack(rep(c_ctx, b_mod, norm1_g, b_gate, head_norm_g, sgu_ln_g, sgu_ln_b, w_s, b_s, norm2_g, final_g), 8),
        _pack(g_rep_parts, 8)[None],
        _pack(rep(m_c_ctx, m_b_mod, m_norm1_g, m_b_gate, m_head_norm_g, m_sgu_ln_g, m_sgu_ln_b, m_w_s, m_b_s, m_norm2_g, m_final_g), 8),
        _pack(rep(v_c_ctx, v_b_mod, v_norm1_g, v_b_gate, v_head_norm_g, v_sgu_ln_g, v_sgu_ln_b, v_w_s, v_b_s, v_norm2_g, v_final_g), 8),
        "adamw_replicated")

    def assemble(k):
        r = _unpack(rep_out[k], rep_shapes)
        s = [o[k] for o in shard_out]
        return [r[0], mod_out[k], r[1], r[2], s[0], r[3][:, :4 * nh], s[6], r[4], r[5], r[6], r[7], r[8], s[2], s[3], s[4], r[9],
                s[1], s[7], s[5], r[10]]

    loss = lax.psum(st_tail[2, 0], ("x", "y", "c"))
    outs = [loss, grad_x[None]]
    for k in range(4):
        outs += assemble(k)
    return tuple(outs)
```

```python
import math

import jax
import jax.numpy as jnp
from jax import lax
from jax.experimental import pallas as pl
from jax.experimental.pallas import tpu as pltpu

F32, BF16 = jnp.float32, jnp.bfloat16
EPS = 1e-6
M_INIT = -1e30
NEG = -1e30
GRID_W = 64
LCH = 256
N_MOD = 6
N_DEV = 8
LANES = 128
ADAM_LR, ADAM_B1, ADAM_B2, ADAM_EPS, ADAM_WD, ADAM_STEP = 0.001, 0.9, 0.999, 1e-08, 0.01, 10
GELU_C = math.sqrt(2.0 / math.pi)
GELU_A = 0.044715
VMEM_LIMIT = 56 * 1024 * 1024
HI = lax.Precision.HIGHEST
SDS = jax.ShapeDtypeStruct
MESH_ID = pl.DeviceIdType.MESH


def _pick(n, cands):
    for c in cands:
        if n % c == 0:
            return c
    raise ValueError(f"no block size for {n} in {cands}")


def _cp(*sem):
    return pltpu.CompilerParams(dimension_semantics=sem if sem else None, vmem_limit_bytes=VMEM_LIMIT)


def _sigmoid(x):
    return 0.5 * jnp.tanh(0.5 * x) + 0.5


def _split3(x):
    hi = x.astype(BF16)
    r = x - hi.astype(F32)
    mid = r.astype(BF16)
    return hi, mid, (r - mid.astype(F32)).astype(BF16)


def _mask_dot(mask_b, x):
    hi, mid, lo = _split3(x)
    return (_dot(mask_b, lo) + _dot(mask_b, mid)) + _dot(mask_b, hi)


def _mask_dot_t(mask_b, x):
    hi, mid, lo = _split3(x)
    return (_dot_tn(mask_b, lo) + _dot_tn(mask_b, mid)) + _dot_tn(mask_b, hi)


def _gelu(x):
    return x * (0.5 * (1.0 + jnp.tanh(GELU_C * x * (1.0 + GELU_A * (x * x)))))


def _gelu_and_grad(x):
    x2 = x * x
    t = jnp.tanh(GELU_C * x * (1.0 + GELU_A * x2))
    half = 0.5 * (1.0 + t)
    return x * half, half + (0.5 * GELU_C) * x * (1.0 - t * t) * (1.0 + 3.0 * GELU_A * x2)


def _log_sigmoid(x):
    return jnp.minimum(x, 0.0) - jnp.log(1.0 + jnp.exp(-jnp.abs(x)))


def _dot(a, b):
    return jnp.dot(a, b, preferred_element_type=F32)


def _dot_nt(a, b):
    return lax.dot_general(a, b, (((1,), (1,)), ((), ())), preferred_element_type=F32)


def _dot_tn(a, b):
    return lax.dot_general(a, b, (((0,), (0,)), ((), ())), preferred_element_type=F32)


def _bf(x):
    return x.astype(BF16)


def _allgather(arrs):
    na = len(arrs)

    def body(*refs):
        x_refs, o_refs = refs[:na], refs[na:2 * na]
        send_sems, recv_sems, local_sems = refs[2 * na:]
        x, y, c = lax.axis_index("x"), lax.axis_index("y"), lax.axis_index("c")
        me, sibling = (x, y, c), (x, y, 1 - c)
        chips = [(1 - x, y), (x, 1 - y), (1 - x, 1 - y)]

        def copy(a, k, block, to, src=None):
            slot = o_refs[a].at[4 * block[0] + 2 * block[1] + block[2]]
            return pltpu.make_async_remote_copy(
                src_ref=slot if src is None else src, dst_ref=slot, send_sem=send_sems.at[7 * a + k],
                recv_sem=recv_sems.at[7 * a + k], device_id=to, device_id_type=MESH_ID)

        mine = [pltpu.make_async_copy(x_refs[a], o_refs[a].at[4 * x + 2 * y + c], local_sems.at[a]) for a in range(na)]
        for cp in mine:
            cp.start()
        first = []
        for a in range(na):
            first.append(copy(a, 0, me, sibling, src=x_refs[a]))
            first += [copy(a, 1 + j, me, (*chip, c), src=x_refs[a]) for j, chip in enumerate(chips)]
        for cp in first:
            cp.start()
        passed = []
        for j, chip in enumerate(chips):
            for a in range(na):
                copy(a, 1 + j, (*chip, c), me).wait_recv()
                passed.append(copy(a, 4 + j, (*chip, c), sibling))
                passed[-1].start()
        for a in range(na):
            copy(a, 0, sibling, me).wait_recv()
            for j, chip in enumerate(chips):
                copy(a, 4 + j, (*chip, 1 - c), me).wait_recv()
        for cp in first + passed:
            cp.wait_send()
        for cp in mine:
            cp.wait()

    anyspec = pl.BlockSpec(memory_space=pl.ANY)
    return pl.pallas_call(
        body, name="weights_allgather",
        out_shape=[SDS((N_DEV,) + a.shape, a.dtype) for a in arrs],
        in_specs=[anyspec] * na, out_specs=[anyspec] * na,
        scratch_shapes=[pltpu.SemaphoreType.DMA((7 * na,)), pltpu.SemaphoreType.DMA((7 * na,)), pltpu.SemaphoreType.DMA((na,))],
    )(*arrs)


def _grad_exchange(per_dest, shared):
    nd, ns = len(per_dest), len(shared)
    na = nd + ns

    def body(*refs):
        in_refs, out_refs = refs[:na], refs[na:2 * na]
        send_sems, recv_sems, local_sems = refs[2 * na:]
        x, y, c = lax.axis_index("x"), lax.axis_index("y"), lax.axis_index("c")
        me = 4 * x + 2 * y + c

        def src(a, idx):
            return in_refs[a].at[idx] if a < nd else in_refs[a]

        loc = [pltpu.make_async_copy(src(a, me), out_refs[a].at[me], local_sems.at[a]) for a in range(na)]
        for cp in loc:
            cp.start()
        sends, recvs = [], []
        for k in range(1, N_DEV):
            px = 1 - x if k & 4 else x
            py = 1 - y if k & 2 else y
            pc = 1 - c if k & 1 else c
            peer, pidx = (px, py, pc), 4 * px + 2 * py + pc
            for a in range(na):
                sem = 7 * a + k - 1
                sends.append(pltpu.make_async_remote_copy(
                    src_ref=src(a, pidx), dst_ref=out_refs[a].at[me], send_sem=send_sems.at[sem],
                    recv_sem=recv_sems.at[sem], device_id=peer, device_id_type=MESH_ID))
                recvs.append(pltpu.make_async_remote_copy(
                    src_ref=src(a, pidx), dst_ref=out_refs[a].at[pidx], send_sem=send_sems.at[sem],
                    recv_sem=recv_sems.at[sem], device_id=peer, device_id_type=MESH_ID))
        for cp in sends:
            cp.start()
        for cp in recvs:
            cp.wait_recv()
        for cp in sends:
            cp.wait_send()
        for cp in loc:
            cp.wait()

    anyspec = pl.BlockSpec(memory_space=pl.ANY)
    return pl.pallas_call(
        body, name="grad_exchange",
        out_shape=[SDS(a.shape, a.dtype) for a in per_dest] + [SDS((N_DEV,) + a.shape, a.dtype) for a in shared],
        in_specs=[anyspec] * na, out_specs=[anyspec] * na,
        scratch_shapes=[pltpu.SemaphoreType.DMA((7 * na,)), pltpu.SemaphoreType.DMA((7 * na,)), pltpu.SemaphoreType.DMA((na,))],
    )(*per_dest, *shared)


_HBM_SPEC = pl.BlockSpec(memory_space=pltpu.HBM)
_SEM_SPEC = pl.BlockSpec(memory_space=pltpu.SEMAPHORE)
_EFFECT = pltpu.SideEffectType.DATAFLOW_SIDE_EFFECTING


def _peer_list(x, y, c):
    out = []
    for k in range(1, N_DEV):
        px = 1 - x if k & 4 else x
        py = 1 - y if k & 2 else y
        pc = 1 - c if k & 1 else c
        out.append(((px, py, pc), 4 * px + 2 * py + pc))
    return out


def _split_copies(src, land, send_sems, recv_sems, per_dest, receive):
    x, y, c = lax.axis_index("x"), lax.axis_index("y"), lax.axis_index("c")
    me = 4 * x + 2 * y + c
    out = []
    for k, (peer, pidx) in enumerate(_peer_list(x, y, c)):
        for a in range(len(src)):
            out.append(pltpu.make_async_remote_copy(
                src_ref=src[a].at[pidx] if per_dest else src[a], dst_ref=land[a].at[pidx if receive else me],
                send_sem=send_sems.at[7 * a + k], recv_sem=recv_sems.at[7 * a + k], device_id=peer, device_id_type=MESH_ID))
    return out


def _own_copies(src, land, own_sems, per_dest):
    me = 4 * lax.axis_index("x") + 2 * lax.axis_index("y") + lax.axis_index("c")
    return [pltpu.make_async_copy(src[a].at[me] if per_dest else src[a], land[a].at[me], own_sems.at[a]) for a in range(len(src))]


def _exchange_start(arrs, per_dest, name):
    na = len(arrs)
    land_shapes = [a.shape if per_dest else (N_DEV,) + a.shape for a in arrs]
    lands = [pltpu.with_memory_space_constraint(lax.empty(s, a.dtype), pltpu.HBM) for s, a in zip(land_shapes, arrs)]

    def body(*refs):
        src, land = refs[:na], refs[na:2 * na]
        send_sems, recv_sems, own_sems, token = refs[2 * na], refs[2 * na + 1], refs[2 * na + 2], refs[-1]
        for cp in _split_copies(src, land, send_sems, recv_sems, per_dest, False) + _own_copies(src, land, own_sems, per_dest):
            cp.start()
        token[...] = jnp.zeros_like(token)

    outs = pl.pallas_call(
        body, name=name,
        out_shape=[pltpu.SemaphoreType.DMA((7 * na,)), pltpu.SemaphoreType.DMA((7 * na,)), pltpu.SemaphoreType.DMA((na,))]
        + [pltpu.HBM(a.shape, a.dtype) for a in arrs] + [pltpu.HBM(s, a.dtype) for s, a in zip(land_shapes, arrs)]
        + [SDS((8, LANES), F32)],
        in_specs=[_HBM_SPEC] * (2 * na),
        out_specs=[_SEM_SPEC] * 3 + [_HBM_SPEC] * (2 * na) + [pl.BlockSpec(memory_space=pltpu.VMEM)],
        input_output_aliases={k: 3 + k for k in range(2 * na)},
        compiler_params=pltpu.CompilerParams(has_side_effects=_EFFECT),
    )(*[pltpu.with_memory_space_constraint(a, pltpu.HBM) for a in arrs], *lands)
    return (na, per_dest, outs[:-1]), outs[-1]


def _exchange_wait(state, after, name):
    na, per_dest, started = state

    def body(*refs):
        src, land = refs[:na], refs[na:2 * na]
        send_sems, recv_sems, own_sems = refs[2 * na], refs[2 * na + 1], refs[2 * na + 2]
        for cp in _split_copies(src, land, send_sems, recv_sems, per_dest, True):
            cp.wait_send()
            cp.wait_recv()
        for cp in _own_copies(src, land, own_sems, per_dest):
            cp.wait()

    bufs = started[3:]
    outs = pl.pallas_call(
        body, name=name,
        out_shape=[pltpu.HBM(b.shape, b.dtype) for b in bufs],
        in_specs=[_HBM_SPEC] * (2 * na) + [_SEM_SPEC] * 3 + [pl.BlockSpec(memory_space=pl.ANY)],
        out_specs=[_HBM_SPEC] * (2 * na),
        input_output_aliases={k: k for k in range(2 * na)},
        compiler_params=pltpu.CompilerParams(has_side_effects=_EFFECT),
    )(*bufs, started[0], started[1], started[2], after)
    return outs[na:]


def _col_pieces(n, segments):
    out = []
    for j in range(N_DEV):
        lo, hi = j * n, (j + 1) * n
        for (k, s0, w, c0) in segments:
            a, b = max(lo, s0), min(hi, s0 + w)
            if a < b:
                out.append((j, a - lo, b - lo, k, c0 + a - s0, c0 + b - s0))
    return out


def _assemble_cols(slots, groups, out_widths, name):
    r, p = slots.shape[1], slots.shape[2]
    tb = _pick(r, (128, 64, 32, 16, 8))
    covered = [0] * len(out_widths)
    for (_, n, segs) in groups:
        for (k, _, w, _) in segs:
            covered[k] += w

    def body(s_ref, *o_refs):
        for k, wd in enumerate(out_widths):
            if covered[k] < wd:
                o_refs[k][...] = jnp.zeros_like(o_refs[k])
        for (off, n, segs) in groups:
            for (j, a0, a1, k, d0, d1) in _col_pieces(n, segs):
                o_refs[k][:, d0:d1] = s_ref[j, :, off + a0:off + a1]

    return pl.pallas_call(
        body, name=name, grid=(r // tb,), in_specs=[pl.BlockSpec((N_DEV, tb, p), lambda i: (0, i, 0))],
        out_specs=[pl.BlockSpec((tb, w), lambda i: (i, 0)) for w in out_widths],
        out_shape=[SDS((r, w), slots.dtype) for w in out_widths], compiler_params=_cp("arbitrary"))(slots)


def _scatter_cols(pieces, segments, n, name):
    r = pieces[0].shape[0]
    tb = _pick(r, (128, 64, 32, 16, 8))

    def body(*refs):
        p_refs, o_ref = refs[:-1], refs[-1]
        for (j, a0, a1, k, d0, d1) in _col_pieces(n, segments):
            o_ref[j, :, a0:a1] = p_refs[k][:, d0:d1]

    return pl.pallas_call(
        body, name=name, grid=(r // tb,), in_specs=[pl.BlockSpec((tb, a.shape[1]), lambda i: (i, 0)) for a in pieces],
        out_specs=pl.BlockSpec((N_DEV, tb, n), lambda i: (0, i, 0)), out_shape=SDS((N_DEV, r, n), pieces[0].dtype),
        compiler_params=_cp("arbitrary"))(*pieces)


def _modulation(cvec, w_mod, b_mod):
    d, n = w_mod.shape

    def body(c_ref, w_ref, b_ref, s_ref, o_ref):
        cv = c_ref[...]
        s = cv * _sigmoid(cv)
        s_ref[...] = s
        o_ref[...] = _dot(_bf(s), w_ref[...]) + b_ref[...]

    return pl.pallas_call(body, name="modulation", out_shape=(SDS((8, d), F32), SDS((8, n), F32)),
                          compiler_params=_cp())(cvec, w_mod, b_mod)


def _norm_mod_proj(x_arr, g, shsc, w_main, w_gate, rows_total, row0, filled, name):
    m_rows, d = x_arr.shape
    n = w_main.shape[1]
    tb = _pick(m_rows, (1024, 256))
    cb = _pick(n, (2048, 1408, 1024, 768, 512, 384, 256, 128))
    gate = w_gate is not None
    nout = 3 if gate else 2
    nin = 5 if gate else 4
    rb = row0 // tb

    def body(*refs):
        x_ref, g_ref, ss_ref, wm_ref = refs[:4]
        wg_ref = refs[4] if gate else None
        outs = refs[len(refs) - 1 - nout:len(refs) - 1]
        hn_ref, z_ref = outs[0], outs[1]
        hn_sc = refs[-1]

        @pl.when(pl.program_id(1) == 0)
        def _():
            x = x_ref[...]
            r = lax.rsqrt(jnp.mean(x * x, axis=-1, keepdims=True) + EPS)
            hb = _bf((x * r * g_ref[...]) * (1.0 + ss_ref[1:2, :]) + ss_ref[0:1, :])
            hn_sc[...] = hb
            hn_ref[...] = hb
            if gate:
                outs[2][...] = _dot(hb, wg_ref[...])

        z_ref[...] = _bf(_dot(hn_sc[...], wm_ref[:, pl.ds(pl.multiple_of(pl.program_id(1) * cb, cb), cb)]))

    in_specs = [pl.BlockSpec((tb, d), lambda i, j: (i, 0)), pl.BlockSpec((1, d), lambda i, j: (0, 0)),
                pl.BlockSpec((2, d), lambda i, j: (0, 0)), _resident((d, n))]
    out_specs = [pl.BlockSpec((tb, d), lambda i, j: (rb + i, 0)), pl.BlockSpec((tb, cb), lambda i, j: (rb + i, j))]
    out_shape = [SDS((rows_total, d), BF16), SDS((rows_total, n), BF16)]
    args = [x_arr, g, shsc, w_main]
    if gate:
        in_specs.append(pl.BlockSpec((d, LANES), lambda i, j: (0, 0)))
        out_specs.append(pl.BlockSpec((tb, LANES), lambda i, j: (rb + i, 0)))
        out_shape.append(SDS((rows_total, LANES), F32))
        args.append(w_gate)
    aliases = {}
    if filled is not None:
        in_specs += [pl.BlockSpec(memory_space=pl.ANY)] * nout
        args += list(filled)
        aliases = {nin + k: k for k in range(nout)}
    return pl.pallas_call(
        body, name=name, grid=(m_rows // tb, n // cb), in_specs=in_specs, out_specs=out_specs, out_shape=out_shape,
        input_output_aliases=aliases, scratch_shapes=[pltpu.VMEM((tb, d), BF16)],
        compiler_params=_cp("arbitrary", "arbitrary"))(*args)


def _seg_masks(row, t_rows, s_rows):
    prev_ok = (row != 0) & (row != t_rows)
    next_ok = (row != t_rows - 1) & (row != s_rows - 1)
    return prev_ok, next_ok


def _shift_rows(z, halo_prev, halo_next, tb):
    loc = lax.broadcasted_iota(jnp.int32, (tb, 1), 0)
    zp = jnp.where(loc == 0, halo_prev, pltpu.roll(z, 1, 0))
    zn = jnp.where(loc == tb - 1, halo_next, pltpu.roll(z, tb - 1, 0))
    return zp, zn


def _qk_conv(z_main, conv_w, t_rows, md, qscale):
    s_rows = z_main.shape[0]
    tb = _pick(s_rows, (1280, 1024, 256))
    cb = _pick(md, (512, 256, 128))
    nb8 = tb // 8

    def body(zm, zp, zn, w_ref, o_ref):
        i, j = pl.program_id(0), pl.program_id(1)
        z = zm[...].astype(F32)
        zprev, znext = _shift_rows(z, zp[7:8, :].astype(F32), zn[0:1, :].astype(F32), tb)
        row = i * tb + lax.broadcasted_iota(jnp.int32, (tb, 1), 0)
        prev_ok, next_ok = _seg_masks(row, t_rows, s_rows)
        pre = (w_ref[0:1, :] * jnp.where(prev_ok, zprev, 0.0) + w_ref[1:2, :] * z
               + w_ref[2:3, :] * jnp.where(next_ok, znext, 0.0))
        scale = jnp.where(j * cb < md, qscale, 1.0)
        o_ref[...] = _bf(pre * _sigmoid(pre) * scale)

    return pl.pallas_call(
        body, name="qk_conv", grid=(s_rows // tb, 2 * md // cb),
        in_specs=[pl.BlockSpec((tb, cb), lambda i, j: (i, j)),
                  pl.BlockSpec((8, cb), lambda i, j: (jnp.maximum(i * nb8 - 1, 0), j)),
                  pl.BlockSpec((8, cb), lambda i, j: (jnp.minimum((i + 1) * nb8, s_rows // 8 - 1), j)),
                  pl.BlockSpec((8, cb), lambda i, j: (0, j))],
        out_specs=pl.BlockSpec((tb, cb), lambda i, j: (i, j)),
        out_shape=SDS((s_rows, 2 * md), BF16), compiler_params=_cp("arbitrary", "arbitrary"))(z_main, z_main, z_main, conv_w)


def _chunk_gates(gates, bias, rev):
    ln = gates.shape[0]
    gz = gates + bias
    logf = _log_sigmoid(gz)
    r_id = lax.broadcasted_iota(jnp.int32, (ln, ln), 0)
    c_id = lax.broadcasted_iota(jnp.int32, (ln, ln), 1)
    mask = (c_id >= r_id) if rev else (c_id <= r_id)
    mb = mask.astype(F32).astype(BF16)
    b_all = _mask_dot(mb, logf)
    g_all = jnp.sum(logf, axis=0, keepdims=True)
    return gz, b_all, b_all.T, gz.T, g_all, mask, mb


def _head_weights(b_col, b_row, i_row, m_in, mask):
    d = jnp.where(mask, b_col - b_row + i_row, NEG)
    inter = b_col + m_in
    m_row = jnp.maximum(inter, jnp.max(d, axis=1, keepdims=True))
    return jnp.exp(d - m_row), jnp.exp(inter - m_row), m_row


def _head_state_coeffs(g, b_col, i_col, m_in):
    a = g - b_col + i_col
    m_new = jnp.maximum(g + m_in, jnp.max(a, axis=0, keepdims=True))
    return jnp.exp(g + m_in - m_new), jnp.exp(a - m_new), m_new


def _mlstm_fwd(qk, z_main, zg, bias, nh):
    s_rows = qk.shape[0]
    md = qk.shape[1] // 2
    dh = md // nh
    nc = s_rows // LCH
    ln = LCH

    def chunk_f(i):
        return jnp.where(i == 0, nc - 1, i - 1)

    def chunk_b(i):
        return jnp.where(i == 0, nc - 1, nc - 1 - i)

    def body(qf, kf, vf, gf, qb, kb, vb, gb, bias_ref, hf_ref, hb_ref, cf_ref, nf_ref, mf_ref, cb_ref, nb_ref, mb_ref,
             c_sc, n_sc, m_sc):
        i = pl.program_id(0)

        @pl.when(i == 0)
        def _():
            c_sc[...] = jnp.zeros_like(c_sc)
            n_sc[...] = jnp.zeros_like(n_sc)
            m_sc[...] = jnp.full(m_sc.shape, M_INIT, F32)

        sides = ((qf, kf, vf, gf, hf_ref, cf_ref, nf_ref, mf_ref), (qb, kb, vb, gb, hb_ref, cb_ref, nb_ref, mb_ref))
        gates = [_chunk_gates(s[3][...], bias_ref[...], dr == 1) for dr, s in enumerate(sides)]
        units = []
        for dr, (q_ref, k_ref, v_ref, _, h_ref, c_out, n_out, m_out) in enumerate(sides):
            gz, b_all, b_t, g_t, g_all, mask, _ = gates[dr]
            for h in range(nh):
                ci, cf = 2 * dr * nh + h, (2 * dr + 1) * nh + h
                sl = slice(h * dh, (h + 1) * dh)
                u = dict(dr=dr, h=h, sl=sl, h_ref=h_ref, q=q_ref[:, sl], k=k_ref[:, sl], v=v_ref[:, sl],
                         c_in=c_sc[dr, h], n_in=n_sc[dr, h, 0:1, :], m_in=m_sc[dr, h, 0:1, 0:1],
                         b_col=b_all[:, cf:cf + 1], i_col=gz[:, ci:ci + 1], g=g_all[:, cf:cf + 1])
                c_out[sl, :] = u["c_in"]
                n_out[:, sl] = n_sc[dr, h]
                m_out[h] = m_sc[dr, h]
                u["w"], u["w_int"], u["m_row"] = _head_weights(u["b_col"], b_t[cf:cf + 1, :], g_t[ci:ci + 1, :], u["m_in"], mask)
                u["qk"] = _dot_nt(u["q"], u["k"])
                units.append(u)
        for u in units:
            u["s_mat"] = u["qk"] * u["w"]
            u["qc"] = _dot(u["q"], _bf(u["c_in"]))
            u["a_old"], u["coef"], u["m_new"] = _head_state_coeffs(u["g"], u["b_col"], u["i_col"], u["m_in"])
            u["kw"] = u["k"].astype(F32) * u["coef"]
        for u in units:
            u["sv"] = _dot(_bf(u["s_mat"]), u["v"])
            u["kv"] = _dot_tn(_bf(u["kw"]), u["v"])
        for u in units:
            dr, h = u["dr"], u["h"]
            num = u["sv"] + u["w_int"] * u["qc"]
            den = (jnp.sum(u["s_mat"], axis=1, keepdims=True)
                   + u["w_int"] * jnp.sum(u["q"].astype(F32) * u["n_in"], axis=1, keepdims=True))
            u["h_ref"][:, u["sl"]] = _bf(num / jnp.maximum(jnp.abs(den), jnp.exp(-u["m_row"])))
            c_sc[dr, h] = u["a_old"] * u["c_in"] + u["kv"]
            n_sc[dr, h] = jnp.broadcast_to(u["a_old"] * u["n_in"] + jnp.sum(u["kw"], axis=0, keepdims=True), (8, dh))
            m_sc[dr, h] = jnp.broadcast_to(u["m_new"], (8, LANES))

    def tok(cfn, col):
        return pl.BlockSpec((ln, md), lambda i: (cfn(i), col))

    def gat(cfn):
        return pl.BlockSpec((ln, LANES), lambda i: (cfn(i), 0))

    def st(cfn, shape):
        return pl.BlockSpec((None,) + shape, lambda i: (cfn(i),) + (0,) * len(shape))

    st_shapes = ((nh * dh, dh), (8, md), (nh, 8, LANES))
    return pl.pallas_call(
        body, name="mlstm_fwd", grid=(nc,),
        in_specs=[tok(chunk_f, 0), tok(chunk_f, 1), tok(chunk_f, 2), gat(chunk_f),
                  tok(chunk_b, 0), tok(chunk_b, 1), tok(chunk_b, 2), gat(chunk_b),
                  pl.BlockSpec((1, LANES), lambda i: (0, 0))],
        out_specs=[tok(chunk_f, 0), tok(chunk_b, 0)] + [st(chunk_f, s) for s in st_shapes] + [st(chunk_b, s) for s in st_shapes],
        out_shape=[SDS((s_rows, md), BF16)] * 2 + [SDS((nc,) + s, F32) for s in st_shapes] * 2,
        scratch_shapes=[pltpu.VMEM((2, nh, dh, dh), F32), pltpu.VMEM((2, nh, 8, dh), F32), pltpu.VMEM((2, nh, 8, LANES), F32)],
        compiler_params=_cp("arbitrary"))(qk, qk, z_main, zg, qk, qk, z_main, zg, bias)


def _head_rms(hs, nh, dh):
    parts, scales = [], []
    for h in range(nh):
        hh = hs[:, h * dh:(h + 1) * dh]
        r = lax.rsqrt(jnp.mean(hh * hh, axis=-1, keepdims=True) + EPS)
        parts.append(hh * r)
        scales.append(r)
    return jnp.concatenate(parts, axis=1), scales


def _layer_norm(v):
    vc = v - jnp.mean(v, axis=-1, keepdims=True)
    r = lax.rsqrt(jnp.mean(vc * vc, axis=-1, keepdims=True) + EPS)
    return vc * r, r


def _sgu_mix(vnb, ws_ref, bs_ref, tb, ng, gd, sc):
    rows = []
    for ch in range(tb // sc):
        cols = []
        for g in range(ng):
            blk = vnb[ch * sc:(ch + 1) * sc, g * gd:(g + 1) * gd]
            cols.append(_dot(_bf(ws_ref[g]), blk) + bs_ref[:, g:g + 1])
        rows.append(jnp.concatenate(cols, axis=1))
    return jnp.concatenate(rows, axis=0)


def _mixer_fwd(hf, hb, z_main, xs, hg, lng, lnb, w_s, b_st, wbm, wbs, wout, mx2, t_rows, nh):
    d = xs.shape[1]
    ng, sc = w_s.shape[0], w_s.shape[1]
    dh, gd = d // nh, d // ng
    tb = _pick(t_rows, (256,))

    def body(hf_ref, hb_ref, zo, zu, zv, zgm, zgg, x_ref, hg_ref, lng_ref, lnb_ref, ws_ref, bs_ref, wbm_ref, wbs_ref,
             wo_ref, mx2_ref, h1_ref, ym_ref, ys_ref, pm_ref, ps_ref, y_ref, out_ref):
        hs = hf_ref[...].astype(F32) + hb_ref[...].astype(F32)
        hn, _ = _head_rms(hs, nh, dh)
        ym = _bf(_sigmoid(zo[...].astype(F32)) * (hn * hg_ref[...]))
        ym_ref[...] = ym
        vhat, _ = _layer_norm(_gelu(zv[...].astype(F32)))
        vnb = _bf(vhat * lng_ref[...] + lnb_ref[...])
        ys = _bf(_gelu(zu[...].astype(F32)) * _sgu_mix(vnb, ws_ref, bs_ref, tb, ng, gd, sc))
        ys_ref[...] = ys
        pm = _dot(ym, wbm_ref[...])
        ps = _dot(ys, wbs_ref[...])
        pm_ref[...] = _bf(pm)
        ps_ref[...] = _bf(ps)
        y = _bf(_sigmoid(zgm[...].astype(F32)) * pm + _sigmoid(zgg[...].astype(F32)) * ps)
        y_ref[...] = y
        out = _dot(y, wo_ref[...])
        out_ref[...] = _bf(out)
        h1_ref[...] = x_ref[...] + mx2_ref[...] * out

    def tok(col):
        return pl.BlockSpec((tb, d), lambda i: (i, col))

    def full(shape):
        return pl.BlockSpec(shape, lambda i: (0,) * len(shape))

    return pl.pallas_call(
        body, name="mixer_fwd", grid=(t_rows // tb,),
        in_specs=[tok(0), tok(0), tok(3), tok(4), tok(5), tok(6), tok(7), tok(0), full((1, d)), full((1, d)), full((1, d)),
                  full((ng, sc, sc)), full((sc, LANES)), full((d, d)), full((d, d)), full((d, d)), full((1, d))],
        out_specs=[tok(0)] * 7,
        out_shape=[SDS((t_rows, d), F32)] + [SDS((t_rows, d), BF16)] * 6,
        compiler_params=_cp("arbitrary"))(hf, hb, z_main, z_main, z_main, z_main, z_main, xs, hg, lng, lnb, w_s, b_st,
                                          wbm, wbs, wout, mx2)


def _resident(shape):
    return pl.BlockSpec(shape, lambda *_: (0,) * len(shape), pipeline_mode=pl.Buffered(1))


def _grid_taps(a_ext, n_ext):
    col = lax.broadcasted_iota(jnp.int32, (n_ext, 1), 0) % GRID_W
    left = jnp.where(col != 0, pltpu.roll(a_ext, 1, 0), 0.0)
    right = jnp.where(col != GRID_W - 1, pltpu.roll(a_ext, n_ext - 1, 0), 0.0)
    return left, right


def _with_halo(prev, main, nxt, i, ni, tb):
    ext = jnp.concatenate([prev, main, nxt], axis=0).astype(F32)
    pos = lax.broadcasted_iota(jnp.int32, (tb + 2 * GRID_W, 1), 0)
    inside = ((pos >= GRID_W) | (i > 0)) & ((pos < tb + GRID_W) | (i < ni - 1))
    return jnp.where(inside, ext, 0.0)


def _halo_specs(tb, cb, t_rows, col0=0):
    nh64 = tb // GRID_W
    return [pl.BlockSpec((tb, cb), lambda i, j: (i, col0 + j)),
            pl.BlockSpec((GRID_W, cb), lambda i, j: (jnp.maximum(i * nh64 - 1, 0), col0 + j)),
            pl.BlockSpec((GRID_W, cb), lambda i, j: (jnp.minimum((i + 1) * nh64, t_rows // GRID_W - 1), col0 + j))]


def _ffn_tail(ab, w_conv9, w_down, h1, mx5, gfin, target, dff):
    t_rows, d = h1.shape
    tb = _pick(t_rows, (256,))
    cb = _pick(dff, (1408, 256, 128))
    ni, nj = t_rows // tb, dff // cb
    n_ext = tb + 2 * GRID_W

    def body(am, ap, an, b_ref, wc_ref, wd_ref, h1_ref, mx5_ref, gf_ref, tg_ref, ac_ref, f_ref, dh2_ref, dffn_ref, st_ref, acc):
        i, j = pl.program_id(0), pl.program_id(1)
        a_ext = _with_halo(ap[...], am[...], an[...], i, ni, tb)
        left, right = _grid_taps(a_ext, n_ext)
        conv = jnp.zeros((tb, cb), F32)
        for di in range(3):
            o = di * GRID_W
            conv = conv + (wc_ref[3 * di:3 * di + 1, :] * left[o:o + tb] + wc_ref[3 * di + 1:3 * di + 2, :] * a_ext[o:o + tb]
                           + wc_ref[3 * di + 2:3 * di + 3, :] * right[o:o + tb])
        ac_ref[...] = _bf(conv)
        fb = _bf(conv * _sigmoid(conv) * b_ref[...].astype(F32))
        f_ref[...] = fb

        @pl.when(j == 0)
        def _():
            acc[...] = jnp.zeros_like(acc)

        @pl.when((i == 0) & (j == 0))
        def _():
            st_ref[...] = jnp.zeros_like(st_ref)

        acc[...] += _dot(fb, wd_ref[pl.ds(pl.multiple_of(j * cb, cb), cb), :])

        @pl.when(j == nj - 1)
        def _():
            ffn = acc[...]
            h2 = h1_ref[...] + mx5_ref[...] * ffn
            r = lax.rsqrt(jnp.mean(h2 * h2, axis=-1, keepdims=True) + EPS)
            xn = h2 * r
            e = xn * gf_ref[...] - tg_ref[...]
            loss = 0.5 * jnp.sum(jnp.sum(e * e, axis=1, keepdims=True), axis=0, keepdims=True) / d
            dy = e * (1.0 / d)
            dxn = dy * gf_ref[...]
            dh2 = r * (dxn - xn * jnp.mean(dxn * xn, axis=-1, keepdims=True))
            dh2_ref[...] = dh2
            dffn_ref[...] = _bf(dh2 * mx5_ref[...])
            st_ref[...] += jnp.concatenate(
                [jnp.sum(dy * xn, axis=0, keepdims=True), jnp.sum(dh2 * ffn, axis=0, keepdims=True),
                 jnp.broadcast_to(loss, (1, d)), jnp.zeros((5, d), F32)], axis=0)

    def tokd():
        return pl.BlockSpec((tb, d), lambda i, j: (i, 0))

    def rowd():
        return pl.BlockSpec((1, d), lambda i, j: (0, 0))

    return pl.pallas_call(
        body, name="ffn_tail", grid=(ni, nj),
        in_specs=_halo_specs(tb, cb, t_rows) + [pl.BlockSpec((tb, cb), lambda i, j: (i, nj + j)),
                                                pl.BlockSpec((16, cb), lambda i, j: (0, j)),
                                                _resident((dff, d)), tokd(), rowd(), rowd(), tokd()],
        out_specs=[pl.BlockSpec((tb, cb), lambda i, j: (i, j)), pl.BlockSpec((tb, cb), lambda i, j: (i, j)), tokd(), tokd(),
                   pl.BlockSpec((8, d), lambda i, j: (0, 0))],
        out_shape=[SDS((t_rows, dff), BF16), SDS((t_rows, dff), BF16), SDS((t_rows, d), F32), SDS((t_rows, d), BF16),
                   SDS((8, d), F32)],
        scratch_shapes=[pltpu.VMEM((tb, d), F32)],
        compiler_params=_cp("arbitrary", "arbitrary"))(ab, ab, ab, ab, w_conv9, w_down, h1, mx5, gfin, target)


def _ffn_bwd_gate(dffn, w_down, aconv, ab, dff):
    t_rows, d = dffn.shape
    tb = _pick(t_rows, (512,))
    cb = _pick(dff, (1408, 256, 128))
    nj = dff // cb

    def body(g_ref, wd_ref, ac_ref, b_ref, db_ref, dac_ref):
        df = _dot_nt(g_ref[...], wd_ref[pl.ds(pl.multiple_of(pl.program_id(1) * cb, cb), cb), :])
        ac = ac_ref[...].astype(F32)
        sa = _sigmoid(ac)
        db_ref[...] = _bf(df * ac * sa)
        dac_ref[...] = _bf(df * b_ref[...].astype(F32) * (sa * (1.0 + ac * (1.0 - sa))))

    blk = pl.BlockSpec((tb, cb), lambda i, j: (i, j))
    return pl.pallas_call(
        body, name="ffn_bwd_gate", grid=(t_rows // tb, nj),
        in_specs=[pl.BlockSpec((tb, d), lambda i, j: (i, 0)), _resident((dff, d)), blk,
                  pl.BlockSpec((tb, cb), lambda i, j: (i, nj + j))],
        out_specs=[blk, blk], out_shape=[SDS((t_rows, dff), BF16)] * 2,
        compiler_params=_cp("arbitrary", "arbitrary"))(dffn, w_down, aconv, ab)


def _ffn_conv_bwd(dac, ab, w_conv9, dff):
    t_rows = dac.shape[0]
    tb = _pick(t_rows, (256,))
    cb = _pick(dff, (1408, 256, 128))
    ni, nj = t_rows // tb, dff // cb
    n_ext = tb + 2 * GRID_W
    nh64 = tb // GRID_W

    def body(dm, dp, dn, am, ap, an, wc_ref, da_ref, gw_ref):
        i = pl.program_id(1)
        d_ext = _with_halo(dp[...], dm[...], dn[...], i, ni, tb)
        a_ext = _with_halo(ap[...], am[...], an[...], i, ni, tb)
        d_left, d_right = _grid_taps(d_ext, n_ext)
        a_left, a_right = _grid_taps(a_ext, n_ext)
        dmain = d_ext[GRID_W:GRID_W + tb]
        da = jnp.zeros((tb, cb), F32)
        rows = []
        for di in range(3):
            o = (2 - di) * GRID_W
            da = da + (wc_ref[3 * di:3 * di + 1, :] * d_right[o:o + tb] + wc_ref[3 * di + 1:3 * di + 2, :] * d_ext[o:o + tb]
                       + wc_ref[3 * di + 2:3 * di + 3, :] * d_left[o:o + tb])
            o = di * GRID_W
            for tap in (a_left, a_ext, a_right):
                rows.append(jnp.sum(dmain * tap[o:o + tb], axis=0, keepdims=True))
        da_ref[...] = _bf(da)

        @pl.when(i == 0)
        def _():
            gw_ref[...] = jnp.zeros_like(gw_ref)

        gw_ref[...] += jnp.concatenate(rows + [jnp.zeros((7, cb), F32)], axis=0)

    def halo(col0):
        return [pl.BlockSpec((tb, cb), lambda j, i: (i, col0 + j)),
                pl.BlockSpec((GRID_W, cb), lambda j, i: (jnp.maximum(i * nh64 - 1, 0), col0 + j)),
                pl.BlockSpec((GRID_W, cb), lambda j, i: (jnp.minimum((i + 1) * nh64, t_rows // GRID_W - 1), col0 + j))]

    return pl.pallas_call(
        body, name="ffn_conv_bwd", grid=(nj, ni),
        in_specs=halo(0) + halo(0) + [pl.BlockSpec((16, cb), lambda j, i: (0, j))],
        out_specs=[pl.BlockSpec((tb, cb), lambda j, i: (i, j)), pl.BlockSpec((16, cb), lambda j, i: (0, j))],
        out_shape=[SDS((t_rows, dff), BF16), SDS((16, dff), F32)],
        compiler_params=_cp("arbitrary", "arbitrary"))(dac, dac, dac, ab, ab, ab, w_conv9)


def _proj_norm_bwd(pairs, x_arr, x_row0, g, scale, resid, m_rows, name, row_blocks=(1024, 256)):
    d = x_arr.shape[1]
    tm = _pick(m_rows, row_blocks)
    te = 256
    ni = m_rows // tm
    starts, total = [], 0
    for (_, _, _, _, k_p, tk_p) in pairs:
        starts.append(total)
        total += k_p // tk_p
    npairs = len(pairs)
    has_dx = resid is not None

    def body(*refs):
        a_refs, b_refs = refs[0:2 * npairs:2], refs[1:2 * npairs:2]
        rest = refs[2 * npairs:]
        if has_dx:
            x_ref, g_ref, sc_ref, r_ref, dx_ref, st_ref, acc = rest
        else:
            x_ref, g_ref, sc_ref, st_ref, acc = rest
        i, k = pl.program_id(0), pl.program_id(1)

        @pl.when(k == 0)
        def _():
            acc[...] = jnp.zeros_like(acc)

        @pl.when((i == 0) & (k == 0))
        def _():
            st_ref[...] = jnp.zeros_like(st_ref)

        for p in range(npairs):
            nk = pairs[p][4] // pairs[p][5]

            @pl.when((k >= starts[p]) & (k < starts[p] + nk))
            def _(p=p):
                acc[...] += _dot_nt(a_refs[p][...], b_refs[p][...])

        @pl.when(k == total - 1)
        def _():
            sums = [jnp.zeros((1, d), F32)] * 3
            for r0 in range(0, tm, te):
                rows = slice(r0, r0 + te)
                dhn = acc[rows, :]
                x = x_ref[rows, :]
                r = lax.rsqrt(jnp.mean(x * x, axis=-1, keepdims=True) + EPS)
                xn = x * r
                dmod = dhn * (1.0 + sc_ref[...])
                dxn = dmod * g_ref[...]
                if has_dx:
                    dx_ref[rows, :] = r * (dxn - xn * jnp.mean(dxn * xn, axis=-1, keepdims=True)) + r_ref[rows, :]
                sums = [sums[0] + jnp.sum(dmod * xn, axis=0, keepdims=True), sums[1] + jnp.sum(dhn, axis=0, keepdims=True),
                        sums[2] + jnp.sum(dhn * (xn * g_ref[...]), axis=0, keepdims=True)]
            st_ref[...] += jnp.concatenate(sums + [jnp.zeros((5, d), F32)], axis=0)

    in_specs, args = [], []
    for p, (a, a_row0, b, b_col0, k_p, tk_p) in enumerate(pairs):
        nk, s0, ar, bc = k_p // tk_p, starts[p], a_row0 // tm, b_col0 // tk_p

        def kk(k, s0=s0, nk=nk):
            return jnp.clip(k - s0, 0, nk - 1)

        in_specs.append(pl.BlockSpec((tm, tk_p), lambda i, k, ar=ar, kk=kk: (ar + i, kk(k))))
        in_specs.append(pl.BlockSpec((d, tk_p), lambda i, k, bc=bc, kk=kk: (0, bc + kk(k)),
                                     pipeline_mode=pl.Buffered(1 if nk == 1 else 2)))
        args += [a, b]
    xr = x_row0 // tm
    in_specs += [pl.BlockSpec((tm, d), lambda i, k: (xr + i, 0)), pl.BlockSpec((1, d), lambda i, k: (0, 0)),
                 pl.BlockSpec((1, d), lambda i, k: (0, 0))]
    args += [x_arr, g, scale]
    out_specs, out_shape = [], []
    if has_dx:
        in_specs.append(pl.BlockSpec((tm, d), lambda i, k: (i, 0)))
        args.append(resid)
        out_specs.append(pl.BlockSpec((tm, d), lambda i, k: (i, 0)))
        out_shape.append(SDS((m_rows, d), F32))
    out_specs.append(pl.BlockSpec((8, d), lambda i, k: (0, 0)))
    out_shape.append(SDS((8, d), F32))
    return pl.pallas_call(
        body, name=name, grid=(ni, total), in_specs=in_specs, out_specs=out_specs, out_shape=out_shape,
        scratch_shapes=[pltpu.VMEM((tm, d), F32)], compiler_params=_cp("arbitrary", "arbitrary"))(*args)


def _wgrad(a, b, k_rows, name):
    m, n = a.shape[1], b.shape[1]
    tm = _pick(m, (1408, 1024, 512, 384, 256, 128))
    tn = _pick(n, (3072, 2816, 2560, 1408, 1024, 768, 512, 384, 256, 128))
    tk = _pick(k_rows, (1280, 1024, 256))
    nk = k_rows // tk

    def body(a_ref, b_ref, o_ref, acc):
        k = pl.program_id(2)

        @pl.when(k == 0)
        def _():
            acc[...] = jnp.zeros_like(acc)

        acc[...] += _dot_tn(a_ref[...], b_ref[...])

        @pl.when(k == nk - 1)
        def _():
            o_ref[...] = _bf(acc[...])

    return pl.pallas_call(
        body, name=name, grid=(m // tm, n // tn, nk),
        in_specs=[pl.BlockSpec((tk, tm), lambda i, j, k: (k, i)), pl.BlockSpec((tk, tn), lambda i, j, k: (k, j))],
        out_specs=pl.BlockSpec((tm, tn), lambda i, j, k: (i, j)), out_shape=SDS((m, n), BF16),
        scratch_shapes=[pltpu.VMEM((tm, tn), F32)],
        compiler_params=_cp("arbitrary", "arbitrary", "arbitrary"))(a, b)


def _lane_put(col, lane_idx):
    lane = lax.broadcasted_iota(jnp.int32, (1, LANES), 1)
    return jnp.where(lane == lane_idx, col, 0.0)


def _mixer_bwd(dh1, out, hf, hb, z_main, pm, ps, hg, lng, lnb, w_s, b_st, wbm, wbs, wout, mx2, t_rows, nh):
    d = dh1.shape[1]
    ng, sc = w_s.shape[0], w_s.shape[1]
    dh, gd = d // nh, d // ng
    tb = _pick(t_rows, (256,))

    def body(dh1_ref, out_ref, hf_ref, hb_ref, zo, zu, zv, zgm, zgg, pm_ref, ps_ref, hg_ref, lng_ref, lnb_ref, ws_ref, bs_ref,
             wbm_ref, wbs_ref, wo_ref, mx2_ref, dz_ref, dhs_ref, dout_ref, dpm_ref, dps_ref, st_ref, dws_ref, dbs_ref):
        i = pl.program_id(0)

        @pl.when(i == 0)
        def _():
            st_ref[...] = jnp.zeros_like(st_ref)
            dws_ref[...] = jnp.zeros_like(dws_ref)
            dbs_ref[...] = jnp.zeros_like(dbs_ref)

        dh1v = dh1_ref[...]
        doutb = _bf(dh1v * mx2_ref[...])
        dout_ref[...] = doutb
        d_mx2 = jnp.sum(dh1v * out_ref[...].astype(F32), axis=0, keepdims=True)
        dy = _dot_nt(doutb, wo_ref[...])
        sgm, sgg = _sigmoid(zgm[...].astype(F32)), _sigmoid(zgg[...].astype(F32))
        dpmb, dpsb = _bf(dy * sgm), _bf(dy * sgg)
        dpm_ref[...] = dpmb
        dps_ref[...] = dpsb
        dz_ref[:, 3 * d:4 * d] = _bf(dy * pm_ref[...].astype(F32) * sgm * (1.0 - sgm))
        dz_ref[:, 4 * d:5 * d] = _bf(dy * ps_ref[...].astype(F32) * sgg * (1.0 - sgg))
        dym = _dot_nt(dpmb, wbm_ref[...])
        dys = _dot_nt(dpsb, wbs_ref[...])
        hs = hf_ref[...].astype(F32) + hb_ref[...].astype(F32)
        hn, scales = _head_rms(hs, nh, dh)
        so = _sigmoid(zo[...].astype(F32))
        dz_ref[:, 0:d] = _bf(dym * (hn * hg_ref[...]) * so * (1.0 - so))
        dhmn = dym * so
        d_hg = jnp.sum(dhmn * hn, axis=0, keepdims=True)
        dhn = dhmn * hg_ref[...]
        for h in range(nh):
            sl = slice(h * dh, (h + 1) * dh)
            dhs_ref[:, sl] = _bf(scales[h] * (dhn[:, sl] - hn[:, sl] * jnp.mean(dhn[:, sl] * hn[:, sl], axis=-1, keepdims=True)))
        zuv, zvv = zu[...].astype(F32), zv[...].astype(F32)
        u, du_dz = _gelu_and_grad(zuv)
        vg, dvg_dz = _gelu_and_grad(zvv)
        vhat, rstd = _layer_norm(vg)
        vnb = _bf(vhat * lng_ref[...] + lnb_ref[...])
        mixed = _sgu_mix(vnb, ws_ref, bs_ref, tb, ng, gd, sc)
        dz_ref[:, d:2 * d] = _bf(dys * mixed * du_dz)
        dmix = dys * u
        rows = []
        dbs = jnp.zeros((sc, LANES), F32)
        for ch in range(tb // sc):
            cols = []
            for g in range(ng):
                dm = dmix[ch * sc:(ch + 1) * sc, g * gd:(g + 1) * gd]
                dmb = _bf(dm)
                dws_ref[g] += _dot_nt(dmb, vnb[ch * sc:(ch + 1) * sc, g * gd:(g + 1) * gd])
                dbs = dbs + _lane_put(jnp.sum(dm, axis=1, keepdims=True), g)
                cols.append(_dot_tn(_bf(ws_ref[g]), dmb))
            rows.append(jnp.concatenate(cols, axis=1))
        dbs_ref[...] += dbs
        dvn = jnp.concatenate(rows, axis=0)
        d_lng = jnp.sum(dvn * vhat, axis=0, keepdims=True)
        d_lnb = jnp.sum(dvn, axis=0, keepdims=True)
        dvh = dvn * lng_ref[...]
        dvg = rstd * (dvh - jnp.mean(dvh, axis=-1, keepdims=True) - vhat * jnp.mean(dvh * vhat, axis=-1, keepdims=True))
        dz_ref[:, 2 * d:3 * d] = _bf(dvg * dvg_dz)
        st_ref[...] += jnp.concatenate([d_mx2, d_hg, d_lng, d_lnb, jnp.zeros((4, d), F32)], axis=0)

    def tok(col):
        return pl.BlockSpec((tb, d), lambda i: (i, col))

    def full(shape):
        return pl.BlockSpec(shape, lambda i: (0,) * len(shape))

    return pl.pallas_call(
        body, name="mixer_bwd", grid=(t_rows // tb,),
        in_specs=[tok(0), tok(0), tok(0), tok(0), tok(3), tok(4), tok(5), tok(6), tok(7), tok(0), tok(0), full((1, d)),
                  full((1, d)), full((1, d)), full((ng, sc, sc)), full((sc, LANES)), full((d, d)), full((d, d)), full((d, d)),
                  full((1, d))],
        out_specs=[pl.BlockSpec((tb, 5 * d), lambda i: (i, 0)), tok(0), tok(0), tok(0), tok(0), full((8, d)), full((ng, sc, sc)),
                   full((sc, LANES))],
        out_shape=[SDS((t_rows, 5 * d), BF16)] + [SDS((t_rows, d), BF16)] * 4 + [SDS((8, d), F32), SDS((ng, sc, sc), F32),
                                                                                SDS((sc, LANES), F32)],
        compiler_params=_cp("arbitrary"))(dh1, out, hf, hb, z_main, z_main, z_main, z_main, z_main, pm, ps, hg, lng, lnb, w_s,
                                          b_st, wbm, wbs, wout, mx2)


def _mlstm_bwd(qk, z_main, zg, bias, dhs, hf, hb, states_f, states_b, nh, t_rows):
    s_rows = qk.shape[0]
    md = qk.shape[1] // 2
    dh = md // nh
    nc = s_rows // LCH
    nx = t_rows // LCH
    ln = LCH

    def chunk_f(i):
        return jnp.where(i == nc - 1, nc - 1, nc - 2 - i)

    def chunk_b(i):
        return jnp.where(i == nc - 1, nc - 1, i)

    def body(qf, kf, vf, gf, dhf, hsf, cf, nf, mf_, qb, kb, vb, gb, dhb, hsb, cb, nb, mb_, bias_ref, dqkvf_ref, dgf_ref, dqkvb_ref,
             dgb_ref, dc_sc, dn_sc):
        i = pl.program_id(0)
        is_ctx = i == nc - 1

        @pl.when(i == 0)
        def _():
            dc_sc[...] = jnp.zeros_like(dc_sc)
            dn_sc[...] = jnp.zeros_like(dn_sc)

        sides = ((qf, kf, vf, gf, dhf, hsf, cf, nf, mf_, dqkvf_ref, dgf_ref), (qb, kb, vb, gb, dhb, hsb, cb, nb, mb_, dqkvb_ref, dgb_ref))
        gates = [_chunk_gates(s[3][...], bias_ref[...], dr == 1) for dr, s in enumerate(sides)]
        units = []
        for dr, (q_ref, k_ref, v_ref, _, dh_ref, hs_ref, c_ref, n_ref, m_ref, dqkv_ref, _) in enumerate(sides):
            gz, b_all, b_t, g_t, g_all, mask, _ = gates[dr]
            for h in range(nh):
                ci, cfl = 2 * dr * nh + h, (2 * dr + 1) * nh + h
                sl = slice(h * dh, (h + 1) * dh)
                u = dict(dr=dr, h=h, sl=sl, ci=ci, cfl=cfl, dqkv_ref=dqkv_ref, q=q_ref[:, sl], k=k_ref[:, sl], v=v_ref[:, sl],
                         dhv=jnp.where(is_ctx, 0.0, dh_ref[:, sl].astype(F32)), hs=hs_ref[:, sl].astype(F32),
                         c_in=c_ref[sl, :], n_in=n_ref[0:1, sl], m_in=m_ref[h, 0:1, 0:1],
                         b_col=b_all[:, cfl:cfl + 1], i_col=gz[:, ci:ci + 1], g=g_all[:, cfl:cfl + 1],
                         dc_new=dc_sc[dr, h], dn_new=dn_sc[dr, h, 0:1, :])
                u["qf32"], u["kf32"] = u["q"].astype(F32), u["k"].astype(F32)
                u["w"], u["w_int"], u["m_row"] = _head_weights(u["b_col"], b_t[cfl:cfl + 1, :], g_t[ci:ci + 1, :], u["m_in"], mask)
                u["qk"] = _dot_nt(u["q"], u["k"])
                units.append(u)
        for u in units:
            s_mat = u["qk"] * u["w"]
            u["s_mat"], u["sb"], u["cb16"], u["dcb"] = s_mat, _bf(s_mat), _bf(u["c_in"]), _bf(u["dc_new"])
            den = jnp.sum(s_mat, axis=1, keepdims=True) + u["w_int"] * jnp.sum(u["qf32"] * u["n_in"], axis=1, keepdims=True)
            e_m = jnp.exp(-u["m_row"])
            dnm = jnp.maximum(jnp.abs(den), e_m)
            hdh = jnp.sum(u["hs"] * u["dhv"], axis=1, keepdims=True)
            u["dden"] = jnp.where(jnp.abs(den) > e_m, -(hdh / dnm) * jnp.sign(den), 0.0)
            u["dnum_b"] = _bf(u["dhv"] / dnm)
            u["a_old"], u["coef"], _ = _head_state_coeffs(u["g"], u["b_col"], u["i_col"], u["m_in"])
            u["dsm"] = _dot_nt(u["dnum_b"], u["v"])
            u["qct"] = _dot_nt(u["dnum_b"], u["cb16"])
            u["vdc"] = _dot_nt(u["v"], u["dcb"])
        for u in units:
            ds = u["dsm"] + u["dden"]
            u["pb"] = _bf(u["w"] * ds)
            u["gmat"] = u["s_mat"] * ds
            u["dv1"] = _dot_tn(u["sb"], u["dnum_b"])
            u["dv2"] = _dot(_bf(u["kf32"] * u["coef"]), u["dcb"])
            u["dcu"] = _dot_tn(_bf(u["qf32"] * u["w_int"]), u["dnum_b"])
        for u in units:
            u["dq1"] = _dot(u["pb"], u["k"])
            u["dk1"] = _dot_tn(u["pb"], u["q"])
        acc = [dict(x1=jnp.zeros((ln, LANES), F32), x2=jnp.zeros((ln, LANES), F32), dig=jnp.zeros((ln, LANES), F32),
                    e_row=jnp.zeros((1, LANES), F32)) for _ in range(2)]
        for u in units:
            dr, h, sl, a = u["dr"], u["h"], u["sl"], acc[u["dr"]]
            dq_inter = u["w_int"] * (u["qct"] + u["dden"] * u["n_in"])
            dk_state = u["coef"] * (u["vdc"] + u["dn_new"])
            u["dqkv_ref"][:, sl] = _bf(u["dq1"] + dq_inter)
            u["dqkv_ref"][:, md + h * dh:md + (h + 1) * dh] = _bf(u["dk1"] + dk_state)
            u["dqkv_ref"][:, 2 * md + h * dh:2 * md + (h + 1) * dh] = _bf(u["dv1"] + u["dv2"])
            row_intra = jnp.sum(u["gmat"], axis=1, keepdims=True)
            col_intra = jnp.sum(u["gmat"].T, axis=1, keepdims=True)
            row_inter = jnp.sum(u["qf32"] * dq_inter, axis=1, keepdims=True)
            col_inter = jnp.sum(u["kf32"] * dk_state, axis=1, keepdims=True)
            e_old = u["a_old"] * (jnp.sum(jnp.sum(u["dc_new"] * u["c_in"], axis=1, keepdims=True), axis=0, keepdims=True)
                                  + jnp.sum(u["dn_new"] * u["n_in"], axis=1, keepdims=True))
            a["x1"] = a["x1"] + _lane_put(row_intra - col_intra + row_inter, u["cfl"])
            a["x2"] = a["x2"] + _lane_put(col_inter, u["cfl"])
            a["e_row"] = a["e_row"] + _lane_put(e_old, u["cfl"])
            a["dig"] = a["dig"] + _lane_put(col_intra + col_inter, u["ci"])
            dc_sc[dr, h] = u["a_old"] * u["dc_new"] + u["dcu"]
            dn_sc[dr, h] = jnp.broadcast_to(
                u["a_old"] * u["dn_new"] + jnp.sum(u["qf32"] * (u["w_int"] * u["dden"]), axis=0, keepdims=True), (8, dh))
        for dr, s in enumerate(sides):
            gz, mfl, a = gates[dr][0], gates[dr][6], acc[dr]
            dlogf = _mask_dot_t(mfl, a["x1"]) + _mask_dot(mfl, a["x2"]) - a["x2"] + a["e_row"]
            s[10][...] = a["dig"] + dlogf / (1.0 + jnp.exp(gz))

    def tok(cfn, col):
        return pl.BlockSpec((ln, md), lambda i: (cfn(i), col))

    def dht(cfn):
        return pl.BlockSpec((ln, md), lambda i: (jnp.minimum(cfn(i), nx - 1), 0))

    def gat(cfn):
        return pl.BlockSpec((ln, LANES), lambda i: (cfn(i), 0))

    def st(cfn, shape):
        return pl.BlockSpec((None,) + shape, lambda i: (cfn(i),) + (0,) * len(shape))

    st_shapes = ((nh * dh, dh), (8, md), (nh, 8, LANES))

    def side(cfn):
        return [tok(cfn, 0), tok(cfn, 1), tok(cfn, 2), gat(cfn), dht(cfn), tok(cfn, 0)] + [st(cfn, s) for s in st_shapes]

    def outs(cfn):
        return [pl.BlockSpec((ln, 3 * md), lambda i: (cfn(i), 0)), gat(cfn)]

    return pl.pallas_call(
        body, name="mlstm_bwd", grid=(nc,),
        in_specs=side(chunk_f) + side(chunk_b) + [pl.BlockSpec((1, LANES), lambda i: (0, 0))],
        out_specs=outs(chunk_f) + outs(chunk_b),
        out_shape=[SDS((s_rows, 3 * md), BF16), SDS((s_rows, LANES), F32)] * 2,
        scratch_shapes=[pltpu.VMEM((2, nh, dh, dh), F32), pltpu.VMEM((2, nh, 8, dh), F32)],
        compiler_params=_cp("arbitrary"))(qk, qk, z_main, zg, dhs, hf, *states_f, qk, qk, z_main, zg, dhs, hb, *states_b, bias)


def _qkv_conv_bwd(dqkv_f, dqkv_b, z_main, conv_w, t_rows, md, qscale):
    s_rows = z_main.shape[0]
    tb = _pick(s_rows, (1280, 1024, 256))
    cb = _pick(md, (512, 256, 128))
    ni, nj, ncq = s_rows // tb, 3 * md // cb, 2 * md // cb
    nb8 = tb // 8
    n_ext = tb + 16

    def body(fm, fp, fn, bm, bp, bn, zm, zp, zn, w_ref, dz_ref, gw_ref):
        j, i = pl.program_id(0), pl.program_id(1)

        @pl.when(j < ncq)
        def _():
            z = jnp.concatenate([zp[...], zm[...], zn[...]], axis=0).astype(F32)
            dqk = (jnp.concatenate([fp[...], fm[...], fn[...]], axis=0).astype(F32)
                   + jnp.concatenate([bp[...], bm[...], bn[...]], axis=0).astype(F32)) * jnp.where(j * cb < md, qscale, 1.0)
            row = i * tb - 8 + lax.broadcasted_iota(jnp.int32, (n_ext, 1), 0)
            prev_ok, next_ok = _seg_masks(row, t_rows, s_rows)
            zprev = jnp.where(prev_ok, pltpu.roll(z, 1, 0), 0.0)
            znext = jnp.where(next_ok, pltpu.roll(z, n_ext - 1, 0), 0.0)
            pre = w_ref[0:1, :] * zprev + w_ref[1:2, :] * z + w_ref[2:3, :] * znext
            sg = _sigmoid(pre)
            dpre = dqk * (sg * (1.0 + pre * (1.0 - sg)))
            dz = (w_ref[1:2, :] * dpre + w_ref[0:1, :] * jnp.where(next_ok, pltpu.roll(dpre, n_ext - 1, 0), 0.0)
                  + w_ref[2:3, :] * jnp.where(prev_ok, pltpu.roll(dpre, 1, 0), 0.0))
            dz_ref[...] = _bf(dz[8:8 + tb])
            dm = dpre[8:8 + tb]

            @pl.when(i == 0)
            def _():
                gw_ref[...] = jnp.zeros_like(gw_ref)

            gw_ref[...] += jnp.concatenate(
                [jnp.sum(dm * zprev[8:8 + tb], axis=0, keepdims=True), jnp.sum(dm * z[8:8 + tb], axis=0, keepdims=True),
                 jnp.sum(dm * znext[8:8 + tb], axis=0, keepdims=True), jnp.zeros((5, cb), F32)], axis=0)

        @pl.when(j >= ncq)
        def _():
            dz_ref[...] = _bf(fm[...].astype(F32) + bm[...].astype(F32))

    def halo(clampj):
        def cj(j):
            return jnp.minimum(j, ncq - 1) if clampj else j
        return [pl.BlockSpec((tb, cb), lambda j, i: (i, cj(j))),
                pl.BlockSpec((8, cb), lambda j, i: (jnp.maximum(i * nb8 - 1, 0), cj(j))),
                pl.BlockSpec((8, cb), lambda j, i: (jnp.minimum((i + 1) * nb8, s_rows // 8 - 1), cj(j)))]

    return pl.pallas_call(
        body, name="qkv_conv_bwd", grid=(nj, ni),
        in_specs=halo(False) + halo(False) + halo(True) + [pl.BlockSpec((8, cb), lambda j, i: (0, jnp.minimum(j, ncq - 1)))],
        out_specs=[pl.BlockSpec((tb, cb), lambda j, i: (i, j)), pl.BlockSpec((8, cb), lambda j, i: (0, jnp.minimum(j, ncq - 1)))],
        out_shape=[SDS((s_rows, 3 * md), BF16), SDS((8, 2 * md), F32)],
        compiler_params=_cp("arbitrary", "arbitrary"))(dqkv_f, dqkv_f, dqkv_f, dqkv_b, dqkv_b, dqkv_b, z_main, z_main, z_main, conv_w)


def _gate_grad_sum(dg_f, dg_b):
    s_rows = dg_f.shape[0]
    tb = _pick(s_rows, (1280, 1024, 256))

    def body(a_ref, b_ref, o_ref, st_ref):
        @pl.when(pl.program_id(0) == 0)
        def _():
            st_ref[...] = jnp.zeros_like(st_ref)

        s = a_ref[...] + b_ref[...]
        o_ref[...] = _bf(s)
        st_ref[...] += jnp.concatenate([jnp.sum(s, axis=0, keepdims=True), jnp.zeros((7, LANES), F32)], axis=0)

    blk = pl.BlockSpec((tb, LANES), lambda i: (i, 0))
    return pl.pallas_call(
        body, name="gate_grad_sum", grid=(s_rows // tb,), in_specs=[blk, blk],
        out_specs=[blk, pl.BlockSpec((8, LANES), lambda i: (0, 0))],
        out_shape=[SDS((s_rows, LANES), BF16), SDS((8, LANES), F32)], compiler_params=_cp("arbitrary"))(dg_f, dg_b)


def _mod_grads(silu_slots, dmx_sh, dmx_slots, dmc_tot, dmc_sh, silu_cctx, c_ctx, w_mod_c):
    d = silu_slots.shape[1]
    ncol, n6 = dmx_sh.shape[1], dmx_slots.shape[1]

    def body(ss_ref, dsh_ref, dsl_ref, dct_ref, dcs_ref, sc_ref, c_ref, w_ref, gw_ref, gb_ref, gc_ref):
        a = jnp.concatenate([ss_ref[...], sc_ref[...], jnp.zeros((7, d), F32)], axis=0)
        b = jnp.concatenate([dsh_ref[...], dcs_ref[...], jnp.zeros((7, ncol), F32)], axis=0)
        gw_ref[0] = lax.dot_general(a, b, (((0,), (0,)), ((), ())), preferred_element_type=F32, precision=HI)
        dct = dct_ref[...]
        gb_ref[...] = jnp.sum(dsl_ref[...], axis=0, keepdims=True) + jnp.concatenate(
            [dct, jnp.zeros((1, n6 - dct.shape[1]), F32)], axis=1)
        t = _dot_nt(_bf(jnp.broadcast_to(dct, (8, dct.shape[1]))), w_ref[...])
        cv = c_ref[...]
        s = _sigmoid(cv)
        gc_ref[...] = t[0:1, :] * (s * (1.0 + cv * (1.0 - s)))

    return pl.pallas_call(body, name="mod_grads", out_shape=[SDS((1, d, ncol), F32), SDS((1, n6), F32), SDS((1, d), F32)],
                          compiler_params=_cp())(silu_slots, dmx_sh, dmx_slots, dmc_tot, dmc_sh, silu_cctx, c_ctx, w_mod_c)


def _slot_sum(slots):
    ns, r = slots.shape[0], slots.shape[1]
    tb = _pick(r, (1024, 512, 256, 128, 64, 32, 16, 8))

    def body(s_ref, o_ref):
        acc = s_ref[0]
        for k in range(1, ns):
            acc = acc + s_ref[k]
        o_ref[...] = acc

    return pl.pallas_call(
        body, name="slot_sum", grid=(r // tb,), in_specs=[pl.BlockSpec((ns, tb, LANES), lambda i: (0, i, 0))],
        out_specs=pl.BlockSpec((tb, LANES), lambda i: (i, 0)), out_shape=SDS((r, LANES), F32),
        compiler_params=_cp("arbitrary"))(slots)


def _adamw(w, gslots, m, v, name):
    lead = ((None,), (0,)) if w.ndim == 3 else ((), ())
    r, cdim = w.shape[-2:]
    ns, rg = gslots.shape[0], gslots.shape[1]
    tb = r if (rg != r or r % 8) else _pick(r, (128, 64, 32, 16, 8))
    bc1, bc2 = 1.0 - ADAM_B1 ** ADAM_STEP, 1.0 - ADAM_B2 ** ADAM_STEP

    def body(w_ref, g_ref, m_ref, v_ref, go_ref, d_ref, mo_ref, vo_ref):
        g = g_ref[0, 0:tb, :].astype(F32)
        for k in range(1, ns):
            g = g + g_ref[k, 0:tb, :].astype(F32)
        mn = ADAM_B1 * m_ref[...] + (1.0 - ADAM_B1) * g
        vn = ADAM_B2 * v_ref[...] + (1.0 - ADAM_B2) * (g * g)
        go_ref[...] = g
        mo_ref[...] = mn
        vo_ref[...] = vn
        d_ref[...] = -ADAM_LR * ((mn / bc1) / (jnp.sqrt(vn / bc2) + ADAM_EPS) + ADAM_WD * w_ref[...])

    blk = pl.BlockSpec(lead[0] + (tb, cdim), lambda i: lead[1] + (i, 0))
    gblk = pl.BlockSpec((ns, tb if rg == r else rg, cdim), lambda i: (0, i, 0))
    return pl.pallas_call(
        body, name=name, grid=(r // tb,), in_specs=[blk, gblk, blk, blk],
        out_specs=[blk] * 4, out_shape=[SDS(w.shape, F32)] * 4, compiler_params=_cp("arbitrary"))(w, gslots, m, v)


def _pack(parts, row_mult):
    flat = jnp.concatenate([p.reshape(-1) for p in parts])
    n = flat.shape[0]
    rows = -(-n // LANES)
    rows = -(-rows // row_mult) * row_mult
    return jnp.pad(flat, (0, rows * LANES - n)).reshape(rows, LANES)


def _unpack(buf, shapes):
    flat = buf.reshape(-1)
    out, off = [], 0
    for s in shapes:
        n = math.prod(s)
        out.append(flat[off:off + n].reshape(s))
        off += n
    return out


def _pad_cols(a, width):
    return jnp.pad(a, ((0, 0), (0, width - a.shape[1])))


def _pad_lanes(a):
    return _pad_cols(a, LANES)


def _up128(n):
    return -(-n // LANES) * LANES


def kernel(x, c, ctx, c_ctx, w_mod, b_mod, norm1_g, w_in, b_gate, conv_qk, head_norm_g, sgu_ln_g, sgu_ln_b, w_s, b_s, w_branch_mlstm, w_branch_sgu, w_out, norm2_g, w_up, w_ffn_conv, w_down, final_g, loss_target, m_c_ctx, m_w_mod, m_b_mod, m_norm1_g, m_w_in, m_b_gate, m_conv_qk, m_head_norm_g, m_sgu_ln_g, m_sgu_ln_b, m_w_s, m_b_s, m_w_branch_mlstm, m_w_branch_sgu, m_w_out, m_norm2_g, m_w_up, m_w_ffn_conv, m_w_down, m_final_g, v_c_ctx, v_w_mod, v_b_mod, v_norm1_g, v_w_in, v_b_gate, v_conv_qk, v_head_norm_g, v_sgu_ln_g, v_sgu_ln_b, v_w_s, v_b_s, v_w_branch_mlstm, v_w_branch_sgu, v_w_out, v_norm2_g, v_w_up, v_w_ffn_conv, v_w_down, v_final_g):
    t, d = x.shape[1], x.shape[2]
    n_ctx = ctx.shape[1]
    s_rows = t + n_ctx
    nh = b_gate.shape[1] // 4
    md = head_norm_g.shape[1]
    dh = md // nh
    ng, sc = w_s.shape[1], w_s.shape[2]
    dff = w_down.shape[1] * N_DEV
    n_in = w_in.shape[2] * N_DEV
    assert md == d and sgu_ln_g.shape[1] == d and n_ctx == LCH and t % LCH == 0 and t % (8 * GRID_W) == 0
    assert n_in == 8 * d + 4 * nh and 4 * nh <= LANES
    me = 4 * lax.axis_index("x") + 2 * lax.axis_index("y") + lax.axis_index("c")

    n_mod, n_insh, n_upsh = w_mod.shape[2], w_in.shape[2], w_up.shape[2]
    p_mod, p_in, p_up = _up128(n_mod), _up128(n_insh), _up128(n_upsh)
    nq, nf = conv_qk.shape[2], w_ffn_conv.shape[3]
    ffn9 = w_ffn_conv[0].reshape(9, nf)
    colpack = jnp.concatenate([_pad_cols(_bf(w_mod[0]), p_mod), _pad_cols(_bf(w_in[0]), p_in)], axis=1)
    convpack = jnp.concatenate([jnp.pad(conv_qk[0], ((0, 13), (0, 0))), jnp.pad(ffn9, ((0, 7), (0, 0)))], axis=1)
    g_col, g_conv = _allgather([colpack, convpack])
    w_mod_f, w_main, w_gate, w_rest = _assemble_cols(
        g_col, [(0, n_mod, [(0, 0, N_DEV * n_mod, 0)]),
                (p_mod, n_insh, [(1, 0, 3 * md, 0), (2, 3 * md, 4 * nh, 0), (1, 3 * md + 4 * nh, 5 * d, 3 * md),
                                 (3, 3 * md + 4 * nh, 5 * d, 0)])],
        [N_MOD * d, 8 * d, LANES, 5 * d], "assemble_weights")
    convw, wconv9 = _assemble_cols(g_conv, [(0, nq, [(0, 0, N_DEV * nq, 0)]), (nq, nf, [(1, 0, N_DEV * nf, 0)])],
                                   [N_DEV * nq, N_DEV * nf], "assemble_conv_weights")
    zero = jnp.minimum(jnp.abs(g_conv[0, 0, 0]), 0.0)
    late_w = [_pad_cols(_bf(w_up[0] + zero), p_up), _bf(w_branch_mlstm[0]), _bf(w_branch_sgu[0]), _bf(w_out[0]), _bf(w_down[0])]
    late_state, late_tok = _exchange_start(late_w, False, "late_weights_start")

    cvec = jnp.concatenate([c, c_ctx[None], jnp.zeros((6, d), F32)], axis=0) + late_tok[0:1, 0:1]
    silu_v, mod = _modulation(cvec, w_mod_f, b_mod)
    mx = [mod[0:1, k * d:(k + 1) * d] for k in range(N_MOD)]
    mc = [mod[1:2, k * d:(k + 1) * d] for k in range(2)]
    x2, ctx2 = x[0], ctx[0]
    in_x = _norm_mod_proj(x2, norm1_g, jnp.concatenate([mx[0], mx[1]], axis=0), w_main, w_gate, s_rows, 0, None, "in_proj")
    hn, z_main, zg = _norm_mod_proj(ctx2, norm1_g, jnp.concatenate([mc[0], mc[1]], axis=0), w_main, w_gate, s_rows, t, in_x,
                                    "in_proj_ctx")
    qscale = dh ** -0.5
    qk = _qk_conv(z_main, convw, t, md, qscale)
    bias = _pad_lanes(b_gate)
    fwd = _mlstm_fwd(qk, z_main, zg, bias, nh)
    hf, hb, states_f, states_b = fwd[0], fwd[1], fwd[2:5], fwd[5:8]
    g_up, g_bm, g_bs, g_out, g_down = _exchange_wait(late_state, fwd[4], "late_weights_wait")
    (w_up_f,) = _assemble_cols(g_up, [(0, n_upsh, [(0, 0, 2 * dff, 0)])], [2 * dff], "assemble_w_up")
    wbm_f, wbs_f, wout_f = (g.reshape(d, d) for g in (g_bm, g_bs, g_out))
    w_down_f = g_down.reshape(dff, d)
    b_st = _pad_lanes(b_s[0].T)
    h1, ym, ys, pm, ps, y, out = _mixer_fwd(hf, hb, z_main, x2, head_norm_g, sgu_ln_g, sgu_ln_b, w_s[0], b_st, wbm_f, wbs_f,
                                            wout_f, mx[2], t, nh)
    hn2, ab = _norm_mod_proj(h1, norm2_g, jnp.concatenate([mx[3], mx[4]], axis=0), w_up_f, None, t, 0, None, "up_proj")
    aconv, f, dh2, dffn, st_tail = _ffn_tail(ab, wconv9, w_down_f, h1, mx[5], final_g[None], loss_target[0], dff)

    db, dac = _ffn_bwd_gate(dffn, w_down_f, aconv, ab, dff)
    da, g_wconv9 = _ffn_conv_bwd(dac, ab, wconv9, dff)
    g_wdown = _wgrad(f, dffn, t, "wgrad_down")
    gwup_slots = _scatter_cols([_wgrad(hn2, da, t, "wgrad_up_a"), _wgrad(hn2, db, t, "wgrad_up_b")],
                               [(0, 0, dff, 0), (1, dff, dff, 0)], n_upsh, "scatter_grad_w_up")
    dh1, st_n2 = _proj_norm_bwd([(da, 0, w_up_f, 0, dff, dff), (db, 0, w_up_f, dff, dff, dff)], h1, 0, norm2_g, mx[4], dh2, t,
                                "up_proj_bwd", (512, 256))
    dz_rest, dhs, dout, dpm, dps, st_mix, g_ws, g_bst = _mixer_bwd(dh1, out, hf, hb, z_main, pm, ps, head_norm_g, sgu_ln_g,
                                                                    sgu_ln_b, w_s[0], b_st, wbm_f, wbs_f, wout_f, mx[2], t, nh)
    g_wout = _wgrad(y, dout, t, "wgrad_out")
    g_wbm = _wgrad(ym, dpm, t, "wgrad_branch_mlstm")
    g_wbs = _wgrad(ys, dps, t, "wgrad_branch_sgu")
    ex_a = [gwup_slots, g_wdown.reshape(N_DEV, dff // N_DEV, d), g_wbm.reshape(N_DEV, d // N_DEV, d),
            g_wbs.reshape(N_DEV, d // N_DEV, d), g_wout.reshape(N_DEV, d // N_DEV, d)]
    ex_a_state, ex_a_tok = _exchange_start(ex_a, True, "grad_exchange_a_start")
    dqkv_f, dg_f, dqkv_b, dg_b = _mlstm_bwd(qk, z_main, zg, bias + ex_a_tok[0:1, :], dhs, hf, hb, states_f, states_b, nh, t)
    dz_qkv, g_convqk = _qkv_conv_bwd(dqkv_f, dqkv_b, z_main, convw, t, md, qscale)
    dz_g, st_gate = _gate_grad_sum(dg_f, dg_b)
    gwin_slots = _scatter_cols(
        [_wgrad(hn, dz_qkv, s_rows, "wgrad_in_qkv"), _wgrad(hn, dz_g, s_rows, "wgrad_in_gate"), _wgrad(hn, dz_rest, t, "wgrad_in_rest")],
        [(0, 0, 3 * md, 0), (1, 3 * md, 4 * nh, 0), (2, 3 * md + 4 * nh, 5 * d, 0)], n_insh, "scatter_grad_w_in")
    gcq_slots = _scatter_cols([g_convqk], [(0, 0, 2 * md, 0)], nq, "scatter_grad_conv_qk")
    gcf_slots = _scatter_cols([g_wconv9], [(0, 0, dff, 0)], nf, "scatter_grad_ffn_conv")
    ex_b_state, ex_b_tok = _exchange_start([gwin_slots, gcq_slots, gcf_slots], True, "grad_exchange_b_start")
    tk = _pick(md, (1024, 512, 256))
    grad_x, st_n1x = _proj_norm_bwd(
        [(dz_qkv, 0, w_main, 0, 3 * md, 3 * md), (dz_rest, 0, w_rest, 0, 5 * d, 5 * d), (dz_g, 0, w_gate, 0, LANES, LANES)],
        x2, 0, norm1_g, mx[1] + ex_b_tok[0:1, 0:1], dh1, t, "in_proj_bwd", (512, 256))
    (st_n1c,) = _proj_norm_bwd([(dz_qkv, t, w_main, 0, 3 * md, tk), (dz_g, t, w_gate, 0, LANES, LANES)],
                               ctx2, 0, norm1_g, mc[1] + ex_b_tok[0:1, 0:1], None, n_ctx, "in_proj_bwd_ctx")

    rx_a = _exchange_wait(ex_a_state, st_n1c, "grad_exchange_a_wait")
    rx_b = _exchange_wait(ex_b_state, st_n1c, "grad_exchange_b_wait")
    recv = [rx_b[0], rx_a[0], rx_a[2], rx_a[3], rx_a[4], rx_a[1], rx_b[1], rx_b[2]]
    small_parts = [st_n1x[1], st_n1x[2], st_mix[0], st_n2[1], st_n2[2], st_tail[1],
                   st_n1c[1], st_n1c[2],
                   silu_v[0], st_n1x[0] + st_n1c[0], st_gate[0], st_mix[1], st_mix[2], st_mix[3],
                   g_ws.reshape(-1), g_bst[:, :ng].T.reshape(-1), st_n2[0], st_tail[0]]
    gsmall = _pack(small_parts, 8)
    (recv_small,) = _grad_exchange([], [gsmall])
    small_sum = _slot_sum(recv_small).reshape(-1)
    small_slots = recv_small.reshape(N_DEV, -1)
    o_silu, o_n1 = 8 * d, 9 * d
    ncol = N_MOD * d // N_DEV
    dmc_tot = small_sum[6 * d:8 * d][None]
    dmc_pad = jnp.concatenate([dmc_tot, jnp.zeros((1, 4 * d), F32)], axis=1)
    g_wmod, g_bmod, g_cctx = _mod_grads(
        small_slots[:, o_silu:o_silu + d], lax.dynamic_slice_in_dim(small_slots[:, :6 * d], me * ncol, ncol, axis=1),
        small_slots[:, :6 * d], dmc_tot, lax.dynamic_slice_in_dim(dmc_pad, me * ncol, ncol, axis=1), silu_v[1:2], c_ctx[None],
        w_mod_f[:, :2 * d])

    shard_w = (w_in, w_up, w_branch_mlstm, w_branch_sgu, w_out, w_down, conv_qk)
    shard_m = (m_w_in, m_w_up, m_w_branch_mlstm, m_w_branch_sgu, m_w_out, m_w_down, m_conv_qk)
    shard_v = (v_w_in, v_w_up, v_w_branch_mlstm, v_w_branch_sgu, v_w_out, v_w_down, v_conv_qk)
    shard_names = ("w_in", "w_up", "w_branch_mlstm", "w_branch_sgu", "w_out", "w_down", "conv_qk")
    shard_out = [_adamw(wa, recv[k], ma, va, "adamw_" + nm)
                 for k, (wa, ma, va, nm) in enumerate(zip(shard_w, shard_m, shard_v, shard_names))]
    shard_out.append([b.reshape(w_ffn_conv.shape) for b in
                      _adamw(ffn9, recv[7], m_w_ffn_conv[0].reshape(9, nf), v_w_ffn_conv[0].reshape(9, nf), "adamw_w_ffn_conv")])
    mod_out = _adamw(w_mod, g_wmod, m_w_mod, v_w_mod, "adamw_w_mod")

    def rep(cc, bm, n1, bg, hg, lg, lb, ws, bs, n2, fg):
        return [cc.reshape(-1), bm.reshape(-1), n1.reshape(-1), _pad_lanes(bg.reshape(1, -1)).reshape(-1), hg.reshape(-1),
                lg.reshape(-1), lb.reshape(-1), ws.reshape(-1), bs.reshape(-1), n2.reshape(-1), fg.reshape(-1)]

    o = o_n1
    g_rep_parts = [g_cctx, g_bmod]
    for n in (d, LANES, d, d, d, ng * sc * sc, ng * sc, d, d):
        g_rep_parts.append(small_sum[o:o + n])
        o += n
    rep_shapes = [(d,), (1, N_MOD * d), (1, d), (1, LANES), (1, d), (1, d), (1, d), (1, ng, sc, sc), (1, ng, sc), (1, d), (d,)]
    rep_out = _adamw(
        _pack(rep(c_ctx, b_mod, norm1_g, b_gate, head_norm_g, sgu_ln_g, sgu_ln_b, w_s, b_s, norm2_g, final_g), 8),
        _pack(g_rep_parts, 8)[None],
        _pack(rep(m_c_ctx, m_b_mod, m_norm1_g, m_b_gate, m_head_norm_g, m_sgu_ln_g, m_sgu_ln_b, m_w_s, m_b_s, m_norm2_g, m_final_g), 8),
        _pack(rep(v_c_ctx, v_b_mod, v_norm1_g, v_b_gate, v_head_norm_g, v_sgu_ln_g, v_sgu_ln_b, v_w_s, v_b_s, v_norm2_g, v_final_g), 8),
        "adamw_replicated")

    def assemble(k):
        r = _unpack(rep_out[k], rep_shapes)
        s = [o[k] for o in shard_out]
        return [r[0], mod_out[k], r[1], r[2], s[0], r[3][:, :4 * nh], s[6], r[4], r[5], r[6], r[7], r[8], s[2], s[3], s[4], r[9],
                s[1], s[7], s[5], r[10]]

    loss = lax.psum(st_tail[2, 0], ("x", "y", "c"))
    outs = [loss, grad_x[None]]
    for k in range(4):
        outs += assemble(k)
    return tuple(outs)
```

```python
import math

import jax
import jax.numpy as jnp
from jax import lax
from jax.experimental import pallas as pl
from jax.experimental.pallas import tpu as pltpu

F32, BF16 = jnp.float32, jnp.bfloat16
EPS = 1e-6
M_INIT = -1e30
NEG = -1e30
GRID_W = 64
LCH = 256
N_MOD = 6
N_DEV = 8
LANES = 128
ADAM_LR, ADAM_B1, ADAM_B2, ADAM_EPS, ADAM_WD, ADAM_STEP = 0.001, 0.9, 0.999, 1e-08, 0.01, 10
GELU_C = math.sqrt(2.0 / math.pi)
GELU_A = 0.044715
VMEM_LIMIT = 56 * 1024 * 1024
HI = lax.Precision.HIGHEST
SDS = jax.ShapeDtypeStruct
MESH_ID = pl.DeviceIdType.MESH


def _pick(n, cands):
    for c in cands:
        if n % c == 0:
            return c
    raise ValueError(f"no block size for {n} in {cands}")


def _cp(*sem):
    return pltpu.CompilerParams(dimension_semantics=sem if sem else None, vmem_limit_bytes=VMEM_LIMIT)


def _sigmoid(x):
    return 0.5 * jnp.tanh(0.5 * x) + 0.5


def _split3(x):
    hi = x.astype(BF16)
    r = x - hi.astype(F32)
    mid = r.astype(BF16)
    return hi, mid, (r - mid.astype(F32)).astype(BF16)


def _mask_dot(mask_b, x):
    hi, mid, lo = _split3(x)
    return (_dot(mask_b, lo) + _dot(mask_b, mid)) + _dot(mask_b, hi)


def _mask_dot_t(mask_b, x):
    hi, mid, lo = _split3(x)
    return (_dot_tn(mask_b, lo) + _dot_tn(mask_b, mid)) + _dot_tn(mask_b, hi)


def _gelu(x):
    return x * (0.5 * (1.0 + jnp.tanh(GELU_C * x * (1.0 + GELU_A * (x * x)))))


def _gelu_and_grad(x):
    x2 = x * x
    t = jnp.tanh(GELU_C * x * (1.0 + GELU_A * x2))
    half = 0.5 * (1.0 + t)
    return x * half, half + (0.5 * GELU_C) * x * (1.0 - t * t) * (1.0 + 3.0 * GELU_A * x2)


def _log_sigmoid(x):
    return jnp.minimum(x, 0.0) - jnp.log(1.0 + jnp.exp(-jnp.abs(x)))


def _dot(a, b):
    return jnp.dot(a, b, preferred_element_type=F32)


def _dot_nt(a, b):
    return lax.dot_general(a, b, (((1,), (1,)), ((), ())), preferred_element_type=F32)


def _dot_tn(a, b):
    return lax.dot_general(a, b, (((0,), (0,)), ((), ())), preferred_element_type=F32)


def _bf(x):
    return x.astype(BF16)


def _allgather(arrs):
    na = len(arrs)

    def body(*refs):
        x_refs, o_refs = refs[:na], refs[na:2 * na]
        send_sems, recv_sems, local_sems = refs[2 * na:]
        x, y, c = lax.axis_index("x"), lax.axis_index("y"), lax.axis_index("c")
        me, sibling = (x, y, c), (x, y, 1 - c)
        chips = [(1 - x, y), (x, 1 - y), (1 - x, 1 - y)]

        def copy(a, k, block, to, src=None):
            slot = o_refs[a].at[4 * block[0] + 2 * block[1] + block[2]]
            return pltpu.make_async_remote_copy(
                src_ref=slot if src is None else src, dst_ref=slot, send_sem=send_sems.at[7 * a + k],
                recv_sem=recv_sems.at[7 * a + k], device_id=to, device_id_type=MESH_ID)

        mine = [pltpu.make_async_copy(x_refs[a], o_refs[a].at[4 * x + 2 * y + c], local_sems.at[a]) for a in range(na)]
        for cp in mine:
            cp.start()
        first = []
        for a in range(na):
            first.append(copy(a, 0, me, sibling, src=x_refs[a]))
            first += [copy(a, 1 + j, me, (*chip, c), src=x_refs[a]) for j, chip in enumerate(chips)]
        for cp in first:
            cp.start()
        passed = []
        for j, chip in enumerate(chips):
            for a in range(na):
                copy(a, 1 + j, (*chip, c), me).wait_recv()
                passed.append(copy(a, 4 + j, (*chip, c), sibling))
                passed[-1].start()
        for a in range(na):
            copy(a, 0, sibling, me).wait_recv()
            for j, chip in enumerate(chips):
                copy(a, 4 + j, (*chip, 1 - c), me).wait_recv()
        for cp in first + passed:
            cp.wait_send()
        for cp in mine:
            cp.wait()

    anyspec = pl.BlockSpec(memory_space=pl.ANY)
    return pl.pallas_call(
        body, name="weights_allgather",
        out_shape=[SDS((N_DEV,) + a.shape, a.dtype) for a in arrs],
        in_specs=[anyspec] * na, out_specs=[anyspec] * na,
        scratch_shapes=[pltpu.SemaphoreType.DMA((7 * na,)), pltpu.SemaphoreType.DMA((7 * na,)), pltpu.SemaphoreType.DMA((na,))],
    )(*arrs)


def _grad_exchange(per_dest, shared):
    nd, ns = len(per_dest), len(shared)
    na = nd + ns

    def body(*refs):
        in_refs, out_refs = refs[:na], refs[na:2 * na]
        send_sems, recv_sems, local_sems = refs[2 * na:]
        x, y, c = lax.axis_index("x"), lax.axis_index("y"), lax.axis_index("c")
        me = 4 * x + 2 * y + c

        def src(a, idx):
            return in_refs[a].at[idx] if a < nd else in_refs[a]

        loc = [pltpu.make_async_copy(src(a, me), out_refs[a].at[me], local_sems.at[a]) for a in range(na)]
        for cp in loc:
            cp.start()
        sends, recvs = [], []
        for k in range(1, N_DEV):
            px = 1 - x if k & 4 else x
            py = 1 - y if k & 2 else y
            pc = 1 - c if k & 1 else c
            peer, pidx = (px, py, pc), 4 * px + 2 * py + pc
            for a in range(na):
                sem = 7 * a + k - 1
                sends.append(pltpu.make_async_remote_copy(
                    src_ref=src(a, pidx), dst_ref=out_refs[a].at[me], send_sem=send_sems.at[sem],
                    recv_sem=recv_sems.at[sem], device_id=peer, device_id_type=MESH_ID))
                recvs.append(pltpu.make_async_remote_copy(
                    src_ref=src(a, pidx), dst_ref=out_refs[a].at[pidx], send_sem=send_sems.at[sem],
                    recv_sem=recv_sems.at[sem], device_id=peer, device_id_type=MESH_ID))
        for cp in sends:
            cp.start()
        for cp in recvs:
            cp.wait_recv()
        for cp in sends:
            cp.wait_send()
        for cp in loc:
            cp.wait()

    anyspec = pl.BlockSpec(memory_space=pl.ANY)
    return pl.pallas_call(
        body, name="grad_exchange",
        out_shape=[SDS(a.shape, a.dtype) for a in per_dest] + [SDS((N_DEV,) + a.shape, a.dtype) for a in shared],
        in_specs=[anyspec] * na, out_specs=[anyspec] * na,
        scratch_shapes=[pltpu.SemaphoreType.DMA((7 * na,)), pltpu.SemaphoreType.DMA((7 * na,)), pltpu.SemaphoreType.DMA((na,))],
    )(*per_dest, *shared)


_HBM_SPEC = pl.BlockSpec(memory_space=pltpu.HBM)
_SEM_SPEC = pl.BlockSpec(memory_space=pltpu.SEMAPHORE)
_EFFECT = pltpu.SideEffectType.DATAFLOW_SIDE_EFFECTING


def _peer_list(x, y, c):
    out = []
    for k in range(1, N_DEV):
        px = 1 - x if k & 4 else x
        py = 1 - y if k & 2 else y
        pc = 1 - c if k & 1 else c
        out.append(((px, py, pc), 4 * px + 2 * py + pc))
    return out


def _split_copies(src, land, send_sems, recv_sems, per_dest, receive):
    x, y, c = lax.axis_index("x"), lax.axis_index("y"), lax.axis_index("c")
    me = 4 * x + 2 * y + c
    out = []
    for k, (peer, pidx) in enumerate(_peer_list(x, y, c)):
        for a in range(len(src)):
            out.append(pltpu.make_async_remote_copy(
                src_ref=src[a].at[pidx] if per_dest else src[a], dst_ref=land[a].at[pidx if receive else me],
                send_sem=send_sems.at[7 * a + k], recv_sem=recv_sems.at[7 * a + k], device_id=peer, device_id_type=MESH_ID))
    return out


def _own_copies(src, land, own_sems, per_dest):
    me = 4 * lax.axis_index("x") + 2 * lax.axis_index("y") + lax.axis_index("c")
    return [pltpu.make_async_copy(src[a].at[me] if per_dest else src[a], land[a].at[me], own_sems.at[a]) for a in range(len(src))]


def _exchange_start(arrs, per_dest, name):
    na = len(arrs)
    land_shapes = [a.shape if per_dest else (N_DEV,) + a.shape for a in arrs]
    lands = [pltpu.with_memory_space_constraint(lax.empty(s, a.dtype), pltpu.HBM) for s, a in zip(land_shapes, arrs)]

    def body(*refs):
        src, land = refs[:na], refs[na:2 * na]
        send_sems, recv_sems, own_sems, token = refs[2 * na], refs[2 * na + 1], refs[2 * na + 2], refs[-1]
        for cp in _split_copies(src, land, send_sems, recv_sems, per_dest, False) + _own_copies(src, land, own_sems, per_dest):
            cp.start()
        token[...] = jnp.zeros_like(token)

    outs = pl.pallas_call(
        body, name=name,
        out_shape=[pltpu.SemaphoreType.DMA((7 * na,)), pltpu.SemaphoreType.DMA((7 * na,)), pltpu.SemaphoreType.DMA((na,))]
        + [pltpu.HBM(a.shape, a.dtype) for a in arrs] + [pltpu.HBM(s, a.dtype) for s, a in zip(land_shapes, arrs)]
        + [SDS((8, LANES), F32)],
        in_specs=[_HBM_SPEC] * (2 * na),
        out_specs=[_SEM_SPEC] * 3 + [_HBM_SPEC] * (2 * na) + [pl.BlockSpec(memory_space=pltpu.VMEM)],
        input_output_aliases={k: 3 + k for k in range(2 * na)},
        compiler_params=pltpu.CompilerParams(has_side_effects=_EFFECT),
    )(*[pltpu.with_memory_space_constraint(a, pltpu.HBM) for a in arrs], *lands)
    return (na, per_dest, outs[:-1]), outs[-1]


def _exchange_wait(state, after, name):
    na, per_dest, started = state

    def body(*refs):
        src, land = refs[:na], refs[na:2 * na]
        send_sems, recv_sems, own_sems = refs[2 * na], refs[2 * na + 1], refs[2 * na + 2]
        for cp in _split_copies(src, land, send_sems, recv_sems, per_dest, True):
            cp.wait_send()
            cp.wait_recv()
        for cp in _own_copies(src, land, own_sems, per_dest):
            cp.wait()

    bufs = started[3:]
    outs = pl.pallas_call(
        body, name=name,
        out_shape=[pltpu.HBM(b.shape, b.dtype) for b in bufs],
        in_specs=[_HBM_SPEC] * (2 * na) + [_SEM_SPEC] * 3 + [pl.BlockSpec(memory_space=pl.ANY)],
        out_specs=[_HBM_SPEC] * (2 * na),
        input_output_aliases={k: k for k in range(2 * na)},
        compiler_params=pltpu.CompilerParams(has_side_effects=_EFFECT),
    )(*bufs, started[0], started[1], started[2], after)
    return outs[na:]


def _col_pieces(n, segments):
    out = []
    for j in range(N_DEV):
        lo, hi = j * n, (j + 1) * n
        for (k, s0, w, c0) in segments:
            a, b = max(lo, s0), min(hi, s0 + w)
            if a < b:
                out.append((j, a - lo, b - lo, k, c0 + a - s0, c0 + b - s0))
    return out


def _assemble_cols(slots, groups, out_widths, name):
    r, p = slots.shape[1], slots.shape[2]
    tb = _pick(r, (128, 64, 32, 16, 8))
    covered = [0] * len(out_widths)
    for (_, n, segs) in groups:
        for (k, _, w, _) in segs:
            covered[k] += w

    def body(s_ref, *o_refs):
        for k, wd in enumerate(out_widths):
            if covered[k] < wd:
                o_refs[k][...] = jnp.zeros_like(o_refs[k])
        for (off, n, segs) in groups:
            for (j, a0, a1, k, d0, d1) in _col_pieces(n, segs):
                o_refs[k][:, d0:d1] = s_ref[j, :, off + a0:off + a1]

    return pl.pallas_call(
        body, name=name, grid=(r // tb,), in_specs=[pl.BlockSpec((N_DEV, tb, p), lambda i: (0, i, 0))],
        out_specs=[pl.BlockSpec((tb, w), lambda i: (i, 0)) for w in out_widths],
        out_shape=[SDS((r, w), slots.dtype) for w in out_widths], compiler_params=_cp("arbitrary"))(slots)


def _scatter_cols(pieces, segments, n, name):
    r = pieces[0].shape[0]
    tb = _pick(r, (128, 64, 32, 16, 8))

    def body(*refs):
        p_refs, o_ref = refs[:-1], refs[-1]
        for (j, a0, a1, k, d0, d1) in _col_pieces(n, segments):
            o_ref[j, :, a0:a1] = p_refs[k][:, d0:d1]

    return pl.pallas_call(
        body, name=name, grid=(r // tb,), in_specs=[pl.BlockSpec((tb, a.shape[1]), lambda i: (i, 0)) for a in pieces],
        out_specs=pl.BlockSpec((N_DEV, tb, n), lambda i: (0, i, 0)), out_shape=SDS((N_DEV, r, n), pieces[0].dtype),
        compiler_params=_cp("arbitrary"))(*pieces)


def _modulation(cvec, w_mod, b_mod):
    d, n = w_mod.shape

    def body(c_ref, w_ref, b_ref, s_ref, o_ref):
        cv = c_ref[...]
        s = cv * _sigmoid(cv)
        s_ref[...] = s
        o_ref[...] = _dot(_bf(s), w_ref[...]) + b_ref[...]

    return pl.pallas_call(body, name="modulation", out_shape=(SDS((8, d), F32), SDS((8, n), F32)),
                          compiler_params=_cp())(cvec, w_mod, b_mod)


def _norm_mod_proj(x_arr, g, shsc, w_main, w_gate, rows_total, row0, filled, name):
    m_rows, d = x_arr.shape
    n = w_main.shape[1]
    tb = _pick(m_rows, (1024, 256))
    cb = _pick(n, (2048, 1408, 1024, 768, 512, 384, 256, 128))
    gate = w_gate is not None
    nout = 3 if gate else 2
    nin = 5 if gate else 4
    rb = row0 // tb

    def body(*refs):
        x_ref, g_ref, ss_ref, wm_ref = refs[:4]
        wg_ref = refs[4] if gate else None
        outs = refs[len(refs) - 1 - nout:len(refs) - 1]
        hn_ref, z_ref = outs[0], outs[1]
        hn_sc = refs[-1]

        @pl.when(pl.program_id(1) == 0)
        def _():
            x = x_ref[...]
            r = lax.rsqrt(jnp.mean(x * x, axis=-1, keepdims=True) + EPS)
            hb = _bf((x * r * g_ref[...]) * (1.0 + ss_ref[1:2, :]) + ss_ref[0:1, :])
            hn_sc[...] = hb
            hn_ref[...] = hb
            if gate:
                outs[2][...] = _dot(hb, wg_ref[...])

        z_ref[...] = _bf(_dot(hn_sc[...], wm_ref[:, pl.ds(pl.multiple_of(pl.program_id(1) * cb, cb), cb)]))

    in_specs = [pl.BlockSpec((tb, d), lambda i, j: (i, 0)), pl.BlockSpec((1, d), lambda i, j: (0, 0)),
                pl.BlockSpec((2, d), lambda i, j: (0, 0)), _resident((d, n))]
    out_specs = [pl.BlockSpec((tb, d), lambda i, j: (rb + i, 0)), pl.BlockSpec((tb, cb), lambda i, j: (rb + i, j))]
    out_shape = [SDS((rows_total, d), BF16), SDS((rows_total, n), BF16)]
    args = [x_arr, g, shsc, w_main]
    if gate:
        in_specs.append(pl.BlockSpec((d, LANES), lambda i, j: (0, 0)))
        out_specs.append(pl.BlockSpec((tb, LANES), lambda i, j: (rb + i, 0)))
        out_shape.append(SDS((rows_total, LANES), F32))
        args.append(w_gate)
    aliases = {}
    if filled is not None:
        in_specs += [pl.BlockSpec(memory_space=pl.ANY)] * nout
        args += list(filled)
        aliases = {nin + k: k for k in range(nout)}
    return pl.pallas_call(
        body, name=name, grid=(m_rows // tb, n // cb), in_specs=in_specs, out_specs=out_specs, out_shape=out_shape,
        input_output_aliases=aliases, scratch_shapes=[pltpu.VMEM((tb, d), BF16)],
        compiler_params=_cp("arbitrary", "arbitrary"))(*args)


def _seg_masks(row, t_rows, s_rows):
    prev_ok = (row != 0) & (row != t_rows)
    next_ok = (row != t_rows - 1) & (row != s_rows - 1)
    return prev_ok, next_ok


def _shift_rows(z, halo_prev, halo_next, tb):
    loc = lax.broadcasted_iota(jnp.int32, (tb, 1), 0)
    zp = jnp.where(loc == 0, halo_prev, pltpu.roll(z, 1, 0))
    zn = jnp.where(loc == tb - 1, halo_next, pltpu.roll(z, tb - 1, 0))
    return zp, zn


def _qk_conv(z_main, conv_w, t_rows, md, qscale):
    s_rows = z_main.shape[0]
    tb = _pick(s_rows, (1280, 1024, 256))
    cb = _pick(md, (512, 256, 128))
    nb8 = tb // 8

    def body(zm, zp, zn, w_ref, o_ref):
        i, j = pl.program_id(0), pl.program_id(1)
        z = zm[...].astype(F32)
        zprev, znext = _shift_rows(z, zp[7:8, :].astype(F32), zn[0:1, :].astype(F32), tb)
        row = i * tb + lax.broadcasted_iota(jnp.int32, (tb, 1), 0)
        prev_ok, next_ok = _seg_masks(row, t_rows, s_rows)
        pre = (w_ref[0:1, :] * jnp.where(prev_ok, zprev, 0.0) + w_ref[1:2, :] * z
               + w_ref[2:3, :] * jnp.where(next_ok, znext, 0.0))
        scale = jnp.where(j * cb < md, qscale, 1.0)
        o_ref[...] = _bf(pre * _sigmoid(pre) * scale)

    return pl.pallas_call(
        body, name="qk_conv", grid=(s_rows // tb, 2 * md // cb),
        in_specs=[pl.BlockSpec((tb, cb), lambda i, j: (i, j)),
                  pl.BlockSpec((8, cb), lambda i, j: (jnp.maximum(i * nb8 - 1, 0), j)),
                  pl.BlockSpec((8, cb), lambda i, j: (jnp.minimum((i + 1) * nb8, s_rows // 8 - 1), j)),
                  pl.BlockSpec((8, cb), lambda i, j: (0, j))],
        out_specs=pl.BlockSpec((tb, cb), lambda i, j: (i, j)),
        out_shape=SDS((s_rows, 2 * md), BF16), compiler_params=_cp("arbitrary", "arbitrary"))(z_main, z_main, z_main, conv_w)


def _chunk_gates(gates, bias, rev):
    ln = gates.shape[0]
    gz = gates + bias
    logf = _log_sigmoid(gz)
    r_id = lax.broadcasted_iota(jnp.int32, (ln, ln), 0)
    c_id = lax.broadcasted_iota(jnp.int32, (ln, ln), 1)
    mask = (c_id >= r_id) if rev else (c_id <= r_id)
    mb = mask.astype(F32).astype(BF16)
    b_all = _mask_dot(mb, logf)
    g_all = jnp.sum(logf, axis=0, keepdims=True)
    return gz, b_all, b_all.T, gz.T, g_all, mask, mb


def _head_weights(b_col, b_row, i_row, m_in, mask):
    d = jnp.where(mask, b_col - b_row + i_row, NEG)
    inter = b_col + m_in
    m_row = jnp.maximum(inter, jnp.max(d, axis=1, keepdims=True))
    return jnp.exp(d - m_row), jnp.exp(inter - m_row), m_row


def _head_state_coeffs(g, b_col, i_col, m_in):
    a = g - b_col + i_col
    m_new = jnp.maximum(g + m_in, jnp.max(a, axis=0, keepdims=True))
    return jnp.exp(g + m_in - m_new), jnp.exp(a - m_new), m_new


def _mlstm_fwd(qk, z_main, zg, bias, nh):
    s_rows = qk.shape[0]
    md = qk.shape[1] // 2
    dh = md // nh
    nc = s_rows // LCH
    ln = LCH

    def chunk_f(i):
        return jnp.where(i == 0, nc - 1, i - 1)

    def chunk_b(i):
        return jnp.where(i == 0, nc - 1, nc - 1 - i)

    def body(qf, kf, vf, gf, qb, kb, vb, gb, bias_ref, hf_ref, hb_ref, cf_ref, nf_ref, mf_ref, cb_ref, nb_ref, mb_ref,
             c_sc, n_sc, m_sc):
        i = pl.program_id(0)

        @pl.when(i == 0)
        def _():
            c_sc[...] = jnp.zeros_like(c_sc)
            n_sc[...] = jnp.zeros_like(n_sc)
            m_sc[...] = jnp.full(m_sc.shape, M_INIT, F32)

        sides = ((qf, kf, vf, gf, hf_ref, cf_ref, nf_ref, mf_ref), (qb, kb, vb, gb, hb_ref, cb_ref, nb_ref, mb_ref))
        gates = [_chunk_gates(s[3][...], bias_ref[...], dr == 1) for dr, s in enumerate(sides)]
        units = []
        for dr, (q_ref, k_ref, v_ref, _, h_ref, c_out, n_out, m_out) in enumerate(sides):
            gz, b_all, b_t, g_t, g_all, mask, _ = gates[dr]
            for h in range(nh):
                ci, cf = 2 * dr * nh + h, (2 * dr + 1) * nh + h
                sl = slice(h * dh, (h + 1) * dh)
                u = dict(dr=dr, h=h, sl=sl, h_ref=h_ref, q=q_ref[:, sl], k=k_ref[:, sl], v=v_ref[:, sl],
                         c_in=c_sc[dr, h], n_in=n_sc[dr, h, 0:1, :], m_in=m_sc[dr, h, 0:1, 0:1],
                         b_col=b_all[:, cf:cf + 1], i_col=gz[:, ci:ci + 1], g=g_all[:, cf:cf + 1])
                c_out[sl, :] = u["c_in"]
                n_out[:, sl] = n_sc[dr, h]
                m_out[h] = m_sc[dr, h]
                u["w"], u["w_int"], u["m_row"] = _head_weights(u["b_col"], b_t[cf:cf + 1, :], g_t[ci:ci + 1, :], u["m_in"], mask)
                u["qk"] = _dot_nt(u["q"], u["k"])
                units.append(u)
        for u in units:
            u["s_mat"] = u["qk"] * u["w"]
            u["qc"] = _dot(u["q"], _bf(u["c_in"]))
            u["a_old"], u["coef"], u["m_new"] = _head_state_coeffs(u["g"], u["b_col"], u["i_col"], u["m_in"])
            u["kw"] = u["k"].astype(F32) * u["coef"]
        for u in units:
            u["sv"] = _dot(_bf(u["s_mat"]), u["v"])
            u["kv"] = _dot_tn(_bf(u["kw"]), u["v"])
        for u in units:
            dr, h = u["dr"], u["h"]
            num = u["sv"] + u["w_int"] * u["qc"]
            den = (jnp.sum(u["s_mat"], axis=1, keepdims=True)
                   + u["w_int"] * jnp.sum(u["q"].astype(F32) * u["n_in"], axis=1, keepdims=True))
            u["h_ref"][:, u["sl"]] = _bf(num / jnp.maximum(jnp.abs(den), jnp.exp(-u["m_row"])))
            c_sc[dr, h] = u["a_old"] * u["c_in"] + u["kv"]
            n_sc[dr, h] = jnp.broadcast_to(u["a_old"] * u["n_in"] + jnp.sum(u["kw"], axis=0, keepdims=True), (8, dh))
            m_sc[dr, h] = jnp.broadcast_to(u["m_new"], (8, LANES))

    def tok(cfn, col):
        return pl.BlockSpec((ln, md), lambda i: (cfn(i), col))

    def gat(cfn):
        return pl.BlockSpec((ln, LANES), lambda i: (cfn(i), 0))

    def st(cfn, shape):
        return pl.BlockSpec((None,) + shape, lambda i: (cfn(i),) + (0,) * len(shape))

    st_shapes = ((nh * dh, dh), (8, md), (nh, 8, LANES))
    return pl.pallas_call(
        body, name="mlstm_fwd", grid=(nc,),
        in_specs=[tok(chunk_f, 0), tok(chunk_f, 1), tok(chunk_f, 2), gat(chunk_f),
                  tok(chunk_b, 0), tok(chunk_b, 1), tok(chunk_b, 2), gat(chunk_b),
                  pl.BlockSpec((1, LANES), lambda i: (0, 0))],
        out_specs=[tok(chunk_f, 0), tok(chunk_b, 0)] + [st(chunk_f, s) for s in st_shapes] + [st(chunk_b, s) for s in st_shapes],
        out_shape=[SDS((s_rows, md), BF16)] * 2 + [SDS((nc,) + s, F32) for s in st_shapes] * 2,
        scratch_shapes=[pltpu.VMEM((2, nh, dh, dh), F32), pltpu.VMEM((2, nh, 8, dh), F32), pltpu.VMEM((2, nh, 8, LANES), F32)],
        compiler_params=_cp("arbitrary"))(qk, qk, z_main, zg, qk, qk, z_main, zg, bias)


def _head_rms(hs, nh, dh):
    parts, scales = [], []
    for h in range(nh):
        hh = hs[:, h * dh:(h + 1) * dh]
        r = lax.rsqrt(jnp.mean(hh * hh, axis=-1, keepdims=True) + EPS)
        parts.append(hh * r)
        scales.append(r)
    return jnp.concatenate(parts, axis=1), scales


def _layer_norm(v):
    vc = v - jnp.mean(v, axis=-1, keepdims=True)
    r = lax.rsqrt(jnp.mean(vc * vc, axis=-1, keepdims=True) + EPS)
    return vc * r, r


def _sgu_mix(vnb, ws_ref, bs_ref, tb, ng, gd, sc):
    rows = []
    for ch in range(tb // sc):
        cols = []
        for g in range(ng):
            blk = vnb[ch * sc:(ch + 1) * sc, g * gd:(g + 1) * gd]
            cols.append(_dot(_bf(ws_ref[g]), blk) + bs_ref[:, g:g + 1])
        rows.append(jnp.concatenate(cols, axis=1))
    return jnp.concatenate(rows, axis=0)


def _mixer_fwd(hf, hb, z_main, xs, hg, lng, lnb, w_s, b_st, wbm, wbs, wout, mx2, t_rows, nh):
    d = xs.shape[1]
    ng, sc = w_s.shape[0], w_s.shape[1]
    dh, gd = d // nh, d // ng
    tb = _pick(t_rows, (256,))

    def body(hf_ref, hb_ref, zo, zu, zv, zgm, zgg, x_ref, hg_ref, lng_ref, lnb_ref, ws_ref, bs_ref, wbm_ref, wbs_ref,
             wo_ref, mx2_ref, h1_ref, ym_ref, ys_ref, pm_ref, ps_ref, y_ref, out_ref):
        hs = hf_ref[...].astype(F32) + hb_ref[...].astype(F32)
        hn, _ = _head_rms(hs, nh, dh)
        ym = _bf(_sigmoid(zo[...].astype(F32)) * (hn * hg_ref[...]))
        ym_ref[...] = ym
        vhat, _ = _layer_norm(_gelu(zv[...].astype(F32)))
        vnb = _bf(vhat * lng_ref[...] + lnb_ref[...])
        ys = _bf(_gelu(zu[...].astype(F32)) * _sgu_mix(vnb, ws_ref, bs_ref, tb, ng, gd, sc))
        ys_ref[...] = ys
        pm = _dot(ym, wbm_ref[...])
        ps = _dot(ys, wbs_ref[...])
        pm_ref[...] = _bf(pm)
        ps_ref[...] = _bf(ps)
        y = _bf(_sigmoid(zgm[...].astype(F32)) * pm + _sigmoid(zgg[...].astype(F32)) * ps)
        y_ref[...] = y
        out = _dot(y, wo_ref[...])
        out_ref[...] = _bf(out)
        h1_ref[...] = x_ref[...] + mx2_ref[...] * out

    def tok(col):
        return pl.BlockSpec((tb, d), lambda i: (i, col))

    def full(shape):
        return pl.BlockSpec(shape, lambda i: (0,) * len(shape))

    return pl.pallas_call(
        body, name="mixer_fwd", grid=(t_rows // tb,),
        in_specs=[tok(0), tok(0), tok(3), tok(4), tok(5), tok(6), tok(7), tok(0), full((1, d)), full((1, d)), full((1, d)),
                  full((ng, sc, sc)), full((sc, LANES)), full((d, d)), full((d, d)), full((d, d)), full((1, d))],
        out_specs=[tok(0)] * 7,
        out_shape=[SDS((t_rows, d), F32)] + [SDS((t_rows, d), BF16)] * 6,
        compiler_params=_cp("arbitrary"))(hf, hb, z_main, z_main, z_main, z_main, z_main, xs, hg, lng, lnb, w_s, b_st,
                                          wbm, wbs, wout, mx2)


def _resident(shape):
    return pl.BlockSpec(shape, lambda *_: (0,) * len(shape), pipeline_mode=pl.Buffered(1))


def _grid_taps(a_ext, n_ext):
    col = lax.broadcasted_iota(jnp.int32, (n_ext, 1), 0) % GRID_W
    left = jnp.where(col != 0, pltpu.roll(a_ext, 1, 0), 0.0)
    right = jnp.where(col != GRID_W - 1, pltpu.roll(a_ext, n_ext - 1, 0), 0.0)
    return left, right


def _with_halo(prev, main, nxt, i, ni, tb):
    ext = jnp.concatenate([prev, main, nxt], axis=0).astype(F32)
    pos = lax.broadcasted_iota(jnp.int32, (tb + 2 * GRID_W, 1), 0)
    inside = ((pos >= GRID_W) | (i > 0)) & ((pos < tb + GRID_W) | (i < ni - 1))
    return jnp.where(inside, ext, 0.0)


def _halo_specs(tb, cb, t_rows, col0=0):
    nh64 = tb // GRID_W
    return [pl.BlockSpec((tb, cb), lambda i, j: (i, col0 + j)),
            pl.BlockSpec((GRID_W, cb), lambda i, j: (jnp.maximum(i * nh64 - 1, 0), col0 + j)),
            pl.BlockSpec((GRID_W, cb), lambda i, j: (jnp.minimum((i + 1) * nh64, t_rows // GRID_W - 1), col0 + j))]


def _ffn_tail(ab, w_conv9, w_down, h1, mx5, gfin, target, dff):
    t_rows, d = h1.shape
    tb = _pick(t_rows, (256,))
    cb = _pick(dff, (1408, 256, 128))
    ni, nj = t_rows // tb, dff // cb
    n_ext = tb + 2 * GRID_W

    def body(am, ap, an, b_ref, wc_ref, wd_ref, h1_ref, mx5_ref, gf_ref, tg_ref, ac_ref, f_ref, dh2_ref, dffn_ref, st_ref, acc):
        i, j = pl.program_id(0), pl.program_id(1)
        a_ext = _with_halo(ap[...], am[...], an[...], i, ni, tb)
        left, right = _grid_taps(a_ext, n_ext)
        conv = jnp.zeros((tb, cb), F32)
        for di in range(3):
            o = di * GRID_W
            conv = conv + (wc_ref[3 * di:3 * di + 1, :] * left[o:o + tb] + wc_ref[3 * di + 1:3 * di + 2, :] * a_ext[o:o + tb]
                           + wc_ref[3 * di + 2:3 * di + 3, :] * right[o:o + tb])
        ac_ref[...] = _bf(conv)
        fb = _bf(conv * _sigmoid(conv) * b_ref[...].astype(F32))
        f_ref[...] = fb

        @pl.when(j == 0)
        def _():
            acc[...] = jnp.zeros_like(acc)

        @pl.when((i == 0) & (j == 0))
        def _():
            st_ref[...] = jnp.zeros_like(st_ref)

        acc[...] += _dot(fb, wd_ref[pl.ds(pl.multiple_of(j * cb, cb), cb), :])

        @pl.when(j == nj - 1)
        def _():
            ffn = acc[...]
            h2 = h1_ref[...] + mx5_ref[...] * ffn
            r = lax.rsqrt(jnp.mean(h2 * h2, axis=-1, keepdims=True) + EPS)
            xn = h2 * r
            e = xn * gf_ref[...] - tg_ref[...]
            loss = 0.5 * jnp.sum(jnp.sum(e * e, axis=1, keepdims=True), axis=0, keepdims=True) / d
            dy = e * (1.0 / d)
            dxn = dy * gf_ref[...]
            dh2 = r * (dxn - xn * jnp.mean(dxn * xn, axis=-1, keepdims=True))
            dh2_ref[...] = dh2
            dffn_ref[...] = _bf(dh2 * mx5_ref[...])
            st_ref[...] += jnp.concatenate(
                [jnp.sum(dy * xn, axis=0, keepdims=True), jnp.sum(dh2 * ffn, axis=0, keepdims=True),
                 jnp.broadcast_to(loss, (1, d)), jnp.zeros((5, d), F32)], axis=0)

    def tokd():
        return pl.BlockSpec((tb, d), lambda i, j: (i, 0))

    def rowd():
        return pl.BlockSpec((1, d), lambda i, j: (0, 0))

    return pl.pallas_call(
        body, name="ffn_tail", grid=(ni, nj),
        in_specs=_halo_specs(tb, cb, t_rows) + [pl.BlockSpec((tb, cb), lambda i, j: (i, nj + j)),
                                                pl.BlockSpec((16, cb), lambda i, j: (0, j)),
                                                _resident((dff, d)), tokd(), rowd(), rowd(), tokd()],
        out_specs=[pl.BlockSpec((tb, cb), lambda i, j: (i, j)), pl.BlockSpec((tb, cb), lambda i, j: (i, j)), tokd(), tokd(),
                   pl.BlockSpec((8, d), lambda i, j: (0, 0))],
        out_shape=[SDS((t_rows, dff), BF16), SDS((t_rows, dff), BF16), SDS((t_rows, d), F32), SDS((t_rows, d), BF16),
                   SDS((8, d), F32)],
        scratch_shapes=[pltpu.VMEM((tb, d), F32)],
        compiler_params=_cp("arbitrary", "arbitrary"))(ab, ab, ab, ab, w_conv9, w_down, h1, mx5, gfin, target)


def _ffn_bwd_gate(dffn, w_down, aconv, ab, dff):
    t_rows, d = dffn.shape
    tb = _pick(t_rows, (512,))
    cb = _pick(dff, (1408, 256, 128))
    nj = dff // cb

    def body(g_ref, wd_ref, ac_ref, b_ref, db_ref, dac_ref):
        df = _dot_nt(g_ref[...], wd_ref[pl.ds(pl.multiple_of(pl.program_id(1) * cb, cb), cb), :])
        ac = ac_ref[...].astype(F32)
        sa = _sigmoid(ac)
        db_ref[...] = _bf(df * ac * sa)
        dac_ref[...] = _bf(df * b_ref[...].astype(F32) * (sa * (1.0 + ac * (1.0 - sa))))

    blk = pl.BlockSpec((tb, cb), lambda i, j: (i, j))
    return pl.pallas_call(
        body, name="ffn_bwd_gate", grid=(t_rows // tb, nj),
        in_specs=[pl.BlockSpec((tb, d), lambda i, j: (i, 0)), _resident((dff, d)), blk,
                  pl.BlockSpec((tb, cb), lambda i, j: (i, nj + j))],
        out_specs=[blk, blk], out_shape=[SDS((t_rows, dff), BF16)] * 2,
        compiler_params=_cp("arbitrary", "arbitrary"))(dffn, w_down, aconv, ab)


def _ffn_conv_bwd(dac, ab, w_conv9, dff):
    t_rows = dac.shape[0]
    tb = _pick(t_rows, (512, 256))
    cb = _pick(dff, (1408, 256, 128))
    ni, nj = t_rows // tb, dff // cb
    n_ext = tb + 2 * GRID_W
    nh64 = tb // GRID_W

    def body(dm, dp, dn, am, ap, an, wc_ref, da_ref, gw_ref):
        i = pl.program_id(1)
        d_ext = _with_halo(dp[...], dm[...], dn[...], i, ni, tb)
        a_ext = _with_halo(ap[...], am[...], an[...], i, ni, tb)
        d_left, d_right = _grid_taps(d_ext, n_ext)
        a_left, a_right = _grid_taps(a_ext, n_ext)
        dmain = d_ext[GRID_W:GRID_W + tb]
        da = jnp.zeros((tb, cb), F32)
        rows = []
        for di in range(3):
            o = (2 - di) * GRID_W
            da = da + (wc_ref[3 * di:3 * di + 1, :] * d_right[o:o + tb] + wc_ref[3 * di + 1:3 * di + 2, :] * d_ext[o:o + tb]
                       + wc_ref[3 * di + 2:3 * di + 3, :] * d_left[o:o + tb])
            o = di * GRID_W
            for tap in (a_left, a_ext, a_right):
                rows.append(jnp.sum(dmain * tap[o:o + tb], axis=0, keepdims=True))
        da_ref[...] = _bf(da)

        @pl.when(i == 0)
        def _():
            gw_ref[...] = jnp.zeros_like(gw_ref)

        gw_ref[...] += jnp.concatenate(rows + [jnp.zeros((7, cb), F32)], axis=0)

    def halo(col0):
        return [pl.BlockSpec((tb, cb), lambda j, i: (i, col0 + j)),
                pl.BlockSpec((GRID_W, cb), lambda j, i: (jnp.maximum(i * nh64 - 1, 0), col0 + j)),
                pl.BlockSpec((GRID_W, cb), lambda j, i: (jnp.minimum((i + 1) * nh64, t_rows // GRID_W - 1), col0 + j))]

    return pl.pallas_call(
        body, name="ffn_conv_bwd", grid=(nj, ni),
        in_specs=halo(0) + halo(0) + [pl.BlockSpec((16, cb), lambda j, i: (0, j))],
        out_specs=[pl.BlockSpec((tb, cb), lambda j, i: (i, j)), pl.BlockSpec((16, cb), lambda j, i: (0, j))],
        out_shape=[SDS((t_rows, dff), BF16), SDS((16, dff), F32)],
        compiler_params=_cp("arbitrary", "arbitrary"))(dac, dac, dac, ab, ab, ab, w_conv9)


def _proj_norm_bwd(pairs, x_arr, x_row0, g, scale, resid, m_rows, name, row_blocks=(1024, 256)):
    d = x_arr.shape[1]
    tm = _pick(m_rows, row_blocks)
    te = 256
    ni = m_rows // tm
    starts, total = [], 0
    for (_, _, _, _, k_p, tk_p) in pairs:
        starts.append(total)
        total += k_p // tk_p
    npairs = len(pairs)
    has_dx = resid is not None

    def body(*refs):
        a_refs, b_refs = refs[0:2 * npairs:2], refs[1:2 * npairs:2]
        rest = refs[2 * npairs:]
        if has_dx:
            x_ref, g_ref, sc_ref, r_ref, dx_ref, st_ref, acc = rest
        else:
            x_ref, g_ref, sc_ref, st_ref, acc = rest
        i, k = pl.program_id(0), pl.program_id(1)

        @pl.when(k == 0)
        def _():
            acc[...] = jnp.zeros_like(acc)

        @pl.when((i == 0) & (k == 0))
        def _():
            st_ref[...] = jnp.zeros_like(st_ref)

        for p in range(npairs):
            nk = pairs[p][4] // pairs[p][5]

            @pl.when((k >= starts[p]) & (k < starts[p] + nk))
            def _(p=p):
                acc[...] += _dot_nt(a_refs[p][...], b_refs[p][...])

        @pl.when(k == total - 1)
        def _():
            sums = [jnp.zeros((1, d), F32)] * 3
            for r0 in range(0, tm, te):
                rows = slice(r0, r0 + te)
                dhn = acc[rows, :]
                x = x_ref[rows, :]
                r = lax.rsqrt(jnp.mean(x * x, axis=-1, keepdims=True) + EPS)
                xn = x * r
                dmod = dhn * (1.0 + sc_ref[...])
                dxn = dmod * g_ref[...]
                if has_dx:
                    dx_ref[rows, :] = r * (dxn - xn * jnp.mean(dxn * xn, axis=-1, keepdims=True)) + r_ref[rows, :]
                sums = [sums[0] + jnp.sum(dmod * xn, axis=0, keepdims=True), sums[1] + jnp.sum(dhn, axis=0, keepdims=True),
                        sums[2] + jnp.sum(dhn * (xn * g_ref[...]), axis=0, keepdims=True)]
            st_ref[...] += jnp.concatenate(sums + [jnp.zeros((5, d), F32)], axis=0)

    in_specs, args = [], []
    for p, (a, a_row0, b, b_col0, k_p, tk_p) in enumerate(pairs):
        nk, s0, ar, bc = k_p // tk_p, starts[p], a_row0 // tm, b_col0 // tk_p

        def kk(k, s0=s0, nk=nk):
            return jnp.clip(k - s0, 0, nk - 1)

        in_specs.append(pl.BlockSpec((tm, tk_p), lambda i, k, ar=ar, kk=kk: (ar + i, kk(k))))
        in_specs.append(pl.BlockSpec((d, tk_p), lambda i, k, bc=bc, kk=kk: (0, bc + kk(k)),
                                     pipeline_mode=pl.Buffered(1 if nk == 1 else 2)))
        args += [a, b]
    xr = x_row0 // tm
    in_specs += [pl.BlockSpec((tm, d), lambda i, k: (xr + i, 0)), pl.BlockSpec((1, d), lambda i, k: (0, 0)),
                 pl.BlockSpec((1, d), lambda i, k: (0, 0))]
    args += [x_arr, g, scale]
    out_specs, out_shape = [], []
    if has_dx:
        in_specs.append(pl.BlockSpec((tm, d), lambda i, k: (i, 0)))
        args.append(resid)
        out_specs.append(pl.BlockSpec((tm, d), lambda i, k: (i, 0)))
        out_shape.append(SDS((m_rows, d), F32))
    out_specs.append(pl.BlockSpec((8, d), lambda i, k: (0, 0)))
    out_shape.append(SDS((8, d), F32))
    return pl.pallas_call(
        body, name=name, grid=(ni, total), in_specs=in_specs, out_specs=out_specs, out_shape=out_shape,
        scratch_shapes=[pltpu.VMEM((tm, d), F32)], compiler_params=_cp("arbitrary", "arbitrary"))(*args)


def _wgrad(a, b, k_rows, name):
    m, n = a.shape[1], b.shape[1]
    tm = _pick(m, (1408, 1024, 512, 384, 256, 128))
    tn = _pick(n, (3072, 2816, 2560, 1408, 1024, 768, 512, 384, 256, 128))
    tk = _pick(k_rows, (1280, 1024, 256))
    nk = k_rows // tk

    def body(a_ref, b_ref, o_ref, acc):
        k = pl.program_id(2)

        @pl.when(k == 0)
        def _():
            acc[...] = jnp.zeros_like(acc)

        acc[...] += _dot_tn(a_ref[...], b_ref[...])

        @pl.when(k == nk - 1)
        def _():
            o_ref[...] = _bf(acc[...])

    return pl.pallas_call(
        body, name=name, grid=(m // tm, n // tn, nk),
        in_specs=[pl.BlockSpec((tk, tm), lambda i, j, k: (k, i)), pl.BlockSpec((tk, tn), lambda i, j, k: (k, j))],
        out_specs=pl.BlockSpec((tm, tn), lambda i, j, k: (i, j)), out_shape=SDS((m, n), BF16),
        scratch_shapes=[pltpu.VMEM((tm, tn), F32)],
        compiler_params=_cp("arbitrary", "arbitrary", "arbitrary"))(a, b)


def _lane_put(col, lane_idx):
    lane = lax.broadcasted_iota(jnp.int32, (1, LANES), 1)
    return jnp.where(lane == lane_idx, col, 0.0)


def _mixer_bwd(dh1, out, hf, hb, z_main, pm, ps, hg, lng, lnb, w_s, b_st, wbm, wbs, wout, mx2, t_rows, nh):
    d = dh1.shape[1]
    ng, sc = w_s.shape[0], w_s.shape[1]
    dh, gd = d // nh, d // ng
    tb = _pick(t_rows, (256,))

    def body(dh1_ref, out_ref, hf_ref, hb_ref, zo, zu, zv, zgm, zgg, pm_ref, ps_ref, hg_ref, lng_ref, lnb_ref, ws_ref, bs_ref,
             wbm_ref, wbs_ref, wo_ref, mx2_ref, dz_ref, dhs_ref, dout_ref, dpm_ref, dps_ref, st_ref, dws_ref, dbs_ref):
        i = pl.program_id(0)

        @pl.when(i == 0)
        def _():
            st_ref[...] = jnp.zeros_like(st_ref)
            dws_ref[...] = jnp.zeros_like(dws_ref)
            dbs_ref[...] = jnp.zeros_like(dbs_ref)

        dh1v = dh1_ref[...]
        doutb = _bf(dh1v * mx2_ref[...])
        dout_ref[...] = doutb
        d_mx2 = jnp.sum(dh1v * out_ref[...].astype(F32), axis=0, keepdims=True)
        dy = _dot_nt(doutb, wo_ref[...])
        sgm, sgg = _sigmoid(zgm[...].astype(F32)), _sigmoid(zgg[...].astype(F32))
        dpmb, dpsb = _bf(dy * sgm), _bf(dy * sgg)
        dpm_ref[...] = dpmb
        dps_ref[...] = dpsb
        dz_ref[:, 3 * d:4 * d] = _bf(dy * pm_ref[...].astype(F32) * sgm * (1.0 - sgm))
        dz_ref[:, 4 * d:5 * d] = _bf(dy * ps_ref[...].astype(F32) * sgg * (1.0 - sgg))
        dym = _dot_nt(dpmb, wbm_ref[...])
        dys = _dot_nt(dpsb, wbs_ref[...])
        hs = hf_ref[...].astype(F32) + hb_ref[...].astype(F32)
        hn, scales = _head_rms(hs, nh, dh)
        so = _sigmoid(zo[...].astype(F32))
        dz_ref[:, 0:d] = _bf(dym * (hn * hg_ref[...]) * so * (1.0 - so))
        dhmn = dym * so
        d_hg = jnp.sum(dhmn * hn, axis=0, keepdims=True)
        dhn = dhmn * hg_ref[...]
        for h in range(nh):
            sl = slice(h * dh, (h + 1) * dh)
            dhs_ref[:, sl] = _bf(scales[h] * (dhn[:, sl] - hn[:, sl] * jnp.mean(dhn[:, sl] * hn[:, sl], axis=-1, keepdims=True)))
        zuv, zvv = zu[...].astype(F32), zv[...].astype(F32)
        u, du_dz = _gelu_and_grad(zuv)
        vg, dvg_dz = _gelu_and_grad(zvv)
        vhat, rstd = _layer_norm(vg)
        vnb = _bf(vhat * lng_ref[...] + lnb_ref[...])
        mixed = _sgu_mix(vnb, ws_ref, bs_ref, tb, ng, gd, sc)
        dz_ref[:, d:2 * d] = _bf(dys * mixed * du_dz)
        dmix = dys * u
        rows = []
        dbs = jnp.zeros((sc, LANES), F32)
        for ch in range(tb // sc):
            cols = []
            for g in range(ng):
                dm = dmix[ch * sc:(ch + 1) * sc, g * gd:(g + 1) * gd]
                dmb = _bf(dm)
                dws_ref[g] += _dot_nt(dmb, vnb[ch * sc:(ch + 1) * sc, g * gd:(g + 1) * gd])
                dbs = dbs + _lane_put(jnp.sum(dm, axis=1, keepdims=True), g)
                cols.append(_dot_tn(_bf(ws_ref[g]), dmb))
            rows.append(jnp.concatenate(cols, axis=1))
        dbs_ref[...] += dbs
        dvn = jnp.concatenate(rows, axis=0)
        d_lng = jnp.sum(dvn * vhat, axis=0, keepdims=True)
        d_lnb = jnp.sum(dvn, axis=0, keepdims=True)
        dvh = dvn * lng_ref[...]
        dvg = rstd * (dvh - jnp.mean(dvh, axis=-1, keepdims=True) - vhat * jnp.mean(dvh * vhat, axis=-1, keepdims=True))
        dz_ref[:, 2 * d:3 * d] = _bf(dvg * dvg_dz)
        st_ref[...] += jnp.concatenate([d_mx2, d_hg, d_lng, d_lnb, jnp.zeros((4, d), F32)], axis=0)

    def tok(col):
        return pl.BlockSpec((tb, d), lambda i: (i, col))

    def full(shape):
        return pl.BlockSpec(shape, lambda i: (0,) * len(shape))

    return pl.pallas_call(
        body, name="mixer_bwd", grid=(t_rows // tb,),
        in_specs=[tok(0), tok(0), tok(0), tok(0), tok(3), tok(4), tok(5), tok(6), tok(7), tok(0), tok(0), full((1, d)),
                  full((1, d)), full((1, d)), full((ng, sc, sc)), full((sc, LANES)), full((d, d)), full((d, d)), full((d, d)),
                  full((1, d))],
        out_specs=[pl.BlockSpec((tb, 5 * d), lambda i: (i, 0)), tok(0), tok(0), tok(0), tok(0), full((8, d)), full((ng, sc, sc)),
                   full((sc, LANES))],
        out_shape=[SDS((t_rows, 5 * d), BF16)] + [SDS((t_rows, d), BF16)] * 4 + [SDS((8, d), F32), SDS((ng, sc, sc), F32),
                                                                                SDS((sc, LANES), F32)],
        compiler_params=_cp("arbitrary"))(dh1, out, hf, hb, z_main, z_main, z_main, z_main, z_main, pm, ps, hg, lng, lnb, w_s,
                                          b_st, wbm, wbs, wout, mx2)


def _mlstm_bwd(qk, z_main, zg, bias, dhs, hf, hb, states_f, states_b, nh, t_rows):
    s_rows = qk.shape[0]
    md = qk.shape[1] // 2
    dh = md // nh
    nc = s_rows // LCH
    nx = t_rows // LCH
    ln = LCH

    def chunk_f(i):
        return jnp.where(i == nc - 1, nc - 1, nc - 2 - i)

    def chunk_b(i):
        return jnp.where(i == nc - 1, nc - 1, i)

    def body(qf, kf, vf, gf, dhf, hsf, cf, nf, mf_, qb, kb, vb, gb, dhb, hsb, cb, nb, mb_, bias_ref, dqkvf_ref, dgf_ref, dqkvb_ref,
             dgb_ref, dc_sc, dn_sc):
        i = pl.program_id(0)
        is_ctx = i == nc - 1

        @pl.when(i == 0)
        def _():
            dc_sc[...] = jnp.zeros_like(dc_sc)
            dn_sc[...] = jnp.zeros_like(dn_sc)

        sides = ((qf, kf, vf, gf, dhf, hsf, cf, nf, mf_, dqkvf_ref, dgf_ref), (qb, kb, vb, gb, dhb, hsb, cb, nb, mb_, dqkvb_ref, dgb_ref))
        gates = [_chunk_gates(s[3][...], bias_ref[...], dr == 1) for dr, s in enumerate(sides)]
        units = []
        for dr, (q_ref, k_ref, v_ref, _, dh_ref, hs_ref, c_ref, n_ref, m_ref, dqkv_ref, _) in enumerate(sides):
            gz, b_all, b_t, g_t, g_all, mask, _ = gates[dr]
            for h in range(nh):
                ci, cfl = 2 * dr * nh + h, (2 * dr + 1) * nh + h
                sl = slice(h * dh, (h + 1) * dh)
                u = dict(dr=dr, h=h, sl=sl, ci=ci, cfl=cfl, dqkv_ref=dqkv_ref, q=q_ref[:, sl], k=k_ref[:, sl], v=v_ref[:, sl],
                         dhv=jnp.where(is_ctx, 0.0, dh_ref[:, sl].astype(F32)), hs=hs_ref[:, sl].astype(F32),
                         c_in=c_ref[sl, :], n_in=n_ref[0:1, sl], m_in=m_ref[h, 0:1, 0:1],
                         b_col=b_all[:, cfl:cfl + 1], i_col=gz[:, ci:ci + 1], g=g_all[:, cfl:cfl + 1],
                         dc_new=dc_sc[dr, h], dn_new=dn_sc[dr, h, 0:1, :])
                u["qf32"], u["kf32"] = u["q"].astype(F32), u["k"].astype(F32)
                u["w"], u["w_int"], u["m_row"] = _head_weights(u["b_col"], b_t[cfl:cfl + 1, :], g_t[ci:ci + 1, :], u["m_in"], mask)
                u["qk"] = _dot_nt(u["q"], u["k"])
                units.append(u)
        for u in units:
            s_mat = u["qk"] * u["w"]
            u["s_mat"], u["sb"], u["cb16"], u["dcb"] = s_mat, _bf(s_mat), _bf(u["c_in"]), _bf(u["dc_new"])
            den = jnp.sum(s_mat, axis=1, keepdims=True) + u["w_int"] * jnp.sum(u["qf32"] * u["n_in"], axis=1, keepdims=True)
            e_m = jnp.exp(-u["m_row"])
            dnm = jnp.maximum(jnp.abs(den), e_m)
            hdh = jnp.sum(u["hs"] * u["dhv"], axis=1, keepdims=True)
            u["dden"] = jnp.where(jnp.abs(den) > e_m, -(hdh / dnm) * jnp.sign(den), 0.0)
            u["dnum_b"] = _bf(u["dhv"] / dnm)
            u["a_old"], u["coef"], _ = _head_state_coeffs(u["g"], u["b_col"], u["i_col"], u["m_in"])
            u["dsm"] = _dot_nt(u["dnum_b"], u["v"])
            u["qct"] = _dot_nt(u["dnum_b"], u["cb16"])
            u["vdc"] = _dot_nt(u["v"], u["dcb"])
        for u in units:
            ds = u["dsm"] + u["dden"]
            u["pb"] = _bf(u["w"] * ds)
            u["gmat"] = u["s_mat"] * ds
            u["dv1"] = _dot_tn(u["sb"], u["dnum_b"])
            u["dv2"] = _dot(_bf(u["kf32"] * u["coef"]), u["dcb"])
            u["dcu"] = _dot_tn(_bf(u["qf32"] * u["w_int"]), u["dnum_b"])
        for u in units:
            u["dq1"] = _dot(u["pb"], u["k"])
            u["dk1"] = _dot_tn(u["pb"], u["q"])
        acc = [dict(x1=jnp.zeros((ln, LANES), F32), x2=jnp.zeros((ln, LANES), F32), dig=jnp.zeros((ln, LANES), F32),
                    e_row=jnp.zeros((1, LANES), F32)) for _ in range(2)]
        for u in units:
            dr, h, sl, a = u["dr"], u["h"], u["sl"], acc[u["dr"]]
            dq_inter = u["w_int"] * (u["qct"] + u["dden"] * u["n_in"])
            dk_state = u["coef"] * (u["vdc"] + u["dn_new"])
            u["dqkv_ref"][:, sl] = _bf(u["dq1"] + dq_inter)
            u["dqkv_ref"][:, md + h * dh:md + (h + 1) * dh] = _bf(u["dk1"] + dk_state)
            u["dqkv_ref"][:, 2 * md + h * dh:2 * md + (h + 1) * dh] = _bf(u["dv1"] + u["dv2"])
            row_intra = jnp.sum(u["gmat"], axis=1, keepdims=True)
            col_intra = jnp.sum(u["gmat"].T, axis=1, keepdims=True)
            row_inter = jnp.sum(u["qf32"] * dq_inter, axis=1, keepdims=True)
            col_inter = jnp.sum(u["kf32"] * dk_state, axis=1, keepdims=True)
            e_old = u["a_old"] * (jnp.sum(jnp.sum(u["dc_new"] * u["c_in"], axis=1, keepdims=True), axis=0, keepdims=True)
                                  + jnp.sum(u["dn_new"] * u["n_in"], axis=1, keepdims=True))
            a["x1"] = a["x1"] + _lane_put(row_intra - col_intra + row_inter, u["cfl"])
            a["x2"] = a["x2"] + _lane_put(col_inter, u["cfl"])
            a["e_row"] = a["e_row"] + _lane_put(e_old, u["cfl"])
            a["dig"] = a["dig"] + _lane_put(col_intra + col_inter, u["ci"])
            dc_sc[dr, h] = u["a_old"] * u["dc_new"] + u["dcu"]
            dn_sc[dr, h] = jnp.broadcast_to(
                u["a_old"] * u["dn_new"] + jnp.sum(u["qf32"] * (u["w_int"] * u["dden"]), axis=0, keepdims=True), (8, dh))
        for dr, s in enumerate(sides):
            gz, mfl, a = gates[dr][0], gates[dr][6], acc[dr]
            dlogf = _mask_dot_t(mfl, a["x1"]) + _mask_dot(mfl, a["x2"]) - a["x2"] + a["e_row"]
            s[10][...] = a["dig"] + dlogf / (1.0 + jnp.exp(gz))

    def tok(cfn, col):
        return pl.BlockSpec((ln, md), lambda i: (cfn(i), col))

    def dht(cfn):
        return pl.BlockSpec((ln, md), lambda i: (jnp.minimum(cfn(i), nx - 1), 0))

    def gat(cfn):
        return pl.BlockSpec((ln, LANES), lambda i: (cfn(i), 0))

    def st(cfn, shape):
        return pl.BlockSpec((None,) + shape, lambda i: (cfn(i),) + (0,) * len(shape))

    st_shapes = ((nh * dh, dh), (8, md), (nh, 8, LANES))

    def side(cfn):
        return [tok(cfn, 0), tok(cfn, 1), tok(cfn, 2), gat(cfn), dht(cfn), tok(cfn, 0)] + [st(cfn, s) for s in st_shapes]

    def outs(cfn):
        return [pl.BlockSpec((ln, 3 * md), lambda i: (cfn(i), 0)), gat(cfn)]

    return pl.pallas_call(
        body, name="mlstm_bwd", grid=(nc,),
        in_specs=side(chunk_f) + side(chunk_b) + [pl.BlockSpec((1, LANES), lambda i: (0, 0))],
        out_specs=outs(chunk_f) + outs(chunk_b),
        out_shape=[SDS((s_rows, 3 * md), BF16), SDS((s_rows, LANES), F32)] * 2,
        scratch_shapes=[pltpu.VMEM((2, nh, dh, dh), F32), pltpu.VMEM((2, nh, 8, dh), F32)],
        compiler_params=_cp("arbitrary"))(qk, qk, z_main, zg, dhs, hf, *states_f, qk, qk, z_main, zg, dhs, hb, *states_b, bias)


def _qkv_conv_bwd(dqkv_f, dqkv_b, z_main, conv_w, t_rows, md, qscale):
    s_rows = z_main.shape[0]
    tb = _pick(s_rows, (1280, 1024, 256))
    cb = _pick(md, (512, 256, 128))
    ni, nj, ncq = s_rows // tb, 3 * md // cb, 2 * md // cb
    nb8 = tb // 8
    n_ext = tb + 16

    def body(fm, fp, fn, bm, bp, bn, zm, zp, zn, w_ref, dz_ref, gw_ref):
        j, i = pl.program_id(0), pl.program_id(1)

        @pl.when(j < ncq)
        def _():
            z = jnp.concatenate([zp[...], zm[...], zn[...]], axis=0).astype(F32)
            dqk = (jnp.concatenate([fp[...], fm[...], fn[...]], axis=0).astype(F32)
                   + jnp.concatenate([bp[...], bm[...], bn[...]], axis=0).astype(F32)) * jnp.where(j * cb < md, qscale, 1.0)
            row = i * tb - 8 + lax.broadcasted_iota(jnp.int32, (n_ext, 1), 0)
            prev_ok, next_ok = _seg_masks(row, t_rows, s_rows)
            zprev = jnp.where(prev_ok, pltpu.roll(z, 1, 0), 0.0)
            znext = jnp.where(next_ok, pltpu.roll(z, n_ext - 1, 0), 0.0)
            pre = w_ref[0:1, :] * zprev + w_ref[1:2, :] * z + w_ref[2:3, :] * znext
            sg = _sigmoid(pre)
            dpre = dqk * (sg * (1.0 + pre * (1.0 - sg)))
            dz = (w_ref[1:2, :] * dpre + w_ref[0:1, :] * jnp.where(next_ok, pltpu.roll(dpre, n_ext - 1, 0), 0.0)
                  + w_ref[2:3, :] * jnp.where(prev_ok, pltpu.roll(dpre, 1, 0), 0.0))
            dz_ref[...] = _bf(dz[8:8 + tb])
            dm = dpre[8:8 + tb]

            @pl.when(i == 0)
            def _():
                gw_ref[...] = jnp.zeros_like(gw_ref)

            gw_ref[...] += jnp.concatenate(
                [jnp.sum(dm * zprev[8:8 + tb], axis=0, keepdims=True), jnp.sum(dm * z[8:8 + tb], axis=0, keepdims=True),
                 jnp.sum(dm * znext[8:8 + tb], axis=0, keepdims=True), jnp.zeros((5, cb), F32)], axis=0)

        @pl.when(j >= ncq)
        def _():
            dz_ref[...] = _bf(fm[...].astype(F32) + bm[...].astype(F32))

    def halo(clampj):
        def cj(j):
            return jnp.minimum(j, ncq - 1) if clampj else j
        return [pl.BlockSpec((tb, cb), lambda j, i: (i, cj(j))),
                pl.BlockSpec((8, cb), lambda j, i: (jnp.maximum(i * nb8 - 1, 0), cj(j))),
                pl.BlockSpec((8, cb), lambda j, i: (jnp.minimum((i + 1) * nb8, s_rows // 8 - 1), cj(j)))]

    return pl.pallas_call(
        body, name="qkv_conv_bwd", grid=(nj, ni),
        in_specs=halo(False) + halo(False) + halo(True) + [pl.BlockSpec((8, cb), lambda j, i: (0, jnp.minimum(j, ncq - 1)))],
        out_specs=[pl.BlockSpec((tb, cb), lambda j, i: (i, j)), pl.BlockSpec((8, cb), lambda j, i: (0, jnp.minimum(j, ncq - 1)))],
        out_shape=[SDS((s_rows, 3 * md), BF16), SDS((8, 2 * md), F32)],
        compiler_params=_cp("arbitrary", "arbitrary"))(dqkv_f, dqkv_f, dqkv_f, dqkv_b, dqkv_b, dqkv_b, z_main, z_main, z_main, conv_w)


def _gate_grad_sum(dg_f, dg_b):
    s_rows = dg_f.shape[0]
    tb = _pick(s_rows, (1280, 1024, 256))

    def body(a_ref, b_ref, o_ref, st_ref):
        @pl.when(pl.program_id(0) == 0)
        def _():
            st_ref[...] = jnp.zeros_like(st_ref)

        s = a_ref[...] + b_ref[...]
        o_ref[...] = _bf(s)
        st_ref[...] += jnp.concatenate([jnp.sum(s, axis=0, keepdims=True), jnp.zeros((7, LANES), F32)], axis=0)

    blk = pl.BlockSpec((tb, LANES), lambda i: (i, 0))
    return pl.pallas_call(
        body, name="gate_grad_sum", grid=(s_rows // tb,), in_specs=[blk, blk],
        out_specs=[blk, pl.BlockSpec((8, LANES), lambda i: (0, 0))],
        out_shape=[SDS((s_rows, LANES), BF16), SDS((8, LANES), F32)], compiler_params=_cp("arbitrary"))(dg_f, dg_b)


def _mod_grads(silu_slots, dmx_sh, dmx_slots, dmc_tot, dmc_sh, silu_cctx, c_ctx, w_mod_c):
    d = silu_slots.shape[1]
    ncol, n6 = dmx_sh.shape[1], dmx_slots.shape[1]

    def body(ss_ref, dsh_ref, dsl_ref, dct_ref, dcs_ref, sc_ref, c_ref, w_ref, gw_ref, gb_ref, gc_ref):
        a = jnp.concatenate([ss_ref[...], sc_ref[...], jnp.zeros((7, d), F32)], axis=0)
        b = jnp.concatenate([dsh_ref[...], dcs_ref[...], jnp.zeros((7, ncol), F32)], axis=0)
        gw_ref[0] = lax.dot_general(a, b, (((0,), (0,)), ((), ())), preferred_element_type=F32, precision=HI)
        dct = dct_ref[...]
        gb_ref[...] = jnp.sum(dsl_ref[...], axis=0, keepdims=True) + jnp.concatenate(
            [dct, jnp.zeros((1, n6 - dct.shape[1]), F32)], axis=1)
        t = _dot_nt(_bf(jnp.broadcast_to(dct, (8, dct.shape[1]))), w_ref[...])
        cv = c_ref[...]
        s = _sigmoid(cv)
        gc_ref[...] = t[0:1, :] * (s * (1.0 + cv * (1.0 - s)))

    return pl.pallas_call(body, name="mod_grads", out_shape=[SDS((1, d, ncol), F32), SDS((1, n6), F32), SDS((1, d), F32)],
                          compiler_params=_cp())(silu_slots, dmx_sh, dmx_slots, dmc_tot, dmc_sh, silu_cctx, c_ctx, w_mod_c)


def _slot_sum(slots):
    ns, r = slots.shape[0], slots.shape[1]
    tb = _pick(r, (1024, 512, 256, 128, 64, 32, 16, 8))

    def body(s_ref, o_ref):
        acc = s_ref[0]
        for k in range(1, ns):
            acc = acc + s_ref[k]
        o_ref[...] = acc

    return pl.pallas_call(
        body, name="slot_sum", grid=(r // tb,), in_specs=[pl.BlockSpec((ns, tb, LANES), lambda i: (0, i, 0))],
        out_specs=pl.BlockSpec((tb, LANES), lambda i: (i, 0)), out_shape=SDS((r, LANES), F32),
        compiler_params=_cp("arbitrary"))(slots)


def _adamw(w, gslots, m, v, name):
    lead = ((None,), (0,)) if w.ndim == 3 else ((), ())
    r, cdim = w.shape[-2:]
    ns, rg = gslots.shape[0], gslots.shape[1]
    tb = r if (rg != r or r % 8) else _pick(r, (128, 64, 32, 16, 8))
    bc1, bc2 = 1.0 - ADAM_B1 ** ADAM_STEP, 1.0 - ADAM_B2 ** ADAM_STEP

    def body(w_ref, g_ref, m_ref, v_ref, go_ref, d_ref, mo_ref, vo_ref):
        g = g_ref[0, 0:tb, :].astype(F32)
        for k in range(1, ns):
            g = g + g_ref[k, 0:tb, :].astype(F32)
        mn = ADAM_B1 * m_ref[...] + (1.0 - ADAM_B1) * g
        vn = ADAM_B2 * v_ref[...] + (1.0 - ADAM_B2) * (g * g)
        go_ref[...] = g
        mo_ref[...] = mn
        vo_ref[...] = vn
        d_ref[...] = -ADAM_LR * ((mn / bc1) / (jnp.sqrt(vn / bc2) + ADAM_EPS) + ADAM_WD * w_ref[...])

    blk = pl.BlockSpec(lead[0] + (tb, cdim), lambda i: lead[1] + (i, 0))
    gblk = pl.BlockSpec((ns, tb if rg == r else rg, cdim), lambda i: (0, i, 0))
    return pl.pallas_call(
        body, name=name, grid=(r // tb,), in_specs=[blk, gblk, blk, blk],
        out_specs=[blk] * 4, out_shape=[SDS(w.shape, F32)] * 4, compiler_params=_cp("arbitrary"))(w, gslots, m, v)


def _pack(parts, row_mult):
    flat = jnp.concatenate([p.reshape(-1) for p in parts])
    n = flat.shape[0]
    rows = -(-n // LANES)
    rows = -(-rows // row_mult) * row_mult
    return jnp.pad(flat, (0, rows * LANES - n)).reshape(rows, LANES)


def _unpack(buf, shapes):
    flat = buf.reshape(-1)
    out, off = [], 0
    for s in shapes:
        n = math.prod(s)
        out.append(flat[off:off + n].reshape(s))
        off += n
    return out


def _pad_cols(a, width):
    return jnp.pad(a, ((0, 0), (0, width - a.shape[1])))


def _pad_lanes(a):
    return _pad_cols(a, LANES)


def _up128(n):
    return -(-n // LANES) * LANES


def kernel(x, c, ctx, c_ctx, w_mod, b_mod, norm1_g, w_in, b_gate, conv_qk, head_norm_g, sgu_ln_g, sgu_ln_b, w_s, b_s, w_branch_mlstm, w_branch_sgu, w_out, norm2_g, w_up, w_ffn_conv, w_down, final_g, loss_target, m_c_ctx, m_w_mod, m_b_mod, m_norm1_g, m_w_in, m_b_gate, m_conv_qk, m_head_norm_g, m_sgu_ln_g, m_sgu_ln_b, m_w_s, m_b_s, m_w_branch_mlstm, m_w_branch_sgu, m_w_out, m_norm2_g, m_w_up, m_w_ffn_conv, m_w_down, m_final_g, v_c_ctx, v_w_mod, v_b_mod, v_norm1_g, v_w_in, v_b_gate, v_conv_qk, v_head_norm_g, v_sgu_ln_g, v_sgu_ln_b, v_w_s, v_b_s, v_w_branch_mlstm, v_w_branch_sgu, v_w_out, v_norm2_g, v_w_up, v_w_ffn_conv, v_w_down, v_final_g):
    t, d = x.shape[1], x.shape[2]
    n_ctx = ctx.shape[1]
    s_rows = t + n_ctx
    nh = b_gate.shape[1] // 4
    md = head_norm_g.shape[1]
    dh = md // nh
    ng, sc = w_s.shape[1], w_s.shape[2]
    dff = w_down.shape[1] * N_DEV
    n_in = w_in.shape[2] * N_DEV
    assert md == d and sgu_ln_g.shape[1] == d and n_ctx == LCH and t % LCH == 0 and t % (8 * GRID_W) == 0
    assert n_in == 8 * d + 4 * nh and 4 * nh <= LANES
    me = 4 * lax.axis_index("x") + 2 * lax.axis_index("y") + lax.axis_index("c")

    n_mod, n_insh, n_upsh = w_mod.shape[2], w_in.shape[2], w_up.shape[2]
    p_mod, p_in, p_up = _up128(n_mod), _up128(n_insh), _up128(n_upsh)
    nq, nf = conv_qk.shape[2], w_ffn_conv.shape[3]
    ffn9 = w_ffn_conv[0].reshape(9, nf)
    colpack = jnp.concatenate([_pad_cols(_bf(w_mod[0]), p_mod), _pad_cols(_bf(w_in[0]), p_in)], axis=1)
    convpack = jnp.concatenate([jnp.pad(conv_qk[0], ((0, 13), (0, 0))), jnp.pad(ffn9, ((0, 7), (0, 0)))], axis=1)
    g_col, g_conv = _allgather([colpack, convpack])
    w_mod_f, w_main, w_gate = _assemble_cols(
        g_col, [(0, n_mod, [(0, 0, N_DEV * n_mod, 0)]),
                (p_mod, n_insh, [(1, 0, 3 * md, 0), (2, 3 * md, 4 * nh, 0), (1, 3 * md + 4 * nh, 5 * d, 3 * md)])],
        [N_MOD * d, 8 * d, LANES], "assemble_weights")
    convw, wconv9 = _assemble_cols(g_conv, [(0, nq, [(0, 0, N_DEV * nq, 0)]), (nq, nf, [(1, 0, N_DEV * nf, 0)])],
                                   [N_DEV * nq, N_DEV * nf], "assemble_conv_weights")
    zero = jnp.minimum(jnp.abs(g_conv[0, 0, 0]), 0.0)
    late_w = [_pad_cols(_bf(w_up[0] + zero), p_up), _bf(w_branch_mlstm[0]), _bf(w_branch_sgu[0]), _bf(w_out[0]), _bf(w_down[0])]
    late_state, late_tok = _exchange_start(late_w, False, "late_weights_start")

    cvec = jnp.concatenate([c, c_ctx[None], jnp.zeros((6, d), F32)], axis=0) + late_tok[0:1, 0:1]
    silu_v, mod = _modulation(cvec, w_mod_f, b_mod)
    mx = [mod[0:1, k * d:(k + 1) * d] for k in range(N_MOD)]
    mc = [mod[1:2, k * d:(k + 1) * d] for k in range(2)]
    x2, ctx2 = x[0], ctx[0]
    in_x = _norm_mod_proj(x2, norm1_g, jnp.concatenate([mx[0], mx[1]], axis=0), w_main, w_gate, s_rows, 0, None, "in_proj")
    hn, z_main, zg = _norm_mod_proj(ctx2, norm1_g, jnp.concatenate([mc[0], mc[1]], axis=0), w_main, w_gate, s_rows, t, in_x,
                                    "in_proj_ctx")
    qscale = dh ** -0.5
    qk = _qk_conv(z_main, convw, t, md, qscale)
    bias = _pad_lanes(b_gate)
    fwd = _mlstm_fwd(qk, z_main, zg, bias, nh)
    hf, hb, states_f, states_b = fwd[0], fwd[1], fwd[2:5], fwd[5:8]
    g_up, g_bm, g_bs, g_out, g_down = _exchange_wait(late_state, fwd[4], "late_weights_wait")
    (w_up_f,) = _assemble_cols(g_up, [(0, n_upsh, [(0, 0, 2 * dff, 0)])], [2 * dff], "assemble_w_up")
    wbm_f, wbs_f, wout_f = (g.reshape(d, d) for g in (g_bm, g_bs, g_out))
    w_down_f = g_down.reshape(dff, d)
    b_st = _pad_lanes(b_s[0].T)
    h1, ym, ys, pm, ps, y, out = _mixer_fwd(hf, hb, z_main, x2, head_norm_g, sgu_ln_g, sgu_ln_b, w_s[0], b_st, wbm_f, wbs_f,
                                            wout_f, mx[2], t, nh)
    hn2, ab = _norm_mod_proj(h1, norm2_g, jnp.concatenate([mx[3], mx[4]], axis=0), w_up_f, None, t, 0, None, "up_proj")
    aconv, f, dh2, dffn, st_tail = _ffn_tail(ab, wconv9, w_down_f, h1, mx[5], final_g[None], loss_target[0], dff)

    db, dac = _ffn_bwd_gate(dffn, w_down_f, aconv, ab, dff)
    da, g_wconv9 = _ffn_conv_bwd(dac, ab, wconv9, dff)
    g_wdown = _wgrad(f, dffn, t, "wgrad_down")
    gwup_slots = _scatter_cols([_wgrad(hn2, da, t, "wgrad_up_a"), _wgrad(hn2, db, t, "wgrad_up_b")],
                               [(0, 0, dff, 0), (1, dff, dff, 0)], n_upsh, "scatter_grad_w_up")
    dh1, st_n2 = _proj_norm_bwd([(da, 0, w_up_f, 0, dff, dff), (db, 0, w_up_f, dff, dff, dff)], h1, 0, norm2_g, mx[4], dh2, t,
                                "up_proj_bwd", (512, 256))
    dz_rest, dhs, dout, dpm, dps, st_mix, g_ws, g_bst = _mixer_bwd(dh1, out, hf, hb, z_main, pm, ps, head_norm_g, sgu_ln_g,
                                                                    sgu_ln_b, w_s[0], b_st, wbm_f, wbs_f, wout_f, mx[2], t, nh)
    g_wout = _wgrad(y, dout, t, "wgrad_out")
    g_wbm = _wgrad(ym, dpm, t, "wgrad_branch_mlstm")
    g_wbs = _wgrad(ys, dps, t, "wgrad_branch_sgu")
    ex_a = [gwup_slots, g_wdown.reshape(N_DEV, dff // N_DEV, d), g_wbm.reshape(N_DEV, d // N_DEV, d),
            g_wbs.reshape(N_DEV, d // N_DEV, d), g_wout.reshape(N_DEV, d // N_DEV, d)]
    ex_a_state, ex_a_tok = _exchange_start(ex_a, True, "grad_exchange_a_start")
    dqkv_f, dg_f, dqkv_b, dg_b = _mlstm_bwd(qk, z_main, zg, bias + ex_a_tok[0:1, :], dhs, hf, hb, states_f, states_b, nh, t)
    dz_qkv, g_convqk = _qkv_conv_bwd(dqkv_f, dqkv_b, z_main, convw, t, md, qscale)
    dz_g, st_gate = _gate_grad_sum(dg_f, dg_b)
    gwin_slots = _scatter_cols(
        [_wgrad(hn, dz_qkv, s_rows, "wgrad_in_qkv"), _wgrad(hn, dz_g, s_rows, "wgrad_in_gate"), _wgrad(hn, dz_rest, t, "wgrad_in_rest")],
        [(0, 0, 3 * md, 0), (1, 3 * md, 4 * nh, 0), (2, 3 * md + 4 * nh, 5 * d, 0)], n_insh, "scatter_grad_w_in")
    gcq_slots = _scatter_cols([g_convqk], [(0, 0, 2 * md, 0)], nq, "scatter_grad_conv_qk")
    gcf_slots = _scatter_cols([g_wconv9], [(0, 0, dff, 0)], nf, "scatter_grad_ffn_conv")
    ex_b_state, ex_b_tok = _exchange_start([gwin_slots, gcq_slots, gcf_slots], True, "grad_exchange_b_start")
    tk = _pick(md, (1024, 512, 256))
    grad_x, st_n1x = _proj_norm_bwd(
        [(dz_qkv, 0, w_main, 0, 3 * md, tk), (dz_rest, 0, w_main, 3 * md, 5 * d, tk), (dz_g, 0, w_gate, 0, LANES, LANES)],
        x2, 0, norm1_g, mx[1] + ex_b_tok[0:1, 0:1], dh1, t, "in_proj_bwd")
    (st_n1c,) = _proj_norm_bwd([(dz_qkv, t, w_main, 0, 3 * md, tk), (dz_g, t, w_gate, 0, LANES, LANES)],
                               ctx2, 0, norm1_g, mc[1] + ex_b_tok[0:1, 0:1], None, n_ctx, "in_proj_bwd_ctx")

    rx_a = _exchange_wait(ex_a_state, st_n1c, "grad_exchange_a_wait")
    rx_b = _exchange_wait(ex_b_state, st_n1c, "grad_exchange_b_wait")
    recv = [rx_b[0], rx_a[0], rx_a[2], rx_a[3], rx_a[4], rx_a[1], rx_b[1], rx_b[2]]
    small_parts = [st_n1x[1], st_n1x[2], st_mix[0], st_n2[1], st_n2[2], st_tail[1],
                   st_n1c[1], st_n1c[2],
                   silu_v[0], st_n1x[0] + st_n1c[0], st_gate[0], st_mix[1], st_mix[2], st_mix[3],
                   g_ws.reshape(-1), g_bst[:, :ng].T.reshape(-1), st_n2[0], st_tail[0]]
    gsmall = _pack(small_parts, 8)
    (recv_small,) = _grad_exchange([], [gsmall])
    small_sum = _slot_sum(recv_small).reshape(-1)
    small_slots = recv_small.reshape(N_DEV, -1)
    o_silu, o_n1 = 8 * d, 9 * d
    ncol = N_MOD * d // N_DEV
    dmc_tot = small_sum[6 * d:8 * d][None]
    dmc_pad = jnp.concatenate([dmc_tot, jnp.zeros((1, 4 * d), F32)], axis=1)
    g_wmod, g_bmod, g_cctx = _mod_grads(
        small_slots[:, o_silu:o_silu + d], lax.dynamic_slice_in_dim(small_slots[:, :6 * d], me * ncol, ncol, axis=1),
        small_slots[:, :6 * d], dmc_tot, lax.dynamic_slice_in_dim(dmc_pad, me * ncol, ncol, axis=1), silu_v[1:2], c_ctx[None],
        w_mod_f[:, :2 * d])

    shard_w = (w_in, w_up, w_branch_mlstm, w_branch_sgu, w_out, w_down, conv_qk)
    shard_m = (m_w_in, m_w_up, m_w_branch_mlstm, m_w_branch_sgu, m_w_out, m_w_down, m_conv_qk)
    shard_v = (v_w_in, v_w_up, v_w_branch_mlstm, v_w_branch_sgu, v_w_out, v_w_down, v_conv_qk)
    shard_names = ("w_in", "w_up", "w_branch_mlstm", "w_branch_sgu", "w_out", "w_down", "conv_qk")
    shard_out = [_adamw(wa, recv[k], ma, va, "adamw_" + nm)
                 for k, (wa, ma, va, nm) in enumerate(zip(shard_w, shard_m, shard_v, shard_names))]
    shard_out.append([b.reshape(w_ffn_conv.shape) for b in
                      _adamw(ffn9, recv[7], m_w_ffn_conv[0].reshape(9, nf), v_w_ffn_conv[0].reshape(9, nf), "adamw_w_ffn_conv")])
    mod_out = _adamw(w_mod, g_wmod, m_w_mod, v_w_mod, "adamw_w_mod")

    def rep(cc, bm, n1, bg, hg, lg, lb, ws, bs, n2, fg):
        return [cc.reshape(-1), bm.reshape(-1), n1.reshape(-1), _pad_lanes(bg.reshape(1, -1)).reshape(-1), hg.reshape(-1),
                lg.reshape(-1), lb.reshape(-1), ws.reshape(-1), bs.reshape(-1), n2.reshape(-1), fg.reshape(-1)]

    o = o_n1
    g_rep_parts = [g_cctx, g_bmod]
    for n in (d, LANES, d, d, d, ng * sc * sc, ng * sc, d, d):
        g_rep_parts.append(small_sum[o:o + n])
        o += n
    rep_shapes = [(d,), (1, N_MOD * d), (1, d), (1, LANES), (1, d), (1, d), (1, d), (1, ng, sc, sc), (1, ng, sc), (1, d), (d,)]
    rep_out = _adamw(
        _pack(rep(c_ctx, b_mod, norm1_g, b_gate, head_norm_g, sgu_ln_g, sgu_ln_b, w_s, b_s, norm2_g, final_g), 8),
        _pack(g_rep_parts, 8)[None],
        _pack(rep(m_c_ctx, m_b_mod, m_norm1_g, m_b_gate, m_head_norm_g, m_sgu_ln_g, m_sgu_ln_b, m_w_s, m_b_s, m_norm2_g, m_final_g), 8),
        _pack(rep(v_c_ctx, v_b_mod, v_norm1_g, v_b_gate, v_head_norm_g, v_sgu_ln_g, v_sgu_ln_b, v_w_s, v_b_s, v_norm2_g, v_final_g), 8),
        "adamw_replicated")

    def assemble(k):
        r = _unpack(rep_out[k], rep_shapes)
        s = [o[k] for o in shard_out]
        return [r[0], mod_out[k], r[1], r[2], s[0], r[3][:, :4 * nh], s[6], r[4], r[5], r[6], r[7], r[8], s[2], s[3], s[4], r[9],
                s[1], s[7], s[5], r[10]]

    loss = lax.psum(st_tail[2, 0], ("x", "y", "c"))
    outs = [loss, grad_x[None]]
    for k in range(4):
        outs += assemble(k)
    return tuple(outs)
```

```python
import math

import jax
import jax.numpy as jnp
from jax import lax
from jax.experimental import pallas as pl
from jax.experimental.pallas import tpu as pltpu

F32, BF16 = jnp.float32, jnp.bfloat16
EPS = 1e-6
M_INIT = -1e30
NEG = -1e30
GRID_W = 64
LCH = 256
N_MOD = 6
N_DEV = 8
LANES = 128
ADAM_LR, ADAM_B1, ADAM_B2, ADAM_EPS, ADAM_WD, ADAM_STEP = 0.001, 0.9, 0.999, 1e-08, 0.01, 10
GELU_C = math.sqrt(2.0 / math.pi)
GELU_A = 0.044715
VMEM_LIMIT = 56 * 1024 * 1024
HI = lax.Precision.HIGHEST
SDS = jax.ShapeDtypeStruct
MESH_ID = pl.DeviceIdType.MESH


def _pick(n, cands):
    for c in cands:
        if n % c == 0:
            return c
    raise ValueError(f"no block size for {n} in {cands}")


def _cp(*sem):
    return pltpu.CompilerParams(dimension_semantics=sem if sem else None, vmem_limit_bytes=VMEM_LIMIT)


def _sigmoid(x):
    return 0.5 * jnp.tanh(0.5 * x) + 0.5


def _split3(x):
    hi = x.astype(BF16)
    r = x - hi.astype(F32)
    mid = r.astype(BF16)
    return hi, mid, (r - mid.astype(F32)).astype(BF16)


def _mask_dot(mask_b, x):
    hi, mid, lo = _split3(x)
    return (_dot(mask_b, lo) + _dot(mask_b, mid)) + _dot(mask_b, hi)


def _mask_dot_t(mask_b, x):
    hi, mid, lo = _split3(x)
    return (_dot_tn(mask_b, lo) + _dot_tn(mask_b, mid)) + _dot_tn(mask_b, hi)


def _gelu(x):
    return x * (0.5 * (1.0 + jnp.tanh(GELU_C * x * (1.0 + GELU_A * (x * x)))))


def _gelu_and_grad(x):
    x2 = x * x
    t = jnp.tanh(GELU_C * x * (1.0 + GELU_A * x2))
    half = 0.5 * (1.0 + t)
    return x * half, half + (0.5 * GELU_C) * x * (1.0 - t * t) * (1.0 + 3.0 * GELU_A * x2)


def _log_sigmoid(x):
    return jnp.minimum(x, 0.0) - jnp.log(1.0 + jnp.exp(-jnp.abs(x)))


def _dot(a, b):
    return jnp.dot(a, b, preferred_element_type=F32)


def _dot_nt(a, b):
    return lax.dot_general(a, b, (((1,), (1,)), ((), ())), preferred_element_type=F32)


def _dot_tn(a, b):
    return lax.dot_general(a, b, (((0,), (0,)), ((), ())), preferred_element_type=F32)


def _bf(x):
    return x.astype(BF16)


def _allgather(arrs):
    na = len(arrs)

    def body(*refs):
        x_refs, o_refs = refs[:na], refs[na:2 * na]
        send_sems, recv_sems, local_sems = refs[2 * na:]
        x, y, c = lax.axis_index("x"), lax.axis_index("y"), lax.axis_index("c")
        me, sibling = (x, y, c), (x, y, 1 - c)
        chips = [(1 - x, y), (x, 1 - y), (1 - x, 1 - y)]

        def copy(a, k, block, to, src=None):
            slot = o_refs[a].at[4 * block[0] + 2 * block[1] + block[2]]
            return pltpu.make_async_remote_copy(
                src_ref=slot if src is None else src, dst_ref=slot, send_sem=send_sems.at[7 * a + k],
                recv_sem=recv_sems.at[7 * a + k], device_id=to, device_id_type=MESH_ID)

        mine = [pltpu.make_async_copy(x_refs[a], o_refs[a].at[4 * x + 2 * y + c], local_sems.at[a]) for a in range(na)]
        for cp in mine:
            cp.start()
        first = []
        for a in range(na):
            first.append(copy(a, 0, me, sibling, src=x_refs[a]))
            first += [copy(a, 1 + j, me, (*chip, c), src=x_refs[a]) for j, chip in enumerate(chips)]
        for cp in first:
            cp.start()
        passed = []
        for j, chip in enumerate(chips):
            for a in range(na):
                copy(a, 1 + j, (*chip, c), me).wait_recv()
                passed.append(copy(a, 4 + j, (*chip, c), sibling))
                passed[-1].start()
        for a in range(na):
            copy(a, 0, sibling, me).wait_recv()
            for j, chip in enumerate(chips):
                copy(a, 4 + j, (*chip, 1 - c), me).wait_recv()
        for cp in first + passed:
            cp.wait_send()
        for cp in mine:
            cp.wait()

    anyspec = pl.BlockSpec(memory_space=pl.ANY)
    return pl.pallas_call(
        body, name="weights_allgather",
        out_shape=[SDS((N_DEV,) + a.shape, a.dtype) for a in arrs],
        in_specs=[anyspec] * na, out_specs=[anyspec] * na,
        scratch_shapes=[pltpu.SemaphoreType.DMA((7 * na,)), pltpu.SemaphoreType.DMA((7 * na,)), pltpu.SemaphoreType.DMA((na,))],
    )(*arrs)


_HBM_SPEC = pl.BlockSpec(memory_space=pltpu.HBM)
_SEM_SPEC = pl.BlockSpec(memory_space=pltpu.SEMAPHORE)
_EFFECT = pltpu.SideEffectType.DATAFLOW_SIDE_EFFECTING


def _peer_list(x, y, c):
    out = []
    for k in range(1, N_DEV):
        px = 1 - x if k & 4 else x
        py = 1 - y if k & 2 else y
        pc = 1 - c if k & 1 else c
        out.append(((px, py, pc), 4 * px + 2 * py + pc))
    return out


def _split_copies(src, land, send_sems, recv_sems, per_dest, receive):
    x, y, c = lax.axis_index("x"), lax.axis_index("y"), lax.axis_index("c")
    me = 4 * x + 2 * y + c
    out = []
    for k, (peer, pidx) in enumerate(_peer_list(x, y, c)):
        for a in range(len(src)):
            out.append(pltpu.make_async_remote_copy(
                src_ref=src[a].at[pidx] if per_dest else src[a], dst_ref=land[a].at[pidx if receive else me],
                send_sem=send_sems.at[7 * a + k], recv_sem=recv_sems.at[7 * a + k], device_id=peer, device_id_type=MESH_ID))
    return out


def _own_copies(src, land, own_sems, per_dest):
    me = 4 * lax.axis_index("x") + 2 * lax.axis_index("y") + lax.axis_index("c")
    return [pltpu.make_async_copy(src[a].at[me] if per_dest else src[a], land[a].at[me], own_sems.at[a]) for a in range(len(src))]


def _exchange_start(arrs, per_dest, name):
    na = len(arrs)
    land_shapes = [a.shape if per_dest else (N_DEV,) + a.shape for a in arrs]
    lands = [pltpu.with_memory_space_constraint(lax.empty(s, a.dtype), pltpu.HBM) for s, a in zip(land_shapes, arrs)]

    def body(*refs):
        src, land = refs[:na], refs[na:2 * na]
        send_sems, recv_sems, own_sems, token = refs[2 * na], refs[2 * na + 1], refs[2 * na + 2], refs[-1]
        for cp in _split_copies(src, land, send_sems, recv_sems, per_dest, False) + _own_copies(src, land, own_sems, per_dest):
            cp.start()
        token[...] = jnp.zeros_like(token)

    outs = pl.pallas_call(
        body, name=name,
        out_shape=[pltpu.SemaphoreType.DMA((7 * na,)), pltpu.SemaphoreType.DMA((7 * na,)), pltpu.SemaphoreType.DMA((na,))]
        + [pltpu.HBM(a.shape, a.dtype) for a in arrs] + [pltpu.HBM(s, a.dtype) for s, a in zip(land_shapes, arrs)]
        + [SDS((8, LANES), F32)],
        in_specs=[_HBM_SPEC] * (2 * na),
        out_specs=[_SEM_SPEC] * 3 + [_HBM_SPEC] * (2 * na) + [pl.BlockSpec(memory_space=pltpu.VMEM)],
        input_output_aliases={k: 3 + k for k in range(2 * na)},
        compiler_params=pltpu.CompilerParams(has_side_effects=_EFFECT),
    )(*[pltpu.with_memory_space_constraint(a, pltpu.HBM) for a in arrs], *lands)
    return (na, per_dest, outs[:-1]), outs[-1]


def _exchange_wait(state, after, name):
    na, per_dest, started = state

    def body(*refs):
        src, land = refs[:na], refs[na:2 * na]
        send_sems, recv_sems, own_sems = refs[2 * na], refs[2 * na + 1], refs[2 * na + 2]
        for cp in _split_copies(src, land, send_sems, recv_sems, per_dest, True):
            cp.wait_send()
            cp.wait_recv()
        for cp in _own_copies(src, land, own_sems, per_dest):
            cp.wait()

    bufs = started[3:]
    outs = pl.pallas_call(
        body, name=name,
        out_shape=[pltpu.HBM(b.shape, b.dtype) for b in bufs],
        in_specs=[_HBM_SPEC] * (2 * na) + [_SEM_SPEC] * 3 + [pl.BlockSpec(memory_space=pl.ANY)],
        out_specs=[_HBM_SPEC] * (2 * na),
        input_output_aliases={k: k for k in range(2 * na)},
        compiler_params=pltpu.CompilerParams(has_side_effects=_EFFECT),
    )(*bufs, started[0], started[1], started[2], after)
    return outs[na:]


def _col_pieces(n, segments):
    out = []
    for j in range(N_DEV):
        lo, hi = j * n, (j + 1) * n
        for (k, s0, w, c0) in segments:
            a, b = max(lo, s0), min(hi, s0 + w)
            if a < b:
                out.append((j, a - lo, b - lo, k, c0 + a - s0, c0 + b - s0))
    return out


def _assemble_cols(slots, groups, out_widths, name):
    r, p = slots.shape[1], slots.shape[2]
    tb = _pick(r, (128, 64, 32, 16, 8))
    covered = [0] * len(out_widths)
    for (_, n, segs) in groups:
        for (k, _, w, _) in segs:
            covered[k] += w

    def body(s_ref, *o_refs):
        for k, wd in enumerate(out_widths):
            if covered[k] < wd:
                o_refs[k][...] = jnp.zeros_like(o_refs[k])
        for (off, n, segs) in groups:
            for (j, a0, a1, k, d0, d1) in _col_pieces(n, segs):
                o_refs[k][:, d0:d1] = s_ref[j, :, off + a0:off + a1]

    return pl.pallas_call(
        body, name=name, grid=(r // tb,), in_specs=[pl.BlockSpec((N_DEV, tb, p), lambda i: (0, i, 0))],
        out_specs=[pl.BlockSpec((tb, w), lambda i: (i, 0)) for w in out_widths],
        out_shape=[SDS((r, w), slots.dtype) for w in out_widths], compiler_params=_cp("arbitrary"))(slots)


def _scatter_cols(pieces, segments, n, name):
    r = pieces[0].shape[0]
    tb = _pick(r, (128, 64, 32, 16, 8))

    def body(*refs):
        p_refs, o_ref = refs[:-1], refs[-1]
        for (j, a0, a1, k, d0, d1) in _col_pieces(n, segments):
            o_ref[j, :, a0:a1] = p_refs[k][:, d0:d1]

    return pl.pallas_call(
        body, name=name, grid=(r // tb,), in_specs=[pl.BlockSpec((tb, a.shape[1]), lambda i: (i, 0)) for a in pieces],
        out_specs=pl.BlockSpec((N_DEV, tb, n), lambda i: (0, i, 0)), out_shape=SDS((N_DEV, r, n), pieces[0].dtype),
        compiler_params=_cp("arbitrary"))(*pieces)


def _modulation(cvec, w_mod, b_mod):
    d, n = w_mod.shape

    def body(c_ref, w_ref, b_ref, s_ref, o_ref):
        cv = c_ref[...]
        s = cv * _sigmoid(cv)
        s_ref[...] = s
        o_ref[...] = _dot(_bf(s), w_ref[...]) + b_ref[...]

    return pl.pallas_call(body, name="modulation", out_shape=(SDS((8, d), F32), SDS((8, n), F32)),
                          compiler_params=_cp())(cvec, w_mod, b_mod)


def _norm_mod_proj(x_arr, g, shsc, w_main, w_gate, rows_total, row0, filled, name):
    m_rows, d = x_arr.shape
    n = w_main.shape[1]
    tb = _pick(m_rows, (1024, 256))
    cb = _pick(n, (2048, 1408, 1024, 768, 512, 384, 256, 128))
    gate = w_gate is not None
    nout = 3 if gate else 2
    nin = 5 if gate else 4
    rb = row0 // tb

    def body(*refs):
        x_ref, g_ref, ss_ref, wm_ref = refs[:4]
        wg_ref = refs[4] if gate else None
        outs = refs[len(refs) - 1 - nout:len(refs) - 1]
        hn_ref, z_ref = outs[0], outs[1]
        hn_sc = refs[-1]

        @pl.when(pl.program_id(1) == 0)
        def _():
            x = x_ref[...]
            r = lax.rsqrt(jnp.mean(x * x, axis=-1, keepdims=True) + EPS)
            hb = _bf((x * r * g_ref[...]) * (1.0 + ss_ref[1:2, :]) + ss_ref[0:1, :])
            hn_sc[...] = hb
            hn_ref[...] = hb
            if gate:
                outs[2][...] = _dot(hb, wg_ref[...])

        z_ref[...] = _bf(_dot(hn_sc[...], wm_ref[:, pl.ds(pl.multiple_of(pl.program_id(1) * cb, cb), cb)]))

    in_specs = [pl.BlockSpec((tb, d), lambda i, j: (i, 0)), pl.BlockSpec((1, d), lambda i, j: (0, 0)),
                pl.BlockSpec((2, d), lambda i, j: (0, 0)), _resident((d, n))]
    out_specs = [pl.BlockSpec((tb, d), lambda i, j: (rb + i, 0)), pl.BlockSpec((tb, cb), lambda i, j: (rb + i, j))]
    out_shape = [SDS((rows_total, d), BF16), SDS((rows_total, n), BF16)]
    args = [x_arr, g, shsc, w_main]
    if gate:
        in_specs.append(pl.BlockSpec((d, LANES), lambda i, j: (0, 0)))
        out_specs.append(pl.BlockSpec((tb, LANES), lambda i, j: (rb + i, 0)))
        out_shape.append(SDS((rows_total, LANES), F32))
        args.append(w_gate)
    aliases = {}
    if filled is not None:
        in_specs += [pl.BlockSpec(memory_space=pl.ANY)] * nout
        args += list(filled)
        aliases = {nin + k: k for k in range(nout)}
    return pl.pallas_call(
        body, name=name, grid=(m_rows // tb, n // cb), in_specs=in_specs, out_specs=out_specs, out_shape=out_shape,
        input_output_aliases=aliases, scratch_shapes=[pltpu.VMEM((tb, d), BF16)],
        compiler_params=_cp("arbitrary", "arbitrary"))(*args)


def _seg_masks(row, t_rows, s_rows):
    prev_ok = (row != 0) & (row != t_rows)
    next_ok = (row != t_rows - 1) & (row != s_rows - 1)
    return prev_ok, next_ok


def _shift_rows(z, halo_prev, halo_next, tb):
    loc = lax.broadcasted_iota(jnp.int32, (tb, 1), 0)
    zp = jnp.where(loc == 0, halo_prev, pltpu.roll(z, 1, 0))
    zn = jnp.where(loc == tb - 1, halo_next, pltpu.roll(z, tb - 1, 0))
    return zp, zn


def _qk_conv(z_main, conv_w, t_rows, md, qscale):
    s_rows = z_main.shape[0]
    tb = _pick(s_rows, (1280, 1024, 256))
    cb = _pick(md, (512, 256, 128))
    nb8 = tb // 8

    def body(zm, zp, zn, w_ref, o_ref):
        i, j = pl.program_id(0), pl.program_id(1)
        z = zm[...].astype(F32)
        zprev, znext = _shift_rows(z, zp[7:8, :].astype(F32), zn[0:1, :].astype(F32), tb)
        row = i * tb + lax.broadcasted_iota(jnp.int32, (tb, 1), 0)
        prev_ok, next_ok = _seg_masks(row, t_rows, s_rows)
        pre = (w_ref[0:1, :] * jnp.where(prev_ok, zprev, 0.0) + w_ref[1:2, :] * z
               + w_ref[2:3, :] * jnp.where(next_ok, znext, 0.0))
        scale = jnp.where(j * cb < md, qscale, 1.0)
        o_ref[...] = _bf(pre * _sigmoid(pre) * scale)

    return pl.pallas_call(
        body, name="qk_conv", grid=(s_rows // tb, 2 * md // cb),
        in_specs=[pl.BlockSpec((tb, cb), lambda i, j: (i, j)),
                  pl.BlockSpec((8, cb), lambda i, j: (jnp.maximum(i * nb8 - 1, 0), j)),
                  pl.BlockSpec((8, cb), lambda i, j: (jnp.minimum((i + 1) * nb8, s_rows // 8 - 1), j)),
                  pl.BlockSpec((8, cb), lambda i, j: (0, j))],
        out_specs=pl.BlockSpec((tb, cb), lambda i, j: (i, j)),
        out_shape=SDS((s_rows, 2 * md), BF16), compiler_params=_cp("arbitrary", "arbitrary"))(z_main, z_main, z_main, conv_w)


def _chunk_gates(gates, bias, rev):
    ln = gates.shape[0]
    gz = gates + bias
    logf = _log_sigmoid(gz)
    r_id = lax.broadcasted_iota(jnp.int32, (ln, ln), 0)
    c_id = lax.broadcasted_iota(jnp.int32, (ln, ln), 1)
    mask = (c_id >= r_id) if rev else (c_id <= r_id)
    mb = mask.astype(F32).astype(BF16)
    b_all = _mask_dot(mb, logf)
    g_all = jnp.sum(logf, axis=0, keepdims=True)
    return gz, b_all, b_all.T, gz.T, g_all, mask, mb


def _head_weights(b_col, b_row, i_row, m_in, mask):
    d = jnp.where(mask, b_col - b_row + i_row, NEG)
    inter = b_col + m_in
    m_row = jnp.maximum(inter, jnp.max(d, axis=1, keepdims=True))
    return jnp.exp(d - m_row), jnp.exp(inter - m_row), m_row


def _head_state_coeffs(g, b_col, i_col, m_in):
    a = g - b_col + i_col
    m_new = jnp.maximum(g + m_in, jnp.max(a, axis=0, keepdims=True))
    return jnp.exp(g + m_in - m_new), jnp.exp(a - m_new), m_new


def _mlstm_fwd(qk, z_main, zg, bias, nh):
    s_rows = qk.shape[0]
    md = qk.shape[1] // 2
    dh = md // nh
    nc = s_rows // LCH
    ln = LCH

    def chunk_f(i):
        return jnp.where(i == 0, nc - 1, i - 1)

    def chunk_b(i):
        return jnp.where(i == 0, nc - 1, nc - 1 - i)

    def body(qf, kf, vf, gf, qb, kb, vb, gb, bias_ref, hf_ref, hb_ref, cf_ref, nf_ref, mf_ref, cb_ref, nb_ref, mb_ref,
             c_sc, n_sc, m_sc):
        i = pl.program_id(0)

        @pl.when(i == 0)
        def _():
            c_sc[...] = jnp.zeros_like(c_sc)
            n_sc[...] = jnp.zeros_like(n_sc)
            m_sc[...] = jnp.full(m_sc.shape, M_INIT, F32)

        sides = ((qf, kf, vf, gf, hf_ref, cf_ref, nf_ref, mf_ref), (qb, kb, vb, gb, hb_ref, cb_ref, nb_ref, mb_ref))
        gates = [_chunk_gates(s[3][...], bias_ref[...], dr == 1) for dr, s in enumerate(sides)]
        units = []
        for dr, (q_ref, k_ref, v_ref, _, h_ref, c_out, n_out, m_out) in enumerate(sides):
            gz, b_all, b_t, g_t, g_all, mask, _ = gates[dr]
            for h in range(nh):
                ci, cf = 2 * dr * nh + h, (2 * dr + 1) * nh + h
                sl = slice(h * dh, (h + 1) * dh)
                u = dict(dr=dr, h=h, sl=sl, h_ref=h_ref, q=q_ref[:, sl], k=k_ref[:, sl], v=v_ref[:, sl],
                         c_in=c_sc[dr, h], n_in=n_sc[dr, h, 0:1, :], m_in=m_sc[dr, h, 0:1, 0:1],
                         b_col=b_all[:, cf:cf + 1], i_col=gz[:, ci:ci + 1], g=g_all[:, cf:cf + 1])
                c_out[sl, :] = u["c_in"]
                n_out[:, sl] = n_sc[dr, h]
                m_out[h] = m_sc[dr, h]
                u["w"], u["w_int"], u["m_row"] = _head_weights(u["b_col"], b_t[cf:cf + 1, :], g_t[ci:ci + 1, :], u["m_in"], mask)
                u["qk"] = _dot_nt(u["q"], u["k"])
                units.append(u)
        for u in units:
            u["s_mat"] = u["qk"] * u["w"]
            u["qc"] = _dot(u["q"], _bf(u["c_in"]))
            u["a_old"], u["coef"], u["m_new"] = _head_state_coeffs(u["g"], u["b_col"], u["i_col"], u["m_in"])
            u["kw"] = u["k"].astype(F32) * u["coef"]
        for u in units:
            u["sv"] = _dot(_bf(u["s_mat"]), u["v"])
            u["kv"] = _dot_tn(_bf(u["kw"]), u["v"])
        for u in units:
            dr, h = u["dr"], u["h"]
            num = u["sv"] + u["w_int"] * u["qc"]
            den = (jnp.sum(u["s_mat"], axis=1, keepdims=True)
                   + u["w_int"] * jnp.sum(u["q"].astype(F32) * u["n_in"], axis=1, keepdims=True))
            u["h_ref"][:, u["sl"]] = _bf(num / jnp.maximum(jnp.abs(den), jnp.exp(-u["m_row"])))
            c_sc[dr, h] = u["a_old"] * u["c_in"] + u["kv"]
            n_sc[dr, h] = jnp.broadcast_to(u["a_old"] * u["n_in"] + jnp.sum(u["kw"], axis=0, keepdims=True), (8, dh))
            m_sc[dr, h] = jnp.broadcast_to(u["m_new"], (8, LANES))

    def tok(cfn, col):
        return pl.BlockSpec((ln, md), lambda i: (cfn(i), col))

    def gat(cfn):
        return pl.BlockSpec((ln, LANES), lambda i: (cfn(i), 0))

    def st(cfn, shape):
        return pl.BlockSpec((None,) + shape, lambda i: (cfn(i),) + (0,) * len(shape))

    st_shapes = ((nh * dh, dh), (8, md), (nh, 8, LANES))
    return pl.pallas_call(
        body, name="mlstm_fwd", grid=(nc,),
        in_specs=[tok(chunk_f, 0), tok(chunk_f, 1), tok(chunk_f, 2), gat(chunk_f),
                  tok(chunk_b, 0), tok(chunk_b, 1), tok(chunk_b, 2), gat(chunk_b),
                  pl.BlockSpec((1, LANES), lambda i: (0, 0))],
        out_specs=[tok(chunk_f, 0), tok(chunk_b, 0)] + [st(chunk_f, s) for s in st_shapes] + [st(chunk_b, s) for s in st_shapes],
        out_shape=[SDS((s_rows, md), BF16)] * 2 + [SDS((nc,) + s, F32) for s in st_shapes] * 2,
        scratch_shapes=[pltpu.VMEM((2, nh, dh, dh), F32), pltpu.VMEM((2, nh, 8, dh), F32), pltpu.VMEM((2, nh, 8, LANES), F32)],
        compiler_params=_cp("arbitrary"))(qk, qk, z_main, zg, qk, qk, z_main, zg, bias)


def _head_rms(hs, nh, dh):
    parts, scales = [], []
    for h in range(nh):
        hh = hs[:, h * dh:(h + 1) * dh]
        r = lax.rsqrt(jnp.mean(hh * hh, axis=-1, keepdims=True) + EPS)
        parts.append(hh * r)
        scales.append(r)
    return jnp.concatenate(parts, axis=1), scales


def _layer_norm(v):
    vc = v - jnp.mean(v, axis=-1, keepdims=True)
    r = lax.rsqrt(jnp.mean(vc * vc, axis=-1, keepdims=True) + EPS)
    return vc * r, r


def _sgu_mix(vnb, ws_ref, bs_ref, tb, ng, gd, sc):
    rows = []
    for ch in range(tb // sc):
        cols = []
        for g in range(ng):
            blk = vnb[ch * sc:(ch + 1) * sc, g * gd:(g + 1) * gd]
            cols.append(_dot(_bf(ws_ref[g]), blk) + bs_ref[:, g:g + 1])
        rows.append(jnp.concatenate(cols, axis=1))
    return jnp.concatenate(rows, axis=0)


def _mixer_fwd(hf, hb, z_main, xs, hg, lng, lnb, w_s, b_st, wbm, wbs, wout, mx2, t_rows, nh):
    d = xs.shape[1]
    ng, sc = w_s.shape[0], w_s.shape[1]
    dh, gd = d // nh, d // ng
    tb = _pick(t_rows, (256,))

    def body(hf_ref, hb_ref, zo, zu, zv, zgm, zgg, x_ref, hg_ref, lng_ref, lnb_ref, ws_ref, bs_ref, wbm_ref, wbs_ref,
             wo_ref, mx2_ref, h1_ref, ym_ref, ys_ref, pm_ref, ps_ref, y_ref, out_ref):
        hs = hf_ref[...].astype(F32) + hb_ref[...].astype(F32)
        hn, _ = _head_rms(hs, nh, dh)
        ym = _bf(_sigmoid(zo[...].astype(F32)) * (hn * hg_ref[...]))
        ym_ref[...] = ym
        vhat, _ = _layer_norm(_gelu(zv[...].astype(F32)))
        vnb = _bf(vhat * lng_ref[...] + lnb_ref[...])
        ys = _bf(_gelu(zu[...].astype(F32)) * _sgu_mix(vnb, ws_ref, bs_ref, tb, ng, gd, sc))
        ys_ref[...] = ys
        pm = _dot(ym, wbm_ref[...])
        ps = _dot(ys, wbs_ref[...])
        pm_ref[...] = _bf(pm)
        ps_ref[...] = _bf(ps)
        y = _bf(_sigmoid(zgm[...].astype(F32)) * pm + _sigmoid(zgg[...].astype(F32)) * ps)
        y_ref[...] = y
        out = _dot(y, wo_ref[...])
        out_ref[...] = _bf(out)
        h1_ref[...] = x_ref[...] + mx2_ref[...] * out

    def tok(col):
        return pl.BlockSpec((tb, d), lambda i: (i, col))

    def full(shape):
        return pl.BlockSpec(shape, lambda i: (0,) * len(shape))

    return pl.pallas_call(
        body, name="mixer_fwd", grid=(t_rows // tb,),
        in_specs=[tok(0), tok(0), tok(3), tok(4), tok(5), tok(6), tok(7), tok(0), full((1, d)), full((1, d)), full((1, d)),
                  full((ng, sc, sc)), full((sc, LANES)), full((d, d)), full((d, d)), full((d, d)), full((1, d))],
        out_specs=[tok(0)] * 7,
        out_shape=[SDS((t_rows, d), F32)] + [SDS((t_rows, d), BF16)] * 6,
        compiler_params=_cp("arbitrary"))(hf, hb, z_main, z_main, z_main, z_main, z_main, xs, hg, lng, lnb, w_s, b_st,
                                          wbm, wbs, wout, mx2)


def _resident(shape):
    return pl.BlockSpec(shape, lambda *_: (0,) * len(shape), pipeline_mode=pl.Buffered(1))


def _grid_taps(a_ext, n_ext):
    col = lax.broadcasted_iota(jnp.int32, (n_ext, 1), 0) % GRID_W
    left = jnp.where(col != 0, pltpu.roll(a_ext, 1, 0), 0.0)
    right = jnp.where(col != GRID_W - 1, pltpu.roll(a_ext, n_ext - 1, 0), 0.0)
    return left, right


def _with_halo(prev, main, nxt, i, ni, tb):
    ext = jnp.concatenate([prev, main, nxt], axis=0).astype(F32)
    pos = lax.broadcasted_iota(jnp.int32, (tb + 2 * GRID_W, 1), 0)
    inside = ((pos >= GRID_W) | (i > 0)) & ((pos < tb + GRID_W) | (i < ni - 1))
    return jnp.where(inside, ext, 0.0)


def _halo_specs(tb, cb, t_rows, col0=0):
    nh64 = tb // GRID_W
    return [pl.BlockSpec((tb, cb), lambda i, j: (i, col0 + j)),
            pl.BlockSpec((GRID_W, cb), lambda i, j: (jnp.maximum(i * nh64 - 1, 0), col0 + j)),
            pl.BlockSpec((GRID_W, cb), lambda i, j: (jnp.minimum((i + 1) * nh64, t_rows // GRID_W - 1), col0 + j))]


def _ffn_tail(ab, w_conv9, w_down, h1, mx5, gfin, target, dff):
    t_rows, d = h1.shape
    tb = _pick(t_rows, (256,))
    cb = _pick(dff, (1408, 256, 128))
    ni, nj = t_rows // tb, dff // cb
    n_ext = tb + 2 * GRID_W

    def body(am, ap, an, b_ref, wc_ref, wd_ref, h1_ref, mx5_ref, gf_ref, tg_ref, ac_ref, f_ref, dh2_ref, dffn_ref, st_ref, acc):
        i, j = pl.program_id(0), pl.program_id(1)
        a_ext = _with_halo(ap[...], am[...], an[...], i, ni, tb)
        left, right = _grid_taps(a_ext, n_ext)
        conv = jnp.zeros((tb, cb), F32)
        for di in range(3):
            o = di * GRID_W
            conv = conv + (wc_ref[3 * di:3 * di + 1, :] * left[o:o + tb] + wc_ref[3 * di + 1:3 * di + 2, :] * a_ext[o:o + tb]
                           + wc_ref[3 * di + 2:3 * di + 3, :] * right[o:o + tb])
        ac_ref[...] = _bf(conv)
        fb = _bf(conv * _sigmoid(conv) * b_ref[...].astype(F32))
        f_ref[...] = fb

        @pl.when(j == 0)
        def _():
            acc[...] = jnp.zeros_like(acc)

        @pl.when((i == 0) & (j == 0))
        def _():
            st_ref[...] = jnp.zeros_like(st_ref)

        acc[...] += _dot(fb, wd_ref[pl.ds(pl.multiple_of(j * cb, cb), cb), :])

        @pl.when(j == nj - 1)
        def _():
            ffn = acc[...]
            h2 = h1_ref[...] + mx5_ref[...] * ffn
            r = lax.rsqrt(jnp.mean(h2 * h2, axis=-1, keepdims=True) + EPS)
            xn = h2 * r
            e = xn * gf_ref[...] - tg_ref[...]
            loss = 0.5 * jnp.sum(jnp.sum(e * e, axis=1, keepdims=True), axis=0, keepdims=True) / d
            dy = e * (1.0 / d)
            dxn = dy * gf_ref[...]
            dh2 = r * (dxn - xn * jnp.mean(dxn * xn, axis=-1, keepdims=True))
            dh2_ref[...] = dh2
            dffn_ref[...] = _bf(dh2 * mx5_ref[...])
            st_ref[...] += jnp.concatenate(
                [jnp.sum(dy * xn, axis=0, keepdims=True), jnp.sum(dh2 * ffn, axis=0, keepdims=True),
                 jnp.broadcast_to(loss, (1, d)), jnp.zeros((5, d), F32)], axis=0)

    def tokd():
        return pl.BlockSpec((tb, d), lambda i, j: (i, 0))

    def rowd():
        return pl.BlockSpec((1, d), lambda i, j: (0, 0))

    return pl.pallas_call(
        body, name="ffn_tail", grid=(ni, nj),
        in_specs=_halo_specs(tb, cb, t_rows) + [pl.BlockSpec((tb, cb), lambda i, j: (i, nj + j)),
                                                pl.BlockSpec((16, cb), lambda i, j: (0, j)),
                                                _resident((dff, d)), tokd(), rowd(), rowd(), tokd()],
        out_specs=[pl.BlockSpec((tb, cb), lambda i, j: (i, j)), pl.BlockSpec((tb, cb), lambda i, j: (i, j)), tokd(), tokd(),
                   pl.BlockSpec((8, d), lambda i, j: (0, 0))],
        out_shape=[SDS((t_rows, dff), BF16), SDS((t_rows, dff), BF16), SDS((t_rows, d), F32), SDS((t_rows, d), BF16),
                   SDS((8, d), F32)],
        scratch_shapes=[pltpu.VMEM((tb, d), F32)],
        compiler_params=_cp("arbitrary", "arbitrary"))(ab, ab, ab, ab, w_conv9, w_down, h1, mx5, gfin, target)


def _ffn_bwd_gate(dffn, w_down, aconv, ab, dff):
    t_rows, d = dffn.shape
    tb = _pick(t_rows, (512,))
    cb = _pick(dff, (1408, 256, 128))
    nj = dff // cb

    def body(g_ref, wd_ref, ac_ref, b_ref, db_ref, dac_ref):
        df = _dot_nt(g_ref[...], wd_ref[pl.ds(pl.multiple_of(pl.program_id(1) * cb, cb), cb), :])
        ac = ac_ref[...].astype(F32)
        sa = _sigmoid(ac)
        db_ref[...] = _bf(df * ac * sa)
        dac_ref[...] = _bf(df * b_ref[...].astype(F32) * (sa * (1.0 + ac * (1.0 - sa))))

    blk = pl.BlockSpec((tb, cb), lambda i, j: (i, j))
    return pl.pallas_call(
        body, name="ffn_bwd_gate", grid=(t_rows // tb, nj),
        in_specs=[pl.BlockSpec((tb, d), lambda i, j: (i, 0)), _resident((dff, d)), blk,
                  pl.BlockSpec((tb, cb), lambda i, j: (i, nj + j))],
        out_specs=[blk, blk], out_shape=[SDS((t_rows, dff), BF16)] * 2,
        compiler_params=_cp("arbitrary", "arbitrary"))(dffn, w_down, aconv, ab)


def _ffn_conv_bwd(dac, ab, w_conv9, dff):
    t_rows = dac.shape[0]
    tb = _pick(t_rows, (512, 256))
    cb = _pick(dff, (1408, 256, 128))
    ni, nj = t_rows // tb, dff // cb
    n_ext = tb + 2 * GRID_W
    nh64 = tb // GRID_W

    def body(dm, dp, dn, am, ap, an, wc_ref, da_ref, gw_ref):
        i = pl.program_id(1)
        d_ext = _with_halo(dp[...], dm[...], dn[...], i, ni, tb)
        a_ext = _with_halo(ap[...], am[...], an[...], i, ni, tb)
        d_left, d_right = _grid_taps(d_ext, n_ext)
        a_left, a_right = _grid_taps(a_ext, n_ext)
        dmain = d_ext[GRID_W:GRID_W + tb]
        da = jnp.zeros((tb, cb), F32)
        rows = []
        for di in range(3):
            o = (2 - di) * GRID_W
            da = da + (wc_ref[3 * di:3 * di + 1, :] * d_right[o:o + tb] + wc_ref[3 * di + 1:3 * di + 2, :] * d_ext[o:o + tb]
                       + wc_ref[3 * di + 2:3 * di + 3, :] * d_left[o:o + tb])
            o = di * GRID_W
            for tap in (a_left, a_ext, a_right):
                rows.append(jnp.sum(dmain * tap[o:o + tb], axis=0, keepdims=True))
        da_ref[...] = _bf(da)

        @pl.when(i == 0)
        def _():
            gw_ref[...] = jnp.zeros_like(gw_ref)

        gw_ref[...] += jnp.concatenate(rows + [jnp.zeros((7, cb), F32)], axis=0)

    def halo(col0):
        return [pl.BlockSpec((tb, cb), lambda j, i: (i, col0 + j)),
                pl.BlockSpec((GRID_W, cb), lambda j, i: (jnp.maximum(i * nh64 - 1, 0), col0 + j)),
                pl.BlockSpec((GRID_W, cb), lambda j, i: (jnp.minimum((i + 1) * nh64, t_rows // GRID_W - 1), col0 + j))]

    return pl.pallas_call(
        body, name="ffn_conv_bwd", grid=(nj, ni),
        in_specs=halo(0) + halo(0) + [pl.BlockSpec((16, cb), lambda j, i: (0, j))],
        out_specs=[pl.BlockSpec((tb, cb), lambda j, i: (i, j)), pl.BlockSpec((16, cb), lambda j, i: (0, j))],
        out_shape=[SDS((t_rows, dff), BF16), SDS((16, dff), F32)],
        compiler_params=_cp("arbitrary", "arbitrary"))(dac, dac, dac, ab, ab, ab, w_conv9)


def _proj_norm_bwd(pairs, x_arr, x_row0, g, scale, resid, m_rows, name, row_blocks=(1024, 256)):
    d = x_arr.shape[1]
    tm = _pick(m_rows, row_blocks)
    te = 256
    ni = m_rows // tm
    starts, total = [], 0
    for (_, _, _, _, k_p, tk_p) in pairs:
        starts.append(total)
        total += k_p // tk_p
    npairs = len(pairs)
    has_dx = resid is not None

    def body(*refs):
        a_refs, b_refs = refs[0:2 * npairs:2], refs[1:2 * npairs:2]
        rest = refs[2 * npairs:]
        if has_dx:
            x_ref, g_ref, sc_ref, r_ref, dx_ref, st_ref, acc = rest
        else:
            x_ref, g_ref, sc_ref, st_ref, acc = rest
        i, k = pl.program_id(0), pl.program_id(1)

        @pl.when(k == 0)
        def _():
            acc[...] = jnp.zeros_like(acc)

        @pl.when((i == 0) & (k == 0))
        def _():
            st_ref[...] = jnp.zeros_like(st_ref)

        for p in range(npairs):
            nk = pairs[p][4] // pairs[p][5]

            @pl.when((k >= starts[p]) & (k < starts[p] + nk))
            def _(p=p):
                acc[...] += _dot_nt(a_refs[p][...], b_refs[p][...])

        @pl.when(k == total - 1)
        def _():
            sums = [jnp.zeros((1, d), F32)] * 3
            for r0 in range(0, tm, te):
                rows = slice(r0, r0 + te)
                dhn = acc[rows, :]
                x = x_ref[rows, :]
                r = lax.rsqrt(jnp.mean(x * x, axis=-1, keepdims=True) + EPS)
                xn = x * r
                dmod = dhn * (1.0 + sc_ref[...])
                dxn = dmod * g_ref[...]
                if has_dx:
                    dx_ref[rows, :] = r * (dxn - xn * jnp.mean(dxn * xn, axis=-1, keepdims=True)) + r_ref[rows, :]
                sums = [sums[0] + jnp.sum(dmod * xn, axis=0, keepdims=True), sums[1] + jnp.sum(dhn, axis=0, keepdims=True),
                        sums[2] + jnp.sum(dhn * (xn * g_ref[...]), axis=0, keepdims=True)]
            st_ref[...] += jnp.concatenate(sums + [jnp.zeros((5, d), F32)], axis=0)

    in_specs, args = [], []
    for p, (a, a_row0, b, b_col0, k_p, tk_p) in enumerate(pairs):
        nk, s0, ar, bc = k_p // tk_p, starts[p], a_row0 // tm, b_col0 // tk_p

        def kk(k, s0=s0, nk=nk):
            return jnp.clip(k - s0, 0, nk - 1)

        in_specs.append(pl.BlockSpec((tm, tk_p), lambda i, k, ar=ar, kk=kk: (ar + i, kk(k))))
        in_specs.append(pl.BlockSpec((d, tk_p), lambda i, k, bc=bc, kk=kk: (0, bc + kk(k)),
                                     pipeline_mode=pl.Buffered(1 if nk == 1 else 2)))
        args += [a, b]
    xr = x_row0 // tm
    in_specs += [pl.BlockSpec((tm, d), lambda i, k: (xr + i, 0)), pl.BlockSpec((1, d), lambda i, k: (0, 0)),
                 pl.BlockSpec((1, d), lambda i, k: (0, 0))]
    args += [x_arr, g, scale]
    out_specs, out_shape = [], []
    if has_dx:
        in_specs.append(pl.BlockSpec((tm, d), lambda i, k: (i, 0)))
        args.append(resid)
        out_specs.append(pl.BlockSpec((tm, d), lambda i, k: (i, 0)))
        out_shape.append(SDS((m_rows, d), F32))
    out_specs.append(pl.BlockSpec((8, d), lambda i, k: (0, 0)))
    out_shape.append(SDS((8, d), F32))
    return pl.pallas_call(
        body, name=name, grid=(ni, total), in_specs=in_specs, out_specs=out_specs, out_shape=out_shape,
        scratch_shapes=[pltpu.VMEM((tm, d), F32)], compiler_params=_cp("arbitrary", "arbitrary"))(*args)


def _wgrad(a, b, k_rows, name):
    m, n = a.shape[1], b.shape[1]
    tm = _pick(m, (1408, 1024, 512, 384, 256, 128))
    tn = _pick(n, (3072, 2816, 2560, 1408, 1024, 768, 512, 384, 256, 128))
    tk = _pick(k_rows, (1280, 1024, 256))
    nk = k_rows // tk

    def body(a_ref, b_ref, o_ref, acc):
        k = pl.program_id(2)

        @pl.when(k == 0)
        def _():
            acc[...] = jnp.zeros_like(acc)

        acc[...] += _dot_tn(a_ref[...], b_ref[...])

        @pl.when(k == nk - 1)
        def _():
            o_ref[...] = _bf(acc[...])

    return pl.pallas_call(
        body, name=name, grid=(m // tm, n // tn, nk),
        in_specs=[pl.BlockSpec((tk, tm), lambda i, j, k: (k, i)), pl.BlockSpec((tk, tn), lambda i, j, k: (k, j))],
        out_specs=pl.BlockSpec((tm, tn), lambda i, j, k: (i, j)), out_shape=SDS((m, n), BF16),
        scratch_shapes=[pltpu.VMEM((tm, tn), F32)],
        compiler_params=_cp("arbitrary", "arbitrary", "arbitrary"))(a, b)


def _lane_put(col, lane_idx):
    lane = lax.broadcasted_iota(jnp.int32, (1, LANES), 1)
    return jnp.where(lane == lane_idx, col, 0.0)


def _mixer_bwd(dh1, out, hf, hb, z_main, pm, ps, hg, lng, lnb, w_s, b_st, wbm, wbs, wout, mx2, t_rows, nh):
    d = dh1.shape[1]
    ng, sc = w_s.shape[0], w_s.shape[1]
    dh, gd = d // nh, d // ng
    tb = _pick(t_rows, (256,))

    def body(dh1_ref, out_ref, hf_ref, hb_ref, zo, zu, zv, zgm, zgg, pm_ref, ps_ref, hg_ref, lng_ref, lnb_ref, ws_ref, bs_ref,
             wbm_ref, wbs_ref, wo_ref, mx2_ref, dz_ref, dhs_ref, dout_ref, dpm_ref, dps_ref, st_ref, dws_ref, dbs_ref):
        i = pl.program_id(0)

        @pl.when(i == 0)
        def _():
            st_ref[...] = jnp.zeros_like(st_ref)
            dws_ref[...] = jnp.zeros_like(dws_ref)
            dbs_ref[...] = jnp.zeros_like(dbs_ref)

        dh1v = dh1_ref[...]
        doutb = _bf(dh1v * mx2_ref[...])
        dout_ref[...] = doutb
        d_mx2 = jnp.sum(dh1v * out_ref[...].astype(F32), axis=0, keepdims=True)
        dy = _dot_nt(doutb, wo_ref[...])
        sgm, sgg = _sigmoid(zgm[...].astype(F32)), _sigmoid(zgg[...].astype(F32))
        dpmb, dpsb = _bf(dy * sgm), _bf(dy * sgg)
        dpm_ref[...] = dpmb
        dps_ref[...] = dpsb
        dz_ref[:, 3 * d:4 * d] = _bf(dy * pm_ref[...].astype(F32) * sgm * (1.0 - sgm))
        dz_ref[:, 4 * d:5 * d] = _bf(dy * ps_ref[...].astype(F32) * sgg * (1.0 - sgg))
        dym = _dot_nt(dpmb, wbm_ref[...])
        dys = _dot_nt(dpsb, wbs_ref[...])
        hs = hf_ref[...].astype(F32) + hb_ref[...].astype(F32)
        hn, scales = _head_rms(hs, nh, dh)
        so = _sigmoid(zo[...].astype(F32))
        dz_ref[:, 0:d] = _bf(dym * (hn * hg_ref[...]) * so * (1.0 - so))
        dhmn = dym * so
        d_hg = jnp.sum(dhmn * hn, axis=0, keepdims=True)
        dhn = dhmn * hg_ref[...]
        for h in range(nh):
            sl = slice(h * dh, (h + 1) * dh)
            dhs_ref[:, sl] = _bf(scales[h] * (dhn[:, sl] - hn[:, sl] * jnp.mean(dhn[:, sl] * hn[:, sl], axis=-1, keepdims=True)))
        zuv, zvv = zu[...].astype(F32), zv[...].astype(F32)
        u, du_dz = _gelu_and_grad(zuv)
        vg, dvg_dz = _gelu_and_grad(zvv)
        vhat, rstd = _layer_norm(vg)
        vnb = _bf(vhat * lng_ref[...] + lnb_ref[...])
        mixed = _sgu_mix(vnb, ws_ref, bs_ref, tb, ng, gd, sc)
        dz_ref[:, d:2 * d] = _bf(dys * mixed * du_dz)
        dmix = dys * u
        rows = []
        dbs = jnp.zeros((sc, LANES), F32)
        for ch in range(tb // sc):
            cols = []
            for g in range(ng):
                dm = dmix[ch * sc:(ch + 1) * sc, g * gd:(g + 1) * gd]
                dmb = _bf(dm)
                dws_ref[g] += _dot_nt(dmb, vnb[ch * sc:(ch + 1) * sc, g * gd:(g + 1) * gd])
                dbs = dbs + _lane_put(jnp.sum(dm, axis=1, keepdims=True), g)
                cols.append(_dot_tn(_bf(ws_ref[g]), dmb))
            rows.append(jnp.concatenate(cols, axis=1))
        dbs_ref[...] += dbs
        dvn = jnp.concatenate(rows, axis=0)
        d_lng = jnp.sum(dvn * vhat, axis=0, keepdims=True)
        d_lnb = jnp.sum(dvn, axis=0, keepdims=True)
        dvh = dvn * lng_ref[...]
        dvg = rstd * (dvh - jnp.mean(dvh, axis=-1, keepdims=True) - vhat * jnp.mean(dvh * vhat, axis=-1, keepdims=True))
        dz_ref[:, 2 * d:3 * d] = _bf(dvg * dvg_dz)
        st_ref[...] += jnp.concatenate([d_mx2, d_hg, d_lng, d_lnb, jnp.zeros((4, d), F32)], axis=0)

    def tok(col):
        return pl.BlockSpec((tb, d), lambda i: (i, col))

    def full(shape):
        return pl.BlockSpec(shape, lambda i: (0,) * len(shape))

    return pl.pallas_call(
        body, name="mixer_bwd", grid=(t_rows // tb,),
        in_specs=[tok(0), tok(0), tok(0), tok(0), tok(3), tok(4), tok(5), tok(6), tok(7), tok(0), tok(0), full((1, d)),
                  full((1, d)), full((1, d)), full((ng, sc, sc)), full((sc, LANES)), full((d, d)), full((d, d)), full((d, d)),
                  full((1, d))],
        out_specs=[pl.BlockSpec((tb, 5 * d), lambda i: (i, 0)), tok(0), tok(0), tok(0), tok(0), full((8, d)), full((ng, sc, sc)),
                   full((sc, LANES))],
        out_shape=[SDS((t_rows, 5 * d), BF16)] + [SDS((t_rows, d), BF16)] * 4 + [SDS((8, d), F32), SDS((ng, sc, sc), F32),
                                                                                SDS((sc, LANES), F32)],
        compiler_params=_cp("arbitrary"))(dh1, out, hf, hb, z_main, z_main, z_main, z_main, z_main, pm, ps, hg, lng, lnb, w_s,
                                          b_st, wbm, wbs, wout, mx2)


def _mlstm_bwd(qk, z_main, zg, bias, dhs, hf, hb, states_f, states_b, nh, t_rows):
    s_rows = qk.shape[0]
    md = qk.shape[1] // 2
    dh = md // nh
    nc = s_rows // LCH
    nx = t_rows // LCH
    ln = LCH

    def chunk_f(i):
        return jnp.where(i == nc - 1, nc - 1, nc - 2 - i)

    def chunk_b(i):
        return jnp.where(i == nc - 1, nc - 1, i)

    def body(qf, kf, vf, gf, dhf, hsf, cf, nf, mf_, qb, kb, vb, gb, dhb, hsb, cb, nb, mb_, bias_ref, dqkvf_ref, dgf_ref, dqkvb_ref,
             dgb_ref, dc_sc, dn_sc):
        i = pl.program_id(0)
        is_ctx = i == nc - 1

        @pl.when(i == 0)
        def _():
            dc_sc[...] = jnp.zeros_like(dc_sc)
            dn_sc[...] = jnp.zeros_like(dn_sc)

        sides = ((qf, kf, vf, gf, dhf, hsf, cf, nf, mf_, dqkvf_ref, dgf_ref), (qb, kb, vb, gb, dhb, hsb, cb, nb, mb_, dqkvb_ref, dgb_ref))
        gates = [_chunk_gates(s[3][...], bias_ref[...], dr == 1) for dr, s in enumerate(sides)]
        units = []
        for dr, (q_ref, k_ref, v_ref, _, dh_ref, hs_ref, c_ref, n_ref, m_ref, dqkv_ref, _) in enumerate(sides):
            gz, b_all, b_t, g_t, g_all, mask, _ = gates[dr]
            for h in range(nh):
                ci, cfl = 2 * dr * nh + h, (2 * dr + 1) * nh + h
                sl = slice(h * dh, (h + 1) * dh)
                u = dict(dr=dr, h=h, sl=sl, ci=ci, cfl=cfl, dqkv_ref=dqkv_ref, q=q_ref[:, sl], k=k_ref[:, sl], v=v_ref[:, sl],
                         dhv=jnp.where(is_ctx, 0.0, dh_ref[:, sl].astype(F32)), hs=hs_ref[:, sl].astype(F32),
                         c_in=c_ref[sl, :], n_in=n_ref[0:1, sl], m_in=m_ref[h, 0:1, 0:1],
                         b_col=b_all[:, cfl:cfl + 1], i_col=gz[:, ci:ci + 1], g=g_all[:, cfl:cfl + 1],
                         dc_new=dc_sc[dr, h], dn_new=dn_sc[dr, h, 0:1, :])
                u["qf32"], u["kf32"] = u["q"].astype(F32), u["k"].astype(F32)
                u["w"], u["w_int"], u["m_row"] = _head_weights(u["b_col"], b_t[cfl:cfl + 1, :], g_t[ci:ci + 1, :], u["m_in"], mask)
                u["qk"] = _dot_nt(u["q"], u["k"])
                units.append(u)
        for u in units:
            s_mat = u["qk"] * u["w"]
            u["s_mat"], u["sb"], u["cb16"], u["dcb"] = s_mat, _bf(s_mat), _bf(u["c_in"]), _bf(u["dc_new"])
            den = jnp.sum(s_mat, axis=1, keepdims=True) + u["w_int"] * jnp.sum(u["qf32"] * u["n_in"], axis=1, keepdims=True)
            e_m = jnp.exp(-u["m_row"])
            dnm = jnp.maximum(jnp.abs(den), e_m)
            hdh = jnp.sum(u["hs"] * u["dhv"], axis=1, keepdims=True)
            u["dden"] = jnp.where(jnp.abs(den) > e_m, -(hdh / dnm) * jnp.sign(den), 0.0)
            u["dnum_b"] = _bf(u["dhv"] / dnm)
            u["a_old"], u["coef"], _ = _head_state_coeffs(u["g"], u["b_col"], u["i_col"], u["m_in"])
            u["dsm"] = _dot_nt(u["dnum_b"], u["v"])
            u["qct"] = _dot_nt(u["dnum_b"], u["cb16"])
            u["vdc"] = _dot_nt(u["v"], u["dcb"])
        for u in units:
            ds = u["dsm"] + u["dden"]
            u["pb"] = _bf(u["w"] * ds)
            u["gmat"] = u["s_mat"] * ds
            u["dv1"] = _dot_tn(u["sb"], u["dnum_b"])
            u["dv2"] = _dot(_bf(u["kf32"] * u["coef"]), u["dcb"])
            u["dcu"] = _dot_tn(_bf(u["qf32"] * u["w_int"]), u["dnum_b"])
        for u in units:
            u["dq1"] = _dot(u["pb"], u["k"])
            u["dk1"] = _dot_tn(u["pb"], u["q"])
        acc = [dict(x1=jnp.zeros((ln, LANES), F32), x2=jnp.zeros((ln, LANES), F32), dig=jnp.zeros((ln, LANES), F32),
                    e_row=jnp.zeros((1, LANES), F32)) for _ in range(2)]
        for u in units:
            dr, h, sl, a = u["dr"], u["h"], u["sl"], acc[u["dr"]]
            dq_inter = u["w_int"] * (u["qct"] + u["dden"] * u["n_in"])
            dk_state = u["coef"] * (u["vdc"] + u["dn_new"])
            u["dqkv_ref"][:, sl] = _bf(u["dq1"] + dq_inter)
            u["dqkv_ref"][:, md + h * dh:md + (h + 1) * dh] = _bf(u["dk1"] + dk_state)
            u["dqkv_ref"][:, 2 * md + h * dh:2 * md + (h + 1) * dh] = _bf(u["dv1"] + u["dv2"])
            row_intra = jnp.sum(u["gmat"], axis=1, keepdims=True)
            col_intra = jnp.sum(u["gmat"].T, axis=1, keepdims=True)
            row_inter = jnp.sum(u["qf32"] * dq_inter, axis=1, keepdims=True)
            col_inter = jnp.sum(u["kf32"] * dk_state, axis=1, keepdims=True)
            e_old = u["a_old"] * (jnp.sum(jnp.sum(u["dc_new"] * u["c_in"], axis=1, keepdims=True), axis=0, keepdims=True)
                                  + jnp.sum(u["dn_new"] * u["n_in"], axis=1, keepdims=True))
            a["x1"] = a["x1"] + _lane_put(row_intra - col_intra + row_inter, u["cfl"])
            a["x2"] = a["x2"] + _lane_put(col_inter, u["cfl"])
            a["e_row"] = a["e_row"] + _lane_put(e_old, u["cfl"])
            a["dig"] = a["dig"] + _lane_put(col_intra + col_inter, u["ci"])
            dc_sc[dr, h] = u["a_old"] * u["dc_new"] + u["dcu"]
            dn_sc[dr, h] = jnp.broadcast_to(
                u["a_old"] * u["dn_new"] + jnp.sum(u["qf32"] * (u["w_int"] * u["dden"]), axis=0, keepdims=True), (8, dh))
        for dr, s in enumerate(sides):
            gz, mfl, a = gates[dr][0], gates[dr][6], acc[dr]
            dlogf = _mask_dot_t(mfl, a["x1"]) + _mask_dot(mfl, a["x2"]) - a["x2"] + a["e_row"]
            s[10][...] = a["dig"] + dlogf / (1.0 + jnp.exp(gz))

    def tok(cfn, col):
        return pl.BlockSpec((ln, md), lambda i: (cfn(i), col))

    def dht(cfn):
        return pl.BlockSpec((ln, md), lambda i: (jnp.minimum(cfn(i), nx - 1), 0))

    def gat(cfn):
        return pl.BlockSpec((ln, LANES), lambda i: (cfn(i), 0))

    def st(cfn, shape):
        return pl.BlockSpec((None,) + shape, lambda i: (cfn(i),) + (0,) * len(shape))

    st_shapes = ((nh * dh, dh), (8, md), (nh, 8, LANES))

    def side(cfn):
        return [tok(cfn, 0), tok(cfn, 1), tok(cfn, 2), gat(cfn), dht(cfn), tok(cfn, 0)] + [st(cfn, s) for s in st_shapes]

    def outs(cfn):
        return [pl.BlockSpec((ln, 3 * md), lambda i: (cfn(i), 0)), gat(cfn)]

    return pl.pallas_call(
        body, name="mlstm_bwd", grid=(nc,),
        in_specs=side(chunk_f) + side(chunk_b) + [pl.BlockSpec((1, LANES), lambda i: (0, 0))],
        out_specs=outs(chunk_f) + outs(chunk_b),
        out_shape=[SDS((s_rows, 3 * md), BF16), SDS((s_rows, LANES), F32)] * 2,
        scratch_shapes=[pltpu.VMEM((2, nh, dh, dh), F32), pltpu.VMEM((2, nh, 8, dh), F32)],
        compiler_params=_cp("arbitrary"))(qk, qk, z_main, zg, dhs, hf, *states_f, qk, qk, z_main, zg, dhs, hb, *states_b, bias)


def _qkv_conv_bwd(dqkv_f, dqkv_b, z_main, conv_w, t_rows, md, qscale):
    s_rows = z_main.shape[0]
    tb = _pick(s_rows, (1280, 1024, 256))
    cb = _pick(md, (512, 256, 128))
    ni, nj, ncq = s_rows // tb, 3 * md // cb, 2 * md // cb
    nb8 = tb // 8
    n_ext = tb + 16

    def body(fm, fp, fn, bm, bp, bn, zm, zp, zn, w_ref, dz_ref, gw_ref):
        j, i = pl.program_id(0), pl.program_id(1)

        @pl.when(j < ncq)
        def _():
            z = jnp.concatenate([zp[...], zm[...], zn[...]], axis=0).astype(F32)
            dqk = (jnp.concatenate([fp[...], fm[...], fn[...]], axis=0).astype(F32)
                   + jnp.concatenate([bp[...], bm[...], bn[...]], axis=0).astype(F32)) * jnp.where(j * cb < md, qscale, 1.0)
            row = i * tb - 8 + lax.broadcasted_iota(jnp.int32, (n_ext, 1), 0)
            prev_ok, next_ok = _seg_masks(row, t_rows, s_rows)
            zprev = jnp.where(prev_ok, pltpu.roll(z, 1, 0), 0.0)
            znext = jnp.where(next_ok, pltpu.roll(z, n_ext - 1, 0), 0.0)
            pre = w_ref[0:1, :] * zprev + w_ref[1:2, :] * z + w_ref[2:3, :] * znext
            sg = _sigmoid(pre)
            dpre = dqk * (sg * (1.0 + pre * (1.0 - sg)))
            dz = (w_ref[1:2, :] * dpre + w_ref[0:1, :] * jnp.where(next_ok, pltpu.roll(dpre, n_ext - 1, 0), 0.0)
                  + w_ref[2:3, :] * jnp.where(prev_ok, pltpu.roll(dpre, 1, 0), 0.0))
            dz_ref[...] = _bf(dz[8:8 + tb])
            dm = dpre[8:8 + tb]

            @pl.when(i == 0)
            def _():
                gw_ref[...] = jnp.zeros_like(gw_ref)

            gw_ref[...] += jnp.concatenate(
                [jnp.sum(dm * zprev[8:8 + tb], axis=0, keepdims=True), jnp.sum(dm * z[8:8 + tb], axis=0, keepdims=True),
                 jnp.sum(dm * znext[8:8 + tb], axis=0, keepdims=True), jnp.zeros((5, cb), F32)], axis=0)

        @pl.when(j >= ncq)
        def _():
            dz_ref[...] = _bf(fm[...].astype(F32) + bm[...].astype(F32))

    def halo(clampj):
        def cj(j):
            return jnp.minimum(j, ncq - 1) if clampj else j
        return [pl.BlockSpec((tb, cb), lambda j, i: (i, cj(j))),
                pl.BlockSpec((8, cb), lambda j, i: (jnp.maximum(i * nb8 - 1, 0), cj(j))),
                pl.BlockSpec((8, cb), lambda j, i: (jnp.minimum((i + 1) * nb8, s_rows // 8 - 1), cj(j)))]

    return pl.pallas_call(
        body, name="qkv_conv_bwd", grid=(nj, ni),
        in_specs=halo(False) + halo(False) + halo(True) + [pl.BlockSpec((8, cb), lambda j, i: (0, jnp.minimum(j, ncq - 1)))],
        out_specs=[pl.BlockSpec((tb, cb), lambda j, i: (i, j)), pl.BlockSpec((8, cb), lambda j, i: (0, jnp.minimum(j, ncq - 1)))],
        out_shape=[SDS((s_rows, 3 * md), BF16), SDS((8, 2 * md), F32)],
        compiler_params=_cp("arbitrary", "arbitrary"))(dqkv_f, dqkv_f, dqkv_f, dqkv_b, dqkv_b, dqkv_b, z_main, z_main, z_main, conv_w)


def _gate_grad_sum(dg_f, dg_b):
    s_rows = dg_f.shape[0]
    tb = _pick(s_rows, (1280, 1024, 256))

    def body(a_ref, b_ref, o_ref, st_ref):
        @pl.when(pl.program_id(0) == 0)
        def _():
            st_ref[...] = jnp.zeros_like(st_ref)

        s = a_ref[...] + b_ref[...]
        o_ref[...] = _bf(s)
        st_ref[...] += jnp.concatenate([jnp.sum(s, axis=0, keepdims=True), jnp.zeros((7, LANES), F32)], axis=0)

    blk = pl.BlockSpec((tb, LANES), lambda i: (i, 0))
    return pl.pallas_call(
        body, name="gate_grad_sum", grid=(s_rows // tb,), in_specs=[blk, blk],
        out_specs=[blk, pl.BlockSpec((8, LANES), lambda i: (0, 0))],
        out_shape=[SDS((s_rows, LANES), BF16), SDS((8, LANES), F32)], compiler_params=_cp("arbitrary"))(dg_f, dg_b)


def _mod_grads(silu_slots, dmx_sh, dmx_slots, dmc_tot, dmc_sh, silu_cctx, c_ctx, w_mod_c):
    d = silu_slots.shape[1]
    ncol, n6 = dmx_sh.shape[1], dmx_slots.shape[1]

    def body(ss_ref, dsh_ref, dsl_ref, dct_ref, dcs_ref, sc_ref, c_ref, w_ref, gw_ref, gb_ref, gc_ref):
        a = jnp.concatenate([ss_ref[...], sc_ref[...], jnp.zeros((7, d), F32)], axis=0)
        b = jnp.concatenate([dsh_ref[...], dcs_ref[...], jnp.zeros((7, ncol), F32)], axis=0)
        gw_ref[0] = lax.dot_general(a, b, (((0,), (0,)), ((), ())), preferred_element_type=F32, precision=HI)
        dct = dct_ref[...]
        gb_ref[...] = jnp.sum(dsl_ref[...], axis=0, keepdims=True) + jnp.concatenate(
            [dct, jnp.zeros((1, n6 - dct.shape[1]), F32)], axis=1)
        t = _dot_nt(_bf(jnp.broadcast_to(dct, (8, dct.shape[1]))), w_ref[...])
        cv = c_ref[...]
        s = _sigmoid(cv)
        gc_ref[...] = t[0:1, :] * (s * (1.0 + cv * (1.0 - s)))

    return pl.pallas_call(body, name="mod_grads", out_shape=[SDS((1, d, ncol), F32), SDS((1, n6), F32), SDS((1, d), F32)],
                          compiler_params=_cp())(silu_slots, dmx_sh, dmx_slots, dmc_tot, dmc_sh, silu_cctx, c_ctx, w_mod_c)


def _slot_sum(slots):
    ns, r = slots.shape[0], slots.shape[1]
    tb = _pick(r, (1024, 512, 256, 128, 64, 32, 16, 8))

    def body(s_ref, o_ref):
        acc = s_ref[0]
        for k in range(1, ns):
            acc = acc + s_ref[k]
        o_ref[...] = acc

    return pl.pallas_call(
        body, name="slot_sum", grid=(r // tb,), in_specs=[pl.BlockSpec((ns, tb, LANES), lambda i: (0, i, 0))],
        out_specs=pl.BlockSpec((tb, LANES), lambda i: (i, 0)), out_shape=SDS((r, LANES), F32),
        compiler_params=_cp("arbitrary"))(slots)


def _adamw(w, gslots, m, v, name):
    lead = ((None,), (0,)) if w.ndim == 3 else ((), ())
    r, cdim = w.shape[-2:]
    ns, rg = gslots.shape[0], gslots.shape[1]
    tb = r if (rg != r or r % 8) else _pick(r, (128, 64, 32, 16, 8))
    bc1, bc2 = 1.0 - ADAM_B1 ** ADAM_STEP, 1.0 - ADAM_B2 ** ADAM_STEP

    def body(w_ref, g_ref, m_ref, v_ref, go_ref, d_ref, mo_ref, vo_ref):
        g = g_ref[0, 0:tb, :].astype(F32)
        for k in range(1, ns):
            g = g + g_ref[k, 0:tb, :].astype(F32)
        mn = ADAM_B1 * m_ref[...] + (1.0 - ADAM_B1) * g
        vn = ADAM_B2 * v_ref[...] + (1.0 - ADAM_B2) * (g * g)
        go_ref[...] = g
        mo_ref[...] = mn
        vo_ref[...] = vn
        d_ref[...] = -ADAM_LR * ((mn / bc1) / (jnp.sqrt(vn / bc2) + ADAM_EPS) + ADAM_WD * w_ref[...])

    blk = pl.BlockSpec(lead[0] + (tb, cdim), lambda i: lead[1] + (i, 0))
    gblk = pl.BlockSpec((ns, tb if rg == r else rg, cdim), lambda i: (0, i, 0))
    return pl.pallas_call(
        body, name=name, grid=(r // tb,), in_specs=[blk, gblk, blk, blk],
        out_specs=[blk] * 4, out_shape=[SDS(w.shape, F32)] * 4, compiler_params=_cp("arbitrary"))(w, gslots, m, v)


def _pack(parts, row_mult):
    flat = jnp.concatenate([p.reshape(-1) for p in parts])
    n = flat.shape[0]
    rows = -(-n // LANES)
    rows = -(-rows // row_mult) * row_mult
    return jnp.pad(flat, (0, rows * LANES - n)).reshape(rows, LANES)


def _unpack(buf, shapes):
    flat = buf.reshape(-1)
    out, off = [], 0
    for s in shapes:
        n = math.prod(s)
        out.append(flat[off:off + n].reshape(s))
        off += n
    return out


def _pad_cols(a, width):
    return jnp.pad(a, ((0, 0), (0, width - a.shape[1])))


def _pad_lanes(a):
    return _pad_cols(a, LANES)


def _up128(n):
    return -(-n // LANES) * LANES


def kernel(x, c, ctx, c_ctx, w_mod, b_mod, norm1_g, w_in, b_gate, conv_qk, head_norm_g, sgu_ln_g, sgu_ln_b, w_s, b_s, w_branch_mlstm, w_branch_sgu, w_out, norm2_g, w_up, w_ffn_conv, w_down, final_g, loss_target, m_c_ctx, m_w_mod, m_b_mod, m_norm1_g, m_w_in, m_b_gate, m_conv_qk, m_head_norm_g, m_sgu_ln_g, m_sgu_ln_b, m_w_s, m_b_s, m_w_branch_mlstm, m_w_branch_sgu, m_w_out, m_norm2_g, m_w_up, m_w_ffn_conv, m_w_down, m_final_g, v_c_ctx, v_w_mod, v_b_mod, v_norm1_g, v_w_in, v_b_gate, v_conv_qk, v_head_norm_g, v_sgu_ln_g, v_sgu_ln_b, v_w_s, v_b_s, v_w_branch_mlstm, v_w_branch_sgu, v_w_out, v_norm2_g, v_w_up, v_w_ffn_conv, v_w_down, v_final_g):
    t, d = x.shape[1], x.shape[2]
    n_ctx = ctx.shape[1]
    s_rows = t + n_ctx
    nh = b_gate.shape[1] // 4
    md = head_norm_g.shape[1]
    dh = md // nh
    ng, sc = w_s.shape[1], w_s.shape[2]
    dff = w_down.shape[1] * N_DEV
    n_in = w_in.shape[2] * N_DEV
    assert md == d and sgu_ln_g.shape[1] == d and n_ctx == LCH and t % LCH == 0 and t % (8 * GRID_W) == 0
    assert n_in == 8 * d + 4 * nh and 4 * nh <= LANES
    me = 4 * lax.axis_index("x") + 2 * lax.axis_index("y") + lax.axis_index("c")

    n_mod, n_insh, n_upsh = w_mod.shape[2], w_in.shape[2], w_up.shape[2]
    p_mod, p_in, p_up = _up128(n_mod), _up128(n_insh), _up128(n_upsh)
    nq, nf = conv_qk.shape[2], w_ffn_conv.shape[3]
    ffn9 = w_ffn_conv[0].reshape(9, nf)
    colpack = jnp.concatenate([_pad_cols(_bf(w_mod[0]), p_mod), _pad_cols(_bf(w_in[0]), p_in)], axis=1)
    convpack = jnp.concatenate([jnp.pad(conv_qk[0], ((0, 13), (0, 0))), jnp.pad(ffn9, ((0, 7), (0, 0)))], axis=1)
    g_col, g_conv = _allgather([colpack, convpack])
    w_mod_f, w_main, w_gate = _assemble_cols(
        g_col, [(0, n_mod, [(0, 0, N_DEV * n_mod, 0)]),
                (p_mod, n_insh, [(1, 0, 3 * md, 0), (2, 3 * md, 4 * nh, 0), (1, 3 * md + 4 * nh, 5 * d, 3 * md)])],
        [N_MOD * d, 8 * d, LANES], "assemble_weights")
    convw, wconv9 = _assemble_cols(g_conv, [(0, nq, [(0, 0, N_DEV * nq, 0)]), (nq, nf, [(1, 0, N_DEV * nf, 0)])],
                                   [N_DEV * nq, N_DEV * nf], "assemble_conv_weights")
    zero = jnp.minimum(jnp.abs(g_conv[0, 0, 0]), 0.0)
    late_w = [_pad_cols(_bf(w_up[0] + zero), p_up), _bf(w_branch_mlstm[0]), _bf(w_branch_sgu[0]), _bf(w_out[0]), _bf(w_down[0])]
    late_state, late_tok = _exchange_start(late_w, False, "late_weights_start")

    cvec = jnp.concatenate([c, c_ctx[None], jnp.zeros((6, d), F32)], axis=0) + late_tok[0:1, 0:1]
    silu_v, mod = _modulation(cvec, w_mod_f, b_mod)
    mx = [mod[0:1, k * d:(k + 1) * d] for k in range(N_MOD)]
    mc = [mod[1:2, k * d:(k + 1) * d] for k in range(2)]
    x2, ctx2 = x[0], ctx[0]
    in_x = _norm_mod_proj(x2, norm1_g, jnp.concatenate([mx[0], mx[1]], axis=0), w_main, w_gate, s_rows, 0, None, "in_proj")
    hn, z_main, zg = _norm_mod_proj(ctx2, norm1_g, jnp.concatenate([mc[0], mc[1]], axis=0), w_main, w_gate, s_rows, t, in_x,
                                    "in_proj_ctx")
    qscale = dh ** -0.5
    qk = _qk_conv(z_main, convw, t, md, qscale)
    bias = _pad_lanes(b_gate)
    fwd = _mlstm_fwd(qk, z_main, zg, bias, nh)
    hf, hb, states_f, states_b = fwd[0], fwd[1], fwd[2:5], fwd[5:8]
    g_up, g_bm, g_bs, g_out, g_down = _exchange_wait(late_state, fwd[4], "late_weights_wait")
    (w_up_f,) = _assemble_cols(g_up, [(0, n_upsh, [(0, 0, 2 * dff, 0)])], [2 * dff], "assemble_w_up")
    wbm_f, wbs_f, wout_f = (g.reshape(d, d) for g in (g_bm, g_bs, g_out))
    w_down_f = g_down.reshape(dff, d)
    b_st = _pad_lanes(b_s[0].T)
    h1, ym, ys, pm, ps, y, out = _mixer_fwd(hf, hb, z_main, x2, head_norm_g, sgu_ln_g, sgu_ln_b, w_s[0], b_st, wbm_f, wbs_f,
                                            wout_f, mx[2], t, nh)
    hn2, ab = _norm_mod_proj(h1, norm2_g, jnp.concatenate([mx[3], mx[4]], axis=0), w_up_f, None, t, 0, None, "up_proj")
    aconv, f, dh2, dffn, st_tail = _ffn_tail(ab, wconv9, w_down_f, h1, mx[5], final_g[None], loss_target[0], dff)

    db, dac = _ffn_bwd_gate(dffn, w_down_f, aconv, ab, dff)
    da, g_wconv9 = _ffn_conv_bwd(dac, ab, wconv9, dff)
    g_wdown = _wgrad(f, dffn, t, "wgrad_down")
    gwup_slots = _scatter_cols([_wgrad(hn2, da, t, "wgrad_up_a"), _wgrad(hn2, db, t, "wgrad_up_b")],
                               [(0, 0, dff, 0), (1, dff, dff, 0)], n_upsh, "scatter_grad_w_up")
    dh1, st_n2 = _proj_norm_bwd([(da, 0, w_up_f, 0, dff, dff), (db, 0, w_up_f, dff, dff, dff)], h1, 0, norm2_g, mx[4], dh2, t,
                                "up_proj_bwd", (512, 256))
    dz_rest, dhs, dout, dpm, dps, st_mix, g_ws, g_bst = _mixer_bwd(dh1, out, hf, hb, z_main, pm, ps, head_norm_g, sgu_ln_g,
                                                                    sgu_ln_b, w_s[0], b_st, wbm_f, wbs_f, wout_f, mx[2], t, nh)
    g_wout = _wgrad(y, dout, t, "wgrad_out")
    g_wbm = _wgrad(ym, dpm, t, "wgrad_branch_mlstm")
    g_wbs = _wgrad(ys, dps, t, "wgrad_branch_sgu")
    ex_a = [gwup_slots, g_wdown.reshape(N_DEV, dff // N_DEV, d), g_wbm.reshape(N_DEV, d // N_DEV, d),
            g_wbs.reshape(N_DEV, d // N_DEV, d), g_wout.reshape(N_DEV, d // N_DEV, d)]
    ex_a_state, ex_a_tok = _exchange_start(ex_a, True, "grad_exchange_a_start")
    dqkv_f, dg_f, dqkv_b, dg_b = _mlstm_bwd(qk, z_main, zg, bias + ex_a_tok[0:1, :], dhs, hf, hb, states_f, states_b, nh, t)
    dz_qkv, g_convqk = _qkv_conv_bwd(dqkv_f, dqkv_b, z_main, convw, t, md, qscale)
    dz_g, st_gate = _gate_grad_sum(dg_f, dg_b)
    gwin_slots = _scatter_cols(
        [_wgrad(hn, dz_qkv, s_rows, "wgrad_in_qkv"), _wgrad(hn, dz_g, s_rows, "wgrad_in_gate"), _wgrad(hn, dz_rest, t, "wgrad_in_rest")],
        [(0, 0, 3 * md, 0), (1, 3 * md, 4 * nh, 0), (2, 3 * md + 4 * nh, 5 * d, 0)], n_insh, "scatter_grad_w_in")
    gcq_slots = _scatter_cols([g_convqk], [(0, 0, 2 * md, 0)], nq, "scatter_grad_conv_qk")
    gcf_slots = _scatter_cols([g_wconv9], [(0, 0, dff, 0)], nf, "scatter_grad_ffn_conv")
    ex_b_state, ex_b_tok = _exchange_start([gwin_slots, gcq_slots, gcf_slots], True, "grad_exchange_b_start")
    tk = _pick(md, (1024, 512, 256))
    grad_x, st_n1x = _proj_norm_bwd(
        [(dz_qkv, 0, w_main, 0, 3 * md, tk), (dz_rest, 0, w_main, 3 * md, 5 * d, tk), (dz_g, 0, w_gate, 0, LANES, LANES)],
        x2, 0, norm1_g, mx[1] + ex_b_tok[0:1, 0:1], dh1, t, "in_proj_bwd")
    (st_n1c,) = _proj_norm_bwd([(dz_qkv, t, w_main, 0, 3 * md, tk), (dz_g, t, w_gate, 0, LANES, LANES)],
                               ctx2, 0, norm1_g, mc[1] + ex_b_tok[0:1, 0:1], None, n_ctx, "in_proj_bwd_ctx")

    rx_a = _exchange_wait(ex_a_state, st_n1c, "grad_exchange_a_wait")
    rx_b = _exchange_wait(ex_b_state, st_n1c, "grad_exchange_b_wait")
    recv = [rx_b[0], rx_a[0], rx_a[2], rx_a[3], rx_a[4], rx_a[1], rx_b[1], rx_b[2]]
    small_parts = [st_n1x[1], st_n1x[2], st_mix[0], st_n2[1], st_n2[2], st_tail[1],
                   st_n1c[1], st_n1c[2],
                   silu_v[0], st_n1x[0] + st_n1c[0], st_gate[0], st_mix[1], st_mix[2], st_mix[3],
                   g_ws.reshape(-1), g_bst[:, :ng].T.reshape(-1), st_n2[0], st_tail[0]]
    small_state, _ = _exchange_start([_pack(small_parts, 8)], False, "small_exchange_start")

    shard_w = (w_in, w_up, w_branch_mlstm, w_branch_sgu, w_out, w_down, conv_qk)
    shard_m = (m_w_in, m_w_up, m_w_branch_mlstm, m_w_branch_sgu, m_w_out, m_w_down, m_conv_qk)
    shard_v = (v_w_in, v_w_up, v_w_branch_mlstm, v_w_branch_sgu, v_w_out, v_w_down, v_conv_qk)
    shard_names = ("w_in", "w_up", "w_branch_mlstm", "w_branch_sgu", "w_out", "w_down", "conv_qk")
    shard_out = [_adamw(wa, recv[k], ma, va, "adamw_" + nm)
                 for k, (wa, ma, va, nm) in enumerate(zip(shard_w, shard_m, shard_v, shard_names))]
    shard_out.append([b.reshape(w_ffn_conv.shape) for b in
                      _adamw(ffn9, recv[7], m_w_ffn_conv[0].reshape(9, nf), v_w_ffn_conv[0].reshape(9, nf), "adamw_w_ffn_conv")])

    (recv_small,) = _exchange_wait(small_state, shard_out[5][1], "small_exchange_wait")
    small_sum = _slot_sum(recv_small).reshape(-1)
    small_slots = recv_small.reshape(N_DEV, -1)
    o_silu, o_n1 = 8 * d, 9 * d
    ncol = N_MOD * d // N_DEV
    dmc_tot = small_sum[6 * d:8 * d][None]
    dmc_pad = jnp.concatenate([dmc_tot, jnp.zeros((1, 4 * d), F32)], axis=1)
    g_wmod, g_bmod, g_cctx = _mod_grads(
        small_slots[:, o_silu:o_silu + d], lax.dynamic_slice_in_dim(small_slots[:, :6 * d], me * ncol, ncol, axis=1),
        small_slots[:, :6 * d], dmc_tot, lax.dynamic_slice_in_dim(dmc_pad, me * ncol, ncol, axis=1), silu_v[1:2], c_ctx[None],
        w_mod_f[:, :2 * d])
    mod_out = _adamw(w_mod, g_wmod, m_w_mod, v_w_mod, "adamw_w_mod")

    def rep(cc, bm, n1, bg, hg, lg, lb, ws, bs, n2, fg):
        return [cc.reshape(-1), bm.reshape(-1), n1.reshape(-1), _pad_lanes(bg.reshape(1, -1)).reshape(-1), hg.reshape(-1),
                lg.reshape(-1), lb.reshape(-1), ws.reshape(-1), bs.reshape(-1), n2.reshape(-1), fg.reshape(-1)]

    o = o_n1
    g_rep_parts = [g_cctx, g_bmod]
    for n in (d, LANES, d, d, d, ng * sc * sc, ng * sc, d, d):
        g_rep_parts.append(small_sum[o:o + n])
        o += n
    rep_shapes = [(d,), (1, N_MOD * d), (1, d), (1, LANES), (1, d), (1, d), (1, d), (1, ng, sc, sc), (1, ng, sc), (1, d), (d,)]
    rep_out = _adamw(
        _pack(rep(c_ctx, b_mod, norm1_g, b_gate, head_norm_g, sgu_ln_g, sgu_ln_b, w_s, b_s, norm2_g, final_g), 8),
        _pack(g_rep_parts, 8)[None],
        _pack(rep(m_c_ctx, m_b_mod, m_norm1_g, m_b_gate, m_head_norm_g, m_sgu_ln_g, m_sgu_ln_b, m_w_s, m_b_s, m_norm2_g, m_final_g), 8),
        _pack(rep(v_c_ctx, v_b_mod, v_norm1_g, v_b_gate, v_head_norm_g, v_sgu_ln_g, v_sgu_ln_b, v_w_s, v_b_s, v_norm2_g, v_final_g), 8),
        "adamw_replicated")

    def assemble(k):
        r = _unpack(rep_out[k], rep_shapes)
        s = [o[k] for o in shard_out]
        return [r[0], mod_out[k], r[1], r[2], s[0], r[3][:, :4 * nh], s[6], r[4], r[5], r[6], r[7], r[8], s[2], s[3], s[4], r[9],
                s[1], s[7], s[5], r[10]]

    loss = lax.psum(st_tail[2, 0], ("x", "y", "c"))
    outs = [loss, grad_x[None]]
    for k in range(4):
        outs += assemble(k)
    return tuple(outs)
```

```python
import math

import jax
import jax.numpy as jnp
from jax import lax
from jax.experimental import pallas as pl
from jax.experimental.pallas import tpu as pltpu

F32, BF16 = jnp.float32, jnp.bfloat16
EPS = 1e-6
M_INIT = -1e30
NEG = -1e30
GRID_W = 64
LCH = 256
N_MOD = 6
N_DEV = 8
LANES = 128
ADAM_LR, ADAM_B1, ADAM_B2, ADAM_EPS, ADAM_WD, ADAM_STEP = 0.001, 0.9, 0.999, 1e-08, 0.01, 10
GELU_C = math.sqrt(2.0 / math.pi)
GELU_A = 0.044715
VMEM_LIMIT = 56 * 1024 * 1024
HI = lax.Precision.HIGHEST
SDS = jax.ShapeDtypeStruct
MESH_ID = pl.DeviceIdType.MESH


def _pick(n, cands):
    for c in cands:
        if n % c == 0:
            return c
    raise ValueError(f"no block size for {n} in {cands}")


def _cp(*sem):
    return pltpu.CompilerParams(dimension_semantics=sem if sem else None, vmem_limit_bytes=VMEM_LIMIT)


def _sigmoid(x):
    return 0.5 * jnp.tanh(0.5 * x) + 0.5


def _split3(x):
    hi = x.astype(BF16)
    r = x - hi.astype(F32)
    mid = r.astype(BF16)
    return hi, mid, (r - mid.astype(F32)).astype(BF16)


def _mask_dot(mask_b, x):
    hi, mid, lo = _split3(x)
    return (_dot(mask_b, lo) + _dot(mask_b, mid)) + _dot(mask_b, hi)


def _mask_dot_t(mask_b, x):
    hi, mid, lo = _split3(x)
    return (_dot_tn(mask_b, lo) + _dot_tn(mask_b, mid)) + _dot_tn(mask_b, hi)


def _gelu(x):
    return x * (0.5 * (1.0 + jnp.tanh(GELU_C * x * (1.0 + GELU_A * (x * x)))))


def _gelu_and_grad(x):
    x2 = x * x
    t = jnp.tanh(GELU_C * x * (1.0 + GELU_A * x2))
    half = 0.5 * (1.0 + t)
    return x * half, half + (0.5 * GELU_C) * x * (1.0 - t * t) * (1.0 + 3.0 * GELU_A * x2)


def _log_sigmoid(x):
    return jnp.minimum(x, 0.0) - jnp.log(1.0 + jnp.exp(-jnp.abs(x)))


def _dot(a, b):
    return jnp.dot(a, b, preferred_element_type=F32)


def _dot_nt(a, b):
    return lax.dot_general(a, b, (((1,), (1,)), ((), ())), preferred_element_type=F32)


def _dot_tn(a, b):
    return lax.dot_general(a, b, (((0,), (0,)), ((), ())), preferred_element_type=F32)


def _bf(x):
    return x.astype(BF16)


def _allgather(arrs):
    na = len(arrs)

    def body(*refs):
        x_refs, o_refs = refs[:na], refs[na:2 * na]
        send_sems, recv_sems, local_sems = refs[2 * na:]
        x, y, c = lax.axis_index("x"), lax.axis_index("y"), lax.axis_index("c")
        me, sibling = (x, y, c), (x, y, 1 - c)
        chips = [(1 - x, y), (x, 1 - y), (1 - x, 1 - y)]

        def copy(a, k, block, to, src=None):
            slot = o_refs[a].at[4 * block[0] + 2 * block[1] + block[2]]
            return pltpu.make_async_remote_copy(
                src_ref=slot if src is None else src, dst_ref=slot, send_sem=send_sems.at[7 * a + k],
                recv_sem=recv_sems.at[7 * a + k], device_id=to, device_id_type=MESH_ID)

        mine = [pltpu.make_async_copy(x_refs[a], o_refs[a].at[4 * x + 2 * y + c], local_sems.at[a]) for a in range(na)]
        for cp in mine:
            cp.start()
        first = []
        for a in range(na):
            first.append(copy(a, 0, me, sibling, src=x_refs[a]))
            first += [copy(a, 1 + j, me, (*chip, c), src=x_refs[a]) for j, chip in enumerate(chips)]
        for cp in first:
            cp.start()
        passed = []
        for j, chip in enumerate(chips):
            for a in range(na):
                copy(a, 1 + j, (*chip, c), me).wait_recv()
                passed.append(copy(a, 4 + j, (*chip, c), sibling))
                passed[-1].start()
        for a in range(na):
            copy(a, 0, sibling, me).wait_recv()
            for j, chip in enumerate(chips):
                copy(a, 4 + j, (*chip, 1 - c), me).wait_recv()
        for cp in first + passed:
            cp.wait_send()
        for cp in mine:
            cp.wait()

    anyspec = pl.BlockSpec(memory_space=pl.ANY)
    return pl.pallas_call(
        body, name="weights_allgather",
        out_shape=[SDS((N_DEV,) + a.shape, a.dtype) for a in arrs],
        in_specs=[anyspec] * na, out_specs=[anyspec] * na,
        scratch_shapes=[pltpu.SemaphoreType.DMA((7 * na,)), pltpu.SemaphoreType.DMA((7 * na,)), pltpu.SemaphoreType.DMA((na,))],
    )(*arrs)


_HBM_SPEC = pl.BlockSpec(memory_space=pltpu.HBM)
_SEM_SPEC = pl.BlockSpec(memory_space=pltpu.SEMAPHORE)
_EFFECT = pltpu.SideEffectType.DATAFLOW_SIDE_EFFECTING


def _peer_list(x, y, c):
    out = []
    for k in range(1, N_DEV):
        px = 1 - x if k & 4 else x
        py = 1 - y if k & 2 else y
        pc = 1 - c if k & 1 else c
        out.append(((px, py, pc), 4 * px + 2 * py + pc))
    return out


def _split_copies(src, land, send_sems, recv_sems, per_dest, receive):
    x, y, c = lax.axis_index("x"), lax.axis_index("y"), lax.axis_index("c")
    me = 4 * x + 2 * y + c
    out = []
    for k, (peer, pidx) in enumerate(_peer_list(x, y, c)):
        for a in range(len(src)):
            out.append(pltpu.make_async_remote_copy(
                src_ref=src[a].at[pidx] if per_dest else src[a], dst_ref=land[a].at[pidx if receive else me],
                send_sem=send_sems.at[7 * a + k], recv_sem=recv_sems.at[7 * a + k], device_id=peer, device_id_type=MESH_ID))
    return out


def _own_copies(src, land, own_sems, per_dest):
    me = 4 * lax.axis_index("x") + 2 * lax.axis_index("y") + lax.axis_index("c")
    return [pltpu.make_async_copy(src[a].at[me] if per_dest else src[a], land[a].at[me], own_sems.at[a]) for a in range(len(src))]


def _exchange_start(arrs, per_dest, name):
    na = len(arrs)
    land_shapes = [a.shape if per_dest else (N_DEV,) + a.shape for a in arrs]
    lands = [pltpu.with_memory_space_constraint(lax.empty(s, a.dtype), pltpu.HBM) for s, a in zip(land_shapes, arrs)]

    def body(*refs):
        src, land = refs[:na], refs[na:2 * na]
        send_sems, recv_sems, own_sems, token = refs[2 * na], refs[2 * na + 1], refs[2 * na + 2], refs[-1]
        for cp in _split_copies(src, land, send_sems, recv_sems, per_dest, False) + _own_copies(src, land, own_sems, per_dest):
            cp.start()
        token[...] = jnp.zeros_like(token)

    outs = pl.pallas_call(
        body, name=name,
        out_shape=[pltpu.SemaphoreType.DMA((7 * na,)), pltpu.SemaphoreType.DMA((7 * na,)), pltpu.SemaphoreType.DMA((na,))]
        + [pltpu.HBM(a.shape, a.dtype) for a in arrs] + [pltpu.HBM(s, a.dtype) for s, a in zip(land_shapes, arrs)]
        + [SDS((8, LANES), F32)],
        in_specs=[_HBM_SPEC] * (2 * na),
        out_specs=[_SEM_SPEC] * 3 + [_HBM_SPEC] * (2 * na) + [pl.BlockSpec(memory_space=pltpu.VMEM)],
        input_output_aliases={k: 3 + k for k in range(2 * na)},
        compiler_params=pltpu.CompilerParams(has_side_effects=_EFFECT),
    )(*[pltpu.with_memory_space_constraint(a, pltpu.HBM) for a in arrs], *lands)
    return (na, per_dest, outs[:-1]), outs[-1]


def _exchange_wait(state, after, name):
    na, per_dest, started = state

    def body(*refs):
        src, land = refs[:na], refs[na:2 * na]
        send_sems, recv_sems, own_sems = refs[2 * na], refs[2 * na + 1], refs[2 * na + 2]
        for cp in _split_copies(src, land, send_sems, recv_sems, per_dest, True):
            cp.wait_send()
            cp.wait_recv()
        for cp in _own_copies(src, land, own_sems, per_dest):
            cp.wait()

    bufs = started[3:]
    outs = pl.pallas_call(
        body, name=name,
        out_shape=[pltpu.HBM(b.shape, b.dtype) for b in bufs],
        in_specs=[_HBM_SPEC] * (2 * na) + [_SEM_SPEC] * 3 + [pl.BlockSpec(memory_space=pl.ANY)],
        out_specs=[_HBM_SPEC] * (2 * na),
        input_output_aliases={k: k for k in range(2 * na)},
        compiler_params=pltpu.CompilerParams(has_side_effects=_EFFECT),
    )(*bufs, started[0], started[1], started[2], after)
    return outs[na:]


def _col_pieces(n, segments):
    out = []
    for j in range(N_DEV):
        lo, hi = j * n, (j + 1) * n
        for (k, s0, w, c0) in segments:
            a, b = max(lo, s0), min(hi, s0 + w)
            if a < b:
                out.append((j, a - lo, b - lo, k, c0 + a - s0, c0 + b - s0))
    return out


def _assemble_cols(slots, groups, out_widths, name):
    r, p = slots.shape[1], slots.shape[2]
    tb = _pick(r, (128, 64, 32, 16, 8))
    covered = [0] * len(out_widths)
    for (_, n, segs) in groups:
        for (k, _, w, _) in segs:
            covered[k] += w

    def body(s_ref, *o_refs):
        for k, wd in enumerate(out_widths):
            if covered[k] < wd:
                o_refs[k][...] = jnp.zeros_like(o_refs[k])
        for (off, n, segs) in groups:
            for (j, a0, a1, k, d0, d1) in _col_pieces(n, segs):
                o_refs[k][:, d0:d1] = s_ref[j, :, off + a0:off + a1]

    return pl.pallas_call(
        body, name=name, grid=(r // tb,), in_specs=[pl.BlockSpec((N_DEV, tb, p), lambda i: (0, i, 0))],
        out_specs=[pl.BlockSpec((tb, w), lambda i: (i, 0)) for w in out_widths],
        out_shape=[SDS((r, w), slots.dtype) for w in out_widths], compiler_params=_cp("arbitrary"))(slots)


def _scatter_cols(pieces, segments, n, name):
    r = pieces[0].shape[0]
    tb = _pick(r, (128, 64, 32, 16, 8))

    def body(*refs):
        p_refs, o_ref = refs[:-1], refs[-1]
        for (j, a0, a1, k, d0, d1) in _col_pieces(n, segments):
            o_ref[j, :, a0:a1] = p_refs[k][:, d0:d1]

    return pl.pallas_call(
        body, name=name, grid=(r // tb,), in_specs=[pl.BlockSpec((tb, a.shape[1]), lambda i: (i, 0)) for a in pieces],
        out_specs=pl.BlockSpec((N_DEV, tb, n), lambda i: (0, i, 0)), out_shape=SDS((N_DEV, r, n), pieces[0].dtype),
        compiler_params=_cp("arbitrary"))(*pieces)


def _modulation(cvec, w_mod, b_mod):
    d, n = w_mod.shape

    def body(c_ref, w_ref, b_ref, s_ref, o_ref):
        cv = c_ref[...]
        s = cv * _sigmoid(cv)
        s_ref[...] = s
        o_ref[...] = _dot(_bf(s), w_ref[...]) + b_ref[...]

    return pl.pallas_call(body, name="modulation", out_shape=(SDS((8, d), F32), SDS((8, n), F32)),
                          compiler_params=_cp())(cvec, w_mod, b_mod)


def _norm_mod_proj(x_arr, g, shsc, w_main, w_gate, rows_total, row0, filled, name):
    m_rows, d = x_arr.shape
    n = w_main.shape[1]
    tb = _pick(m_rows, (1024, 256))
    cb = _pick(n, (2048, 1408, 1024, 768, 512, 384, 256, 128))
    gate = w_gate is not None
    nout = 3 if gate else 2
    nin = 5 if gate else 4
    rb = row0 // tb

    def body(*refs):
        x_ref, g_ref, ss_ref, wm_ref = refs[:4]
        wg_ref = refs[4] if gate else None
        outs = refs[len(refs) - 1 - nout:len(refs) - 1]
        hn_ref, z_ref = outs[0], outs[1]
        hn_sc = refs[-1]

        @pl.when(pl.program_id(1) == 0)
        def _():
            x = x_ref[...]
            r = lax.rsqrt(jnp.mean(x * x, axis=-1, keepdims=True) + EPS)
            hb = _bf((x * r * g_ref[...]) * (1.0 + ss_ref[1:2, :]) + ss_ref[0:1, :])
            hn_sc[...] = hb
            hn_ref[...] = hb
            if gate:
                outs[2][...] = _dot(hb, wg_ref[...])

        z_ref[...] = _bf(_dot(hn_sc[...], wm_ref[:, pl.ds(pl.multiple_of(pl.program_id(1) * cb, cb), cb)]))

    in_specs = [pl.BlockSpec((tb, d), lambda i, j: (i, 0)), pl.BlockSpec((1, d), lambda i, j: (0, 0)),
                pl.BlockSpec((2, d), lambda i, j: (0, 0)), _resident((d, n))]
    out_specs = [pl.BlockSpec((tb, d), lambda i, j: (rb + i, 0)), pl.BlockSpec((tb, cb), lambda i, j: (rb + i, j))]
    out_shape = [SDS((rows_total, d), BF16), SDS((rows_total, n), BF16)]
    args = [x_arr, g, shsc, w_main]
    if gate:
        in_specs.append(pl.BlockSpec((d, LANES), lambda i, j: (0, 0)))
        out_specs.append(pl.BlockSpec((tb, LANES), lambda i, j: (rb + i, 0)))
        out_shape.append(SDS((rows_total, LANES), F32))
        args.append(w_gate)
    aliases = {}
    if filled is not None:
        in_specs += [pl.BlockSpec(memory_space=pl.ANY)] * nout
        args += list(filled)
        aliases = {nin + k: k for k in range(nout)}
    return pl.pallas_call(
        body, name=name, grid=(m_rows // tb, n // cb), in_specs=in_specs, out_specs=out_specs, out_shape=out_shape,
        input_output_aliases=aliases, scratch_shapes=[pltpu.VMEM((tb, d), BF16)],
        compiler_params=_cp("arbitrary", "arbitrary"))(*args)


def _seg_masks(row, t_rows, s_rows):
    prev_ok = (row != 0) & (row != t_rows)
    next_ok = (row != t_rows - 1) & (row != s_rows - 1)
    return prev_ok, next_ok


def _shift_rows(z, halo_prev, halo_next, tb):
    loc = lax.broadcasted_iota(jnp.int32, (tb, 1), 0)
    zp = jnp.where(loc == 0, halo_prev, pltpu.roll(z, 1, 0))
    zn = jnp.where(loc == tb - 1, halo_next, pltpu.roll(z, tb - 1, 0))
    return zp, zn


def _qk_conv(z_main, conv_w, t_rows, md, qscale):
    s_rows = z_main.shape[0]
    tb = _pick(s_rows, (1280, 1024, 256))
    cb = _pick(md, (512, 256, 128))
    nb8 = tb // 8

    def body(zm, zp, zn, w_ref, o_ref):
        i, j = pl.program_id(0), pl.program_id(1)
        z = zm[...].astype(F32)
        zprev, znext = _shift_rows(z, zp[7:8, :].astype(F32), zn[0:1, :].astype(F32), tb)
        row = i * tb + lax.broadcasted_iota(jnp.int32, (tb, 1), 0)
        prev_ok, next_ok = _seg_masks(row, t_rows, s_rows)
        pre = (w_ref[0:1, :] * jnp.where(prev_ok, zprev, 0.0) + w_ref[1:2, :] * z
               + w_ref[2:3, :] * jnp.where(next_ok, znext, 0.0))
        scale = jnp.where(j * cb < md, qscale, 1.0)
        o_ref[...] = _bf(pre * _sigmoid(pre) * scale)

    return pl.pallas_call(
        body, name="qk_conv", grid=(s_rows // tb, 2 * md // cb),
        in_specs=[pl.BlockSpec((tb, cb), lambda i, j: (i, j)),
                  pl.BlockSpec((8, cb), lambda i, j: (jnp.maximum(i * nb8 - 1, 0), j)),
                  pl.BlockSpec((8, cb), lambda i, j: (jnp.minimum((i + 1) * nb8, s_rows // 8 - 1), j)),
                  pl.BlockSpec((8, cb), lambda i, j: (0, j))],
        out_specs=pl.BlockSpec((tb, cb), lambda i, j: (i, j)),
        out_shape=SDS((s_rows, 2 * md), BF16), compiler_params=_cp("arbitrary", "arbitrary"))(z_main, z_main, z_main, conv_w)


def _chunk_gates(gates, bias, rev):
    ln = gates.shape[0]
    gz = gates + bias
    logf = _log_sigmoid(gz)
    r_id = lax.broadcasted_iota(jnp.int32, (ln, ln), 0)
    c_id = lax.broadcasted_iota(jnp.int32, (ln, ln), 1)
    mask = (c_id >= r_id) if rev else (c_id <= r_id)
    mb = mask.astype(F32).astype(BF16)
    b_all = _mask_dot(mb, logf)
    g_all = jnp.sum(logf, axis=0, keepdims=True)
    return gz, b_all, b_all.T, gz.T, g_all, mask, mb


def _head_weights(b_col, b_row, i_row, m_in, mask):
    d = jnp.where(mask, b_col - b_row + i_row, NEG)
    inter = b_col + m_in
    m_row = jnp.maximum(inter, jnp.max(d, axis=1, keepdims=True))
    return jnp.exp(d - m_row), jnp.exp(inter - m_row), m_row


def _head_state_coeffs(g, b_col, i_col, m_in):
    a = g - b_col + i_col
    m_new = jnp.maximum(g + m_in, jnp.max(a, axis=0, keepdims=True))
    return jnp.exp(g + m_in - m_new), jnp.exp(a - m_new), m_new


def _mlstm_fwd(qk, z_main, zg, bias, nh):
    s_rows = qk.shape[0]
    md = qk.shape[1] // 2
    dh = md // nh
    nc = s_rows // LCH
    ln = LCH

    def chunk_f(i):
        return jnp.where(i == 0, nc - 1, i - 1)

    def chunk_b(i):
        return jnp.where(i == 0, nc - 1, nc - 1 - i)

    def body(qf, kf, vf, gf, qb, kb, vb, gb, bias_ref, hf_ref, hb_ref, cf_ref, nf_ref, mf_ref, cb_ref, nb_ref, mb_ref,
             c_sc, n_sc, m_sc):
        i = pl.program_id(0)

        @pl.when(i == 0)
        def _():
            c_sc[...] = jnp.zeros_like(c_sc)
            n_sc[...] = jnp.zeros_like(n_sc)
            m_sc[...] = jnp.full(m_sc.shape, M_INIT, F32)

        sides = ((qf, kf, vf, gf, hf_ref, cf_ref, nf_ref, mf_ref), (qb, kb, vb, gb, hb_ref, cb_ref, nb_ref, mb_ref))
        gates = [_chunk_gates(s[3][...], bias_ref[...], dr == 1) for dr, s in enumerate(sides)]
        units = []
        for dr, (q_ref, k_ref, v_ref, _, h_ref, c_out, n_out, m_out) in enumerate(sides):
            gz, b_all, b_t, g_t, g_all, mask, _ = gates[dr]
            for h in range(nh):
                ci, cf = 2 * dr * nh + h, (2 * dr + 1) * nh + h
                sl = slice(h * dh, (h + 1) * dh)
                u = dict(dr=dr, h=h, sl=sl, h_ref=h_ref, q=q_ref[:, sl], k=k_ref[:, sl], v=v_ref[:, sl],
                         c_in=c_sc[dr, h], n_in=n_sc[dr, h, 0:1, :], m_in=m_sc[dr, h, 0:1, 0:1],
                         b_col=b_all[:, cf:cf + 1], i_col=gz[:, ci:ci + 1], g=g_all[:, cf:cf + 1])
                c_out[sl, :] = u["c_in"]
                n_out[:, sl] = n_sc[dr, h]
                m_out[h] = m_sc[dr, h]
                u["w"], u["w_int"], u["m_row"] = _head_weights(u["b_col"], b_t[cf:cf + 1, :], g_t[ci:ci + 1, :], u["m_in"], mask)
                u["qk"] = _dot_nt(u["q"], u["k"])
                units.append(u)
        for u in units:
            u["s_mat"] = u["qk"] * u["w"]
            u["qc"] = _dot(u["q"], _bf(u["c_in"]))
            u["a_old"], u["coef"], u["m_new"] = _head_state_coeffs(u["g"], u["b_col"], u["i_col"], u["m_in"])
            u["kw"] = u["k"].astype(F32) * u["coef"]
        for u in units:
            u["sv"] = _dot(_bf(u["s_mat"]), u["v"])
            u["kv"] = _dot_tn(_bf(u["kw"]), u["v"])
        for u in units:
            dr, h = u["dr"], u["h"]
            num = u["sv"] + u["w_int"] * u["qc"]
            den = (jnp.sum(u["s_mat"], axis=1, keepdims=True)
                   + u["w_int"] * jnp.sum(u["q"].astype(F32) * u["n_in"], axis=1, keepdims=True))
            u["h_ref"][:, u["sl"]] = _bf(num / jnp.maximum(jnp.abs(den), jnp.exp(-u["m_row"])))
            c_sc[dr, h] = u["a_old"] * u["c_in"] + u["kv"]
            n_sc[dr, h] = jnp.broadcast_to(u["a_old"] * u["n_in"] + jnp.sum(u["kw"], axis=0, keepdims=True), (8, dh))
            m_sc[dr, h] = jnp.broadcast_to(u["m_new"], (8, LANES))

    def tok(cfn, col):
        return pl.BlockSpec((ln, md), lambda i: (cfn(i), col))

    def gat(cfn):
        return pl.BlockSpec((ln, LANES), lambda i: (cfn(i), 0))

    def st(cfn, shape):
        return pl.BlockSpec((None,) + shape, lambda i: (cfn(i),) + (0,) * len(shape))

    st_shapes = ((nh * dh, dh), (8, md), (nh, 8, LANES))
    return pl.pallas_call(
        body, name="mlstm_fwd", grid=(nc,),
        in_specs=[tok(chunk_f, 0), tok(chunk_f, 1), tok(chunk_f, 2), gat(chunk_f),
                  tok(chunk_b, 0), tok(chunk_b, 1), tok(chunk_b, 2), gat(chunk_b),
                  pl.BlockSpec((1, LANES), lambda i: (0, 0))],
        out_specs=[tok(chunk_f, 0), tok(chunk_b, 0)] + [st(chunk_f, s) for s in st_shapes] + [st(chunk_b, s) for s in st_shapes],
        out_shape=[SDS((s_rows, md), BF16)] * 2 + [SDS((nc,) + s, F32) for s in st_shapes] * 2,
        scratch_shapes=[pltpu.VMEM((2, nh, dh, dh), F32), pltpu.VMEM((2, nh, 8, dh), F32), pltpu.VMEM((2, nh, 8, LANES), F32)],
        compiler_params=_cp("arbitrary"))(qk, qk, z_main, zg, qk, qk, z_main, zg, bias)


def _head_rms(hs, nh, dh):
    parts, scales = [], []
    for h in range(nh):
        hh = hs[:, h * dh:(h + 1) * dh]
        r = lax.rsqrt(jnp.mean(hh * hh, axis=-1, keepdims=True) + EPS)
        parts.append(hh * r)
        scales.append(r)
    return jnp.concatenate(parts, axis=1), scales


def _layer_norm(v):
    vc = v - jnp.mean(v, axis=-1, keepdims=True)
    r = lax.rsqrt(jnp.mean(vc * vc, axis=-1, keepdims=True) + EPS)
    return vc * r, r


def _sgu_mix(vnb, ws_ref, bs_ref, tb, ng, gd, sc):
    rows = []
    for ch in range(tb // sc):
        cols = []
        for g in range(ng):
            blk = vnb[ch * sc:(ch + 1) * sc, g * gd:(g + 1) * gd]
            cols.append(_dot(_bf(ws_ref[g]), blk) + bs_ref[:, g:g + 1])
        rows.append(jnp.concatenate(cols, axis=1))
    return jnp.concatenate(rows, axis=0)


def _mixer_fwd(hf, hb, z_main, xs, hg, lng, lnb, w_s, b_st, wbm, wbs, wout, mx2, t_rows, nh):
    d = xs.shape[1]
    ng, sc = w_s.shape[0], w_s.shape[1]
    dh, gd = d // nh, d // ng
    tb = _pick(t_rows, (256,))

    def body(hf_ref, hb_ref, zo, zu, zv, zgm, zgg, x_ref, hg_ref, lng_ref, lnb_ref, ws_ref, bs_ref, wbm_ref, wbs_ref,
             wo_ref, mx2_ref, h1_ref, ym_ref, ys_ref, pm_ref, ps_ref, y_ref, out_ref):
        hs = hf_ref[...].astype(F32) + hb_ref[...].astype(F32)
        hn, _ = _head_rms(hs, nh, dh)
        ym = _bf(_sigmoid(zo[...].astype(F32)) * (hn * hg_ref[...]))
        ym_ref[...] = ym
        vhat, _ = _layer_norm(_gelu(zv[...].astype(F32)))
        vnb = _bf(vhat * lng_ref[...] + lnb_ref[...])
        ys = _bf(_gelu(zu[...].astype(F32)) * _sgu_mix(vnb, ws_ref, bs_ref, tb, ng, gd, sc))
        ys_ref[...] = ys
        pm = _dot(ym, wbm_ref[...])
        ps = _dot(ys, wbs_ref[...])
        pm_ref[...] = _bf(pm)
        ps_ref[...] = _bf(ps)
        y = _bf(_sigmoid(zgm[...].astype(F32)) * pm + _sigmoid(zgg[...].astype(F32)) * ps)
        y_ref[...] = y
        out = _dot(y, wo_ref[...])
        out_ref[...] = _bf(out)
        h1_ref[...] = x_ref[...] + mx2_ref[...] * out

    def tok(col):
        return pl.BlockSpec((tb, d), lambda i: (i, col))

    def full(shape):
        return pl.BlockSpec(shape, lambda i: (0,) * len(shape))

    return pl.pallas_call(
        body, name="mixer_fwd", grid=(t_rows // tb,),
        in_specs=[tok(0), tok(0), tok(3), tok(4), tok(5), tok(6), tok(7), tok(0), full((1, d)), full((1, d)), full((1, d)),
                  full((ng, sc, sc)), full((sc, LANES)), full((d, d)), full((d, d)), full((d, d)), full((1, d))],
        out_specs=[tok(0)] * 7,
        out_shape=[SDS((t_rows, d), F32)] + [SDS((t_rows, d), BF16)] * 6,
        compiler_params=_cp("arbitrary"))(hf, hb, z_main, z_main, z_main, z_main, z_main, xs, hg, lng, lnb, w_s, b_st,
                                          wbm, wbs, wout, mx2)


def _resident(shape):
    return pl.BlockSpec(shape, lambda *_: (0,) * len(shape), pipeline_mode=pl.Buffered(1))


def _grid_taps(a_ext, n_ext):
    col = lax.broadcasted_iota(jnp.int32, (n_ext, 1), 0) % GRID_W
    left = jnp.where(col != 0, pltpu.roll(a_ext, 1, 0), 0.0)
    right = jnp.where(col != GRID_W - 1, pltpu.roll(a_ext, n_ext - 1, 0), 0.0)
    return left, right


def _with_halo(prev, main, nxt, i, ni, tb):
    ext = jnp.concatenate([prev, main, nxt], axis=0).astype(F32)
    pos = lax.broadcasted_iota(jnp.int32, (tb + 2 * GRID_W, 1), 0)
    inside = ((pos >= GRID_W) | (i > 0)) & ((pos < tb + GRID_W) | (i < ni - 1))
    return jnp.where(inside, ext, 0.0)


def _halo_specs(tb, cb, t_rows, col0=0):
    nh64 = tb // GRID_W
    return [pl.BlockSpec((tb, cb), lambda i, j: (i, col0 + j)),
            pl.BlockSpec((GRID_W, cb), lambda i, j: (jnp.maximum(i * nh64 - 1, 0), col0 + j)),
            pl.BlockSpec((GRID_W, cb), lambda i, j: (jnp.minimum((i + 1) * nh64, t_rows // GRID_W - 1), col0 + j))]


def _ffn_tail(ab, w_conv9, w_down, h1, mx5, gfin, target, dff):
    t_rows, d = h1.shape
    tb = _pick(t_rows, (256,))
    cb = _pick(dff, (1408, 256, 128))
    ni, nj = t_rows // tb, dff // cb
    n_ext = tb + 2 * GRID_W

    def body(am, ap, an, b_ref, wc_ref, wd_ref, h1_ref, mx5_ref, gf_ref, tg_ref, ac_ref, f_ref, dh2_ref, dffn_ref, st_ref, acc):
        i, j = pl.program_id(0), pl.program_id(1)
        a_ext = _with_halo(ap[...], am[...], an[...], i, ni, tb)
        left, right = _grid_taps(a_ext, n_ext)
        conv = jnp.zeros((tb, cb), F32)
        for di in range(3):
            o = di * GRID_W
            conv = conv + (wc_ref[3 * di:3 * di + 1, :] * left[o:o + tb] + wc_ref[3 * di + 1:3 * di + 2, :] * a_ext[o:o + tb]
                           + wc_ref[3 * di + 2:3 * di + 3, :] * right[o:o + tb])
        ac_ref[...] = _bf(conv)
        fb = _bf(conv * _sigmoid(conv) * b_ref[...].astype(F32))
        f_ref[...] = fb

        @pl.when(j == 0)
        def _():
            acc[...] = jnp.zeros_like(acc)

        @pl.when((i == 0) & (j == 0))
        def _():
            st_ref[...] = jnp.zeros_like(st_ref)

        acc[...] += _dot(fb, wd_ref[pl.ds(pl.multiple_of(j * cb, cb), cb), :])

        @pl.when(j == nj - 1)
        def _():
            ffn = acc[...]
            h2 = h1_ref[...] + mx5_ref[...] * ffn
            r = lax.rsqrt(jnp.mean(h2 * h2, axis=-1, keepdims=True) + EPS)
            xn = h2 * r
            e = xn * gf_ref[...] - tg_ref[...]
            loss = 0.5 * jnp.sum(jnp.sum(e * e, axis=1, keepdims=True), axis=0, keepdims=True) / d
            dy = e * (1.0 / d)
            dxn = dy * gf_ref[...]
            dh2 = r * (dxn - xn * jnp.mean(dxn * xn, axis=-1, keepdims=True))
            dh2_ref[...] = dh2
            dffn_ref[...] = _bf(dh2 * mx5_ref[...])
            st_ref[...] += jnp.concatenate(
                [jnp.sum(dy * xn, axis=0, keepdims=True), jnp.sum(dh2 * ffn, axis=0, keepdims=True),
                 jnp.broadcast_to(loss, (1, d)), jnp.zeros((5, d), F32)], axis=0)

    def tokd():
        return pl.BlockSpec((tb, d), lambda i, j: (i, 0))

    def rowd():
        return pl.BlockSpec((1, d), lambda i, j: (0, 0))

    return pl.pallas_call(
        body, name="ffn_tail", grid=(ni, nj),
        in_specs=_halo_specs(tb, cb, t_rows) + [pl.BlockSpec((tb, cb), lambda i, j: (i, nj + j)),
                                                pl.BlockSpec((16, cb), lambda i, j: (0, j)),
                                                _resident((dff, d)), tokd(), rowd(), rowd(), tokd()],
        out_specs=[pl.BlockSpec((tb, cb), lambda i, j: (i, j)), pl.BlockSpec((tb, cb), lambda i, j: (i, j)), tokd(), tokd(),
                   pl.BlockSpec((8, d), lambda i, j: (0, 0))],
        out_shape=[SDS((t_rows, dff), BF16), SDS((t_rows, dff), BF16), SDS((t_rows, d), F32), SDS((t_rows, d), BF16),
                   SDS((8, d), F32)],
        scratch_shapes=[pltpu.VMEM((tb, d), F32)],
        compiler_params=_cp("arbitrary", "arbitrary"))(ab, ab, ab, ab, w_conv9, w_down, h1, mx5, gfin, target)


def _ffn_bwd_gate(dffn, w_down, aconv, ab, dff):
    t_rows, d = dffn.shape
    tb = _pick(t_rows, (512,))
    cb = _pick(dff, (1408, 256, 128))
    nj = dff // cb

    def body(g_ref, wd_ref, ac_ref, b_ref, db_ref, dac_ref):
        df = _dot_nt(g_ref[...], wd_ref[pl.ds(pl.multiple_of(pl.program_id(1) * cb, cb), cb), :])
        ac = ac_ref[...].astype(F32)
        sa = _sigmoid(ac)
        db_ref[...] = _bf(df * ac * sa)
        dac_ref[...] = _bf(df * b_ref[...].astype(F32) * (sa * (1.0 + ac * (1.0 - sa))))

    blk = pl.BlockSpec((tb, cb), lambda i, j: (i, j))
    return pl.pallas_call(
        body, name="ffn_bwd_gate", grid=(t_rows // tb, nj),
        in_specs=[pl.BlockSpec((tb, d), lambda i, j: (i, 0)), _resident((dff, d)), blk,
                  pl.BlockSpec((tb, cb), lambda i, j: (i, nj + j))],
        out_specs=[blk, blk], out_shape=[SDS((t_rows, dff), BF16)] * 2,
        compiler_params=_cp("arbitrary", "arbitrary"))(dffn, w_down, aconv, ab)


def _ffn_conv_bwd(dac, ab, w_conv9, dff):
    t_rows = dac.shape[0]
    tb = _pick(t_rows, (512, 256))
    cb = _pick(dff, (1408, 256, 128))
    ni, nj = t_rows // tb, dff // cb
    n_ext = tb + 2 * GRID_W
    nh64 = tb // GRID_W

    def body(dm, dp, dn, am, ap, an, wc_ref, da_ref, gw_ref):
        i = pl.program_id(1)
        d_ext = _with_halo(dp[...], dm[...], dn[...], i, ni, tb)
        a_ext = _with_halo(ap[...], am[...], an[...], i, ni, tb)
        d_left, d_right = _grid_taps(d_ext, n_ext)
        a_left, a_right = _grid_taps(a_ext, n_ext)
        dmain = d_ext[GRID_W:GRID_W + tb]
        da = jnp.zeros((tb, cb), F32)
        rows = []
        for di in range(3):
            o = (2 - di) * GRID_W
            da = da + (wc_ref[3 * di:3 * di + 1, :] * d_right[o:o + tb] + wc_ref[3 * di + 1:3 * di + 2, :] * d_ext[o:o + tb]
                       + wc_ref[3 * di + 2:3 * di + 3, :] * d_left[o:o + tb])
            o = di * GRID_W
            for tap in (a_left, a_ext, a_right):
                rows.append(jnp.sum(dmain * tap[o:o + tb], axis=0, keepdims=True))
        da_ref[...] = _bf(da)

        @pl.when(i == 0)
        def _():
            gw_ref[...] = jnp.zeros_like(gw_ref)

        gw_ref[...] += jnp.concatenate(rows + [jnp.zeros((7, cb), F32)], axis=0)

    def halo(col0):
        return [pl.BlockSpec((tb, cb), lambda j, i: (i, col0 + j)),
                pl.BlockSpec((GRID_W, cb), lambda j, i: (jnp.maximum(i * nh64 - 1, 0), col0 + j)),
                pl.BlockSpec((GRID_W, cb), lambda j, i: (jnp.minimum((i + 1) * nh64, t_rows // GRID_W - 1), col0 + j))]

    return pl.pallas_call(
        body, name="ffn_conv_bwd", grid=(nj, ni),
        in_specs=halo(0) + halo(0) + [pl.BlockSpec((16, cb), lambda j, i: (0, j))],
        out_specs=[pl.BlockSpec((tb, cb), lambda j, i: (i, j)), pl.BlockSpec((16, cb), lambda j, i: (0, j))],
        out_shape=[SDS((t_rows, dff), BF16), SDS((16, dff), F32)],
        compiler_params=_cp("arbitrary", "arbitrary"))(dac, dac, dac, ab, ab, ab, w_conv9)


def _proj_norm_bwd(pairs, x_arr, x_row0, g, scale, resid, m_rows, name, row_blocks=(1024, 256)):
    d = x_arr.shape[1]
    tm = _pick(m_rows, row_blocks)
    te = 256
    ni = m_rows // tm
    starts, total = [], 0
    for (_, _, _, _, k_p, tk_p) in pairs:
        starts.append(total)
        total += k_p // tk_p
    npairs = len(pairs)
    has_dx = resid is not None

    def body(*refs):
        a_refs, b_refs = refs[0:2 * npairs:2], refs[1:2 * npairs:2]
        rest = refs[2 * npairs:]
        if has_dx:
            x_ref, g_ref, sc_ref, r_ref, dx_ref, st_ref, acc = rest
        else:
            x_ref, g_ref, sc_ref, st_ref, acc = rest
        i, k = pl.program_id(0), pl.program_id(1)

        @pl.when(k == 0)
        def _():
            acc[...] = jnp.zeros_like(acc)

        @pl.when((i == 0) & (k == 0))
        def _():
            st_ref[...] = jnp.zeros_like(st_ref)

        for p in range(npairs):
            nk = pairs[p][4] // pairs[p][5]

            @pl.when((k >= starts[p]) & (k < starts[p] + nk))
            def _(p=p):
                acc[...] += _dot_nt(a_refs[p][...], b_refs[p][...])

        @pl.when(k == total - 1)
        def _():
            sums = [jnp.zeros((1, d), F32)] * 3
            for r0 in range(0, tm, te):
                rows = slice(r0, r0 + te)
                dhn = acc[rows, :]
                x = x_ref[rows, :]
                r = lax.rsqrt(jnp.mean(x * x, axis=-1, keepdims=True) + EPS)
                xn = x * r
                dmod = dhn * (1.0 + sc_ref[...])
                dxn = dmod * g_ref[...]
                if has_dx:
                    dx_ref[rows, :] = r * (dxn - xn * jnp.mean(dxn * xn, axis=-1, keepdims=True)) + r_ref[rows, :]
                sums = [sums[0] + jnp.sum(dmod * xn, axis=0, keepdims=True), sums[1] + jnp.sum(dhn, axis=0, keepdims=True),
                        sums[2] + jnp.sum(dhn * (xn * g_ref[...]), axis=0, keepdims=True)]
            st_ref[...] += jnp.concatenate(sums + [jnp.zeros((5, d), F32)], axis=0)

    in_specs, args = [], []
    for p, (a, a_row0, b, b_col0, k_p, tk_p) in enumerate(pairs):
        nk, s0, ar, bc = k_p // tk_p, starts[p], a_row0 // tm, b_col0 // tk_p

        def kk(k, s0=s0, nk=nk):
            return jnp.clip(k - s0, 0, nk - 1)

        in_specs.append(pl.BlockSpec((tm, tk_p), lambda i, k, ar=ar, kk=kk: (ar + i, kk(k))))
        in_specs.append(pl.BlockSpec((d, tk_p), lambda i, k, bc=bc, kk=kk: (0, bc + kk(k)),
                                     pipeline_mode=pl.Buffered(1 if nk == 1 else 2)))
        args += [a, b]
    xr = x_row0 // tm
    in_specs += [pl.BlockSpec((tm, d), lambda i, k: (xr + i, 0)), pl.BlockSpec((1, d), lambda i, k: (0, 0)),
                 pl.BlockSpec((1, d), lambda i, k: (0, 0))]
    args += [x_arr, g, scale]
    out_specs, out_shape = [], []
    if has_dx:
        in_specs.append(pl.BlockSpec((tm, d), lambda i, k: (i, 0)))
        args.append(resid)
        out_specs.append(pl.BlockSpec((tm, d), lambda i, k: (i, 0)))
        out_shape.append(SDS((m_rows, d), F32))
    out_specs.append(pl.BlockSpec((8, d), lambda i, k: (0, 0)))
    out_shape.append(SDS((8, d), F32))
    return pl.pallas_call(
        body, name=name, grid=(ni, total), in_specs=in_specs, out_specs=out_specs, out_shape=out_shape,
        scratch_shapes=[pltpu.VMEM((tm, d), F32)], compiler_params=_cp("arbitrary", "arbitrary"))(*args)


def _wgrad(a, b, k_rows, name):
    m, n = a.shape[1], b.shape[1]
    tm = _pick(m, (1408, 1024, 512, 384, 256, 128))
    tn = _pick(n, (3072, 2816, 2560, 1408, 1024, 768, 512, 384, 256, 128))
    tk = _pick(k_rows, (1280, 1024, 256))
    nk = k_rows // tk

    def body(a_ref, b_ref, o_ref, acc):
        k = pl.program_id(2)

        @pl.when(k == 0)
        def _():
            acc[...] = jnp.zeros_like(acc)

        acc[...] += _dot_tn(a_ref[...], b_ref[...])

        @pl.when(k == nk - 1)
        def _():
            o_ref[...] = _bf(acc[...])

    return pl.pallas_call(
        body, name=name, grid=(m // tm, n // tn, nk),
        in_specs=[pl.BlockSpec((tk, tm), lambda i, j, k: (k, i)), pl.BlockSpec((tk, tn), lambda i, j, k: (k, j))],
        out_specs=pl.BlockSpec((tm, tn), lambda i, j, k: (i, j)), out_shape=SDS((m, n), BF16),
        scratch_shapes=[pltpu.VMEM((tm, tn), F32)],
        compiler_params=_cp("arbitrary", "arbitrary", "arbitrary"))(a, b)


def _lane_put(col, lane_idx):
    lane = lax.broadcasted_iota(jnp.int32, (1, LANES), 1)
    return jnp.where(lane == lane_idx, col, 0.0)


def _mixer_bwd(dh1, out, hf, hb, z_main, pm, ps, hg, lng, lnb, w_s, b_st, wbm, wbs, wout, mx2, t_rows, nh):
    d = dh1.shape[1]
    ng, sc = w_s.shape[0], w_s.shape[1]
    dh, gd = d // nh, d // ng
    tb = _pick(t_rows, (256,))

    def body(dh1_ref, out_ref, hf_ref, hb_ref, zo, zu, zv, zgm, zgg, pm_ref, ps_ref, hg_ref, lng_ref, lnb_ref, ws_ref, bs_ref,
             wbm_ref, wbs_ref, wo_ref, mx2_ref, dz_ref, dhs_ref, dout_ref, dpm_ref, dps_ref, st_ref, dws_ref, dbs_ref):
        i = pl.program_id(0)

        @pl.when(i == 0)
        def _():
            st_ref[...] = jnp.zeros_like(st_ref)
            dws_ref[...] = jnp.zeros_like(dws_ref)
            dbs_ref[...] = jnp.zeros_like(dbs_ref)

        dh1v = dh1_ref[...]
        doutb = _bf(dh1v * mx2_ref[...])
        dout_ref[...] = doutb
        d_mx2 = jnp.sum(dh1v * out_ref[...].astype(F32), axis=0, keepdims=True)
        dy = _dot_nt(doutb, wo_ref[...])
        sgm, sgg = _sigmoid(zgm[...].astype(F32)), _sigmoid(zgg[...].astype(F32))
        dpmb, dpsb = _bf(dy * sgm), _bf(dy * sgg)
        dpm_ref[...] = dpmb
        dps_ref[...] = dpsb
        dz_ref[:, 3 * d:4 * d] = _bf(dy * pm_ref[...].astype(F32) * sgm * (1.0 - sgm))
        dz_ref[:, 4 * d:5 * d] = _bf(dy * ps_ref[...].astype(F32) * sgg * (1.0 - sgg))
        dym = _dot_nt(dpmb, wbm_ref[...])
        dys = _dot_nt(dpsb, wbs_ref[...])
        hs = hf_ref[...].astype(F32) + hb_ref[...].astype(F32)
        hn, scales = _head_rms(hs, nh, dh)
        so = _sigmoid(zo[...].astype(F32))
        dz_ref[:, 0:d] = _bf(dym * (hn * hg_ref[...]) * so * (1.0 - so))
        dhmn = dym * so
        d_hg = jnp.sum(dhmn * hn, axis=0, keepdims=True)
        dhn = dhmn * hg_ref[...]
        for h in range(nh):
            sl = slice(h * dh, (h + 1) * dh)
            dhs_ref[:, sl] = _bf(scales[h] * (dhn[:, sl] - hn[:, sl] * jnp.mean(dhn[:, sl] * hn[:, sl], axis=-1, keepdims=True)))
        zuv, zvv = zu[...].astype(F32), zv[...].astype(F32)
        u, du_dz = _gelu_and_grad(zuv)
        vg, dvg_dz = _gelu_and_grad(zvv)
        vhat, rstd = _layer_norm(vg)
        vnb = _bf(vhat * lng_ref[...] + lnb_ref[...])
        mixed = _sgu_mix(vnb, ws_ref, bs_ref, tb, ng, gd, sc)
        dz_ref[:, d:2 * d] = _bf(dys * mixed * du_dz)
        dmix = dys * u
        rows = []
        dbs = jnp.zeros((sc, LANES), F32)
        for ch in range(tb // sc):
            cols = []
            for g in range(ng):
                dm = dmix[ch * sc:(ch + 1) * sc, g * gd:(g + 1) * gd]
                dmb = _bf(dm)
                dws_ref[g] += _dot_nt(dmb, vnb[ch * sc:(ch + 1) * sc, g * gd:(g + 1) * gd])
                dbs = dbs + _lane_put(jnp.sum(dm, axis=1, keepdims=True), g)
                cols.append(_dot_tn(_bf(ws_ref[g]), dmb))
            rows.append(jnp.concatenate(cols, axis=1))
        dbs_ref[...] += dbs
        dvn = jnp.concatenate(rows, axis=0)
        d_lng = jnp.sum(dvn * vhat, axis=0, keepdims=True)
        d_lnb = jnp.sum(dvn, axis=0, keepdims=True)
        dvh = dvn * lng_ref[...]
        dvg = rstd * (dvh - jnp.mean(dvh, axis=-1, keepdims=True) - vhat * jnp.mean(dvh * vhat, axis=-1, keepdims=True))
        dz_ref[:, 2 * d:3 * d] = _bf(dvg * dvg_dz)
        st_ref[...] += jnp.concatenate([d_mx2, d_hg, d_lng, d_lnb, jnp.zeros((4, d), F32)], axis=0)

    def tok(col):
        return pl.BlockSpec((tb, d), lambda i: (i, col))

    def full(shape):
        return pl.BlockSpec(shape, lambda i: (0,) * len(shape))

    return pl.pallas_call(
        body, name="mixer_bwd", grid=(t_rows // tb,),
        in_specs=[tok(0), tok(0), tok(0), tok(0), tok(3), tok(4), tok(5), tok(6), tok(7), tok(0), tok(0), full((1, d)),
                  full((1, d)), full((1, d)), full((ng, sc, sc)), full((sc, LANES)), full((d, d)), full((d, d)), full((d, d)),
                  full((1, d))],
        out_specs=[pl.BlockSpec((tb, 5 * d), lambda i: (i, 0)), tok(0), tok(0), tok(0), tok(0), full((8, d)), full((ng, sc, sc)),
                   full((sc, LANES))],
        out_shape=[SDS((t_rows, 5 * d), BF16)] + [SDS((t_rows, d), BF16)] * 4 + [SDS((8, d), F32), SDS((ng, sc, sc), F32),
                                                                                SDS((sc, LANES), F32)],
        compiler_params=_cp("arbitrary"))(dh1, out, hf, hb, z_main, z_main, z_main, z_main, z_main, pm, ps, hg, lng, lnb, w_s,
                                          b_st, wbm, wbs, wout, mx2)


def _mlstm_bwd(qk, z_main, zg, bias, dhs, hf, hb, states_f, states_b, nh, t_rows):
    s_rows = qk.shape[0]
    md = qk.shape[1] // 2
    dh = md // nh
    nc = s_rows // LCH
    nx = t_rows // LCH
    ln = LCH

    def chunk_f(i):
        return jnp.where(i == nc - 1, nc - 1, nc - 2 - i)

    def chunk_b(i):
        return jnp.where(i == nc - 1, nc - 1, i)

    def body(qf, kf, vf, gf, dhf, hsf, cf, nf, mf_, qb, kb, vb, gb, dhb, hsb, cb, nb, mb_, bias_ref, dqkvf_ref, dgf_ref, dqkvb_ref,
             dgb_ref, dc_sc, dn_sc):
        i = pl.program_id(0)
        is_ctx = i == nc - 1

        @pl.when(i == 0)
        def _():
            dc_sc[...] = jnp.zeros_like(dc_sc)
            dn_sc[...] = jnp.zeros_like(dn_sc)

        sides = ((qf, kf, vf, gf, dhf, hsf, cf, nf, mf_, dqkvf_ref, dgf_ref), (qb, kb, vb, gb, dhb, hsb, cb, nb, mb_, dqkvb_ref, dgb_ref))
        gates = [_chunk_gates(s[3][...], bias_ref[...], dr == 1) for dr, s in enumerate(sides)]
        units = []
        for dr, (q_ref, k_ref, v_ref, _, dh_ref, hs_ref, c_ref, n_ref, m_ref, dqkv_ref, _) in enumerate(sides):
            gz, b_all, b_t, g_t, g_all, mask, _ = gates[dr]
            for h in range(nh):
                ci, cfl = 2 * dr * nh + h, (2 * dr + 1) * nh + h
                sl = slice(h * dh, (h + 1) * dh)
                u = dict(dr=dr, h=h, sl=sl, ci=ci, cfl=cfl, dqkv_ref=dqkv_ref, q=q_ref[:, sl], k=k_ref[:, sl], v=v_ref[:, sl],
                         dhv=jnp.where(is_ctx, 0.0, dh_ref[:, sl].astype(F32)), hs=hs_ref[:, sl].astype(F32),
                         c_in=c_ref[sl, :], n_in=n_ref[0:1, sl], m_in=m_ref[h, 0:1, 0:1],
                         b_col=b_all[:, cfl:cfl + 1], i_col=gz[:, ci:ci + 1], g=g_all[:, cfl:cfl + 1],
                         dc_new=dc_sc[dr, h], dn_new=dn_sc[dr, h, 0:1, :])
                u["qf32"], u["kf32"] = u["q"].astype(F32), u["k"].astype(F32)
                u["w"], u["w_int"], u["m_row"] = _head_weights(u["b_col"], b_t[cfl:cfl + 1, :], g_t[ci:ci + 1, :], u["m_in"], mask)
                u["qk"] = _dot_nt(u["q"], u["k"])
                units.append(u)
        for u in units:
            s_mat = u["qk"] * u["w"]
            u["s_mat"], u["sb"], u["cb16"], u["dcb"] = s_mat, _bf(s_mat), _bf(u["c_in"]), _bf(u["dc_new"])
            den = jnp.sum(s_mat, axis=1, keepdims=True) + u["w_int"] * jnp.sum(u["qf32"] * u["n_in"], axis=1, keepdims=True)
            e_m = jnp.exp(-u["m_row"])
            dnm = jnp.maximum(jnp.abs(den), e_m)
            hdh = jnp.sum(u["hs"] * u["dhv"], axis=1, keepdims=True)
            u["dden"] = jnp.where(jnp.abs(den) > e_m, -(hdh / dnm) * jnp.sign(den), 0.0)
            u["dnum_b"] = _bf(u["dhv"] / dnm)
            u["a_old"], u["coef"], _ = _head_state_coeffs(u["g"], u["b_col"], u["i_col"], u["m_in"])
            u["dsm"] = _dot_nt(u["dnum_b"], u["v"])
            u["qct"] = _dot_nt(u["dnum_b"], u["cb16"])
            u["vdc"] = _dot_nt(u["v"], u["dcb"])
        for u in units:
            ds = u["dsm"] + u["dden"]
            u["pb"] = _bf(u["w"] * ds)
            u["gmat"] = u["s_mat"] * ds
            u["dv1"] = _dot_tn(u["sb"], u["dnum_b"])
            u["dv2"] = _dot(_bf(u["kf32"] * u["coef"]), u["dcb"])
            u["dcu"] = _dot_tn(_bf(u["qf32"] * u["w_int"]), u["dnum_b"])
        for u in units:
            u["dq1"] = _dot(u["pb"], u["k"])
            u["dk1"] = _dot_tn(u["pb"], u["q"])
        acc = [dict(x1=jnp.zeros((ln, LANES), F32), x2=jnp.zeros((ln, LANES), F32), dig=jnp.zeros((ln, LANES), F32),
                    e_row=jnp.zeros((1, LANES), F32)) for _ in range(2)]
        for u in units:
            dr, h, sl, a = u["dr"], u["h"], u["sl"], acc[u["dr"]]
            dq_inter = u["w_int"] * (u["qct"] + u["dden"] * u["n_in"])
            dk_state = u["coef"] * (u["vdc"] + u["dn_new"])
            u["dqkv_ref"][:, sl] = _bf(u["dq1"] + dq_inter)
            u["dqkv_ref"][:, md + h * dh:md + (h + 1) * dh] = _bf(u["dk1"] + dk_state)
            u["dqkv_ref"][:, 2 * md + h * dh:2 * md + (h + 1) * dh] = _bf(u["dv1"] + u["dv2"])
            row_intra = jnp.sum(u["gmat"], axis=1, keepdims=True)
            col_intra = jnp.sum(u["gmat"].T, axis=1, keepdims=True)
            row_inter = jnp.sum(u["qf32"] * dq_inter, axis=1, keepdims=True)
            col_inter = jnp.sum(u["kf32"] * dk_state, axis=1, keepdims=True)
            e_old = u["a_old"] * (jnp.sum(jnp.sum(u["dc_new"] * u["c_in"], axis=1, keepdims=True), axis=0, keepdims=True)
                                  + jnp.sum(u["dn_new"] * u["n_in"], axis=1, keepdims=True))
            a["x1"] = a["x1"] + _lane_put(row_intra - col_intra + row_inter, u["cfl"])
            a["x2"] = a["x2"] + _lane_put(col_inter, u["cfl"])
            a["e_row"] = a["e_row"] + _lane_put(e_old, u["cfl"])
            a["dig"] = a["dig"] + _lane_put(col_intra + col_inter, u["ci"])
            dc_sc[dr, h] = u["a_old"] * u["dc_new"] + u["dcu"]
            dn_sc[dr, h] = jnp.broadcast_to(
                u["a_old"] * u["dn_new"] + jnp.sum(u["qf32"] * (u["w_int"] * u["dden"]), axis=0, keepdims=True), (8, dh))
        for dr, s in enumerate(sides):
            gz, mfl, a = gates[dr][0], gates[dr][6], acc[dr]
            dlogf = _mask_dot_t(mfl, a["x1"]) + _mask_dot(mfl, a["x2"]) - a["x2"] + a["e_row"]
            s[10][...] = a["dig"] + dlogf / (1.0 + jnp.exp(gz))

    def tok(cfn, col):
        return pl.BlockSpec((ln, md), lambda i: (cfn(i), col))

    def dht(cfn):
        return pl.BlockSpec((ln, md), lambda i: (jnp.minimum(cfn(i), nx - 1), 0))

    def gat(cfn):
        return pl.BlockSpec((ln, LANES), lambda i: (cfn(i), 0))

    def st(cfn, shape):
        return pl.BlockSpec((None,) + shape, lambda i: (cfn(i),) + (0,) * len(shape))

    st_shapes = ((nh * dh, dh), (8, md), (nh, 8, LANES))

    def side(cfn):
        return [tok(cfn, 0), tok(cfn, 1), tok(cfn, 2), gat(cfn), dht(cfn), tok(cfn, 0)] + [st(cfn, s) for s in st_shapes]

    def outs(cfn):
        return [pl.BlockSpec((ln, 3 * md), lambda i: (cfn(i), 0)), gat(cfn)]

    return pl.pallas_call(
        body, name="mlstm_bwd", grid=(nc,),
        in_specs=side(chunk_f) + side(chunk_b) + [pl.BlockSpec((1, LANES), lambda i: (0, 0))],
        out_specs=outs(chunk_f) + outs(chunk_b),
        out_shape=[SDS((s_rows, 3 * md), BF16), SDS((s_rows, LANES), F32)] * 2,
        scratch_shapes=[pltpu.VMEM((2, nh, dh, dh), F32), pltpu.VMEM((2, nh, 8, dh), F32)],
        compiler_params=_cp("arbitrary"))(qk, qk, z_main, zg, dhs, hf, *states_f, qk, qk, z_main, zg, dhs, hb, *states_b, bias)


def _qkv_conv_bwd(dqkv_f, dqkv_b, z_main, conv_w, t_rows, md, qscale):
    s_rows = z_main.shape[0]
    tb = _pick(s_rows, (1280, 1024, 256))
    cb = _pick(md, (512, 256, 128))
    ni, nj, ncq = s_rows // tb, 3 * md // cb, 2 * md // cb
    nb8 = tb // 8
    n_ext = tb + 16

    def body(fm, fp, fn, bm, bp, bn, zm, zp, zn, w_ref, dz_ref, gw_ref):
        j, i = pl.program_id(0), pl.program_id(1)

        @pl.when(j < ncq)
        def _():
            z = jnp.concatenate([zp[...], zm[...], zn[...]], axis=0).astype(F32)
            dqk = (jnp.concatenate([fp[...], fm[...], fn[...]], axis=0).astype(F32)
                   + jnp.concatenate([bp[...], bm[...], bn[...]], axis=0).astype(F32)) * jnp.where(j * cb < md, qscale, 1.0)
            row = i * tb - 8 + lax.broadcasted_iota(jnp.int32, (n_ext, 1), 0)
            prev_ok, next_ok = _seg_masks(row, t_rows, s_rows)
            zprev = jnp.where(prev_ok, pltpu.roll(z, 1, 0), 0.0)
            znext = jnp.where(next_ok, pltpu.roll(z, n_ext - 1, 0), 0.0)
            pre = w_ref[0:1, :] * zprev + w_ref[1:2, :] * z + w_ref[2:3, :] * znext
            sg = _sigmoid(pre)
            dpre = dqk * (sg * (1.0 + pre * (1.0 - sg)))
            dz = (w_ref[1:2, :] * dpre + w_ref[0:1, :] * jnp.where(next_ok, pltpu.roll(dpre, n_ext - 1, 0), 0.0)
                  + w_ref[2:3, :] * jnp.where(prev_ok, pltpu.roll(dpre, 1, 0), 0.0))
            dz_ref[...] = _bf(dz[8:8 + tb])
            dm = dpre[8:8 + tb]

            @pl.when(i == 0)
            def _():
                gw_ref[...] = jnp.zeros_like(gw_ref)

            gw_ref[...] += jnp.concatenate(
                [jnp.sum(dm * zprev[8:8 + tb], axis=0, keepdims=True), jnp.sum(dm * z[8:8 + tb], axis=0, keepdims=True),
                 jnp.sum(dm * znext[8:8 + tb], axis=0, keepdims=True), jnp.zeros((5, cb), F32)], axis=0)

        @pl.when(j >= ncq)
        def _():
            dz_ref[...] = _bf(fm[...].astype(F32) + bm[...].astype(F32))

    def halo(clampj):
        def cj(j):
            return jnp.minimum(j, ncq - 1) if clampj else j
        return [pl.BlockSpec((tb, cb), lambda j, i: (i, cj(j))),
                pl.BlockSpec((8, cb), lambda j, i: (jnp.maximum(i * nb8 - 1, 0), cj(j))),
                pl.BlockSpec((8, cb), lambda j, i: (jnp.minimum((i + 1) * nb8, s_rows // 8 - 1), cj(j)))]

    return pl.pallas_call(
        body, name="qkv_conv_bwd", grid=(nj, ni),
        in_specs=halo(False) + halo(False) + halo(True) + [pl.BlockSpec((8, cb), lambda j, i: (0, jnp.minimum(j, ncq - 1)))],
        out_specs=[pl.BlockSpec((tb, cb), lambda j, i: (i, j)), pl.BlockSpec((8, cb), lambda j, i: (0, jnp.minimum(j, ncq - 1)))],
        out_shape=[SDS((s_rows, 3 * md), BF16), SDS((8, 2 * md), F32)],
        compiler_params=_cp("arbitrary", "arbitrary"))(dqkv_f, dqkv_f, dqkv_f, dqkv_b, dqkv_b, dqkv_b, z_main, z_main, z_main, conv_w)


def _gate_grad_sum(dg_f, dg_b):
    s_rows = dg_f.shape[0]
    tb = _pick(s_rows, (1280, 1024, 256))

    def body(a_ref, b_ref, o_ref, st_ref):
        @pl.when(pl.program_id(0) == 0)
        def _():
            st_ref[...] = jnp.zeros_like(st_ref)

        s = a_ref[...] + b_ref[...]
        o_ref[...] = _bf(s)
        st_ref[...] += jnp.concatenate([jnp.sum(s, axis=0, keepdims=True), jnp.zeros((7, LANES), F32)], axis=0)

    blk = pl.BlockSpec((tb, LANES), lambda i: (i, 0))
    return pl.pallas_call(
        body, name="gate_grad_sum", grid=(s_rows // tb,), in_specs=[blk, blk],
        out_specs=[blk, pl.BlockSpec((8, LANES), lambda i: (0, 0))],
        out_shape=[SDS((s_rows, LANES), BF16), SDS((8, LANES), F32)], compiler_params=_cp("arbitrary"))(dg_f, dg_b)


def _mod_grads(silu_slots, dmx_sh, dmx_slots, dmc_tot, dmc_sh, silu_cctx, c_ctx, w_mod_c):
    d = silu_slots.shape[1]
    ncol, n6 = dmx_sh.shape[1], dmx_slots.shape[1]

    def body(ss_ref, dsh_ref, dsl_ref, dct_ref, dcs_ref, sc_ref, c_ref, w_ref, gw_ref, gb_ref, gc_ref):
        a = jnp.concatenate([ss_ref[...], sc_ref[...], jnp.zeros((7, d), F32)], axis=0)
        b = jnp.concatenate([dsh_ref[...], dcs_ref[...], jnp.zeros((7, ncol), F32)], axis=0)
        gw_ref[0] = lax.dot_general(a, b, (((0,), (0,)), ((), ())), preferred_element_type=F32, precision=HI)
        dct = dct_ref[...]
        gb_ref[...] = jnp.sum(dsl_ref[...], axis=0, keepdims=True) + jnp.concatenate(
            [dct, jnp.zeros((1, n6 - dct.shape[1]), F32)], axis=1)
        t = _dot_nt(_bf(jnp.broadcast_to(dct, (8, dct.shape[1]))), w_ref[...])
        cv = c_ref[...]
        s = _sigmoid(cv)
        gc_ref[...] = t[0:1, :] * (s * (1.0 + cv * (1.0 - s)))

    return pl.pallas_call(body, name="mod_grads", out_shape=[SDS((1, d, ncol), F32), SDS((1, n6), F32), SDS((1, d), F32)],
                          compiler_params=_cp())(silu_slots, dmx_sh, dmx_slots, dmc_tot, dmc_sh, silu_cctx, c_ctx, w_mod_c)


def _slot_sum(slots):
    ns, r = slots.shape[0], slots.shape[1]
    tb = _pick(r, (1024, 512, 256, 128, 64, 32, 16, 8))

    def body(s_ref, o_ref):
        acc = s_ref[0]
        for k in range(1, ns):
            acc = acc + s_ref[k]
        o_ref[...] = acc

    return pl.pallas_call(
        body, name="slot_sum", grid=(r // tb,), in_specs=[pl.BlockSpec((ns, tb, LANES), lambda i: (0, i, 0))],
        out_specs=pl.BlockSpec((tb, LANES), lambda i: (i, 0)), out_shape=SDS((r, LANES), F32),
        compiler_params=_cp("arbitrary"))(slots)


def _adamw(w, gslots, m, v, name, after=None):
    lead = ((None,), (0,)) if w.ndim == 3 else ((), ())
    r, cdim = w.shape[-2:]
    ns, rg = gslots.shape[0], gslots.shape[1]
    tb = r if (rg != r or r % 8) else _pick(r, (128, 64, 32, 16, 8))
    bc1, bc2 = 1.0 - ADAM_B1 ** ADAM_STEP, 1.0 - ADAM_B2 ** ADAM_STEP

    def body(w_ref, g_ref, m_ref, v_ref, *rest):
        go_ref, d_ref, mo_ref, vo_ref = rest[-4:]
        g = g_ref[0, 0:tb, :].astype(F32)
        for k in range(1, ns):
            g = g + g_ref[k, 0:tb, :].astype(F32)
        mn = ADAM_B1 * m_ref[...] + (1.0 - ADAM_B1) * g
        vn = ADAM_B2 * v_ref[...] + (1.0 - ADAM_B2) * (g * g)
        go_ref[...] = g
        mo_ref[...] = mn
        vo_ref[...] = vn
        d_ref[...] = -ADAM_LR * ((mn / bc1) / (jnp.sqrt(vn / bc2) + ADAM_EPS) + ADAM_WD * w_ref[...])

    blk = pl.BlockSpec(lead[0] + (tb, cdim), lambda i: lead[1] + (i, 0))
    gblk = pl.BlockSpec((ns, tb if rg == r else rg, cdim), lambda i: (0, i, 0))
    extra_specs = [] if after is None else [pl.BlockSpec((8, LANES), lambda i: (0, 0))]
    extra_args = [] if after is None else [after]
    return pl.pallas_call(
        body, name=name, grid=(r // tb,), in_specs=[blk, gblk, blk, blk] + extra_specs,
        out_specs=[blk] * 4, out_shape=[SDS(w.shape, F32)] * 4, compiler_params=_cp("arbitrary"))(w, gslots, m, v, *extra_args)


def _pack(parts, row_mult):
    flat = jnp.concatenate([p.reshape(-1) for p in parts])
    n = flat.shape[0]
    rows = -(-n // LANES)
    rows = -(-rows // row_mult) * row_mult
    return jnp.pad(flat, (0, rows * LANES - n)).reshape(rows, LANES)


def _unpack(buf, shapes):
    flat = buf.reshape(-1)
    out, off = [], 0
    for s in shapes:
        n = math.prod(s)
        out.append(flat[off:off + n].reshape(s))
        off += n
    return out


def _pad_cols(a, width):
    return jnp.pad(a, ((0, 0), (0, width - a.shape[1])))


def _pad_lanes(a):
    return _pad_cols(a, LANES)


def _up128(n):
    return -(-n // LANES) * LANES


def kernel(x, c, ctx, c_ctx, w_mod, b_mod, norm1_g, w_in, b_gate, conv_qk, head_norm_g, sgu_ln_g, sgu_ln_b, w_s, b_s, w_branch_mlstm, w_branch_sgu, w_out, norm2_g, w_up, w_ffn_conv, w_down, final_g, loss_target, m_c_ctx, m_w_mod, m_b_mod, m_norm1_g, m_w_in, m_b_gate, m_conv_qk, m_head_norm_g, m_sgu_ln_g, m_sgu_ln_b, m_w_s, m_b_s, m_w_branch_mlstm, m_w_branch_sgu, m_w_out, m_norm2_g, m_w_up, m_w_ffn_conv, m_w_down, m_final_g, v_c_ctx, v_w_mod, v_b_mod, v_norm1_g, v_w_in, v_b_gate, v_conv_qk, v_head_norm_g, v_sgu_ln_g, v_sgu_ln_b, v_w_s, v_b_s, v_w_branch_mlstm, v_w_branch_sgu, v_w_out, v_norm2_g, v_w_up, v_w_ffn_conv, v_w_down, v_final_g):
    t, d = x.shape[1], x.shape[2]
    n_ctx = ctx.shape[1]
    s_rows = t + n_ctx
    nh = b_gate.shape[1] // 4
    md = head_norm_g.shape[1]
    dh = md // nh
    ng, sc = w_s.shape[1], w_s.shape[2]
    dff = w_down.shape[1] * N_DEV
    n_in = w_in.shape[2] * N_DEV
    assert md == d and sgu_ln_g.shape[1] == d and n_ctx == LCH and t % LCH == 0 and t % (8 * GRID_W) == 0
    assert n_in == 8 * d + 4 * nh and 4 * nh <= LANES
    me = 4 * lax.axis_index("x") + 2 * lax.axis_index("y") + lax.axis_index("c")

    n_mod, n_insh, n_upsh = w_mod.shape[2], w_in.shape[2], w_up.shape[2]
    p_mod, p_in, p_up = _up128(n_mod), _up128(n_insh), _up128(n_upsh)
    nq, nf = conv_qk.shape[2], w_ffn_conv.shape[3]
    ffn9 = w_ffn_conv[0].reshape(9, nf)
    colpack = jnp.concatenate([_pad_cols(_bf(w_mod[0]), p_mod), _pad_cols(_bf(w_in[0]), p_in)], axis=1)
    convpack = jnp.concatenate([jnp.pad(conv_qk[0], ((0, 13), (0, 0))), jnp.pad(ffn9, ((0, 7), (0, 0)))], axis=1)
    g_col, g_conv = _allgather([colpack, convpack])
    w_mod_f, w_main, w_gate = _assemble_cols(
        g_col, [(0, n_mod, [(0, 0, N_DEV * n_mod, 0)]),
                (p_mod, n_insh, [(1, 0, 3 * md, 0), (2, 3 * md, 4 * nh, 0), (1, 3 * md + 4 * nh, 5 * d, 3 * md)])],
        [N_MOD * d, 8 * d, LANES], "assemble_weights")
    convw, wconv9 = _assemble_cols(g_conv, [(0, nq, [(0, 0, N_DEV * nq, 0)]), (nq, nf, [(1, 0, N_DEV * nf, 0)])],
                                   [N_DEV * nq, N_DEV * nf], "assemble_conv_weights")
    zero = jnp.minimum(jnp.abs(g_conv[0, 0, 0]), 0.0)
    late_w = [_pad_cols(_bf(w_up[0] + zero), p_up), _bf(w_branch_mlstm[0]), _bf(w_branch_sgu[0]), _bf(w_out[0]), _bf(w_down[0])]
    late_state, late_tok = _exchange_start(late_w, False, "late_weights_start")

    cvec = jnp.concatenate([c, c_ctx[None], jnp.zeros((6, d), F32)], axis=0) + late_tok[0:1, 0:1]
    silu_v, mod = _modulation(cvec, w_mod_f, b_mod)
    mx = [mod[0:1, k * d:(k + 1) * d] for k in range(N_MOD)]
    mc = [mod[1:2, k * d:(k + 1) * d] for k in range(2)]
    x2, ctx2 = x[0], ctx[0]
    in_x = _norm_mod_proj(x2, norm1_g, jnp.concatenate([mx[0], mx[1]], axis=0), w_main, w_gate, s_rows, 0, None, "in_proj")
    hn, z_main, zg = _norm_mod_proj(ctx2, norm1_g, jnp.concatenate([mc[0], mc[1]], axis=0), w_main, w_gate, s_rows, t, in_x,
                                    "in_proj_ctx")
    qscale = dh ** -0.5
    qk = _qk_conv(z_main, convw, t, md, qscale)
    bias = _pad_lanes(b_gate)
    fwd = _mlstm_fwd(qk, z_main, zg, bias, nh)
    hf, hb, states_f, states_b = fwd[0], fwd[1], fwd[2:5], fwd[5:8]
    g_up, g_bm, g_bs, g_out, g_down = _exchange_wait(late_state, fwd[4], "late_weights_wait")
    (w_up_f,) = _assemble_cols(g_up, [(0, n_upsh, [(0, 0, 2 * dff, 0)])], [2 * dff], "assemble_w_up")
    wbm_f, wbs_f, wout_f = (g.reshape(d, d) for g in (g_bm, g_bs, g_out))
    w_down_f = g_down.reshape(dff, d)
    b_st = _pad_lanes(b_s[0].T)
    h1, ym, ys, pm, ps, y, out = _mixer_fwd(hf, hb, z_main, x2, head_norm_g, sgu_ln_g, sgu_ln_b, w_s[0], b_st, wbm_f, wbs_f,
                                            wout_f, mx[2], t, nh)
    hn2, ab = _norm_mod_proj(h1, norm2_g, jnp.concatenate([mx[3], mx[4]], axis=0), w_up_f, None, t, 0, None, "up_proj")
    aconv, f, dh2, dffn, st_tail = _ffn_tail(ab, wconv9, w_down_f, h1, mx[5], final_g[None], loss_target[0], dff)

    db, dac = _ffn_bwd_gate(dffn, w_down_f, aconv, ab, dff)
    da, g_wconv9 = _ffn_conv_bwd(dac, ab, wconv9, dff)
    g_wdown = _wgrad(f, dffn, t, "wgrad_down")
    gwup_slots = _scatter_cols([_wgrad(hn2, da, t, "wgrad_up_a"), _wgrad(hn2, db, t, "wgrad_up_b")],
                               [(0, 0, dff, 0), (1, dff, dff, 0)], n_upsh, "scatter_grad_w_up")
    dh1, st_n2 = _proj_norm_bwd([(da, 0, w_up_f, 0, dff, dff), (db, 0, w_up_f, dff, dff, dff)], h1, 0, norm2_g, mx[4], dh2, t,
                                "up_proj_bwd", (512, 256))
    dz_rest, dhs, dout, dpm, dps, st_mix, g_ws, g_bst = _mixer_bwd(dh1, out, hf, hb, z_main, pm, ps, head_norm_g, sgu_ln_g,
                                                                    sgu_ln_b, w_s[0], b_st, wbm_f, wbs_f, wout_f, mx[2], t, nh)
    g_wout = _wgrad(y, dout, t, "wgrad_out")
    g_wbm = _wgrad(ym, dpm, t, "wgrad_branch_mlstm")
    g_wbs = _wgrad(ys, dps, t, "wgrad_branch_sgu")
    ex_a = [gwup_slots, g_wdown.reshape(N_DEV, dff // N_DEV, d), g_wbm.reshape(N_DEV, d // N_DEV, d),
            g_wbs.reshape(N_DEV, d // N_DEV, d), g_wout.reshape(N_DEV, d // N_DEV, d)]
    ex_a_state, ex_a_tok = _exchange_start(ex_a, True, "grad_exchange_a_start")
    dqkv_f, dg_f, dqkv_b, dg_b = _mlstm_bwd(qk, z_main, zg, bias + ex_a_tok[0:1, :], dhs, hf, hb, states_f, states_b, nh, t)
    dz_qkv, g_convqk = _qkv_conv_bwd(dqkv_f, dqkv_b, z_main, convw, t, md, qscale)
    dz_g, st_gate = _gate_grad_sum(dg_f, dg_b)
    gwin_slots = _scatter_cols(
        [_wgrad(hn, dz_qkv, s_rows, "wgrad_in_qkv"), _wgrad(hn, dz_g, s_rows, "wgrad_in_gate"), _wgrad(hn, dz_rest, t, "wgrad_in_rest")],
        [(0, 0, 3 * md, 0), (1, 3 * md, 4 * nh, 0), (2, 3 * md + 4 * nh, 5 * d, 0)], n_insh, "scatter_grad_w_in")
    gcq_slots = _scatter_cols([g_convqk], [(0, 0, 2 * md, 0)], nq, "scatter_grad_conv_qk")
    gcf_slots = _scatter_cols([g_wconv9], [(0, 0, dff, 0)], nf, "scatter_grad_ffn_conv")
    ex_b_state, ex_b_tok = _exchange_start([gwin_slots, gcq_slots, gcf_slots], True, "grad_exchange_b_start")
    tk = _pick(md, (1024, 512, 256))
    grad_x, st_n1x = _proj_norm_bwd(
        [(dz_qkv, 0, w_main, 0, 3 * md, tk), (dz_rest, 0, w_main, 3 * md, 5 * d, tk), (dz_g, 0, w_gate, 0, LANES, LANES)],
        x2, 0, norm1_g, mx[1] + ex_b_tok[0:1, 0:1], dh1, t, "in_proj_bwd")
    (st_n1c,) = _proj_norm_bwd([(dz_qkv, t, w_main, 0, 3 * md, tk), (dz_g, t, w_gate, 0, LANES, LANES)],
                               ctx2, 0, norm1_g, mc[1] + ex_b_tok[0:1, 0:1], None, n_ctx, "in_proj_bwd_ctx")

    rx_a = _exchange_wait(ex_a_state, st_n1c, "grad_exchange_a_wait")
    rx_b = _exchange_wait(ex_b_state, st_n1c, "grad_exchange_b_wait")
    recv = [rx_b[0], rx_a[0], rx_a[2], rx_a[3], rx_a[4], rx_a[1], rx_b[1], rx_b[2]]
    small_parts = [st_n1x[1], st_n1x[2], st_mix[0], st_n2[1], st_n2[2], st_tail[1],
                   st_n1c[1], st_n1c[2],
                   silu_v[0], st_n1x[0] + st_n1c[0], st_gate[0], st_mix[1], st_mix[2], st_mix[3],
                   g_ws.reshape(-1), g_bst[:, :ng].T.reshape(-1), st_n2[0], st_tail[0]]
    small_state, small_tok = _exchange_start([_pack(small_parts, 8)], False, "small_exchange_start")

    shard_w = (w_in, w_up, w_branch_mlstm, w_branch_sgu, w_out, w_down, conv_qk)
    shard_m = (m_w_in, m_w_up, m_w_branch_mlstm, m_w_branch_sgu, m_w_out, m_w_down, m_conv_qk)
    shard_v = (v_w_in, v_w_up, v_w_branch_mlstm, v_w_branch_sgu, v_w_out, v_w_down, v_conv_qk)
    shard_names = ("w_in", "w_up", "w_branch_mlstm", "w_branch_sgu", "w_out", "w_down", "conv_qk")
    shard_out = [_adamw(wa, recv[k], ma, va, "adamw_" + nm, small_tok)
                 for k, (wa, ma, va, nm) in enumerate(zip(shard_w, shard_m, shard_v, shard_names))]
    shard_out.append([b.reshape(w_ffn_conv.shape) for b in
                      _adamw(ffn9, recv[7], m_w_ffn_conv[0].reshape(9, nf), v_w_ffn_conv[0].reshape(9, nf), "adamw_w_ffn_conv",
                             small_tok)])

    (recv_small,) = _exchange_wait(small_state, shard_out[5][1], "small_exchange_wait")
    small_sum = _slot_sum(recv_small).reshape(-1)
    small_slots = recv_small.reshape(N_DEV, -1)
    o_silu, o_n1 = 8 * d, 9 * d
    ncol = N_MOD * d // N_DEV
    dmc_tot = small_sum[6 * d:8 * d][None]
    dmc_pad = jnp.concatenate([dmc_tot, jnp.zeros((1, 4 * d), F32)], axis=1)
    g_wmod, g_bmod, g_cctx = _mod_grads(
        small_slots[:, o_silu:o_silu + d], lax.dynamic_slice_in_dim(small_slots[:, :6 * d], me * ncol, ncol, axis=1),
        small_slots[:, :6 * d], dmc_tot, lax.dynamic_slice_in_dim(dmc_pad, me * ncol, ncol, axis=1), silu_v[1:2], c_ctx[None],
        w_mod_f[:, :2 * d])
    mod_out = _adamw(w_mod, g_wmod, m_w_mod, v_w_mod, "adamw_w_mod")

    def rep(cc, bm, n1, bg, hg, lg, lb, ws, bs, n2, fg):
        return [cc.reshape(-1), bm.reshape(-1), n1.reshape(-1), _pad_lanes(bg.reshape(1, -1)).reshape(-1), hg.reshape(-1),
                lg.reshape(-1), lb.reshape(-1), ws.reshape(-1), bs.reshape(-1), n2.reshape(-1), fg.reshape(-1)]

    o = o_n1
    g_rep_parts = [g_cctx, g_bmod]
    for n in (d, LANES, d, d, d, ng * sc * sc, ng * sc, d, d):
        g_rep_parts.append(small_sum[o:o + n])
        o += n
    rep_shapes = [(d,), (1, N_MOD * d), (1, d), (1, LANES), (1, d), (1, d), (1, d), (1, ng, sc, sc), (1, ng, sc), (1, d), (d,)]
    rep_out = _adamw(
        _pack(rep(c_ctx, b_mod, norm1_g, b_gate, head_norm_g, sgu_ln_g, sgu_ln_b, w_s, b_s, norm2_g, final_g), LANES),
        _pack(g_rep_parts, LANES)[None],
        _pack(rep(m_c_ctx, m_b_mod, m_norm1_g, m_b_gate, m_head_norm_g, m_sgu_ln_g, m_sgu_ln_b, m_w_s, m_b_s, m_norm2_g, m_final_g), LANES),
        _pack(rep(v_c_ctx, v_b_mod, v_norm1_g, v_b_gate, v_head_norm_g, v_sgu_ln_g, v_sgu_ln_b, v_w_s, v_b_s, v_norm2_g, v_final_g), LANES),
        "adamw_replicated")

    def assemble(k):
        r = _unpack(rep_out[k], rep_shapes)
        s = [o[k] for o in shard_out]
        return [r[0], mod_out[k], r[1], r[2], s[0], r[3][:, :4 * nh], s[6], r[4], r[5], r[6], r[7], r[8], s[2], s[3], s[4], r[9],
                s[1], s[7], s[5], r[10]]

    loss = lax.psum(st_tail[2, 0], ("x", "y", "c"))
    outs = [loss, grad_x[None]]
    for k in range(4):
        outs += assemble(k)
    return tuple(outs)
```

```python
import math

import jax
import jax.numpy as jnp
from jax import lax
from jax.experimental import pallas as pl
from jax.experimental.pallas import tpu as pltpu

F32, BF16 = jnp.float32, jnp.bfloat16
EPS = 1e-6
M_INIT = -1e30
NEG = -1e30
GRID_W = 64
LCH = 256
N_MOD = 6
N_DEV = 8
LANES = 128
ADAM_LR, ADAM_B1, ADAM_B2, ADAM_EPS, ADAM_WD, ADAM_STEP = 0.001, 0.9, 0.999, 1e-08, 0.01, 10
GELU_C = math.sqrt(2.0 / math.pi)
GELU_A = 0.044715
VMEM_LIMIT = 56 * 1024 * 1024
HI = lax.Precision.HIGHEST
SDS = jax.ShapeDtypeStruct
MESH_ID = pl.DeviceIdType.MESH


def _pick(n, cands):
    for c in cands:
        if n % c == 0:
            return c
    raise ValueError(f"no block size for {n} in {cands}")


def _cp(*sem):
    return pltpu.CompilerParams(dimension_semantics=sem if sem else None, vmem_limit_bytes=VMEM_LIMIT)


def _sigmoid(x):
    return 0.5 * jnp.tanh(0.5 * x) + 0.5


def _split3(x):
    hi = x.astype(BF16)
    r = x - hi.astype(F32)
    mid = r.astype(BF16)
    return hi, mid, (r - mid.astype(F32)).astype(BF16)


def _mask_dot(mask_b, x):
    hi, mid, lo = _split3(x)
    return (_dot(mask_b, lo) + _dot(mask_b, mid)) + _dot(mask_b, hi)


def _mask_dot_t(mask_b, x):
    hi, mid, lo = _split3(x)
    return (_dot_tn(mask_b, lo) + _dot_tn(mask_b, mid)) + _dot_tn(mask_b, hi)


def _gelu(x):
    return x * (0.5 * (1.0 + jnp.tanh(GELU_C * x * (1.0 + GELU_A * (x * x)))))


def _gelu_and_grad(x):
    x2 = x * x
    t = jnp.tanh(GELU_C * x * (1.0 + GELU_A * x2))
    half = 0.5 * (1.0 + t)
    return x * half, half + (0.5 * GELU_C) * x * (1.0 - t * t) * (1.0 + 3.0 * GELU_A * x2)


def _log_sigmoid(x):
    return jnp.minimum(x, 0.0) - jnp.log(1.0 + jnp.exp(-jnp.abs(x)))


def _dot(a, b):
    return jnp.dot(a, b, preferred_element_type=F32)


def _dot_nt(a, b):
    return lax.dot_general(a, b, (((1,), (1,)), ((), ())), preferred_element_type=F32)


def _dot_tn(a, b):
    return lax.dot_general(a, b, (((0,), (0,)), ((), ())), preferred_element_type=F32)


def _bf(x):
    return x.astype(BF16)


def _allgather(arrs):
    na = len(arrs)

    def body(*refs):
        x_refs, o_refs = refs[:na], refs[na:2 * na]
        send_sems, recv_sems, local_sems = refs[2 * na:]
        x, y, c = lax.axis_index("x"), lax.axis_index("y"), lax.axis_index("c")
        me, sibling = (x, y, c), (x, y, 1 - c)
        chips = [(1 - x, y), (x, 1 - y), (1 - x, 1 - y)]

        def copy(a, k, block, to, src=None):
            slot = o_refs[a].at[4 * block[0] + 2 * block[1] + block[2]]
            return pltpu.make_async_remote_copy(
                src_ref=slot if src is None else src, dst_ref=slot, send_sem=send_sems.at[7 * a + k],
                recv_sem=recv_sems.at[7 * a + k], device_id=to, device_id_type=MESH_ID)

        mine = [pltpu.make_async_copy(x_refs[a], o_refs[a].at[4 * x + 2 * y + c], local_sems.at[a]) for a in range(na)]
        for cp in mine:
            cp.start()
        first = []
        for a in range(na):
            first.append(copy(a, 0, me, sibling, src=x_refs[a]))
            first += [copy(a, 1 + j, me, (*chip, c), src=x_refs[a]) for j, chip in enumerate(chips)]
        for cp in first:
            cp.start()
        passed = []
        for j, chip in enumerate(chips):
            for a in range(na):
                copy(a, 1 + j, (*chip, c), me).wait_recv()
                passed.append(copy(a, 4 + j, (*chip, c), sibling))
                passed[-1].start()
        for a in range(na):
            copy(a, 0, sibling, me).wait_recv()
            for j, chip in enumerate(chips):
                copy(a, 4 + j, (*chip, 1 - c), me).wait_recv()
        for cp in first + passed:
            cp.wait_send()
        for cp in mine:
            cp.wait()

    anyspec = pl.BlockSpec(memory_space=pl.ANY)
    return pl.pallas_call(
        body, name="weights_allgather",
        out_shape=[SDS((N_DEV,) + a.shape, a.dtype) for a in arrs],
        in_specs=[anyspec] * na, out_specs=[anyspec] * na,
        scratch_shapes=[pltpu.SemaphoreType.DMA((7 * na,)), pltpu.SemaphoreType.DMA((7 * na,)), pltpu.SemaphoreType.DMA((na,))],
    )(*arrs)


_HBM_SPEC = pl.BlockSpec(memory_space=pltpu.HBM)
_SEM_SPEC = pl.BlockSpec(memory_space=pltpu.SEMAPHORE)
_EFFECT = pltpu.SideEffectType.DATAFLOW_SIDE_EFFECTING


def _peer_list(x, y, c):
    out = []
    for k in range(1, N_DEV):
        px = 1 - x if k & 4 else x
        py = 1 - y if k & 2 else y
        pc = 1 - c if k & 1 else c
        out.append(((px, py, pc), 4 * px + 2 * py + pc))
    return out


def _split_copies(src, land, send_sems, recv_sems, per_dest, receive):
    x, y, c = lax.axis_index("x"), lax.axis_index("y"), lax.axis_index("c")
    me = 4 * x + 2 * y + c
    out = []
    for k, (peer, pidx) in enumerate(_peer_list(x, y, c)):
        for a in range(len(src)):
            out.append(pltpu.make_async_remote_copy(
                src_ref=src[a].at[pidx] if per_dest else src[a], dst_ref=land[a].at[pidx if receive else me],
                send_sem=send_sems.at[7 * a + k], recv_sem=recv_sems.at[7 * a + k], device_id=peer, device_id_type=MESH_ID))
    return out


def _own_copies(src, land, own_sems, per_dest):
    me = 4 * lax.axis_index("x") + 2 * lax.axis_index("y") + lax.axis_index("c")
    return [pltpu.make_async_copy(src[a].at[me] if per_dest else src[a], land[a].at[me], own_sems.at[a]) for a in range(len(src))]


def _exchange_start(arrs, per_dest, name):
    na = len(arrs)
    land_shapes = [a.shape if per_dest else (N_DEV,) + a.shape for a in arrs]
    lands = [pltpu.with_memory_space_constraint(lax.empty(s, a.dtype), pltpu.HBM) for s, a in zip(land_shapes, arrs)]

    def body(*refs):
        src, land = refs[:na], refs[na:2 * na]
        send_sems, recv_sems, own_sems, token = refs[2 * na], refs[2 * na + 1], refs[2 * na + 2], refs[-1]
        for cp in _split_copies(src, land, send_sems, recv_sems, per_dest, False) + _own_copies(src, land, own_sems, per_dest):
            cp.start()
        token[...] = jnp.zeros_like(token)

    outs = pl.pallas_call(
        body, name=name,
        out_shape=[pltpu.SemaphoreType.DMA((7 * na,)), pltpu.SemaphoreType.DMA((7 * na,)), pltpu.SemaphoreType.DMA((na,))]
        + [pltpu.HBM(a.shape, a.dtype) for a in arrs] + [pltpu.HBM(s, a.dtype) for s, a in zip(land_shapes, arrs)]
        + [SDS((8, LANES), F32)],
        in_specs=[_HBM_SPEC] * (2 * na),
        out_specs=[_SEM_SPEC] * 3 + [_HBM_SPEC] * (2 * na) + [pl.BlockSpec(memory_space=pltpu.VMEM)],
        input_output_aliases={k: 3 + k for k in range(2 * na)},
        compiler_params=pltpu.CompilerParams(has_side_effects=_EFFECT),
    )(*[pltpu.with_memory_space_constraint(a, pltpu.HBM) for a in arrs], *lands)
    return (na, per_dest, outs[:-1]), outs[-1]


def _exchange_wait(state, after, name):
    na, per_dest, started = state

    def body(*refs):
        src, land = refs[:na], refs[na:2 * na]
        send_sems, recv_sems, own_sems = refs[2 * na], refs[2 * na + 1], refs[2 * na + 2]
        for cp in _split_copies(src, land, send_sems, recv_sems, per_dest, True):
            cp.wait_send()
            cp.wait_recv()
        for cp in _own_copies(src, land, own_sems, per_dest):
            cp.wait()

    bufs = started[3:]
    outs = pl.pallas_call(
        body, name=name,
        out_shape=[pltpu.HBM(b.shape, b.dtype) for b in bufs],
        in_specs=[_HBM_SPEC] * (2 * na) + [_SEM_SPEC] * 3 + [pl.BlockSpec(memory_space=pl.ANY)],
        out_specs=[_HBM_SPEC] * (2 * na),
        input_output_aliases={k: k for k in range(2 * na)},
        compiler_params=pltpu.CompilerParams(has_side_effects=_EFFECT),
    )(*bufs, started[0], started[1], started[2], after)
    return outs[na:]


def _col_pieces(n, segments):
    out = []
    for j in range(N_DEV):
        lo, hi = j * n, (j + 1) * n
        for (k, s0, w, c0) in segments:
            a, b = max(lo, s0), min(hi, s0 + w)
            if a < b:
                out.append((j, a - lo, b - lo, k, c0 + a - s0, c0 + b - s0))
    return out


def _assemble_cols(slots, groups, out_widths, name):
    r, p = slots.shape[1], slots.shape[2]
    tb = _pick(r, (128, 64, 32, 16, 8))
    covered = [0] * len(out_widths)
    for (_, n, segs) in groups:
        for (k, _, w, _) in segs:
            covered[k] += w

    def body(s_ref, *o_refs):
        for k, wd in enumerate(out_widths):
            if covered[k] < wd:
                o_refs[k][...] = jnp.zeros_like(o_refs[k])
        for (off, n, segs) in groups:
            for (j, a0, a1, k, d0, d1) in _col_pieces(n, segs):
                o_refs[k][:, d0:d1] = s_ref[j, :, off + a0:off + a1]

    return pl.pallas_call(
        body, name=name, grid=(r // tb,), in_specs=[pl.BlockSpec((N_DEV, tb, p), lambda i: (0, i, 0))],
        out_specs=[pl.BlockSpec((tb, w), lambda i: (i, 0)) for w in out_widths],
        out_shape=[SDS((r, w), slots.dtype) for w in out_widths], compiler_params=_cp("arbitrary"))(slots)


def _scatter_cols(pieces, segments, n, name):
    r = pieces[0].shape[0]
    tb = _pick(r, (128, 64, 32, 16, 8))

    def body(*refs):
        p_refs, o_ref = refs[:-1], refs[-1]
        for (j, a0, a1, k, d0, d1) in _col_pieces(n, segments):
            o_ref[j, :, a0:a1] = p_refs[k][:, d0:d1]

    return pl.pallas_call(
        body, name=name, grid=(r // tb,), in_specs=[pl.BlockSpec((tb, a.shape[1]), lambda i: (i, 0)) for a in pieces],
        out_specs=pl.BlockSpec((N_DEV, tb, n), lambda i: (0, i, 0)), out_shape=SDS((N_DEV, r, n), pieces[0].dtype),
        compiler_params=_cp("arbitrary"))(*pieces)


def _modulation(cvec, w_mod, b_mod):
    d, n = w_mod.shape

    def body(c_ref, w_ref, b_ref, s_ref, o_ref):
        cv = c_ref[...]
        s = cv * _sigmoid(cv)
        s_ref[...] = s
        o_ref[...] = _dot(_bf(s), w_ref[...]) + b_ref[...]

    return pl.pallas_call(body, name="modulation", out_shape=(SDS((8, d), F32), SDS((8, n), F32)),
                          compiler_params=_cp())(cvec, w_mod, b_mod)


def _norm_mod_proj(x_arr, g, shsc, w_main, w_gate, rows_total, row0, filled, name):
    m_rows, d = x_arr.shape
    n = w_main.shape[1]
    tb = _pick(m_rows, (1024, 256))
    cb = _pick(n, (2048, 1408, 1024, 768, 512, 384, 256, 128))
    gate = w_gate is not None
    nout = 3 if gate else 2
    nin = 5 if gate else 4
    rb = row0 // tb

    def body(*refs):
        x_ref, g_ref, ss_ref, wm_ref = refs[:4]
        wg_ref = refs[4] if gate else None
        outs = refs[len(refs) - 1 - nout:len(refs) - 1]
        hn_ref, z_ref = outs[0], outs[1]
        hn_sc = refs[-1]

        @pl.when(pl.program_id(1) == 0)
        def _():
            x = x_ref[...]
            r = lax.rsqrt(jnp.mean(x * x, axis=-1, keepdims=True) + EPS)
            hb = _bf((x * r * g_ref[...]) * (1.0 + ss_ref[1:2, :]) + ss_ref[0:1, :])
            hn_sc[...] = hb
            hn_ref[...] = hb
            if gate:
                outs[2][...] = _dot(hb, wg_ref[...])

        z_ref[...] = _bf(_dot(hn_sc[...], wm_ref[:, pl.ds(pl.multiple_of(pl.program_id(1) * cb, cb), cb)]))

    in_specs = [pl.BlockSpec((tb, d), lambda i, j: (i, 0)), pl.BlockSpec((1, d), lambda i, j: (0, 0)),
                pl.BlockSpec((2, d), lambda i, j: (0, 0)), _resident((d, n))]
    out_specs = [pl.BlockSpec((tb, d), lambda i, j: (rb + i, 0)), pl.BlockSpec((tb, cb), lambda i, j: (rb + i, j))]
    out_shape = [SDS((rows_total, d), BF16), SDS((rows_total, n), BF16)]
    args = [x_arr, g, shsc, w_main]
    if gate:
        in_specs.append(pl.BlockSpec((d, LANES), lambda i, j: (0, 0)))
        out_specs.append(pl.BlockSpec((tb, LANES), lambda i, j: (rb + i, 0)))
        out_shape.append(SDS((rows_total, LANES), F32))
        args.append(w_gate)
    aliases = {}
    if filled is not None:
        in_specs += [pl.BlockSpec(memory_space=pl.ANY)] * nout
        args += list(filled)
        aliases = {nin + k: k for k in range(nout)}
    return pl.pallas_call(
        body, name=name, grid=(m_rows // tb, n // cb), in_specs=in_specs, out_specs=out_specs, out_shape=out_shape,
        input_output_aliases=aliases, scratch_shapes=[pltpu.VMEM((tb, d), BF16)],
        compiler_params=_cp("arbitrary", "arbitrary"))(*args)


def _seg_masks(row, t_rows, s_rows):
    prev_ok = (row != 0) & (row != t_rows)
    next_ok = (row != t_rows - 1) & (row != s_rows - 1)
    return prev_ok, next_ok


def _shift_rows(z, halo_prev, halo_next, tb):
    loc = lax.broadcasted_iota(jnp.int32, (tb, 1), 0)
    zp = jnp.where(loc == 0, halo_prev, pltpu.roll(z, 1, 0))
    zn = jnp.where(loc == tb - 1, halo_next, pltpu.roll(z, tb - 1, 0))
    return zp, zn


def _qk_conv(z_main, conv_w, t_rows, md, qscale):
    s_rows = z_main.shape[0]
    tb = _pick(s_rows, (1280, 1024, 256))
    cb = _pick(md, (512, 256, 128))
    nb8 = tb // 8

    def body(zm, zp, zn, w_ref, o_ref):
        i, j = pl.program_id(0), pl.program_id(1)
        z = zm[...].astype(F32)
        zprev, znext = _shift_rows(z, zp[7:8, :].astype(F32), zn[0:1, :].astype(F32), tb)
        row = i * tb + lax.broadcasted_iota(jnp.int32, (tb, 1), 0)
        prev_ok, next_ok = _seg_masks(row, t_rows, s_rows)
        pre = (w_ref[0:1, :] * jnp.where(prev_ok, zprev, 0.0) + w_ref[1:2, :] * z
               + w_ref[2:3, :] * jnp.where(next_ok, znext, 0.0))
        scale = jnp.where(j * cb < md, qscale, 1.0)
        o_ref[...] = _bf(pre * _sigmoid(pre) * scale)

    return pl.pallas_call(
        body, name="qk_conv", grid=(s_rows // tb, 2 * md // cb),
        in_specs=[pl.BlockSpec((tb, cb), lambda i, j: (i, j)),
                  pl.BlockSpec((8, cb), lambda i, j: (jnp.maximum(i * nb8 - 1, 0), j)),
                  pl.BlockSpec((8, cb), lambda i, j: (jnp.minimum((i + 1) * nb8, s_rows // 8 - 1), j)),
                  pl.BlockSpec((8, cb), lambda i, j: (0, j))],
        out_specs=pl.BlockSpec((tb, cb), lambda i, j: (i, j)),
        out_shape=SDS((s_rows, 2 * md), BF16), compiler_params=_cp("arbitrary", "arbitrary"))(z_main, z_main, z_main, conv_w)


def _chunk_gates(gates, bias, rev):
    ln = gates.shape[0]
    gz = gates + bias
    logf = _log_sigmoid(gz)
    r_id = lax.broadcasted_iota(jnp.int32, (ln, ln), 0)
    c_id = lax.broadcasted_iota(jnp.int32, (ln, ln), 1)
    mask = (c_id >= r_id) if rev else (c_id <= r_id)
    mb = mask.astype(F32).astype(BF16)
    b_all = _mask_dot(mb, logf)
    g_all = jnp.sum(logf, axis=0, keepdims=True)
    return gz, b_all, b_all.T, gz.T, g_all, mask, mb


def _head_weights(b_col, b_row, i_row, m_in, mask):
    d = jnp.where(mask, b_col - b_row + i_row, NEG)
    inter = b_col + m_in
    m_row = jnp.maximum(inter, jnp.max(d, axis=1, keepdims=True))
    return jnp.exp(d - m_row), jnp.exp(inter - m_row), m_row


def _head_state_coeffs(g, b_col, i_col, m_in):
    a = g - b_col + i_col
    m_new = jnp.maximum(g + m_in, jnp.max(a, axis=0, keepdims=True))
    return jnp.exp(g + m_in - m_new), jnp.exp(a - m_new), m_new


def _mlstm_fwd(qk, z_main, zg, bias, nh):
    s_rows = qk.shape[0]
    md = qk.shape[1] // 2
    dh = md // nh
    nc = s_rows // LCH
    ln = LCH

    def chunk_f(i):
        return jnp.where(i == 0, nc - 1, i - 1)

    def chunk_b(i):
        return jnp.where(i == 0, nc - 1, nc - 1 - i)

    def body(qf, kf, vf, gf, qb, kb, vb, gb, bias_ref, hf_ref, hb_ref, cf_ref, nf_ref, mf_ref, cb_ref, nb_ref, mb_ref,
             c_sc, n_sc, m_sc):
        i = pl.program_id(0)

        @pl.when(i == 0)
        def _():
            c_sc[...] = jnp.zeros_like(c_sc)
            n_sc[...] = jnp.zeros_like(n_sc)
            m_sc[...] = jnp.full(m_sc.shape, M_INIT, F32)

        sides = ((qf, kf, vf, gf, hf_ref, cf_ref, nf_ref, mf_ref), (qb, kb, vb, gb, hb_ref, cb_ref, nb_ref, mb_ref))
        gates = [_chunk_gates(s[3][...], bias_ref[...], dr == 1) for dr, s in enumerate(sides)]
        units = []
        for dr, (q_ref, k_ref, v_ref, _, h_ref, c_out, n_out, m_out) in enumerate(sides):
            gz, b_all, b_t, g_t, g_all, mask, _ = gates[dr]
            for h in range(nh):
                ci, cf = 2 * dr * nh + h, (2 * dr + 1) * nh + h
                sl = slice(h * dh, (h + 1) * dh)
                u = dict(dr=dr, h=h, sl=sl, h_ref=h_ref, q=q_ref[:, sl], k=k_ref[:, sl], v=v_ref[:, sl],
                         c_in=c_sc[dr, h], n_in=n_sc[dr, h, 0:1, :], m_in=m_sc[dr, h, 0:1, 0:1],
                         b_col=b_all[:, cf:cf + 1], i_col=gz[:, ci:ci + 1], g=g_all[:, cf:cf + 1])
                c_out[sl, :] = u["c_in"]
                n_out[:, sl] = n_sc[dr, h]
                m_out[h] = m_sc[dr, h]
                u["w"], u["w_int"], u["m_row"] = _head_weights(u["b_col"], b_t[cf:cf + 1, :], g_t[ci:ci + 1, :], u["m_in"], mask)
                u["qk"] = _dot_nt(u["q"], u["k"])
                units.append(u)
        for u in units:
            u["s_mat"] = u["qk"] * u["w"]
            u["qc"] = _dot(u["q"], _bf(u["c_in"]))
            u["a_old"], u["coef"], u["m_new"] = _head_state_coeffs(u["g"], u["b_col"], u["i_col"], u["m_in"])
            u["kw"] = u["k"].astype(F32) * u["coef"]
        for u in units:
            u["sv"] = _dot(_bf(u["s_mat"]), u["v"])
            u["kv"] = _dot_tn(_bf(u["kw"]), u["v"])
        for u in units:
            dr, h = u["dr"], u["h"]
            num = u["sv"] + u["w_int"] * u["qc"]
            den = (jnp.sum(u["s_mat"], axis=1, keepdims=True)
                   + u["w_int"] * jnp.sum(u["q"].astype(F32) * u["n_in"], axis=1, keepdims=True))
            u["h_ref"][:, u["sl"]] = _bf(num / jnp.maximum(jnp.abs(den), jnp.exp(-u["m_row"])))
            c_sc[dr, h] = u["a_old"] * u["c_in"] + u["kv"]
            n_sc[dr, h] = jnp.broadcast_to(u["a_old"] * u["n_in"] + jnp.sum(u["kw"], axis=0, keepdims=True), (8, dh))
            m_sc[dr, h] = jnp.broadcast_to(u["m_new"], (8, LANES))

    def tok(cfn, col):
        return pl.BlockSpec((ln, md), lambda i: (cfn(i), col))

    def gat(cfn):
        return pl.BlockSpec((ln, LANES), lambda i: (cfn(i), 0))

    def st(cfn, shape):
        return pl.BlockSpec((None,) + shape, lambda i: (cfn(i),) + (0,) * len(shape))

    st_shapes = ((nh * dh, dh), (8, md), (nh, 8, LANES))
    return pl.pallas_call(
        body, name="mlstm_fwd", grid=(nc,),
        in_specs=[tok(chunk_f, 0), tok(chunk_f, 1), tok(chunk_f, 2), gat(chunk_f),
                  tok(chunk_b, 0), tok(chunk_b, 1), tok(chunk_b, 2), gat(chunk_b),
                  pl.BlockSpec((1, LANES), lambda i: (0, 0))],
        out_specs=[tok(chunk_f, 0), tok(chunk_b, 0)] + [st(chunk_f, s) for s in st_shapes] + [st(chunk_b, s) for s in st_shapes],
        out_shape=[SDS((s_rows, md), BF16)] * 2 + [SDS((nc,) + s, F32) for s in st_shapes] * 2,
        scratch_shapes=[pltpu.VMEM((2, nh, dh, dh), F32), pltpu.VMEM((2, nh, 8, dh), F32), pltpu.VMEM((2, nh, 8, LANES), F32)],
        compiler_params=_cp("arbitrary"))(qk, qk, z_main, zg, qk, qk, z_main, zg, bias)


def _head_rms(hs, nh, dh):
    parts, scales = [], []
    for h in range(nh):
        hh = hs[:, h * dh:(h + 1) * dh]
        r = lax.rsqrt(jnp.mean(hh * hh, axis=-1, keepdims=True) + EPS)
        parts.append(hh * r)
        scales.append(r)
    return jnp.concatenate(parts, axis=1), scales


def _layer_norm(v):
    vc = v - jnp.mean(v, axis=-1, keepdims=True)
    r = lax.rsqrt(jnp.mean(vc * vc, axis=-1, keepdims=True) + EPS)
    return vc * r, r


def _sgu_mix(vnb, ws_ref, bs_ref, tb, ng, gd, sc):
    rows = []
    for ch in range(tb // sc):
        cols = []
        for g in range(ng):
            blk = vnb[ch * sc:(ch + 1) * sc, g * gd:(g + 1) * gd]
            cols.append(_dot(_bf(ws_ref[g]), blk) + bs_ref[:, g:g + 1])
        rows.append(jnp.concatenate(cols, axis=1))
    return jnp.concatenate(rows, axis=0)


def _mixer_fwd(hf, hb, z_main, xs, hg, lng, lnb, w_s, b_st, wbm, wbs, wout, mx2, t_rows, nh):
    d = xs.shape[1]
    ng, sc = w_s.shape[0], w_s.shape[1]
    dh, gd = d // nh, d // ng
    tb = _pick(t_rows, (256,))

    def body(hf_ref, hb_ref, zo, zu, zv, zgm, zgg, x_ref, hg_ref, lng_ref, lnb_ref, ws_ref, bs_ref, wbm_ref, wbs_ref,
             wo_ref, mx2_ref, h1_ref, ym_ref, ys_ref, pm_ref, ps_ref, y_ref, out_ref):
        hs = hf_ref[...].astype(F32) + hb_ref[...].astype(F32)
        hn, _ = _head_rms(hs, nh, dh)
        ym = _bf(_sigmoid(zo[...].astype(F32)) * (hn * hg_ref[...]))
        ym_ref[...] = ym
        vhat, _ = _layer_norm(_gelu(zv[...].astype(F32)))
        vnb = _bf(vhat * lng_ref[...] + lnb_ref[...])
        ys = _bf(_gelu(zu[...].astype(F32)) * _sgu_mix(vnb, ws_ref, bs_ref, tb, ng, gd, sc))
        ys_ref[...] = ys
        pm = _dot(ym, wbm_ref[...])
        ps = _dot(ys, wbs_ref[...])
        pm_ref[...] = _bf(pm)
        ps_ref[...] = _bf(ps)
        y = _bf(_sigmoid(zgm[...].astype(F32)) * pm + _sigmoid(zgg[...].astype(F32)) * ps)
        y_ref[...] = y
        out = _dot(y, wo_ref[...])
        out_ref[...] = _bf(out)
        h1_ref[...] = x_ref[...] + mx2_ref[...] * out

    def tok(col):
        return pl.BlockSpec((tb, d), lambda i: (i, col))

    def full(shape):
        return pl.BlockSpec(shape, lambda i: (0,) * len(shape))

    return pl.pallas_call(
        body, name="mixer_fwd", grid=(t_rows // tb,),
        in_specs=[tok(0), tok(0), tok(3), tok(4), tok(5), tok(6), tok(7), tok(0), full((1, d)), full((1, d)), full((1, d)),
                  full((ng, sc, sc)), full((sc, LANES)), full((d, d)), full((d, d)), full((d, d)), full((1, d))],
        out_specs=[tok(0)] * 7,
        out_shape=[SDS((t_rows, d), F32)] + [SDS((t_rows, d), BF16)] * 6,
        compiler_params=_cp("arbitrary"))(hf, hb, z_main, z_main, z_main, z_main, z_main, xs, hg, lng, lnb, w_s, b_st,
                                          wbm, wbs, wout, mx2)


def _resident(shape):
    return pl.BlockSpec(shape, lambda *_: (0,) * len(shape), pipeline_mode=pl.Buffered(1))


def _grid_taps(a_ext, n_ext):
    col = lax.broadcasted_iota(jnp.int32, (n_ext, 1), 0) % GRID_W
    left = jnp.where(col != 0, pltpu.roll(a_ext, 1, 0), 0.0)
    right = jnp.where(col != GRID_W - 1, pltpu.roll(a_ext, n_ext - 1, 0), 0.0)
    return left, right


def _with_halo(prev, main, nxt, i, ni, tb):
    ext = jnp.concatenate([prev, main, nxt], axis=0).astype(F32)
    pos = lax.broadcasted_iota(jnp.int32, (tb + 2 * GRID_W, 1), 0)
    inside = ((pos >= GRID_W) | (i > 0)) & ((pos < tb + GRID_W) | (i < ni - 1))
    return jnp.where(inside, ext, 0.0)


def _halo_specs(tb, cb, t_rows, col0=0):
    nh64 = tb // GRID_W
    return [pl.BlockSpec((tb, cb), lambda i, j: (i, col0 + j)),
            pl.BlockSpec((GRID_W, cb), lambda i, j: (jnp.maximum(i * nh64 - 1, 0), col0 + j)),
            pl.BlockSpec((GRID_W, cb), lambda i, j: (jnp.minimum((i + 1) * nh64, t_rows // GRID_W - 1), col0 + j))]


def _ffn_tail(ab, w_conv9, w_down, h1, mx5, gfin, target, dff):
    t_rows, d = h1.shape
    tb = _pick(t_rows, (256,))
    cb = _pick(dff, (1408, 256, 128))
    ni, nj = t_rows // tb, dff // cb
    n_ext = tb + 2 * GRID_W

    def body(am, ap, an, b_ref, wc_ref, wd_ref, h1_ref, mx5_ref, gf_ref, tg_ref, ac_ref, f_ref, dh2_ref, dffn_ref, st_ref, acc):
        i, j = pl.program_id(0), pl.program_id(1)
        a_ext = _with_halo(ap[...], am[...], an[...], i, ni, tb)
        left, right = _grid_taps(a_ext, n_ext)
        conv = jnp.zeros((tb, cb), F32)
        for di in range(3):
            o = di * GRID_W
            conv = conv + (wc_ref[3 * di:3 * di + 1, :] * left[o:o + tb] + wc_ref[3 * di + 1:3 * di + 2, :] * a_ext[o:o + tb]
                           + wc_ref[3 * di + 2:3 * di + 3, :] * right[o:o + tb])
        ac_ref[...] = _bf(conv)
        fb = _bf(conv * _sigmoid(conv) * b_ref[...].astype(F32))
        f_ref[...] = fb

        @pl.when(j == 0)
        def _():
            acc[...] = jnp.zeros_like(acc)

        @pl.when((i == 0) & (j == 0))
        def _():
            st_ref[...] = jnp.zeros_like(st_ref)

        acc[...] += _dot(fb, wd_ref[pl.ds(pl.multiple_of(j * cb, cb), cb), :])

        @pl.when(j == nj - 1)
        def _():
            ffn = acc[...]
            h2 = h1_ref[...] + mx5_ref[...] * ffn
            r = lax.rsqrt(jnp.mean(h2 * h2, axis=-1, keepdims=True) + EPS)
            xn = h2 * r
            e = xn * gf_ref[...] - tg_ref[...]
            loss = 0.5 * jnp.sum(jnp.sum(e * e, axis=1, keepdims=True), axis=0, keepdims=True) / d
            dy = e * (1.0 / d)
            dxn = dy * gf_ref[...]
            dh2 = r * (dxn - xn * jnp.mean(dxn * xn, axis=-1, keepdims=True))
            dh2_ref[...] = dh2
            dffn_ref[...] = _bf(dh2 * mx5_ref[...])
            st_ref[...] += jnp.concatenate(
                [jnp.sum(dy * xn, axis=0, keepdims=True), jnp.sum(dh2 * ffn, axis=0, keepdims=True),
                 jnp.broadcast_to(loss, (1, d)), jnp.zeros((5, d), F32)], axis=0)

    def tokd():
        return pl.BlockSpec((tb, d), lambda i, j: (i, 0))

    def rowd():
        return pl.BlockSpec((1, d), lambda i, j: (0, 0))

    return pl.pallas_call(
        body, name="ffn_tail", grid=(ni, nj),
        in_specs=_halo_specs(tb, cb, t_rows) + [pl.BlockSpec((tb, cb), lambda i, j: (i, nj + j)),
                                                pl.BlockSpec((16, cb), lambda i, j: (0, j)),
                                                _resident((dff, d)), tokd(), rowd(), rowd(), tokd()],
        out_specs=[pl.BlockSpec((tb, cb), lambda i, j: (i, j)), pl.BlockSpec((tb, cb), lambda i, j: (i, j)), tokd(), tokd(),
                   pl.BlockSpec((8, d), lambda i, j: (0, 0))],
        out_shape=[SDS((t_rows, dff), BF16), SDS((t_rows, dff), BF16), SDS((t_rows, d), F32), SDS((t_rows, d), BF16),
                   SDS((8, d), F32)],
        scratch_shapes=[pltpu.VMEM((tb, d), F32)],
        compiler_params=_cp("arbitrary", "arbitrary"))(ab, ab, ab, ab, w_conv9, w_down, h1, mx5, gfin, target)


def _ffn_bwd_gate(dffn, w_down, aconv, ab, dff):
    t_rows, d = dffn.shape
    tb = _pick(t_rows, (512,))
    cb = _pick(dff, (1408, 256, 128))
    nj = dff // cb

    def body(g_ref, wd_ref, ac_ref, b_ref, db_ref, dac_ref):
        df = _dot_nt(g_ref[...], wd_ref[pl.ds(pl.multiple_of(pl.program_id(1) * cb, cb), cb), :])
        ac = ac_ref[...].astype(F32)
        sa = _sigmoid(ac)
        db_ref[...] = _bf(df * ac * sa)
        dac_ref[...] = _bf(df * b_ref[...].astype(F32) * (sa * (1.0 + ac * (1.0 - sa))))

    blk = pl.BlockSpec((tb, cb), lambda i, j: (i, j))
    return pl.pallas_call(
        body, name="ffn_bwd_gate", grid=(t_rows // tb, nj),
        in_specs=[pl.BlockSpec((tb, d), lambda i, j: (i, 0)), _resident((dff, d)), blk,
                  pl.BlockSpec((tb, cb), lambda i, j: (i, nj + j))],
        out_specs=[blk, blk], out_shape=[SDS((t_rows, dff), BF16)] * 2,
        compiler_params=_cp("arbitrary", "arbitrary"))(dffn, w_down, aconv, ab)


def _ffn_conv_bwd(dac, ab, w_conv9, dff):
    t_rows = dac.shape[0]
    tb = _pick(t_rows, (512, 256))
    cb = _pick(dff, (1408, 256, 128))
    ni, nj = t_rows // tb, dff // cb
    n_ext = tb + 2 * GRID_W
    nh64 = tb // GRID_W

    def body(dm, dp, dn, am, ap, an, wc_ref, da_ref, gw_ref):
        i = pl.program_id(1)
        d_ext = _with_halo(dp[...], dm[...], dn[...], i, ni, tb)
        a_ext = _with_halo(ap[...], am[...], an[...], i, ni, tb)
        d_left, d_right = _grid_taps(d_ext, n_ext)
        a_left, a_right = _grid_taps(a_ext, n_ext)
        dmain = d_ext[GRID_W:GRID_W + tb]
        da = jnp.zeros((tb, cb), F32)
        rows = []
        for di in range(3):
            o = (2 - di) * GRID_W
            da = da + (wc_ref[3 * di:3 * di + 1, :] * d_right[o:o + tb] + wc_ref[3 * di + 1:3 * di + 2, :] * d_ext[o:o + tb]
                       + wc_ref[3 * di + 2:3 * di + 3, :] * d_left[o:o + tb])
            o = di * GRID_W
            for tap in (a_left, a_ext, a_right):
                rows.append(jnp.sum(dmain * tap[o:o + tb], axis=0, keepdims=True))
        da_ref[...] = _bf(da)

        @pl.when(i == 0)
        def _():
            gw_ref[...] = jnp.zeros_like(gw_ref)

        gw_ref[...] += jnp.concatenate(rows + [jnp.zeros((7, cb), F32)], axis=0)

    def halo(col0):
        return [pl.BlockSpec((tb, cb), lambda j, i: (i, col0 + j)),
                pl.BlockSpec((GRID_W, cb), lambda j, i: (jnp.maximum(i * nh64 - 1, 0), col0 + j)),
                pl.BlockSpec((GRID_W, cb), lambda j, i: (jnp.minimum((i + 1) * nh64, t_rows // GRID_W - 1), col0 + j))]

    return pl.pallas_call(
        body, name="ffn_conv_bwd", grid=(nj, ni),
        in_specs=halo(0) + halo(0) + [pl.BlockSpec((16, cb), lambda j, i: (0, j))],
        out_specs=[pl.BlockSpec((tb, cb), lambda j, i: (i, j)), pl.BlockSpec((16, cb), lambda j, i: (0, j))],
        out_shape=[SDS((t_rows, dff), BF16), SDS((16, dff), F32)],
        compiler_params=_cp("arbitrary", "arbitrary"))(dac, dac, dac, ab, ab, ab, w_conv9)


def _proj_norm_bwd(pairs, x_arr, x_row0, g, scale, resid, m_rows, name, row_blocks=(1024, 256)):
    d = x_arr.shape[1]
    tm = _pick(m_rows, row_blocks)
    te = 256
    ni = m_rows // tm
    starts, total = [], 0
    for (_, _, _, _, k_p, tk_p) in pairs:
        starts.append(total)
        total += k_p // tk_p
    npairs = len(pairs)
    has_dx = resid is not None

    def body(*refs):
        a_refs, b_refs = refs[0:2 * npairs:2], refs[1:2 * npairs:2]
        rest = refs[2 * npairs:]
        if has_dx:
            x_ref, g_ref, sc_ref, r_ref, dx_ref, st_ref, acc = rest
        else:
            x_ref, g_ref, sc_ref, st_ref, acc = rest
        i, k = pl.program_id(0), pl.program_id(1)

        @pl.when(k == 0)
        def _():
            acc[...] = jnp.zeros_like(acc)

        @pl.when((i == 0) & (k == 0))
        def _():
            st_ref[...] = jnp.zeros_like(st_ref)

        for p in range(npairs):
            nk = pairs[p][4] // pairs[p][5]

            @pl.when((k >= starts[p]) & (k < starts[p] + nk))
            def _(p=p):
                acc[...] += _dot_nt(a_refs[p][...], b_refs[p][...])

        @pl.when(k == total - 1)
        def _():
            sums = [jnp.zeros((1, d), F32)] * 3
            for r0 in range(0, tm, te):
                rows = slice(r0, r0 + te)
                dhn = acc[rows, :]
                x = x_ref[rows, :]
                r = lax.rsqrt(jnp.mean(x * x, axis=-1, keepdims=True) + EPS)
                xn = x * r
                dmod = dhn * (1.0 + sc_ref[...])
                dxn = dmod * g_ref[...]
                if has_dx:
                    dx_ref[rows, :] = r * (dxn - xn * jnp.mean(dxn * xn, axis=-1, keepdims=True)) + r_ref[rows, :]
                sums = [sums[0] + jnp.sum(dmod * xn, axis=0, keepdims=True), sums[1] + jnp.sum(dhn, axis=0, keepdims=True),
                        sums[2] + jnp.sum(dhn * (xn * g_ref[...]), axis=0, keepdims=True)]
            st_ref[...] += jnp.concatenate(sums + [jnp.zeros((5, d), F32)], axis=0)

    in_specs, args = [], []
    for p, (a, a_row0, b, b_col0, k_p, tk_p) in enumerate(pairs):
        nk, s0, ar, bc = k_p // tk_p, starts[p], a_row0 // tm, b_col0 // tk_p

        def kk(k, s0=s0, nk=nk):
            return jnp.clip(k - s0, 0, nk - 1)

        in_specs.append(pl.BlockSpec((tm, tk_p), lambda i, k, ar=ar, kk=kk: (ar + i, kk(k))))
        in_specs.append(pl.BlockSpec((d, tk_p), lambda i, k, bc=bc, kk=kk: (0, bc + kk(k)),
                                     pipeline_mode=pl.Buffered(1 if nk == 1 else 2)))
        args += [a, b]
    xr = x_row0 // tm
    in_specs += [pl.BlockSpec((tm, d), lambda i, k: (xr + i, 0)), pl.BlockSpec((1, d), lambda i, k: (0, 0)),
                 pl.BlockSpec((1, d), lambda i, k: (0, 0))]
    args += [x_arr, g, scale]
    out_specs, out_shape = [], []
    if has_dx:
        in_specs.append(pl.BlockSpec((tm, d), lambda i, k: (i, 0)))
        args.append(resid)
        out_specs.append(pl.BlockSpec((tm, d), lambda i, k: (i, 0)))
        out_shape.append(SDS((m_rows, d), F32))
    out_specs.append(pl.BlockSpec((8, d), lambda i, k: (0, 0)))
    out_shape.append(SDS((8, d), F32))
    return pl.pallas_call(
        body, name=name, grid=(ni, total), in_specs=in_specs, out_specs=out_specs, out_shape=out_shape,
        scratch_shapes=[pltpu.VMEM((tm, d), F32)], compiler_params=_cp("arbitrary", "arbitrary"))(*args)


def _wgrad(a, b, k_rows, name):
    m, n = a.shape[1], b.shape[1]
    tm = _pick(m, (1408, 1024, 512, 384, 256, 128))
    tn = _pick(n, (3072, 2816, 2560, 1408, 1024, 768, 512, 384, 256, 128))
    tk = _pick(k_rows, (1280, 1024, 256))
    nk = k_rows // tk

    def body(a_ref, b_ref, o_ref, acc):
        k = pl.program_id(2)

        @pl.when(k == 0)
        def _():
            acc[...] = jnp.zeros_like(acc)

        acc[...] += _dot_tn(a_ref[...], b_ref[...])

        @pl.when(k == nk - 1)
        def _():
            o_ref[...] = _bf(acc[...])

    return pl.pallas_call(
        body, name=name, grid=(m // tm, n // tn, nk),
        in_specs=[pl.BlockSpec((tk, tm), lambda i, j, k: (k, i)), pl.BlockSpec((tk, tn), lambda i, j, k: (k, j))],
        out_specs=pl.BlockSpec((tm, tn), lambda i, j, k: (i, j)), out_shape=SDS((m, n), BF16),
        scratch_shapes=[pltpu.VMEM((tm, tn), F32)],
        compiler_params=_cp("arbitrary", "arbitrary", "arbitrary"))(a, b)


def _lane_put(col, lane_idx):
    lane = lax.broadcasted_iota(jnp.int32, (1, LANES), 1)
    return jnp.where(lane == lane_idx, col, 0.0)


def _mixer_bwd(dh1, out, hf, hb, z_main, pm, ps, hg, lng, lnb, w_s, b_st, wbm, wbs, wout, mx2, t_rows, nh):
    d = dh1.shape[1]
    ng, sc = w_s.shape[0], w_s.shape[1]
    dh, gd = d // nh, d // ng
    tb = _pick(t_rows, (256,))

    def body(dh1_ref, out_ref, hf_ref, hb_ref, zo, zu, zv, zgm, zgg, pm_ref, ps_ref, hg_ref, lng_ref, lnb_ref, ws_ref, bs_ref,
             wbm_ref, wbs_ref, wo_ref, mx2_ref, dz_ref, dhs_ref, dout_ref, dpm_ref, dps_ref, st_ref, dws_ref, dbs_ref):
        i = pl.program_id(0)

        @pl.when(i == 0)
        def _():
            st_ref[...] = jnp.zeros_like(st_ref)
            dws_ref[...] = jnp.zeros_like(dws_ref)
            dbs_ref[...] = jnp.zeros_like(dbs_ref)

        dh1v = dh1_ref[...]
        doutb = _bf(dh1v * mx2_ref[...])
        dout_ref[...] = doutb
        d_mx2 = jnp.sum(dh1v * out_ref[...].astype(F32), axis=0, keepdims=True)
        dy = _dot_nt(doutb, wo_ref[...])
        sgm, sgg = _sigmoid(zgm[...].astype(F32)), _sigmoid(zgg[...].astype(F32))
        dpmb, dpsb = _bf(dy * sgm), _bf(dy * sgg)
        dpm_ref[...] = dpmb
        dps_ref[...] = dpsb
        dz_ref[:, 3 * d:4 * d] = _bf(dy * pm_ref[...].astype(F32) * sgm * (1.0 - sgm))
        dz_ref[:, 4 * d:5 * d] = _bf(dy * ps_ref[...].astype(F32) * sgg * (1.0 - sgg))
        dym = _dot_nt(dpmb, wbm_ref[...])
        dys = _dot_nt(dpsb, wbs_ref[...])
        hs = hf_ref[...].astype(F32) + hb_ref[...].astype(F32)
        hn, scales = _head_rms(hs, nh, dh)
        so = _sigmoid(zo[...].astype(F32))
        dz_ref[:, 0:d] = _bf(dym * (hn * hg_ref[...]) * so * (1.0 - so))
        dhmn = dym * so
        d_hg = jnp.sum(dhmn * hn, axis=0, keepdims=True)
        dhn = dhmn * hg_ref[...]
        for h in range(nh):
            sl = slice(h * dh, (h + 1) * dh)
            dhs_ref[:, sl] = _bf(scales[h] * (dhn[:, sl] - hn[:, sl] * jnp.mean(dhn[:, sl] * hn[:, sl], axis=-1, keepdims=True)))
        zuv, zvv = zu[...].astype(F32), zv[...].astype(F32)
        u, du_dz = _gelu_and_grad(zuv)
        vg, dvg_dz = _gelu_and_grad(zvv)
        vhat, rstd = _layer_norm(vg)
        vnb = _bf(vhat * lng_ref[...] + lnb_ref[...])
        mixed = _sgu_mix(vnb, ws_ref, bs_ref, tb, ng, gd, sc)
        dz_ref[:, d:2 * d] = _bf(dys * mixed * du_dz)
        dmix = dys * u
        rows = []
        dbs = jnp.zeros((sc, LANES), F32)
        for ch in range(tb // sc):
            cols = []
            for g in range(ng):
                dm = dmix[ch * sc:(ch + 1) * sc, g * gd:(g + 1) * gd]
                dmb = _bf(dm)
                dws_ref[g] += _dot_nt(dmb, vnb[ch * sc:(ch + 1) * sc, g * gd:(g + 1) * gd])
                dbs = dbs + _lane_put(jnp.sum(dm, axis=1, keepdims=True), g)
                cols.append(_dot_tn(_bf(ws_ref[g]), dmb))
            rows.append(jnp.concatenate(cols, axis=1))
        dbs_ref[...] += dbs
        dvn = jnp.concatenate(rows, axis=0)
        d_lng = jnp.sum(dvn * vhat, axis=0, keepdims=True)
        d_lnb = jnp.sum(dvn, axis=0, keepdims=True)
        dvh = dvn * lng_ref[...]
        dvg = rstd * (dvh - jnp.mean(dvh, axis=-1, keepdims=True) - vhat * jnp.mean(dvh * vhat, axis=-1, keepdims=True))
        dz_ref[:, 2 * d:3 * d] = _bf(dvg * dvg_dz)
        st_ref[...] += jnp.concatenate([d_mx2, d_hg, d_lng, d_lnb, jnp.zeros((4, d), F32)], axis=0)

    def tok(col):
        return pl.BlockSpec((tb, d), lambda i: (i, col))

    def full(shape):
        return pl.BlockSpec(shape, lambda i: (0,) * len(shape))

    return pl.pallas_call(
        body, name="mixer_bwd", grid=(t_rows // tb,),
        in_specs=[tok(0), tok(0), tok(0), tok(0), tok(3), tok(4), tok(5), tok(6), tok(7), tok(0), tok(0), full((1, d)),
                  full((1, d)), full((1, d)), full((ng, sc, sc)), full((sc, LANES)), full((d, d)), full((d, d)), full((d, d)),
                  full((1, d))],
        out_specs=[pl.BlockSpec((tb, 5 * d), lambda i: (i, 0)), tok(0), tok(0), tok(0), tok(0), full((8, d)), full((ng, sc, sc)),
                   full((sc, LANES))],
        out_shape=[SDS((t_rows, 5 * d), BF16)] + [SDS((t_rows, d), BF16)] * 4 + [SDS((8, d), F32), SDS((ng, sc, sc), F32),
                                                                                SDS((sc, LANES), F32)],
        compiler_params=_cp("arbitrary"))(dh1, out, hf, hb, z_main, z_main, z_main, z_main, z_main, pm, ps, hg, lng, lnb, w_s,
                                          b_st, wbm, wbs, wout, mx2)


def _mlstm_bwd(qk, z_main, zg, bias, dhs, hf, hb, states_f, states_b, nh, t_rows):
    s_rows = qk.shape[0]
    md = qk.shape[1] // 2
    dh = md // nh
    nc = s_rows // LCH
    nx = t_rows // LCH
    ln = LCH

    def chunk_f(i):
        return jnp.where(i == nc - 1, nc - 1, nc - 2 - i)

    def chunk_b(i):
        return jnp.where(i == nc - 1, nc - 1, i)

    def body(qf, kf, vf, gf, dhf, hsf, cf, nf, mf_, qb, kb, vb, gb, dhb, hsb, cb, nb, mb_, bias_ref, dqkvf_ref, dgf_ref, dqkvb_ref,
             dgb_ref, dc_sc, dn_sc):
        i = pl.program_id(0)
        is_ctx = i == nc - 1

        @pl.when(i == 0)
        def _():
            dc_sc[...] = jnp.zeros_like(dc_sc)
            dn_sc[...] = jnp.zeros_like(dn_sc)

        sides = ((qf, kf, vf, gf, dhf, hsf, cf, nf, mf_, dqkvf_ref, dgf_ref), (qb, kb, vb, gb, dhb, hsb, cb, nb, mb_, dqkvb_ref, dgb_ref))
        gates = [_chunk_gates(s[3][...], bias_ref[...], dr == 1) for dr, s in enumerate(sides)]
        units = []
        for dr, (q_ref, k_ref, v_ref, _, dh_ref, hs_ref, c_ref, n_ref, m_ref, dqkv_ref, _) in enumerate(sides):
            gz, b_all, b_t, g_t, g_all, mask, _ = gates[dr]
            for h in range(nh):
                ci, cfl = 2 * dr * nh + h, (2 * dr + 1) * nh + h
                sl = slice(h * dh, (h + 1) * dh)
                u = dict(dr=dr, h=h, sl=sl, ci=ci, cfl=cfl, dqkv_ref=dqkv_ref, q=q_ref[:, sl], k=k_ref[:, sl], v=v_ref[:, sl],
                         dhv=jnp.where(is_ctx, 0.0, dh_ref[:, sl].astype(F32)), hs=hs_ref[:, sl].astype(F32),
                         c_in=c_ref[sl, :], n_in=n_ref[0:1, sl], m_in=m_ref[h, 0:1, 0:1],
                         b_col=b_all[:, cfl:cfl + 1], i_col=gz[:, ci:ci + 1], g=g_all[:, cfl:cfl + 1],
                         dc_new=dc_sc[dr, h], dn_new=dn_sc[dr, h, 0:1, :])
                u["qf32"], u["kf32"] = u["q"].astype(F32), u["k"].astype(F32)
                u["w"], u["w_int"], u["m_row"] = _head_weights(u["b_col"], b_t[cfl:cfl + 1, :], g_t[ci:ci + 1, :], u["m_in"], mask)
                u["qk"] = _dot_nt(u["q"], u["k"])
                units.append(u)
        for u in units:
            s_mat = u["qk"] * u["w"]
            u["s_mat"], u["sb"], u["cb16"], u["dcb"] = s_mat, _bf(s_mat), _bf(u["c_in"]), _bf(u["dc_new"])
            den = jnp.sum(s_mat, axis=1, keepdims=True) + u["w_int"] * jnp.sum(u["qf32"] * u["n_in"], axis=1, keepdims=True)
            e_m = jnp.exp(-u["m_row"])
            dnm = jnp.maximum(jnp.abs(den), e_m)
            hdh = jnp.sum(u["hs"] * u["dhv"], axis=1, keepdims=True)
            u["dden"] = jnp.where(jnp.abs(den) > e_m, -(hdh / dnm) * jnp.sign(den), 0.0)
            u["dnum_b"] = _bf(u["dhv"] / dnm)
            u["a_old"], u["coef"], _ = _head_state_coeffs(u["g"], u["b_col"], u["i_col"], u["m_in"])
            u["dsm"] = _dot_nt(u["dnum_b"], u["v"])
            u["qct"] = _dot_nt(u["dnum_b"], u["cb16"])
            u["vdc"] = _dot_nt(u["v"], u["dcb"])
        for u in units:
            ds = u["dsm"] + u["dden"]
            u["pb"] = _bf(u["w"] * ds)
            u["gmat"] = u["s_mat"] * ds
            u["dv1"] = _dot_tn(u["sb"], u["dnum_b"])
            u["dv2"] = _dot(_bf(u["kf32"] * u["coef"]), u["dcb"])
            u["dcu"] = _dot_tn(_bf(u["qf32"] * u["w_int"]), u["dnum_b"])
        for u in units:
            u["dq1"] = _dot(u["pb"], u["k"])
            u["dk1"] = _dot_tn(u["pb"], u["q"])
        acc = [dict(x1=jnp.zeros((ln, LANES), F32), x2=jnp.zeros((ln, LANES), F32), dig=jnp.zeros((ln, LANES), F32),
                    e_row=jnp.zeros((1, LANES), F32)) for _ in range(2)]
        for u in units:
            dr, h, sl, a = u["dr"], u["h"], u["sl"], acc[u["dr"]]
            dq_inter = u["w_int"] * (u["qct"] + u["dden"] * u["n_in"])
            dk_state = u["coef"] * (u["vdc"] + u["dn_new"])
            u["dqkv_ref"][:, sl] = _bf(u["dq1"] + dq_inter)
            u["dqkv_ref"][:, md + h * dh:md + (h + 1) * dh] = _bf(u["dk1"] + dk_state)
            u["dqkv_ref"][:, 2 * md + h * dh:2 * md + (h + 1) * dh] = _bf(u["dv1"] + u["dv2"])
            row_intra = jnp.sum(u["gmat"], axis=1, keepdims=True)
            col_intra = jnp.sum(u["gmat"].T, axis=1, keepdims=True)
            row_inter = jnp.sum(u["qf32"] * dq_inter, axis=1, keepdims=True)
            col_inter = jnp.sum(u["kf32"] * dk_state, axis=1, keepdims=True)
            e_old = u["a_old"] * (jnp.sum(jnp.sum(u["dc_new"] * u["c_in"], axis=1, keepdims=True), axis=0, keepdims=True)
                                  + jnp.sum(u["dn_new"] * u["n_in"], axis=1, keepdims=True))
            a["x1"] = a["x1"] + _lane_put(row_intra - col_intra + row_inter, u["cfl"])
            a["x2"] = a["x2"] + _lane_put(col_inter, u["cfl"])
            a["e_row"] = a["e_row"] + _lane_put(e_old, u["cfl"])
            a["dig"] = a["dig"] + _lane_put(col_intra + col_inter, u["ci"])
            dc_sc[dr, h] = u["a_old"] * u["dc_new"] + u["dcu"]
            dn_sc[dr, h] = jnp.broadcast_to(
                u["a_old"] * u["dn_new"] + jnp.sum(u["qf32"] * (u["w_int"] * u["dden"]), axis=0, keepdims=True), (8, dh))
        for dr, s in enumerate(sides):
            gz, mfl, a = gates[dr][0], gates[dr][6], acc[dr]
            dlogf = _mask_dot_t(mfl, a["x1"]) + _mask_dot(mfl, a["x2"]) - a["x2"] + a["e_row"]
            s[10][...] = a["dig"] + dlogf / (1.0 + jnp.exp(gz))

    def tok(cfn, col):
        return pl.BlockSpec((ln, md), lambda i: (cfn(i), col))

    def dht(cfn):
        return pl.BlockSpec((ln, md), lambda i: (jnp.minimum(cfn(i), nx - 1), 0))

    def gat(cfn):
        return pl.BlockSpec((ln, LANES), lambda i: (cfn(i), 0))

    def st(cfn, shape):
        return pl.BlockSpec((None,) + shape, lambda i: (cfn(i),) + (0,) * len(shape))

    st_shapes = ((nh * dh, dh), (8, md), (nh, 8, LANES))

    def side(cfn):
        return [tok(cfn, 0), tok(cfn, 1), tok(cfn, 2), gat(cfn), dht(cfn), tok(cfn, 0)] + [st(cfn, s) for s in st_shapes]

    def outs(cfn):
        return [pl.BlockSpec((ln, 3 * md), lambda i: (cfn(i), 0)), gat(cfn)]

    return pl.pallas_call(
        body, name="mlstm_bwd", grid=(nc,),
        in_specs=side(chunk_f) + side(chunk_b) + [pl.BlockSpec((1, LANES), lambda i: (0, 0))],
        out_specs=outs(chunk_f) + outs(chunk_b),
        out_shape=[SDS((s_rows, 3 * md), BF16), SDS((s_rows, LANES), F32)] * 2,
        scratch_shapes=[pltpu.VMEM((2, nh, dh, dh), F32), pltpu.VMEM((2, nh, 8, dh), F32)],
        compiler_params=_cp("arbitrary"))(qk, qk, z_main, zg, dhs, hf, *states_f, qk, qk, z_main, zg, dhs, hb, *states_b, bias)


def _qkv_conv_bwd(dqkv_f, dqkv_b, z_main, conv_w, t_rows, md, qscale):
    s_rows = z_main.shape[0]
    tb = _pick(s_rows, (1280, 1024, 256))
    cb = _pick(md, (512, 256, 128))
    ni, nj, ncq = s_rows // tb, 3 * md // cb, 2 * md // cb
    nb8 = tb // 8
    n_ext = tb + 16

    def body(fm, fp, fn, bm, bp, bn, zm, zp, zn, w_ref, dz_ref, gw_ref):
        j, i = pl.program_id(0), pl.program_id(1)

        @pl.when(j < ncq)
        def _():
            z = jnp.concatenate([zp[...], zm[...], zn[...]], axis=0).astype(F32)
            dqk = (jnp.concatenate([fp[...], fm[...], fn[...]], axis=0).astype(F32)
                   + jnp.concatenate([bp[...], bm[...], bn[...]], axis=0).astype(F32)) * jnp.where(j * cb < md, qscale, 1.0)
            row = i * tb - 8 + lax.broadcasted_iota(jnp.int32, (n_ext, 1), 0)
            prev_ok, next_ok = _seg_masks(row, t_rows, s_rows)
            zprev = jnp.where(prev_ok, pltpu.roll(z, 1, 0), 0.0)
            znext = jnp.where(next_ok, pltpu.roll(z, n_ext - 1, 0), 0.0)
            pre = w_ref[0:1, :] * zprev + w_ref[1:2, :] * z + w_ref[2:3, :] * znext
            sg = _sigmoid(pre)
            dpre = dqk * (sg * (1.0 + pre * (1.0 - sg)))
            dz = (w_ref[1:2, :] * dpre + w_ref[0:1, :] * jnp.where(next_ok, pltpu.roll(dpre, n_ext - 1, 0), 0.0)
                  + w_ref[2:3, :] * jnp.where(prev_ok, pltpu.roll(dpre, 1, 0), 0.0))
            dz_ref[...] = _bf(dz[8:8 + tb])
            dm = dpre[8:8 + tb]

            @pl.when(i == 0)
            def _():
                gw_ref[...] = jnp.zeros_like(gw_ref)

            gw_ref[...] += jnp.concatenate(
                [jnp.sum(dm * zprev[8:8 + tb], axis=0, keepdims=True), jnp.sum(dm * z[8:8 + tb], axis=0, keepdims=True),
                 jnp.sum(dm * znext[8:8 + tb], axis=0, keepdims=True), jnp.zeros((5, cb), F32)], axis=0)

        @pl.when(j >= ncq)
        def _():
            dz_ref[...] = _bf(fm[...].astype(F32) + bm[...].astype(F32))

    def halo(clampj):
        def cj(j):
            return jnp.minimum(j, ncq - 1) if clampj else j
        return [pl.BlockSpec((tb, cb), lambda j, i: (i, cj(j))),
                pl.BlockSpec((8, cb), lambda j, i: (jnp.maximum(i * nb8 - 1, 0), cj(j))),
                pl.BlockSpec((8, cb), lambda j, i: (jnp.minimum((i + 1) * nb8, s_rows // 8 - 1), cj(j)))]

    return pl.pallas_call(
        body, name="qkv_conv_bwd", grid=(nj, ni),
        in_specs=halo(False) + halo(False) + halo(True) + [pl.BlockSpec((8, cb), lambda j, i: (0, jnp.minimum(j, ncq - 1)))],
        out_specs=[pl.BlockSpec((tb, cb), lambda j, i: (i, j)), pl.BlockSpec((8, cb), lambda j, i: (0, jnp.minimum(j, ncq - 1)))],
        out_shape=[SDS((s_rows, 3 * md), BF16), SDS((8, 2 * md), F32)],
        compiler_params=_cp("arbitrary", "arbitrary"))(dqkv_f, dqkv_f, dqkv_f, dqkv_b, dqkv_b, dqkv_b, z_main, z_main, z_main, conv_w)


def _gate_grad_sum(dg_f, dg_b):
    s_rows = dg_f.shape[0]
    tb = _pick(s_rows, (1280, 1024, 256))

    def body(a_ref, b_ref, o_ref, st_ref):
        @pl.when(pl.program_id(0) == 0)
        def _():
            st_ref[...] = jnp.zeros_like(st_ref)

        s = a_ref[...] + b_ref[...]
        o_ref[...] = _bf(s)
        st_ref[...] += jnp.concatenate([jnp.sum(s, axis=0, keepdims=True), jnp.zeros((7, LANES), F32)], axis=0)

    blk = pl.BlockSpec((tb, LANES), lambda i: (i, 0))
    return pl.pallas_call(
        body, name="gate_grad_sum", grid=(s_rows // tb,), in_specs=[blk, blk],
        out_specs=[blk, pl.BlockSpec((8, LANES), lambda i: (0, 0))],
        out_shape=[SDS((s_rows, LANES), BF16), SDS((8, LANES), F32)], compiler_params=_cp("arbitrary"))(dg_f, dg_b)


def _mod_grads(silu_slots, dmx_sh, dmx_slots, dmc_tot, dmc_sh, silu_cctx, c_ctx, w_mod_c):
    d = silu_slots.shape[1]
    ncol, n6 = dmx_sh.shape[1], dmx_slots.shape[1]

    def body(ss_ref, dsh_ref, dsl_ref, dct_ref, dcs_ref, sc_ref, c_ref, w_ref, gw_ref, gb_ref, gc_ref):
        a = jnp.concatenate([ss_ref[...], sc_ref[...], jnp.zeros((7, d), F32)], axis=0)
        b = jnp.concatenate([dsh_ref[...], dcs_ref[...], jnp.zeros((7, ncol), F32)], axis=0)
        gw_ref[0] = lax.dot_general(a, b, (((0,), (0,)), ((), ())), preferred_element_type=F32, precision=HI)
        dct = dct_ref[...]
        gb_ref[...] = jnp.sum(dsl_ref[...], axis=0, keepdims=True) + jnp.concatenate(
            [dct, jnp.zeros((1, n6 - dct.shape[1]), F32)], axis=1)
        t = _dot_nt(_bf(jnp.broadcast_to(dct, (8, dct.shape[1]))), w_ref[...])
        cv = c_ref[...]
        s = _sigmoid(cv)
        gc_ref[...] = t[0:1, :] * (s * (1.0 + cv * (1.0 - s)))

    return pl.pallas_call(body, name="mod_grads", out_shape=[SDS((1, d, ncol), F32), SDS((1, n6), F32), SDS((1, d), F32)],
                          compiler_params=_cp())(silu_slots, dmx_sh, dmx_slots, dmc_tot, dmc_sh, silu_cctx, c_ctx, w_mod_c)


def _slot_sum(slots):
    ns, r = slots.shape[0], slots.shape[1]
    tb = _pick(r, (1024, 512, 256, 128, 64, 32, 16, 8))

    def body(s_ref, o_ref):
        acc = s_ref[0]
        for k in range(1, ns):
            acc = acc + s_ref[k]
        o_ref[...] = acc

    return pl.pallas_call(
        body, name="slot_sum", grid=(r // tb,), in_specs=[pl.BlockSpec((ns, tb, LANES), lambda i: (0, i, 0))],
        out_specs=pl.BlockSpec((tb, LANES), lambda i: (i, 0)), out_shape=SDS((r, LANES), F32),
        compiler_params=_cp("arbitrary"))(slots)


def _adamw(w, gslots, m, v, name, after=None):
    lead = ((None,), (0,)) if w.ndim == 3 else ((), ())
    r, cdim = w.shape[-2:]
    ns, rg = gslots.shape[0], gslots.shape[1]
    tb = r if (rg != r or r % 8) else _pick(r, (128, 64, 32, 16, 8))
    bc1, bc2 = 1.0 - ADAM_B1 ** ADAM_STEP, 1.0 - ADAM_B2 ** ADAM_STEP

    def body(w_ref, g_ref, m_ref, v_ref, *rest):
        go_ref, d_ref, mo_ref, vo_ref = rest[-4:]
        g = g_ref[0, 0:tb, :].astype(F32)
        for k in range(1, ns):
            g = g + g_ref[k, 0:tb, :].astype(F32)
        mn = ADAM_B1 * m_ref[...] + (1.0 - ADAM_B1) * g
        vn = ADAM_B2 * v_ref[...] + (1.0 - ADAM_B2) * (g * g)
        go_ref[...] = g
        mo_ref[...] = mn
        vo_ref[...] = vn
        d_ref[...] = -ADAM_LR * ((mn / bc1) / (jnp.sqrt(vn / bc2) + ADAM_EPS) + ADAM_WD * w_ref[...])

    blk = pl.BlockSpec(lead[0] + (tb, cdim), lambda i: lead[1] + (i, 0))
    gblk = pl.BlockSpec((ns, tb if rg == r else rg, cdim), lambda i: (0, i, 0))
    extra_specs = [] if after is None else [pl.BlockSpec((8, LANES), lambda i: (0, 0))]
    extra_args = [] if after is None else [after]
    return pl.pallas_call(
        body, name=name, grid=(r // tb,), in_specs=[blk, gblk, blk, blk] + extra_specs,
        out_specs=[blk] * 4, out_shape=[SDS(w.shape, F32)] * 4, compiler_params=_cp("arbitrary"))(w, gslots, m, v, *extra_args)


def _pack(parts, row_mult):
    flat = jnp.concatenate([p.reshape(-1) for p in parts])
    n = flat.shape[0]
    rows = -(-n // LANES)
    rows = -(-rows // row_mult) * row_mult
    return jnp.pad(flat, (0, rows * LANES - n)).reshape(rows, LANES)


def _unpack(buf, shapes):
    flat = buf.reshape(-1)
    out, off = [], 0
    for s in shapes:
        n = math.prod(s)
        out.append(flat[off:off + n].reshape(s))
        off += n
    return out


def _pad_cols(a, width):
    return jnp.pad(a, ((0, 0), (0, width - a.shape[1])))


def _pad_lanes(a):
    return _pad_cols(a, LANES)


def _up128(n):
    return -(-n // LANES) * LANES


def kernel(x, c, ctx, c_ctx, w_mod, b_mod, norm1_g, w_in, b_gate, conv_qk, head_norm_g, sgu_ln_g, sgu_ln_b, w_s, b_s, w_branch_mlstm, w_branch_sgu, w_out, norm2_g, w_up, w_ffn_conv, w_down, final_g, loss_target, m_c_ctx, m_w_mod, m_b_mod, m_norm1_g, m_w_in, m_b_gate, m_conv_qk, m_head_norm_g, m_sgu_ln_g, m_sgu_ln_b, m_w_s, m_b_s, m_w_branch_mlstm, m_w_branch_sgu, m_w_out, m_norm2_g, m_w_up, m_w_ffn_conv, m_w_down, m_final_g, v_c_ctx, v_w_mod, v_b_mod, v_norm1_g, v_w_in, v_b_gate, v_conv_qk, v_head_norm_g, v_sgu_ln_g, v_sgu_ln_b, v_w_s, v_b_s, v_w_branch_mlstm, v_w_branch_sgu, v_w_out, v_norm2_g, v_w_up, v_w_ffn_conv, v_w_down, v_final_g):
    t, d = x.shape[1], x.shape[2]
    n_ctx = ctx.shape[1]
    s_rows = t + n_ctx
    nh = b_gate.shape[1] // 4
    md = head_norm_g.shape[1]
    dh = md // nh
    ng, sc = w_s.shape[1], w_s.shape[2]
    dff = w_down.shape[1] * N_DEV
    n_in = w_in.shape[2] * N_DEV
    assert md == d and sgu_ln_g.shape[1] == d and n_ctx == LCH and t % LCH == 0 and t % (8 * GRID_W) == 0
    assert n_in == 8 * d + 4 * nh and 4 * nh <= LANES
    me = 4 * lax.axis_index("x") + 2 * lax.axis_index("y") + lax.axis_index("c")

    n_mod, n_insh, n_upsh = w_mod.shape[2], w_in.shape[2], w_up.shape[2]
    p_mod, p_in, p_up = _up128(n_mod), _up128(n_insh), _up128(n_upsh)
    nq, nf = conv_qk.shape[2], w_ffn_conv.shape[3]
    ffn9 = w_ffn_conv[0].reshape(9, nf)
    colpack = jnp.concatenate([_pad_cols(_bf(w_mod[0]), p_mod), _pad_cols(_bf(w_in[0]), p_in)], axis=1)
    convpack = jnp.concatenate([jnp.pad(conv_qk[0], ((0, 13), (0, 0))), jnp.pad(ffn9, ((0, 7), (0, 0)))], axis=1)
    g_col, g_conv = _allgather([colpack, convpack])
    w_mod_f, w_main, w_gate = _assemble_cols(
        g_col, [(0, n_mod, [(0, 0, N_DEV * n_mod, 0)]),
                (p_mod, n_insh, [(1, 0, 3 * md, 0), (2, 3 * md, 4 * nh, 0), (1, 3 * md + 4 * nh, 5 * d, 3 * md)])],
        [N_MOD * d, 8 * d, LANES], "assemble_weights")
    convw, wconv9 = _assemble_cols(g_conv, [(0, nq, [(0, 0, N_DEV * nq, 0)]), (nq, nf, [(1, 0, N_DEV * nf, 0)])],
                                   [N_DEV * nq, N_DEV * nf], "assemble_conv_weights")
    zero = jnp.minimum(jnp.abs(g_conv[0, 0, 0]), 0.0)
    late_w = [_pad_cols(_bf(w_up[0] + zero), p_up), _bf(w_branch_mlstm[0]), _bf(w_branch_sgu[0]), _bf(w_out[0]), _bf(w_down[0])]
    late_state, late_tok = _exchange_start(late_w, False, "late_weights_start")

    cvec = jnp.concatenate([c, c_ctx[None], jnp.zeros((6, d), F32)], axis=0) + late_tok[0:1, 0:1]
    silu_v, mod = _modulation(cvec, w_mod_f, b_mod)
    mx = [mod[0:1, k * d:(k + 1) * d] for k in range(N_MOD)]
    mc = [mod[1:2, k * d:(k + 1) * d] for k in range(2)]
    x2, ctx2 = x[0], ctx[0]
    in_x = _norm_mod_proj(x2, norm1_g, jnp.concatenate([mx[0], mx[1]], axis=0), w_main, w_gate, s_rows, 0, None, "in_proj")
    hn, z_main, zg = _norm_mod_proj(ctx2, norm1_g, jnp.concatenate([mc[0], mc[1]], axis=0), w_main, w_gate, s_rows, t, in_x,
                                    "in_proj_ctx")
    qscale = dh ** -0.5
    qk = _qk_conv(z_main, convw, t, md, qscale)
    bias = _pad_lanes(b_gate)
    fwd = _mlstm_fwd(qk, z_main, zg, bias, nh)
    hf, hb, states_f, states_b = fwd[0], fwd[1], fwd[2:5], fwd[5:8]
    g_up, g_bm, g_bs, g_out, g_down = _exchange_wait(late_state, fwd[4], "late_weights_wait")
    (w_up_f,) = _assemble_cols(g_up, [(0, n_upsh, [(0, 0, 2 * dff, 0)])], [2 * dff], "assemble_w_up")
    wbm_f, wbs_f, wout_f = (g.reshape(d, d) for g in (g_bm, g_bs, g_out))
    w_down_f = g_down.reshape(dff, d)
    b_st = _pad_lanes(b_s[0].T)
    h1, ym, ys, pm, ps, y, out = _mixer_fwd(hf, hb, z_main, x2, head_norm_g, sgu_ln_g, sgu_ln_b, w_s[0], b_st, wbm_f, wbs_f,
                                            wout_f, mx[2], t, nh)
    hn2, ab = _norm_mod_proj(h1, norm2_g, jnp.concatenate([mx[3], mx[4]], axis=0), w_up_f, None, t, 0, None, "up_proj")
    aconv, f, dh2, dffn, st_tail = _ffn_tail(ab, wconv9, w_down_f, h1, mx[5], final_g[None], loss_target[0], dff)

    db, dac = _ffn_bwd_gate(dffn, w_down_f, aconv, ab, dff)
    da, g_wconv9 = _ffn_conv_bwd(dac, ab, wconv9, dff)
    g_wdown = _wgrad(f, dffn, t, "wgrad_down")
    gwup_slots = _scatter_cols([_wgrad(hn2, da, t, "wgrad_up_a"), _wgrad(hn2, db, t, "wgrad_up_b")],
                               [(0, 0, dff, 0), (1, dff, dff, 0)], n_upsh, "scatter_grad_w_up")
    dh1, st_n2 = _proj_norm_bwd([(da, 0, w_up_f, 0, dff, dff), (db, 0, w_up_f, dff, dff, dff)], h1, 0, norm2_g, mx[4], dh2, t,
                                "up_proj_bwd", (512, 256))
    dz_rest, dhs, dout, dpm, dps, st_mix, g_ws, g_bst = _mixer_bwd(dh1, out, hf, hb, z_main, pm, ps, head_norm_g, sgu_ln_g,
                                                                    sgu_ln_b, w_s[0], b_st, wbm_f, wbs_f, wout_f, mx[2], t, nh)
    g_wout = _wgrad(y, dout, t, "wgrad_out")
    g_wbm = _wgrad(ym, dpm, t, "wgrad_branch_mlstm")
    g_wbs = _wgrad(ys, dps, t, "wgrad_branch_sgu")
    ex_a = [gwup_slots, g_wdown.reshape(N_DEV, dff // N_DEV, d), g_wbm.reshape(N_DEV, d // N_DEV, d),
            g_wbs.reshape(N_DEV, d // N_DEV, d), g_wout.reshape(N_DEV, d // N_DEV, d)]
    ex_a_state, ex_a_tok = _exchange_start(ex_a, True, "grad_exchange_a_start")
    dqkv_f, dg_f, dqkv_b, dg_b = _mlstm_bwd(qk, z_main, zg, bias + ex_a_tok[0:1, :], dhs, hf, hb, states_f, states_b, nh, t)
    dz_qkv, g_convqk = _qkv_conv_bwd(dqkv_f, dqkv_b, z_main, convw, t, md, qscale)
    dz_g, st_gate = _gate_grad_sum(dg_f, dg_b)
    gwin_slots = _scatter_cols(
        [_wgrad(hn, dz_qkv, s_rows, "wgrad_in_qkv"), _wgrad(hn, dz_g, s_rows, "wgrad_in_gate"), _wgrad(hn, dz_rest, t, "wgrad_in_rest")],
        [(0, 0, 3 * md, 0), (1, 3 * md, 4 * nh, 0), (2, 3 * md + 4 * nh, 5 * d, 0)], n_insh, "scatter_grad_w_in")
    gcq_slots = _scatter_cols([g_convqk], [(0, 0, 2 * md, 0)], nq, "scatter_grad_conv_qk")
    gcf_slots = _scatter_cols([g_wconv9], [(0, 0, dff, 0)], nf, "scatter_grad_ffn_conv")
    ex_b_state, ex_b_tok = _exchange_start([gwin_slots, gcq_slots, gcf_slots], True, "grad_exchange_b_start")
    tk = _pick(md, (1024, 512, 256))
    grad_x, st_n1x = _proj_norm_bwd(
        [(dz_qkv, 0, w_main, 0, 3 * md, tk), (dz_rest, 0, w_main, 3 * md, 5 * d, tk), (dz_g, 0, w_gate, 0, LANES, LANES)],
        x2, 0, norm1_g, mx[1] + ex_b_tok[0:1, 0:1], dh1, t, "in_proj_bwd")
    (st_n1c,) = _proj_norm_bwd([(dz_qkv, t, w_main, 0, 3 * md, tk), (dz_g, t, w_gate, 0, LANES, LANES)],
                               ctx2, 0, norm1_g, mc[1] + ex_b_tok[0:1, 0:1], None, n_ctx, "in_proj_bwd_ctx")

    rx_a = _exchange_wait(ex_a_state, st_n1c, "grad_exchange_a_wait")
    rx_b = _exchange_wait(ex_b_state, st_n1c, "grad_exchange_b_wait")
    recv = [rx_b[0], rx_a[0], rx_a[2], rx_a[3], rx_a[4], rx_a[1], rx_b[1], rx_b[2]]
    small_parts = [st_n1x[1], st_n1x[2], st_mix[0], st_n2[1], st_n2[2], st_tail[1],
                   st_n1c[1], st_n1c[2],
                   silu_v[0], st_n1x[0] + st_n1c[0], st_gate[0], st_mix[1], st_mix[2], st_mix[3],
                   g_ws.reshape(-1), g_bst[:, :ng].T.reshape(-1), st_n2[0], st_tail[0],
                   st_tail[2, :LANES]]
    small_state, small_tok = _exchange_start([_pack(small_parts, 8)], False, "small_exchange_start")

    shard_w = (w_in, w_up, w_branch_mlstm, w_branch_sgu, w_out, w_down, conv_qk)
    shard_m = (m_w_in, m_w_up, m_w_branch_mlstm, m_w_branch_sgu, m_w_out, m_w_down, m_conv_qk)
    shard_v = (v_w_in, v_w_up, v_w_branch_mlstm, v_w_branch_sgu, v_w_out, v_w_down, v_conv_qk)
    shard_names = ("w_in", "w_up", "w_branch_mlstm", "w_branch_sgu", "w_out", "w_down", "conv_qk")
    shard_out = [_adamw(wa, recv[k], ma, va, "adamw_" + nm, small_tok)
                 for k, (wa, ma, va, nm) in enumerate(zip(shard_w, shard_m, shard_v, shard_names))]
    shard_out.append([b.reshape(w_ffn_conv.shape) for b in
                      _adamw(ffn9, recv[7], m_w_ffn_conv[0].reshape(9, nf), v_w_ffn_conv[0].reshape(9, nf), "adamw_w_ffn_conv",
                             small_tok)])

    (recv_small,) = _exchange_wait(small_state, shard_out[5][1], "small_exchange_wait")
    small_sum = _slot_sum(recv_small).reshape(-1)
    small_slots = recv_small.reshape(N_DEV, -1)
    o_silu, o_n1 = 8 * d, 9 * d
    ncol = N_MOD * d // N_DEV
    dmc_tot = small_sum[6 * d:8 * d][None]
    dmc_pad = jnp.concatenate([dmc_tot, jnp.zeros((1, 4 * d), F32)], axis=1)
    g_wmod, g_bmod, g_cctx = _mod_grads(
        small_slots[:, o_silu:o_silu + d], lax.dynamic_slice_in_dim(small_slots[:, :6 * d], me * ncol, ncol, axis=1),
        small_slots[:, :6 * d], dmc_tot, lax.dynamic_slice_in_dim(dmc_pad, me * ncol, ncol, axis=1), silu_v[1:2], c_ctx[None],
        w_mod_f[:, :2 * d])
    mod_out = _adamw(w_mod, g_wmod, m_w_mod, v_w_mod, "adamw_w_mod")

    def rep(cc, bm, n1, bg, hg, lg, lb, ws, bs, n2, fg):
        return [cc.reshape(-1), bm.reshape(-1), n1.reshape(-1), _pad_lanes(bg.reshape(1, -1)).reshape(-1), hg.reshape(-1),
                lg.reshape(-1), lb.reshape(-1), ws.reshape(-1), bs.reshape(-1), n2.reshape(-1), fg.reshape(-1)]

    o = o_n1
    g_rep_parts = [g_cctx, g_bmod]
    for n in (d, LANES, d, d, d, ng * sc * sc, ng * sc, d, d):
        g_rep_parts.append(small_sum[o:o + n])
        o += n
    rep_shapes = [(d,), (1, N_MOD * d), (1, d), (1, LANES), (1, d), (1, d), (1, d), (1, ng, sc, sc), (1, ng, sc), (1, d), (d,)]
    rep_out = _adamw(
        _pack(rep(c_ctx, b_mod, norm1_g, b_gate, head_norm_g, sgu_ln_g, sgu_ln_b, w_s, b_s, norm2_g, final_g), LANES),
        _pack(g_rep_parts, LANES)[None],
        _pack(rep(m_c_ctx, m_b_mod, m_norm1_g, m_b_gate, m_head_norm_g, m_sgu_ln_g, m_sgu_ln_b, m_w_s, m_b_s, m_norm2_g, m_final_g), LANES),
        _pack(rep(v_c_ctx, v_b_mod, v_norm1_g, v_b_gate, v_head_norm_g, v_sgu_ln_g, v_sgu_ln_b, v_w_s, v_b_s, v_norm2_g, v_final_g), LANES),
        "adamw_replicated")

    def assemble(k):
        r = _unpack(rep_out[k], rep_shapes)
        s = [o[k] for o in shard_out]
        return [r[0], mod_out[k], r[1], r[2], s[0], r[3][:, :4 * nh], s[6], r[4], r[5], r[6], r[7], r[8], s[2], s[3], s[4], r[9],
                s[1], s[7], s[5], r[10]]

    loss = small_sum[o]
    outs = [loss, grad_x[None]]
    for k in range(4):
        outs += assemble(k)
    return tuple(outs)
```

```python
import math

import jax
import jax.numpy as jnp
from jax import lax
from jax.experimental import pallas as pl
from jax.experimental.pallas import tpu as pltpu

F32, BF16 = jnp.float32, jnp.bfloat16
EPS = 1e-6
M_INIT = -1e30
NEG = -1e30
GRID_W = 64
LCH = 256
N_MOD = 6
N_DEV = 8
LANES = 128
ADAM_LR, ADAM_B1, ADAM_B2, ADAM_EPS, ADAM_WD, ADAM_STEP = 0.001, 0.9, 0.999, 1e-08, 0.01, 10
GELU_C = math.sqrt(2.0 / math.pi)
GELU_A = 0.044715
VMEM_LIMIT = 56 * 1024 * 1024
HI = lax.Precision.HIGHEST
SDS = jax.ShapeDtypeStruct
MESH_ID = pl.DeviceIdType.MESH


def _pick(n, cands):
    for c in cands:
        if n % c == 0:
            return c
    raise ValueError(f"no block size for {n} in {cands}")


def _cp(*sem):
    return pltpu.CompilerParams(dimension_semantics=sem if sem else None, vmem_limit_bytes=VMEM_LIMIT)


def _sigmoid(x):
    return 0.5 * jnp.tanh(0.5 * x) + 0.5


def _split3(x):
    hi = x.astype(BF16)
    r = x - hi.astype(F32)
    mid = r.astype(BF16)
    return hi, mid, (r - mid.astype(F32)).astype(BF16)


def _mask_dot(mask_b, x):
    hi, mid, lo = _split3(x)
    return (_dot(mask_b, lo) + _dot(mask_b, mid)) + _dot(mask_b, hi)


def _mask_dot_t(mask_b, x):
    hi, mid, lo = _split3(x)
    return (_dot_tn(mask_b, lo) + _dot_tn(mask_b, mid)) + _dot_tn(mask_b, hi)


def _gelu(x):
    return x * (0.5 * (1.0 + jnp.tanh(GELU_C * x * (1.0 + GELU_A * (x * x)))))


def _gelu_and_grad(x):
    x2 = x * x
    t = jnp.tanh(GELU_C * x * (1.0 + GELU_A * x2))
    half = 0.5 * (1.0 + t)
    return x * half, half + (0.5 * GELU_C) * x * (1.0 - t * t) * (1.0 + 3.0 * GELU_A * x2)


def _log_sigmoid(x):
    return jnp.minimum(x, 0.0) - jnp.log(1.0 + jnp.exp(-jnp.abs(x)))


def _dot(a, b):
    return jnp.dot(a, b, preferred_element_type=F32)


def _dot_nt(a, b):
    return lax.dot_general(a, b, (((1,), (1,)), ((), ())), preferred_element_type=F32)


def _dot_tn(a, b):
    return lax.dot_general(a, b, (((0,), (0,)), ((), ())), preferred_element_type=F32)


def _bf(x):
    return x.astype(BF16)


def _allgather(arrs):
    na = len(arrs)

    def body(*refs):
        x_refs, o_refs = refs[:na], refs[na:2 * na]
        send_sems, recv_sems, local_sems = refs[2 * na:]
        x, y, c = lax.axis_index("x"), lax.axis_index("y"), lax.axis_index("c")
        me, sibling = (x, y, c), (x, y, 1 - c)
        chips = [(1 - x, y), (x, 1 - y), (1 - x, 1 - y)]

        def copy(a, k, block, to, src=None):
            slot = o_refs[a].at[4 * block[0] + 2 * block[1] + block[2]]
            return pltpu.make_async_remote_copy(
                src_ref=slot if src is None else src, dst_ref=slot, send_sem=send_sems.at[7 * a + k],
                recv_sem=recv_sems.at[7 * a + k], device_id=to, device_id_type=MESH_ID)

        mine = [pltpu.make_async_copy(x_refs[a], o_refs[a].at[4 * x + 2 * y + c], local_sems.at[a]) for a in range(na)]
        for cp in mine:
            cp.start()
        first = []
        for a in range(na):
            first.append(copy(a, 0, me, sibling, src=x_refs[a]))
            first += [copy(a, 1 + j, me, (*chip, c), src=x_refs[a]) for j, chip in enumerate(chips)]
        for cp in first:
            cp.start()
        passed = []
        for j, chip in enumerate(chips):
            for a in range(na):
                copy(a, 1 + j, (*chip, c), me).wait_recv()
                passed.append(copy(a, 4 + j, (*chip, c), sibling))
                passed[-1].start()
        for a in range(na):
            copy(a, 0, sibling, me).wait_recv()
            for j, chip in enumerate(chips):
                copy(a, 4 + j, (*chip, 1 - c), me).wait_recv()
        for cp in first + passed:
            cp.wait_send()
        for cp in mine:
            cp.wait()

    anyspec = pl.BlockSpec(memory_space=pl.ANY)
    return pl.pallas_call(
        body, name="weights_allgather",
        out_shape=[SDS((N_DEV,) + a.shape, a.dtype) for a in arrs],
        in_specs=[anyspec] * na, out_specs=[anyspec] * na,
        scratch_shapes=[pltpu.SemaphoreType.DMA((7 * na,)), pltpu.SemaphoreType.DMA((7 * na,)), pltpu.SemaphoreType.DMA((na,))],
    )(*arrs)


_HBM_SPEC = pl.BlockSpec(memory_space=pltpu.HBM)
_SEM_SPEC = pl.BlockSpec(memory_space=pltpu.SEMAPHORE)
_EFFECT = pltpu.SideEffectType.DATAFLOW_SIDE_EFFECTING


def _peer_list(x, y, c):
    out = []
    for k in range(1, N_DEV):
        px = 1 - x if k & 4 else x
        py = 1 - y if k & 2 else y
        pc = 1 - c if k & 1 else c
        out.append(((px, py, pc), 4 * px + 2 * py + pc))
    return out


def _split_copies(src, land, send_sems, recv_sems, per_dest, receive):
    x, y, c = lax.axis_index("x"), lax.axis_index("y"), lax.axis_index("c")
    me = 4 * x + 2 * y + c
    out = []
    for k, (peer, pidx) in enumerate(_peer_list(x, y, c)):
        for a in range(len(src)):
            out.append(pltpu.make_async_remote_copy(
                src_ref=src[a].at[pidx] if per_dest else src[a], dst_ref=land[a].at[pidx if receive else me],
                send_sem=send_sems.at[7 * a + k], recv_sem=recv_sems.at[7 * a + k], device_id=peer, device_id_type=MESH_ID))
    return out


def _own_copies(src, land, own_sems, per_dest):
    me = 4 * lax.axis_index("x") + 2 * lax.axis_index("y") + lax.axis_index("c")
    return [pltpu.make_async_copy(src[a].at[me] if per_dest else src[a], land[a].at[me], own_sems.at[a]) for a in range(len(src))]


def _exchange_start(arrs, per_dest, name):
    na = len(arrs)
    land_shapes = [a.shape if per_dest else (N_DEV,) + a.shape for a in arrs]
    lands = [pltpu.with_memory_space_constraint(lax.empty(s, a.dtype), pltpu.HBM) for s, a in zip(land_shapes, arrs)]

    def body(*refs):
        src, land = refs[:na], refs[na:2 * na]
        send_sems, recv_sems, own_sems, token = refs[2 * na], refs[2 * na + 1], refs[2 * na + 2], refs[-1]
        for cp in _split_copies(src, land, send_sems, recv_sems, per_dest, False) + _own_copies(src, land, own_sems, per_dest):
            cp.start()
        token[...] = jnp.zeros_like(token)

    outs = pl.pallas_call(
        body, name=name,
        out_shape=[pltpu.SemaphoreType.DMA((7 * na,)), pltpu.SemaphoreType.DMA((7 * na,)), pltpu.SemaphoreType.DMA((na,))]
        + [pltpu.HBM(a.shape, a.dtype) for a in arrs] + [pltpu.HBM(s, a.dtype) for s, a in zip(land_shapes, arrs)]
        + [SDS((8, LANES), F32)],
        in_specs=[_HBM_SPEC] * (2 * na),
        out_specs=[_SEM_SPEC] * 3 + [_HBM_SPEC] * (2 * na) + [pl.BlockSpec(memory_space=pltpu.VMEM)],
        input_output_aliases={k: 3 + k for k in range(2 * na)},
        compiler_params=pltpu.CompilerParams(has_side_effects=_EFFECT),
    )(*[pltpu.with_memory_space_constraint(a, pltpu.HBM) for a in arrs], *lands)
    return (na, per_dest, outs[:-1]), outs[-1]


def _exchange_wait(state, after, name):
    na, per_dest, started = state

    def body(*refs):
        src, land = refs[:na], refs[na:2 * na]
        send_sems, recv_sems, own_sems = refs[2 * na], refs[2 * na + 1], refs[2 * na + 2]
        for cp in _split_copies(src, land, send_sems, recv_sems, per_dest, True):
            cp.wait_send()
            cp.wait_recv()
        for cp in _own_copies(src, land, own_sems, per_dest):
            cp.wait()

    bufs = started[3:]
    outs = pl.pallas_call(
        body, name=name,
        out_shape=[pltpu.HBM(b.shape, b.dtype) for b in bufs],
        in_specs=[_HBM_SPEC] * (2 * na) + [_SEM_SPEC] * 3 + [pl.BlockSpec(memory_space=pl.ANY)],
        out_specs=[_HBM_SPEC] * (2 * na),
        input_output_aliases={k: k for k in range(2 * na)},
        compiler_params=pltpu.CompilerParams(has_side_effects=_EFFECT),
    )(*bufs, started[0], started[1], started[2], after)
    return outs[na:]


def _col_pieces(n, segments):
    out = []
    for j in range(N_DEV):
        lo, hi = j * n, (j + 1) * n
        for (k, s0, w, c0) in segments:
            a, b = max(lo, s0), min(hi, s0 + w)
            if a < b:
                out.append((j, a - lo, b - lo, k, c0 + a - s0, c0 + b - s0))
    return out


def _assemble_cols(slots, groups, out_widths, name):
    r, p = slots.shape[1], slots.shape[2]
    tb = _pick(r, (128, 64, 32, 16, 8))
    covered = [0] * len(out_widths)
    for (_, n, segs) in groups:
        for (k, _, w, _) in segs:
            covered[k] += w

    def body(s_ref, *o_refs):
        for k, wd in enumerate(out_widths):
            if covered[k] < wd:
                o_refs[k][...] = jnp.zeros_like(o_refs[k])
        for (off, n, segs) in groups:
            for (j, a0, a1, k, d0, d1) in _col_pieces(n, segs):
                o_refs[k][:, d0:d1] = s_ref[j, :, off + a0:off + a1]

    return pl.pallas_call(
        body, name=name, grid=(r // tb,), in_specs=[pl.BlockSpec((N_DEV, tb, p), lambda i: (0, i, 0))],
        out_specs=[pl.BlockSpec((tb, w), lambda i: (i, 0)) for w in out_widths],
        out_shape=[SDS((r, w), slots.dtype) for w in out_widths], compiler_params=_cp("arbitrary"))(slots)


def _scatter_cols(pieces, segments, n, name):
    r = pieces[0].shape[0]
    tb = _pick(r, (128, 64, 32, 16, 8))

    def body(*refs):
        p_refs, o_ref = refs[:-1], refs[-1]
        for (j, a0, a1, k, d0, d1) in _col_pieces(n, segments):
            o_ref[j, :, a0:a1] = p_refs[k][:, d0:d1]

    return pl.pallas_call(
        body, name=name, grid=(r // tb,), in_specs=[pl.BlockSpec((tb, a.shape[1]), lambda i: (i, 0)) for a in pieces],
        out_specs=pl.BlockSpec((N_DEV, tb, n), lambda i: (0, i, 0)), out_shape=SDS((N_DEV, r, n), pieces[0].dtype),
        compiler_params=_cp("arbitrary"))(*pieces)


def _modulation(cvec, w_mod, b_mod):
    d, n = w_mod.shape

    def body(c_ref, w_ref, b_ref, s_ref, o_ref):
        cv = c_ref[...]
        s = cv * _sigmoid(cv)
        s_ref[...] = s
        o_ref[...] = _dot(_bf(s), w_ref[...]) + b_ref[...]

    return pl.pallas_call(body, name="modulation", out_shape=(SDS((8, d), F32), SDS((8, n), F32)),
                          compiler_params=_cp())(cvec, w_mod, b_mod)


def _norm_mod_proj(x_arr, g, shsc, w_main, w_gate, rows_total, row0, filled, name):
    m_rows, d = x_arr.shape
    n = w_main.shape[1]
    tb = _pick(m_rows, (1024, 256))
    cb = _pick(n, (2048, 1408, 1024, 768, 512, 384, 256, 128))
    gate = w_gate is not None
    nout = 3 if gate else 2
    nin = 5 if gate else 4
    rb = row0 // tb

    def body(*refs):
        x_ref, g_ref, ss_ref, wm_ref = refs[:4]
        wg_ref = refs[4] if gate else None
        outs = refs[len(refs) - 1 - nout:len(refs) - 1]
        hn_ref, z_ref = outs[0], outs[1]
        hn_sc = refs[-1]

        @pl.when(pl.program_id(1) == 0)
        def _():
            x = x_ref[...]
            r = lax.rsqrt(jnp.mean(x * x, axis=-1, keepdims=True) + EPS)
            hb = _bf((x * r * g_ref[...]) * (1.0 + ss_ref[1:2, :]) + ss_ref[0:1, :])
            hn_sc[...] = hb
            hn_ref[...] = hb
            if gate:
                outs[2][...] = _dot(hb, wg_ref[...])

        z_ref[...] = _bf(_dot(hn_sc[...], wm_ref[:, pl.ds(pl.multiple_of(pl.program_id(1) * cb, cb), cb)]))

    in_specs = [pl.BlockSpec((tb, d), lambda i, j: (i, 0)), pl.BlockSpec((1, d), lambda i, j: (0, 0)),
                pl.BlockSpec((2, d), lambda i, j: (0, 0)), _resident((d, n))]
    out_specs = [pl.BlockSpec((tb, d), lambda i, j: (rb + i, 0)), pl.BlockSpec((tb, cb), lambda i, j: (rb + i, j))]
    out_shape = [SDS((rows_total, d), BF16), SDS((rows_total, n), BF16)]
    args = [x_arr, g, shsc, w_main]
    if gate:
        in_specs.append(pl.BlockSpec((d, LANES), lambda i, j: (0, 0)))
        out_specs.append(pl.BlockSpec((tb, LANES), lambda i, j: (rb + i, 0)))
        out_shape.append(SDS((rows_total, LANES), F32))
        args.append(w_gate)
    aliases = {}
    if filled is not None:
        in_specs += [pl.BlockSpec(memory_space=pl.ANY)] * nout
        args += list(filled)
        aliases = {nin + k: k for k in range(nout)}
    return pl.pallas_call(
        body, name=name, grid=(m_rows // tb, n // cb), in_specs=in_specs, out_specs=out_specs, out_shape=out_shape,
        input_output_aliases=aliases, scratch_shapes=[pltpu.VMEM((tb, d), BF16)],
        compiler_params=_cp("arbitrary", "arbitrary"))(*args)


def _seg_masks(row, t_rows, s_rows):
    prev_ok = (row != 0) & (row != t_rows)
    next_ok = (row != t_rows - 1) & (row != s_rows - 1)
    return prev_ok, next_ok


def _shift_rows(z, halo_prev, halo_next, tb):
    loc = lax.broadcasted_iota(jnp.int32, (tb, 1), 0)
    zp = jnp.where(loc == 0, halo_prev, pltpu.roll(z, 1, 0))
    zn = jnp.where(loc == tb - 1, halo_next, pltpu.roll(z, tb - 1, 0))
    return zp, zn


def _qk_conv(z_main, conv_w, t_rows, md, qscale):
    s_rows = z_main.shape[0]
    tb = _pick(s_rows, (1280, 1024, 256))
    cb = _pick(md, (512, 256, 128))
    nb8 = tb // 8

    def body(zm, zp, zn, w_ref, o_ref):
        i, j = pl.program_id(0), pl.program_id(1)
        z = zm[...].astype(F32)
        zprev, znext = _shift_rows(z, zp[7:8, :].astype(F32), zn[0:1, :].astype(F32), tb)
        row = i * tb + lax.broadcasted_iota(jnp.int32, (tb, 1), 0)
        prev_ok, next_ok = _seg_masks(row, t_rows, s_rows)
        pre = (w_ref[0:1, :] * jnp.where(prev_ok, zprev, 0.0) + w_ref[1:2, :] * z
               + w_ref[2:3, :] * jnp.where(next_ok, znext, 0.0))
        scale = jnp.where(j * cb < md, qscale, 1.0)
        o_ref[...] = _bf(pre * _sigmoid(pre) * scale)

    return pl.pallas_call(
        body, name="qk_conv", grid=(s_rows // tb, 2 * md // cb),
        in_specs=[pl.BlockSpec((tb, cb), lambda i, j: (i, j)),
                  pl.BlockSpec((8, cb), lambda i, j: (jnp.maximum(i * nb8 - 1, 0), j)),
                  pl.BlockSpec((8, cb), lambda i, j: (jnp.minimum((i + 1) * nb8, s_rows // 8 - 1), j)),
                  pl.BlockSpec((8, cb), lambda i, j: (0, j))],
        out_specs=pl.BlockSpec((tb, cb), lambda i, j: (i, j)),
        out_shape=SDS((s_rows, 2 * md), BF16), compiler_params=_cp("arbitrary", "arbitrary"))(z_main, z_main, z_main, conv_w)


def _chunk_gates(gates, bias, rev):
    ln = gates.shape[0]
    gz = gates + bias
    logf = _log_sigmoid(gz)
    r_id = lax.broadcasted_iota(jnp.int32, (ln, ln), 0)
    c_id = lax.broadcasted_iota(jnp.int32, (ln, ln), 1)
    mask = (c_id >= r_id) if rev else (c_id <= r_id)
    mb = mask.astype(F32).astype(BF16)
    b_all = _mask_dot(mb, logf)
    g_all = jnp.sum(logf, axis=0, keepdims=True)
    return gz, b_all, b_all.T, gz.T, g_all, mask, mb


def _head_weights(b_col, b_row, i_row, m_in, mask):
    d = jnp.where(mask, b_col + (i_row - b_row), NEG)
    inter = b_col + m_in
    m_row = jnp.maximum(inter, jnp.max(d, axis=1, keepdims=True))
    return jnp.exp(d - m_row), jnp.exp(inter - m_row), m_row


def _head_state_coeffs(g, b_col, i_col, m_in):
    a = g - b_col + i_col
    m_new = jnp.maximum(g + m_in, jnp.max(a, axis=0, keepdims=True))
    return jnp.exp(g + m_in - m_new), jnp.exp(a - m_new), m_new


def _mlstm_fwd(qk, z_main, zg, bias, nh):
    s_rows = qk.shape[0]
    md = qk.shape[1] // 2
    dh = md // nh
    nc = s_rows // LCH
    ln = LCH

    def chunk_f(i):
        return jnp.where(i == 0, nc - 1, i - 1)

    def chunk_b(i):
        return jnp.where(i == 0, nc - 1, nc - 1 - i)

    def body(qf, kf, vf, gf, qb, kb, vb, gb, bias_ref, hf_ref, hb_ref, cf_ref, nf_ref, mf_ref, cb_ref, nb_ref, mb_ref,
             c_sc, n_sc, m_sc):
        i = pl.program_id(0)

        @pl.when(i == 0)
        def _():
            c_sc[...] = jnp.zeros_like(c_sc)
            n_sc[...] = jnp.zeros_like(n_sc)
            m_sc[...] = jnp.full(m_sc.shape, M_INIT, F32)

        sides = ((qf, kf, vf, gf, hf_ref, cf_ref, nf_ref, mf_ref), (qb, kb, vb, gb, hb_ref, cb_ref, nb_ref, mb_ref))
        gates = [_chunk_gates(s[3][...], bias_ref[...], dr == 1) for dr, s in enumerate(sides)]
        units = []
        for dr, (q_ref, k_ref, v_ref, _, h_ref, c_out, n_out, m_out) in enumerate(sides):
            gz, b_all, b_t, g_t, g_all, mask, _ = gates[dr]
            for h in range(nh):
                ci, cf = 2 * dr * nh + h, (2 * dr + 1) * nh + h
                sl = slice(h * dh, (h + 1) * dh)
                u = dict(dr=dr, h=h, sl=sl, h_ref=h_ref, q=q_ref[:, sl], k=k_ref[:, sl], v=v_ref[:, sl],
                         c_in=c_sc[dr, h], n_in=n_sc[dr, h, 0:1, :], m_in=m_sc[dr, h, 0:1, 0:1],
                         b_col=b_all[:, cf:cf + 1], i_col=gz[:, ci:ci + 1], g=g_all[:, cf:cf + 1])
                c_out[sl, :] = u["c_in"]
                n_out[:, sl] = n_sc[dr, h]
                m_out[h] = m_sc[dr, h]
                u["w"], u["w_int"], u["m_row"] = _head_weights(u["b_col"], b_t[cf:cf + 1, :], g_t[ci:ci + 1, :], u["m_in"], mask)
                u["qk"] = _dot_nt(u["q"], u["k"])
                units.append(u)
        for u in units:
            u["s_mat"] = u["qk"] * u["w"]
            u["qc"] = _dot(u["q"], _bf(u["c_in"]))
            u["a_old"], u["coef"], u["m_new"] = _head_state_coeffs(u["g"], u["b_col"], u["i_col"], u["m_in"])
            u["kw"] = u["k"].astype(F32) * u["coef"]
        for u in units:
            u["sv"] = _dot(_bf(u["s_mat"]), u["v"])
            u["kv"] = _dot_tn(_bf(u["kw"]), u["v"])
        for u in units:
            dr, h = u["dr"], u["h"]
            num = u["sv"] + u["w_int"] * u["qc"]
            den = (jnp.sum(u["s_mat"], axis=1, keepdims=True)
                   + u["w_int"] * jnp.sum(u["q"].astype(F32) * u["n_in"], axis=1, keepdims=True))
            u["h_ref"][:, u["sl"]] = _bf(num / jnp.maximum(jnp.abs(den), jnp.exp(-u["m_row"])))
            c_sc[dr, h] = u["a_old"] * u["c_in"] + u["kv"]
            n_sc[dr, h] = jnp.broadcast_to(u["a_old"] * u["n_in"] + jnp.sum(u["kw"], axis=0, keepdims=True), (8, dh))
            m_sc[dr, h] = jnp.broadcast_to(u["m_new"], (8, LANES))

    def tok(cfn, col):
        return pl.BlockSpec((ln, md), lambda i: (cfn(i), col))

    def gat(cfn):
        return pl.BlockSpec((ln, LANES), lambda i: (cfn(i), 0))

    def st(cfn, shape):
        return pl.BlockSpec((None,) + shape, lambda i: (cfn(i),) + (0,) * len(shape))

    st_shapes = ((nh * dh, dh), (8, md), (nh, 8, LANES))
    return pl.pallas_call(
        body, name="mlstm_fwd", grid=(nc,),
        in_specs=[tok(chunk_f, 0), tok(chunk_f, 1), tok(chunk_f, 2), gat(chunk_f),
                  tok(chunk_b, 0), tok(chunk_b, 1), tok(chunk_b, 2), gat(chunk_b),
                  pl.BlockSpec((1, LANES), lambda i: (0, 0))],
        out_specs=[tok(chunk_f, 0), tok(chunk_b, 0)] + [st(chunk_f, s) for s in st_shapes] + [st(chunk_b, s) for s in st_shapes],
        out_shape=[SDS((s_rows, md), BF16)] * 2 + [SDS((nc,) + s, F32) for s in st_shapes] * 2,
        scratch_shapes=[pltpu.VMEM((2, nh, dh, dh), F32), pltpu.VMEM((2, nh, 8, dh), F32), pltpu.VMEM((2, nh, 8, LANES), F32)],
        compiler_params=_cp("arbitrary"))(qk, qk, z_main, zg, qk, qk, z_main, zg, bias)


def _head_rms(hs, nh, dh):
    parts, scales = [], []
    for h in range(nh):
        hh = hs[:, h * dh:(h + 1) * dh]
        r = lax.rsqrt(jnp.mean(hh * hh, axis=-1, keepdims=True) + EPS)
        parts.append(hh * r)
        scales.append(r)
    return jnp.concatenate(parts, axis=1), scales


def _layer_norm(v):
    vc = v - jnp.mean(v, axis=-1, keepdims=True)
    r = lax.rsqrt(jnp.mean(vc * vc, axis=-1, keepdims=True) + EPS)
    return vc * r, r


def _sgu_mix(vnb, ws_ref, bs_ref, tb, ng, gd, sc):
    rows = []
    for ch in range(tb // sc):
        cols = []
        for g in range(ng):
            blk = vnb[ch * sc:(ch + 1) * sc, g * gd:(g + 1) * gd]
            cols.append(_dot(_bf(ws_ref[g]), blk) + bs_ref[:, g:g + 1])
        rows.append(jnp.concatenate(cols, axis=1))
    return jnp.concatenate(rows, axis=0)


def _mixer_fwd(hf, hb, z_main, xs, hg, lng, lnb, w_s, b_st, wbm, wbs, wout, mx2, t_rows, nh):
    d = xs.shape[1]
    ng, sc = w_s.shape[0], w_s.shape[1]
    dh, gd = d // nh, d // ng
    tb = _pick(t_rows, (256,))

    def body(hf_ref, hb_ref, zo, zu, zv, zgm, zgg, x_ref, hg_ref, lng_ref, lnb_ref, ws_ref, bs_ref, wbm_ref, wbs_ref,
             wo_ref, mx2_ref, h1_ref, ym_ref, ys_ref, pm_ref, ps_ref, y_ref, out_ref):
        hs = hf_ref[...].astype(F32) + hb_ref[...].astype(F32)
        hn, _ = _head_rms(hs, nh, dh)
        ym = _bf(_sigmoid(zo[...].astype(F32)) * (hn * hg_ref[...]))
        ym_ref[...] = ym
        vhat, _ = _layer_norm(_gelu(zv[...].astype(F32)))
        vnb = _bf(vhat * lng_ref[...] + lnb_ref[...])
        ys = _bf(_gelu(zu[...].astype(F32)) * _sgu_mix(vnb, ws_ref, bs_ref, tb, ng, gd, sc))
        ys_ref[...] = ys
        pm = _dot(ym, wbm_ref[...])
        ps = _dot(ys, wbs_ref[...])
        pm_ref[...] = _bf(pm)
        ps_ref[...] = _bf(ps)
        y = _bf(_sigmoid(zgm[...].astype(F32)) * pm + _sigmoid(zgg[...].astype(F32)) * ps)
        y_ref[...] = y
        out = _dot(y, wo_ref[...])
        out_ref[...] = _bf(out)
        h1_ref[...] = x_ref[...] + mx2_ref[...] * out

    def tok(col):
        return pl.BlockSpec((tb, d), lambda i: (i, col))

    def full(shape):
        return pl.BlockSpec(shape, lambda i: (0,) * len(shape))

    return pl.pallas_call(
        body, name="mixer_fwd", grid=(t_rows // tb,),
        in_specs=[tok(0), tok(0), tok(3), tok(4), tok(5), tok(6), tok(7), tok(0), full((1, d)), full((1, d)), full((1, d)),
                  full((ng, sc, sc)), full((sc, LANES)), full((d, d)), full((d, d)), full((d, d)), full((1, d))],
        out_specs=[tok(0)] * 7,
        out_shape=[SDS((t_rows, d), F32)] + [SDS((t_rows, d), BF16)] * 6,
        compiler_params=_cp("arbitrary"))(hf, hb, z_main, z_main, z_main, z_main, z_main, xs, hg, lng, lnb, w_s, b_st,
                                          wbm, wbs, wout, mx2)


def _resident(shape):
    return pl.BlockSpec(shape, lambda *_: (0,) * len(shape), pipeline_mode=pl.Buffered(1))


def _grid_taps(a_ext, n_ext):
    col = lax.broadcasted_iota(jnp.int32, (n_ext, 1), 0) % GRID_W
    left = jnp.where(col != 0, pltpu.roll(a_ext, 1, 0), 0.0)
    right = jnp.where(col != GRID_W - 1, pltpu.roll(a_ext, n_ext - 1, 0), 0.0)
    return left, right


def _with_halo(prev, main, nxt, i, ni, tb):
    prev = jnp.where(i > 0, prev, jnp.zeros_like(prev))
    nxt = jnp.where(i < ni - 1, nxt, jnp.zeros_like(nxt))
    return jnp.concatenate([prev, main, nxt], axis=0).astype(F32)


def _halo_specs(tb, cb, t_rows, col0=0):
    nh64 = tb // GRID_W
    return [pl.BlockSpec((tb, cb), lambda i, j: (i, col0 + j)),
            pl.BlockSpec((GRID_W, cb), lambda i, j: (jnp.maximum(i * nh64 - 1, 0), col0 + j)),
            pl.BlockSpec((GRID_W, cb), lambda i, j: (jnp.minimum((i + 1) * nh64, t_rows // GRID_W - 1), col0 + j))]


def _ffn_tail(ab, w_conv9, w_down, h1, mx5, gfin, target, dff):
    t_rows, d = h1.shape
    tb = _pick(t_rows, (256,))
    cb = _pick(dff, (1408, 256, 128))
    ni, nj = t_rows // tb, dff // cb
    n_ext = tb + 2 * GRID_W

    def body(am, ap, an, b_ref, wc_ref, wd_ref, h1_ref, mx5_ref, gf_ref, tg_ref, ac_ref, f_ref, dh2_ref, dffn_ref, st_ref, acc):
        i, j = pl.program_id(0), pl.program_id(1)
        a_ext = _with_halo(ap[...], am[...], an[...], i, ni, tb)
        left, right = _grid_taps(a_ext, n_ext)
        conv = jnp.zeros((tb, cb), F32)
        for di in range(3):
            o = di * GRID_W
            conv = conv + (wc_ref[3 * di:3 * di + 1, :] * left[o:o + tb] + wc_ref[3 * di + 1:3 * di + 2, :] * a_ext[o:o + tb]
                           + wc_ref[3 * di + 2:3 * di + 3, :] * right[o:o + tb])
        ac_ref[...] = _bf(conv)
        fb = _bf(conv * _sigmoid(conv) * b_ref[...].astype(F32))
        f_ref[...] = fb

        @pl.when(j == 0)
        def _():
            acc[...] = jnp.zeros_like(acc)

        @pl.when((i == 0) & (j == 0))
        def _():
            st_ref[...] = jnp.zeros_like(st_ref)

        acc[...] += _dot(fb, wd_ref[pl.ds(pl.multiple_of(j * cb, cb), cb), :])

        @pl.when(j == nj - 1)
        def _():
            ffn = acc[...]
            h2 = h1_ref[...] + mx5_ref[...] * ffn
            r = lax.rsqrt(jnp.mean(h2 * h2, axis=-1, keepdims=True) + EPS)
            xn = h2 * r
            e = xn * gf_ref[...] - tg_ref[...]
            loss = 0.5 * jnp.sum(jnp.sum(e * e, axis=1, keepdims=True), axis=0, keepdims=True) / d
            dy = e * (1.0 / d)
            dxn = dy * gf_ref[...]
            dh2 = r * (dxn - xn * jnp.mean(dxn * xn, axis=-1, keepdims=True))
            dh2_ref[...] = dh2
            dffn_ref[...] = _bf(dh2 * mx5_ref[...])
            st_ref[...] += jnp.concatenate(
                [jnp.sum(dy * xn, axis=0, keepdims=True), jnp.sum(dh2 * ffn, axis=0, keepdims=True),
                 jnp.broadcast_to(loss, (1, d)), jnp.zeros((5, d), F32)], axis=0)

    def tokd():
        return pl.BlockSpec((tb, d), lambda i, j: (i, 0))

    def rowd():
        return pl.BlockSpec((1, d), lambda i, j: (0, 0))

    return pl.pallas_call(
        body, name="ffn_tail", grid=(ni, nj),
        in_specs=_halo_specs(tb, cb, t_rows) + [pl.BlockSpec((tb, cb), lambda i, j: (i, nj + j)),
                                                pl.BlockSpec((16, cb), lambda i, j: (0, j)),
                                                _resident((dff, d)), tokd(), rowd(), rowd(), tokd()],
        out_specs=[pl.BlockSpec((tb, cb), lambda i, j: (i, j)), pl.BlockSpec((tb, cb), lambda i, j: (i, j)), tokd(), tokd(),
                   pl.BlockSpec((8, d), lambda i, j: (0, 0))],
        out_shape=[SDS((t_rows, dff), BF16), SDS((t_rows, dff), BF16), SDS((t_rows, d), F32), SDS((t_rows, d), BF16),
                   SDS((8, d), F32)],
        scratch_shapes=[pltpu.VMEM((tb, d), F32)],
        compiler_params=_cp("arbitrary", "arbitrary"))(ab, ab, ab, ab, w_conv9, w_down, h1, mx5, gfin, target)


def _ffn_bwd_gate(dffn, w_down, aconv, ab, dff):
    t_rows, d = dffn.shape
    tb = _pick(t_rows, (512,))
    cb = _pick(dff, (1408, 256, 128))
    nj = dff // cb

    def body(g_ref, wd_ref, ac_ref, b_ref, db_ref, dac_ref):
        df = _dot_nt(g_ref[...], wd_ref[pl.ds(pl.multiple_of(pl.program_id(1) * cb, cb), cb), :])
        ac = ac_ref[...].astype(F32)
        sa = _sigmoid(ac)
        db_ref[...] = _bf(df * ac * sa)
        dac_ref[...] = _bf(df * b_ref[...].astype(F32) * (sa * (1.0 + ac * (1.0 - sa))))

    blk = pl.BlockSpec((tb, cb), lambda i, j: (i, j))
    return pl.pallas_call(
        body, name="ffn_bwd_gate", grid=(t_rows // tb, nj),
        in_specs=[pl.BlockSpec((tb, d), lambda i, j: (i, 0)), _resident((dff, d)), blk,
                  pl.BlockSpec((tb, cb), lambda i, j: (i, nj + j))],
        out_specs=[blk, blk], out_shape=[SDS((t_rows, dff), BF16)] * 2,
        compiler_params=_cp("arbitrary", "arbitrary"))(dffn, w_down, aconv, ab)


def _ffn_conv_bwd(dac, ab, w_conv9, dff):
    t_rows = dac.shape[0]
    tb = _pick(t_rows, (512, 256))
    cb = _pick(dff, (1408, 256, 128))
    ni, nj = t_rows // tb, dff // cb
    n_ext = tb + 2 * GRID_W
    nh64 = tb // GRID_W

    def body(dm, dp, dn, am, ap, an, wc_ref, da_ref, gw_ref):
        i = pl.program_id(1)
        d_ext = _with_halo(dp[...], dm[...], dn[...], i, ni, tb)
        a_ext = _with_halo(ap[...], am[...], an[...], i, ni, tb)
        d_left, d_right = _grid_taps(d_ext, n_ext)
        a_left, a_right = _grid_taps(a_ext, n_ext)
        dmain = d_ext[GRID_W:GRID_W + tb]
        da = jnp.zeros((tb, cb), F32)
        rows = []
        for di in range(3):
            o = (2 - di) * GRID_W
            da = da + (wc_ref[3 * di:3 * di + 1, :] * d_right[o:o + tb] + wc_ref[3 * di + 1:3 * di + 2, :] * d_ext[o:o + tb]
                       + wc_ref[3 * di + 2:3 * di + 3, :] * d_left[o:o + tb])
            o = di * GRID_W
            for tap in (a_left, a_ext, a_right):
                rows.append(jnp.sum(dmain * tap[o:o + tb], axis=0, keepdims=True))
        da_ref[...] = _bf(da)

        @pl.when(i == 0)
        def _():
            gw_ref[...] = jnp.zeros_like(gw_ref)

        gw_ref[...] += jnp.concatenate(rows + [jnp.zeros((7, cb), F32)], axis=0)

    def halo(col0):
        return [pl.BlockSpec((tb, cb), lambda j, i: (i, col0 + j)),
                pl.BlockSpec((GRID_W, cb), lambda j, i: (jnp.maximum(i * nh64 - 1, 0), col0 + j)),
                pl.BlockSpec((GRID_W, cb), lambda j, i: (jnp.minimum((i + 1) * nh64, t_rows // GRID_W - 1), col0 + j))]

    return pl.pallas_call(
        body, name="ffn_conv_bwd", grid=(nj, ni),
        in_specs=halo(0) + halo(0) + [pl.BlockSpec((16, cb), lambda j, i: (0, j))],
        out_specs=[pl.BlockSpec((tb, cb), lambda j, i: (i, j)), pl.BlockSpec((16, cb), lambda j, i: (0, j))],
        out_shape=[SDS((t_rows, dff), BF16), SDS((16, dff), F32)],
        compiler_params=_cp("arbitrary", "arbitrary"))(dac, dac, dac, ab, ab, ab, w_conv9)


def _proj_norm_bwd(pairs, x_arr, x_row0, g, scale, resid, m_rows, name, row_blocks=(1024, 256)):
    d = x_arr.shape[1]
    tm = _pick(m_rows, row_blocks)
    te = 256
    ni = m_rows // tm
    starts, total = [], 0
    for (_, _, _, _, k_p, tk_p) in pairs:
        starts.append(total)
        total += k_p // tk_p
    npairs = len(pairs)
    has_dx = resid is not None

    def body(*refs):
        a_refs, b_refs = refs[0:2 * npairs:2], refs[1:2 * npairs:2]
        rest = refs[2 * npairs:]
        if has_dx:
            x_ref, g_ref, sc_ref, r_ref, dx_ref, st_ref, acc = rest
        else:
            x_ref, g_ref, sc_ref, st_ref, acc = rest
        i, k = pl.program_id(0), pl.program_id(1)

        @pl.when(k == 0)
        def _():
            acc[...] = jnp.zeros_like(acc)

        @pl.when((i == 0) & (k == 0))
        def _():
            st_ref[...] = jnp.zeros_like(st_ref)

        for p in range(npairs):
            nk = pairs[p][4] // pairs[p][5]

            @pl.when((k >= starts[p]) & (k < starts[p] + nk))
            def _(p=p):
                acc[...] += _dot_nt(a_refs[p][...], b_refs[p][...])

        @pl.when(k == total - 1)
        def _():
            sums = [jnp.zeros((1, d), F32)] * 3
            for r0 in range(0, tm, te):
                rows = slice(r0, r0 + te)
                dhn = acc[rows, :]
                x = x_ref[rows, :]
                r = lax.rsqrt(jnp.mean(x * x, axis=-1, keepdims=True) + EPS)
                xn = x * r
                dmod = dhn * (1.0 + sc_ref[...])
                dxn = dmod * g_ref[...]
                if has_dx:
                    dx_ref[rows, :] = r * (dxn - xn * jnp.mean(dxn * xn, axis=-1, keepdims=True)) + r_ref[rows, :]
                sums = [sums[0] + jnp.sum(dmod * xn, axis=0, keepdims=True), sums[1] + jnp.sum(dhn, axis=0, keepdims=True),
                        sums[2] + jnp.sum(dhn * (xn * g_ref[...]), axis=0, keepdims=True)]
            st_ref[...] += jnp.concatenate(sums + [jnp.zeros((5, d), F32)], axis=0)

    in_specs, args = [], []
    for p, (a, a_row0, b, b_col0, k_p, tk_p) in enumerate(pairs):
        nk, s0, ar, bc = k_p // tk_p, starts[p], a_row0 // tm, b_col0 // tk_p

        def kk(k, s0=s0, nk=nk):
            return jnp.clip(k - s0, 0, nk - 1)

        in_specs.append(pl.BlockSpec((tm, tk_p), lambda i, k, ar=ar, kk=kk: (ar + i, kk(k))))
        in_specs.append(pl.BlockSpec((d, tk_p), lambda i, k, bc=bc, kk=kk: (0, bc + kk(k)),
                                     pipeline_mode=pl.Buffered(1 if nk == 1 else 2)))
        args += [a, b]
    xr = x_row0 // tm
    in_specs += [pl.BlockSpec((tm, d), lambda i, k: (xr + i, 0)), pl.BlockSpec((1, d), lambda i, k: (0, 0)),
                 pl.BlockSpec((1, d), lambda i, k: (0, 0))]
    args += [x_arr, g, scale]
    out_specs, out_shape = [], []
    if has_dx:
        in_specs.append(pl.BlockSpec((tm, d), lambda i, k: (i, 0)))
        args.append(resid)
        out_specs.append(pl.BlockSpec((tm, d), lambda i, k: (i, 0)))
        out_shape.append(SDS((m_rows, d), F32))
    out_specs.append(pl.BlockSpec((8, d), lambda i, k: (0, 0)))
    out_shape.append(SDS((8, d), F32))
    return pl.pallas_call(
        body, name=name, grid=(ni, total), in_specs=in_specs, out_specs=out_specs, out_shape=out_shape,
        scratch_shapes=[pltpu.VMEM((tm, d), F32)], compiler_params=_cp("arbitrary", "arbitrary"))(*args)


def _wgrad(a, b, k_rows, name):
    m, n = a.shape[1], b.shape[1]
    tm = _pick(m, (1408, 1024, 512, 384, 256, 128))
    tn = _pick(n, (3072, 2816, 2560, 1408, 1024, 768, 512, 384, 256, 128))
    tk = _pick(k_rows, (1280, 1024, 256))
    nk = k_rows // tk

    def body(a_ref, b_ref, o_ref, acc):
        k = pl.program_id(2)

        @pl.when(k == 0)
        def _():
            acc[...] = jnp.zeros_like(acc)

        acc[...] += _dot_tn(a_ref[...], b_ref[...])

        @pl.when(k == nk - 1)
        def _():
            o_ref[...] = _bf(acc[...])

    return pl.pallas_call(
        body, name=name, grid=(m // tm, n // tn, nk),
        in_specs=[pl.BlockSpec((tk, tm), lambda i, j, k: (k, i)), pl.BlockSpec((tk, tn), lambda i, j, k: (k, j))],
        out_specs=pl.BlockSpec((tm, tn), lambda i, j, k: (i, j)), out_shape=SDS((m, n), BF16),
        scratch_shapes=[pltpu.VMEM((tm, tn), F32)],
        compiler_params=_cp("arbitrary", "arbitrary", "arbitrary"))(a, b)


def _lane_put(col, lane_idx):
    lane = lax.broadcasted_iota(jnp.int32, (1, LANES), 1)
    return jnp.where(lane == lane_idx, col, 0.0)


def _mixer_bwd(dh1, out, hf, hb, z_main, pm, ps, hg, lng, lnb, w_s, b_st, wbm, wbs, wout, mx2, t_rows, nh):
    d = dh1.shape[1]
    ng, sc = w_s.shape[0], w_s.shape[1]
    dh, gd = d // nh, d // ng
    tb = _pick(t_rows, (256,))

    def body(dh1_ref, out_ref, hf_ref, hb_ref, zo, zu, zv, zgm, zgg, pm_ref, ps_ref, hg_ref, lng_ref, lnb_ref, ws_ref, bs_ref,
             wbm_ref, wbs_ref, wo_ref, mx2_ref, dz_ref, dhs_ref, dout_ref, dpm_ref, dps_ref, st_ref, dws_ref, dbs_ref):
        i = pl.program_id(0)

        @pl.when(i == 0)
        def _():
            st_ref[...] = jnp.zeros_like(st_ref)
            dws_ref[...] = jnp.zeros_like(dws_ref)
            dbs_ref[...] = jnp.zeros_like(dbs_ref)

        dh1v = dh1_ref[...]
        doutb = _bf(dh1v * mx2_ref[...])
        dout_ref[...] = doutb
        d_mx2 = jnp.sum(dh1v * out_ref[...].astype(F32), axis=0, keepdims=True)
        dy = _dot_nt(doutb, wo_ref[...])
        sgm, sgg = _sigmoid(zgm[...].astype(F32)), _sigmoid(zgg[...].astype(F32))
        dpmb, dpsb = _bf(dy * sgm), _bf(dy * sgg)
        dpm_ref[...] = dpmb
        dps_ref[...] = dpsb
        dz_ref[:, 3 * d:4 * d] = _bf(dy * pm_ref[...].astype(F32) * sgm * (1.0 - sgm))
        dz_ref[:, 4 * d:5 * d] = _bf(dy * ps_ref[...].astype(F32) * sgg * (1.0 - sgg))
        dym = _dot_nt(dpmb, wbm_ref[...])
        dys = _dot_nt(dpsb, wbs_ref[...])
        hs = hf_ref[...].astype(F32) + hb_ref[...].astype(F32)
        hn, scales = _head_rms(hs, nh, dh)
        so = _sigmoid(zo[...].astype(F32))
        dz_ref[:, 0:d] = _bf(dym * (hn * hg_ref[...]) * so * (1.0 - so))
        dhmn = dym * so
        d_hg = jnp.sum(dhmn * hn, axis=0, keepdims=True)
        dhn = dhmn * hg_ref[...]
        for h in range(nh):
            sl = slice(h * dh, (h + 1) * dh)
            dhs_ref[:, sl] = _bf(scales[h] * (dhn[:, sl] - hn[:, sl] * jnp.mean(dhn[:, sl] * hn[:, sl], axis=-1, keepdims=True)))
        zuv, zvv = zu[...].astype(F32), zv[...].astype(F32)
        u, du_dz = _gelu_and_grad(zuv)
        vg, dvg_dz = _gelu_and_grad(zvv)
        vhat, rstd = _layer_norm(vg)
        vnb = _bf(vhat * lng_ref[...] + lnb_ref[...])
        mixed = _sgu_mix(vnb, ws_ref, bs_ref, tb, ng, gd, sc)
        dz_ref[:, d:2 * d] = _bf(dys * mixed * du_dz)
        dmix = dys * u
        rows = []
        dbs = jnp.zeros((sc, LANES), F32)
        for ch in range(tb // sc):
            cols = []
            for g in range(ng):
                dm = dmix[ch * sc:(ch + 1) * sc, g * gd:(g + 1) * gd]
                dmb = _bf(dm)
                dws_ref[g] += _dot_nt(dmb, vnb[ch * sc:(ch + 1) * sc, g * gd:(g + 1) * gd])
                dbs = dbs + _lane_put(jnp.sum(dm, axis=1, keepdims=True), g)
                cols.append(_dot_tn(_bf(ws_ref[g]), dmb))
            rows.append(jnp.concatenate(cols, axis=1))
        dbs_ref[...] += dbs
        dvn = jnp.concatenate(rows, axis=0)
        d_lng = jnp.sum(dvn * vhat, axis=0, keepdims=True)
        d_lnb = jnp.sum(dvn, axis=0, keepdims=True)
        dvh = dvn * lng_ref[...]
        dvg = rstd * (dvh - jnp.mean(dvh, axis=-1, keepdims=True) - vhat * jnp.mean(dvh * vhat, axis=-1, keepdims=True))
        dz_ref[:, 2 * d:3 * d] = _bf(dvg * dvg_dz)
        st_ref[...] += jnp.concatenate([d_mx2, d_hg, d_lng, d_lnb, jnp.zeros((4, d), F32)], axis=0)

    def tok(col):
        return pl.BlockSpec((tb, d), lambda i: (i, col))

    def full(shape):
        return pl.BlockSpec(shape, lambda i: (0,) * len(shape))

    return pl.pallas_call(
        body, name="mixer_bwd", grid=(t_rows // tb,),
        in_specs=[tok(0), tok(0), tok(0), tok(0), tok(3), tok(4), tok(5), tok(6), tok(7), tok(0), tok(0), full((1, d)),
                  full((1, d)), full((1, d)), full((ng, sc, sc)), full((sc, LANES)), full((d, d)), full((d, d)), full((d, d)),
                  full((1, d))],
        out_specs=[pl.BlockSpec((tb, 5 * d), lambda i: (i, 0)), tok(0), tok(0), tok(0), tok(0), full((8, d)), full((ng, sc, sc)),
                   full((sc, LANES))],
        out_shape=[SDS((t_rows, 5 * d), BF16)] + [SDS((t_rows, d), BF16)] * 4 + [SDS((8, d), F32), SDS((ng, sc, sc), F32),
                                                                                SDS((sc, LANES), F32)],
        compiler_params=_cp("arbitrary"))(dh1, out, hf, hb, z_main, z_main, z_main, z_main, z_main, pm, ps, hg, lng, lnb, w_s,
                                          b_st, wbm, wbs, wout, mx2)


def _mlstm_bwd(qk, z_main, zg, bias, dhs, hf, hb, states_f, states_b, nh, t_rows):
    s_rows = qk.shape[0]
    md = qk.shape[1] // 2
    dh = md // nh
    nc = s_rows // LCH
    nx = t_rows // LCH
    ln = LCH

    def chunk_f(i):
        return jnp.where(i == nc - 1, nc - 1, nc - 2 - i)

    def chunk_b(i):
        return jnp.where(i == nc - 1, nc - 1, i)

    def body(qf, kf, vf, gf, dhf, hsf, cf, nf, mf_, qb, kb, vb, gb, dhb, hsb, cb, nb, mb_, bias_ref, dqkvf_ref, dgf_ref, dqkvb_ref,
             dgb_ref, dc_sc, dn_sc):
        i = pl.program_id(0)
        is_ctx = i == nc - 1

        @pl.when(i == 0)
        def _():
            dc_sc[...] = jnp.zeros_like(dc_sc)
            dn_sc[...] = jnp.zeros_like(dn_sc)

        sides = ((qf, kf, vf, gf, dhf, hsf, cf, nf, mf_, dqkvf_ref, dgf_ref), (qb, kb, vb, gb, dhb, hsb, cb, nb, mb_, dqkvb_ref, dgb_ref))
        gates = [_chunk_gates(s[3][...], bias_ref[...], dr == 1) for dr, s in enumerate(sides)]
        units = []
        for dr, (q_ref, k_ref, v_ref, _, dh_ref, hs_ref, c_ref, n_ref, m_ref, dqkv_ref, _) in enumerate(sides):
            gz, b_all, b_t, g_t, g_all, mask, _ = gates[dr]
            for h in range(nh):
                ci, cfl = 2 * dr * nh + h, (2 * dr + 1) * nh + h
                sl = slice(h * dh, (h + 1) * dh)
                u = dict(dr=dr, h=h, sl=sl, ci=ci, cfl=cfl, dqkv_ref=dqkv_ref, q=q_ref[:, sl], k=k_ref[:, sl], v=v_ref[:, sl],
                         dhv=jnp.where(is_ctx, 0.0, dh_ref[:, sl].astype(F32)), hs=hs_ref[:, sl].astype(F32),
                         c_in=c_ref[sl, :], n_in=n_ref[0:1, sl], m_in=m_ref[h, 0:1, 0:1],
                         b_col=b_all[:, cfl:cfl + 1], i_col=gz[:, ci:ci + 1], g=g_all[:, cfl:cfl + 1],
                         dc_new=dc_sc[dr, h], dn_new=dn_sc[dr, h, 0:1, :])
                u["qf32"], u["kf32"] = u["q"].astype(F32), u["k"].astype(F32)
                u["w"], u["w_int"], u["m_row"] = _head_weights(u["b_col"], b_t[cfl:cfl + 1, :], g_t[ci:ci + 1, :], u["m_in"], mask)
                u["qk"] = _dot_nt(u["q"], u["k"])
                units.append(u)
        for u in units:
            s_mat = u["qk"] * u["w"]
            u["s_mat"], u["sb"], u["cb16"], u["dcb"] = s_mat, _bf(s_mat), _bf(u["c_in"]), _bf(u["dc_new"])
            den = jnp.sum(s_mat, axis=1, keepdims=True) + u["w_int"] * jnp.sum(u["qf32"] * u["n_in"], axis=1, keepdims=True)
            e_m = jnp.exp(-u["m_row"])
            dnm = jnp.maximum(jnp.abs(den), e_m)
            hdh = jnp.sum(u["hs"] * u["dhv"], axis=1, keepdims=True)
            u["dden"] = jnp.where(jnp.abs(den) > e_m, -(hdh / dnm) * jnp.sign(den), 0.0)
            u["dnum_b"] = _bf(u["dhv"] / dnm)
            u["a_old"], u["coef"], _ = _head_state_coeffs(u["g"], u["b_col"], u["i_col"], u["m_in"])
            u["dsm"] = _dot_nt(u["dnum_b"], u["v"])
            u["qct"] = _dot_nt(u["dnum_b"], u["cb16"])
            u["vdc"] = _dot_nt(u["v"], u["dcb"])
        for u in units:
            ds = u["dsm"] + u["dden"]
            u["pb"] = _bf(u["w"] * ds)
            u["gmat"] = u["s_mat"] * ds
            u["dv1"] = _dot_tn(u["sb"], u["dnum_b"])
            u["dv2"] = _dot(_bf(u["kf32"] * u["coef"]), u["dcb"])
            u["dcu"] = _dot_tn(_bf(u["qf32"] * u["w_int"]), u["dnum_b"])
        for u in units:
            u["dq1"] = _dot(u["pb"], u["k"])
            u["dk1"] = _dot_tn(u["pb"], u["q"])
        acc = [dict(x1=jnp.zeros((ln, LANES), F32), x2=jnp.zeros((ln, LANES), F32), dig=jnp.zeros((ln, LANES), F32),
                    e_row=jnp.zeros((1, LANES), F32)) for _ in range(2)]
        for u in units:
            dr, h, sl, a = u["dr"], u["h"], u["sl"], acc[u["dr"]]
            dq_inter = u["w_int"] * (u["qct"] + u["dden"] * u["n_in"])
            dk_state = u["coef"] * (u["vdc"] + u["dn_new"])
            u["dqkv_ref"][:, sl] = _bf(u["dq1"] + dq_inter)
            u["dqkv_ref"][:, md + h * dh:md + (h + 1) * dh] = _bf(u["dk1"] + dk_state)
            u["dqkv_ref"][:, 2 * md + h * dh:2 * md + (h + 1) * dh] = _bf(u["dv1"] + u["dv2"])
            row_intra = jnp.sum(u["gmat"], axis=1, keepdims=True)
            col_intra = jnp.sum(u["gmat"].T, axis=1, keepdims=True)
            row_inter = jnp.sum(u["qf32"] * dq_inter, axis=1, keepdims=True)
            col_inter = jnp.sum(u["kf32"] * dk_state, axis=1, keepdims=True)
            e_old = u["a_old"] * (jnp.sum(jnp.sum(u["dc_new"] * u["c_in"], axis=1, keepdims=True), axis=0, keepdims=True)
                                  + jnp.sum(u["dn_new"] * u["n_in"], axis=1, keepdims=True))
            a["x1"] = a["x1"] + _lane_put(row_intra - col_intra + row_inter, u["cfl"])
            a["x2"] = a["x2"] + _lane_put(col_inter, u["cfl"])
            a["e_row"] = a["e_row"] + _lane_put(e_old, u["cfl"])
            a["dig"] = a["dig"] + _lane_put(col_intra + col_inter, u["ci"])
            dc_sc[dr, h] = u["a_old"] * u["dc_new"] + u["dcu"]
            dn_sc[dr, h] = jnp.broadcast_to(
                u["a_old"] * u["dn_new"] + jnp.sum(u["qf32"] * (u["w_int"] * u["dden"]), axis=0, keepdims=True), (8, dh))
        for dr, s in enumerate(sides):
            gz, mfl, a = gates[dr][0], gates[dr][6], acc[dr]
            dlogf = _mask_dot_t(mfl, a["x1"]) + _mask_dot(mfl, a["x2"]) - a["x2"] + a["e_row"]
            s[10][...] = a["dig"] + dlogf / (1.0 + jnp.exp(gz))

    def tok(cfn, col):
        return pl.BlockSpec((ln, md), lambda i: (cfn(i), col))

    def dht(cfn):
        return pl.BlockSpec((ln, md), lambda i: (jnp.minimum(cfn(i), nx - 1), 0))

    def gat(cfn):
        return pl.BlockSpec((ln, LANES), lambda i: (cfn(i), 0))

    def st(cfn, shape):
        return pl.BlockSpec((None,) + shape, lambda i: (cfn(i),) + (0,) * len(shape))

    st_shapes = ((nh * dh, dh), (8, md), (nh, 8, LANES))

    def side(cfn):
        return [tok(cfn, 0), tok(cfn, 1), tok(cfn, 2), gat(cfn), dht(cfn), tok(cfn, 0)] + [st(cfn, s) for s in st_shapes]

    def outs(cfn):
        return [pl.BlockSpec((ln, 3 * md), lambda i: (cfn(i), 0)), gat(cfn)]

    return pl.pallas_call(
        body, name="mlstm_bwd", grid=(nc,),
        in_specs=side(chunk_f) + side(chunk_b) + [pl.BlockSpec((1, LANES), lambda i: (0, 0))],
        out_specs=outs(chunk_f) + outs(chunk_b),
        out_shape=[SDS((s_rows, 3 * md), BF16), SDS((s_rows, LANES), F32)] * 2,
        scratch_shapes=[pltpu.VMEM((2, nh, dh, dh), F32), pltpu.VMEM((2, nh, 8, dh), F32)],
        compiler_params=_cp("arbitrary"))(qk, qk, z_main, zg, dhs, hf, *states_f, qk, qk, z_main, zg, dhs, hb, *states_b, bias)


def _qkv_conv_bwd(dqkv_f, dqkv_b, z_main, conv_w, t_rows, md, qscale):
    s_rows = z_main.shape[0]
    tb = _pick(s_rows, (1280, 1024, 256))
    cb = _pick(md, (512, 256, 128))
    ni, nj, ncq = s_rows // tb, 3 * md // cb, 2 * md // cb
    nb8 = tb // 8
    n_ext = tb + 16

    def body(fm, fp, fn, bm, bp, bn, zm, zp, zn, w_ref, dz_ref, gw_ref):
        j, i = pl.program_id(0), pl.program_id(1)

        @pl.when(j < ncq)
        def _():
            z = jnp.concatenate([zp[...], zm[...], zn[...]], axis=0).astype(F32)
            dqk = (jnp.concatenate([fp[...], fm[...], fn[...]], axis=0).astype(F32)
                   + jnp.concatenate([bp[...], bm[...], bn[...]], axis=0).astype(F32)) * jnp.where(j * cb < md, qscale, 1.0)
            row = i * tb - 8 + lax.broadcasted_iota(jnp.int32, (n_ext, 1), 0)
            prev_ok, next_ok = _seg_masks(row, t_rows, s_rows)
            zprev = jnp.where(prev_ok, pltpu.roll(z, 1, 0), 0.0)
            znext = jnp.where(next_ok, pltpu.roll(z, n_ext - 1, 0), 0.0)
            pre = w_ref[0:1, :] * zprev + w_ref[1:2, :] * z + w_ref[2:3, :] * znext
            sg = _sigmoid(pre)
            dpre = dqk * (sg * (1.0 + pre * (1.0 - sg)))
            dz = (w_ref[1:2, :] * dpre + w_ref[0:1, :] * jnp.where(next_ok, pltpu.roll(dpre, n_ext - 1, 0), 0.0)
                  + w_ref[2:3, :] * jnp.where(prev_ok, pltpu.roll(dpre, 1, 0), 0.0))
            dz_ref[...] = _bf(dz[8:8 + tb])
            dm = dpre[8:8 + tb]

            @pl.when(i == 0)
            def _():
                gw_ref[...] = jnp.zeros_like(gw_ref)

            gw_ref[...] += jnp.concatenate(
                [jnp.sum(dm * zprev[8:8 + tb], axis=0, keepdims=True), jnp.sum(dm * z[8:8 + tb], axis=0, keepdims=True),
                 jnp.sum(dm * znext[8:8 + tb], axis=0, keepdims=True), jnp.zeros((5, cb), F32)], axis=0)

        @pl.when(j >= ncq)
        def _():
            dz_ref[...] = _bf(fm[...].astype(F32) + bm[...].astype(F32))

    def halo(clampj):
        def cj(j):
            return jnp.minimum(j, ncq - 1) if clampj else j
        return [pl.BlockSpec((tb, cb), lambda j, i: (i, cj(j))),
                pl.BlockSpec((8, cb), lambda j, i: (jnp.maximum(i * nb8 - 1, 0), cj(j))),
                pl.BlockSpec((8, cb), lambda j, i: (jnp.minimum((i + 1) * nb8, s_rows // 8 - 1), cj(j)))]

    return pl.pallas_call(
        body, name="qkv_conv_bwd", grid=(nj, ni),
        in_specs=halo(False) + halo(False) + halo(True) + [pl.BlockSpec((8, cb), lambda j, i: (0, jnp.minimum(j, ncq - 1)))],
        out_specs=[pl.BlockSpec((tb, cb), lambda j, i: (i, j)), pl.BlockSpec((8, cb), lambda j, i: (0, jnp.minimum(j, ncq - 1)))],
        out_shape=[SDS((s_rows, 3 * md), BF16), SDS((8, 2 * md), F32)],
        compiler_params=_cp("arbitrary", "arbitrary"))(dqkv_f, dqkv_f, dqkv_f, dqkv_b, dqkv_b, dqkv_b, z_main, z_main, z_main, conv_w)


def _gate_grad_sum(dg_f, dg_b):
    s_rows = dg_f.shape[0]
    tb = _pick(s_rows, (1280, 1024, 256))

    def body(a_ref, b_ref, o_ref, st_ref):
        @pl.when(pl.program_id(0) == 0)
        def _():
            st_ref[...] = jnp.zeros_like(st_ref)

        s = a_ref[...] + b_ref[...]
        o_ref[...] = _bf(s)
        st_ref[...] += jnp.concatenate([jnp.sum(s, axis=0, keepdims=True), jnp.zeros((7, LANES), F32)], axis=0)

    blk = pl.BlockSpec((tb, LANES), lambda i: (i, 0))
    return pl.pallas_call(
        body, name="gate_grad_sum", grid=(s_rows // tb,), in_specs=[blk, blk],
        out_specs=[blk, pl.BlockSpec((8, LANES), lambda i: (0, 0))],
        out_shape=[SDS((s_rows, LANES), BF16), SDS((8, LANES), F32)], compiler_params=_cp("arbitrary"))(dg_f, dg_b)


def _mod_grads(silu_slots, dmx_sh, dmx_slots, dmc_tot, dmc_sh, silu_cctx, c_ctx, w_mod_c):
    d = silu_slots.shape[1]
    ncol, n6 = dmx_sh.shape[1], dmx_slots.shape[1]

    def body(ss_ref, dsh_ref, dsl_ref, dct_ref, dcs_ref, sc_ref, c_ref, w_ref, gw_ref, gb_ref, gc_ref):
        a = jnp.concatenate([ss_ref[...], sc_ref[...], jnp.zeros((7, d), F32)], axis=0)
        b = jnp.concatenate([dsh_ref[...], dcs_ref[...], jnp.zeros((7, ncol), F32)], axis=0)
        gw_ref[0] = lax.dot_general(a, b, (((0,), (0,)), ((), ())), preferred_element_type=F32, precision=HI)
        dct = dct_ref[...]
        gb_ref[...] = jnp.sum(dsl_ref[...], axis=0, keepdims=True) + jnp.concatenate(
            [dct, jnp.zeros((1, n6 - dct.shape[1]), F32)], axis=1)
        t = _dot_nt(_bf(jnp.broadcast_to(dct, (8, dct.shape[1]))), w_ref[...])
        cv = c_ref[...]
        s = _sigmoid(cv)
        gc_ref[...] = t[0:1, :] * (s * (1.0 + cv * (1.0 - s)))

    return pl.pallas_call(body, name="mod_grads", out_shape=[SDS((1, d, ncol), F32), SDS((1, n6), F32), SDS((1, d), F32)],
                          compiler_params=_cp())(silu_slots, dmx_sh, dmx_slots, dmc_tot, dmc_sh, silu_cctx, c_ctx, w_mod_c)


def _slot_sum(slots):
    ns, r = slots.shape[0], slots.shape[1]
    tb = _pick(r, (1024, 512, 256, 128, 64, 32, 16, 8))

    def body(s_ref, o_ref):
        acc = s_ref[0]
        for k in range(1, ns):
            acc = acc + s_ref[k]
        o_ref[...] = acc

    return pl.pallas_call(
        body, name="slot_sum", grid=(r // tb,), in_specs=[pl.BlockSpec((ns, tb, LANES), lambda i: (0, i, 0))],
        out_specs=pl.BlockSpec((tb, LANES), lambda i: (i, 0)), out_shape=SDS((r, LANES), F32),
        compiler_params=_cp("arbitrary"))(slots)


def _adamw(w, gslots, m, v, name, after=None):
    lead = ((None,), (0,)) if w.ndim == 3 else ((), ())
    r, cdim = w.shape[-2:]
    ns, rg = gslots.shape[0], gslots.shape[1]
    tb = r if (rg != r or r % 8) else _pick(r, (128, 64, 32, 16, 8))
    bc1, bc2 = 1.0 - ADAM_B1 ** ADAM_STEP, 1.0 - ADAM_B2 ** ADAM_STEP

    def body(w_ref, g_ref, m_ref, v_ref, *rest):
        go_ref, d_ref, mo_ref, vo_ref = rest[-4:]
        g = g_ref[0, 0:tb, :].astype(F32)
        for k in range(1, ns):
            g = g + g_ref[k, 0:tb, :].astype(F32)
        mn = ADAM_B1 * m_ref[...] + (1.0 - ADAM_B1) * g
        vn = ADAM_B2 * v_ref[...] + (1.0 - ADAM_B2) * (g * g)
        go_ref[...] = g
        mo_ref[...] = mn
        vo_ref[...] = vn
        d_ref[...] = -ADAM_LR * ((mn / bc1) / (jnp.sqrt(vn / bc2) + ADAM_EPS) + ADAM_WD * w_ref[...])

    blk = pl.BlockSpec(lead[0] + (tb, cdim), lambda i: lead[1] + (i, 0))
    gblk = pl.BlockSpec((ns, tb if rg == r else rg, cdim), lambda i: (0, i, 0))
    extra_specs = [] if after is None else [pl.BlockSpec((8, LANES), lambda i: (0, 0))]
    extra_args = [] if after is None else [after]
    return pl.pallas_call(
        body, name=name, grid=(r // tb,), in_specs=[blk, gblk, blk, blk] + extra_specs,
        out_specs=[blk] * 4, out_shape=[SDS(w.shape, F32)] * 4, compiler_params=_cp("arbitrary"))(w, gslots, m, v, *extra_args)


def _pack(parts, row_mult):
    flat = jnp.concatenate([p.reshape(-1) for p in parts])
    n = flat.shape[0]
    rows = -(-n // LANES)
    rows = -(-rows // row_mult) * row_mult
    return jnp.pad(flat, (0, rows * LANES - n)).reshape(rows, LANES)


def _unpack(buf, shapes):
    flat = buf.reshape(-1)
    out, off = [], 0
    for s in shapes:
        n = math.prod(s)
        out.append(flat[off:off + n].reshape(s))
        off += n
    return out


def _pad_cols(a, width):
    return jnp.pad(a, ((0, 0), (0, width - a.shape[1])))


def _pad_lanes(a):
    return _pad_cols(a, LANES)


def _up128(n):
    return -(-n // LANES) * LANES


def kernel(x, c, ctx, c_ctx, w_mod, b_mod, norm1_g, w_in, b_gate, conv_qk, head_norm_g, sgu_ln_g, sgu_ln_b, w_s, b_s, w_branch_mlstm, w_branch_sgu, w_out, norm2_g, w_up, w_ffn_conv, w_down, final_g, loss_target, m_c_ctx, m_w_mod, m_b_mod, m_norm1_g, m_w_in, m_b_gate, m_conv_qk, m_head_norm_g, m_sgu_ln_g, m_sgu_ln_b, m_w_s, m_b_s, m_w_branch_mlstm, m_w_branch_sgu, m_w_out, m_norm2_g, m_w_up, m_w_ffn_conv, m_w_down, m_final_g, v_c_ctx, v_w_mod, v_b_mod, v_norm1_g, v_w_in, v_b_gate, v_conv_qk, v_head_norm_g, v_sgu_ln_g, v_sgu_ln_b, v_w_s, v_b_s, v_w_branch_mlstm, v_w_branch_sgu, v_w_out, v_norm2_g, v_w_up, v_w_ffn_conv, v_w_down, v_final_g):
    t, d = x.shape[1], x.shape[2]
    n_ctx = ctx.shape[1]
    s_rows = t + n_ctx
    nh = b_gate.shape[1] // 4
    md = head_norm_g.shape[1]
    dh = md // nh
    ng, sc = w_s.shape[1], w_s.shape[2]
    dff = w_down.shape[1] * N_DEV
    n_in = w_in.shape[2] * N_DEV
    assert md == d and sgu_ln_g.shape[1] == d and n_ctx == LCH and t % LCH == 0 and t % (8 * GRID_W) == 0
    assert n_in == 8 * d + 4 * nh and 4 * nh <= LANES
    me = 4 * lax.axis_index("x") + 2 * lax.axis_index("y") + lax.axis_index("c")

    n_mod, n_insh, n_upsh = w_mod.shape[2], w_in.shape[2], w_up.shape[2]
    p_mod, p_in, p_up = _up128(n_mod), _up128(n_insh), _up128(n_upsh)
    nq, nf = conv_qk.shape[2], w_ffn_conv.shape[3]
    ffn9 = w_ffn_conv[0].reshape(9, nf)
    colpack = jnp.concatenate([_pad_cols(_bf(w_mod[0]), p_mod), _pad_cols(_bf(w_in[0]), p_in)], axis=1)
    convpack = jnp.concatenate([jnp.pad(conv_qk[0], ((0, 13), (0, 0))), jnp.pad(ffn9, ((0, 7), (0, 0)))], axis=1)
    g_col, g_conv = _allgather([colpack, convpack])
    w_mod_f, w_main, w_gate = _assemble_cols(
        g_col, [(0, n_mod, [(0, 0, N_DEV * n_mod, 0)]),
                (p_mod, n_insh, [(1, 0, 3 * md, 0), (2, 3 * md, 4 * nh, 0), (1, 3 * md + 4 * nh, 5 * d, 3 * md)])],
        [N_MOD * d, 8 * d, LANES], "assemble_weights")
    convw, wconv9 = _assemble_cols(g_conv, [(0, nq, [(0, 0, N_DEV * nq, 0)]), (nq, nf, [(1, 0, N_DEV * nf, 0)])],
                                   [N_DEV * nq, N_DEV * nf], "assemble_conv_weights")
    zero = jnp.minimum(jnp.abs(g_conv[0, 0, 0]), 0.0)
    late_w = [_pad_cols(_bf(w_up[0] + zero), p_up), _bf(w_branch_mlstm[0]), _bf(w_branch_sgu[0]), _bf(w_out[0]), _bf(w_down[0])]
    late_state, late_tok = _exchange_start(late_w, False, "late_weights_start")

    cvec = jnp.concatenate([c, c_ctx[None], jnp.zeros((6, d), F32)], axis=0) + late_tok[0:1, 0:1]
    silu_v, mod = _modulation(cvec, w_mod_f, b_mod)
    mx = [mod[0:1, k * d:(k + 1) * d] for k in range(N_MOD)]
    mc = [mod[1:2, k * d:(k + 1) * d] for k in range(2)]
    x2, ctx2 = x[0], ctx[0]
    in_x = _norm_mod_proj(x2, norm1_g, jnp.concatenate([mx[0], mx[1]], axis=0), w_main, w_gate, s_rows, 0, None, "in_proj")
    hn, z_main, zg = _norm_mod_proj(ctx2, norm1_g, jnp.concatenate([mc[0], mc[1]], axis=0), w_main, w_gate, s_rows, t, in_x,
                                    "in_proj_ctx")
    qscale = dh ** -0.5
    qk = _qk_conv(z_main, convw, t, md, qscale)
    bias = _pad_lanes(b_gate)
    fwd = _mlstm_fwd(qk, z_main, zg, bias, nh)
    hf, hb, states_f, states_b = fwd[0], fwd[1], fwd[2:5], fwd[5:8]
    g_up, g_bm, g_bs, g_out, g_down = _exchange_wait(late_state, fwd[4], "late_weights_wait")
    (w_up_f,) = _assemble_cols(g_up, [(0, n_upsh, [(0, 0, 2 * dff, 0)])], [2 * dff], "assemble_w_up")
    wbm_f, wbs_f, wout_f = (g.reshape(d, d) for g in (g_bm, g_bs, g_out))
    w_down_f = g_down.reshape(dff, d)
    b_st = _pad_lanes(b_s[0].T)
    h1, ym, ys, pm, ps, y, out = _mixer_fwd(hf, hb, z_main, x2, head_norm_g, sgu_ln_g, sgu_ln_b, w_s[0], b_st, wbm_f, wbs_f,
                                            wout_f, mx[2], t, nh)
    hn2, ab = _norm_mod_proj(h1, norm2_g, jnp.concatenate([mx[3], mx[4]], axis=0), w_up_f, None, t, 0, None, "up_proj")
    aconv, f, dh2, dffn, st_tail = _ffn_tail(ab, wconv9, w_down_f, h1, mx[5], final_g[None], loss_target[0], dff)

    db, dac = _ffn_bwd_gate(dffn, w_down_f, aconv, ab, dff)
    da, g_wconv9 = _ffn_conv_bwd(dac, ab, wconv9, dff)
    g_wdown = _wgrad(f, dffn, t, "wgrad_down")
    gwup_slots = _scatter_cols([_wgrad(hn2, da, t, "wgrad_up_a"), _wgrad(hn2, db, t, "wgrad_up_b")],
                               [(0, 0, dff, 0), (1, dff, dff, 0)], n_upsh, "scatter_grad_w_up")
    dh1, st_n2 = _proj_norm_bwd([(da, 0, w_up_f, 0, dff, dff), (db, 0, w_up_f, dff, dff, dff)], h1, 0, norm2_g, mx[4], dh2, t,
                                "up_proj_bwd", (512, 256))
    dz_rest, dhs, dout, dpm, dps, st_mix, g_ws, g_bst = _mixer_bwd(dh1, out, hf, hb, z_main, pm, ps, head_norm_g, sgu_ln_g,
                                                                    sgu_ln_b, w_s[0], b_st, wbm_f, wbs_f, wout_f, mx[2], t, nh)
    g_wout = _wgrad(y, dout, t, "wgrad_out")
    g_wbm = _wgrad(ym, dpm, t, "wgrad_branch_mlstm")
    g_wbs = _wgrad(ys, dps, t, "wgrad_branch_sgu")
    ex_a = [gwup_slots, g_wdown.reshape(N_DEV, dff // N_DEV, d), g_wbm.reshape(N_DEV, d // N_DEV, d),
            g_wbs.reshape(N_DEV, d // N_DEV, d), g_wout.reshape(N_DEV, d // N_DEV, d)]
    ex_a_state, ex_a_tok = _exchange_start(ex_a, True, "grad_exchange_a_start")
    dqkv_f, dg_f, dqkv_b, dg_b = _mlstm_bwd(qk, z_main, zg, bias + ex_a_tok[0:1, :], dhs, hf, hb, states_f, states_b, nh, t)
    dz_qkv, g_convqk = _qkv_conv_bwd(dqkv_f, dqkv_b, z_main, convw, t, md, qscale)
    dz_g, st_gate = _gate_grad_sum(dg_f, dg_b)
    gwin_slots = _scatter_cols(
        [_wgrad(hn, dz_qkv, s_rows, "wgrad_in_qkv"), _wgrad(hn, dz_g, s_rows, "wgrad_in_gate"), _wgrad(hn, dz_rest, t, "wgrad_in_rest")],
        [(0, 0, 3 * md, 0), (1, 3 * md, 4 * nh, 0), (2, 3 * md + 4 * nh, 5 * d, 0)], n_insh, "scatter_grad_w_in")
    gcq_slots = _scatter_cols([g_convqk], [(0, 0, 2 * md, 0)], nq, "scatter_grad_conv_qk")
    gcf_slots = _scatter_cols([g_wconv9], [(0, 0, dff, 0)], nf, "scatter_grad_ffn_conv")
    ex_b_state, ex_b_tok = _exchange_start([gwin_slots, gcq_slots, gcf_slots], True, "grad_exchange_b_start")
    tk = _pick(md, (1024, 512, 256))
    grad_x, st_n1x = _proj_norm_bwd(
        [(dz_qkv, 0, w_main, 0, 3 * md, tk), (dz_rest, 0, w_main, 3 * md, 5 * d, tk), (dz_g, 0, w_gate, 0, LANES, LANES)],
        x2, 0, norm1_g, mx[1] + ex_b_tok[0:1, 0:1], dh1, t, "in_proj_bwd")
    (st_n1c,) = _proj_norm_bwd([(dz_qkv, t, w_main, 0, 3 * md, tk), (dz_g, t, w_gate, 0, LANES, LANES)],
                               ctx2, 0, norm1_g, mc[1] + ex_b_tok[0:1, 0:1], None, n_ctx, "in_proj_bwd_ctx")

    rx_a = _exchange_wait(ex_a_state, st_n1c, "grad_exchange_a_wait")
    rx_b = _exchange_wait(ex_b_state, st_n1c, "grad_exchange_b_wait")
    recv = [rx_b[0], rx_a[0], rx_a[2], rx_a[3], rx_a[4], rx_a[1], rx_b[1], rx_b[2]]
    small_parts = [st_n1x[1], st_n1x[2], st_mix[0], st_n2[1], st_n2[2], st_tail[1],
                   st_n1c[1], st_n1c[2],
                   silu_v[0], st_n1x[0] + st_n1c[0], st_gate[0], st_mix[1], st_mix[2], st_mix[3],
                   g_ws.reshape(-1), g_bst[:, :ng].T.reshape(-1), st_n2[0], st_tail[0],
                   st_tail[2, :LANES]]
    small_state, small_tok = _exchange_start([_pack(small_parts, 8)], False, "small_exchange_start")

    shard_w = (w_in, w_up, w_branch_mlstm, w_branch_sgu, w_out, w_down, conv_qk)
    shard_m = (m_w_in, m_w_up, m_w_branch_mlstm, m_w_branch_sgu, m_w_out, m_w_down, m_conv_qk)
    shard_v = (v_w_in, v_w_up, v_w_branch_mlstm, v_w_branch_sgu, v_w_out, v_w_down, v_conv_qk)
    shard_names = ("w_in", "w_up", "w_branch_mlstm", "w_branch_sgu", "w_out", "w_down", "conv_qk")
    shard_out = [_adamw(wa, recv[k], ma, va, "adamw_" + nm, small_tok)
                 for k, (wa, ma, va, nm) in enumerate(zip(shard_w, shard_m, shard_v, shard_names))]
    shard_out.append([b.reshape(w_ffn_conv.shape) for b in
                      _adamw(ffn9, recv[7], m_w_ffn_conv[0].reshape(9, nf), v_w_ffn_conv[0].reshape(9, nf), "adamw_w_ffn_conv",
                             small_tok)])

    (recv_small,) = _exchange_wait(small_state, shard_out[5][1], "small_exchange_wait")
    small_sum = _slot_sum(recv_small).reshape(-1)
    small_slots = recv_small.reshape(N_DEV, -1)
    o_silu, o_n1 = 8 * d, 9 * d
    ncol = N_MOD * d // N_DEV
    dmc_tot = small_sum[6 * d:8 * d][None]
    dmc_pad = jnp.concatenate([dmc_tot, jnp.zeros((1, 4 * d), F32)], axis=1)
    g_wmod, g_bmod, g_cctx = _mod_grads(
        small_slots[:, o_silu:o_silu + d], lax.dynamic_slice_in_dim(small_slots[:, :6 * d], me * ncol, ncol, axis=1),
        small_slots[:, :6 * d], dmc_tot, lax.dynamic_slice_in_dim(dmc_pad, me * ncol, ncol, axis=1), silu_v[1:2], c_ctx[None],
        w_mod_f[:, :2 * d])
    mod_out = _adamw(w_mod, g_wmod, m_w_mod, v_w_mod, "adamw_w_mod")

    def rep(cc, bm, n1, bg, hg, lg, lb, ws, bs, n2, fg):
        return [cc.reshape(-1), bm.reshape(-1), n1.reshape(-1), _pad_lanes(bg.reshape(1, -1)).reshape(-1), hg.reshape(-1),
                lg.reshape(-1), lb.reshape(-1), ws.reshape(-1), bs.reshape(-1), n2.reshape(-1), fg.reshape(-1)]

    o = o_n1
    g_rep_parts = [g_cctx, g_bmod]
    for n in (d, LANES, d, d, d, ng * sc * sc, ng * sc, d, d):
        g_rep_parts.append(small_sum[o:o + n])
        o += n
    rep_shapes = [(d,), (1, N_MOD * d), (1, d), (1, LANES), (1, d), (1, d), (1, d), (1, ng, sc, sc), (1, ng, sc), (1, d), (d,)]
    rep_out = _adamw(
        _pack(rep(c_ctx, b_mod, norm1_g, b_gate, head_norm_g, sgu_ln_g, sgu_ln_b, w_s, b_s, norm2_g, final_g), LANES),
        _pack(g_rep_parts, LANES)[None],
        _pack(rep(m_c_ctx, m_b_mod, m_norm1_g, m_b_gate, m_head_norm_g, m_sgu_ln_g, m_sgu_ln_b, m_w_s, m_b_s, m_norm2_g, m_final_g), LANES),
        _pack(rep(v_c_ctx, v_b_mod, v_norm1_g, v_b_gate, v_head_norm_g, v_sgu_ln_g, v_sgu_ln_b, v_w_s, v_b_s, v_norm2_g, v_final_g), LANES),
        "adamw_replicated")

    def assemble(k):
        r = _unpack(rep_out[k], rep_shapes)
        s = [o[k] for o in shard_out]
        return [r[0], mod_out[k], r[1], r[2], s[0], r[3][:, :4 * nh], s[6], r[4], r[5], r[6], r[7], r[8], s[2], s[3], s[4], r[9],
                s[1], s[7], s[5], r[10]]

    loss = small_sum[o]
    outs = [loss, grad_x[None]]
    for k in range(4):
        outs += assemble(k)
    return tuple(outs)
```

```python
import math

import jax
import jax.numpy as jnp
from jax import lax
from jax.experimental import pallas as pl
from jax.experimental.pallas import tpu as pltpu

F32, BF16 = jnp.float32, jnp.bfloat16
EPS = 1e-6
M_INIT = -1e30
NEG = -1e30
GRID_W = 64
LCH = 256
N_MOD = 6
N_DEV = 8
LANES = 128
ADAM_LR, ADAM_B1, ADAM_B2, ADAM_EPS, ADAM_WD, ADAM_STEP = 0.001, 0.9, 0.999, 1e-08, 0.01, 10
GELU_C = math.sqrt(2.0 / math.pi)
GELU_A = 0.044715
VMEM_LIMIT = 56 * 1024 * 1024
HI = lax.Precision.HIGHEST
SDS = jax.ShapeDtypeStruct
MESH_ID = pl.DeviceIdType.MESH


def _pick(n, cands):
    for c in cands:
        if n % c == 0:
            return c
    raise ValueError(f"no block size for {n} in {cands}")


def _cp(*sem):
    return pltpu.CompilerParams(dimension_semantics=sem if sem else None, vmem_limit_bytes=VMEM_LIMIT)


def _sigmoid(x):
    return 0.5 * jnp.tanh(0.5 * x) + 0.5


def _split3(x):
    hi = x.astype(BF16)
    r = x - hi.astype(F32)
    mid = r.astype(BF16)
    return hi, mid, (r - mid.astype(F32)).astype(BF16)


def _mask_dot(mask_b, x):
    hi, mid, lo = _split3(x)
    return (_dot(mask_b, lo) + _dot(mask_b, mid)) + _dot(mask_b, hi)


def _mask_dot_t(mask_b, x):
    hi, mid, lo = _split3(x)
    return (_dot_tn(mask_b, lo) + _dot_tn(mask_b, mid)) + _dot_tn(mask_b, hi)


def _gelu(x):
    return x * (0.5 * (1.0 + jnp.tanh(GELU_C * x * (1.0 + GELU_A * (x * x)))))


def _gelu_and_grad(x):
    x2 = x * x
    t = jnp.tanh(GELU_C * x * (1.0 + GELU_A * x2))
    half = 0.5 * (1.0 + t)
    return x * half, half + (0.5 * GELU_C) * x * (1.0 - t * t) * (1.0 + 3.0 * GELU_A * x2)


def _log_sigmoid(x):
    return jnp.minimum(x, 0.0) - jnp.log(1.0 + jnp.exp(-jnp.abs(x)))


def _dot(a, b):
    return jnp.dot(a, b, preferred_element_type=F32)


def _dot_nt(a, b):
    return lax.dot_general(a, b, (((1,), (1,)), ((), ())), preferred_element_type=F32)


def _dot_tn(a, b):
    return lax.dot_general(a, b, (((0,), (0,)), ((), ())), preferred_element_type=F32)


def _bf(x):
    return x.astype(BF16)


def _allgather(arrs):
    na = len(arrs)

    def body(*refs):
        x_refs, o_refs = refs[:na], refs[na:2 * na]
        send_sems, recv_sems, local_sems = refs[2 * na:]
        x, y, c = lax.axis_index("x"), lax.axis_index("y"), lax.axis_index("c")
        me, sibling = (x, y, c), (x, y, 1 - c)
        chips = [(1 - x, y), (x, 1 - y), (1 - x, 1 - y)]

        def copy(a, k, block, to, src=None):
            slot = o_refs[a].at[4 * block[0] + 2 * block[1] + block[2]]
            return pltpu.make_async_remote_copy(
                src_ref=slot if src is None else src, dst_ref=slot, send_sem=send_sems.at[7 * a + k],
                recv_sem=recv_sems.at[7 * a + k], device_id=to, device_id_type=MESH_ID)

        mine = [pltpu.make_async_copy(x_refs[a], o_refs[a].at[4 * x + 2 * y + c], local_sems.at[a]) for a in range(na)]
        for cp in mine:
            cp.start()
        first = []
        for a in range(na):
            first.append(copy(a, 0, me, sibling, src=x_refs[a]))
            first += [copy(a, 1 + j, me, (*chip, c), src=x_refs[a]) for j, chip in enumerate(chips)]
        for cp in first:
            cp.start()
        passed = []
        for j, chip in enumerate(chips):
            for a in range(na):
                copy(a, 1 + j, (*chip, c), me).wait_recv()
                passed.append(copy(a, 4 + j, (*chip, c), sibling))
                passed[-1].start()
        for a in range(na):
            copy(a, 0, sibling, me).wait_recv()
            for j, chip in enumerate(chips):
                copy(a, 4 + j, (*chip, 1 - c), me).wait_recv()
        for cp in first + passed:
            cp.wait_send()
        for cp in mine:
            cp.wait()

    anyspec = pl.BlockSpec(memory_space=pl.ANY)
    return pl.pallas_call(
        body, name="weights_allgather",
        out_shape=[SDS((N_DEV,) + a.shape, a.dtype) for a in arrs],
        in_specs=[anyspec] * na, out_specs=[anyspec] * na,
        scratch_shapes=[pltpu.SemaphoreType.DMA((7 * na,)), pltpu.SemaphoreType.DMA((7 * na,)), pltpu.SemaphoreType.DMA((na,))],
    )(*arrs)


_HBM_SPEC = pl.BlockSpec(memory_space=pltpu.HBM)
_SEM_SPEC = pl.BlockSpec(memory_space=pltpu.SEMAPHORE)
_EFFECT = pltpu.SideEffectType.DATAFLOW_SIDE_EFFECTING


def _peer_list(x, y, c):
    out = []
    for k in range(1, N_DEV):
        px = 1 - x if k & 4 else x
        py = 1 - y if k & 2 else y
        pc = 1 - c if k & 1 else c
        out.append(((px, py, pc), 4 * px + 2 * py + pc))
    return out


def _split_copies(src, land, send_sems, recv_sems, per_dest, receive):
    x, y, c = lax.axis_index("x"), lax.axis_index("y"), lax.axis_index("c")
    me = 4 * x + 2 * y + c
    out = []
    for k, (peer, pidx) in enumerate(_peer_list(x, y, c)):
        for a in range(len(src)):
            out.append(pltpu.make_async_remote_copy(
                src_ref=src[a].at[pidx] if per_dest else src[a], dst_ref=land[a].at[pidx if receive else me],
                send_sem=send_sems.at[7 * a + k], recv_sem=recv_sems.at[7 * a + k], device_id=peer, device_id_type=MESH_ID))
    return out


def _own_copies(src, land, own_sems, per_dest):
    me = 4 * lax.axis_index("x") + 2 * lax.axis_index("y") + lax.axis_index("c")
    return [pltpu.make_async_copy(src[a].at[me] if per_dest else src[a], land[a].at[me], own_sems.at[a]) for a in range(len(src))]


def _exchange_start(arrs, per_dest, name):
    na = len(arrs)
    land_shapes = [a.shape if per_dest else (N_DEV,) + a.shape for a in arrs]
    lands = [pltpu.with_memory_space_constraint(lax.empty(s, a.dtype), pltpu.HBM) for s, a in zip(land_shapes, arrs)]

    def body(*refs):
        src, land = refs[:na], refs[na:2 * na]
        send_sems, recv_sems, own_sems, token = refs[2 * na], refs[2 * na + 1], refs[2 * na + 2], refs[-1]
        for cp in _split_copies(src, land, send_sems, recv_sems, per_dest, False) + _own_copies(src, land, own_sems, per_dest):
            cp.start()
        token[...] = jnp.zeros_like(token)

    outs = pl.pallas_call(
        body, name=name,
        out_shape=[pltpu.SemaphoreType.DMA((7 * na,)), pltpu.SemaphoreType.DMA((7 * na,)), pltpu.SemaphoreType.DMA((na,))]
        + [pltpu.HBM(a.shape, a.dtype) for a in arrs] + [pltpu.HBM(s, a.dtype) for s, a in zip(land_shapes, arrs)]
        + [SDS((8, LANES), F32)],
        in_specs=[_HBM_SPEC] * (2 * na),
        out_specs=[_SEM_SPEC] * 3 + [_HBM_SPEC] * (2 * na) + [pl.BlockSpec(memory_space=pltpu.VMEM)],
        input_output_aliases={k: 3 + k for k in range(2 * na)},
        compiler_params=pltpu.CompilerParams(has_side_effects=_EFFECT),
    )(*[pltpu.with_memory_space_constraint(a, pltpu.HBM) for a in arrs], *lands)
    return (na, per_dest, outs[:-1]), outs[-1]


def _exchange_wait(state, after, name):
    na, per_dest, started = state

    def body(*refs):
        src, land = refs[:na], refs[na:2 * na]
        send_sems, recv_sems, own_sems = refs[2 * na], refs[2 * na + 1], refs[2 * na + 2]
        for cp in _split_copies(src, land, send_sems, recv_sems, per_dest, True):
            cp.wait_send()
            cp.wait_recv()
        for cp in _own_copies(src, land, own_sems, per_dest):
            cp.wait()

    bufs = started[3:]
    outs = pl.pallas_call(
        body, name=name,
        out_shape=[pltpu.HBM(b.shape, b.dtype) for b in bufs],
        in_specs=[_HBM_SPEC] * (2 * na) + [_SEM_SPEC] * 3 + [pl.BlockSpec(memory_space=pl.ANY)],
        out_specs=[_HBM_SPEC] * (2 * na),
        input_output_aliases={k: k for k in range(2 * na)},
        compiler_params=pltpu.CompilerParams(has_side_effects=_EFFECT),
    )(*bufs, started[0], started[1], started[2], after)
    return outs[na:]


def _col_pieces(n, segments):
    out = []
    for j in range(N_DEV):
        lo, hi = j * n, (j + 1) * n
        for (k, s0, w, c0) in segments:
            a, b = max(lo, s0), min(hi, s0 + w)
            if a < b:
                out.append((j, a - lo, b - lo, k, c0 + a - s0, c0 + b - s0))
    return out


def _assemble_cols(slots, groups, out_widths, name):
    r, p = slots.shape[1], slots.shape[2]
    tb = _pick(r, (128, 64, 32, 16, 8))
    covered = [0] * len(out_widths)
    for (_, n, segs) in groups:
        for (k, _, w, _) in segs:
            covered[k] += w

    def body(s_ref, *o_refs):
        for k, wd in enumerate(out_widths):
            if covered[k] < wd:
                o_refs[k][...] = jnp.zeros_like(o_refs[k])
        for (off, n, segs) in groups:
            for (j, a0, a1, k, d0, d1) in _col_pieces(n, segs):
                o_refs[k][:, d0:d1] = s_ref[j, :, off + a0:off + a1]

    return pl.pallas_call(
        body, name=name, grid=(r // tb,), in_specs=[pl.BlockSpec((N_DEV, tb, p), lambda i: (0, i, 0))],
        out_specs=[pl.BlockSpec((tb, w), lambda i: (i, 0)) for w in out_widths],
        out_shape=[SDS((r, w), slots.dtype) for w in out_widths], compiler_params=_cp("arbitrary"))(slots)


def _scatter_cols(pieces, segments, n, name):
    r = pieces[0].shape[0]
    tb = _pick(r, (128, 64, 32, 16, 8))

    def body(*refs):
        p_refs, o_ref = refs[:-1], refs[-1]
        for (j, a0, a1, k, d0, d1) in _col_pieces(n, segments):
            o_ref[j, :, a0:a1] = p_refs[k][:, d0:d1]

    return pl.pallas_call(
        body, name=name, grid=(r // tb,), in_specs=[pl.BlockSpec((tb, a.shape[1]), lambda i: (i, 0)) for a in pieces],
        out_specs=pl.BlockSpec((N_DEV, tb, n), lambda i: (0, i, 0)), out_shape=SDS((N_DEV, r, n), pieces[0].dtype),
        compiler_params=_cp("arbitrary"))(*pieces)


def _modulation(cvec, w_mod, b_mod):
    d, n = w_mod.shape

    def body(c_ref, w_ref, b_ref, s_ref, o_ref):
        cv = c_ref[...]
        s = cv * _sigmoid(cv)
        s_ref[...] = s
        o_ref[...] = _dot(_bf(s), w_ref[...]) + b_ref[...]

    return pl.pallas_call(body, name="modulation", out_shape=(SDS((8, d), F32), SDS((8, n), F32)),
                          compiler_params=_cp())(cvec, w_mod, b_mod)


def _norm_mod_proj(x_arr, g, shsc, w_main, w_gate, rows_total, row0, filled, name):
    m_rows, d = x_arr.shape
    n = w_main.shape[1]
    tb = _pick(m_rows, (1024, 256))
    cb = _pick(n, (2048, 1408, 1024, 768, 512, 384, 256, 128))
    gate = w_gate is not None
    nout = 3 if gate else 2
    nin = 5 if gate else 4
    rb = row0 // tb

    def body(*refs):
        x_ref, g_ref, ss_ref, wm_ref = refs[:4]
        wg_ref = refs[4] if gate else None
        outs = refs[len(refs) - 1 - nout:len(refs) - 1]
        hn_ref, z_ref = outs[0], outs[1]
        hn_sc = refs[-1]

        @pl.when(pl.program_id(1) == 0)
        def _():
            x = x_ref[...]
            r = lax.rsqrt(jnp.mean(x * x, axis=-1, keepdims=True) + EPS)
            hb = _bf((x * r * g_ref[...]) * (1.0 + ss_ref[1:2, :]) + ss_ref[0:1, :])
            hn_sc[...] = hb
            hn_ref[...] = hb
            if gate:
                outs[2][...] = _dot(hb, wg_ref[...])

        z_ref[...] = _bf(_dot(hn_sc[...], wm_ref[:, pl.ds(pl.multiple_of(pl.program_id(1) * cb, cb), cb)]))

    in_specs = [pl.BlockSpec((tb, d), lambda i, j: (i, 0)), pl.BlockSpec((1, d), lambda i, j: (0, 0)),
                pl.BlockSpec((2, d), lambda i, j: (0, 0)), _resident((d, n))]
    out_specs = [pl.BlockSpec((tb, d), lambda i, j: (rb + i, 0)), pl.BlockSpec((tb, cb), lambda i, j: (rb + i, j))]
    out_shape = [SDS((rows_total, d), BF16), SDS((rows_total, n), BF16)]
    args = [x_arr, g, shsc, w_main]
    if gate:
        in_specs.append(pl.BlockSpec((d, LANES), lambda i, j: (0, 0)))
        out_specs.append(pl.BlockSpec((tb, LANES), lambda i, j: (rb + i, 0)))
        out_shape.append(SDS((rows_total, LANES), F32))
        args.append(w_gate)
    aliases = {}
    if filled is not None:
        in_specs += [pl.BlockSpec(memory_space=pl.ANY)] * nout
        args += list(filled)
        aliases = {nin + k: k for k in range(nout)}
    return pl.pallas_call(
        body, name=name, grid=(m_rows // tb, n // cb), in_specs=in_specs, out_specs=out_specs, out_shape=out_shape,
        input_output_aliases=aliases, scratch_shapes=[pltpu.VMEM((tb, d), BF16)],
        compiler_params=_cp("arbitrary", "arbitrary"))(*args)


def _seg_masks(row, t_rows, s_rows):
    prev_ok = (row != 0) & (row != t_rows)
    next_ok = (row != t_rows - 1) & (row != s_rows - 1)
    return prev_ok, next_ok


def _shift_rows(z, halo_prev, halo_next, tb):
    loc = lax.broadcasted_iota(jnp.int32, (tb, 1), 0)
    zp = jnp.where(loc == 0, halo_prev, pltpu.roll(z, 1, 0))
    zn = jnp.where(loc == tb - 1, halo_next, pltpu.roll(z, tb - 1, 0))
    return zp, zn


def _qk_conv(z_main, conv_w, t_rows, md, qscale):
    s_rows = z_main.shape[0]
    tb = _pick(s_rows, (1280, 1024, 256))
    cb = _pick(md, (512, 256, 128))
    nb8 = tb // 8

    def body(zm, zp, zn, w_ref, o_ref):
        i, j = pl.program_id(0), pl.program_id(1)
        z = zm[...].astype(F32)
        zprev, znext = _shift_rows(z, zp[7:8, :].astype(F32), zn[0:1, :].astype(F32), tb)
        row = i * tb + lax.broadcasted_iota(jnp.int32, (tb, 1), 0)
        prev_ok, next_ok = _seg_masks(row, t_rows, s_rows)
        pre = (w_ref[0:1, :] * jnp.where(prev_ok, zprev, 0.0) + w_ref[1:2, :] * z
               + w_ref[2:3, :] * jnp.where(next_ok, znext, 0.0))
        scale = jnp.where(j * cb < md, qscale, 1.0)
        o_ref[...] = _bf(pre * _sigmoid(pre) * scale)

    return pl.pallas_call(
        body, name="qk_conv", grid=(s_rows // tb, 2 * md // cb),
        in_specs=[pl.BlockSpec((tb, cb), lambda i, j: (i, j)),
                  pl.BlockSpec((8, cb), lambda i, j: (jnp.maximum(i * nb8 - 1, 0), j)),
                  pl.BlockSpec((8, cb), lambda i, j: (jnp.minimum((i + 1) * nb8, s_rows // 8 - 1), j)),
                  pl.BlockSpec((8, cb), lambda i, j: (0, j))],
        out_specs=pl.BlockSpec((tb, cb), lambda i, j: (i, j)),
        out_shape=SDS((s_rows, 2 * md), BF16), compiler_params=_cp("arbitrary", "arbitrary"))(z_main, z_main, z_main, conv_w)


def _chunk_gates(gates, bias, rev):
    ln = gates.shape[0]
    gz = gates + bias
    logf = _log_sigmoid(gz)
    r_id = lax.broadcasted_iota(jnp.int32, (ln, ln), 0)
    c_id = lax.broadcasted_iota(jnp.int32, (ln, ln), 1)
    mask = (c_id >= r_id) if rev else (c_id <= r_id)
    mb = mask.astype(F32).astype(BF16)
    b_all = _mask_dot(mb, logf)
    g_all = jnp.sum(logf, axis=0, keepdims=True)
    return gz, b_all, b_all.T, gz.T, g_all, mask, mb


def _head_weights(b_col, b_row, i_row, m_in, mask):
    d = jnp.where(mask, b_col - b_row + i_row, NEG)
    inter = b_col + m_in
    m_row = jnp.maximum(inter, jnp.max(d, axis=1, keepdims=True))
    return jnp.exp(d - m_row), jnp.exp(inter - m_row), m_row


def _head_state_coeffs(g, b_col, i_col, m_in):
    a = g - b_col + i_col
    m_new = jnp.maximum(g + m_in, jnp.max(a, axis=0, keepdims=True))
    return jnp.exp(g + m_in - m_new), jnp.exp(a - m_new), m_new


def _mlstm_fwd(qk, z_main, zg, bias, nh):
    s_rows = qk.shape[0]
    md = qk.shape[1] // 2
    dh = md // nh
    nc = s_rows // LCH
    ln = LCH

    def chunk_f(i):
        return jnp.where(i == 0, nc - 1, i - 1)

    def chunk_b(i):
        return jnp.where(i == 0, nc - 1, nc - 1 - i)

    def body(qf, kf, vf, gf, qb, kb, vb, gb, bias_ref, hf_ref, hb_ref, cf_ref, nf_ref, mf_ref, cb_ref, nb_ref, mb_ref,
             c_sc, n_sc, m_sc):
        i = pl.program_id(0)

        @pl.when(i == 0)
        def _():
            c_sc[...] = jnp.zeros_like(c_sc)
            n_sc[...] = jnp.zeros_like(n_sc)
            m_sc[...] = jnp.full(m_sc.shape, M_INIT, F32)

        sides = ((qf, kf, vf, gf, hf_ref, cf_ref, nf_ref, mf_ref), (qb, kb, vb, gb, hb_ref, cb_ref, nb_ref, mb_ref))
        gates = [_chunk_gates(s[3][...], bias_ref[...], dr == 1) for dr, s in enumerate(sides)]
        units = []
        for dr, (q_ref, k_ref, v_ref, _, h_ref, c_out, n_out, m_out) in enumerate(sides):
            gz, b_all, b_t, g_t, g_all, mask, _ = gates[dr]
            for h in range(nh):
                ci, cf = 2 * dr * nh + h, (2 * dr + 1) * nh + h
                sl = slice(h * dh, (h + 1) * dh)
                u = dict(dr=dr, h=h, sl=sl, h_ref=h_ref, q=q_ref[:, sl], k=k_ref[:, sl], v=v_ref[:, sl],
                         c_in=c_sc[dr, h], n_in=n_sc[dr, h, 0:1, :], m_in=m_sc[dr, h, 0:1, 0:1],
                         b_col=b_all[:, cf:cf + 1], i_col=gz[:, ci:ci + 1], g=g_all[:, cf:cf + 1])
                c_out[sl, :] = u["c_in"]
                n_out[:, sl] = n_sc[dr, h]
                m_out[h] = m_sc[dr, h]
                u["w"], u["w_int"], u["m_row"] = _head_weights(u["b_col"], b_t[cf:cf + 1, :], g_t[ci:ci + 1, :], u["m_in"], mask)
                u["qk"] = _dot_nt(u["q"], u["k"])
                units.append(u)
        for u in units:
            u["s_mat"] = u["qk"] * u["w"]
            u["qc"] = _dot(u["q"], _bf(u["c_in"]))
            u["a_old"], u["coef"], u["m_new"] = _head_state_coeffs(u["g"], u["b_col"], u["i_col"], u["m_in"])
            u["kw"] = u["k"].astype(F32) * u["coef"]
        for u in units:
            u["sv"] = _dot(_bf(u["s_mat"]), u["v"])
            u["kv"] = _dot_tn(_bf(u["kw"]), u["v"])
        for u in units:
            dr, h = u["dr"], u["h"]
            num = u["sv"] + u["w_int"] * u["qc"]
            den = (jnp.sum(u["s_mat"], axis=1, keepdims=True)
                   + u["w_int"] * jnp.sum(u["q"].astype(F32) * u["n_in"], axis=1, keepdims=True))
            u["h_ref"][:, u["sl"]] = _bf(num / jnp.maximum(jnp.abs(den), jnp.exp(-u["m_row"])))
            c_sc[dr, h] = u["a_old"] * u["c_in"] + u["kv"]
            n_sc[dr, h] = jnp.broadcast_to(u["a_old"] * u["n_in"] + jnp.sum(u["kw"], axis=0, keepdims=True), (8, dh))
            m_sc[dr, h] = jnp.broadcast_to(u["m_new"], (8, LANES))

    def tok(cfn, col):
        return pl.BlockSpec((ln, md), lambda i: (cfn(i), col))

    def gat(cfn):
        return pl.BlockSpec((ln, LANES), lambda i: (cfn(i), 0))

    def st(cfn, shape):
        return pl.BlockSpec((None,) + shape, lambda i: (cfn(i),) + (0,) * len(shape))

    st_shapes = ((nh * dh, dh), (8, md), (nh, 8, LANES))
    return pl.pallas_call(
        body, name="mlstm_fwd", grid=(nc,),
        in_specs=[tok(chunk_f, 0), tok(chunk_f, 1), tok(chunk_f, 2), gat(chunk_f),
                  tok(chunk_b, 0), tok(chunk_b, 1), tok(chunk_b, 2), gat(chunk_b),
                  pl.BlockSpec((1, LANES), lambda i: (0, 0))],
        out_specs=[tok(chunk_f, 0), tok(chunk_b, 0)] + [st(chunk_f, s) for s in st_shapes] + [st(chunk_b, s) for s in st_shapes],
        out_shape=[SDS((s_rows, md), BF16)] * 2 + [SDS((nc,) + s, F32) for s in st_shapes] * 2,
        scratch_shapes=[pltpu.VMEM((2, nh, dh, dh), F32), pltpu.VMEM((2, nh, 8, dh), F32), pltpu.VMEM((2, nh, 8, LANES), F32)],
        compiler_params=_cp("arbitrary"))(qk, qk, z_main, zg, qk, qk, z_main, zg, bias)


def _head_rms(hs, nh, dh):
    parts, scales = [], []
    for h in range(nh):
        hh = hs[:, h * dh:(h + 1) * dh]
        r = lax.rsqrt(jnp.mean(hh * hh, axis=-1, keepdims=True) + EPS)
        parts.append(hh * r)
        scales.append(r)
    return jnp.concatenate(parts, axis=1), scales


def _layer_norm(v):
    vc = v - jnp.mean(v, axis=-1, keepdims=True)
    r = lax.rsqrt(jnp.mean(vc * vc, axis=-1, keepdims=True) + EPS)
    return vc * r, r


def _sgu_mix(vnb, ws_ref, bs_ref, tb, ng, gd, sc):
    rows = []
    for ch in range(tb // sc):
        cols = []
        for g in range(ng):
            blk = vnb[ch * sc:(ch + 1) * sc, g * gd:(g + 1) * gd]
            cols.append(_dot(_bf(ws_ref[g]), blk) + bs_ref[:, g:g + 1])
        rows.append(jnp.concatenate(cols, axis=1))
    return jnp.concatenate(rows, axis=0)


def _mixer_fwd(hf, hb, z_main, xs, hg, lng, lnb, w_s, b_st, wbm, wbs, wout, mx2, t_rows, nh):
    d = xs.shape[1]
    ng, sc = w_s.shape[0], w_s.shape[1]
    dh, gd = d // nh, d // ng
    tb = _pick(t_rows, (256,))

    def body(hf_ref, hb_ref, zo, zu, zv, zgm, zgg, x_ref, hg_ref, lng_ref, lnb_ref, ws_ref, bs_ref, wbm_ref, wbs_ref,
             wo_ref, mx2_ref, h1_ref, ym_ref, ys_ref, pm_ref, ps_ref, y_ref, out_ref):
        hs = hf_ref[...].astype(F32) + hb_ref[...].astype(F32)
        hn, _ = _head_rms(hs, nh, dh)
        ym = _bf(_sigmoid(zo[...].astype(F32)) * (hn * hg_ref[...]))
        ym_ref[...] = ym
        vhat, _ = _layer_norm(_gelu(zv[...].astype(F32)))
        vnb = _bf(vhat * lng_ref[...] + lnb_ref[...])
        ys = _bf(_gelu(zu[...].astype(F32)) * _sgu_mix(vnb, ws_ref, bs_ref, tb, ng, gd, sc))
        ys_ref[...] = ys
        pm = _dot(ym, wbm_ref[...])
        ps = _dot(ys, wbs_ref[...])
        pm_ref[...] = _bf(pm)
        ps_ref[...] = _bf(ps)
        y = _bf(_sigmoid(zgm[...].astype(F32)) * pm + _sigmoid(zgg[...].astype(F32)) * ps)
        y_ref[...] = y
        out = _dot(y, wo_ref[...])
        out_ref[...] = _bf(out)
        h1_ref[...] = x_ref[...] + mx2_ref[...] * out

    def tok(col):
        return pl.BlockSpec((tb, d), lambda i: (i, col))

    def full(shape):
        return pl.BlockSpec(shape, lambda i: (0,) * len(shape))

    return pl.pallas_call(
        body, name="mixer_fwd", grid=(t_rows // tb,),
        in_specs=[tok(0), tok(0), tok(3), tok(4), tok(5), tok(6), tok(7), tok(0), full((1, d)), full((1, d)), full((1, d)),
                  full((ng, sc, sc)), full((sc, LANES)), full((d, d)), full((d, d)), full((d, d)), full((1, d))],
        out_specs=[tok(0)] * 7,
        out_shape=[SDS((t_rows, d), F32)] + [SDS((t_rows, d), BF16)] * 6,
        compiler_params=_cp("arbitrary"))(hf, hb, z_main, z_main, z_main, z_main, z_main, xs, hg, lng, lnb, w_s, b_st,
                                          wbm, wbs, wout, mx2)


def _resident(shape):
    return pl.BlockSpec(shape, lambda *_: (0,) * len(shape), pipeline_mode=pl.Buffered(1))


def _grid_taps(a_ext, n_ext):
    col = lax.broadcasted_iota(jnp.int32, (n_ext, 1), 0) % GRID_W
    left = jnp.where(col != 0, pltpu.roll(a_ext, 1, 0), 0.0)
    right = jnp.where(col != GRID_W - 1, pltpu.roll(a_ext, n_ext - 1, 0), 0.0)
    return left, right


def _with_halo(prev, main, nxt, i, ni, tb):
    ext = jnp.concatenate([prev, main, nxt], axis=0).astype(F32)
    pos = lax.broadcasted_iota(jnp.int32, (tb + 2 * GRID_W, 1), 0)
    inside = ((pos >= GRID_W) | (i > 0)) & ((pos < tb + GRID_W) | (i < ni - 1))
    return jnp.where(inside, ext, 0.0)


def _halo_specs(tb, cb, t_rows, col0=0):
    nh64 = tb // GRID_W
    return [pl.BlockSpec((tb, cb), lambda i, j: (i, col0 + j)),
            pl.BlockSpec((GRID_W, cb), lambda i, j: (jnp.maximum(i * nh64 - 1, 0), col0 + j)),
            pl.BlockSpec((GRID_W, cb), lambda i, j: (jnp.minimum((i + 1) * nh64, t_rows // GRID_W - 1), col0 + j))]


def _ffn_tail(ab, w_conv9, w_down, h1, mx5, gfin, target, dff):
    t_rows, d = h1.shape
    tb = _pick(t_rows, (256,))
    cb = _pick(dff, (1408, 256, 128))
    ni, nj = t_rows // tb, dff // cb
    n_ext = tb + 2 * GRID_W

    def body(am, ap, an, b_ref, wc_ref, wd_ref, h1_ref, mx5_ref, gf_ref, tg_ref, ac_ref, f_ref, dh2_ref, dffn_ref, st_ref, acc):
        i, j = pl.program_id(0), pl.program_id(1)
        a_ext = _with_halo(ap[...], am[...], an[...], i, ni, tb)
        left, right = _grid_taps(a_ext, n_ext)
        conv = jnp.zeros((tb, cb), F32)
        for di in range(3):
            o = di * GRID_W
            conv = conv + (wc_ref[3 * di:3 * di + 1, :] * left[o:o + tb] + wc_ref[3 * di + 1:3 * di + 2, :] * a_ext[o:o + tb]
                           + wc_ref[3 * di + 2:3 * di + 3, :] * right[o:o + tb])
        ac_ref[...] = _bf(conv)
        fb = _bf(conv * _sigmoid(conv) * b_ref[...].astype(F32))
        f_ref[...] = fb

        @pl.when(j == 0)
        def _():
            acc[...] = jnp.zeros_like(acc)

        @pl.when((i == 0) & (j == 0))
        def _():
            st_ref[...] = jnp.zeros_like(st_ref)

        acc[...] += _dot(fb, wd_ref[pl.ds(pl.multiple_of(j * cb, cb), cb), :])

        @pl.when(j == nj - 1)
        def _():
            ffn = acc[...]
            h2 = h1_ref[...] + mx5_ref[...] * ffn
            r = lax.rsqrt(jnp.mean(h2 * h2, axis=-1, keepdims=True) + EPS)
            xn = h2 * r
            e = xn * gf_ref[...] - tg_ref[...]
            loss = 0.5 * jnp.sum(jnp.sum(e * e, axis=1, keepdims=True), axis=0, keepdims=True) / d
            dy = e * (1.0 / d)
            dxn = dy * gf_ref[...]
            dh2 = r * (dxn - xn * jnp.mean(dxn * xn, axis=-1, keepdims=True))
            dh2_ref[...] = dh2
            dffn_ref[...] = _bf(dh2 * mx5_ref[...])
            st_ref[...] += jnp.concatenate(
                [jnp.sum(dy * xn, axis=0, keepdims=True), jnp.sum(dh2 * ffn, axis=0, keepdims=True),
                 jnp.broadcast_to(loss, (1, d)), jnp.zeros((5, d), F32)], axis=0)

    def tokd():
        return pl.BlockSpec((tb, d), lambda i, j: (i, 0))

    def rowd():
        return pl.BlockSpec((1, d), lambda i, j: (0, 0))

    return pl.pallas_call(
        body, name="ffn_tail", grid=(ni, nj),
        in_specs=_halo_specs(tb, cb, t_rows) + [pl.BlockSpec((tb, cb), lambda i, j: (i, nj + j)),
                                                pl.BlockSpec((16, cb), lambda i, j: (0, j)),
                                                _resident((dff, d)), tokd(), rowd(), rowd(), tokd()],
        out_specs=[pl.BlockSpec((tb, cb), lambda i, j: (i, j)), pl.BlockSpec((tb, cb), lambda i, j: (i, j)), tokd(), tokd(),
                   pl.BlockSpec((8, d), lambda i, j: (0, 0))],
        out_shape=[SDS((t_rows, dff), BF16), SDS((t_rows, dff), BF16), SDS((t_rows, d), F32), SDS((t_rows, d), BF16),
                   SDS((8, d), F32)],
        scratch_shapes=[pltpu.VMEM((tb, d), F32)],
        compiler_params=_cp("arbitrary", "arbitrary"))(ab, ab, ab, ab, w_conv9, w_down, h1, mx5, gfin, target)


def _ffn_bwd_gate(dffn, w_down, aconv, ab, dff):
    t_rows, d = dffn.shape
    tb = _pick(t_rows, (512,))
    cb = _pick(dff, (1408, 256, 128))
    nj = dff // cb

    def body(g_ref, wd_ref, ac_ref, b_ref, db_ref, dac_ref):
        df = _dot_nt(g_ref[...], wd_ref[pl.ds(pl.multiple_of(pl.program_id(1) * cb, cb), cb), :])
        ac = ac_ref[...].astype(F32)
        sa = _sigmoid(ac)
        db_ref[...] = _bf(df * ac * sa)
        dac_ref[...] = _bf(df * b_ref[...].astype(F32) * (sa * (1.0 + ac * (1.0 - sa))))

    blk = pl.BlockSpec((tb, cb), lambda i, j: (i, j))
    return pl.pallas_call(
        body, name="ffn_bwd_gate", grid=(t_rows // tb, nj),
        in_specs=[pl.BlockSpec((tb, d), lambda i, j: (i, 0)), _resident((dff, d)), blk,
                  pl.BlockSpec((tb, cb), lambda i, j: (i, nj + j))],
        out_specs=[blk, blk], out_shape=[SDS((t_rows, dff), BF16)] * 2,
        compiler_params=_cp("arbitrary", "arbitrary"))(dffn, w_down, aconv, ab)


def _ffn_conv_bwd(dac, ab, w_conv9, dff):
    t_rows = dac.shape[0]
    tb = _pick(t_rows, (512, 256))
    cb = _pick(dff, (1408, 256, 128))
    ni, nj = t_rows // tb, dff // cb
    n_ext = tb + 2 * GRID_W
    nh64 = tb // GRID_W

    def body(dm, dp, dn, am, ap, an, wc_ref, da_ref, gw_ref):
        i = pl.program_id(1)
        d_ext = _with_halo(dp[...], dm[...], dn[...], i, ni, tb)
        a_ext = _with_halo(ap[...], am[...], an[...], i, ni, tb)
        d_left, d_right = _grid_taps(d_ext, n_ext)
        a_left, a_right = _grid_taps(a_ext, n_ext)
        dmain = d_ext[GRID_W:GRID_W + tb]
        da = jnp.zeros((tb, cb), F32)
        rows = []
        for di in range(3):
            o = (2 - di) * GRID_W
            da = da + (wc_ref[3 * di:3 * di + 1, :] * d_right[o:o + tb] + wc_ref[3 * di + 1:3 * di + 2, :] * d_ext[o:o + tb]
                       + wc_ref[3 * di + 2:3 * di + 3, :] * d_left[o:o + tb])
            o = di * GRID_W
            for tap in (a_left, a_ext, a_right):
                rows.append(jnp.sum(dmain * tap[o:o + tb], axis=0, keepdims=True))
        da_ref[...] = _bf(da)

        @pl.when(i == 0)
        def _():
            gw_ref[...] = jnp.zeros_like(gw_ref)

        gw_ref[...] += jnp.concatenate(rows + [jnp.zeros((7, cb), F32)], axis=0)

    def halo(col0):
        return [pl.BlockSpec((tb, cb), lambda j, i: (i, col0 + j)),
                pl.BlockSpec((GRID_W, cb), lambda j, i: (jnp.maximum(i * nh64 - 1, 0), col0 + j)),
                pl.BlockSpec((GRID_W, cb), lambda j, i: (jnp.minimum((i + 1) * nh64, t_rows // GRID_W - 1), col0 + j))]

    return pl.pallas_call(
        body, name="ffn_conv_bwd", grid=(nj, ni),
        in_specs=halo(0) + halo(0) + [pl.BlockSpec((16, cb), lambda j, i: (0, j))],
        out_specs=[pl.BlockSpec((tb, cb), lambda j, i: (i, j)), pl.BlockSpec((16, cb), lambda j, i: (0, j))],
        out_shape=[SDS((t_rows, dff), BF16), SDS((16, dff), F32)],
        compiler_params=_cp("arbitrary", "arbitrary"))(dac, dac, dac, ab, ab, ab, w_conv9)


def _proj_norm_bwd(pairs, x_arr, x_row0, g, scale, resid, m_rows, name, row_blocks=(1024, 256)):
    d = x_arr.shape[1]
    tm = _pick(m_rows, row_blocks)
    te = 256
    ni = m_rows // tm
    starts, total = [], 0
    for (_, _, _, _, k_p, tk_p) in pairs:
        starts.append(total)
        total += k_p // tk_p
    npairs = len(pairs)
    has_dx = resid is not None

    def body(*refs):
        a_refs, b_refs = refs[0:2 * npairs:2], refs[1:2 * npairs:2]
        rest = refs[2 * npairs:]
        if has_dx:
            x_ref, g_ref, sc_ref, r_ref, dx_ref, st_ref, acc = rest
        else:
            x_ref, g_ref, sc_ref, st_ref, acc = rest
        i, k = pl.program_id(0), pl.program_id(1)

        @pl.when(k == 0)
        def _():
            acc[...] = jnp.zeros_like(acc)

        @pl.when((i == 0) & (k == 0))
        def _():
            st_ref[...] = jnp.zeros_like(st_ref)

        for p in range(npairs):
            nk = pairs[p][4] // pairs[p][5]

            @pl.when((k >= starts[p]) & (k < starts[p] + nk))
            def _(p=p):
                acc[...] += _dot_nt(a_refs[p][...], b_refs[p][...])

        @pl.when(k == total - 1)
        def _():
            sums = [jnp.zeros((1, d), F32)] * 3
            for r0 in range(0, tm, te):
                rows = slice(r0, r0 + te)
                dhn = acc[rows, :]
                x = x_ref[rows, :]
                r = lax.rsqrt(jnp.mean(x * x, axis=-1, keepdims=True) + EPS)
                xn = x * r
                dmod = dhn * (1.0 + sc_ref[...])
                dxn = dmod * g_ref[...]
                if has_dx:
                    dx_ref[rows, :] = r * (dxn - xn * jnp.mean(dxn * xn, axis=-1, keepdims=True)) + r_ref[rows, :]
                sums = [sums[0] + jnp.sum(dmod * xn, axis=0, keepdims=True), sums[1] + jnp.sum(dhn, axis=0, keepdims=True),
                        sums[2] + jnp.sum(dhn * (xn * g_ref[...]), axis=0, keepdims=True)]
            st_ref[...] += jnp.concatenate(sums + [jnp.zeros((5, d), F32)], axis=0)

    in_specs, args = [], []
    for p, (a, a_row0, b, b_col0, k_p, tk_p) in enumerate(pairs):
        nk, s0, ar, bc = k_p // tk_p, starts[p], a_row0 // tm, b_col0 // tk_p

        def kk(k, s0=s0, nk=nk):
            return jnp.clip(k - s0, 0, nk - 1)

        in_specs.append(pl.BlockSpec((tm, tk_p), lambda i, k, ar=ar, kk=kk: (ar + i, kk(k))))
        in_specs.append(pl.BlockSpec((d, tk_p), lambda i, k, bc=bc, kk=kk: (0, bc + kk(k)),
                                     pipeline_mode=pl.Buffered(1 if nk == 1 else 2)))
        args += [a, b]
    xr = x_row0 // tm
    in_specs += [pl.BlockSpec((tm, d), lambda i, k: (xr + i, 0)), pl.BlockSpec((1, d), lambda i, k: (0, 0)),
                 pl.BlockSpec((1, d), lambda i, k: (0, 0))]
    args += [x_arr, g, scale]
    out_specs, out_shape = [], []
    if has_dx:
        in_specs.append(pl.BlockSpec((tm, d), lambda i, k: (i, 0)))
        args.append(resid)
        out_specs.append(pl.BlockSpec((tm, d), lambda i, k: (i, 0)))
        out_shape.append(SDS((m_rows, d), F32))
    out_specs.append(pl.BlockSpec((8, d), lambda i, k: (0, 0)))
    out_shape.append(SDS((8, d), F32))
    return pl.pallas_call(
        body, name=name, grid=(ni, total), in_specs=in_specs, out_specs=out_specs, out_shape=out_shape,
        scratch_shapes=[pltpu.VMEM((tm, d), F32)], compiler_params=_cp("arbitrary", "arbitrary"))(*args)


def _wgrad(a, b, k_rows, name):
    m, n = a.shape[1], b.shape[1]
    tm = _pick(m, (1408, 1024, 512, 384, 256, 128))
    tn = _pick(n, (3072, 2816, 2560, 1408, 1024, 768, 512, 384, 256, 128))
    tk = _pick(k_rows, (1280, 1024, 256))
    nk = k_rows // tk

    def body(a_ref, b_ref, o_ref, acc):
        k = pl.program_id(2)

        @pl.when(k == 0)
        def _():
            acc[...] = jnp.zeros_like(acc)

        acc[...] += _dot_tn(a_ref[...], b_ref[...])

        @pl.when(k == nk - 1)
        def _():
            o_ref[...] = _bf(acc[...])

    return pl.pallas_call(
        body, name=name, grid=(m // tm, n // tn, nk),
        in_specs=[pl.BlockSpec((tk, tm), lambda i, j, k: (k, i)), pl.BlockSpec((tk, tn), lambda i, j, k: (k, j))],
        out_specs=pl.BlockSpec((tm, tn), lambda i, j, k: (i, j)), out_shape=SDS((m, n), BF16),
        scratch_shapes=[pltpu.VMEM((tm, tn), F32)],
        compiler_params=_cp("arbitrary", "arbitrary", "arbitrary"))(a, b)


def _lane_put(col, lane_idx):
    lane = lax.broadcasted_iota(jnp.int32, (1, LANES), 1)
    return jnp.where(lane == lane_idx, col, 0.0)


def _mixer_bwd(dh1, out, hf, hb, z_main, pm, ps, hg, lng, lnb, w_s, b_st, wbm, wbs, wout, mx2, y, t_rows, nh):
    d = dh1.shape[1]
    ng, sc = w_s.shape[0], w_s.shape[1]
    dh, gd = d // nh, d // ng
    tb = _pick(t_rows, (256,))

    def body(dh1_ref, out_ref, hf_ref, hb_ref, zo, zu, zv, zgm, zgg, pm_ref, ps_ref, hg_ref, lng_ref, lnb_ref, ws_ref, bs_ref,
             wbm_ref, wbs_ref, wo_ref, mx2_ref, y_ref, dz_ref, dhs_ref, gwo_ref, dpm_ref, dps_ref, st_ref, dws_ref, dbs_ref,
             gwo_acc):
        i = pl.program_id(0)

        @pl.when(i == 0)
        def _():
            st_ref[...] = jnp.zeros_like(st_ref)
            dws_ref[...] = jnp.zeros_like(dws_ref)
            dbs_ref[...] = jnp.zeros_like(dbs_ref)
            gwo_acc[...] = jnp.zeros_like(gwo_acc)

        dh1v = dh1_ref[...]
        doutb = _bf(dh1v * mx2_ref[...])
        gwo_acc[...] += _dot_tn(y_ref[...], doutb)
        d_mx2 = jnp.sum(dh1v * out_ref[...].astype(F32), axis=0, keepdims=True)
        dy = _dot_nt(doutb, wo_ref[...])
        sgm, sgg = _sigmoid(zgm[...].astype(F32)), _sigmoid(zgg[...].astype(F32))
        dpmb, dpsb = _bf(dy * sgm), _bf(dy * sgg)
        dpm_ref[...] = dpmb
        dps_ref[...] = dpsb
        dz_ref[:, 3 * d:4 * d] = _bf(dy * pm_ref[...].astype(F32) * sgm * (1.0 - sgm))
        dz_ref[:, 4 * d:5 * d] = _bf(dy * ps_ref[...].astype(F32) * sgg * (1.0 - sgg))
        dym = _dot_nt(dpmb, wbm_ref[...])
        dys = _dot_nt(dpsb, wbs_ref[...])
        hs = hf_ref[...].astype(F32) + hb_ref[...].astype(F32)
        hn, scales = _head_rms(hs, nh, dh)
        so = _sigmoid(zo[...].astype(F32))
        dz_ref[:, 0:d] = _bf(dym * (hn * hg_ref[...]) * so * (1.0 - so))
        dhmn = dym * so
        d_hg = jnp.sum(dhmn * hn, axis=0, keepdims=True)
        dhn = dhmn * hg_ref[...]
        for h in range(nh):
            sl = slice(h * dh, (h + 1) * dh)
            dhs_ref[:, sl] = _bf(scales[h] * (dhn[:, sl] - hn[:, sl] * jnp.mean(dhn[:, sl] * hn[:, sl], axis=-1, keepdims=True)))
        zuv, zvv = zu[...].astype(F32), zv[...].astype(F32)
        u, du_dz = _gelu_and_grad(zuv)
        vg, dvg_dz = _gelu_and_grad(zvv)
        vhat, rstd = _layer_norm(vg)
        vnb = _bf(vhat * lng_ref[...] + lnb_ref[...])
        mixed = _sgu_mix(vnb, ws_ref, bs_ref, tb, ng, gd, sc)
        dz_ref[:, d:2 * d] = _bf(dys * mixed * du_dz)
        dmix = dys * u
        rows = []
        dbs = jnp.zeros((sc, LANES), F32)
        for ch in range(tb // sc):
            cols = []
            for g in range(ng):
                dm = dmix[ch * sc:(ch + 1) * sc, g * gd:(g + 1) * gd]
                dmb = _bf(dm)
                dws_ref[g] += _dot_nt(dmb, vnb[ch * sc:(ch + 1) * sc, g * gd:(g + 1) * gd])
                dbs = dbs + _lane_put(jnp.sum(dm, axis=1, keepdims=True), g)
                cols.append(_dot_tn(_bf(ws_ref[g]), dmb))
            rows.append(jnp.concatenate(cols, axis=1))
        dbs_ref[...] += dbs
        dvn = jnp.concatenate(rows, axis=0)
        d_lng = jnp.sum(dvn * vhat, axis=0, keepdims=True)
        d_lnb = jnp.sum(dvn, axis=0, keepdims=True)
        dvh = dvn * lng_ref[...]
        dvg = rstd * (dvh - jnp.mean(dvh, axis=-1, keepdims=True) - vhat * jnp.mean(dvh * vhat, axis=-1, keepdims=True))
        dz_ref[:, 2 * d:3 * d] = _bf(dvg * dvg_dz)
        st_ref[...] += jnp.concatenate([d_mx2, d_hg, d_lng, d_lnb, jnp.zeros((4, d), F32)], axis=0)

        @pl.when(i == t_rows // tb - 1)
        def _():
            gwo_ref[...] = _bf(gwo_acc[...])

    def tok(col):
        return pl.BlockSpec((tb, d), lambda i: (i, col))

    def full(shape):
        return pl.BlockSpec(shape, lambda i: (0,) * len(shape))

    return pl.pallas_call(
        body, name="mixer_bwd", grid=(t_rows // tb,),
        in_specs=[tok(0), tok(0), tok(0), tok(0), tok(3), tok(4), tok(5), tok(6), tok(7), tok(0), tok(0), full((1, d)),
                  full((1, d)), full((1, d)), full((ng, sc, sc)), full((sc, LANES)), _resident((d, d)), _resident((d, d)),
                  _resident((d, d)), full((1, d)), tok(0)],
        out_specs=[pl.BlockSpec((tb, 5 * d), lambda i: (i, 0)), tok(0), full((d, d)), tok(0), tok(0), full((8, d)),
                   full((ng, sc, sc)), full((sc, LANES))],
        out_shape=[SDS((t_rows, 5 * d), BF16), SDS((t_rows, d), BF16), SDS((d, d), BF16), SDS((t_rows, d), BF16),
                   SDS((t_rows, d), BF16), SDS((8, d), F32), SDS((ng, sc, sc), F32), SDS((sc, LANES), F32)],
        scratch_shapes=[pltpu.VMEM((d, d), F32)],
        compiler_params=_cp("arbitrary"))(dh1, out, hf, hb, z_main, z_main, z_main, z_main, z_main, pm, ps, hg, lng, lnb, w_s,
                                          b_st, wbm, wbs, wout, mx2, y)


def _mlstm_bwd(qk, z_main, zg, bias, dhs, hf, hb, states_f, states_b, nh, t_rows):
    s_rows = qk.shape[0]
    md = qk.shape[1] // 2
    dh = md // nh
    nc = s_rows // LCH
    nx = t_rows // LCH
    ln = LCH

    def chunk_f(i):
        return jnp.where(i == nc - 1, nc - 1, nc - 2 - i)

    def chunk_b(i):
        return jnp.where(i == nc - 1, nc - 1, i)

    def body(qf, kf, vf, gf, dhf, hsf, cf, nf, mf_, qb, kb, vb, gb, dhb, hsb, cb, nb, mb_, bias_ref, dqkvf_ref, dgf_ref, dqkvb_ref,
             dgb_ref, dc_sc, dn_sc):
        i = pl.program_id(0)
        is_ctx = i == nc - 1

        @pl.when(i == 0)
        def _():
            dc_sc[...] = jnp.zeros_like(dc_sc)
            dn_sc[...] = jnp.zeros_like(dn_sc)

        sides = ((qf, kf, vf, gf, dhf, hsf, cf, nf, mf_, dqkvf_ref, dgf_ref), (qb, kb, vb, gb, dhb, hsb, cb, nb, mb_, dqkvb_ref, dgb_ref))
        gates = [_chunk_gates(s[3][...], bias_ref[...], dr == 1) for dr, s in enumerate(sides)]
        units = []
        for dr, (q_ref, k_ref, v_ref, _, dh_ref, hs_ref, c_ref, n_ref, m_ref, dqkv_ref, _) in enumerate(sides):
            gz, b_all, b_t, g_t, g_all, mask, _ = gates[dr]
            for h in range(nh):
                ci, cfl = 2 * dr * nh + h, (2 * dr + 1) * nh + h
                sl = slice(h * dh, (h + 1) * dh)
                u = dict(dr=dr, h=h, sl=sl, ci=ci, cfl=cfl, dqkv_ref=dqkv_ref, q=q_ref[:, sl], k=k_ref[:, sl], v=v_ref[:, sl],
                         dhv=jnp.where(is_ctx, 0.0, dh_ref[:, sl].astype(F32)), hs=hs_ref[:, sl].astype(F32),
                         c_in=c_ref[sl, :], n_in=n_ref[0:1, sl], m_in=m_ref[h, 0:1, 0:1],
                         b_col=b_all[:, cfl:cfl + 1], i_col=gz[:, ci:ci + 1], g=g_all[:, cfl:cfl + 1],
                         dc_new=dc_sc[dr, h], dn_new=dn_sc[dr, h, 0:1, :])
                u["qf32"], u["kf32"] = u["q"].astype(F32), u["k"].astype(F32)
                u["w"], u["w_int"], u["m_row"] = _head_weights(u["b_col"], b_t[cfl:cfl + 1, :], g_t[ci:ci + 1, :], u["m_in"], mask)
                u["qk"] = _dot_nt(u["q"], u["k"])
                units.append(u)
        for u in units:
            s_mat = u["qk"] * u["w"]
            u["s_mat"], u["sb"], u["cb16"], u["dcb"] = s_mat, _bf(s_mat), _bf(u["c_in"]), _bf(u["dc_new"])
            den = jnp.sum(s_mat, axis=1, keepdims=True) + u["w_int"] * jnp.sum(u["qf32"] * u["n_in"], axis=1, keepdims=True)
            e_m = jnp.exp(-u["m_row"])
            dnm = jnp.maximum(jnp.abs(den), e_m)
            hdh = jnp.sum(u["hs"] * u["dhv"], axis=1, keepdims=True)
            u["dden"] = jnp.where(jnp.abs(den) > e_m, -(hdh / dnm) * jnp.sign(den), 0.0)
            u["dnum_b"] = _bf(u["dhv"] / dnm)
            u["a_old"], u["coef"], _ = _head_state_coeffs(u["g"], u["b_col"], u["i_col"], u["m_in"])
            u["dsm"] = _dot_nt(u["dnum_b"], u["v"])
            u["qct"] = _dot_nt(u["dnum_b"], u["cb16"])
            u["vdc"] = _dot_nt(u["v"], u["dcb"])
        for u in units:
            ds = u["dsm"] + u["dden"]
            u["pb"] = _bf(u["w"] * ds)
            u["gmat"] = u["s_mat"] * ds
            u["dv1"] = _dot_tn(u["sb"], u["dnum_b"])
            u["dv2"] = _dot(_bf(u["kf32"] * u["coef"]), u["dcb"])
            u["dcu"] = _dot_tn(_bf(u["qf32"] * u["w_int"]), u["dnum_b"])
        for u in units:
            u["dq1"] = _dot(u["pb"], u["k"])
            u["dk1"] = _dot_tn(u["pb"], u["q"])
        acc = [dict(x1=jnp.zeros((ln, LANES), F32), x2=jnp.zeros((ln, LANES), F32), dig=jnp.zeros((ln, LANES), F32),
                    e_row=jnp.zeros((1, LANES), F32)) for _ in range(2)]
        for u in units:
            dr, h, sl, a = u["dr"], u["h"], u["sl"], acc[u["dr"]]
            dq_inter = u["w_int"] * (u["qct"] + u["dden"] * u["n_in"])
            dk_state = u["coef"] * (u["vdc"] + u["dn_new"])
            u["dqkv_ref"][:, sl] = _bf(u["dq1"] + dq_inter)
            u["dqkv_ref"][:, md + h * dh:md + (h + 1) * dh] = _bf(u["dk1"] + dk_state)
            u["dqkv_ref"][:, 2 * md + h * dh:2 * md + (h + 1) * dh] = _bf(u["dv1"] + u["dv2"])
            row_intra = jnp.sum(u["gmat"], axis=1, keepdims=True)
            col_intra = jnp.sum(u["gmat"].T, axis=1, keepdims=True)
            row_inter = jnp.sum(u["qf32"] * dq_inter, axis=1, keepdims=True)
            col_inter = jnp.sum(u["kf32"] * dk_state, axis=1, keepdims=True)
            e_old = u["a_old"] * (jnp.sum(jnp.sum(u["dc_new"] * u["c_in"], axis=1, keepdims=True), axis=0, keepdims=True)
                                  + jnp.sum(u["dn_new"] * u["n_in"], axis=1, keepdims=True))
            a["x1"] = a["x1"] + _lane_put(row_intra - col_intra + row_inter, u["cfl"])
            a["x2"] = a["x2"] + _lane_put(col_inter, u["cfl"])
            a["e_row"] = a["e_row"] + _lane_put(e_old, u["cfl"])
            a["dig"] = a["dig"] + _lane_put(col_intra + col_inter, u["ci"])
            dc_sc[dr, h] = u["a_old"] * u["dc_new"] + u["dcu"]
            dn_sc[dr, h] = jnp.broadcast_to(
                u["a_old"] * u["dn_new"] + jnp.sum(u["qf32"] * (u["w_int"] * u["dden"]), axis=0, keepdims=True), (8, dh))
        for dr, s in enumerate(sides):
            gz, mfl, a = gates[dr][0], gates[dr][6], acc[dr]
            dlogf = _mask_dot_t(mfl, a["x1"]) + _mask_dot(mfl, a["x2"]) - a["x2"] + a["e_row"]
            s[10][...] = a["dig"] + dlogf / (1.0 + jnp.exp(gz))

    def tok(cfn, col):
        return pl.BlockSpec((ln, md), lambda i: (cfn(i), col))

    def dht(cfn):
        return pl.BlockSpec((ln, md), lambda i: (jnp.minimum(cfn(i), nx - 1), 0))

    def gat(cfn):
        return pl.BlockSpec((ln, LANES), lambda i: (cfn(i), 0))

    def st(cfn, shape):
        return pl.BlockSpec((None,) + shape, lambda i: (cfn(i),) + (0,) * len(shape))

    st_shapes = ((nh * dh, dh), (8, md), (nh, 8, LANES))

    def side(cfn):
        return [tok(cfn, 0), tok(cfn, 1), tok(cfn, 2), gat(cfn), dht(cfn), tok(cfn, 0)] + [st(cfn, s) for s in st_shapes]

    def outs(cfn):
        return [pl.BlockSpec((ln, 3 * md), lambda i: (cfn(i), 0)), gat(cfn)]

    return pl.pallas_call(
        body, name="mlstm_bwd", grid=(nc,),
        in_specs=side(chunk_f) + side(chunk_b) + [pl.BlockSpec((1, LANES), lambda i: (0, 0))],
        out_specs=outs(chunk_f) + outs(chunk_b),
        out_shape=[SDS((s_rows, 3 * md), BF16), SDS((s_rows, LANES), F32)] * 2,
        scratch_shapes=[pltpu.VMEM((2, nh, dh, dh), F32), pltpu.VMEM((2, nh, 8, dh), F32)],
        compiler_params=_cp("arbitrary"))(qk, qk, z_main, zg, dhs, hf, *states_f, qk, qk, z_main, zg, dhs, hb, *states_b, bias)


def _qkv_conv_bwd(dqkv_f, dqkv_b, z_main, conv_w, t_rows, md, qscale):
    s_rows = z_main.shape[0]
    tb = _pick(s_rows, (1280, 1024, 256))
    cb = _pick(md, (512, 256, 128))
    ni, nj, ncq = s_rows // tb, 3 * md // cb, 2 * md // cb
    nb8 = tb // 8
    n_ext = tb + 16

    def body(fm, fp, fn, bm, bp, bn, zm, zp, zn, w_ref, dz_ref, gw_ref):
        j, i = pl.program_id(0), pl.program_id(1)

        @pl.when(j < ncq)
        def _():
            z = jnp.concatenate([zp[...], zm[...], zn[...]], axis=0).astype(F32)
            dqk = (jnp.concatenate([fp[...], fm[...], fn[...]], axis=0).astype(F32)
                   + jnp.concatenate([bp[...], bm[...], bn[...]], axis=0).astype(F32)) * jnp.where(j * cb < md, qscale, 1.0)
            row = i * tb - 8 + lax.broadcasted_iota(jnp.int32, (n_ext, 1), 0)
            prev_ok, next_ok = _seg_masks(row, t_rows, s_rows)
            zprev = jnp.where(prev_ok, pltpu.roll(z, 1, 0), 0.0)
            znext = jnp.where(next_ok, pltpu.roll(z, n_ext - 1, 0), 0.0)
            pre = w_ref[0:1, :] * zprev + w_ref[1:2, :] * z + w_ref[2:3, :] * znext
            sg = _sigmoid(pre)
            dpre = dqk * (sg * (1.0 + pre * (1.0 - sg)))
            dz = (w_ref[1:2, :] * dpre + w_ref[0:1, :] * jnp.where(next_ok, pltpu.roll(dpre, n_ext - 1, 0), 0.0)
                  + w_ref[2:3, :] * jnp.where(prev_ok, pltpu.roll(dpre, 1, 0), 0.0))
            dz_ref[...] = _bf(dz[8:8 + tb])
            dm = dpre[8:8 + tb]

            @pl.when(i == 0)
            def _():
                gw_ref[...] = jnp.zeros_like(gw_ref)

            gw_ref[...] += jnp.concatenate(
                [jnp.sum(dm * zprev[8:8 + tb], axis=0, keepdims=True), jnp.sum(dm * z[8:8 + tb], axis=0, keepdims=True),
                 jnp.sum(dm * znext[8:8 + tb], axis=0, keepdims=True), jnp.zeros((5, cb), F32)], axis=0)

        @pl.when(j >= ncq)
        def _():
            dz_ref[...] = _bf(fm[...].astype(F32) + bm[...].astype(F32))

    def halo(clampj):
        def cj(j):
            return jnp.minimum(j, ncq - 1) if clampj else j
        return [pl.BlockSpec((tb, cb), lambda j, i: (i, cj(j))),
                pl.BlockSpec((8, cb), lambda j, i: (jnp.maximum(i * nb8 - 1, 0), cj(j))),
                pl.BlockSpec((8, cb), lambda j, i: (jnp.minimum((i + 1) * nb8, s_rows // 8 - 1), cj(j)))]

    return pl.pallas_call(
        body, name="qkv_conv_bwd", grid=(nj, ni),
        in_specs=halo(False) + halo(False) + halo(True) + [pl.BlockSpec((8, cb), lambda j, i: (0, jnp.minimum(j, ncq - 1)))],
        out_specs=[pl.BlockSpec((tb, cb), lambda j, i: (i, j)), pl.BlockSpec((8, cb), lambda j, i: (0, jnp.minimum(j, ncq - 1)))],
        out_shape=[SDS((s_rows, 3 * md), BF16), SDS((8, 2 * md), F32)],
        compiler_params=_cp("arbitrary", "arbitrary"))(dqkv_f, dqkv_f, dqkv_f, dqkv_b, dqkv_b, dqkv_b, z_main, z_main, z_main, conv_w)


def _gate_grad_sum(dg_f, dg_b):
    s_rows = dg_f.shape[0]
    tb = _pick(s_rows, (1280, 1024, 256))

    def body(a_ref, b_ref, o_ref, st_ref):
        @pl.when(pl.program_id(0) == 0)
        def _():
            st_ref[...] = jnp.zeros_like(st_ref)

        s = a_ref[...] + b_ref[...]
        o_ref[...] = _bf(s)
        st_ref[...] += jnp.concatenate([jnp.sum(s, axis=0, keepdims=True), jnp.zeros((7, LANES), F32)], axis=0)

    blk = pl.BlockSpec((tb, LANES), lambda i: (i, 0))
    return pl.pallas_call(
        body, name="gate_grad_sum", grid=(s_rows // tb,), in_specs=[blk, blk],
        out_specs=[blk, pl.BlockSpec((8, LANES), lambda i: (0, 0))],
        out_shape=[SDS((s_rows, LANES), BF16), SDS((8, LANES), F32)], compiler_params=_cp("arbitrary"))(dg_f, dg_b)


def _mod_grads(silu_slots, dmx_sh, dmx_slots, dmc_tot, dmc_sh, silu_cctx, c_ctx, w_mod_c):
    d = silu_slots.shape[1]
    ncol, n6 = dmx_sh.shape[1], dmx_slots.shape[1]

    def body(ss_ref, dsh_ref, dsl_ref, dct_ref, dcs_ref, sc_ref, c_ref, w_ref, gw_ref, gb_ref, gc_ref):
        a = jnp.concatenate([ss_ref[...], sc_ref[...], jnp.zeros((7, d), F32)], axis=0)
        b = jnp.concatenate([dsh_ref[...], dcs_ref[...], jnp.zeros((7, ncol), F32)], axis=0)
        gw_ref[0] = lax.dot_general(a, b, (((0,), (0,)), ((), ())), preferred_element_type=F32, precision=HI)
        dct = dct_ref[...]
        gb_ref[...] = jnp.sum(dsl_ref[...], axis=0, keepdims=True) + jnp.concatenate(
            [dct, jnp.zeros((1, n6 - dct.shape[1]), F32)], axis=1)
        t = _dot_nt(_bf(jnp.broadcast_to(dct, (8, dct.shape[1]))), w_ref[...])
        cv = c_ref[...]
        s = _sigmoid(cv)
        gc_ref[...] = t[0:1, :] * (s * (1.0 + cv * (1.0 - s)))

    return pl.pallas_call(body, name="mod_grads", out_shape=[SDS((1, d, ncol), F32), SDS((1, n6), F32), SDS((1, d), F32)],
                          compiler_params=_cp())(silu_slots, dmx_sh, dmx_slots, dmc_tot, dmc_sh, silu_cctx, c_ctx, w_mod_c)


def _slot_sum(slots):
    ns, r = slots.shape[0], slots.shape[1]
    tb = _pick(r, (1024, 512, 256, 128, 64, 32, 16, 8))

    def body(s_ref, o_ref):
        acc = s_ref[0]
        for k in range(1, ns):
            acc = acc + s_ref[k]
        o_ref[...] = acc

    return pl.pallas_call(
        body, name="slot_sum", grid=(r // tb,), in_specs=[pl.BlockSpec((ns, tb, LANES), lambda i: (0, i, 0))],
        out_specs=pl.BlockSpec((tb, LANES), lambda i: (i, 0)), out_shape=SDS((r, LANES), F32),
        compiler_params=_cp("arbitrary"))(slots)


def _adamw(w, gslots, m, v, name, after=None):
    lead = ((None,), (0,)) if w.ndim == 3 else ((), ())
    r, cdim = w.shape[-2:]
    ns, rg = gslots.shape[0], gslots.shape[1]
    tb = r if (rg != r or r % 8) else _pick(r, (128, 64, 32, 16, 8))
    bc1, bc2 = 1.0 - ADAM_B1 ** ADAM_STEP, 1.0 - ADAM_B2 ** ADAM_STEP

    def body(w_ref, g_ref, m_ref, v_ref, *rest):
        go_ref, d_ref, mo_ref, vo_ref = rest[-4:]
        g = g_ref[0, 0:tb, :].astype(F32)
        for k in range(1, ns):
            g = g + g_ref[k, 0:tb, :].astype(F32)
        mn = ADAM_B1 * m_ref[...] + (1.0 - ADAM_B1) * g
        vn = ADAM_B2 * v_ref[...] + (1.0 - ADAM_B2) * (g * g)
        go_ref[...] = g
        mo_ref[...] = mn
        vo_ref[...] = vn
        d_ref[...] = -ADAM_LR * ((mn / bc1) / (jnp.sqrt(vn / bc2) + ADAM_EPS) + ADAM_WD * w_ref[...])

    blk = pl.BlockSpec(lead[0] + (tb, cdim), lambda i: lead[1] + (i, 0))
    gblk = pl.BlockSpec((ns, tb if rg == r else rg, cdim), lambda i: (0, i, 0))
    extra_specs = [] if after is None else [pl.BlockSpec((8, LANES), lambda i: (0, 0))]
    extra_args = [] if after is None else [after]
    return pl.pallas_call(
        body, name=name, grid=(r // tb,), in_specs=[blk, gblk, blk, blk] + extra_specs,
        out_specs=[blk] * 4, out_shape=[SDS(w.shape, F32)] * 4, compiler_params=_cp("arbitrary"))(w, gslots, m, v, *extra_args)


def _pack(parts, row_mult):
    flat = jnp.concatenate([p.reshape(-1) for p in parts])
    n = flat.shape[0]
    rows = -(-n // LANES)
    rows = -(-rows // row_mult) * row_mult
    return jnp.pad(flat, (0, rows * LANES - n)).reshape(rows, LANES)


def _unpack(buf, shapes):
    flat = buf.reshape(-1)
    out, off = [], 0
    for s in shapes:
        n = math.prod(s)
        out.append(flat[off:off + n].reshape(s))
        off += n
    return out


def _pad_cols(a, width):
    return jnp.pad(a, ((0, 0), (0, width - a.shape[1])))


def _pad_lanes(a):
    return _pad_cols(a, LANES)


def _up128(n):
    return -(-n // LANES) * LANES


def kernel(x, c, ctx, c_ctx, w_mod, b_mod, norm1_g, w_in, b_gate, conv_qk, head_norm_g, sgu_ln_g, sgu_ln_b, w_s, b_s, w_branch_mlstm, w_branch_sgu, w_out, norm2_g, w_up, w_ffn_conv, w_down, final_g, loss_target, m_c_ctx, m_w_mod, m_b_mod, m_norm1_g, m_w_in, m_b_gate, m_conv_qk, m_head_norm_g, m_sgu_ln_g, m_sgu_ln_b, m_w_s, m_b_s, m_w_branch_mlstm, m_w_branch_sgu, m_w_out, m_norm2_g, m_w_up, m_w_ffn_conv, m_w_down, m_final_g, v_c_ctx, v_w_mod, v_b_mod, v_norm1_g, v_w_in, v_b_gate, v_conv_qk, v_head_norm_g, v_sgu_ln_g, v_sgu_ln_b, v_w_s, v_b_s, v_w_branch_mlstm, v_w_branch_sgu, v_w_out, v_norm2_g, v_w_up, v_w_ffn_conv, v_w_down, v_final_g):
    t, d = x.shape[1], x.shape[2]
    n_ctx = ctx.shape[1]
    s_rows = t + n_ctx
    nh = b_gate.shape[1] // 4
    md = head_norm_g.shape[1]
    dh = md // nh
    ng, sc = w_s.shape[1], w_s.shape[2]
    dff = w_down.shape[1] * N_DEV
    n_in = w_in.shape[2] * N_DEV
    assert md == d and sgu_ln_g.shape[1] == d and n_ctx == LCH and t % LCH == 0 and t % (8 * GRID_W) == 0
    assert n_in == 8 * d + 4 * nh and 4 * nh <= LANES
    me = 4 * lax.axis_index("x") + 2 * lax.axis_index("y") + lax.axis_index("c")

    n_mod, n_insh, n_upsh = w_mod.shape[2], w_in.shape[2], w_up.shape[2]
    p_mod, p_in, p_up = _up128(n_mod), _up128(n_insh), _up128(n_upsh)
    nq, nf = conv_qk.shape[2], w_ffn_conv.shape[3]
    ffn9 = w_ffn_conv[0].reshape(9, nf)
    colpack = jnp.concatenate([_pad_cols(_bf(w_mod[0]), p_mod), _pad_cols(_bf(w_in[0]), p_in)], axis=1)
    convpack = jnp.concatenate([jnp.pad(conv_qk[0], ((0, 13), (0, 0))), jnp.pad(ffn9, ((0, 7), (0, 0)))], axis=1)
    g_col, g_conv = _allgather([colpack, convpack])
    w_mod_f, w_main, w_gate = _assemble_cols(
        g_col, [(0, n_mod, [(0, 0, N_DEV * n_mod, 0)]),
                (p_mod, n_insh, [(1, 0, 3 * md, 0), (2, 3 * md, 4 * nh, 0), (1, 3 * md + 4 * nh, 5 * d, 3 * md)])],
        [N_MOD * d, 8 * d, LANES], "assemble_weights")
    convw, wconv9 = _assemble_cols(g_conv, [(0, nq, [(0, 0, N_DEV * nq, 0)]), (nq, nf, [(1, 0, N_DEV * nf, 0)])],
                                   [N_DEV * nq, N_DEV * nf], "assemble_conv_weights")
    zero = jnp.minimum(jnp.abs(g_conv[0, 0, 0]), 0.0)
    late_w = [_pad_cols(_bf(w_up[0] + zero), p_up), _bf(w_branch_mlstm[0]), _bf(w_branch_sgu[0]), _bf(w_out[0]), _bf(w_down[0])]
    late_state, late_tok = _exchange_start(late_w, False, "late_weights_start")

    cvec = jnp.concatenate([c, c_ctx[None], jnp.zeros((6, d), F32)], axis=0) + late_tok[0:1, 0:1]
    silu_v, mod = _modulation(cvec, w_mod_f, b_mod)
    mx = [mod[0:1, k * d:(k + 1) * d] for k in range(N_MOD)]
    mc = [mod[1:2, k * d:(k + 1) * d] for k in range(2)]
    x2, ctx2 = x[0], ctx[0]
    in_x = _norm_mod_proj(x2, norm1_g, jnp.concatenate([mx[0], mx[1]], axis=0), w_main, w_gate, s_rows, 0, None, "in_proj")
    hn, z_main, zg = _norm_mod_proj(ctx2, norm1_g, jnp.concatenate([mc[0], mc[1]], axis=0), w_main, w_gate, s_rows, t, in_x,
                                    "in_proj_ctx")
    qscale = dh ** -0.5
    qk = _qk_conv(z_main, convw, t, md, qscale)
    bias = _pad_lanes(b_gate)
    fwd = _mlstm_fwd(qk, z_main, zg, bias, nh)
    hf, hb, states_f, states_b = fwd[0], fwd[1], fwd[2:5], fwd[5:8]
    g_up, g_bm, g_bs, g_out, g_down = _exchange_wait(late_state, fwd[4], "late_weights_wait")
    (w_up_f,) = _assemble_cols(g_up, [(0, n_upsh, [(0, 0, 2 * dff, 0)])], [2 * dff], "assemble_w_up")
    wbm_f, wbs_f, wout_f = (g.reshape(d, d) for g in (g_bm, g_bs, g_out))
    w_down_f = g_down.reshape(dff, d)
    b_st = _pad_lanes(b_s[0].T)
    h1, ym, ys, pm, ps, y, out = _mixer_fwd(hf, hb, z_main, x2, head_norm_g, sgu_ln_g, sgu_ln_b, w_s[0], b_st, wbm_f, wbs_f,
                                            wout_f, mx[2], t, nh)
    hn2, ab = _norm_mod_proj(h1, norm2_g, jnp.concatenate([mx[3], mx[4]], axis=0), w_up_f, None, t, 0, None, "up_proj")
    aconv, f, dh2, dffn, st_tail = _ffn_tail(ab, wconv9, w_down_f, h1, mx[5], final_g[None], loss_target[0], dff)

    db, dac = _ffn_bwd_gate(dffn, w_down_f, aconv, ab, dff)
    da, g_wconv9 = _ffn_conv_bwd(dac, ab, wconv9, dff)
    g_wdown = _wgrad(f, dffn, t, "wgrad_down")
    gwup_slots = _scatter_cols([_wgrad(hn2, da, t, "wgrad_up_a"), _wgrad(hn2, db, t, "wgrad_up_b")],
                               [(0, 0, dff, 0), (1, dff, dff, 0)], n_upsh, "scatter_grad_w_up")
    dh1, st_n2 = _proj_norm_bwd([(da, 0, w_up_f, 0, dff, dff), (db, 0, w_up_f, dff, dff, dff)], h1, 0, norm2_g, mx[4], dh2, t,
                                "up_proj_bwd", (512, 256))
    dz_rest, dhs, g_wout, dpm, dps, st_mix, g_ws, g_bst = _mixer_bwd(dh1, out, hf, hb, z_main, pm, ps, head_norm_g, sgu_ln_g,
                                                                      sgu_ln_b, w_s[0], b_st, wbm_f, wbs_f, wout_f, mx[2], y, t, nh)
    g_wbm = _wgrad(ym, dpm, t, "wgrad_branch_mlstm")
    g_wbs = _wgrad(ys, dps, t, "wgrad_branch_sgu")
    ex_a = [gwup_slots, g_wdown.reshape(N_DEV, dff // N_DEV, d), g_wbm.reshape(N_DEV, d // N_DEV, d),
            g_wbs.reshape(N_DEV, d // N_DEV, d), g_wout.reshape(N_DEV, d // N_DEV, d)]
    ex_a_state, ex_a_tok = _exchange_start(ex_a, True, "grad_exchange_a_start")
    dqkv_f, dg_f, dqkv_b, dg_b = _mlstm_bwd(qk, z_main, zg, bias + ex_a_tok[0:1, :], dhs, hf, hb, states_f, states_b, nh, t)
    dz_qkv, g_convqk = _qkv_conv_bwd(dqkv_f, dqkv_b, z_main, convw, t, md, qscale)
    dz_g, st_gate = _gate_grad_sum(dg_f, dg_b)
    gwin_slots = _scatter_cols(
        [_wgrad(hn, dz_qkv, s_rows, "wgrad_in_qkv"), _wgrad(hn, dz_g, s_rows, "wgrad_in_gate"), _wgrad(hn, dz_rest, t, "wgrad_in_rest")],
        [(0, 0, 3 * md, 0), (1, 3 * md, 4 * nh, 0), (2, 3 * md + 4 * nh, 5 * d, 0)], n_insh, "scatter_grad_w_in")
    gcq_slots = _scatter_cols([g_convqk], [(0, 0, 2 * md, 0)], nq, "scatter_grad_conv_qk")
    gcf_slots = _scatter_cols([g_wconv9], [(0, 0, dff, 0)], nf, "scatter_grad_ffn_conv")
    ex_b_state, ex_b_tok = _exchange_start([gwin_slots, gcq_slots, gcf_slots], True, "grad_exchange_b_start")
    tk = _pick(md, (1024, 512, 256))
    grad_x, st_n1x = _proj_norm_bwd(
        [(dz_qkv, 0, w_main, 0, 3 * md, tk), (dz_rest, 0, w_main, 3 * md, 5 * d, tk), (dz_g, 0, w_gate, 0, LANES, LANES)],
        x2, 0, norm1_g, mx[1] + ex_b_tok[0:1, 0:1], dh1, t, "in_proj_bwd")
    (st_n1c,) = _proj_norm_bwd([(dz_qkv, t, w_main, 0, 3 * md, tk), (dz_g, t, w_gate, 0, LANES, LANES)],
                               ctx2, 0, norm1_g, mc[1] + ex_b_tok[0:1, 0:1], None, n_ctx, "in_proj_bwd_ctx")

    rx_a = _exchange_wait(ex_a_state, st_n1c, "grad_exchange_a_wait")
    rx_b = _exchange_wait(ex_b_state, st_n1c, "grad_exchange_b_wait")
    recv = [rx_b[0], rx_a[0], rx_a[2], rx_a[3], rx_a[4], rx_a[1], rx_b[1], rx_b[2]]
    small_parts = [st_n1x[1], st_n1x[2], st_mix[0], st_n2[1], st_n2[2], st_tail[1],
                   st_n1c[1], st_n1c[2],
                   silu_v[0], st_n1x[0] + st_n1c[0], st_gate[0], st_mix[1], st_mix[2], st_mix[3],
                   g_ws.reshape(-1), g_bst[:, :ng].T.reshape(-1), st_n2[0], st_tail[0],
                   st_tail[2, :LANES]]
    small_state, small_tok = _exchange_start([_pack(small_parts, 8)], False, "small_exchange_start")

    shard_w = (w_in, w_up, w_branch_mlstm, w_branch_sgu, w_out, w_down, conv_qk)
    shard_m = (m_w_in, m_w_up, m_w_branch_mlstm, m_w_branch_sgu, m_w_out, m_w_down, m_conv_qk)
    shard_v = (v_w_in, v_w_up, v_w_branch_mlstm, v_w_branch_sgu, v_w_out, v_w_down, v_conv_qk)
    shard_names = ("w_in", "w_up", "w_branch_mlstm", "w_branch_sgu", "w_out", "w_down", "conv_qk")
    shard_out = [_adamw(wa, recv[k], ma, va, "adamw_" + nm, small_tok)
                 for k, (wa, ma, va, nm) in enumerate(zip(shard_w, shard_m, shard_v, shard_names))]
    shard_out.append([b.reshape(w_ffn_conv.shape) for b in
                      _adamw(ffn9, recv[7], m_w_ffn_conv[0].reshape(9, nf), v_w_ffn_conv[0].reshape(9, nf), "adamw_w_ffn_conv",
                             small_tok)])

    (recv_small,) = _exchange_wait(small_state, shard_out[5][1], "small_exchange_wait")
    small_sum = _slot_sum(recv_small).reshape(-1)
    small_slots = recv_small.reshape(N_DEV, -1)
    o_silu, o_n1 = 8 * d, 9 * d
    ncol = N_MOD * d // N_DEV
    dmc_tot = small_sum[6 * d:8 * d][None]
    dmc_pad = jnp.concatenate([dmc_tot, jnp.zeros((1, 4 * d), F32)], axis=1)
    g_wmod, g_bmod, g_cctx = _mod_grads(
        small_slots[:, o_silu:o_silu + d], lax.dynamic_slice_in_dim(small_slots[:, :6 * d], me * ncol, ncol, axis=1),
        small_slots[:, :6 * d], dmc_tot, lax.dynamic_slice_in_dim(dmc_pad, me * ncol, ncol, axis=1), silu_v[1:2], c_ctx[None],
        w_mod_f[:, :2 * d])
    mod_out = _adamw(w_mod, g_wmod, m_w_mod, v_w_mod, "adamw_w_mod")

    def rep(cc, bm, n1, bg, hg, lg, lb, ws, bs, n2, fg):
        return [cc.reshape(-1), bm.reshape(-1), n1.reshape(-1), _pad_lanes(bg.reshape(1, -1)).reshape(-1), hg.reshape(-1),
                lg.reshape(-1), lb.reshape(-1), ws.reshape(-1), bs.reshape(-1), n2.reshape(-1), fg.reshape(-1)]

    o = o_n1
    g_rep_parts = [g_cctx, g_bmod]
    for n in (d, LANES, d, d, d, ng * sc * sc, ng * sc, d, d):
        g_rep_parts.append(small_sum[o:o + n])
        o += n
    rep_shapes = [(d,), (1, N_MOD * d), (1, d), (1, LANES), (1, d), (1, d), (1, d), (1, ng, sc, sc), (1, ng, sc), (1, d), (d,)]
    rep_out = _adamw(
        _pack(rep(c_ctx, b_mod, norm1_g, b_gate, head_norm_g, sgu_ln_g, sgu_ln_b, w_s, b_s, norm2_g, final_g), LANES),
        _pack(g_rep_parts, LANES)[None],
        _pack(rep(m_c_ctx, m_b_mod, m_norm1_g, m_b_gate, m_head_norm_g, m_sgu_ln_g, m_sgu_ln_b, m_w_s, m_b_s, m_norm2_g, m_final_g), LANES),
        _pack(rep(v_c_ctx, v_b_mod, v_norm1_g, v_b_gate, v_head_norm_g, v_sgu_ln_g, v_sgu_ln_b, v_w_s, v_b_s, v_norm2_g, v_final_g), LANES),
        "adamw_replicated")

    def assemble(k):
        r = _unpack(rep_out[k], rep_shapes)
        s = [o[k] for o in shard_out]
        return [r[0], mod_out[k], r[1], r[2], s[0], r[3][:, :4 * nh], s[6], r[4], r[5], r[6], r[7], r[8], s[2], s[3], s[4], r[9],
                s[1], s[7], s[5], r[10]]

    loss = small_sum[o]
    outs = [loss, grad_x[None]]
    for k in range(4):
        outs += assemble(k)
    return tuple(outs)
```
